```python
import math
import jax, jax.numpy as jnp
from jax import lax
import numpy as np

D_MODEL = 2048
BATCH = 8
SEQ = 2048
DEPTH = 1

N_MEM = 256
HEAD_DIM = 128
FOX_HEADS = 8
FOX_W = FOX_HEADS * HEAD_DIM
LRU_W = D_MODEL - FOX_W
LRU_BLOCKS = 8
LRU_BLOCK = LRU_W // LRU_BLOCKS
LRU_C = 8.0
CONV_W = 4
MIX_W = FOX_W + LRU_W
IN_W = 3 * FOX_W + FOX_HEADS + 2 * LRU_W
XATT_HEADS = 4
XATT_W = XATT_HEADS * HEAD_DIM
FFN_HIDDEN = int(math.ceil((8 * D_MODEL / 3) / 256) * 256)
Q_BLOCK = 128
RMS_EPS = 1e-6

SPLITS = (FOX_W, 2 * FOX_W, 3 * FOX_W, 3 * FOX_W + FOX_HEADS, 3 * FOX_W + FOX_HEADS + LRU_W)

kernel_name = "hymba_fox_rglru_memxattn_block"


def rmsnorm(x, g):
    xf = x.astype(jnp.float32)
    y = xf * lax.rsqrt(jnp.mean(xf * xf, axis=-1, keepdims=True) + RMS_EPS)
    return (y * g.astype(jnp.float32)).astype(x.dtype)


def forgetting_attention(q, k, v, c):
    B, H, S, dh = q.shape
    nb = S // Q_BLOCK
    scale = 1.0 / math.sqrt(dh)
    qb = q.reshape(B, H, nb, Q_BLOCK, dh).transpose(2, 0, 1, 3, 4)
    cb = c.reshape(B, H, nb, Q_BLOCK).transpose(2, 0, 1, 3)
    k_pos = jnp.arange(S)

    def one_block(args):
        q_i, c_i, i = args
        s = jnp.einsum('bhqd,bhkd->bhqk', q_i, k, preferred_element_type=jnp.float32) * scale
        s = s + c_i[..., None] - c[:, :, None, :]
        q_pos = i * Q_BLOCK + jnp.arange(Q_BLOCK)
        causal = k_pos[None, :] <= q_pos[:, None]
        s = jnp.where(causal, s, -jnp.inf)
        p = jax.nn.softmax(s, axis=-1)
        return jnp.einsum('bhqk,bhkd->bhqd', p.astype(v.dtype), v)

    o = lax.map(one_block, (qb, cb, jnp.arange(nb)))
    return o.transpose(1, 2, 0, 3, 4).reshape(B, H, S, dh)


def causal_depthwise_conv(u, w, b):
    S = u.shape[1]
    up = jnp.pad(u, ((0, 0), (CONV_W - 1, 0), (0, 0)))
    return b + sum(w[j] * up[:, j:j + S] for j in range(CONV_W))


def rg_lru(u, w_ra, b_ra, w_ri, b_ri, lam):
    B, S, W = u.shape
    ub = u.reshape(B, S, LRU_BLOCKS, LRU_BLOCK)
    r = jax.nn.sigmoid(jnp.einsum('bsnc,ncd->bsnd', ub, w_ra).reshape(B, S, W) + b_ra)
    i = jax.nn.sigmoid(jnp.einsum('bsnc,ncd->bsnd', ub, w_ri).reshape(B, S, W) + b_ri)
    log_a = -LRU_C * r.astype(jnp.float32) * jax.nn.softplus(-lam.astype(jnp.float32))
    a = jnp.exp(log_a)
    b_in = jnp.sqrt(-jnp.expm1(2.0 * log_a)) * (i * u).astype(jnp.float32)

    def combine(left, right):
        a1, b1 = left
        a2, b2 = right
        return a1 * a2, a2 * b1 + b2

    _, h = lax.associative_scan(combine, (a, b_in), axis=1)
    return h.astype(u.dtype)


def _fwd_setup_inputs(seed: int = 0) -> dict:
    key = jax.random.key(seed)
    ks = jax.random.split(key, 32)
    f32 = jnp.float32

    def nrm(k, shape, scale):
        return jax.random.normal(k, shape, f32) * scale

    def gain(k, shape):
        return 1.0 + 0.02 * jax.random.normal(k, shape, f32)

    L = DEPTH
    a_c = jax.random.uniform(ks[13], (L, LRU_W), f32, 0.9, 0.999)
    s_lam = a_c ** (1.0 / LRU_C)
    lam = jnp.log(s_lam) - jnp.log1p(-s_lam)
    return {
        "x": nrm(ks[0], (BATCH, SEQ, D_MODEL), 1.0),
        "mem": nrm(ks[1], (BATCH, N_MEM, D_MODEL), 1.0),
        "g_mix": gain(ks[2], (L, D_MODEL)),
        "w_in": nrm(ks[3], (L, D_MODEL, IN_W), D_MODEL ** -0.5),
        "b_f": jax.random.uniform(ks[4], (L, FOX_HEADS), f32, 3.0, 5.0),
        "g_q": gain(ks[5], (L, HEAD_DIM)),
        "g_k": gain(ks[6], (L, HEAD_DIM)),
        "conv_w": nrm(ks[7], (L, CONV_W, LRU_W), CONV_W ** -0.5),
        "conv_b": nrm(ks[8], (L, LRU_W), 0.02),
        "w_ra": nrm(ks[9], (L, LRU_BLOCKS, LRU_BLOCK, LRU_BLOCK), LRU_BLOCK ** -0.5),
        "b_ra": nrm(ks[10], (L, LRU_W), 0.02),
        "w_ri": nrm(ks[11], (L, LRU_BLOCKS, LRU_BLOCK, LRU_BLOCK), LRU_BLOCK ** -0.5),
        "b_ri": nrm(ks[12], (L, LRU_W), 0.02),
        "lam": lam,
        "g_fox_out": gain(ks[14], (L, FOX_W)),
        "g_lru_out": gain(ks[15], (L, LRU_W)),
        "w_out": nrm(ks[16], (L, MIX_W, D_MODEL), MIX_W ** -0.5),
        "g_xattn": gain(ks[17], (L, D_MODEL)),
        "g_mem": gain(ks[18], (L, D_MODEL)),
        "w_cq": nrm(ks[19], (L, D_MODEL, XATT_W), D_MODEL ** -0.5),
        "w_ckv": nrm(ks[20], (L, D_MODEL, 2 * XATT_W), D_MODEL ** -0.5),
        "g_cq": gain(ks[21], (L, HEAD_DIM)),
        "g_ck": gain(ks[22], (L, HEAD_DIM)),
        "w_co": nrm(ks[23], (L, XATT_W, D_MODEL), XATT_W ** -0.5),
        "g_ffn": gain(ks[24], (L, D_MODEL)),
        "w_gate_up": nrm(ks[25], (L, D_MODEL, 2 * FFN_HIDDEN), D_MODEL ** -0.5),
        "w_down": nrm(ks[26], (L, FFN_HIDDEN, D_MODEL), FFN_HIDDEN ** -0.5),
    }


def _fwd_reference(x, mem, g_mix, w_in, b_f, g_q, g_k, conv_w, conv_b, w_ra, b_ra, w_ri, b_ri,
              lam, g_fox_out, g_lru_out, w_out, g_xattn, g_mem, w_cq, w_ckv, g_cq, g_ck,
              w_co, g_ffn, w_gate_up, w_down):
    B, S, _ = x.shape
    M = mem.shape[1]
    for l in range(DEPTH):
        h = rmsnorm(x, g_mix[l])
        proj = h @ w_in[l]
        q, k, v, f_logit, u, gate = jnp.split(proj, SPLITS, axis=-1)
        q = rmsnorm(q.reshape(B, S, FOX_HEADS, HEAD_DIM), g_q[l]).transpose(0, 2, 1, 3)
        k = rmsnorm(k.reshape(B, S, FOX_HEADS, HEAD_DIM), g_k[l]).transpose(0, 2, 1, 3)
        v = v.reshape(B, S, FOX_HEADS, HEAD_DIM).transpose(0, 2, 1, 3)
        log_f = jax.nn.log_sigmoid((f_logit + b_f[l]).astype(jnp.float32))
        c = lax.cumsum(log_f, axis=1).transpose(0, 2, 1)
        o_fox = forgetting_attention(q, k, v, c)
        o_fox = o_fox.transpose(0, 2, 1, 3).reshape(B, S, FOX_W)

        u = causal_depthwise_conv(u, conv_w[l], conv_b[l])
        y_lru = rg_lru(u, w_ra[l], b_ra[l], w_ri[l], b_ri[l], lam[l]) * jax.nn.gelu(gate)

        mix = jnp.concatenate([rmsnorm(o_fox, g_fox_out[l]), rmsnorm(y_lru, g_lru_out[l])], axis=-1)
        x = x + mix @ w_out[l]

        hq = rmsnorm(x, g_xattn[l])
        mn = rmsnorm(mem, g_mem[l])
        cq = rmsnorm((hq @ w_cq[l]).reshape(B, S, XATT_HEADS, HEAD_DIM), g_cq[l])
        ck, cv = jnp.split(mn @ w_ckv[l], 2, axis=-1)
        ck = rmsnorm(ck.reshape(B, M, XATT_HEADS, HEAD_DIM), g_ck[l])
        cv = cv.reshape(B, M, XATT_HEADS, HEAD_DIM)
        s = jnp.einsum('bshd,bmhd->bhsm', cq, ck, preferred_element_type=jnp.float32) / math.sqrt(HEAD_DIM)
        p = jax.nn.softmax(s, axis=-1)
        o_x = jnp.einsum('bhsm,bmhd->bshd', p.astype(cv.dtype), cv).reshape(B, S, XATT_W)
        x = x + o_x @ w_co[l]

        hf = rmsnorm(x, g_ffn[l])
        f_gate, f_up = jnp.split(hf @ w_gate_up[l], 2, axis=-1)
        x = x + (jax.nn.silu(f_gate) * f_up) @ w_down[l]
    return x


import jax as _jax
import jax.numpy as _jnp

TWIN_FORMAT = 'train_step'
FWD_PARAMS = ['x', 'mem', 'g_mix', 'w_in', 'b_f', 'g_q', 'g_k', 'conv_w', 'conv_b', 'w_ra', 'b_ra', 'w_ri', 'b_ri', 'lam', 'g_fox_out', 'g_lru_out', 'w_out', 'g_xattn', 'g_mem', 'w_cq', 'w_ckv', 'g_cq', 'g_ck', 'w_co', 'g_ffn', 'w_gate_up', 'w_down']
TWIN_WEIGHTS = ['g_mix', 'w_in', 'b_f', 'g_q', 'g_k', 'conv_w', 'conv_b', 'w_ra', 'b_ra', 'w_ri', 'b_ri', 'lam', 'g_fox_out', 'g_lru_out', 'w_out', 'g_xattn', 'g_mem', 'w_cq', 'w_ckv', 'g_cq', 'g_ck', 'w_co', 'g_ffn', 'w_gate_up', 'w_down']
TWIN_DIFF_INPUT = 'x'
TWIN_INPUTS = ['x', 'mem', 'g_mix', 'w_in', 'b_f', 'g_q', 'g_k', 'conv_w', 'conv_b', 'w_ra', 'b_ra', 'w_ri', 'b_ri', 'lam', 'g_fox_out', 'g_lru_out', 'w_out', 'g_xattn', 'g_mem', 'w_cq', 'w_ckv', 'g_cq', 'g_ck', 'w_co', 'g_ffn', 'w_gate_up', 'w_down', 'loss_target', 'm_g_mix', 'm_w_in', 'm_b_f', 'm_g_q', 'm_g_k', 'm_conv_w', 'm_conv_b', 'm_w_ra', 'm_b_ra', 'm_w_ri', 'm_b_ri', 'm_lam', 'm_g_fox_out', 'm_g_lru_out', 'm_w_out', 'm_g_xattn', 'm_g_mem', 'm_w_cq', 'm_w_ckv', 'm_g_cq', 'm_g_ck', 'm_w_co', 'm_g_ffn', 'm_w_gate_up', 'm_w_down', 'v_g_mix', 'v_w_in', 'v_b_f', 'v_g_q', 'v_g_k', 'v_conv_w', 'v_conv_b', 'v_w_ra', 'v_b_ra', 'v_w_ri', 'v_b_ri', 'v_lam', 'v_g_fox_out', 'v_g_lru_out', 'v_w_out', 'v_g_xattn', 'v_g_mem', 'v_w_cq', 'v_w_ckv', 'v_g_cq', 'v_g_ck', 'v_w_co', 'v_g_ffn', 'v_w_gate_up', 'v_w_down']
TWIN_OUTPUTS = ['loss', 'grad_x', 'grad_g_mix', 'grad_w_in', 'grad_b_f', 'grad_g_q', 'grad_g_k', 'grad_conv_w', 'grad_conv_b', 'grad_w_ra', 'grad_b_ra', 'grad_w_ri', 'grad_b_ri', 'grad_lam', 'grad_g_fox_out', 'grad_g_lru_out', 'grad_w_out', 'grad_g_xattn', 'grad_g_mem', 'grad_w_cq', 'grad_w_ckv', 'grad_g_cq', 'grad_g_ck', 'grad_w_co', 'grad_g_ffn', 'grad_w_gate_up', 'grad_w_down', 'delta_g_mix', 'delta_w_in', 'delta_b_f', 'delta_g_q', 'delta_g_k', 'delta_conv_w', 'delta_conv_b', 'delta_w_ra', 'delta_b_ra', 'delta_w_ri', 'delta_b_ri', 'delta_lam', 'delta_g_fox_out', 'delta_g_lru_out', 'delta_w_out', 'delta_g_xattn', 'delta_g_mem', 'delta_w_cq', 'delta_w_ckv', 'delta_g_cq', 'delta_g_ck', 'delta_w_co', 'delta_g_ffn', 'delta_w_gate_up', 'delta_w_down', 'new_m_g_mix', 'new_m_w_in', 'new_m_b_f', 'new_m_g_q', 'new_m_g_k', 'new_m_conv_w', 'new_m_conv_b', 'new_m_w_ra', 'new_m_b_ra', 'new_m_w_ri', 'new_m_b_ri', 'new_m_lam', 'new_m_g_fox_out', 'new_m_g_lru_out', 'new_m_w_out', 'new_m_g_xattn', 'new_m_g_mem', 'new_m_w_cq', 'new_m_w_ckv', 'new_m_g_cq', 'new_m_g_ck', 'new_m_w_co', 'new_m_g_ffn', 'new_m_w_gate_up', 'new_m_w_down', 'new_v_g_mix', 'new_v_w_in', 'new_v_b_f', 'new_v_g_q', 'new_v_g_k', 'new_v_conv_w', 'new_v_conv_b', 'new_v_w_ra', 'new_v_b_ra', 'new_v_w_ri', 'new_v_b_ri', 'new_v_lam', 'new_v_g_fox_out', 'new_v_g_lru_out', 'new_v_w_out', 'new_v_g_xattn', 'new_v_g_mem', 'new_v_w_cq', 'new_v_w_ckv', 'new_v_g_cq', 'new_v_g_ck', 'new_v_w_co', 'new_v_g_ffn', 'new_v_w_gate_up', 'new_v_w_down']
TWIN_LEAF_KINDS = {'loss': 'loss', 'grad_x': 'grad_x', 'grad_g_mix': 'grad_w', 'grad_w_in': 'grad_w', 'grad_b_f': 'grad_w', 'grad_g_q': 'grad_w', 'grad_g_k': 'grad_w', 'grad_conv_w': 'grad_w', 'grad_conv_b': 'grad_w', 'grad_w_ra': 'grad_w', 'grad_b_ra': 'grad_w', 'grad_w_ri': 'grad_w', 'grad_b_ri': 'grad_w', 'grad_lam': 'grad_w', 'grad_g_fox_out': 'grad_w', 'grad_g_lru_out': 'grad_w', 'grad_w_out': 'grad_w', 'grad_g_xattn': 'grad_w', 'grad_g_mem': 'grad_w', 'grad_w_cq': 'grad_w', 'grad_w_ckv': 'grad_w', 'grad_g_cq': 'grad_w', 'grad_g_ck': 'grad_w', 'grad_w_co': 'grad_w', 'grad_g_ffn': 'grad_w', 'grad_w_gate_up': 'grad_w', 'grad_w_down': 'grad_w', 'delta_g_mix': 'delta_w', 'delta_w_in': 'delta_w', 'delta_b_f': 'delta_w', 'delta_g_q': 'delta_w', 'delta_g_k': 'delta_w', 'delta_conv_w': 'delta_w', 'delta_conv_b': 'delta_w', 'delta_w_ra': 'delta_w', 'delta_b_ra': 'delta_w', 'delta_w_ri': 'delta_w', 'delta_b_ri': 'delta_w', 'delta_lam': 'delta_w', 'delta_g_fox_out': 'delta_w', 'delta_g_lru_out': 'delta_w', 'delta_w_out': 'delta_w', 'delta_g_xattn': 'delta_w', 'delta_g_mem': 'delta_w', 'delta_w_cq': 'delta_w', 'delta_w_ckv': 'delta_w', 'delta_g_cq': 'delta_w', 'delta_g_ck': 'delta_w', 'delta_w_co': 'delta_w', 'delta_g_ffn': 'delta_w', 'delta_w_gate_up': 'delta_w', 'delta_w_down': 'delta_w', 'new_m_g_mix': 'new_m', 'new_m_w_in': 'new_m', 'new_m_b_f': 'new_m', 'new_m_g_q': 'new_m', 'new_m_g_k': 'new_m', 'new_m_conv_w': 'new_m', 'new_m_conv_b': 'new_m', 'new_m_w_ra': 'new_m', 'new_m_b_ra': 'new_m', 'new_m_w_ri': 'new_m', 'new_m_b_ri': 'new_m', 'new_m_lam': 'new_m', 'new_m_g_fox_out': 'new_m', 'new_m_g_lru_out': 'new_m', 'new_m_w_out': 'new_m', 'new_m_g_xattn': 'new_m', 'new_m_g_mem': 'new_m', 'new_m_w_cq': 'new_m', 'new_m_w_ckv': 'new_m', 'new_m_g_cq': 'new_m', 'new_m_g_ck': 'new_m', 'new_m_w_co': 'new_m', 'new_m_g_ffn': 'new_m', 'new_m_w_gate_up': 'new_m', 'new_m_w_down': 'new_m', 'new_v_g_mix': 'new_v', 'new_v_w_in': 'new_v', 'new_v_b_f': 'new_v', 'new_v_g_q': 'new_v', 'new_v_g_k': 'new_v', 'new_v_conv_w': 'new_v', 'new_v_conv_b': 'new_v', 'new_v_w_ra': 'new_v', 'new_v_b_ra': 'new_v', 'new_v_w_ri': 'new_v', 'new_v_b_ri': 'new_v', 'new_v_lam': 'new_v', 'new_v_g_fox_out': 'new_v', 'new_v_g_lru_out': 'new_v', 'new_v_w_out': 'new_v', 'new_v_g_xattn': 'new_v', 'new_v_g_mem': 'new_v', 'new_v_w_cq': 'new_v', 'new_v_w_ckv': 'new_v', 'new_v_g_cq': 'new_v', 'new_v_g_ck': 'new_v', 'new_v_w_co': 'new_v', 'new_v_g_ffn': 'new_v', 'new_v_w_gate_up': 'new_v', 'new_v_w_down': 'new_v'}


def _forward(args):
    return _fwd_reference(*[args[k] for k in FWD_PARAMS])


def _output_shape():
    out = _jax.eval_shape(lambda: _forward(_fwd_setup_inputs(0)))
    return out.shape, out.dtype

N_MICROBATCH = 1
ADAM_LR = 0.001
ADAM_B1 = 0.9
ADAM_B2 = 0.999
ADAM_EPS = 1e-08
ADAM_WD = 0.01
ADAM_STEP = 10
PER_EXAMPLE_BATCH_AXIS = {'x': 0, 'mem': 0, 'loss_target': 0}
SHARED_INPUTS = []
_WEIGHT_DTYPES = {'g_mix': _jnp.float32, 'w_in': _jnp.float32, 'b_f': _jnp.float32, 'g_q': _jnp.float32, 'g_k': _jnp.float32, 'conv_w': _jnp.float32, 'conv_b': _jnp.float32, 'w_ra': _jnp.float32, 'b_ra': _jnp.float32, 'w_ri': _jnp.float32, 'b_ri': _jnp.float32, 'lam': _jnp.float32, 'g_fox_out': _jnp.float32, 'g_lru_out': _jnp.float32, 'w_out': _jnp.float32, 'g_xattn': _jnp.float32, 'g_mem': _jnp.float32, 'w_cq': _jnp.float32, 'w_ckv': _jnp.float32, 'g_cq': _jnp.float32, 'g_ck': _jnp.float32, 'w_co': _jnp.float32, 'g_ffn': _jnp.float32, 'w_gate_up': _jnp.float32, 'w_down': _jnp.float32}
MOMENT_SCALE = {'g_mix': 2.733760e-01, 'w_in': 1.631972e-01, 'b_f': 1.494960e+00, 'g_q': 3.775431e-01, 'g_k': 3.803392e-01, 'conv_w': 3.861906e-01, 'conv_b': 6.601162e+00, 'w_ra': 1.565011e-01, 'b_ra': 1.096262e-01, 'w_ri': 2.935087e-01, 'b_ri': 1.777942e-01, 'lam': 1.745134e-01, 'g_fox_out': 8.462323e+00, 'g_lru_out': 1.233065e+01, 'w_out': 3.291713e-01, 'g_xattn': 1.924906e-02, 'g_mem': 1.024869e-01, 'w_cq': 3.786335e-02, 'w_ckv': 8.538319e-02, 'g_cq': 1.181904e+00, 'g_ck': 1.182923e+00, 'w_co': 4.920185e-02, 'g_ffn': 6.204979e+00, 'w_gate_up': 7.774325e-02, 'w_down': 9.450099e-02}


def _to_microbatches(a, axis):
    t = _jnp.moveaxis(a, axis, 0)
    t = t.reshape((N_MICROBATCH, t.shape[0] // N_MICROBATCH) + t.shape[1:])
    return _jnp.moveaxis(t, 1, axis + 1)


def setup_inputs(seed: int = 0) -> dict:
    inp = _fwd_setup_inputs(seed)
    key = _jax.random.fold_in(_jax.random.key(seed), 7919)
    shape, _ = _output_shape()
    out = dict(inp)
    out["loss_target"] = _jax.random.normal(_jax.random.fold_in(key, 0), shape, _jnp.float32)
    for i, name in enumerate(TWIN_WEIGHTS):
        w = inp[name].astype(_jnp.float32)
        if MOMENT_SCALE is None:
            s = _jnp.sqrt(_jnp.mean(_jnp.square(w)) + 1e-30)
        else:
            s = MOMENT_SCALE[name]
        km, kv = _jax.random.split(_jax.random.fold_in(key, i + 1))
        out[name] = w
        out["m_" + name] = s * _jax.random.normal(km, w.shape, _jnp.float32)
        out["v_" + name] = (s * s) * _jax.random.uniform(kv, w.shape, _jnp.float32, 0.5, 1.5)
    if N_MICROBATCH > 1:
        for name, axis in PER_EXAMPLE_BATCH_AXIS.items():
            out[name] = _to_microbatches(out[name], axis)
    return {'x': out['x'], 'mem': out['mem'], 'g_mix': out['g_mix'], 'w_in': out['w_in'], 'b_f': out['b_f'], 'g_q': out['g_q'], 'g_k': out['g_k'], 'conv_w': out['conv_w'], 'conv_b': out['conv_b'], 'w_ra': out['w_ra'], 'b_ra': out['b_ra'], 'w_ri': out['w_ri'], 'b_ri': out['b_ri'], 'lam': out['lam'], 'g_fox_out': out['g_fox_out'], 'g_lru_out': out['g_lru_out'], 'w_out': out['w_out'], 'g_xattn': out['g_xattn'], 'g_mem': out['g_mem'], 'w_cq': out['w_cq'], 'w_ckv': out['w_ckv'], 'g_cq': out['g_cq'], 'g_ck': out['g_ck'], 'w_co': out['w_co'], 'g_ffn': out['g_ffn'], 'w_gate_up': out['w_gate_up'], 'w_down': out['w_down'], 'loss_target': out['loss_target'], 'm_g_mix': out['m_g_mix'], 'm_w_in': out['m_w_in'], 'm_b_f': out['m_b_f'], 'm_g_q': out['m_g_q'], 'm_g_k': out['m_g_k'], 'm_conv_w': out['m_conv_w'], 'm_conv_b': out['m_conv_b'], 'm_w_ra': out['m_w_ra'], 'm_b_ra': out['m_b_ra'], 'm_w_ri': out['m_w_ri'], 'm_b_ri': out['m_b_ri'], 'm_lam': out['m_lam'], 'm_g_fox_out': out['m_g_fox_out'], 'm_g_lru_out': out['m_g_lru_out'], 'm_w_out': out['m_w_out'], 'm_g_xattn': out['m_g_xattn'], 'm_g_mem': out['m_g_mem'], 'm_w_cq': out['m_w_cq'], 'm_w_ckv': out['m_w_ckv'], 'm_g_cq': out['m_g_cq'], 'm_g_ck': out['m_g_ck'], 'm_w_co': out['m_w_co'], 'm_g_ffn': out['m_g_ffn'], 'm_w_gate_up': out['m_w_gate_up'], 'm_w_down': out['m_w_down'], 'v_g_mix': out['v_g_mix'], 'v_w_in': out['v_w_in'], 'v_b_f': out['v_b_f'], 'v_g_q': out['v_g_q'], 'v_g_k': out['v_g_k'], 'v_conv_w': out['v_conv_w'], 'v_conv_b': out['v_conv_b'], 'v_w_ra': out['v_w_ra'], 'v_b_ra': out['v_b_ra'], 'v_w_ri': out['v_w_ri'], 'v_b_ri': out['v_b_ri'], 'v_lam': out['v_lam'], 'v_g_fox_out': out['v_g_fox_out'], 'v_g_lru_out': out['v_g_lru_out'], 'v_w_out': out['v_w_out'], 'v_g_xattn': out['v_g_xattn'], 'v_g_mem': out['v_g_mem'], 'v_w_cq': out['v_w_cq'], 'v_w_ckv': out['v_w_ckv'], 'v_g_cq': out['v_g_cq'], 'v_g_ck': out['v_g_ck'], 'v_w_co': out['v_w_co'], 'v_g_ffn': out['v_g_ffn'], 'v_w_gate_up': out['v_w_gate_up'], 'v_w_down': out['v_w_down']}


def _loss(weights, diff, rest, loss_target):
    with _jax.named_scope("forward"):
        args = {**rest, TWIN_DIFF_INPUT: diff, **{k: w.astype(_WEIGHT_DTYPES[k]) for k, w in weights.items()}}
        y = _forward(args)
    with _jax.named_scope("loss_head"):
        err = _jnp.square(y.astype(_jnp.float32) - loss_target)
        return 0.5 * _jnp.sum(_jnp.mean(err, axis=-1)) if err.ndim else 0.5 * err


def _adamw(w, g, m, v):
    m = ADAM_B1 * m + (1.0 - ADAM_B1) * g
    v = ADAM_B2 * v + (1.0 - ADAM_B2) * _jnp.square(g)
    m_hat = m / (1.0 - ADAM_B1 ** ADAM_STEP)
    v_hat = v / (1.0 - ADAM_B2 ** ADAM_STEP)
    delta = -ADAM_LR * (m_hat / (_jnp.sqrt(v_hat) + ADAM_EPS) + ADAM_WD * w)
    return delta, m, v


def reference(x, mem, g_mix, w_in, b_f, g_q, g_k, conv_w, conv_b, w_ra, b_ra, w_ri, b_ri, lam, g_fox_out, g_lru_out, w_out, g_xattn, g_mem, w_cq, w_ckv, g_cq, g_ck, w_co, g_ffn, w_gate_up, w_down, loss_target, m_g_mix, m_w_in, m_b_f, m_g_q, m_g_k, m_conv_w, m_conv_b, m_w_ra, m_b_ra, m_w_ri, m_b_ri, m_lam, m_g_fox_out, m_g_lru_out, m_w_out, m_g_xattn, m_g_mem, m_w_cq, m_w_ckv, m_g_cq, m_g_ck, m_w_co, m_g_ffn, m_w_gate_up, m_w_down, v_g_mix, v_w_in, v_b_f, v_g_q, v_g_k, v_conv_w, v_conv_b, v_w_ra, v_b_ra, v_w_ri, v_b_ri, v_lam, v_g_fox_out, v_g_lru_out, v_w_out, v_g_xattn, v_g_mem, v_w_cq, v_w_ckv, v_g_cq, v_g_ck, v_w_co, v_g_ffn, v_w_gate_up, v_w_down):
    given = dict(x=x, mem=mem, g_mix=g_mix, w_in=w_in, b_f=b_f, g_q=g_q, g_k=g_k, conv_w=conv_w, conv_b=conv_b, w_ra=w_ra, b_ra=b_ra, w_ri=w_ri, b_ri=b_ri, lam=lam, g_fox_out=g_fox_out, g_lru_out=g_lru_out, w_out=w_out, g_xattn=g_xattn, g_mem=g_mem, w_cq=w_cq, w_ckv=w_ckv, g_cq=g_cq, g_ck=g_ck, w_co=w_co, g_ffn=g_ffn, w_gate_up=w_gate_up, w_down=w_down, loss_target=loss_target, m_g_mix=m_g_mix, m_w_in=m_w_in, m_b_f=m_b_f, m_g_q=m_g_q, m_g_k=m_g_k, m_conv_w=m_conv_w, m_conv_b=m_conv_b, m_w_ra=m_w_ra, m_b_ra=m_b_ra, m_w_ri=m_w_ri, m_b_ri=m_b_ri, m_lam=m_lam, m_g_fox_out=m_g_fox_out, m_g_lru_out=m_g_lru_out, m_w_out=m_w_out, m_g_xattn=m_g_xattn, m_g_mem=m_g_mem, m_w_cq=m_w_cq, m_w_ckv=m_w_ckv, m_g_cq=m_g_cq, m_g_ck=m_g_ck, m_w_co=m_w_co, m_g_ffn=m_g_ffn, m_w_gate_up=m_w_gate_up, m_w_down=m_w_down, v_g_mix=v_g_mix, v_w_in=v_w_in, v_b_f=v_b_f, v_g_q=v_g_q, v_g_k=v_g_k, v_conv_w=v_conv_w, v_conv_b=v_conv_b, v_w_ra=v_w_ra, v_b_ra=v_b_ra, v_w_ri=v_w_ri, v_b_ri=v_b_ri, v_lam=v_lam, v_g_fox_out=v_g_fox_out, v_g_lru_out=v_g_lru_out, v_w_out=v_w_out, v_g_xattn=v_g_xattn, v_g_mem=v_g_mem, v_w_cq=v_w_cq, v_w_ckv=v_w_ckv, v_g_cq=v_g_cq, v_g_ck=v_g_ck, v_w_co=v_w_co, v_g_ffn=v_g_ffn, v_w_gate_up=v_w_gate_up, v_w_down=v_w_down)
    weights = {n: given[n] for n in TWIN_WEIGHTS}
    shared = {n: given[n] for n in SHARED_INPUTS}
    per_example = {n: given[n] for n in ['x', 'mem']}
    grad_fn = _jax.value_and_grad(_loss, argnums=(0, 1))

    def one_microbatch(ex, loss_target):
        ex = dict(ex)
        diff = ex.pop(TWIN_DIFF_INPUT)
        return grad_fn(weights, diff, {**shared, **ex}, loss_target)

    if N_MICROBATCH == 1:
        loss, (grad_w, grad_x) = one_microbatch(per_example, given["loss_target"])
    else:
        def body(carry, xs):
            loss_sum, grad_sum = carry
            l_k, (gw_k, gx_k) = one_microbatch(xs[0], xs[1])
            with _jax.named_scope("update"):
                return (loss_sum + l_k, _jax.tree.map(_jnp.add, grad_sum, gw_k)), gx_k

        init = (_jnp.zeros((), _jnp.float32), _jax.tree.map(_jnp.zeros_like, weights))
        (loss, grad_w), grad_x = _jax.lax.scan(body, init, (per_example, given["loss_target"]))
    with _jax.named_scope("update"):
        delta_w, new_m, new_v = {}, {}, {}
        for n in TWIN_WEIGHTS:
            delta_w[n], new_m[n], new_v[n] = _adamw(weights[n], grad_w[n], given["m_" + n], given["v_" + n])
    return (loss, grad_x, *[grad_w[n] for n in TWIN_WEIGHTS], *[delta_w[n] for n in TWIN_WEIGHTS],
            *[new_m[n] for n in TWIN_WEIGHTS], *[new_v[n] for n in TWIN_WEIGHTS])
```

```python
import functools
import math

import jax
import jax.numpy as jnp
from jax import lax
from jax.experimental import pallas as pl
from jax.experimental.pallas import tpu as pltpu

F32 = jnp.float32
BF16 = jnp.bfloat16
MESH = pl.DeviceIdType.MESH
AXES = ("x", "y", "c")
N_DEV = 8
N_CHIP = 4

HEAD = 128
FOX_H = 8
XATT_H = 4
LRU_NB = 8
CONV_K = 4
LRU_C = 8.0
RMS_EPS = 1e-6
ATT_T = 256
ROW_T = 256
NEG = -1e30
V7X_VMEM_LIMIT = 48 * 1024 * 1024

ADAM_LR = 0.001
ADAM_B1 = 0.9
ADAM_B2 = 0.999
ADAM_EPS = 1e-08
ADAM_WD = 0.01
ADAM_STEP = 10

NT = (((1,), (1,)), ((), ()))
TN = (((0,), (0,)), ((), ()))
NN = (((1,), (0,)), ((), ()))


def _params(*sem):
    return pltpu.CompilerParams(dimension_semantics=sem or None, vmem_limit_bytes=V7X_VMEM_LIMIT)


def _dot(a, b, dn=NN):
    return lax.dot_general(a, b, dn, preferred_element_type=F32)


def _rms(x, g):
    r = lax.rsqrt(jnp.mean(x * x, axis=-1, keepdims=True) + RMS_EPS)
    return x * r * g


def _rms_grad(x, g, dy):
    r = lax.rsqrt(jnp.mean(x * x, axis=-1, keepdims=True) + RMS_EPS)
    xh = x * r
    dxh = dy * g
    dx = r * (dxh - xh * jnp.mean(dxh * xh, axis=-1, keepdims=True))
    return dx, jnp.sum(dy * xh, axis=0, keepdims=True)


def _gelu(x):
    k = math.sqrt(2.0 / math.pi)
    return 0.5 * x * (1.0 + jnp.tanh(k * (x + 0.044715 * (x * x * x))))


def _gelu_grad(x):
    k = math.sqrt(2.0 / math.pi)
    t = jnp.tanh(k * (x + 0.044715 * (x * x * x)))
    return 0.5 * (1.0 + t) + 0.5 * x * (1.0 - t * t) * k * (1.0 + 3.0 * 0.044715 * x * x)


def _one_minus_exp(z):
    series = -(z + 0.5 * z * z + (1.0 / 6.0) * z * z * z)
    return jnp.where(z > -1e-3, series, 1.0 - jnp.exp(z))


def _row_tile(rows, cap=ROW_T, mult=16):
    t = min(rows, cap)
    while rows % t or (t % mult and t != rows):
        t -= 1
    return t


def _mat_dims(z):
    return (z.shape[-2], z.shape[-1] * (z.shape[0] if z.ndim == 3 else 1))


def _mat_spec(arr, rblk, cblk, rsel, csel):
    if arr.ndim == 2:
        return pl.BlockSpec((rblk, cblk), lambda i, j, k: ((i, j, k)[rsel], (i, j, k)[csel]))
    nw = arr.shape[2]
    assert nw % cblk == 0, (arr.shape, cblk)
    per = nw // cblk
    return pl.BlockSpec((None, rblk, cblk),
                        lambda i, j, k: ((i, j, k)[csel] // per, (i, j, k)[rsel], (i, j, k)[csel] % per))


def _mm(a, b, mode, out_dtype, name, add=None, out_blocks=None, tm=512, tn=512, tk=512):
    ar, ac = _mat_dims(a)
    br, bc = _mat_dims(b)
    if mode == "nn":
        M, K, N = ar, ac, bc
        assert br == K
    elif mode == "nt":
        M, K, N = ar, ac, br
        assert bc == K
    else:
        M, K, N = ac, ar, bc
        assert br == K
    tm, tn, tk = min(tm, M), min(tn, N), min(tk, K)
    assert M % tm == 0 and N % tn == 0 and K % tk == 0, (name, M, N, K, tm, tn, tk)
    nk = K // tk
    if mode == "nn":
        specs = [_mat_spec(a, tm, tk, 0, 2), _mat_spec(b, tk, tn, 2, 1)]
        dn = NN
    elif mode == "nt":
        specs = [_mat_spec(a, tm, tk, 0, 2), _mat_spec(b, tn, tk, 1, 2)]
        dn = NT
    else:
        specs = [_mat_spec(a, tk, tm, 2, 0), _mat_spec(b, tk, tn, 2, 1)]
        dn = TN
    args = [a, b]
    if add is not None:
        specs.append(pl.BlockSpec((tm, tn), lambda i, j, k: (i, j)))
        args.append(add)
    if out_blocks is None:
        out_shape = jax.ShapeDtypeStruct((M, N), out_dtype)
        out_spec = pl.BlockSpec((tm, tn), lambda i, j, k: (i, j))
    else:
        nb = out_blocks
        nw = N // nb
        assert nw % tn == 0
        per = nw // tn
        out_shape = jax.ShapeDtypeStruct((nb, M, nw), out_dtype)
        out_spec = pl.BlockSpec((None, tm, tn), lambda i, j, k: (j // per, i, j % per))

    def body(*refs):
        a_ref, b_ref = refs[0], refs[1]
        add_ref = refs[2] if add is not None else None
        o_ref, acc_ref = refs[-2], refs[-1]
        k = pl.program_id(2)

        @pl.when(k == 0)
        def _():
            acc_ref[...] = jnp.zeros_like(acc_ref)

        acc_ref[...] += _dot(a_ref[...], b_ref[...], dn)

        @pl.when(k == nk - 1)
        def _():
            r = acc_ref[...]
            if add_ref is not None:
                r = r + add_ref[...]
            o_ref[...] = r.astype(out_dtype)

    return pl.pallas_call(
        body, name=name, grid=(M // tm, N // tn, nk), in_specs=specs, out_specs=out_spec, out_shape=out_shape,
        scratch_shapes=[pltpu.VMEM((tm, tn), F32)],
        compiler_params=_params("parallel", "parallel", "arbitrary"),
    )(*args)


def _rms_fwd(xs, gs, name):
    n = len(xs)
    rows = xs[0].shape[0]
    widths = [x.shape[1] for x in xs]
    tr = _row_tile(rows)

    def body(*refs):
        o_ref = refs[2 * n]
        off = 0
        for t in range(n):
            o_ref[:, off:off + widths[t]] = _rms(refs[t][...], refs[n + t][...]).astype(BF16)
            off += widths[t]

    return pl.pallas_call(
        body, name=name, grid=(rows // tr,),
        in_specs=[pl.BlockSpec((tr, w), lambda i: (i, 0)) for w in widths]
        + [pl.BlockSpec((1, w), lambda i: (0, 0)) for w in widths],
        out_specs=pl.BlockSpec((tr, sum(widths)), lambda i: (i, 0)),
        out_shape=jax.ShapeDtypeStruct((rows, sum(widths)), BF16),
        compiler_params=_params("parallel"),
    )(*xs, *gs)


def _rms_bwd(x, g, dy, name, dy_col=0, res=None, want_dx=True):
    rows, w = x.shape
    tr = _row_tile(rows)
    has_res = res is not None

    def body(*refs):
        x_ref, g_ref, dy_ref = refs[:3]
        res_ref = refs[3] if has_res else None
        outs = refs[3 + has_res:]
        dg_ref = outs[-1]
        dx, dg = _rms_grad(x_ref[...], g_ref[...], dy_ref[...])

        @pl.when(pl.program_id(0) == 0)
        def _():
            dg_ref[...] = jnp.zeros_like(dg_ref)

        dg_ref[...] += dg
        if want_dx:
            if has_res:
                dx = dx + res_ref[...]
            outs[0][...] = dx
            outs[1][...] = dx.astype(BF16)

    row_spec = pl.BlockSpec((tr, w), lambda i: (i, 0))
    in_specs = [row_spec, pl.BlockSpec((1, w), lambda i: (0, 0)), pl.BlockSpec((tr, w), lambda i: (i, dy_col))]
    args = [x, g, dy]
    if has_res:
        in_specs.append(row_spec)
        args.append(res)
    out_specs, out_shape = [], []
    if want_dx:
        out_specs += [row_spec, row_spec]
        out_shape += [jax.ShapeDtypeStruct((rows, w), F32), jax.ShapeDtypeStruct((rows, w), BF16)]
    out_specs.append(pl.BlockSpec((1, w), lambda i: (0, 0)))
    out_shape.append(jax.ShapeDtypeStruct((1, w), F32))
    return pl.pallas_call(
        body, name=name, grid=(rows // tr,), in_specs=in_specs, out_specs=out_specs, out_shape=out_shape,
        compiler_params=_params("arbitrary"),
    )(*args)


def _loss_head(y, target, name):
    rows, w = y.shape
    tr = _row_tile(rows)

    def body(y_ref, t_ref, dy_ref, dyb_ref, acc_ref):
        e = y_ref[...] - t_ref[...]

        @pl.when(pl.program_id(0) == 0)
        def _():
            acc_ref[...] = jnp.zeros_like(acc_ref)

        acc_ref[...] += jnp.sum(e * e)
        dy = e * (1.0 / w)
        dy_ref[...] = dy
        dyb_ref[...] = dy.astype(BF16)

    row_spec = pl.BlockSpec((tr, w), lambda i: (i, 0))
    return pl.pallas_call(
        body, name=name, grid=(rows // tr,), in_specs=[row_spec, row_spec],
        out_specs=[row_spec, row_spec, pl.BlockSpec((8, 128), lambda i: (0, 0))],
        out_shape=[jax.ShapeDtypeStruct((rows, w), F32), jax.ShapeDtypeStruct((rows, w), BF16),
                   jax.ShapeDtypeStruct((8, 128), F32)],
        compiler_params=_params("arbitrary"),
    )(y, target)


def _qk_fwd(proj, g_q, g_k, name):
    rows = proj.shape[0]
    w = FOX_H * HEAD
    tr = _row_tile(rows)

    def body(q_ref, k_ref, v_ref, gq_ref, gk_ref, qn_ref, kn_ref, vb_ref):
        for h in range(FOX_H):
            sl = slice(h * HEAD, (h + 1) * HEAD)
            qn_ref[:, sl] = _rms(q_ref[:, sl], gq_ref[...]).astype(BF16)
            kn_ref[:, sl] = _rms(k_ref[:, sl], gk_ref[...]).astype(BF16)
        vb_ref[...] = v_ref[...].astype(BF16)

    gain = pl.BlockSpec((1, HEAD), lambda i: (0, 0))
    out = pl.BlockSpec((tr, w), lambda i: (i, 0))
    return pl.pallas_call(
        body, name=name, grid=(rows // tr,),
        in_specs=[pl.BlockSpec((tr, w), lambda i: (i, 0)), pl.BlockSpec((tr, w), lambda i: (i, 1)),
                  pl.BlockSpec((tr, w), lambda i: (i, 2)), gain, gain],
        out_specs=[out, out, out], out_shape=[jax.ShapeDtypeStruct((rows, w), BF16)] * 3,
        compiler_params=_params("parallel"),
    )(proj, proj, proj, g_q, g_k)


def _qk_bwd(proj, g_q, g_k, dqn, dkn, name):
    rows = proj.shape[0]
    w = FOX_H * HEAD
    tr = _row_tile(rows)

    def body(q_ref, k_ref, gq_ref, gk_ref, dqn_ref, dkn_ref, dq_ref, dk_ref, dgq_ref, dgk_ref):
        @pl.when(pl.program_id(0) == 0)
        def _():
            dgq_ref[...] = jnp.zeros_like(dgq_ref)
            dgk_ref[...] = jnp.zeros_like(dgk_ref)

        for h in range(FOX_H):
            sl = slice(h * HEAD, (h + 1) * HEAD)
            dq, dgq = _rms_grad(q_ref[:, sl], gq_ref[...], dqn_ref[:, sl])
            dk, dgk = _rms_grad(k_ref[:, sl], gk_ref[...], dkn_ref[:, sl])
            dq_ref[:, sl] = dq.astype(BF16)
            dk_ref[:, sl] = dk.astype(BF16)
            dgq_ref[...] += dgq
            dgk_ref[...] += dgk

    gain = pl.BlockSpec((1, HEAD), lambda i: (0, 0))
    row = pl.BlockSpec((tr, w), lambda i: (i, 0))
    return pl.pallas_call(
        body, name=name, grid=(rows // tr,),
        in_specs=[row, pl.BlockSpec((tr, w), lambda i: (i, 1)), gain, gain, row, row],
        out_specs=[row, row, gain, gain],
        out_shape=[jax.ShapeDtypeStruct((rows, w), BF16)] * 2 + [jax.ShapeDtypeStruct((1, HEAD), F32)] * 2,
        compiler_params=_params("arbitrary"),
    )(proj, proj, g_q, g_k, dqn, dkn)


def _fgate_fwd(f_logit, b_pad, name):
    S = f_logit.shape[0]
    T = ATT_T
    nb = S // T

    def body(f_ref, b_ref, col_ref, row_ref, c_scr):
        tri = (lax.broadcasted_iota(jnp.int32, (T, T), 0) >= lax.broadcasted_iota(jnp.int32, (T, T), 1)).astype(F32)
        carry = jnp.zeros((1, HEAD), F32)
        for blk in range(nb):
            z = f_ref[blk * T:(blk + 1) * T, :] + b_ref[...]
            lf = jnp.minimum(z, 0.0) - jnp.log1p(jnp.exp(-jnp.abs(z)))
            cb = jnp.dot(tri, lf, precision=lax.Precision.HIGHEST, preferred_element_type=F32) + carry
            c_scr[blk * T:(blk + 1) * T, :] = cb
            carry = cb[T - 1:T, :]
        c = c_scr[...]
        lane = lax.broadcasted_iota(jnp.int32, c.shape, 1)
        ct = c.T
        for h in range(FOX_H):
            col_ref[h] = jnp.sum(jnp.where(lane == h, c, 0.0), axis=1, keepdims=True)
            for jb in range(nb):
                row_ref[h, jb] = ct[h:h + 1, jb * T:(jb + 1) * T]

    return pl.pallas_call(
        body, name=name,
        out_shape=[jax.ShapeDtypeStruct((FOX_H, S, 1), F32), jax.ShapeDtypeStruct((FOX_H, nb, 1, T), F32)],
        scratch_shapes=[pltpu.VMEM((S, HEAD), F32)], compiler_params=_params(),
    )(f_logit, b_pad)


def _fgate_bwd(f_logit, b_pad, dc_col, dc_row, name):
    S = f_logit.shape[0]
    T = ATT_T
    nb = S // T

    def body(f_ref, b_ref, dcol_ref, drow_ref, df_ref, db_ref, dc_scr, rt_scr):
        lane = lax.broadcasted_iota(jnp.int32, (S, HEAD), 1)
        sub = lax.broadcasted_iota(jnp.int32, (HEAD, T), 0)
        dc = jnp.zeros((S, HEAD), F32)
        for h in range(FOX_H):
            dc = jnp.where(lane == h, dcol_ref[h], dc)
        for jb in range(nb):
            rt = jnp.zeros((HEAD, T), F32)
            for h in range(FOX_H):
                rt = jnp.where(sub == h, drow_ref[h, jb], rt)
            rt_scr[jb * T:(jb + 1) * T, :] = rt.T
        dc_scr[...] = dc - rt_scr[...]
        tri = (lax.broadcasted_iota(jnp.int32, (T, T), 0) <= lax.broadcasted_iota(jnp.int32, (T, T), 1)).astype(F32)
        carry = jnp.zeros((1, HEAD), F32)
        db = jnp.zeros((1, HEAD), F32)
        for blk in reversed(range(nb)):
            rows = slice(blk * T, (blk + 1) * T)
            dlf = jnp.dot(tri, dc_scr[rows, :], precision=lax.Precision.HIGHEST, preferred_element_type=F32) + carry
            carry = dlf[0:1, :]
            z = f_ref[rows, :] + b_ref[...]
            df = dlf * jax.nn.sigmoid(-z)
            df_ref[rows, :] = df.astype(BF16)
            db = db + jnp.sum(df, axis=0, keepdims=True)
        db_ref[...] = db

    return pl.pallas_call(
        body, name=name,
        out_shape=[jax.ShapeDtypeStruct((S, HEAD), BF16), jax.ShapeDtypeStruct((1, HEAD), F32)],
        scratch_shapes=[pltpu.VMEM((S, HEAD), F32), pltpu.VMEM((S, HEAD), F32)], compiler_params=_params(),
    )(f_logit, b_pad, dc_col, dc_row)


def _fox_fwd(qn, kn, vb, c_col, c_row, name):
    S = qn.shape[0]
    T = ATT_T
    nb = S // T
    scale = 1.0 / math.sqrt(HEAD)

    def body(q_ref, k_ref, v_ref, cc_ref, cr_ref, o_ref, lse_ref):
        i = pl.program_id(1)
        q = q_ref[...]
        cc = cc_ref[...]
        rows = i * T + lax.broadcasted_iota(jnp.int32, (T, T), 0)
        lanes = lax.broadcasted_iota(jnp.int32, (T, T), 1)

        def step(j, carry):
            m, l, acc = carry
            sl = pl.ds(pl.multiple_of(j * T, T), T)
            s = _dot(q, k_ref[sl, :], NT) * scale + cc - cr_ref[j]
            s = jnp.where(rows >= j * T + lanes, s, NEG)
            m2 = jnp.maximum(m, jnp.max(s, axis=1, keepdims=True))
            p = jnp.exp(s - m2)
            al = jnp.exp(m - m2)
            return m2, al * l + jnp.sum(p, axis=1, keepdims=True), al * acc + _dot(p.astype(BF16), v_ref[sl, :])

        init = (jnp.full((T, 1), NEG, F32), jnp.zeros((T, 1), F32), jnp.zeros((T, HEAD), F32))
        m, l, acc = lax.fori_loop(0, i + 1, step, init)
        o_ref[...] = acc / l
        lse_ref[...] = m + jnp.log(l)

    head = pl.BlockSpec((S, HEAD), lambda h, i: (0, h))
    return pl.pallas_call(
        body, name=name, grid=(FOX_H, nb),
        in_specs=[pl.BlockSpec((T, HEAD), lambda h, i: (i, h)), head, head,
                  pl.BlockSpec((None, T, 1), lambda h, i: (h, i, 0)),
                  pl.BlockSpec((None, nb, 1, T), lambda h, i: (h, 0, 0, 0))],
        out_specs=[pl.BlockSpec((T, HEAD), lambda h, i: (i, h)), pl.BlockSpec((None, T, 1), lambda h, i: (h, i, 0))],
        out_shape=[jax.ShapeDtypeStruct((S, FOX_H * HEAD), F32), jax.ShapeDtypeStruct((FOX_H, S, 1), F32)],
        compiler_params=_params("parallel", "parallel"),
    )(qn, kn, vb, c_col, c_row)


def _fox_bwd(qn, kn, vb, c_col, c_row, lse, o, do, name):
    S = qn.shape[0]
    T = ATT_T
    nb = S // T
    scale = 1.0 / math.sqrt(HEAD)

    def body(q_ref, k_ref, v_ref, cc_ref, cr_ref, lse_ref, o_ref, do_ref, dq_ref, dk_ref, dv_ref, dcc_ref, dcr_ref):
        j = pl.program_id(1)

        @pl.when(j == 0)
        def _():
            dq_ref[...] = jnp.zeros_like(dq_ref)
            dcc_ref[...] = jnp.zeros_like(dcc_ref)

        k = k_ref[...]
        v = v_ref[...]
        cr = cr_ref[...]
        cols = j * T + lax.broadcasted_iota(jnp.int32, (T, T), 1)
        subl = lax.broadcasted_iota(jnp.int32, (T, T), 0)

        def step(i, carry):
            dk, dv, dcr = carry
            sl = pl.ds(pl.multiple_of(i * T, T), T)
            q = q_ref[sl, :]
            d_o = do_ref[sl, :]
            s = _dot(q, k, NT) * scale + cc_ref[sl, :] - cr
            s = jnp.where(i * T + subl >= cols, s, NEG)
            p = jnp.exp(s - lse_ref[sl, :])
            dob = d_o.astype(BF16)
            dp = _dot(dob, v, NT)
            delta = jnp.sum(d_o * o_ref[sl, :], axis=1, keepdims=True)
            ds = p * (dp - delta)
            dsb = ds.astype(BF16)
            dq_ref[sl, :] += _dot(dsb, k) * scale
            dcc_ref[sl, :] += jnp.sum(ds, axis=1, keepdims=True)
            return (dk + _dot(dsb, q, TN), dv + _dot(p.astype(BF16), dob, TN), dcr + jnp.sum(ds, axis=0, keepdims=True))

        init = (jnp.zeros((T, HEAD), F32), jnp.zeros((T, HEAD), F32), jnp.zeros((1, T), F32))
        dk, dv, dcr = lax.fori_loop(j, nb, step, init)
        dk_ref[...] = dk * scale
        dv_ref[...] = dv.astype(BF16)
        dcr_ref[...] = dcr

    head = pl.BlockSpec((S, HEAD), lambda h, j: (0, h))
    tile = pl.BlockSpec((T, HEAD), lambda h, j: (j, h))
    col = pl.BlockSpec((None, S, 1), lambda h, j: (h, 0, 0))
    row = pl.BlockSpec((None, None, 1, T), lambda h, j: (h, j, 0, 0))
    w = FOX_H * HEAD
    return pl.pallas_call(
        body, name=name, grid=(FOX_H, nb),
        in_specs=[head, tile, tile, col, row, col, head, head],
        out_specs=[head, tile, tile, col, row],
        out_shape=[jax.ShapeDtypeStruct((S, w), F32), jax.ShapeDtypeStruct((S, w), F32), jax.ShapeDtypeStruct((S, w), BF16),
                   jax.ShapeDtypeStruct((FOX_H, S, 1), F32), jax.ShapeDtypeStruct((FOX_H, nb, 1, T), F32)],
        compiler_params=_params("arbitrary", "arbitrary"),
    )(qn, kn, vb, c_col, c_row, lse, o, do)


def _lru_gates(uc, wra, bra, wri, bri, lam):
    ucb = uc.astype(BF16)
    r = jax.nn.sigmoid(_dot(ucb, wra.astype(BF16)) + bra)
    ig = jax.nn.sigmoid(_dot(ucb, wri.astype(BF16)) + bri)
    sp = jnp.maximum(-lam, 0.0) + jnp.log1p(jnp.exp(-jnp.abs(lam)))
    log_a = -LRU_C * r * sp
    a = jnp.exp(log_a)
    mult = jnp.sqrt(_one_minus_exp(2.0 * log_a))
    return r, ig, sp, a, mult


def _conv(pad_ref, cw, cb, S):
    uc = cb
    for j in range(CONV_K):
        uc = uc + cw[j:j + 1, :] * pad_ref[5 + j:5 + j + S, :]
    return uc


def _lru_specs(S, n_proj_cols):
    u_col = 3 * FOX_H
    g_col = u_col + LRU_NB
    blk = pl.BlockSpec((S, HEAD), lambda n: (0, n))
    vec = pl.BlockSpec((1, HEAD), lambda n: (0, n))
    mat = pl.BlockSpec((None, HEAD, HEAD), lambda n: (n, 0, 0))
    return dict(
        u=pl.BlockSpec((S, HEAD), lambda n: (0, u_col + n)), gate=pl.BlockSpec((S, HEAD), lambda n: (0, g_col + n)),
        blk=blk, vec=vec, mat=mat, cw=pl.BlockSpec((CONV_K, HEAD), lambda n: (0, n)))


def _lru_fwd(proj, conv_w, conv_b, w_ra, b_ra, w_ri, b_ri, lam, name):
    S = proj.shape[0]
    sp_ = _lru_specs(S, proj.shape[1])
    rows8 = S // 8

    def body(u_ref, gt_ref, cw_ref, cb_ref, wra_ref, bra_ref, wri_ref, bri_ref, lam_ref, y_ref, h_ref, pad, a_scr, b_scr):
        pad[0:8, :] = jnp.zeros((8, HEAD), F32)
        pad[8:S + 8, :] = u_ref[...]
        uc = _conv(pad, cw_ref[...], cb_ref[...], S)
        r, ig, sp, a, mult = _lru_gates(uc, wra_ref[...], bra_ref[...], wri_ref[...], bri_ref[...], lam_ref[...])
        a_scr[...] = a
        b_scr[...] = mult * (ig * uc)
        sub = lax.broadcasted_iota(jnp.int32, (8, HEAD), 0)

        def step(t, carry):
            sl = pl.ds(pl.multiple_of(t * 8, 8), 8)
            A, B = a_scr[sl, :], b_scr[sl, :]
            for d in (1, 2, 4):
                m = sub >= d
                B = jnp.where(m, A * pltpu.roll(B, d, 0) + B, B)
                A = jnp.where(m, A * pltpu.roll(A, d, 0), A)
            h = A * carry + B
            h_ref[sl, :] = h
            return h[7:8, :]

        lax.fori_loop(0, rows8, step, jnp.zeros((1, HEAD), F32))
        y_ref[...] = h_ref[...] * _gelu(gt_ref[...])

    w = LRU_NB * HEAD
    return pl.pallas_call(
        body, name=name, grid=(LRU_NB,),
        in_specs=[sp_["u"], sp_["gate"], sp_["cw"], sp_["vec"], sp_["mat"], sp_["vec"], sp_["mat"], sp_["vec"], sp_["vec"]],
        out_specs=[sp_["blk"], sp_["blk"]],
        out_shape=[jax.ShapeDtypeStruct((S, w), F32)] * 2,
        scratch_shapes=[pltpu.VMEM((S + 8, HEAD), F32), pltpu.VMEM((S, HEAD), F32), pltpu.VMEM((S, HEAD), F32)],
        compiler_params=_params("parallel"),
    )(proj, proj, conv_w, conv_b, w_ra, b_ra, w_ri, b_ri, lam)


def _lru_bwd(proj, h, dy, conv_w, conv_b, w_ra, b_ra, w_ri, b_ri, lam, name):
    S = proj.shape[0]
    sp_ = _lru_specs(S, proj.shape[1])
    rows8 = S // 8

    def body(u_ref, gt_ref, h_ref, dy_ref, cw_ref, cb_ref, wra_ref, bra_ref, wri_ref, bri_ref, lam_ref,
             du_ref, dgt_ref, dcw_ref, dcb_ref, dwra_ref, dbra_ref, dwri_ref, dbri_ref, dlam_ref,
             pad, an_scr, d_scr, g_scr, hp_scr):
        zero8 = jnp.zeros((8, HEAD), F32)
        pad[0:8, :] = zero8
        pad[8:S + 8, :] = u_ref[...]
        cw = cw_ref[...]
        uc = _conv(pad, cw, cb_ref[...], S)
        wra, wri, lam_v = wra_ref[...], wri_ref[...], lam_ref[...]
        r, ig, sp, a, mult = _lru_gates(uc, wra, bra_ref[...], wri, bri_ref[...], lam_v)
        gate = gt_ref[...]
        dy_v = dy_ref[...]
        hv = h_ref[...]
        dgt_ref[...] = (dy_v * hv * _gelu_grad(gate)).astype(BF16)
        d_scr[...] = dy_v * _gelu(gate)
        g_scr[0:S, :] = a
        g_scr[S:S + 8, :] = zero8
        an_scr[...] = g_scr[1:S + 1, :]
        sub = lax.broadcasted_iota(jnp.int32, (8, HEAD), 0)

        def step(t, carry):
            sl = pl.ds(pl.multiple_of((rows8 - 1 - t) * 8, 8), 8)
            A, D = an_scr[sl, :], d_scr[sl, :]
            for d in (1, 2, 4):
                m = sub + d <= 7
                D = jnp.where(m, A * pltpu.roll(D, 8 - d, 0) + D, D)
                A = jnp.where(m, A * pltpu.roll(A, 8 - d, 0), A)
            g = A * carry + D
            g_scr[sl, :] = g
            return g[0:1, :]

        lax.fori_loop(0, rows8, step, jnp.zeros((1, HEAD), F32))
        g = g_scr[0:S, :]
        hp_scr[0:8, :] = zero8
        hp_scr[8:S + 8, :] = hv
        da = g * hp_scr[7:S + 7, :]
        iu = ig * uc
        dmult = g * iu
        diu = g * mult
        dig = diu * uc
        duc = diu * ig
        dlog_a = da * a - dmult * (a * a) / mult
        dr = dlog_a * (-LRU_C * sp)
        dsp = jnp.sum(dlog_a * (-LRU_C * r), axis=0, keepdims=True)
        dlam_ref[...] = -dsp * jax.nn.sigmoid(-lam_v)
        dpr = dr * r * (1.0 - r)
        dpi = dig * ig * (1.0 - ig)
        dbra_ref[...] = jnp.sum(dpr, axis=0, keepdims=True)
        dbri_ref[...] = jnp.sum(dpi, axis=0, keepdims=True)
        ucb = uc.astype(BF16)
        dprb, dpib = dpr.astype(BF16), dpi.astype(BF16)
        dwra_ref[...] = _dot(ucb, dprb, TN)
        dwri_ref[...] = _dot(ucb, dpib, TN)
        duc = duc + _dot(dprb, wra.astype(BF16), NT) + _dot(dpib, wri.astype(BF16), NT)
        dcb_ref[...] = jnp.sum(duc, axis=0, keepdims=True)
        for j in range(CONV_K):
            dcw_ref[j:j + 1, :] = jnp.sum(duc * pad[5 + j:5 + j + S, :], axis=0, keepdims=True)
        g_scr[0:S, :] = duc
        g_scr[S:S + 8, :] = zero8
        du = jnp.zeros((S, HEAD), F32)
        for j in range(CONV_K):
            du = du + cw[j:j + 1, :] * g_scr[3 - j:3 - j + S, :]
        du_ref[...] = du.astype(BF16)

    w = LRU_NB * HEAD
    bf = jax.ShapeDtypeStruct((S, w), BF16)
    vec = jax.ShapeDtypeStruct((1, w), F32)
    mat = jax.ShapeDtypeStruct((LRU_NB, HEAD, HEAD), F32)
    return pl.pallas_call(
        body, name=name, grid=(LRU_NB,),
        in_specs=[sp_["u"], sp_["gate"], sp_["blk"], sp_["blk"], sp_["cw"], sp_["vec"], sp_["mat"], sp_["vec"],
                  sp_["mat"], sp_["vec"], sp_["vec"]],
        out_specs=[sp_["blk"], sp_["blk"], sp_["cw"], sp_["vec"], sp_["mat"], sp_["vec"], sp_["mat"], sp_["vec"], sp_["vec"]],
        out_shape=[bf, bf, jax.ShapeDtypeStruct((CONV_K, w), F32), vec, mat, vec, mat, vec, vec],
        scratch_shapes=[pltpu.VMEM((S + 8, HEAD), F32), pltpu.VMEM((S, HEAD), F32), pltpu.VMEM((S, HEAD), F32),
                        pltpu.VMEM((S + 8, HEAD), F32), pltpu.VMEM((S + 8, HEAD), F32)],
        compiler_params=_params("parallel"),
    )(proj, proj, h, dy, conv_w, conv_b, w_ra, b_ra, w_ri, b_ri, lam)


def _xattn_fwd(cq, ckv, g_cq, g_ck, name):
    S, w = cq.shape
    M = ckv.shape[0]
    tr = _row_tile(S)
    scale = 1.0 / math.sqrt(HEAD)

    def body(cq_ref, ckv_ref, gq_ref, gk_ref, o_ref):
        for h in range(XATT_H):
            sl = slice(h * HEAD, (h + 1) * HEAD)
            qn = _rms(cq_ref[:, sl], gq_ref[...]).astype(BF16)
            kn = _rms(ckv_ref[:, sl], gk_ref[...]).astype(BF16)
            v = ckv_ref[:, w + h * HEAD:w + (h + 1) * HEAD].astype(BF16)
            s = _dot(qn, kn, NT) * scale
            p = jnp.exp(s - jnp.max(s, axis=1, keepdims=True))
            p = p / jnp.sum(p, axis=1, keepdims=True)
            o_ref[:, sl] = _dot(p.astype(BF16), v).astype(BF16)

    gain = pl.BlockSpec((1, HEAD), lambda i: (0, 0))
    return pl.pallas_call(
        body, name=name, grid=(S // tr,),
        in_specs=[pl.BlockSpec((tr, w), lambda i: (i, 0)), pl.BlockSpec((M, 2 * w), lambda i: (0, 0)), gain, gain],
        out_specs=pl.BlockSpec((tr, w), lambda i: (i, 0)), out_shape=jax.ShapeDtypeStruct((S, w), BF16),
        compiler_params=_params("parallel"),
    )(cq, ckv, g_cq, g_ck)


def _xattn_bwd(cq, ckv, g_cq, g_ck, do, name):
    S, w = cq.shape
    M = ckv.shape[0]
    tr = _row_tile(S)
    nsteps = S // tr
    scale = 1.0 / math.sqrt(HEAD)

    def body(cq_ref, ckv_ref, gq_ref, gk_ref, do_ref, dcq_ref, dckv_ref, dgq_ref, dgk_ref, dkn_scr, dv_scr):
        step = pl.program_id(0)

        @pl.when(step == 0)
        def _():
            dkn_scr[...] = jnp.zeros_like(dkn_scr)
            dv_scr[...] = jnp.zeros_like(dv_scr)
            dgq_ref[...] = jnp.zeros_like(dgq_ref)

        for h in range(XATT_H):
            sl = slice(h * HEAD, (h + 1) * HEAD)
            q_raw = cq_ref[:, sl]
            qn = _rms(q_raw, gq_ref[...]).astype(BF16)
            kn = _rms(ckv_ref[:, sl], gk_ref[...]).astype(BF16)
            v = ckv_ref[:, w + h * HEAD:w + (h + 1) * HEAD].astype(BF16)
            s = _dot(qn, kn, NT) * scale
            p = jnp.exp(s - jnp.max(s, axis=1, keepdims=True))
            p = p / jnp.sum(p, axis=1, keepdims=True)
            dob = do_ref[:, sl].astype(BF16)
            dp = _dot(dob, v, NT)
            ds = p * (dp - jnp.sum(p * dp, axis=1, keepdims=True)) * scale
            dsb = ds.astype(BF16)
            dv_scr[:, sl] += _dot(p.astype(BF16), dob, TN)
            dkn_scr[:, sl] += _dot(dsb, qn, TN)
            dq, dgq = _rms_grad(q_raw, gq_ref[...], _dot(dsb, kn))
            dcq_ref[:, sl] = dq.astype(BF16)
            dgq_ref[...] += dgq

        @pl.when(step == nsteps - 1)
        def _():
            dgk = jnp.zeros((1, HEAD), F32)
            for h in range(XATT_H):
                sl = slice(h * HEAD, (h + 1) * HEAD)
                dk, dgk_h = _rms_grad(ckv_ref[:, sl], gk_ref[...], dkn_scr[:, sl])
                dckv_ref[:, sl] = dk.astype(BF16)
                dgk = dgk + dgk_h
            dckv_ref[:, w:2 * w] = dv_scr[...].astype(BF16)
            dgk_ref[...] = dgk

    gain = pl.BlockSpec((1, HEAD), lambda i: (0, 0))
    row = pl.BlockSpec((tr, w), lambda i: (i, 0))
    mem = pl.BlockSpec((M, 2 * w), lambda i: (0, 0))
    return pl.pallas_call(
        body, name=name, grid=(nsteps,), in_specs=[row, mem, gain, gain, row], out_specs=[row, mem, gain, gain],
        out_shape=[jax.ShapeDtypeStruct((S, w), BF16), jax.ShapeDtypeStruct((M, 2 * w), BF16),
                   jax.ShapeDtypeStruct((1, HEAD), F32), jax.ShapeDtypeStruct((1, HEAD), F32)],
        scratch_shapes=[pltpu.VMEM((M, w), F32), pltpu.VMEM((M, w), F32)],
        compiler_params=_params("arbitrary"),
    )(cq, ckv, g_cq, g_ck, do)


def _swiglu_fwd(gu, name, n_ct=4):
    S, w2 = gu.shape
    f = w2 // 2
    tc = f // n_ct
    tr = _row_tile(S)

    def body(g_ref, u_ref, o_ref):
        g = g_ref[...]
        o_ref[...] = (g * jax.nn.sigmoid(g) * u_ref[...]).astype(BF16)

    return pl.pallas_call(
        body, name=name, grid=(S // tr, n_ct),
        in_specs=[pl.BlockSpec((tr, tc), lambda i, j: (i, j)), pl.BlockSpec((tr, tc), lambda i, j: (i, j + n_ct))],
        out_specs=pl.BlockSpec((tr, tc), lambda i, j: (i, j)), out_shape=jax.ShapeDtypeStruct((S, f), BF16),
        compiler_params=_params("parallel", "parallel"),
    )(gu, gu)


def _swiglu_bwd(gu, dact, name, n_ct=4):
    S, w2 = gu.shape
    f = w2 // 2
    tc = f // n_ct
    tr = _row_tile(S)

    def body(g_ref, u_ref, da_ref, o_ref):
        g = g_ref[...]
        sg = jax.nn.sigmoid(g)
        da = da_ref[...]
        half = pl.program_id(1) // n_ct

        @pl.when(half == 0)
        def _():
            o_ref[...] = (da * u_ref[...] * (sg * (1.0 + g * (1.0 - sg)))).astype(BF16)

        @pl.when(half == 1)
        def _():
            o_ref[...] = (da * (g * sg)).astype(BF16)

    return pl.pallas_call(
        body, name=name, grid=(S // tr, 2 * n_ct),
        in_specs=[pl.BlockSpec((tr, tc), lambda i, j: (i, j % n_ct)),
                  pl.BlockSpec((tr, tc), lambda i, j: (i, j % n_ct + n_ct)),
                  pl.BlockSpec((tr, tc), lambda i, j: (i, j % n_ct))],
        out_specs=pl.BlockSpec((tr, tc), lambda i, j: (i, j)), out_shape=jax.ShapeDtypeStruct((S, w2), BF16),
        compiler_params=_params("parallel", "parallel"),
    )(gu, gu, dact)


def _adamw_math(w, g, m, v):
    m = ADAM_B1 * m + (1.0 - ADAM_B1) * g
    v = ADAM_B2 * v + (1.0 - ADAM_B2) * (g * g)
    m_hat = m / (1.0 - ADAM_B1 ** ADAM_STEP)
    v_hat = v / (1.0 - ADAM_B2 ** ADAM_STEP)
    delta = -ADAM_LR * (m_hat / (jnp.sqrt(v_hat) + ADAM_EPS) + ADAM_WD * w)
    return delta, m, v


def _reduce_adamw(parts, w, m, v, name):
    P, R, C = parts.shape
    tr = _row_tile(R, cap=max(16, (1 << 18) // max(C, 1) // 16 * 16))

    def body(p_ref, w_ref, m_ref, v_ref, g_ref, d_ref, nm_ref, nv_ref):
        g = p_ref[0].astype(F32)
        for q in range(1, P):
            g = g + p_ref[q].astype(F32)
        g_ref[...] = g
        d_ref[...], nm_ref[...], nv_ref[...] = _adamw_math(w_ref[...], g, m_ref[...], v_ref[...])

    blk = pl.BlockSpec((tr, C), lambda i: (i, 0))
    sds = jax.ShapeDtypeStruct((R, C), F32)
    return pl.pallas_call(
        body, name=name, grid=(R // tr,), in_specs=[pl.BlockSpec((P, tr, C), lambda i: (0, i, 0)), blk, blk, blk],
        out_specs=[blk] * 4, out_shape=[sds] * 4, compiler_params=_params("parallel"),
    )(parts, w, m, v)


def _pair_sum(g8, recv, core, name):
    _, R, C = g8.shape
    tr = _row_tile(R, cap=max(16, (1 << 19) // max(C, 1) // 16 * 16))

    def body(c_ref, a_ref, b_ref, o_ref):
        o_ref[...] = (a_ref[...].astype(F32) + b_ref[...].astype(F32)).astype(BF16)

    return pl.pallas_call(
        body, name=name,
        grid_spec=pltpu.PrefetchScalarGridSpec(
            num_scalar_prefetch=1, grid=(N_CHIP, R // tr),
            in_specs=[pl.BlockSpec((None, tr, C), lambda q, i, c: (2 * q + c[0], i, 0)),
                      pl.BlockSpec((None, tr, C), lambda q, i, c: (q, i, 0))],
            out_specs=pl.BlockSpec((None, tr, C), lambda q, i, c: (q, i, 0))),
        out_shape=jax.ShapeDtypeStruct((N_CHIP, R, C), BF16),
        compiler_params=_params("parallel", "parallel"),
    )(core, g8, recv)


ANY = pl.BlockSpec(memory_space=pl.ANY)


def _place():
    return lax.axis_index("x"), lax.axis_index("y"), lax.axis_index("c")


def _all_gather(shards, name):
    n = len(shards)

    def body(*refs):
        ins, outs = refs[:n], refs[n:2 * n]
        send_sems, recv_sems, local_sems = refs[2 * n:]
        x, y, c = _place()
        me, sibling = (x, y, c), (x, y, 1 - c)
        chips = [(1 - x, y), (x, 1 - y), (1 - x, 1 - y)]

        def copy(a, k, block, to, src=None):
            dst = outs[a].at[4 * block[0] + 2 * block[1] + block[2]]
            return pltpu.make_async_remote_copy(
                src_ref=dst if src is None else src, dst_ref=dst, send_sem=send_sems.at[a, k],
                recv_sem=recv_sems.at[a, k], device_id=to, device_id_type=MESH)

        mine = [pltpu.make_async_copy(ins[a], outs[a].at[4 * x + 2 * y + c], local_sems.at[a]) for a in range(n)]
        for cp in mine:
            cp.start()
        sent = []
        for a in range(n):
            sent.append(copy(a, 0, me, sibling, src=ins[a]))
            sent += [copy(a, 1 + j, me, (*chip, c), src=ins[a]) for j, chip in enumerate(chips)]
        for cp in sent:
            cp.start()
        for j, chip in enumerate(chips):
            for a in range(n):
                copy(a, 1 + j, (*chip, c), me).wait_recv()
                fwd = copy(a, 4 + j, (*chip, c), sibling)
                fwd.start()
                sent.append(fwd)
        for a in range(n):
            copy(a, 0, sibling, me).wait_recv()
        for j, chip in enumerate(chips):
            for a in range(n):
                copy(a, 4 + j, (*chip, 1 - c), me).wait_recv()
        for cp in sent:
            cp.wait_send()
        for cp in mine:
            cp.wait()

    return pl.pallas_call(
        body, name=name, in_specs=[ANY] * n, out_specs=[ANY] * n,
        out_shape=[jax.ShapeDtypeStruct((N_DEV,) + s.shape, s.dtype) for s in shards],
        scratch_shapes=[pltpu.SemaphoreType.DMA((n, 7)), pltpu.SemaphoreType.DMA((n, 7)), pltpu.SemaphoreType.DMA((n,))],
    )(*shards)


def _pair_exchange(g8s, name):
    n = len(g8s)

    def body(*refs):
        ins, outs = refs[:n], refs[n:2 * n]
        send_sems, recv_sems = refs[2 * n:]
        x, y, c = _place()
        copies = []
        for a in range(n):
            for q in range(N_CHIP):
                copies.append(pltpu.make_async_remote_copy(
                    src_ref=ins[a].at[2 * q + 1 - c], dst_ref=outs[a].at[q], send_sem=send_sems.at[a, q],
                    recv_sem=recv_sems.at[a, q], device_id=(x, y, 1 - c), device_id_type=MESH))
        for cp in copies:
            cp.start()
        for cp in copies:
            cp.wait()

    return pl.pallas_call(
        body, name=name, in_specs=[ANY] * n, out_specs=[ANY] * n,
        out_shape=[jax.ShapeDtypeStruct((N_CHIP,) + g.shape[1:], g.dtype) for g in g8s],
        scratch_shapes=[pltpu.SemaphoreType.DMA((n, N_CHIP)), pltpu.SemaphoreType.DMA((n, N_CHIP))],
    )(*g8s)


def _chip_exchange(p4s, name):
    n = len(p4s)

    def body(*refs):
        ins, outs = refs[:n], refs[n:2 * n]
        send_sems, recv_sems, local_sems = refs[2 * n:]
        x, y, c = _place()
        my_chip = 2 * x + y
        others = [(1 - x, y), (x, 1 - y), (1 - x, 1 - y)]
        local = [pltpu.make_async_copy(ins[a].at[my_chip], outs[a].at[my_chip], local_sems.at[a]) for a in range(n)]
        for cp in local:
            cp.start()
        copies = []
        for a in range(n):
            for k, (px, py) in enumerate(others):
                copies.append(pltpu.make_async_remote_copy(
                    src_ref=ins[a].at[2 * px + py], dst_ref=outs[a].at[my_chip], send_sem=send_sems.at[a, k],
                    recv_sem=recv_sems.at[a, k], device_id=(px, py, c), device_id_type=MESH))
        for cp in copies:
            cp.start()
        for a in range(n):
            for k, (px, py) in enumerate(others):
                pltpu.make_async_remote_copy(
                    src_ref=ins[a].at[my_chip], dst_ref=outs[a].at[2 * px + py], send_sem=send_sems.at[a, k],
                    recv_sem=recv_sems.at[a, k], device_id=(px, py, c), device_id_type=MESH).wait_recv()
        for cp in copies:
            cp.wait_send()
        for cp in local:
            cp.wait()

    return pl.pallas_call(
        body, name=name, in_specs=[ANY] * n, out_specs=[ANY] * n,
        out_shape=[jax.ShapeDtypeStruct(p.shape, p.dtype) for p in p4s],
        scratch_shapes=[pltpu.SemaphoreType.DMA((n, 3)), pltpu.SemaphoreType.DMA((n, 3)), pltpu.SemaphoreType.DMA((n,))],
    )(*p4s)


def _pack(arrs):
    parts = []
    for a in arrs:
        flat = a.reshape(-1).astype(F32)
        padded = -(-flat.shape[0] // 1024) * 1024
        parts.append(jnp.pad(flat, (0, padded - flat.shape[0])).reshape(padded // 128, 128))
    return jnp.concatenate(parts, axis=0)


def _unpack(packed, like):
    out, row = [], 0
    for a in like:
        size = math.prod(a.shape)
        rows = -(-size // 1024) * 8
        out.append(packed[row:row + rows].reshape(-1)[:size].reshape(a.shape))
        row += rows
    return out


def _sum_devices(parts, name):
    P, R, C = parts.shape
    tr = _row_tile(R, cap=512, mult=8)

    def body(p_ref, o_ref):
        g = p_ref[0]
        for q in range(1, P):
            g = g + p_ref[q]
        o_ref[...] = g

    return pl.pallas_call(
        body, name=name, grid=(R // tr,), in_specs=[pl.BlockSpec((P, tr, C), lambda i: (0, i, 0))],
        out_specs=pl.BlockSpec((tr, C), lambda i: (i, 0)), out_shape=jax.ShapeDtypeStruct((R, C), F32),
        compiler_params=_params("parallel"),
    )(parts)


def _adamw_packed(w, g, m, v, name):
    R, C = w.shape
    tr = _row_tile(R, cap=512, mult=8)

    def body(w_ref, g_ref, m_ref, v_ref, d_ref, nm_ref, nv_ref):
        d_ref[...], nm_ref[...], nv_ref[...] = _adamw_math(w_ref[...], g_ref[...], m_ref[...], v_ref[...])

    blk = pl.BlockSpec((tr, C), lambda i: (i, 0))
    sds = jax.ShapeDtypeStruct((R, C), F32)
    return pl.pallas_call(
        body, name=name, grid=(R // tr,), in_specs=[blk] * 4, out_specs=[blk] * 3, out_shape=[sds] * 3,
        compiler_params=_params("parallel"),
    )(w, g, m, v)


def _local_step(x, mem, target, sp, W):
    S, D = x.shape
    fw = FOX_H * HEAD
    h0 = _rms_fwd([x], [sp["g_mix"]], "rms_mix")
    proj = _mm(h0, W["main"], "nn", F32, "proj_main", tk=2048)
    f_logit = _mm(h0, W["f"], "nn", F32, "proj_f", tk=2048)
    qn, kn, vb = _qk_fwd(proj, sp["g_q"], sp["g_k"], "qk_norm")
    b_pad = jnp.pad(sp["b_f"], ((0, 0), (0, HEAD - FOX_H)))
    c_col, c_row = _fgate_fwd(f_logit, b_pad, "forget_cumsum")
    o_fox, lse = _fox_fwd(qn, kn, vb, c_col, c_row, "fox_fwd")
    lru_p = (sp["conv_w"], sp["conv_b"], sp["w_ra"], sp["b_ra"], sp["w_ri"], sp["b_ri"], sp["lam"])
    y_lru, h_lru = _lru_fwd(proj, *lru_p, "lru_fwd")
    mix = _rms_fwd([o_fox, y_lru], [sp["g_fox_out"], sp["g_lru_out"]], "rms_mix_out")
    x1 = _mm(mix, W["out"], "nn", F32, "out_proj", add=x, tk=2048)
    hq = _rms_fwd([x1], [sp["g_xattn"]], "rms_xattn")
    mn = _rms_fwd([mem], [sp["g_mem"]], "rms_mem")
    cq = _mm(hq, W["cq"], "nn", F32, "xattn_q", tk=2048)
    ckv = _mm(mn, W["ckv"], "nn", F32, "xattn_kv", tk=2048)
    o_x = _xattn_fwd(cq, ckv, sp["g_cq"], sp["g_ck"], "xattn_fwd")
    x2 = _mm(o_x, W["co"], "nn", F32, "xattn_out", add=x1, tn=256)
    hf = _rms_fwd([x2], [sp["g_ffn"]], "rms_ffn")
    gu_w = W["gu"].shape[2]
    gu = _mm(hf, W["gu"], "nn", F32, "ffn_up", tn=gu_w, tk=1024)
    act = _swiglu_fwd(gu, "swiglu_fwd")
    x3 = _mm(act, W["down"], "nn", F32, "ffn_down", add=x2)
    dy, dyb, sq = _loss_head(x3, target, "loss_head")
    G = {}
    dact = _mm(dyb, W["down"], "nt", F32, "ffn_down_dx", tk=2048)
    G["down"] = _mm(act, dyb, "tn", BF16, "ffn_down_dw")
    dgu = _swiglu_bwd(gu, dact, "swiglu_bwd")
    dhf = _mm(dgu, W["gu"], "nt", F32, "ffn_up_dx", tk=gu_w)
    G["gu"] = _mm(hf, dgu, "tn", BF16, "ffn_up_dw", out_blocks=N_DEV, tn=gu_w)
    dx2, dx2b, dg_ffn = _rms_bwd(x2, sp["g_ffn"], dhf, "rms_ffn_bwd", res=dy)
    d_ox = _mm(dx2b, W["co"], "nt", F32, "xattn_out_dx", tk=256)
    G["co"] = _mm(o_x, dx2b, "tn", BF16, "xattn_out_dw", out_blocks=N_DEV, tn=256)
    dcq, dckv, dg_cq, dg_ck = _xattn_bwd(cq, ckv, sp["g_cq"], sp["g_ck"], d_ox, "xattn_bwd")
    dhq = _mm(dcq, W["cq"], "nt", F32, "xattn_q_dx")
    G["cq"] = _mm(hq, dcq, "tn", BF16, "xattn_q_dw")
    dmn = _mm(dckv, W["ckv"], "nt", F32, "xattn_kv_dx")
    G["ckv"] = _mm(mn, dckv, "tn", BF16, "xattn_kv_dw")
    (dg_mem,) = _rms_bwd(mem, sp["g_mem"], dmn, "rms_mem_bwd", want_dx=False)
    dx1, dx1b, dg_xattn = _rms_bwd(x1, sp["g_xattn"], dhq, "rms_xattn_bwd", res=dx2)
    dmix = _mm(dx1b, W["out"], "nt", F32, "out_proj_dx", tk=2048)
    G["out"] = _mm(mix, dx1b, "tn", BF16, "out_proj_dw")
    do_fox, _, dg_fox_out = _rms_bwd(o_fox, sp["g_fox_out"], dmix, "rms_fox_out_bwd", dy_col=0)
    dy_lru, _, dg_lru_out = _rms_bwd(y_lru, sp["g_lru_out"], dmix, "rms_lru_out_bwd", dy_col=1)
    du, dgate, dconv_w, dconv_b, dw_ra, db_ra, dw_ri, db_ri, dlam = _lru_bwd(proj, h_lru, dy_lru, *lru_p, "lru_bwd")
    dqn, dkn, dv, dc_col, dc_row = _fox_bwd(qn, kn, vb, c_col, c_row, lse, o_fox, do_fox, "fox_bwd")
    dq, dk, dg_q, dg_k = _qk_bwd(proj, sp["g_q"], sp["g_k"], dqn, dkn, "qk_norm_bwd")
    df, db_f = _fgate_bwd(f_logit, b_pad, dc_col, dc_row, "forget_cumsum_bwd")
    dproj = jnp.concatenate([dq, dk, dv, du, dgate], axis=1)
    dh_f = _mm(df, W["f"], "nt", F32, "proj_f_dx")
    dh0 = _mm(dproj, W["main"], "nt", F32, "proj_main_dx", add=dh_f, tk=1024)
    G["main"] = _mm(h0, dproj, "tn", BF16, "proj_main_dw")
    G["f"] = _mm(h0, df, "tn", BF16, "proj_f_dw")
    grad_x, _, dg_mix = _rms_bwd(x, sp["g_mix"], dh0, "rms_mix_bwd", res=dx1)
    small = dict(g_mix=dg_mix, b_f=db_f[:, :FOX_H], g_q=dg_q, g_k=dg_k, conv_w=dconv_w, conv_b=dconv_b, w_ra=dw_ra,
                 b_ra=db_ra, w_ri=dw_ri, b_ri=db_ri, lam=dlam, g_fox_out=dg_fox_out, g_lru_out=dg_lru_out,
                 g_xattn=dg_xattn, g_mem=dg_mem, g_cq=dg_cq, g_ck=dg_ck, g_ffn=dg_ffn)
    return sq, grad_x, G, small


BIG = ("w_in", "w_out", "w_cq", "w_ckv", "w_co", "w_gate_up", "w_down")
SMALL = ("g_mix", "b_f", "g_q", "g_k", "conv_b", "w_ra", "b_ra", "w_ri", "b_ri", "lam", "g_fox_out", "g_lru_out",
         "g_xattn", "g_mem", "g_cq", "g_ck", "g_ffn")
ORDER = ("g_mix", "w_in", "b_f", "g_q", "g_k", "conv_w", "conv_b", "w_ra", "b_ra", "w_ri", "b_ri", "lam", "g_fox_out",
         "g_lru_out", "w_out", "g_xattn", "g_mem", "w_cq", "w_ckv", "g_cq", "g_ck", "w_co", "g_ffn", "w_gate_up", "w_down")


def kernel(x, mem, g_mix, w_in, b_f, g_q, g_k, conv_w, conv_b, w_ra, b_ra, w_ri, b_ri, lam, g_fox_out, g_lru_out, w_out, g_xattn, g_mem, w_cq, w_ckv, g_cq, g_ck, w_co, g_ffn, w_gate_up, w_down, loss_target, m_g_mix, m_w_in, m_b_f, m_g_q, m_g_k, m_conv_w, m_conv_b, m_w_ra, m_b_ra, m_w_ri, m_b_ri, m_lam, m_g_fox_out, m_g_lru_out, m_w_out, m_g_xattn, m_g_mem, m_w_cq, m_w_ckv, m_g_cq, m_g_ck, m_w_co, m_g_ffn, m_w_gate_up, m_w_down, v_g_mix, v_w_in, v_b_f, v_g_q, v_g_k, v_conv_w, v_conv_b, v_w_ra, v_b_ra, v_w_ri, v_b_ri, v_lam, v_g_fox_out, v_g_lru_out, v_w_out, v_g_xattn, v_g_mem, v_w_cq, v_w_ckv, v_g_cq, v_g_ck, v_w_co, v_g_ffn, v_w_gate_up, v_w_down):
    given = dict(locals())
    w = {n: given[n] for n in ORDER}
    m = {n: given["m_" + n] for n in ORDER}
    v = {n: given["v_" + n] for n in ORDER}
    S, D = x.shape[1], x.shape[2]
    fw = FOX_H * HEAD
    in_w = w_in.shape[2] * N_DEV
    dev = 4 * lax.axis_index("x") + 2 * lax.axis_index("y") + lax.axis_index("c")

    shards = [w[n][0].astype(BF16) for n in BIG] + [conv_w[0]]
    g_in, g_out, g_cq_w, g_ckv, g_co, g_gu, g_down, g_conv = _all_gather(shards, "gather_weights")
    w_in_full = jnp.transpose(g_in, (1, 0, 2)).reshape(D, in_w)
    W = dict(
        main=jnp.concatenate([w_in_full[:, :3 * fw], w_in_full[:, 3 * fw + FOX_H:]], axis=1),
        f=jnp.pad(w_in_full[:, 3 * fw:3 * fw + FOX_H], ((0, 0), (0, HEAD - FOX_H))),
        out=g_out.reshape(-1, g_out.shape[2]), cq=g_cq_w.reshape(-1, g_cq_w.shape[2]),
        ckv=g_ckv.reshape(-1, g_ckv.shape[2]), co=g_co, gu=g_gu, down=g_down.reshape(-1, g_down.shape[2]))
    sp = {n: w[n] for n in SMALL if n not in ("w_ra", "w_ri")}
    sp["w_ra"], sp["w_ri"] = w_ra[0], w_ri[0]
    sp["conv_w"] = jnp.transpose(g_conv, (1, 0, 2)).reshape(CONV_K, -1)

    sq, grad_x, G, gs = _local_step(x[0], mem[0], loss_target[0], sp, W)
    loss = lax.psum(0.5 * sq[0, 0] / D, AXES)

    dw_in = jnp.concatenate([G["main"][:, :3 * fw], G["f"][:, :FOX_H], G["main"][:, 3 * fw:]], axis=1)
    g8 = [jnp.transpose(dw_in.reshape(D, N_DEV, in_w // N_DEV), (1, 0, 2)),
          G["out"].reshape(N_DEV, -1, G["out"].shape[1]), G["cq"].reshape(N_DEV, -1, G["cq"].shape[1]),
          G["ckv"].reshape(N_DEV, -1, G["ckv"].shape[1]), G["co"], G["gu"],
          G["down"].reshape(N_DEV, -1, G["down"].shape[1])]
    core = jnp.reshape(lax.axis_index("c"), (1,)).astype(jnp.int32)
    from_sibling = _pair_exchange(g8, "reduce_pair_exchange")
    p4 = [_pair_sum(g, r, core, "reduce_pair_sum_" + n) for g, r, n in zip(g8, from_sibling, BIG)]
    by_chip = _chip_exchange(p4, "reduce_chip_exchange")
    out = {}
    for n, parts in zip(BIG, by_chip):
        res = _reduce_adamw(parts, w[n][0], m[n][0], v[n][0], "adamw_" + n)
        out[n] = tuple(r[None] for r in res)

    small_names = SMALL + ("conv_w",)
    gathered = _all_gather([_pack([gs[n] for n in small_names])], "gather_small_grads")[0]
    g_sum = _unpack(_sum_devices(gathered, "sum_small_grads"), [gs[n] for n in small_names])
    g_small = dict(zip(small_names, g_sum))
    cw = conv_w.shape[2]
    g_small["conv_w"] = lax.dynamic_slice_in_dim(g_small["conv_w"], dev * cw, cw, axis=1)
    g_small = {n: g_small[n].reshape(w[n].shape) for n in small_names}
    packed = [_pack([d[n] for n in small_names]) for d in (w, g_small, m, v)]
    upd = _adamw_packed(*packed, "adamw_small")
    like = [w[n] for n in small_names]
    d_s, m_s, v_s = (dict(zip(small_names, _unpack(u, like))) for u in upd)
    for n in small_names:
        out[n] = (g_small[n], d_s[n], m_s[n], v_s[n])

    return (loss, grad_x[None], *[out[n][0] for n in ORDER], *[out[n][1] for n in ORDER],
            *[out[n][2] for n in ORDER], *[out[n][3] for n in ORDER])
```

```python
import functools
import math

import jax
import jax.numpy as jnp
from jax import lax
from jax.experimental import pallas as pl
from jax.experimental.pallas import tpu as pltpu

F32 = jnp.float32
BF16 = jnp.bfloat16
MESH = pl.DeviceIdType.MESH
AXES = ("x", "y", "c")
N_DEV = 8
N_CHIP = 4

HEAD = 128
FOX_H = 8
XATT_H = 4
LRU_NB = 8
CONV_K = 4
LRU_C = 8.0
RMS_EPS = 1e-6
ATT_T = 256
ROW_T = 256
NEG = -1e30
V7X_VMEM_LIMIT = 48 * 1024 * 1024

ADAM_LR = 0.001
ADAM_B1 = 0.9
ADAM_B2 = 0.999
ADAM_EPS = 1e-08
ADAM_WD = 0.01
ADAM_STEP = 10

NT = (((1,), (1,)), ((), ()))
TN = (((0,), (0,)), ((), ()))
NN = (((1,), (0,)), ((), ()))


def _params(*sem):
    return pltpu.CompilerParams(dimension_semantics=sem or None, vmem_limit_bytes=V7X_VMEM_LIMIT)


def _dot(a, b, dn=NN):
    return lax.dot_general(a, b, dn, preferred_element_type=F32)


def _rms(x, g):
    r = lax.rsqrt(jnp.mean(x * x, axis=-1, keepdims=True) + RMS_EPS)
    return x * r * g


def _rms_grad(x, g, dy):
    r = lax.rsqrt(jnp.mean(x * x, axis=-1, keepdims=True) + RMS_EPS)
    xh = x * r
    dxh = dy * g
    dx = r * (dxh - xh * jnp.mean(dxh * xh, axis=-1, keepdims=True))
    return dx, jnp.sum(dy * xh, axis=0, keepdims=True)


def _gelu(x):
    k = math.sqrt(2.0 / math.pi)
    return 0.5 * x * (1.0 + jnp.tanh(k * (x + 0.044715 * (x * x * x))))


def _gelu_grad(x):
    k = math.sqrt(2.0 / math.pi)
    t = jnp.tanh(k * (x + 0.044715 * (x * x * x)))
    return 0.5 * (1.0 + t) + 0.5 * x * (1.0 - t * t) * k * (1.0 + 3.0 * 0.044715 * x * x)


def _one_minus_exp(z):
    series = -(z + 0.5 * z * z + (1.0 / 6.0) * z * z * z)
    return jnp.where(z > -1e-3, series, 1.0 - jnp.exp(z))


def _row_tile(rows, cap=ROW_T, mult=16):
    t = min(rows, cap)
    while rows % t or (t % mult and t != rows):
        t -= 1
    return t


def _mat_dims(z):
    return (z.shape[-2], z.shape[-1] * (z.shape[0] if z.ndim == 3 else 1))


def _mat_spec(arr, rblk, cblk, rsel, csel):
    if arr.ndim == 2:
        return pl.BlockSpec((rblk, cblk), lambda i, j, k: ((i, j, k)[rsel], (i, j, k)[csel]))
    nw = arr.shape[2]
    assert nw % cblk == 0, (arr.shape, cblk)
    per = nw // cblk
    return pl.BlockSpec((None, rblk, cblk),
                        lambda i, j, k: ((i, j, k)[csel] // per, (i, j, k)[rsel], (i, j, k)[csel] % per))


def _mm(a, b, mode, out_dtype, name, add=None, out_blocks=None, tm=1024, tn=1024, tk=None):
    ar, ac = _mat_dims(a)
    br, bc = _mat_dims(b)
    if mode == "nn":
        M, K, N = ar, ac, bc
        assert br == K
    elif mode == "nt":
        M, K, N = ar, ac, br
        assert bc == K
    else:
        M, K, N = ac, ar, bc
        assert br == K
    tm, tn = min(tm, M), min(tn, N)
    tk = K if tk is None or mode == "tn" else min(tk, K)
    assert M % tm == 0 and N % tn == 0 and K % tk == 0, (name, M, N, K, tm, tn, tk)
    nk = K // tk
    nj = N // tn
    if mode == "nn":
        specs = [_mat_spec(a, tm, tk, 0, 2), _mat_spec(b, tk, tn, 2, 1)]
        dn = NN
    elif mode == "nt":
        specs = [_mat_spec(a, tm, tk, 0, 2), _mat_spec(b, tn, tk, 1, 2)]
        dn = NT
    else:
        specs = [_mat_spec(a, tk, tm, 2, 0), _mat_spec(b, tk, tn, 2, 1)]
        dn = TN
    args = [a, b]
    if add is not None:
        specs.append(pl.BlockSpec((tm, tn), lambda i, j, k: (i, j)))
        args.append(add)
    if out_blocks is None:
        out_shape = jax.ShapeDtypeStruct((M, N), out_dtype)
        out_spec = pl.BlockSpec((tm, tn), lambda i, j, k: (i, j))
    else:
        nb = out_blocks
        nw = N // nb
        assert nw % tn == 0
        per = nw // tn
        out_shape = jax.ShapeDtypeStruct((nb, M, nw), out_dtype)
        out_spec = pl.BlockSpec((None, tm, tn), lambda i, j, k: (j // per, i, j % per))
    keep_t = mode == "tn" and nj > 1
    scratch = []
    if nk > 1:
        scratch.append(pltpu.VMEM((tm, tn), F32))
    if keep_t:
        scratch.append(pltpu.VMEM((tm, tk), a.dtype))

    def body(*refs):
        a_ref, b_ref = refs[0], refs[1]
        add_ref = refs[2] if add is not None else None
        n_in = 3 if add is not None else 2
        o_ref = refs[n_in]

        def finish(r):
            if add_ref is not None:
                r = r + add_ref[...]
            o_ref[...] = r.astype(out_dtype)

        if keep_t:
            at_ref = refs[-1]

            @pl.when(pl.program_id(1) == 0)
            def _():
                at_ref[...] = a_ref[...].T

            finish(_dot(at_ref[...], b_ref[...], NN))
        elif nk == 1:
            finish(_dot(a_ref[...], b_ref[...], dn))
        else:
            acc_ref = refs[n_in + 1]
            k = pl.program_id(2)

            @pl.when(k == 0)
            def _():
                acc_ref[...] = jnp.zeros_like(acc_ref)

            acc_ref[...] += _dot(a_ref[...], b_ref[...], dn)

            @pl.when(k == nk - 1)
            def _():
                finish(acc_ref[...])

    return pl.pallas_call(
        body, name=name, grid=(M // tm, nj, nk), in_specs=specs, out_specs=out_spec, out_shape=out_shape,
        scratch_shapes=scratch,
        compiler_params=_params("parallel", "arbitrary" if keep_t else "parallel", "arbitrary"),
    )(*args)


def _rms_fwd(xs, gs, name):
    n = len(xs)
    rows = xs[0].shape[0]
    widths = [x.shape[1] for x in xs]
    tr = _row_tile(rows)

    def body(*refs):
        o_ref = refs[2 * n]
        off = 0
        for t in range(n):
            o_ref[:, off:off + widths[t]] = _rms(refs[t][...], refs[n + t][...]).astype(BF16)
            off += widths[t]

    return pl.pallas_call(
        body, name=name, grid=(rows // tr,),
        in_specs=[pl.BlockSpec((tr, w), lambda i: (i, 0)) for w in widths]
        + [pl.BlockSpec((1, w), lambda i: (0, 0)) for w in widths],
        out_specs=pl.BlockSpec((tr, sum(widths)), lambda i: (i, 0)),
        out_shape=jax.ShapeDtypeStruct((rows, sum(widths)), BF16),
        compiler_params=_params("parallel"),
    )(*xs, *gs)


def _rms_bwd(x, g, dy, name, dy_col=0, res=None, want_dx=True):
    rows, w = x.shape
    tr = _row_tile(rows)
    has_res = res is not None

    def body(*refs):
        x_ref, g_ref, dy_ref = refs[:3]
        res_ref = refs[3] if has_res else None
        outs = refs[3 + has_res:]
        dg_ref = outs[-1]
        dx, dg = _rms_grad(x_ref[...], g_ref[...], dy_ref[...])

        @pl.when(pl.program_id(0) == 0)
        def _():
            dg_ref[...] = jnp.zeros_like(dg_ref)

        dg_ref[...] += dg
        if want_dx:
            if has_res:
                dx = dx + res_ref[...]
            outs[0][...] = dx
            outs[1][...] = dx.astype(BF16)

    row_spec = pl.BlockSpec((tr, w), lambda i: (i, 0))
    in_specs = [row_spec, pl.BlockSpec((1, w), lambda i: (0, 0)), pl.BlockSpec((tr, w), lambda i: (i, dy_col))]
    args = [x, g, dy]
    if has_res:
        in_specs.append(row_spec)
        args.append(res)
    out_specs, out_shape = [], []
    if want_dx:
        out_specs += [row_spec, row_spec]
        out_shape += [jax.ShapeDtypeStruct((rows, w), F32), jax.ShapeDtypeStruct((rows, w), BF16)]
    out_specs.append(pl.BlockSpec((1, w), lambda i: (0, 0)))
    out_shape.append(jax.ShapeDtypeStruct((1, w), F32))
    return pl.pallas_call(
        body, name=name, grid=(rows // tr,), in_specs=in_specs, out_specs=out_specs, out_shape=out_shape,
        compiler_params=_params("arbitrary"),
    )(*args)


def _loss_head(y, target, name):
    rows, w = y.shape
    tr = _row_tile(rows)

    def body(y_ref, t_ref, dy_ref, dyb_ref, acc_ref):
        e = y_ref[...] - t_ref[...]

        @pl.when(pl.program_id(0) == 0)
        def _():
            acc_ref[...] = jnp.zeros_like(acc_ref)

        acc_ref[...] += jnp.sum(e * e)
        dy = e * (1.0 / w)
        dy_ref[...] = dy
        dyb_ref[...] = dy.astype(BF16)

    row_spec = pl.BlockSpec((tr, w), lambda i: (i, 0))
    return pl.pallas_call(
        body, name=name, grid=(rows // tr,), in_specs=[row_spec, row_spec],
        out_specs=[row_spec, row_spec, pl.BlockSpec((8, 128), lambda i: (0, 0))],
        out_shape=[jax.ShapeDtypeStruct((rows, w), F32), jax.ShapeDtypeStruct((rows, w), BF16),
                   jax.ShapeDtypeStruct((8, 128), F32)],
        compiler_params=_params("arbitrary"),
    )(y, target)


def _qk_fwd(proj, g_q, g_k, name):
    rows = proj.shape[0]
    w = FOX_H * HEAD
    tr = _row_tile(rows)

    def body(q_ref, k_ref, v_ref, gq_ref, gk_ref, qn_ref, kn_ref, vb_ref):
        for h in range(FOX_H):
            sl = slice(h * HEAD, (h + 1) * HEAD)
            qn_ref[:, sl] = _rms(q_ref[:, sl], gq_ref[...]).astype(BF16)
            kn_ref[:, sl] = _rms(k_ref[:, sl], gk_ref[...]).astype(BF16)
        vb_ref[...] = v_ref[...].astype(BF16)

    gain = pl.BlockSpec((1, HEAD), lambda i: (0, 0))
    out = pl.BlockSpec((tr, w), lambda i: (i, 0))
    return pl.pallas_call(
        body, name=name, grid=(rows // tr,),
        in_specs=[pl.BlockSpec((tr, w), lambda i: (i, 0)), pl.BlockSpec((tr, w), lambda i: (i, 1)),
                  pl.BlockSpec((tr, w), lambda i: (i, 2)), gain, gain],
        out_specs=[out, out, out], out_shape=[jax.ShapeDtypeStruct((rows, w), BF16)] * 3,
        compiler_params=_params("parallel"),
    )(proj, proj, proj, g_q, g_k)


def _qk_bwd(proj, g_q, g_k, dqn, dkn, name):
    rows = proj.shape[0]
    w = FOX_H * HEAD
    tr = _row_tile(rows)

    def body(q_ref, k_ref, gq_ref, gk_ref, dqn_ref, dkn_ref, dq_ref, dk_ref, dgq_ref, dgk_ref):
        @pl.when(pl.program_id(0) == 0)
        def _():
            dgq_ref[...] = jnp.zeros_like(dgq_ref)
            dgk_ref[...] = jnp.zeros_like(dgk_ref)

        for h in range(FOX_H):
            sl = slice(h * HEAD, (h + 1) * HEAD)
            dq, dgq = _rms_grad(q_ref[:, sl], gq_ref[...], dqn_ref[:, sl])
            dk, dgk = _rms_grad(k_ref[:, sl], gk_ref[...], dkn_ref[:, sl])
            dq_ref[:, sl] = dq.astype(BF16)
            dk_ref[:, sl] = dk.astype(BF16)
            dgq_ref[...] += dgq
            dgk_ref[...] += dgk

    gain = pl.BlockSpec((1, HEAD), lambda i: (0, 0))
    row = pl.BlockSpec((tr, w), lambda i: (i, 0))
    return pl.pallas_call(
        body, name=name, grid=(rows // tr,),
        in_specs=[row, pl.BlockSpec((tr, w), lambda i: (i, 1)), gain, gain, row, row],
        out_specs=[row, row, gain, gain],
        out_shape=[jax.ShapeDtypeStruct((rows, w), BF16)] * 2 + [jax.ShapeDtypeStruct((1, HEAD), F32)] * 2,
        compiler_params=_params("arbitrary"),
    )(proj, proj, g_q, g_k, dqn, dkn)


def _fgate_fwd(f_logit, b_pad, name):
    S = f_logit.shape[0]
    T = ATT_T
    nb = S // T

    def body(f_ref, b_ref, col_ref, row_ref, c_scr):
        tri = (lax.broadcasted_iota(jnp.int32, (T, T), 0) >= lax.broadcasted_iota(jnp.int32, (T, T), 1)).astype(F32)
        carry = jnp.zeros((1, HEAD), F32)
        for blk in range(nb):
            z = f_ref[blk * T:(blk + 1) * T, :] + b_ref[...]
            lf = jnp.minimum(z, 0.0) - jnp.log1p(jnp.exp(-jnp.abs(z)))
            cb = jnp.dot(tri, lf, precision=lax.Precision.HIGHEST, preferred_element_type=F32) + carry
            c_scr[blk * T:(blk + 1) * T, :] = cb
            carry = cb[T - 1:T, :]
        c = c_scr[...]
        lane = lax.broadcasted_iota(jnp.int32, c.shape, 1)
        ct = c.T
        for h in range(FOX_H):
            col_ref[h] = jnp.sum(jnp.where(lane == h, c, 0.0), axis=1, keepdims=True)
            for jb in range(nb):
                row_ref[h, jb] = ct[h:h + 1, jb * T:(jb + 1) * T]

    return pl.pallas_call(
        body, name=name,
        out_shape=[jax.ShapeDtypeStruct((FOX_H, S, 1), F32), jax.ShapeDtypeStruct((FOX_H, nb, 1, T), F32)],
        scratch_shapes=[pltpu.VMEM((S, HEAD), F32)], compiler_params=_params(),
    )(f_logit, b_pad)


def _fgate_bwd(f_logit, b_pad, dc_col, dc_row, name):
    S = f_logit.shape[0]
    T = ATT_T
    nb = S // T

    def body(f_ref, b_ref, dcol_ref, drow_ref, df_ref, db_ref, dc_scr, rt_scr):
        lane = lax.broadcasted_iota(jnp.int32, (S, HEAD), 1)
        sub = lax.broadcasted_iota(jnp.int32, (HEAD, T), 0)
        dc = jnp.zeros((S, HEAD), F32)
        for h in range(FOX_H):
            dc = jnp.where(lane == h, dcol_ref[h], dc)
        for jb in range(nb):
            rt = jnp.zeros((HEAD, T), F32)
            for h in range(FOX_H):
                rt = jnp.where(sub == h, drow_ref[h, jb], rt)
            rt_scr[jb * T:(jb + 1) * T, :] = rt.T
        dc_scr[...] = dc - rt_scr[...]
        tri = (lax.broadcasted_iota(jnp.int32, (T, T), 0) <= lax.broadcasted_iota(jnp.int32, (T, T), 1)).astype(F32)
        carry = jnp.zeros((1, HEAD), F32)
        db = jnp.zeros((1, HEAD), F32)
        for blk in reversed(range(nb)):
            rows = slice(blk * T, (blk + 1) * T)
            dlf = jnp.dot(tri, dc_scr[rows, :], precision=lax.Precision.HIGHEST, preferred_element_type=F32) + carry
            carry = dlf[0:1, :]
            z = f_ref[rows, :] + b_ref[...]
            df = dlf * jax.nn.sigmoid(-z)
            df_ref[rows, :] = df.astype(BF16)
            db = db + jnp.sum(df, axis=0, keepdims=True)
        db_ref[...] = db

    return pl.pallas_call(
        body, name=name,
        out_shape=[jax.ShapeDtypeStruct((S, HEAD), BF16), jax.ShapeDtypeStruct((1, HEAD), F32)],
        scratch_shapes=[pltpu.VMEM((S, HEAD), F32), pltpu.VMEM((S, HEAD), F32)], compiler_params=_params(),
    )(f_logit, b_pad, dc_col, dc_row)


def _fox_fwd(qn, kn, vb, c_col, c_row, name):
    S = qn.shape[0]
    T = ATT_T
    nb = S // T
    scale = 1.0 / math.sqrt(HEAD)

    def body(q_ref, k_ref, v_ref, cc_ref, cr_ref, o_ref, lse_ref):
        i = pl.program_id(1)
        q = q_ref[...]
        cc = cc_ref[...]
        rows = i * T + lax.broadcasted_iota(jnp.int32, (T, T), 0)
        lanes = lax.broadcasted_iota(jnp.int32, (T, T), 1)

        def step(j, carry):
            m, l, acc = carry
            sl = pl.ds(pl.multiple_of(j * T, T), T)
            s = _dot(q, k_ref[sl, :], NT) * scale + cc - cr_ref[j]
            s = jnp.where(rows >= j * T + lanes, s, NEG)
            m2 = jnp.maximum(m, jnp.max(s, axis=1, keepdims=True))
            p = jnp.exp(s - m2)
            al = jnp.exp(m - m2)
            return m2, al * l + jnp.sum(p, axis=1, keepdims=True), al * acc + _dot(p.astype(BF16), v_ref[sl, :])

        init = (jnp.full((T, 1), NEG, F32), jnp.zeros((T, 1), F32), jnp.zeros((T, HEAD), F32))
        m, l, acc = lax.fori_loop(0, i + 1, step, init)
        o_ref[...] = acc / l
        lse_ref[...] = m + jnp.log(l)

    head = pl.BlockSpec((S, HEAD), lambda h, i: (0, h))
    return pl.pallas_call(
        body, name=name, grid=(FOX_H, nb),
        in_specs=[pl.BlockSpec((T, HEAD), lambda h, i: (i, h)), head, head,
                  pl.BlockSpec((None, T, 1), lambda h, i: (h, i, 0)),
                  pl.BlockSpec((None, nb, 1, T), lambda h, i: (h, 0, 0, 0))],
        out_specs=[pl.BlockSpec((T, HEAD), lambda h, i: (i, h)), pl.BlockSpec((None, T, 1), lambda h, i: (h, i, 0))],
        out_shape=[jax.ShapeDtypeStruct((S, FOX_H * HEAD), F32), jax.ShapeDtypeStruct((FOX_H, S, 1), F32)],
        compiler_params=_params("parallel", "parallel"),
    )(qn, kn, vb, c_col, c_row)


def _fox_bwd(qn, kn, vb, c_col, c_row, lse, o, do, name):
    S = qn.shape[0]
    T = ATT_T
    nb = S // T
    scale = 1.0 / math.sqrt(HEAD)

    def body(q_ref, k_ref, v_ref, cc_ref, cr_ref, lse_ref, o_ref, do_ref, dq_ref, dk_ref, dv_ref, dcc_ref, dcr_ref):
        j = pl.program_id(1)

        @pl.when(j == 0)
        def _():
            dq_ref[...] = jnp.zeros_like(dq_ref)
            dcc_ref[...] = jnp.zeros_like(dcc_ref)

        k = k_ref[...]
        v = v_ref[...]
        cr = cr_ref[...]
        cols = j * T + lax.broadcasted_iota(jnp.int32, (T, T), 1)
        subl = lax.broadcasted_iota(jnp.int32, (T, T), 0)

        def step(i, carry):
            dk, dv, dcr = carry
            sl = pl.ds(pl.multiple_of(i * T, T), T)
            q = q_ref[sl, :]
            d_o = do_ref[sl, :]
            s = _dot(q, k, NT) * scale + cc_ref[sl, :] - cr
            s = jnp.where(i * T + subl >= cols, s, NEG)
            p = jnp.exp(s - lse_ref[sl, :])
            dob = d_o.astype(BF16)
            dp = _dot(dob, v, NT)
            delta = jnp.sum(d_o * o_ref[sl, :], axis=1, keepdims=True)
            ds = p * (dp - delta)
            dsb = ds.astype(BF16)
            dq_ref[sl, :] += _dot(dsb, k) * scale
            dcc_ref[sl, :] += jnp.sum(ds, axis=1, keepdims=True)
            return (dk + _dot(dsb, q, TN), dv + _dot(p.astype(BF16), dob, TN), dcr + jnp.sum(ds, axis=0, keepdims=True))

        init = (jnp.zeros((T, HEAD), F32), jnp.zeros((T, HEAD), F32), jnp.zeros((1, T), F32))
        dk, dv, dcr = lax.fori_loop(j, nb, step, init)
        dk_ref[...] = dk * scale
        dv_ref[...] = dv.astype(BF16)
        dcr_ref[...] = dcr

    head = pl.BlockSpec((S, HEAD), lambda h, j: (0, h))
    tile = pl.BlockSpec((T, HEAD), lambda h, j: (j, h))
    col = pl.BlockSpec((None, S, 1), lambda h, j: (h, 0, 0))
    row = pl.BlockSpec((None, None, 1, T), lambda h, j: (h, j, 0, 0))
    w = FOX_H * HEAD
    return pl.pallas_call(
        body, name=name, grid=(FOX_H, nb),
        in_specs=[head, tile, tile, col, row, col, head, head],
        out_specs=[head, tile, tile, col, row],
        out_shape=[jax.ShapeDtypeStruct((S, w), F32), jax.ShapeDtypeStruct((S, w), F32), jax.ShapeDtypeStruct((S, w), BF16),
                   jax.ShapeDtypeStruct((FOX_H, S, 1), F32), jax.ShapeDtypeStruct((FOX_H, nb, 1, T), F32)],
        compiler_params=_params("arbitrary", "arbitrary"),
    )(qn, kn, vb, c_col, c_row, lse, o, do)


def _lru_gates(uc, wra, bra, wri, bri, lam):
    ucb = uc.astype(BF16)
    r = jax.nn.sigmoid(_dot(ucb, wra.astype(BF16)) + bra)
    ig = jax.nn.sigmoid(_dot(ucb, wri.astype(BF16)) + bri)
    sp = jnp.maximum(-lam, 0.0) + jnp.log1p(jnp.exp(-jnp.abs(lam)))
    log_a = -LRU_C * r * sp
    a = jnp.exp(log_a)
    mult = jnp.sqrt(_one_minus_exp(2.0 * log_a))
    return r, ig, sp, a, mult


def _conv(pad_ref, cw, cb, S):
    uc = cb
    for j in range(CONV_K):
        uc = uc + cw[j:j + 1, :] * pad_ref[5 + j:5 + j + S, :]
    return uc


def _lru_specs(S, n_proj_cols):
    u_col = 3 * FOX_H
    g_col = u_col + LRU_NB
    blk = pl.BlockSpec((S, HEAD), lambda n: (0, n))
    vec = pl.BlockSpec((1, HEAD), lambda n: (0, n))
    mat = pl.BlockSpec((None, HEAD, HEAD), lambda n: (n, 0, 0))
    return dict(
        u=pl.BlockSpec((S, HEAD), lambda n: (0, u_col + n)), gate=pl.BlockSpec((S, HEAD), lambda n: (0, g_col + n)),
        blk=blk, vec=vec, mat=mat, cw=pl.BlockSpec((CONV_K, HEAD), lambda n: (0, n)))


def _lru_fwd(proj, conv_w, conv_b, w_ra, b_ra, w_ri, b_ri, lam, name):
    S = proj.shape[0]
    sp_ = _lru_specs(S, proj.shape[1])
    rows8 = S // 8

    def body(u_ref, gt_ref, cw_ref, cb_ref, wra_ref, bra_ref, wri_ref, bri_ref, lam_ref, y_ref, h_ref, pad, a_scr, b_scr):
        pad[0:8, :] = jnp.zeros((8, HEAD), F32)
        pad[8:S + 8, :] = u_ref[...]
        uc = _conv(pad, cw_ref[...], cb_ref[...], S)
        r, ig, sp, a, mult = _lru_gates(uc, wra_ref[...], bra_ref[...], wri_ref[...], bri_ref[...], lam_ref[...])
        a_scr[...] = a
        b_scr[...] = mult * (ig * uc)
        sub = lax.broadcasted_iota(jnp.int32, (8, HEAD), 0)

        def step(t, carry):
            sl = pl.ds(pl.multiple_of(t * 8, 8), 8)
            A, B = a_scr[sl, :], b_scr[sl, :]
            for d in (1, 2, 4):
                m = sub >= d
                B = jnp.where(m, A * pltpu.roll(B, d, 0) + B, B)
                A = jnp.where(m, A * pltpu.roll(A, d, 0), A)
            h = A * carry + B
            h_ref[sl, :] = h
            return h[7:8, :]

        lax.fori_loop(0, rows8, step, jnp.zeros((1, HEAD), F32))
        y_ref[...] = h_ref[...] * _gelu(gt_ref[...])

    w = LRU_NB * HEAD
    return pl.pallas_call(
        body, name=name, grid=(LRU_NB,),
        in_specs=[sp_["u"], sp_["gate"], sp_["cw"], sp_["vec"], sp_["mat"], sp_["vec"], sp_["mat"], sp_["vec"], sp_["vec"]],
        out_specs=[sp_["blk"], sp_["blk"]],
        out_shape=[jax.ShapeDtypeStruct((S, w), F32)] * 2,
        scratch_shapes=[pltpu.VMEM((S + 8, HEAD), F32), pltpu.VMEM((S, HEAD), F32), pltpu.VMEM((S, HEAD), F32)],
        compiler_params=_params("parallel"),
    )(proj, proj, conv_w, conv_b, w_ra, b_ra, w_ri, b_ri, lam)


def _lru_bwd(proj, h, dy, conv_w, conv_b, w_ra, b_ra, w_ri, b_ri, lam, name):
    S = proj.shape[0]
    sp_ = _lru_specs(S, proj.shape[1])
    rows8 = S // 8

    def body(u_ref, gt_ref, h_ref, dy_ref, cw_ref, cb_ref, wra_ref, bra_ref, wri_ref, bri_ref, lam_ref,
             du_ref, dgt_ref, dcw_ref, dcb_ref, dwra_ref, dbra_ref, dwri_ref, dbri_ref, dlam_ref,
             pad, an_scr, d_scr, g_scr, hp_scr):
        zero8 = jnp.zeros((8, HEAD), F32)
        pad[0:8, :] = zero8
        pad[8:S + 8, :] = u_ref[...]
        cw = cw_ref[...]
        uc = _conv(pad, cw, cb_ref[...], S)
        wra, wri, lam_v = wra_ref[...], wri_ref[...], lam_ref[...]
        r, ig, sp, a, mult = _lru_gates(uc, wra, bra_ref[...], wri, bri_ref[...], lam_v)
        gate = gt_ref[...]
        dy_v = dy_ref[...]
        hv = h_ref[...]
        dgt_ref[...] = (dy_v * hv * _gelu_grad(gate)).astype(BF16)
        d_scr[...] = dy_v * _gelu(gate)
        g_scr[0:S, :] = a
        g_scr[S:S + 8, :] = zero8
        an_scr[...] = g_scr[1:S + 1, :]
        sub = lax.broadcasted_iota(jnp.int32, (8, HEAD), 0)

        def step(t, carry):
            sl = pl.ds(pl.multiple_of((rows8 - 1 - t) * 8, 8), 8)
            A, D = an_scr[sl, :], d_scr[sl, :]
            for d in (1, 2, 4):
                m = sub + d <= 7
                D = jnp.where(m, A * pltpu.roll(D, 8 - d, 0) + D, D)
                A = jnp.where(m, A * pltpu.roll(A, 8 - d, 0), A)
            g = A * carry + D
            g_scr[sl, :] = g
            return g[0:1, :]

        lax.fori_loop(0, rows8, step, jnp.zeros((1, HEAD), F32))
        g = g_scr[0:S, :]
        hp_scr[0:8, :] = zero8
        hp_scr[8:S + 8, :] = hv
        da = g * hp_scr[7:S + 7, :]
        iu = ig * uc
        dmult = g * iu
        diu = g * mult
        dig = diu * uc
        duc = diu * ig
        dlog_a = da * a - dmult * (a * a) / mult
        dr = dlog_a * (-LRU_C * sp)
        dsp = jnp.sum(dlog_a * (-LRU_C * r), axis=0, keepdims=True)
        dlam_ref[...] = -dsp * jax.nn.sigmoid(-lam_v)
        dpr = dr * r * (1.0 - r)
        dpi = dig * ig * (1.0 - ig)
        dbra_ref[...] = jnp.sum(dpr, axis=0, keepdims=True)
        dbri_ref[...] = jnp.sum(dpi, axis=0, keepdims=True)
        ucb = uc.astype(BF16)
        dprb, dpib = dpr.astype(BF16), dpi.astype(BF16)
        dwra_ref[...] = _dot(ucb, dprb, TN)
        dwri_ref[...] = _dot(ucb, dpib, TN)
        duc = duc + _dot(dprb, wra.astype(BF16), NT) + _dot(dpib, wri.astype(BF16), NT)
        dcb_ref[...] = jnp.sum(duc, axis=0, keepdims=True)
        for j in range(CONV_K):
            dcw_ref[j:j + 1, :] = jnp.sum(duc * pad[5 + j:5 + j + S, :], axis=0, keepdims=True)
        g_scr[0:S, :] = duc
        g_scr[S:S + 8, :] = zero8
        du = jnp.zeros((S, HEAD), F32)
        for j in range(CONV_K):
            du = du + cw[j:j + 1, :] * g_scr[3 - j:3 - j + S, :]
        du_ref[...] = du.astype(BF16)

    w = LRU_NB * HEAD
    bf = jax.ShapeDtypeStruct((S, w), BF16)
    vec = jax.ShapeDtypeStruct((1, w), F32)
    mat = jax.ShapeDtypeStruct((LRU_NB, HEAD, HEAD), F32)
    return pl.pallas_call(
        body, name=name, grid=(LRU_NB,),
        in_specs=[sp_["u"], sp_["gate"], sp_["blk"], sp_["blk"], sp_["cw"], sp_["vec"], sp_["mat"], sp_["vec"],
                  sp_["mat"], sp_["vec"], sp_["vec"]],
        out_specs=[sp_["blk"], sp_["blk"], sp_["cw"], sp_["vec"], sp_["mat"], sp_["vec"], sp_["mat"], sp_["vec"], sp_["vec"]],
        out_shape=[bf, bf, jax.ShapeDtypeStruct((CONV_K, w), F32), vec, mat, vec, mat, vec, vec],
        scratch_shapes=[pltpu.VMEM((S + 8, HEAD), F32), pltpu.VMEM((S, HEAD), F32), pltpu.VMEM((S, HEAD), F32),
                        pltpu.VMEM((S + 8, HEAD), F32), pltpu.VMEM((S + 8, HEAD), F32)],
        compiler_params=_params("parallel"),
    )(proj, proj, h, dy, conv_w, conv_b, w_ra, b_ra, w_ri, b_ri, lam)


def _xattn_fwd(cq, ckv, g_cq, g_ck, name):
    S, w = cq.shape
    M = ckv.shape[0]
    tr = _row_tile(S)
    scale = 1.0 / math.sqrt(HEAD)

    def body(cq_ref, ckv_ref, gq_ref, gk_ref, o_ref):
        for h in range(XATT_H):
            sl = slice(h * HEAD, (h + 1) * HEAD)
            qn = _rms(cq_ref[:, sl], gq_ref[...]).astype(BF16)
            kn = _rms(ckv_ref[:, sl], gk_ref[...]).astype(BF16)
            v = ckv_ref[:, w + h * HEAD:w + (h + 1) * HEAD].astype(BF16)
            s = _dot(qn, kn, NT) * scale
            p = jnp.exp(s - jnp.max(s, axis=1, keepdims=True))
            p = p / jnp.sum(p, axis=1, keepdims=True)
            o_ref[:, sl] = _dot(p.astype(BF16), v).astype(BF16)

    gain = pl.BlockSpec((1, HEAD), lambda i: (0, 0))
    return pl.pallas_call(
        body, name=name, grid=(S // tr,),
        in_specs=[pl.BlockSpec((tr, w), lambda i: (i, 0)), pl.BlockSpec((M, 2 * w), lambda i: (0, 0)), gain, gain],
        out_specs=pl.BlockSpec((tr, w), lambda i: (i, 0)), out_shape=jax.ShapeDtypeStruct((S, w), BF16),
        compiler_params=_params("parallel"),
    )(cq, ckv, g_cq, g_ck)


def _xattn_bwd(cq, ckv, g_cq, g_ck, do, name):
    S, w = cq.shape
    M = ckv.shape[0]
    tr = _row_tile(S)
    nsteps = S // tr
    scale = 1.0 / math.sqrt(HEAD)

    def body(cq_ref, ckv_ref, gq_ref, gk_ref, do_ref, dcq_ref, dckv_ref, dgq_ref, dgk_ref, dkn_scr, dv_scr):
        step = pl.program_id(0)

        @pl.when(step == 0)
        def _():
            dkn_scr[...] = jnp.zeros_like(dkn_scr)
            dv_scr[...] = jnp.zeros_like(dv_scr)
            dgq_ref[...] = jnp.zeros_like(dgq_ref)

        for h in range(XATT_H):
            sl = slice(h * HEAD, (h + 1) * HEAD)
            q_raw = cq_ref[:, sl]
            qn = _rms(q_raw, gq_ref[...]).astype(BF16)
            kn = _rms(ckv_ref[:, sl], gk_ref[...]).astype(BF16)
            v = ckv_ref[:, w + h * HEAD:w + (h + 1) * HEAD].astype(BF16)
            s = _dot(qn, kn, NT) * scale
            p = jnp.exp(s - jnp.max(s, axis=1, keepdims=True))
            p = p / jnp.sum(p, axis=1, keepdims=True)
            dob = do_ref[:, sl].astype(BF16)
            dp = _dot(dob, v, NT)
            ds = p * (dp - jnp.sum(p * dp, axis=1, keepdims=True)) * scale
            dsb = ds.astype(BF16)
            dv_scr[:, sl] += _dot(p.astype(BF16), dob, TN)
            dkn_scr[:, sl] += _dot(dsb, qn, TN)
            dq, dgq = _rms_grad(q_raw, gq_ref[...], _dot(dsb, kn))
            dcq_ref[:, sl] = dq.astype(BF16)
            dgq_ref[...] += dgq

        @pl.when(step == nsteps - 1)
        def _():
            dgk = jnp.zeros((1, HEAD), F32)
            for h in range(XATT_H):
                sl = slice(h * HEAD, (h + 1) * HEAD)
                dk, dgk_h = _rms_grad(ckv_ref[:, sl], gk_ref[...], dkn_scr[:, sl])
                dckv_ref[:, sl] = dk.astype(BF16)
                dgk = dgk + dgk_h
            dckv_ref[:, w:2 * w] = dv_scr[...].astype(BF16)
            dgk_ref[...] = dgk

    gain = pl.BlockSpec((1, HEAD), lambda i: (0, 0))
    row = pl.BlockSpec((tr, w), lambda i: (i, 0))
    mem = pl.BlockSpec((M, 2 * w), lambda i: (0, 0))
    return pl.pallas_call(
        body, name=name, grid=(nsteps,), in_specs=[row, mem, gain, gain, row], out_specs=[row, mem, gain, gain],
        out_shape=[jax.ShapeDtypeStruct((S, w), BF16), jax.ShapeDtypeStruct((M, 2 * w), BF16),
                   jax.ShapeDtypeStruct((1, HEAD), F32), jax.ShapeDtypeStruct((1, HEAD), F32)],
        scratch_shapes=[pltpu.VMEM((M, w), F32), pltpu.VMEM((M, w), F32)],
        compiler_params=_params("arbitrary"),
    )(cq, ckv, g_cq, g_ck, do)


def _swiglu_fwd(gu, name, n_ct=4):
    S, w2 = gu.shape
    f = w2 // 2
    tc = f // n_ct
    tr = _row_tile(S)

    def body(g_ref, u_ref, o_ref):
        g = g_ref[...]
        o_ref[...] = (g * jax.nn.sigmoid(g) * u_ref[...]).astype(BF16)

    return pl.pallas_call(
        body, name=name, grid=(S // tr, n_ct),
        in_specs=[pl.BlockSpec((tr, tc), lambda i, j: (i, j)), pl.BlockSpec((tr, tc), lambda i, j: (i, j + n_ct))],
        out_specs=pl.BlockSpec((tr, tc), lambda i, j: (i, j)), out_shape=jax.ShapeDtypeStruct((S, f), BF16),
        compiler_params=_params("parallel", "parallel"),
    )(gu, gu)


def _swiglu_bwd(gu, dact, name, n_ct=4):
    S, w2 = gu.shape
    f = w2 // 2
    tc = f // n_ct
    tr = _row_tile(S)

    def body(g_ref, u_ref, da_ref, o_ref):
        g = g_ref[...]
        sg = jax.nn.sigmoid(g)
        da = da_ref[...]
        half = pl.program_id(1) // n_ct

        @pl.when(half == 0)
        def _():
            o_ref[...] = (da * u_ref[...] * (sg * (1.0 + g * (1.0 - sg)))).astype(BF16)

        @pl.when(half == 1)
        def _():
            o_ref[...] = (da * (g * sg)).astype(BF16)

    return pl.pallas_call(
        body, name=name, grid=(S // tr, 2 * n_ct),
        in_specs=[pl.BlockSpec((tr, tc), lambda i, j: (i, j % n_ct)),
                  pl.BlockSpec((tr, tc), lambda i, j: (i, j % n_ct + n_ct)),
                  pl.BlockSpec((tr, tc), lambda i, j: (i, j % n_ct))],
        out_specs=pl.BlockSpec((tr, tc), lambda i, j: (i, j)), out_shape=jax.ShapeDtypeStruct((S, w2), BF16),
        compiler_params=_params("parallel", "parallel"),
    )(gu, gu, dact)


def _adamw_math(w, g, m, v):
    m = ADAM_B1 * m + (1.0 - ADAM_B1) * g
    v = ADAM_B2 * v + (1.0 - ADAM_B2) * (g * g)
    m_hat = m / (1.0 - ADAM_B1 ** ADAM_STEP)
    v_hat = v / (1.0 - ADAM_B2 ** ADAM_STEP)
    delta = -ADAM_LR * (m_hat / (jnp.sqrt(v_hat) + ADAM_EPS) + ADAM_WD * w)
    return delta, m, v


def _tile2(R, C, elems):
    if R % 16 == 0:
        return _row_tile(R, cap=max(16, elems // C // 16 * 16)), C
    tc = 128
    while C % (2 * tc) == 0 and R * 2 * tc <= elems:
        tc *= 2
    return R, tc


def _reduce_adamw(own, land, chip, w, m, v, name):
    _, R, C = own.shape
    tr, tc = _tile2(R, C, 1 << 18)

    def body(s_ref, o_ref, l1_ref, l2_ref, l3_ref, w_ref, m_ref, v_ref, g_ref, d_ref, nm_ref, nv_ref):
        g = o_ref[...].astype(F32) + l1_ref[...].astype(F32) + l2_ref[...].astype(F32) + l3_ref[...].astype(F32)
        g_ref[...] = g
        d_ref[...], nm_ref[...], nv_ref[...] = _adamw_math(w_ref[...], g, m_ref[...], v_ref[...])

    def part(flip):
        return pl.BlockSpec((None, tr, tc), lambda i, j, s: (s[0] ^ flip, i, j))

    blk = pl.BlockSpec((tr, tc), lambda i, j, s: (i, j))
    sds = jax.ShapeDtypeStruct((R, C), F32)
    return pl.pallas_call(
        body, name=name,
        grid_spec=pltpu.PrefetchScalarGridSpec(
            num_scalar_prefetch=1, grid=(R // tr, C // tc),
            in_specs=[part(0), part(1), part(2), part(3), blk, blk, blk], out_specs=[blk] * 4),
        out_shape=[sds] * 4, compiler_params=_params("parallel", "parallel"),
    )(chip, own, land, land, land, w, m, v)


def _pair_sum(g8, recv, core, name):
    _, R, C = g8.shape
    tr, tc = _tile2(R, C, 1 << 19)

    def body(c_ref, a_ref, b_ref, o_ref):
        o_ref[...] = (a_ref[...].astype(F32) + b_ref[...].astype(F32)).astype(BF16)

    return pl.pallas_call(
        body, name=name,
        grid_spec=pltpu.PrefetchScalarGridSpec(
            num_scalar_prefetch=1, grid=(N_CHIP, R // tr, C // tc),
            in_specs=[pl.BlockSpec((None, tr, tc), lambda q, i, j, c: (2 * q + c[0], i, j)),
                      pl.BlockSpec((None, tr, tc), lambda q, i, j, c: (q, i, j))],
            out_specs=pl.BlockSpec((None, tr, tc), lambda q, i, j, c: (q, i, j))),
        out_shape=jax.ShapeDtypeStruct((N_CHIP, R, C), BF16),
        compiler_params=_params("parallel", "parallel", "parallel"),
    )(core, g8, recv)


ANY = pl.BlockSpec(memory_space=pl.ANY)


def _place():
    return lax.axis_index("x"), lax.axis_index("y"), lax.axis_index("c")


def _all_gather(shards, name):
    n = len(shards)

    def body(*refs):
        ins, outs = refs[:n], refs[n:2 * n]
        send_sems, recv_sems, local_sems = refs[2 * n:]
        x, y, c = _place()
        me, sibling = (x, y, c), (x, y, 1 - c)
        chips = [(1 - x, y), (x, 1 - y), (1 - x, 1 - y)]

        def copy(a, k, block, to, src=None):
            dst = outs[a].at[4 * block[0] + 2 * block[1] + block[2]]
            return pltpu.make_async_remote_copy(
                src_ref=dst if src is None else src, dst_ref=dst, send_sem=send_sems.at[a, k],
                recv_sem=recv_sems.at[a, k], device_id=to, device_id_type=MESH)

        mine = [pltpu.make_async_copy(ins[a], outs[a].at[4 * x + 2 * y + c], local_sems.at[a]) for a in range(n)]
        for cp in mine:
            cp.start()
        sent = []
        for a in range(n):
            sent.append(copy(a, 0, me, sibling, src=ins[a]))
            sent += [copy(a, 1 + j, me, (*chip, c), src=ins[a]) for j, chip in enumerate(chips)]
        for cp in sent:
            cp.start()
        for j, chip in enumerate(chips):
            for a in range(n):
                copy(a, 1 + j, (*chip, c), me).wait_recv()
                fwd = copy(a, 4 + j, (*chip, c), sibling)
                fwd.start()
                sent.append(fwd)
        for a in range(n):
            copy(a, 0, sibling, me).wait_recv()
        for j, chip in enumerate(chips):
            for a in range(n):
                copy(a, 4 + j, (*chip, 1 - c), me).wait_recv()
        for cp in sent:
            cp.wait_send()
        for cp in mine:
            cp.wait()

    return pl.pallas_call(
        body, name=name, in_specs=[ANY] * n, out_specs=[ANY] * n,
        out_shape=[jax.ShapeDtypeStruct((N_DEV,) + s.shape, s.dtype) for s in shards],
        scratch_shapes=[pltpu.SemaphoreType.DMA((n, 7)), pltpu.SemaphoreType.DMA((n, 7)), pltpu.SemaphoreType.DMA((n,))],
    )(*shards)


def _pair_exchange(g8s, name):
    n = len(g8s)

    def body(*refs):
        ins, outs = refs[:n], refs[n:2 * n]
        send_sems, recv_sems = refs[2 * n:]
        x, y, c = _place()
        copies = []
        for a in range(n):
            for q in range(N_CHIP):
                copies.append(pltpu.make_async_remote_copy(
                    src_ref=ins[a].at[2 * q + 1 - c], dst_ref=outs[a].at[q], send_sem=send_sems.at[a, q],
                    recv_sem=recv_sems.at[a, q], device_id=(x, y, 1 - c), device_id_type=MESH))
        for cp in copies:
            cp.start()
        for cp in copies:
            cp.wait()

    return pl.pallas_call(
        body, name=name, in_specs=[ANY] * n, out_specs=[ANY] * n,
        out_shape=[jax.ShapeDtypeStruct((N_CHIP,) + g.shape[1:], g.dtype) for g in g8s],
        scratch_shapes=[pltpu.SemaphoreType.DMA((n, N_CHIP)), pltpu.SemaphoreType.DMA((n, N_CHIP))],
    )(*g8s)


def _chip_plan(srcs, lands):
    x, y, c = _place()
    mine = 2 * x + y
    plan = []
    for a in range(len(srcs)):
        for px, py in [(1 - x, y), (x, 1 - y), (1 - x, 1 - y)]:
            peer = 2 * px + py
            plan.append((srcs[a].at[peer], lands[a].at[mine], lands[a].at[peer], (px, py, c)))
    return plan


def _gather_plan(srcs, lands):
    x, y, c = _place()
    mine = 4 * x + 2 * y + c
    plan = []
    for a in range(len(srcs)):
        for px, py, pc in [(x, y, 1 - c), (1 - x, y, c), (x, 1 - y, c), (1 - x, 1 - y, c)]:
            plan.append((srcs[a], lands[a].at[mine], lands[a].at[4 * px + 2 * py + pc], (px, py, pc)))
    return plan


def _remote(src, dst, send_sem, recv_sem, peer):
    return pltpu.make_async_remote_copy(src_ref=src, dst_ref=dst, send_sem=send_sem, recv_sem=recv_sem,
                                        device_id=peer, device_id_type=MESH)


def _chip_exchange(p4s, name):
    n = len(p4s)

    def body(*refs):
        send_sems, recv_sems = refs[2 * n:]
        plan = _chip_plan(refs[:n], refs[n:2 * n])
        for k, (src, dst, _, peer) in enumerate(plan):
            _remote(src, dst, send_sems.at[k], recv_sems.at[k], peer).start()
        for k, (src, dst, mine, peer) in enumerate(plan):
            _remote(src, mine, send_sems.at[k], recv_sems.at[k], peer).wait_recv()
        for k, (src, dst, _, peer) in enumerate(plan):
            _remote(src, dst, send_sems.at[k], recv_sems.at[k], peer).wait_send()

    return pl.pallas_call(
        body, name=name, in_specs=[ANY] * n, out_specs=[ANY] * n,
        out_shape=[jax.ShapeDtypeStruct(p.shape, p.dtype) for p in p4s],
        scratch_shapes=[pltpu.SemaphoreType.DMA((3 * n,)), pltpu.SemaphoreType.DMA((3 * n,))],
    )(*p4s)


HBM = pl.BlockSpec(memory_space=pltpu.HBM)
SEMS = pl.BlockSpec(memory_space=pltpu.SEMAPHORE)
DATAFLOW = pltpu.SideEffectType.DATAFLOW_SIDE_EFFECTING


def _split_start(srcs, lands, after, plan_fn, per_array, name):
    n = len(srcs)
    ncopy = per_array * n

    def body(*refs):
        send_sems, recv_sems = refs[2 * n + 1], refs[2 * n + 2]
        token = refs[-1]
        for k, (src, dst, _, peer) in enumerate(plan_fn(refs[:n], refs[n:2 * n])):
            _remote(src, dst, send_sems.at[k], recv_sems.at[k], peer).start()
        token[...] = jnp.zeros_like(token)

    thru = [pltpu.HBM(a.shape, a.dtype) for a in (*srcs, *lands)]
    out = pl.pallas_call(
        body, name=name,
        out_shape=(pltpu.SemaphoreType.DMA((ncopy,)), pltpu.SemaphoreType.DMA((ncopy,)), *thru,
                   jax.ShapeDtypeStruct((8, 128), F32)),
        in_specs=[HBM] * (2 * n) + [ANY], out_specs=(SEMS, SEMS, *[HBM] * (2 * n), pl.BlockSpec(memory_space=pltpu.VMEM)),
        input_output_aliases={i: 2 + i for i in range(2 * n)},
        compiler_params=pltpu.CompilerParams(has_side_effects=DATAFLOW),
    )(*[pltpu.with_memory_space_constraint(a, pltpu.HBM) for a in (*srcs, *lands)], after)
    return out[0], out[1], list(out[2:2 + n]), list(out[2 + n:2 + 2 * n]), out[-1]


def _split_wait(send_sems, recv_sems, srcs, lands, after, plan_fn, name):
    n = len(srcs)

    def body(*refs):
        send_ref, recv_ref = refs[2 * n], refs[2 * n + 1]
        for k, (src, dst, mine, peer) in enumerate(plan_fn(refs[:n], refs[n:2 * n])):
            _remote(src, dst, send_ref.at[k], recv_ref.at[k], peer).wait_send()
            _remote(src, mine, send_ref.at[k], recv_ref.at[k], peer).wait_recv()

    thru = [pltpu.HBM(a.shape, a.dtype) for a in (*srcs, *lands)]
    out = pl.pallas_call(
        body, name=name, out_shape=tuple(thru),
        in_specs=[HBM] * (2 * n) + [SEMS, SEMS, ANY], out_specs=tuple([HBM] * (2 * n)),
        input_output_aliases={i: i for i in range(2 * n)},
        compiler_params=pltpu.CompilerParams(has_side_effects=DATAFLOW),
    )(*srcs, *lands, send_sems, recv_sems, after)
    return list(out[:n]), list(out[n:])


def _gather_forward(shards, lands, name):
    n = len(shards)

    def body(*refs):
        ins, landed, out = refs[:n], refs[n:2 * n], refs[2 * n:3 * n]
        send_sems, recv_sems, local_sems = refs[3 * n:]
        x, y, c = _place()
        chips = [(1 - x, y), (x, 1 - y), (1 - x, 1 - y)]
        mine = [pltpu.make_async_copy(ins[a], out[a].at[4 * x + 2 * y + c], local_sems.at[a]) for a in range(n)]
        for cp in mine:
            cp.start()
        sent = []
        for a in range(n):
            for j, (px, py) in enumerate(chips):
                blk = 4 * px + 2 * py + c
                sent.append(_remote(landed[a].at[blk], out[a].at[blk], send_sems.at[a, j], recv_sems.at[a, j], (x, y, 1 - c)))
        for cp in sent:
            cp.start()
        for a in range(n):
            for j, (px, py) in enumerate(chips):
                blk = 4 * px + 2 * py + 1 - c
                _remote(landed[a].at[blk], out[a].at[blk], send_sems.at[a, j], recv_sems.at[a, j], (x, y, 1 - c)).wait_recv()
        for cp in sent:
            cp.wait_send()
        for cp in mine:
            cp.wait()

    return pl.pallas_call(
        body, name=name, in_specs=[ANY] * (2 * n), out_specs=[ANY] * n,
        out_shape=[jax.ShapeDtypeStruct(l.shape, l.dtype) for l in lands],
        input_output_aliases={n + a: a for a in range(n)},
        scratch_shapes=[pltpu.SemaphoreType.DMA((n, 3)), pltpu.SemaphoreType.DMA((n, 3)), pltpu.SemaphoreType.DMA((n,))],
    )(*shards, *lands)


def _pack(arrs):
    parts = []
    for a in arrs:
        flat = a.reshape(-1).astype(F32)
        padded = -(-flat.shape[0] // 1024) * 1024
        parts.append(jnp.pad(flat, (0, padded - flat.shape[0])).reshape(padded // 128, 128))
    return jnp.concatenate(parts, axis=0)


def _unpack(packed, like):
    out, row = [], 0
    for a in like:
        size = math.prod(a.shape)
        rows = -(-size // 1024) * 8
        out.append(packed[row:row + rows].reshape(-1)[:size].reshape(a.shape))
        row += rows
    return out


def _sum_devices(parts, name):
    P, R, C = parts.shape
    tr = _row_tile(R, cap=512, mult=8)

    def body(p_ref, o_ref):
        g = p_ref[0]
        for q in range(1, P):
            g = g + p_ref[q]
        o_ref[...] = g

    return pl.pallas_call(
        body, name=name, grid=(R // tr,), in_specs=[pl.BlockSpec((P, tr, C), lambda i: (0, i, 0))],
        out_specs=pl.BlockSpec((tr, C), lambda i: (i, 0)), out_shape=jax.ShapeDtypeStruct((R, C), F32),
        compiler_params=_params("parallel"),
    )(parts)


def _adamw_packed(w, g, m, v, name):
    R, C = w.shape
    tr = _row_tile(R, cap=512, mult=8)

    def body(w_ref, g_ref, m_ref, v_ref, d_ref, nm_ref, nv_ref):
        d_ref[...], nm_ref[...], nv_ref[...] = _adamw_math(w_ref[...], g_ref[...], m_ref[...], v_ref[...])

    blk = pl.BlockSpec((tr, C), lambda i: (i, 0))
    sds = jax.ShapeDtypeStruct((R, C), F32)
    return pl.pallas_call(
        body, name=name, grid=(R // tr,), in_specs=[blk] * 4, out_specs=[blk] * 3, out_shape=[sds] * 3,
        compiler_params=_params("parallel"),
    )(w, g, m, v)


def _local_step(x, mem, target, sp, W, ffn_weights, ffn_grads_ready):
    S, D = x.shape
    h0 = _rms_fwd([x], [sp["g_mix"]], "rms_mix")
    proj = _mm(h0, W["main_t"], "nt", F32, "proj_main", tm=2048, tn=512)
    f_logit = _mm(h0, W["f_t"], "nt", F32, "proj_f", tm=2048)
    qn, kn, vb = _qk_fwd(proj, sp["g_q"], sp["g_k"], "qk_norm")
    b_pad = jnp.pad(sp["b_f"], ((0, 0), (0, HEAD - FOX_H)))
    c_col, c_row = _fgate_fwd(f_logit, b_pad, "forget_cumsum")
    o_fox, lse = _fox_fwd(qn, kn, vb, c_col, c_row, "fox_fwd")
    lru_p = (sp["conv_w"], sp["conv_b"], sp["w_ra"], sp["b_ra"], sp["w_ri"], sp["b_ri"], sp["lam"])
    y_lru, h_lru = _lru_fwd(proj, *lru_p, "lru_fwd")
    mix = _rms_fwd([o_fox, y_lru], [sp["g_fox_out"], sp["g_lru_out"]], "rms_mix_out")
    x1 = _mm(mix, W["out"], "nn", F32, "out_proj", add=x)
    hq = _rms_fwd([x1], [sp["g_xattn"]], "rms_xattn")
    mn = _rms_fwd([mem], [sp["g_mem"]], "rms_mem")
    cq = _mm(hq, W["cq"], "nn", F32, "xattn_q", tm=2048)
    ckv = _mm(mn, W["ckv"], "nn", F32, "xattn_kv")
    o_x = _xattn_fwd(cq, ckv, sp["g_cq"], sp["g_ck"], "xattn_fwd")
    x2 = _mm(o_x, W["co"], "nn", F32, "xattn_out", add=x1, tm=2048, tn=W["co"].shape[2])
    hf = _rms_fwd([x2], [sp["g_ffn"]], "rms_ffn")
    w_gu, w_down = ffn_weights(hf)
    gu_w = w_gu.shape[2]
    gu = _mm(hf, w_gu, "nn", F32, "ffn_up", tn=gu_w)
    act = _swiglu_fwd(gu, "swiglu_fwd")
    x3 = _mm(act, w_down, "nn", F32, "ffn_down", add=x2, tk=w_down.shape[0] // 2)
    dy, dyb, sq = _loss_head(x3, target, "loss_head")
    G = {}
    dact = _mm(dyb, w_down, "nt", F32, "ffn_down_dx", tm=2048, tn=512)
    g_down = _mm(act, dyb, "tn", BF16, "ffn_down_dw", tm=w_down.shape[0] // 4, tn=2048)
    dgu = _swiglu_bwd(gu, dact, "swiglu_bwd")
    dhf = _mm(dgu, w_gu, "nt", F32, "ffn_up_dx", tn=2048, tk=gu_w)
    g_gu = _mm(hf, dgu, "tn", BF16, "ffn_up_dw", out_blocks=N_DEV, tn=gu_w)
    hook = ffn_grads_ready(g_gu, g_down)
    dx2, dx2b, dg_ffn = _rms_bwd(x2, sp["g_ffn"] + hook, dhf, "rms_ffn_bwd", res=dy)
    d_ox = _mm(dx2b, W["co"], "nt", F32, "xattn_out_dx", tm=2048, tk=W["co"].shape[2])
    G["co"] = _mm(o_x, dx2b, "tn", BF16, "xattn_out_dw", out_blocks=N_DEV, tn=W["co"].shape[2])
    dcq, dckv, dg_cq, dg_ck = _xattn_bwd(cq, ckv, sp["g_cq"], sp["g_ck"], d_ox, "xattn_bwd")
    dhq = _mm(dcq, W["cq"], "nt", F32, "xattn_q_dx", tm=2048)
    G["cq"] = _mm(hq, dcq, "tn", BF16, "xattn_q_dw")
    dmn = _mm(dckv, W["ckv"], "nt", F32, "xattn_kv_dx")
    G["ckv"] = _mm(mn, dckv, "tn", BF16, "xattn_kv_dw")
    (dg_mem,) = _rms_bwd(mem, sp["g_mem"], dmn, "rms_mem_bwd", want_dx=False)
    dx1, dx1b, dg_xattn = _rms_bwd(x1, sp["g_xattn"], dhq, "rms_xattn_bwd", res=dx2)
    dmix = _mm(dx1b, W["out"], "nt", F32, "out_proj_dx")
    G["out"] = _mm(mix, dx1b, "tn", BF16, "out_proj_dw", tn=2048)
    do_fox, _, dg_fox_out = _rms_bwd(o_fox, sp["g_fox_out"], dmix, "rms_fox_out_bwd", dy_col=0)
    dy_lru, _, dg_lru_out = _rms_bwd(y_lru, sp["g_lru_out"], dmix, "rms_lru_out_bwd", dy_col=1)
    du, dgate, dconv_w, dconv_b, dw_ra, db_ra, dw_ri, db_ri, dlam = _lru_bwd(proj, h_lru, dy_lru, *lru_p, "lru_bwd")
    dqn, dkn, dv, dc_col, dc_row = _fox_bwd(qn, kn, vb, c_col, c_row, lse, o_fox, do_fox, "fox_bwd")
    dq, dk, dg_q, dg_k = _qk_bwd(proj, sp["g_q"], sp["g_k"], dqn, dkn, "qk_norm_bwd")
    df, db_f = _fgate_bwd(f_logit, b_pad, dc_col, dc_row, "forget_cumsum_bwd")
    dproj = jnp.concatenate([dq, dk, dv, du, dgate], axis=1)
    dh_f = _mm(df, W["f_t"], "nn", F32, "proj_f_dx", tm=2048)
    dh0 = _mm(dproj, W["main_t"], "nn", F32, "proj_main_dx", add=dh_f, tk=dproj.shape[1] // 2)
    G["main_t"] = _mm(dproj, h0, "tn", BF16, "proj_main_dw", tn=2048)
    G["f_t"] = _mm(df, h0, "tn", BF16, "proj_f_dw", tn=2048)
    grad_x, _, dg_mix = _rms_bwd(x, sp["g_mix"], dh0, "rms_mix_bwd", res=dx1)
    small = dict(g_mix=dg_mix, b_f=db_f[:, :FOX_H], g_q=dg_q, g_k=dg_k, conv_w=dconv_w, conv_b=dconv_b, w_ra=dw_ra,
                 b_ra=db_ra, w_ri=dw_ri, b_ri=db_ri, lam=dlam, g_fox_out=dg_fox_out, g_lru_out=dg_lru_out,
                 g_xattn=dg_xattn, g_mem=dg_mem, g_cq=dg_cq, g_ck=dg_ck, g_ffn=dg_ffn)
    return sq, grad_x, G, small


BIG = ("w_in", "w_out", "w_cq", "w_ckv", "w_co", "w_gate_up", "w_down")
SMALL = ("g_mix", "b_f", "g_q", "g_k", "conv_b", "w_ra", "b_ra", "w_ri", "b_ri", "lam", "g_fox_out", "g_lru_out",
         "g_xattn", "g_mem", "g_cq", "g_ck", "g_ffn")
ORDER = ("g_mix", "w_in", "b_f", "g_q", "g_k", "conv_w", "conv_b", "w_ra", "b_ra", "w_ri", "b_ri", "lam", "g_fox_out",
         "g_lru_out", "w_out", "g_xattn", "g_mem", "w_cq", "w_ckv", "g_cq", "g_ck", "w_co", "g_ffn", "w_gate_up", "w_down")


def kernel(x, mem, g_mix, w_in, b_f, g_q, g_k, conv_w, conv_b, w_ra, b_ra, w_ri, b_ri, lam, g_fox_out, g_lru_out, w_out, g_xattn, g_mem, w_cq, w_ckv, g_cq, g_ck, w_co, g_ffn, w_gate_up, w_down, loss_target, m_g_mix, m_w_in, m_b_f, m_g_q, m_g_k, m_conv_w, m_conv_b, m_w_ra, m_b_ra, m_w_ri, m_b_ri, m_lam, m_g_fox_out, m_g_lru_out, m_w_out, m_g_xattn, m_g_mem, m_w_cq, m_w_ckv, m_g_cq, m_g_ck, m_w_co, m_g_ffn, m_w_gate_up, m_w_down, v_g_mix, v_w_in, v_b_f, v_g_q, v_g_k, v_conv_w, v_conv_b, v_w_ra, v_b_ra, v_w_ri, v_b_ri, v_lam, v_g_fox_out, v_g_lru_out, v_w_out, v_g_xattn, v_g_mem, v_w_cq, v_w_ckv, v_g_cq, v_g_ck, v_w_co, v_g_ffn, v_w_gate_up, v_w_down):
    given = dict(locals())
    w = {n: given[n] for n in ORDER}
    m = {n: given["m_" + n] for n in ORDER}
    v = {n: given["v_" + n] for n in ORDER}
    D = x.shape[2]
    fw = FOX_H * HEAD
    dev = 4 * lax.axis_index("x") + 2 * lax.axis_index("y") + lax.axis_index("c")
    core = jnp.reshape(lax.axis_index("c"), (1,)).astype(jnp.int32)
    chip = jnp.reshape(2 * lax.axis_index("x") + lax.axis_index("y"), (1,)).astype(jnp.int32)

    def shard(d, n):
        return jnp.swapaxes(d[n][0], 0, 1) if n == "w_in" else d[n][0]

    def unshard(a, n):
        return (jnp.swapaxes(a, 0, 1) if n == "w_in" else a)[None]

    first = ("w_in", "w_out", "w_cq", "w_ckv", "w_co")
    ffn = ("w_gate_up", "w_down")
    g_in, g_out, g_cq_w, g_ckv, g_co, g_conv = _all_gather(
        [shard(w, n).astype(BF16) for n in first] + [conv_w[0]], "gather_weights")
    ffn_shards = [shard(w, n).astype(BF16) for n in ffn]
    ffn_lands = [lax.empty((N_DEV,) + s.shape, s.dtype) for s in ffn_shards]
    gs_send, gs_recv, ffn_shards, ffn_lands, gs_token = _split_start(
        ffn_shards, ffn_lands, g_conv, _gather_plan, 4, "gather_ffn_start")

    def ffn_weights(after):
        shards, lands = _split_wait(gs_send, gs_recv, ffn_shards, ffn_lands, after, _gather_plan, "gather_ffn_wait")
        g_gu, g_down = _gather_forward(shards, lands, "gather_ffn_forward")
        return g_gu, g_down.reshape(-1, g_down.shape[2])

    wt_in = g_in.reshape(-1, D)
    W = dict(
        main_t=jnp.concatenate([wt_in[:3 * fw], wt_in[3 * fw + FOX_H:]], axis=0),
        f_t=jnp.pad(wt_in[3 * fw:3 * fw + FOX_H], ((0, HEAD - FOX_H), (0, 0))),
        out=g_out.reshape(-1, g_out.shape[2]), cq=g_cq_w.reshape(-1, g_cq_w.shape[2]),
        ckv=g_ckv.reshape(-1, g_ckv.shape[2]), co=g_co)
    sp = {n: w[n] for n in SMALL if n not in ("w_ra", "w_ri")}
    sp["w_ra"], sp["w_ri"] = w_ra[0], w_ri[0]
    sp["conv_w"] = jnp.transpose(g_conv, (1, 0, 2)).reshape(CONV_K, -1)
    sp["g_mix"] = sp["g_mix"] + gs_token[0, 0]

    def reduce_begin(names, g8, tag):
        from_sibling = _pair_exchange(g8, "reduce_pair_exchange_" + tag)
        return [_pair_sum(g, r, core, "reduce_pair_sum_" + n) for g, r, n in zip(g8, from_sibling, names)]

    def reduce_end(names, p4, lands):
        return {n: tuple(unshard(r, n) for r in _reduce_adamw(p, l, chip, shard(w, n), shard(m, n), shard(v, n), "adamw_" + n))
                for n, p, l in zip(names, p4, lands)}

    pending = {}

    def ffn_grads_ready(g_gu, g_down):
        p4 = reduce_begin(ffn, [g_gu, g_down.reshape(N_DEV, -1, g_down.shape[1])], "ffn")
        lands = [lax.empty(p.shape, p.dtype) for p in p4]
        pending["ffn"] = _split_start(p4, lands, chip, _chip_plan, 3, "reduce_ffn_start")
        return pending["ffn"][4][0, 0]

    sq, grad_x, G, gs = _local_step(x[0], mem[0], loss_target[0], sp, W, ffn_weights, ffn_grads_ready)
    loss = lax.psum(0.5 * sq[0, 0] / D, AXES)

    dwt_in = jnp.concatenate([G["main_t"][:3 * fw], G["f_t"][:FOX_H], G["main_t"][3 * fw:]], axis=0)
    g8 = [dwt_in.reshape(N_DEV, -1, D), G["out"].reshape(N_DEV, -1, G["out"].shape[1]),
          G["cq"].reshape(N_DEV, -1, G["cq"].shape[1]), G["ckv"].reshape(N_DEV, -1, G["ckv"].shape[1]), G["co"]]
    p4 = reduce_begin(first, g8, "rest")
    out = reduce_end(first, p4, _chip_exchange(p4, "reduce_chip_exchange"))
    rs_send, rs_recv, ffn_p4, ffn_lands4, _ = pending["ffn"]
    out.update(reduce_end(ffn, *_split_wait(rs_send, rs_recv, ffn_p4, ffn_lands4, grad_x, _chip_plan, "reduce_ffn_wait")))

    small_names = SMALL + ("conv_w",)
    gathered = _all_gather([_pack([gs[n] for n in small_names])], "gather_small_grads")[0]
    g_sum = _unpack(_sum_devices(gathered, "sum_small_grads"), [gs[n] for n in small_names])
    g_small = dict(zip(small_names, g_sum))
    cw = conv_w.shape[2]
    g_small["conv_w"] = lax.dynamic_slice_in_dim(g_small["conv_w"], dev * cw, cw, axis=1)
    g_small = {n: g_small[n].reshape(w[n].shape) for n in small_names}
    packed = [_pack([d[n] for n in small_names]) for d in (w, g_small, m, v)]
    upd = _adamw_packed(*packed, "adamw_small")
    like = [w[n] for n in small_names]
    d_s, m_s, v_s = (dict(zip(small_names, _unpack(u, like))) for u in upd)
    for n in small_names:
        out[n] = (g_small[n], d_s[n], m_s[n], v_s[n])

    return (loss, grad_x[None], *[out[n][0] for n in ORDER], *[out[n][1] for n in ORDER],
            *[out[n][2] for n in ORDER], *[out[n][3] for n in ORDER])
```

```python
import functools
import math

import jax
import jax.numpy as jnp
from jax import lax
from jax.experimental import pallas as pl
from jax.experimental.pallas import tpu as pltpu

F32 = jnp.float32
BF16 = jnp.bfloat16
MESH = pl.DeviceIdType.MESH
AXES = ("x", "y", "c")
N_DEV = 8
N_CHIP = 4

HEAD = 128
FOX_H = 8
XATT_H = 4
LRU_NB = 8
CONV_K = 4
LRU_C = 8.0
RMS_EPS = 1e-6
ATT_T = 256
ROW_T = 256
NEG = -1e30
V7X_VMEM_LIMIT = 48 * 1024 * 1024

ADAM_LR = 0.001
ADAM_B1 = 0.9
ADAM_B2 = 0.999
ADAM_EPS = 1e-08
ADAM_WD = 0.01
ADAM_STEP = 10

NT = (((1,), (1,)), ((), ()))
TN = (((0,), (0,)), ((), ()))
NN = (((1,), (0,)), ((), ()))

ANY = pl.BlockSpec(memory_space=pl.ANY)


def _behind(after):
    return ([], []) if after is None else ([ANY], [after])


def _params(*sem):
    return pltpu.CompilerParams(dimension_semantics=sem or None, vmem_limit_bytes=V7X_VMEM_LIMIT)


def _dot(a, b, dn=NN):
    return lax.dot_general(a, b, dn, preferred_element_type=F32)


def _rms(x, g):
    r = lax.rsqrt(jnp.mean(x * x, axis=-1, keepdims=True) + RMS_EPS)
    return x * r * g


def _rms_grad(x, g, dy):
    r = lax.rsqrt(jnp.mean(x * x, axis=-1, keepdims=True) + RMS_EPS)
    xh = x * r
    dxh = dy * g
    dx = r * (dxh - xh * jnp.mean(dxh * xh, axis=-1, keepdims=True))
    return dx, jnp.sum(dy * xh, axis=0, keepdims=True)


def _gelu(x):
    k = math.sqrt(2.0 / math.pi)
    return 0.5 * x * (1.0 + jnp.tanh(k * (x + 0.044715 * (x * x * x))))


def _gelu_grad(x):
    k = math.sqrt(2.0 / math.pi)
    t = jnp.tanh(k * (x + 0.044715 * (x * x * x)))
    return 0.5 * (1.0 + t) + 0.5 * x * (1.0 - t * t) * k * (1.0 + 3.0 * 0.044715 * x * x)


def _one_minus_exp(z):
    series = -(z + 0.5 * z * z + (1.0 / 6.0) * z * z * z)
    return jnp.where(z > -1e-3, series, 1.0 - jnp.exp(z))


def _row_tile(rows, cap=ROW_T, mult=16):
    t = min(rows, cap)
    while rows % t or (t % mult and t != rows):
        t -= 1
    return t


def _mat_dims(z):
    return (z.shape[-2], z.shape[-1] * (z.shape[0] if z.ndim == 3 else 1))


def _mat_spec(arr, rblk, cblk, rsel, csel):
    if arr.ndim == 2:
        return pl.BlockSpec((rblk, cblk), lambda i, j, k: ((i, j, k)[rsel], (i, j, k)[csel]))
    nw = arr.shape[2]
    assert nw % cblk == 0, (arr.shape, cblk)
    per = nw // cblk
    return pl.BlockSpec((None, rblk, cblk),
                        lambda i, j, k: ((i, j, k)[csel] // per, (i, j, k)[rsel], (i, j, k)[csel] % per))


def _mm(a, b, mode, out_dtype, name, add=None, out_blocks=None, tm=1024, tn=1024, tk=None, after=None):
    ar, ac = _mat_dims(a)
    br, bc = _mat_dims(b)
    if mode == "nn":
        M, K, N = ar, ac, bc
        assert br == K
    elif mode == "nt":
        M, K, N = ar, ac, br
        assert bc == K
    else:
        M, K, N = ac, ar, bc
        assert br == K
    tm, tn = min(tm, M), min(tn, N)
    tk = K if tk is None or mode == "tn" else min(tk, K)
    assert M % tm == 0 and N % tn == 0 and K % tk == 0, (name, M, N, K, tm, tn, tk)
    nk = K // tk
    nj = N // tn
    if mode == "nn":
        specs = [_mat_spec(a, tm, tk, 0, 2), _mat_spec(b, tk, tn, 2, 1)]
        dn = NN
    elif mode == "nt":
        specs = [_mat_spec(a, tm, tk, 0, 2), _mat_spec(b, tn, tk, 1, 2)]
        dn = NT
    else:
        specs = [_mat_spec(a, tk, tm, 2, 0), _mat_spec(b, tk, tn, 2, 1)]
        dn = TN
    args = [a, b]
    if add is not None:
        specs.append(pl.BlockSpec((tm, tn), lambda i, j, k: (i, j)))
        args.append(add)
    specs += _behind(after)[0]
    args += _behind(after)[1]
    n_in = len(args)
    if out_blocks is None:
        out_shape = jax.ShapeDtypeStruct((M, N), out_dtype)
        out_spec = pl.BlockSpec((tm, tn), lambda i, j, k: (i, j))
    else:
        nb = out_blocks
        nw = N // nb
        assert nw % tn == 0
        per = nw // tn
        out_shape = jax.ShapeDtypeStruct((nb, M, nw), out_dtype)
        out_spec = pl.BlockSpec((None, tm, tn), lambda i, j, k: (j // per, i, j % per))
    keep_t = mode == "tn" and nj > 1
    scratch = []
    if nk > 1:
        scratch.append(pltpu.VMEM((tm, tn), F32))
    if keep_t:
        scratch.append(pltpu.VMEM((tm, tk), a.dtype))

    def body(*refs):
        a_ref, b_ref = refs[0], refs[1]
        add_ref = refs[2] if add is not None else None
        o_ref = refs[n_in]

        def finish(r):
            if add_ref is not None:
                r = r + add_ref[...]
            o_ref[...] = r.astype(out_dtype)

        if keep_t:
            at_ref = refs[-1]

            @pl.when(pl.program_id(1) == 0)
            def _():
                at_ref[...] = a_ref[...].T

            finish(_dot(at_ref[...], b_ref[...], NN))
        elif nk == 1:
            finish(_dot(a_ref[...], b_ref[...], dn))
        else:
            acc_ref = refs[n_in + 1]
            k = pl.program_id(2)

            @pl.when(k == 0)
            def _():
                acc_ref[...] = jnp.zeros_like(acc_ref)

            acc_ref[...] += _dot(a_ref[...], b_ref[...], dn)

            @pl.when(k == nk - 1)
            def _():
                finish(acc_ref[...])

    return pl.pallas_call(
        body, name=name, grid=(M // tm, nj, nk), in_specs=specs, out_specs=out_spec, out_shape=out_shape,
        scratch_shapes=scratch,
        compiler_params=_params("parallel", "arbitrary" if keep_t else "parallel", "arbitrary"),
    )(*args)


def _rms_fwd(xs, gs, name, after=None):
    n = len(xs)
    rows = xs[0].shape[0]
    widths = [x.shape[1] for x in xs]
    tr = _row_tile(rows)
    more_specs, more = _behind(after)

    def body(*refs):
        o_ref = refs[2 * n + len(more)]
        off = 0
        for t in range(n):
            o_ref[:, off:off + widths[t]] = _rms(refs[t][...], refs[n + t][...]).astype(BF16)
            off += widths[t]

    return pl.pallas_call(
        body, name=name, grid=(rows // tr,),
        in_specs=[pl.BlockSpec((tr, w), lambda i: (i, 0)) for w in widths]
        + [pl.BlockSpec((1, w), lambda i: (0, 0)) for w in widths] + more_specs,
        out_specs=pl.BlockSpec((tr, sum(widths)), lambda i: (i, 0)),
        out_shape=jax.ShapeDtypeStruct((rows, sum(widths)), BF16),
        compiler_params=_params("parallel"),
    )(*xs, *gs, *more)


def _rms_bwd(x, g, dy, name, dy_col=0, res=None, want_dx=True, after=None):
    rows, w = x.shape
    tr = _row_tile(rows)
    has_res = res is not None
    more_specs, more = _behind(after)

    def body(*refs):
        x_ref, g_ref, dy_ref = refs[:3]
        res_ref = refs[3] if has_res else None
        outs = refs[3 + has_res + len(more):]
        dg_ref = outs[-1]
        dx, dg = _rms_grad(x_ref[...], g_ref[...], dy_ref[...])

        @pl.when(pl.program_id(0) == 0)
        def _():
            dg_ref[...] = jnp.zeros_like(dg_ref)

        dg_ref[...] += dg
        if want_dx:
            if has_res:
                dx = dx + res_ref[...]
            outs[0][...] = dx
            outs[1][...] = dx.astype(BF16)

    row_spec = pl.BlockSpec((tr, w), lambda i: (i, 0))
    in_specs = [row_spec, pl.BlockSpec((1, w), lambda i: (0, 0)), pl.BlockSpec((tr, w), lambda i: (i, dy_col))]
    args = [x, g, dy]
    if has_res:
        in_specs.append(row_spec)
        args.append(res)
    in_specs += more_specs
    args += more
    out_specs, out_shape = [], []
    if want_dx:
        out_specs += [row_spec, row_spec]
        out_shape += [jax.ShapeDtypeStruct((rows, w), F32), jax.ShapeDtypeStruct((rows, w), BF16)]
    out_specs.append(pl.BlockSpec((1, w), lambda i: (0, 0)))
    out_shape.append(jax.ShapeDtypeStruct((1, w), F32))
    return pl.pallas_call(
        body, name=name, grid=(rows // tr,), in_specs=in_specs, out_specs=out_specs, out_shape=out_shape,
        compiler_params=_params("arbitrary"),
    )(*args)


def _loss_head(y, target, name):
    rows, w = y.shape
    tr = _row_tile(rows)

    def body(y_ref, t_ref, dy_ref, dyb_ref, acc_ref):
        e = y_ref[...] - t_ref[...]

        @pl.when(pl.program_id(0) == 0)
        def _():
            acc_ref[...] = jnp.zeros_like(acc_ref)

        acc_ref[...] += jnp.sum(e * e)
        dy = e * (1.0 / w)
        dy_ref[...] = dy
        dyb_ref[...] = dy.astype(BF16)

    row_spec = pl.BlockSpec((tr, w), lambda i: (i, 0))
    return pl.pallas_call(
        body, name=name, grid=(rows // tr,), in_specs=[row_spec, row_spec],
        out_specs=[row_spec, row_spec, pl.BlockSpec((8, 128), lambda i: (0, 0))],
        out_shape=[jax.ShapeDtypeStruct((rows, w), F32), jax.ShapeDtypeStruct((rows, w), BF16),
                   jax.ShapeDtypeStruct((8, 128), F32)],
        compiler_params=_params("arbitrary"),
    )(y, target)


def _qk_fwd(proj, g_q, g_k, name, after=None):
    rows = proj.shape[0]
    w = FOX_H * HEAD
    tr = _row_tile(rows)
    more_specs, more = _behind(after)

    def body(q_ref, k_ref, v_ref, gq_ref, gk_ref, *rest):
        qn_ref, kn_ref, vb_ref = rest[len(more):]
        for h in range(FOX_H):
            sl = slice(h * HEAD, (h + 1) * HEAD)
            qn_ref[:, sl] = _rms(q_ref[:, sl], gq_ref[...]).astype(BF16)
            kn_ref[:, sl] = _rms(k_ref[:, sl], gk_ref[...]).astype(BF16)
        vb_ref[...] = v_ref[...].astype(BF16)

    gain = pl.BlockSpec((1, HEAD), lambda i: (0, 0))
    out = pl.BlockSpec((tr, w), lambda i: (i, 0))
    return pl.pallas_call(
        body, name=name, grid=(rows // tr,),
        in_specs=[pl.BlockSpec((tr, w), lambda i: (i, 0)), pl.BlockSpec((tr, w), lambda i: (i, 1)),
                  pl.BlockSpec((tr, w), lambda i: (i, 2)), gain, gain] + more_specs,
        out_specs=[out, out, out], out_shape=[jax.ShapeDtypeStruct((rows, w), BF16)] * 3,
        compiler_params=_params("parallel"),
    )(proj, proj, proj, g_q, g_k, *more)


def _qk_bwd(proj, g_q, g_k, dqn, dkn, name):
    rows = proj.shape[0]
    w = FOX_H * HEAD
    tr = _row_tile(rows)

    def body(q_ref, k_ref, gq_ref, gk_ref, dqn_ref, dkn_ref, dq_ref, dk_ref, dgq_ref, dgk_ref):
        @pl.when(pl.program_id(0) == 0)
        def _():
            dgq_ref[...] = jnp.zeros_like(dgq_ref)
            dgk_ref[...] = jnp.zeros_like(dgk_ref)

        for h in range(FOX_H):
            sl = slice(h * HEAD, (h + 1) * HEAD)
            dq, dgq = _rms_grad(q_ref[:, sl], gq_ref[...], dqn_ref[:, sl])
            dk, dgk = _rms_grad(k_ref[:, sl], gk_ref[...], dkn_ref[:, sl])
            dq_ref[:, sl] = dq.astype(BF16)
            dk_ref[:, sl] = dk.astype(BF16)
            dgq_ref[...] += dgq
            dgk_ref[...] += dgk

    gain = pl.BlockSpec((1, HEAD), lambda i: (0, 0))
    row = pl.BlockSpec((tr, w), lambda i: (i, 0))
    return pl.pallas_call(
        body, name=name, grid=(rows // tr,),
        in_specs=[row, pl.BlockSpec((tr, w), lambda i: (i, 1)), gain, gain, row, row],
        out_specs=[row, row, gain, gain],
        out_shape=[jax.ShapeDtypeStruct((rows, w), BF16)] * 2 + [jax.ShapeDtypeStruct((1, HEAD), F32)] * 2,
        compiler_params=_params("arbitrary"),
    )(proj, proj, g_q, g_k, dqn, dkn)


def _fgate_fwd(f_logit, b_pad, name):
    S = f_logit.shape[0]
    T = ATT_T
    nb = S // T

    def body(f_ref, b_ref, col_ref, row_ref, c_scr):
        tri = (lax.broadcasted_iota(jnp.int32, (T, T), 0) >= lax.broadcasted_iota(jnp.int32, (T, T), 1)).astype(F32)
        carry = jnp.zeros((1, HEAD), F32)
        for blk in range(nb):
            z = f_ref[blk * T:(blk + 1) * T, :] + b_ref[...]
            lf = jnp.minimum(z, 0.0) - jnp.log1p(jnp.exp(-jnp.abs(z)))
            cb = jnp.dot(tri, lf, precision=lax.Precision.HIGHEST, preferred_element_type=F32) + carry
            c_scr[blk * T:(blk + 1) * T, :] = cb
            carry = cb[T - 1:T, :]
        c = c_scr[...]
        lane = lax.broadcasted_iota(jnp.int32, c.shape, 1)
        ct = c.T
        for h in range(FOX_H):
            col_ref[h] = jnp.sum(jnp.where(lane == h, c, 0.0), axis=1, keepdims=True)
            for jb in range(nb):
                row_ref[h, jb] = ct[h:h + 1, jb * T:(jb + 1) * T]

    return pl.pallas_call(
        body, name=name,
        out_shape=[jax.ShapeDtypeStruct((FOX_H, S, 1), F32), jax.ShapeDtypeStruct((FOX_H, nb, 1, T), F32)],
        scratch_shapes=[pltpu.VMEM((S, HEAD), F32)], compiler_params=_params(),
    )(f_logit, b_pad)


def _fgate_bwd(f_logit, b_pad, dc_col, dc_row, name):
    S = f_logit.shape[0]
    T = ATT_T
    nb = S // T

    def body(f_ref, b_ref, dcol_ref, drow_ref, df_ref, db_ref, dc_scr, rt_scr):
        lane = lax.broadcasted_iota(jnp.int32, (S, HEAD), 1)
        sub = lax.broadcasted_iota(jnp.int32, (HEAD, T), 0)
        dc = jnp.zeros((S, HEAD), F32)
        for h in range(FOX_H):
            dc = jnp.where(lane == h, dcol_ref[h], dc)
        for jb in range(nb):
            rt = jnp.zeros((HEAD, T), F32)
            for h in range(FOX_H):
                rt = jnp.where(sub == h, drow_ref[h, jb], rt)
            rt_scr[jb * T:(jb + 1) * T, :] = rt.T
        dc_scr[...] = dc - rt_scr[...]
        tri = (lax.broadcasted_iota(jnp.int32, (T, T), 0) <= lax.broadcasted_iota(jnp.int32, (T, T), 1)).astype(F32)
        carry = jnp.zeros((1, HEAD), F32)
        db = jnp.zeros((1, HEAD), F32)
        for blk in reversed(range(nb)):
            rows = slice(blk * T, (blk + 1) * T)
            dlf = jnp.dot(tri, dc_scr[rows, :], precision=lax.Precision.HIGHEST, preferred_element_type=F32) + carry
            carry = dlf[0:1, :]
            z = f_ref[rows, :] + b_ref[...]
            df = dlf * jax.nn.sigmoid(-z)
            df_ref[rows, :] = df.astype(BF16)
            db = db + jnp.sum(df, axis=0, keepdims=True)
        db_ref[...] = db

    return pl.pallas_call(
        body, name=name,
        out_shape=[jax.ShapeDtypeStruct((S, HEAD), BF16), jax.ShapeDtypeStruct((1, HEAD), F32)],
        scratch_shapes=[pltpu.VMEM((S, HEAD), F32), pltpu.VMEM((S, HEAD), F32)], compiler_params=_params(),
    )(f_logit, b_pad, dc_col, dc_row)


def _fox_fwd(qn, kn, vb, c_col, c_row, name):
    S = qn.shape[0]
    T = ATT_T
    nb = S // T
    scale = 1.0 / math.sqrt(HEAD)

    def body(q_ref, k_ref, v_ref, cc_ref, cr_ref, o_ref, lse_ref):
        i = pl.program_id(1)
        q = q_ref[...]
        cc = cc_ref[...]
        rows = i * T + lax.broadcasted_iota(jnp.int32, (T, T), 0)
        lanes = lax.broadcasted_iota(jnp.int32, (T, T), 1)

        def step(j, carry):
            m, l, acc = carry
            sl = pl.ds(pl.multiple_of(j * T, T), T)
            s = _dot(q, k_ref[sl, :], NT) * scale + cc - cr_ref[j]
            s = jnp.where(rows >= j * T + lanes, s, NEG)
            m2 = jnp.maximum(m, jnp.max(s, axis=1, keepdims=True))
            p = jnp.exp(s - m2)
            al = jnp.exp(m - m2)
            return m2, al * l + jnp.sum(p, axis=1, keepdims=True), al * acc + _dot(p.astype(BF16), v_ref[sl, :])

        init = (jnp.full((T, 1), NEG, F32), jnp.zeros((T, 1), F32), jnp.zeros((T, HEAD), F32))
        m, l, acc = lax.fori_loop(0, i + 1, step, init)
        o_ref[...] = acc / l
        lse_ref[...] = m + jnp.log(l)

    head = pl.BlockSpec((S, HEAD), lambda h, i: (0, h))
    return pl.pallas_call(
        body, name=name, grid=(FOX_H, nb),
        in_specs=[pl.BlockSpec((T, HEAD), lambda h, i: (i, h)), head, head,
                  pl.BlockSpec((None, T, 1), lambda h, i: (h, i, 0)),
                  pl.BlockSpec((None, nb, 1, T), lambda h, i: (h, 0, 0, 0))],
        out_specs=[pl.BlockSpec((T, HEAD), lambda h, i: (i, h)), pl.BlockSpec((None, T, 1), lambda h, i: (h, i, 0))],
        out_shape=[jax.ShapeDtypeStruct((S, FOX_H * HEAD), F32), jax.ShapeDtypeStruct((FOX_H, S, 1), F32)],
        compiler_params=_params("parallel", "parallel"),
    )(qn, kn, vb, c_col, c_row)


def _fox_bwd(qn, kn, vb, c_col, c_row, lse, o, do, name):
    S = qn.shape[0]
    T = ATT_T
    nb = S // T
    scale = 1.0 / math.sqrt(HEAD)

    def body(q_ref, k_ref, v_ref, cc_ref, cr_ref, lse_ref, o_ref, do_ref, dq_ref, dk_ref, dv_ref, dcc_ref, dcr_ref):
        j = pl.program_id(1)

        @pl.when(j == 0)
        def _():
            dq_ref[...] = jnp.zeros_like(dq_ref)
            dcc_ref[...] = jnp.zeros_like(dcc_ref)

        k = k_ref[...]
        v = v_ref[...]
        cr = cr_ref[...]
        cols = j * T + lax.broadcasted_iota(jnp.int32, (T, T), 1)
        subl = lax.broadcasted_iota(jnp.int32, (T, T), 0)

        def step(i, carry):
            dk, dv, dcr = carry
            sl = pl.ds(pl.multiple_of(i * T, T), T)
            q = q_ref[sl, :]
            d_o = do_ref[sl, :]
            s = _dot(q, k, NT) * scale + cc_ref[sl, :] - cr
            s = jnp.where(i * T + subl >= cols, s, NEG)
            p = jnp.exp(s - lse_ref[sl, :])
            dob = d_o.astype(BF16)
            dp = _dot(dob, v, NT)
            delta = jnp.sum(d_o * o_ref[sl, :], axis=1, keepdims=True)
            ds = p * (dp - delta)
            dsb = ds.astype(BF16)
            dq_ref[sl, :] += _dot(dsb, k) * scale
            dcc_ref[sl, :] += jnp.sum(ds, axis=1, keepdims=True)
            return (dk + _dot(dsb, q, TN), dv + _dot(p.astype(BF16), dob, TN), dcr + jnp.sum(ds, axis=0, keepdims=True))

        init = (jnp.zeros((T, HEAD), F32), jnp.zeros((T, HEAD), F32), jnp.zeros((1, T), F32))
        dk, dv, dcr = lax.fori_loop(j, nb, step, init)
        dk_ref[...] = dk * scale
        dv_ref[...] = dv.astype(BF16)
        dcr_ref[...] = dcr

    head = pl.BlockSpec((S, HEAD), lambda h, j: (0, h))
    tile = pl.BlockSpec((T, HEAD), lambda h, j: (j, h))
    col = pl.BlockSpec((None, S, 1), lambda h, j: (h, 0, 0))
    row = pl.BlockSpec((None, None, 1, T), lambda h, j: (h, j, 0, 0))
    w = FOX_H * HEAD
    return pl.pallas_call(
        body, name=name, grid=(FOX_H, nb),
        in_specs=[head, tile, tile, col, row, col, head, head],
        out_specs=[head, tile, tile, col, row],
        out_shape=[jax.ShapeDtypeStruct((S, w), F32), jax.ShapeDtypeStruct((S, w), F32), jax.ShapeDtypeStruct((S, w), BF16),
                   jax.ShapeDtypeStruct((FOX_H, S, 1), F32), jax.ShapeDtypeStruct((FOX_H, nb, 1, T), F32)],
        compiler_params=_params("arbitrary", "arbitrary"),
    )(qn, kn, vb, c_col, c_row, lse, o, do)


def _lru_gates(uc, wra, bra, wri, bri, lam):
    ucb = uc.astype(BF16)
    r = jax.nn.sigmoid(_dot(ucb, wra.astype(BF16)) + bra)
    ig = jax.nn.sigmoid(_dot(ucb, wri.astype(BF16)) + bri)
    sp = jnp.maximum(-lam, 0.0) + jnp.log1p(jnp.exp(-jnp.abs(lam)))
    log_a = -LRU_C * r * sp
    a = jnp.exp(log_a)
    mult = jnp.sqrt(_one_minus_exp(2.0 * log_a))
    return r, ig, sp, a, mult


def _conv(pad_ref, cw, cb, S):
    uc = cb
    for j in range(CONV_K):
        uc = uc + cw[j:j + 1, :] * pad_ref[5 + j:5 + j + S, :]
    return uc


def _lru_specs(S, n_proj_cols):
    u_col = 3 * FOX_H
    g_col = u_col + LRU_NB
    blk = pl.BlockSpec((S, HEAD), lambda n: (0, n))
    vec = pl.BlockSpec((1, HEAD), lambda n: (0, n))
    mat = pl.BlockSpec((None, HEAD, HEAD), lambda n: (n, 0, 0))
    return dict(
        u=pl.BlockSpec((S, HEAD), lambda n: (0, u_col + n)), gate=pl.BlockSpec((S, HEAD), lambda n: (0, g_col + n)),
        blk=blk, vec=vec, mat=mat, cw=pl.BlockSpec((CONV_K, HEAD), lambda n: (0, n)))


def _lru_fwd(proj, conv_w, conv_b, w_ra, b_ra, w_ri, b_ri, lam, name):
    S = proj.shape[0]
    sp_ = _lru_specs(S, proj.shape[1])
    rows8 = S // 8

    def body(u_ref, gt_ref, cw_ref, cb_ref, wra_ref, bra_ref, wri_ref, bri_ref, lam_ref, y_ref, h_ref, pad, a_scr, b_scr):
        pad[0:8, :] = jnp.zeros((8, HEAD), F32)
        pad[8:S + 8, :] = u_ref[...]
        uc = _conv(pad, cw_ref[...], cb_ref[...], S)
        r, ig, sp, a, mult = _lru_gates(uc, wra_ref[...], bra_ref[...], wri_ref[...], bri_ref[...], lam_ref[...])
        a_scr[...] = a
        b_scr[...] = mult * (ig * uc)
        sub = lax.broadcasted_iota(jnp.int32, (8, HEAD), 0)

        def step(t, carry):
            sl = pl.ds(pl.multiple_of(t * 8, 8), 8)
            A, B = a_scr[sl, :], b_scr[sl, :]
            for d in (1, 2, 4):
                m = sub >= d
                B = jnp.where(m, A * pltpu.roll(B, d, 0) + B, B)
                A = jnp.where(m, A * pltpu.roll(A, d, 0), A)
            h = A * carry + B
            h_ref[sl, :] = h
            return h[7:8, :]

        lax.fori_loop(0, rows8, step, jnp.zeros((1, HEAD), F32))
        y_ref[...] = h_ref[...] * _gelu(gt_ref[...])

    w = LRU_NB * HEAD
    return pl.pallas_call(
        body, name=name, grid=(LRU_NB,),
        in_specs=[sp_["u"], sp_["gate"], sp_["cw"], sp_["vec"], sp_["mat"], sp_["vec"], sp_["mat"], sp_["vec"], sp_["vec"]],
        out_specs=[sp_["blk"], sp_["blk"]],
        out_shape=[jax.ShapeDtypeStruct((S, w), F32)] * 2,
        scratch_shapes=[pltpu.VMEM((S + 8, HEAD), F32), pltpu.VMEM((S, HEAD), F32), pltpu.VMEM((S, HEAD), F32)],
        compiler_params=_params("parallel"),
    )(proj, proj, conv_w, conv_b, w_ra, b_ra, w_ri, b_ri, lam)


def _lru_bwd(proj, h, dy, conv_w, conv_b, w_ra, b_ra, w_ri, b_ri, lam, name):
    S = proj.shape[0]
    sp_ = _lru_specs(S, proj.shape[1])
    rows8 = S // 8

    def body(u_ref, gt_ref, h_ref, dy_ref, cw_ref, cb_ref, wra_ref, bra_ref, wri_ref, bri_ref, lam_ref,
             du_ref, dgt_ref, dcw_ref, dcb_ref, dwra_ref, dbra_ref, dwri_ref, dbri_ref, dlam_ref,
             pad, an_scr, d_scr, g_scr, hp_scr):
        zero8 = jnp.zeros((8, HEAD), F32)
        pad[0:8, :] = zero8
        pad[8:S + 8, :] = u_ref[...]
        cw = cw_ref[...]
        uc = _conv(pad, cw, cb_ref[...], S)
        wra, wri, lam_v = wra_ref[...], wri_ref[...], lam_ref[...]
        r, ig, sp, a, mult = _lru_gates(uc, wra, bra_ref[...], wri, bri_ref[...], lam_v)
        gate = gt_ref[...]
        dy_v = dy_ref[...]
        hv = h_ref[...]
        dgt_ref[...] = (dy_v * hv * _gelu_grad(gate)).astype(BF16)
        d_scr[...] = dy_v * _gelu(gate)
        g_scr[0:S, :] = a
        g_scr[S:S + 8, :] = zero8
        an_scr[...] = g_scr[1:S + 1, :]
        sub = lax.broadcasted_iota(jnp.int32, (8, HEAD), 0)

        def step(t, carry):
            sl = pl.ds(pl.multiple_of((rows8 - 1 - t) * 8, 8), 8)
            A, D = an_scr[sl, :], d_scr[sl, :]
            for d in (1, 2, 4):
                m = sub + d <= 7
                D = jnp.where(m, A * pltpu.roll(D, 8 - d, 0) + D, D)
                A = jnp.where(m, A * pltpu.roll(A, 8 - d, 0), A)
            g = A * carry + D
            g_scr[sl, :] = g
            return g[0:1, :]

        lax.fori_loop(0, rows8, step, jnp.zeros((1, HEAD), F32))
        g = g_scr[0:S, :]
        hp_scr[0:8, :] = zero8
        hp_scr[8:S + 8, :] = hv
        da = g * hp_scr[7:S + 7, :]
        iu = ig * uc
        dmult = g * iu
        diu = g * mult
        dig = diu * uc
        duc = diu * ig
        dlog_a = da * a - dmult * (a * a) / mult
        dr = dlog_a * (-LRU_C * sp)
        dsp = jnp.sum(dlog_a * (-LRU_C * r), axis=0, keepdims=True)
        dlam_ref[...] = -dsp * jax.nn.sigmoid(-lam_v)
        dpr = dr * r * (1.0 - r)
        dpi = dig * ig * (1.0 - ig)
        dbra_ref[...] = jnp.sum(dpr, axis=0, keepdims=True)
        dbri_ref[...] = jnp.sum(dpi, axis=0, keepdims=True)
        ucb = uc.astype(BF16)
        dprb, dpib = dpr.astype(BF16), dpi.astype(BF16)
        dwra_ref[...] = _dot(ucb, dprb, TN)
        dwri_ref[...] = _dot(ucb, dpib, TN)
        duc = duc + _dot(dprb, wra.astype(BF16), NT) + _dot(dpib, wri.astype(BF16), NT)
        dcb_ref[...] = jnp.sum(duc, axis=0, keepdims=True)
        for j in range(CONV_K):
            dcw_ref[j:j + 1, :] = jnp.sum(duc * pad[5 + j:5 + j + S, :], axis=0, keepdims=True)
        g_scr[0:S, :] = duc
        g_scr[S:S + 8, :] = zero8
        du = jnp.zeros((S, HEAD), F32)
        for j in range(CONV_K):
            du = du + cw[j:j + 1, :] * g_scr[3 - j:3 - j + S, :]
        du_ref[...] = du.astype(BF16)

    w = LRU_NB * HEAD
    bf = jax.ShapeDtypeStruct((S, w), BF16)
    vec = jax.ShapeDtypeStruct((1, w), F32)
    mat = jax.ShapeDtypeStruct((LRU_NB, HEAD, HEAD), F32)
    return pl.pallas_call(
        body, name=name, grid=(LRU_NB,),
        in_specs=[sp_["u"], sp_["gate"], sp_["blk"], sp_["blk"], sp_["cw"], sp_["vec"], sp_["mat"], sp_["vec"],
                  sp_["mat"], sp_["vec"], sp_["vec"]],
        out_specs=[sp_["blk"], sp_["blk"], sp_["cw"], sp_["vec"], sp_["mat"], sp_["vec"], sp_["mat"], sp_["vec"], sp_["vec"]],
        out_shape=[bf, bf, jax.ShapeDtypeStruct((CONV_K, w), F32), vec, mat, vec, mat, vec, vec],
        scratch_shapes=[pltpu.VMEM((S + 8, HEAD), F32), pltpu.VMEM((S, HEAD), F32), pltpu.VMEM((S, HEAD), F32),
                        pltpu.VMEM((S + 8, HEAD), F32), pltpu.VMEM((S + 8, HEAD), F32)],
        compiler_params=_params("parallel"),
    )(proj, proj, h, dy, conv_w, conv_b, w_ra, b_ra, w_ri, b_ri, lam)


def _xattn_fwd(cq, ckv, g_cq, g_ck, name):
    S, w = cq.shape
    M = ckv.shape[0]
    tr = _row_tile(S)
    scale = 1.0 / math.sqrt(HEAD)

    def body(cq_ref, ckv_ref, gq_ref, gk_ref, o_ref):
        for h in range(XATT_H):
            sl = slice(h * HEAD, (h + 1) * HEAD)
            qn = _rms(cq_ref[:, sl], gq_ref[...]).astype(BF16)
            kn = _rms(ckv_ref[:, sl], gk_ref[...]).astype(BF16)
            v = ckv_ref[:, w + h * HEAD:w + (h + 1) * HEAD].astype(BF16)
            s = _dot(qn, kn, NT) * scale
            p = jnp.exp(s - jnp.max(s, axis=1, keepdims=True))
            p = p / jnp.sum(p, axis=1, keepdims=True)
            o_ref[:, sl] = _dot(p.astype(BF16), v).astype(BF16)

    gain = pl.BlockSpec((1, HEAD), lambda i: (0, 0))
    return pl.pallas_call(
        body, name=name, grid=(S // tr,),
        in_specs=[pl.BlockSpec((tr, w), lambda i: (i, 0)), pl.BlockSpec((M, 2 * w), lambda i: (0, 0)), gain, gain],
        out_specs=pl.BlockSpec((tr, w), lambda i: (i, 0)), out_shape=jax.ShapeDtypeStruct((S, w), BF16),
        compiler_params=_params("parallel"),
    )(cq, ckv, g_cq, g_ck)


def _xattn_bwd(cq, ckv, g_cq, g_ck, do, name):
    S, w = cq.shape
    M = ckv.shape[0]
    tr = _row_tile(S)
    nsteps = S // tr
    scale = 1.0 / math.sqrt(HEAD)

    def body(cq_ref, ckv_ref, gq_ref, gk_ref, do_ref, dcq_ref, dckv_ref, dgq_ref, dgk_ref, dkn_scr, dv_scr):
        step = pl.program_id(0)

        @pl.when(step == 0)
        def _():
            dkn_scr[...] = jnp.zeros_like(dkn_scr)
            dv_scr[...] = jnp.zeros_like(dv_scr)
            dgq_ref[...] = jnp.zeros_like(dgq_ref)

        for h in range(XATT_H):
            sl = slice(h * HEAD, (h + 1) * HEAD)
            q_raw = cq_ref[:, sl]
            qn = _rms(q_raw, gq_ref[...]).astype(BF16)
            kn = _rms(ckv_ref[:, sl], gk_ref[...]).astype(BF16)
            v = ckv_ref[:, w + h * HEAD:w + (h + 1) * HEAD].astype(BF16)
            s = _dot(qn, kn, NT) * scale
            p = jnp.exp(s - jnp.max(s, axis=1, keepdims=True))
            p = p / jnp.sum(p, axis=1, keepdims=True)
            dob = do_ref[:, sl].astype(BF16)
            dp = _dot(dob, v, NT)
            ds = p * (dp - jnp.sum(p * dp, axis=1, keepdims=True)) * scale
            dsb = ds.astype(BF16)
            dv_scr[:, sl] += _dot(p.astype(BF16), dob, TN)
            dkn_scr[:, sl] += _dot(dsb, qn, TN)
            dq, dgq = _rms_grad(q_raw, gq_ref[...], _dot(dsb, kn))
            dcq_ref[:, sl] = dq.astype(BF16)
            dgq_ref[...] += dgq

        @pl.when(step == nsteps - 1)
        def _():
            dgk = jnp.zeros((1, HEAD), F32)
            for h in range(XATT_H):
                sl = slice(h * HEAD, (h + 1) * HEAD)
                dk, dgk_h = _rms_grad(ckv_ref[:, sl], gk_ref[...], dkn_scr[:, sl])
                dckv_ref[:, sl] = dk.astype(BF16)
                dgk = dgk + dgk_h
            dckv_ref[:, w:2 * w] = dv_scr[...].astype(BF16)
            dgk_ref[...] = dgk

    gain = pl.BlockSpec((1, HEAD), lambda i: (0, 0))
    row = pl.BlockSpec((tr, w), lambda i: (i, 0))
    mem = pl.BlockSpec((M, 2 * w), lambda i: (0, 0))
    return pl.pallas_call(
        body, name=name, grid=(nsteps,), in_specs=[row, mem, gain, gain, row], out_specs=[row, mem, gain, gain],
        out_shape=[jax.ShapeDtypeStruct((S, w), BF16), jax.ShapeDtypeStruct((M, 2 * w), BF16),
                   jax.ShapeDtypeStruct((1, HEAD), F32), jax.ShapeDtypeStruct((1, HEAD), F32)],
        scratch_shapes=[pltpu.VMEM((M, w), F32), pltpu.VMEM((M, w), F32)],
        compiler_params=_params("arbitrary"),
    )(cq, ckv, g_cq, g_ck, do)


def _swiglu_fwd(gu, name, n_ct=4):
    S, w2 = gu.shape
    f = w2 // 2
    tc = f // n_ct
    tr = _row_tile(S)

    def body(g_ref, u_ref, o_ref):
        g = g_ref[...]
        o_ref[...] = (g * jax.nn.sigmoid(g) * u_ref[...]).astype(BF16)

    return pl.pallas_call(
        body, name=name, grid=(S // tr, n_ct),
        in_specs=[pl.BlockSpec((tr, tc), lambda i, j: (i, j)), pl.BlockSpec((tr, tc), lambda i, j: (i, j + n_ct))],
        out_specs=pl.BlockSpec((tr, tc), lambda i, j: (i, j)), out_shape=jax.ShapeDtypeStruct((S, f), BF16),
        compiler_params=_params("parallel", "parallel"),
    )(gu, gu)


def _swiglu_bwd(gu, dact, name, n_ct=4):
    S, w2 = gu.shape
    f = w2 // 2
    tc = f // n_ct
    tr = _row_tile(S)

    def body(g_ref, u_ref, da_ref, o_ref):
        g = g_ref[...]
        sg = jax.nn.sigmoid(g)
        da = da_ref[...]
        half = pl.program_id(1) // n_ct

        @pl.when(half == 0)
        def _():
            o_ref[...] = (da * u_ref[...] * (sg * (1.0 + g * (1.0 - sg)))).astype(BF16)

        @pl.when(half == 1)
        def _():
            o_ref[...] = (da * (g * sg)).astype(BF16)

    return pl.pallas_call(
        body, name=name, grid=(S // tr, 2 * n_ct),
        in_specs=[pl.BlockSpec((tr, tc), lambda i, j: (i, j % n_ct)),
                  pl.BlockSpec((tr, tc), lambda i, j: (i, j % n_ct + n_ct)),
                  pl.BlockSpec((tr, tc), lambda i, j: (i, j % n_ct))],
        out_specs=pl.BlockSpec((tr, tc), lambda i, j: (i, j)), out_shape=jax.ShapeDtypeStruct((S, w2), BF16),
        compiler_params=_params("parallel", "parallel"),
    )(gu, gu, dact)


def _adamw_math(w, g, m, v):
    m = ADAM_B1 * m + (1.0 - ADAM_B1) * g
    v = ADAM_B2 * v + (1.0 - ADAM_B2) * (g * g)
    m_hat = m / (1.0 - ADAM_B1 ** ADAM_STEP)
    v_hat = v / (1.0 - ADAM_B2 ** ADAM_STEP)
    delta = -ADAM_LR * (m_hat / (jnp.sqrt(v_hat) + ADAM_EPS) + ADAM_WD * w)
    return delta, m, v


def _tile2(R, C, elems):
    if R % 16 == 0:
        return _row_tile(R, cap=max(16, elems // C // 16 * 16)), C
    tc = 128
    while C % (2 * tc) == 0 and R * 2 * tc <= elems:
        tc *= 2
    return R, tc


def _reduce_adamw(own, land, chip, w, m, v, name):
    _, R, C = own.shape
    tr, tc = _tile2(R, C, 1 << 18)

    def body(s_ref, o_ref, l1_ref, l2_ref, l3_ref, w_ref, m_ref, v_ref, g_ref, d_ref, nm_ref, nv_ref):
        g = o_ref[...].astype(F32) + l1_ref[...].astype(F32) + l2_ref[...].astype(F32) + l3_ref[...].astype(F32)
        g_ref[...] = g
        d_ref[...], nm_ref[...], nv_ref[...] = _adamw_math(w_ref[...], g, m_ref[...], v_ref[...])

    def part(flip):
        return pl.BlockSpec((None, tr, tc), lambda i, j, s: (s[0] ^ flip, i, j))

    blk = pl.BlockSpec((tr, tc), lambda i, j, s: (i, j))
    sds = jax.ShapeDtypeStruct((R, C), F32)
    return pl.pallas_call(
        body, name=name,
        grid_spec=pltpu.PrefetchScalarGridSpec(
            num_scalar_prefetch=1, grid=(R // tr, C // tc),
            in_specs=[part(0), part(1), part(2), part(3), blk, blk, blk], out_specs=[blk] * 4),
        out_shape=[sds] * 4, compiler_params=_params("parallel", "parallel"),
    )(chip, own, land, land, land, w, m, v)


def _pair_sum(g8, recv, core, name):
    _, R, C = g8.shape
    tr, tc = _tile2(R, C, 1 << 19)

    def body(c_ref, a_ref, b_ref, o_ref):
        o_ref[...] = (a_ref[...].astype(F32) + b_ref[...].astype(F32)).astype(BF16)

    return pl.pallas_call(
        body, name=name,
        grid_spec=pltpu.PrefetchScalarGridSpec(
            num_scalar_prefetch=1, grid=(N_CHIP, R // tr, C // tc),
            in_specs=[pl.BlockSpec((None, tr, tc), lambda q, i, j, c: (2 * q + c[0], i, j)),
                      pl.BlockSpec((None, tr, tc), lambda q, i, j, c: (q, i, j))],
            out_specs=pl.BlockSpec((None, tr, tc), lambda q, i, j, c: (q, i, j))),
        out_shape=jax.ShapeDtypeStruct((N_CHIP, R, C), BF16),
        compiler_params=_params("parallel", "parallel", "parallel"),
    )(core, g8, recv)


def _place():
    return lax.axis_index("x"), lax.axis_index("y"), lax.axis_index("c")


def _land_with_own(shard, dev, name):
    R, C = shard.shape
    tr, tc = _tile2(R, C, 1 << 19)

    def body(d_ref, s_ref, o_ref):
        o_ref[...] = s_ref[...]

    return pl.pallas_call(
        body, name=name,
        grid_spec=pltpu.PrefetchScalarGridSpec(
            num_scalar_prefetch=1, grid=(R // tr, C // tc),
            in_specs=[pl.BlockSpec((tr, tc), lambda i, j, d: (i, j))],
            out_specs=pl.BlockSpec((None, tr, tc), lambda i, j, d: (d[0], i, j))),
        out_shape=jax.ShapeDtypeStruct((N_DEV, R, C), shard.dtype),
        compiler_params=_params("parallel", "parallel"),
    )(dev, shard)


def _all_gather(shards, lands, name):
    n = len(shards)

    def body(*refs):
        ins, outs = refs[:n], refs[2 * n:3 * n]
        send_sems, recv_sems = refs[3 * n:]
        x, y, c = _place()
        me, sibling = (x, y, c), (x, y, 1 - c)
        chips = [(1 - x, y), (x, 1 - y), (1 - x, 1 - y)]

        def copy(a, k, block, to, src=None):
            dst = outs[a].at[4 * block[0] + 2 * block[1] + block[2]]
            return pltpu.make_async_remote_copy(
                src_ref=dst if src is None else src, dst_ref=dst, send_sem=send_sems.at[a, k],
                recv_sem=recv_sems.at[a, k], device_id=to, device_id_type=MESH)

        sent = []
        for a in range(n):
            sent.append(copy(a, 0, me, sibling, src=ins[a]))
            sent += [copy(a, 1 + j, me, (*chip, c), src=ins[a]) for j, chip in enumerate(chips)]
        for cp in sent:
            cp.start()
        for j, chip in enumerate(chips):
            for a in range(n):
                copy(a, 1 + j, (*chip, c), me).wait_recv()
                fwd = copy(a, 4 + j, (*chip, c), sibling)
                fwd.start()
                sent.append(fwd)
        for a in range(n):
            copy(a, 0, sibling, me).wait_recv()
        for j, chip in enumerate(chips):
            for a in range(n):
                copy(a, 4 + j, (*chip, 1 - c), me).wait_recv()
        for cp in sent:
            cp.wait_send()

    return pl.pallas_call(
        body, name=name, in_specs=[ANY] * (2 * n), out_specs=[ANY] * n,
        out_shape=[jax.ShapeDtypeStruct(l.shape, l.dtype) for l in lands],
        input_output_aliases={n + a: a for a in range(n)},
        scratch_shapes=[pltpu.SemaphoreType.DMA((n, 7)), pltpu.SemaphoreType.DMA((n, 7))],
    )(*shards, *lands)


def _pair_exchange(g8s, name):
    n = len(g8s)

    def body(*refs):
        ins, outs = refs[:n], refs[n:2 * n]
        send_sems, recv_sems = refs[2 * n:]
        x, y, c = _place()
        copies = []
        for a in range(n):
            for q in range(N_CHIP):
                copies.append(pltpu.make_async_remote_copy(
                    src_ref=ins[a].at[2 * q + 1 - c], dst_ref=outs[a].at[q], send_sem=send_sems.at[a, q],
                    recv_sem=recv_sems.at[a, q], device_id=(x, y, 1 - c), device_id_type=MESH))
        for cp in copies:
            cp.start()
        for cp in copies:
            cp.wait()

    return pl.pallas_call(
        body, name=name, in_specs=[ANY] * n, out_specs=[ANY] * n,
        out_shape=[jax.ShapeDtypeStruct((N_CHIP,) + g.shape[1:], g.dtype) for g in g8s],
        scratch_shapes=[pltpu.SemaphoreType.DMA((n, N_CHIP)), pltpu.SemaphoreType.DMA((n, N_CHIP))],
    )(*g8s)


def _chip_plan(srcs, lands):
    x, y, c = _place()
    mine = 2 * x + y
    plan = []
    for a in range(len(srcs)):
        for px, py in [(1 - x, y), (x, 1 - y), (1 - x, 1 - y)]:
            peer = 2 * px + py
            plan.append((srcs[a].at[peer], lands[a].at[mine], lands[a].at[peer], (px, py, c)))
    return plan


def _gather_plan(srcs, lands):
    x, y, c = _place()
    mine = 4 * x + 2 * y + c
    plan = []
    for a in range(len(srcs)):
        for px, py, pc in [(x, y, 1 - c), (1 - x, y, c), (x, 1 - y, c), (1 - x, 1 - y, c)]:
            plan.append((srcs[a], lands[a].at[mine], lands[a].at[4 * px + 2 * py + pc], (px, py, pc)))
    return plan


def _remote(src, dst, send_sem, recv_sem, peer):
    return pltpu.make_async_remote_copy(src_ref=src, dst_ref=dst, send_sem=send_sem, recv_sem=recv_sem,
                                        device_id=peer, device_id_type=MESH)


HBM = pl.BlockSpec(memory_space=pltpu.HBM)
SEMS = pl.BlockSpec(memory_space=pltpu.SEMAPHORE)
DATAFLOW = pltpu.SideEffectType.DATAFLOW_SIDE_EFFECTING


def _split_start(srcs, lands, after, plan_fn, per_array, name):
    n = len(srcs)
    ncopy = per_array * n

    def body(*refs):
        send_sems, recv_sems = refs[2 * n + 1], refs[2 * n + 2]
        token = refs[-1]
        for k, (src, dst, _, peer) in enumerate(plan_fn(refs[:n], refs[n:2 * n])):
            _remote(src, dst, send_sems.at[k], recv_sems.at[k], peer).start()
        token[...] = jnp.zeros_like(token)

    thru = [pltpu.HBM(a.shape, a.dtype) for a in (*srcs, *lands)]
    out = pl.pallas_call(
        body, name=name,
        out_shape=(pltpu.SemaphoreType.DMA((ncopy,)), pltpu.SemaphoreType.DMA((ncopy,)), *thru,
                   jax.ShapeDtypeStruct((8, 128), F32)),
        in_specs=[HBM] * (2 * n) + [ANY], out_specs=(SEMS, SEMS, *[HBM] * (2 * n), pl.BlockSpec(memory_space=pltpu.VMEM)),
        input_output_aliases={i: 2 + i for i in range(2 * n)},
        compiler_params=pltpu.CompilerParams(has_side_effects=DATAFLOW),
    )(*[pltpu.with_memory_space_constraint(a, pltpu.HBM) for a in (*srcs, *lands)], after)
    return out[0], out[1], list(out[2:2 + n]), list(out[2 + n:2 + 2 * n]), out[-1]


def _split_wait(send_sems, recv_sems, srcs, lands, after, plan_fn, name):
    n = len(srcs)

    def body(*refs):
        send_ref, recv_ref = refs[2 * n], refs[2 * n + 1]
        for k, (src, dst, mine, peer) in enumerate(plan_fn(refs[:n], refs[n:2 * n])):
            _remote(src, dst, send_ref.at[k], recv_ref.at[k], peer).wait_send()
            _remote(src, mine, send_ref.at[k], recv_ref.at[k], peer).wait_recv()

    thru = [pltpu.HBM(a.shape, a.dtype) for a in (*srcs, *lands)]
    out = pl.pallas_call(
        body, name=name, out_shape=tuple(thru),
        in_specs=[HBM] * (2 * n) + [SEMS, SEMS, ANY], out_specs=tuple([HBM] * (2 * n)),
        input_output_aliases={i: i for i in range(2 * n)},
        compiler_params=pltpu.CompilerParams(has_side_effects=DATAFLOW),
    )(*srcs, *lands, send_sems, recv_sems, after)
    return list(out[:n]), list(out[n:])


def _gather_forward(lands, name):
    n = len(lands)

    def body(*refs):
        landed, out = refs[:n], refs[n:2 * n]
        send_sems, recv_sems = refs[2 * n:]
        x, y, c = _place()
        chips = [(1 - x, y), (x, 1 - y), (1 - x, 1 - y)]
        sent = []
        for a in range(n):
            for j, (px, py) in enumerate(chips):
                blk = 4 * px + 2 * py + c
                sent.append(_remote(landed[a].at[blk], out[a].at[blk], send_sems.at[a, j], recv_sems.at[a, j], (x, y, 1 - c)))
        for cp in sent:
            cp.start()
        for a in range(n):
            for j, (px, py) in enumerate(chips):
                blk = 4 * px + 2 * py + 1 - c
                _remote(landed[a].at[blk], out[a].at[blk], send_sems.at[a, j], recv_sems.at[a, j], (x, y, 1 - c)).wait_recv()
        for cp in sent:
            cp.wait_send()

    return pl.pallas_call(
        body, name=name, in_specs=[ANY] * n, out_specs=[ANY] * n,
        out_shape=[jax.ShapeDtypeStruct(l.shape, l.dtype) for l in lands],
        input_output_aliases={a: a for a in range(n)},
        scratch_shapes=[pltpu.SemaphoreType.DMA((n, 3)), pltpu.SemaphoreType.DMA((n, 3))],
    )(*lands)


def _pack(arrs):
    parts = []
    for a in arrs:
        flat = a.reshape(-1).astype(F32)
        padded = -(-flat.shape[0] // 1024) * 1024
        parts.append(jnp.pad(flat, (0, padded - flat.shape[0])).reshape(padded // 128, 128))
    return jnp.concatenate(parts, axis=0)


def _unpack(packed, like):
    out, row = [], 0
    for a in like:
        size = math.prod(a.shape)
        rows = -(-size // 1024) * 8
        out.append(packed[row:row + rows].reshape(-1)[:size].reshape(a.shape))
        row += rows
    return out


def _sum_devices(parts, name):
    P, R, C = parts.shape
    tr = _row_tile(R, cap=512, mult=8)

    def body(p_ref, o_ref):
        g = p_ref[0]
        for q in range(1, P):
            g = g + p_ref[q]
        o_ref[...] = g

    return pl.pallas_call(
        body, name=name, grid=(R // tr,), in_specs=[pl.BlockSpec((P, tr, C), lambda i: (0, i, 0))],
        out_specs=pl.BlockSpec((tr, C), lambda i: (i, 0)), out_shape=jax.ShapeDtypeStruct((R, C), F32),
        compiler_params=_params("parallel"),
    )(parts)


def _adamw_packed(w, g, m, v, name):
    R, C = w.shape
    tr = _row_tile(R, cap=512, mult=8)

    def body(w_ref, g_ref, m_ref, v_ref, d_ref, nm_ref, nv_ref):
        d_ref[...], nm_ref[...], nv_ref[...] = _adamw_math(w_ref[...], g_ref[...], m_ref[...], v_ref[...])

    blk = pl.BlockSpec((tr, C), lambda i: (i, 0))
    sds = jax.ShapeDtypeStruct((R, C), F32)
    return pl.pallas_call(
        body, name=name, grid=(R // tr,), in_specs=[blk] * 4, out_specs=[blk] * 3, out_shape=[sds] * 3,
        compiler_params=_params("parallel"),
    )(w, g, m, v)


def _local_step(x, mem, target, sp, W, want, done):
    h0 = _rms_fwd([x], [sp["g_mix"]], "rms_mix")
    proj = _mm(h0, W["main_t"], "nt", F32, "proj_main", tm=2048, tn=512)
    f_logit = _mm(h0, W["f_t"], "nt", F32, "proj_f", tm=2048)
    (w_out, w_cq, w_ckv, w_co), tok = want("attn", f_logit)
    qn, kn, vb = _qk_fwd(proj, sp["g_q"], sp["g_k"], "qk_norm", after=tok)
    b_pad = jnp.pad(sp["b_f"], ((0, 0), (0, HEAD - FOX_H)))
    c_col, c_row = _fgate_fwd(f_logit, b_pad, "forget_cumsum")
    o_fox, lse = _fox_fwd(qn, kn, vb, c_col, c_row, "fox_fwd")
    lru_p = (sp["conv_w"], sp["conv_b"], sp["w_ra"], sp["b_ra"], sp["w_ri"], sp["b_ri"], sp["lam"])
    y_lru, h_lru = _lru_fwd(proj, *lru_p, "lru_fwd")
    mix = _rms_fwd([o_fox, y_lru], [sp["g_fox_out"], sp["g_lru_out"]], "rms_mix_out")
    x1 = _mm(mix, w_out, "nn", F32, "out_proj", add=x)
    hq = _rms_fwd([x1], [sp["g_xattn"]], "rms_xattn")
    mn = _rms_fwd([mem], [sp["g_mem"]], "rms_mem")
    cq = _mm(hq, w_cq, "nn", F32, "xattn_q", tm=2048)
    ckv = _mm(mn, w_ckv, "nn", F32, "xattn_kv")
    o_x = _xattn_fwd(cq, ckv, sp["g_cq"], sp["g_ck"], "xattn_fwd")
    co_w = w_co.shape[2]
    x2 = _mm(o_x, w_co, "nn", F32, "xattn_out", add=x1, tm=2048, tn=co_w)
    hf = _rms_fwd([x2], [sp["g_ffn"]], "rms_ffn")
    (w_gu,), tok = want("gate_up", hf)
    gu_w = w_gu.shape[2]
    gu = _mm(hf, w_gu, "nn", F32, "ffn_up", tn=gu_w, after=tok)
    act = _swiglu_fwd(gu, "swiglu_fwd")
    (w_down,), _ = want("down", act)
    x3 = _mm(act, w_down, "nn", F32, "ffn_down", add=x2, tk=w_down.shape[0] // 2)
    dy, dyb, sq = _loss_head(x3, target, "loss_head")
    dact = _mm(dyb, w_down, "nt", F32, "ffn_down_dx", tm=2048, tn=512)
    g_down = _mm(act, dyb, "tn", BF16, "ffn_down_dw", tm=w_down.shape[0] // 4, tn=2048)
    dgu = _swiglu_bwd(gu, dact, "swiglu_bwd")
    dhf = _mm(dgu, w_gu, "nt", F32, "ffn_up_dx", tn=2048, tk=gu_w)
    g_gu = _mm(hf, dgu, "tn", BF16, "ffn_up_dw", out_blocks=N_DEV, tn=gu_w)
    tok = done("ffn", [g_gu, g_down])
    dx2, dx2b, dg_ffn = _rms_bwd(x2, sp["g_ffn"], dhf, "rms_ffn_bwd", res=dy, after=tok)
    d_ox = _mm(dx2b, w_co, "nt", F32, "xattn_out_dx", tm=2048, tk=co_w)
    g_co = _mm(o_x, dx2b, "tn", BF16, "xattn_out_dw", out_blocks=N_DEV, tn=co_w)
    dcq, dckv, dg_cq, dg_ck = _xattn_bwd(cq, ckv, sp["g_cq"], sp["g_ck"], d_ox, "xattn_bwd")
    dhq = _mm(dcq, w_cq, "nt", F32, "xattn_q_dx", tm=2048)
    g_cq = _mm(hq, dcq, "tn", BF16, "xattn_q_dw")
    dmn = _mm(dckv, w_ckv, "nt", F32, "xattn_kv_dx")
    g_ckv = _mm(mn, dckv, "tn", BF16, "xattn_kv_dw")
    (dg_mem,) = _rms_bwd(mem, sp["g_mem"], dmn, "rms_mem_bwd", want_dx=False)
    dx1, dx1b, dg_xattn = _rms_bwd(x1, sp["g_xattn"], dhq, "rms_xattn_bwd", res=dx2)
    dmix = _mm(dx1b, w_out, "nt", F32, "out_proj_dx")
    g_out = _mm(mix, dx1b, "tn", BF16, "out_proj_dw", tn=2048)
    tok = done("attn", [g_out, g_cq, g_ckv, g_co])
    do_fox, _, dg_fox_out = _rms_bwd(o_fox, sp["g_fox_out"], dmix, "rms_fox_out_bwd", dy_col=0, after=tok)
    dy_lru, _, dg_lru_out = _rms_bwd(y_lru, sp["g_lru_out"], dmix, "rms_lru_out_bwd", dy_col=1)
    du, dgate, dconv_w, dconv_b, dw_ra, db_ra, dw_ri, db_ri, dlam = _lru_bwd(proj, h_lru, dy_lru, *lru_p, "lru_bwd")
    dqn, dkn, dv, dc_col, dc_row = _fox_bwd(qn, kn, vb, c_col, c_row, lse, o_fox, do_fox, "fox_bwd")
    dq, dk, dg_q, dg_k = _qk_bwd(proj, sp["g_q"], sp["g_k"], dqn, dkn, "qk_norm_bwd")
    df, db_f = _fgate_bwd(f_logit, b_pad, dc_col, dc_row, "forget_cumsum_bwd")
    dproj = jnp.concatenate([dq, dk, dv, du, dgate], axis=1)
    dh_f = _mm(df, W["f_t"], "nn", F32, "proj_f_dx", tm=2048)
    dh0 = _mm(dproj, W["main_t"], "nn", F32, "proj_main_dx", add=dh_f, tk=dproj.shape[1] // 2)
    g_main_t = _mm(dproj, h0, "tn", BF16, "proj_main_dw", tn=2048)
    g_f_t = _mm(df, h0, "tn", BF16, "proj_f_dw", tn=2048)
    grad_x, _, dg_mix = _rms_bwd(x, sp["g_mix"], dh0, "rms_mix_bwd", res=dx1)
    small = dict(g_mix=dg_mix, b_f=db_f[:, :FOX_H], g_q=dg_q, g_k=dg_k, conv_w=dconv_w, conv_b=dconv_b, w_ra=dw_ra,
                 b_ra=db_ra, w_ri=dw_ri, b_ri=db_ri, lam=dlam, g_fox_out=dg_fox_out, g_lru_out=dg_lru_out,
                 g_xattn=dg_xattn, g_mem=dg_mem, g_cq=dg_cq, g_ck=dg_ck, g_ffn=dg_ffn)
    return sq, grad_x, g_main_t, g_f_t, small


BIG = ("w_in", "w_out", "w_cq", "w_ckv", "w_co", "w_gate_up", "w_down")
SMALL = ("g_mix", "b_f", "g_q", "g_k", "conv_b", "w_ra", "b_ra", "w_ri", "b_ri", "lam", "g_fox_out", "g_lru_out",
         "g_xattn", "g_mem", "g_cq", "g_ck", "g_ffn")
ORDER = ("g_mix", "w_in", "b_f", "g_q", "g_k", "conv_w", "conv_b", "w_ra", "b_ra", "w_ri", "b_ri", "lam", "g_fox_out",
         "g_lru_out", "w_out", "g_xattn", "g_mem", "w_cq", "w_ckv", "g_cq", "g_ck", "w_co", "g_ffn", "w_gate_up", "w_down")


def kernel(x, mem, g_mix, w_in, b_f, g_q, g_k, conv_w, conv_b, w_ra, b_ra, w_ri, b_ri, lam, g_fox_out, g_lru_out, w_out, g_xattn, g_mem, w_cq, w_ckv, g_cq, g_ck, w_co, g_ffn, w_gate_up, w_down, loss_target, m_g_mix, m_w_in, m_b_f, m_g_q, m_g_k, m_conv_w, m_conv_b, m_w_ra, m_b_ra, m_w_ri, m_b_ri, m_lam, m_g_fox_out, m_g_lru_out, m_w_out, m_g_xattn, m_g_mem, m_w_cq, m_w_ckv, m_g_cq, m_g_ck, m_w_co, m_g_ffn, m_w_gate_up, m_w_down, v_g_mix, v_w_in, v_b_f, v_g_q, v_g_k, v_conv_w, v_conv_b, v_w_ra, v_b_ra, v_w_ri, v_b_ri, v_lam, v_g_fox_out, v_g_lru_out, v_w_out, v_g_xattn, v_g_mem, v_w_cq, v_w_ckv, v_g_cq, v_g_ck, v_w_co, v_g_ffn, v_w_gate_up, v_w_down):
    given = dict(locals())
    w = {n: given[n] for n in ORDER}
    m = {n: given["m_" + n] for n in ORDER}
    v = {n: given["v_" + n] for n in ORDER}
    D = x.shape[2]
    fw = FOX_H * HEAD
    dev_index = 4 * lax.axis_index("x") + 2 * lax.axis_index("y") + lax.axis_index("c")
    dev = jnp.reshape(dev_index, (1,)).astype(jnp.int32)
    core = jnp.reshape(lax.axis_index("c"), (1,)).astype(jnp.int32)
    chip = jnp.reshape(2 * lax.axis_index("x") + lax.axis_index("y"), (1,)).astype(jnp.int32)

    def shard(d, n):
        return jnp.swapaxes(d[n][0], 0, 1) if n == "w_in" else d[n][0]

    def unshard(a, n):
        return (jnp.swapaxes(a, 0, 1) if n == "w_in" else a)[None]

    gather_groups = dict(attn=("w_out", "w_cq", "w_ckv", "w_co"), gate_up=("w_gate_up",), down=("w_down",))
    gather_order = ("attn", "gate_up", "down")
    reduce_groups = dict(ffn=("w_gate_up", "w_down"), attn=("w_out", "w_cq", "w_ckv", "w_co"), w_in=("w_in",))
    column_blocked = ("w_co", "w_gate_up")
    flying = {}

    def land(a, tag):
        return _land_with_own(a, dev, "own_" + tag)

    def launch(group, after):
        shards = [shard(w, n).astype(BF16) for n in gather_groups[group]]
        lands = [land(s, n) for s, n in zip(shards, gather_groups[group])]
        flying[group] = _split_start(shards, lands, after, _gather_plan, 4, "gather_" + group + "_start")
        return flying[group][4]

    def want(group, after):
        send, recv, shards, lands, _ = flying.pop(group)
        _, lands = _split_wait(send, recv, shards, lands, after, _gather_plan, "gather_" + group + "_wait")
        full = _gather_forward(lands, "gather_" + group + "_forward")
        later = gather_order[gather_order.index(group) + 1:]
        tok = launch(later[0], full[0]) if later else None
        return [g if n in column_blocked else g.reshape(-1, g.shape[2]) for n, g in zip(gather_groups[group], full)], tok

    first = [shard(w, "w_in").astype(BF16), conv_w[0]]
    g_in, g_conv = _all_gather(first, [land(first[0], "w_in"), land(first[1], "conv_w")], "gather_w_in")
    tok = launch("attn", g_conv)
    wt_in = g_in.reshape(-1, D)
    W = dict(main_t=jnp.concatenate([wt_in[:3 * fw], wt_in[3 * fw + FOX_H:]], axis=0),
             f_t=jnp.pad(wt_in[3 * fw:3 * fw + FOX_H], ((0, HEAD - FOX_H), (0, 0))))
    sp = {n: w[n] for n in SMALL if n not in ("w_ra", "w_ri")}
    sp["w_ra"], sp["w_ri"] = w_ra[0], w_ri[0]
    sp["conv_w"] = jnp.transpose(g_conv, (1, 0, 2)).reshape(CONV_K, -1)
    sp["g_mix"] = sp["g_mix"] + tok[0, 0]

    reducing = {}

    def done(group, grads):
        names = reduce_groups[group]
        g8 = [g if g.ndim == 3 else g.reshape(N_DEV, -1, g.shape[1]) for g in grads]
        from_sibling = _pair_exchange(g8, "reduce_" + group + "_pair")
        p4 = [_pair_sum(g, r, core, "reduce_pair_sum_" + n) for g, r, n in zip(g8, from_sibling, names)]
        lands = [lax.empty(p.shape, p.dtype) for p in p4]
        reducing[group] = _split_start(p4, lands, chip, _chip_plan, 3, "reduce_" + group + "_start")
        return reducing[group][4]

    def finish(group, after):
        send, recv, p4, lands, _ = reducing.pop(group)
        p4, lands = _split_wait(send, recv, p4, lands, after, _chip_plan, "reduce_" + group + "_wait")
        return {n: tuple(unshard(r, n) for r in _reduce_adamw(p, l, chip, shard(w, n), shard(m, n), shard(v, n), "adamw_" + n))
                for n, p, l in zip(reduce_groups[group], p4, lands)}

    sq, grad_x, g_main_t, g_f_t, gs = _local_step(x[0], mem[0], loss_target[0], sp, W, want, done)
    loss = lax.psum(0.5 * sq[0, 0] / D, AXES)

    dwt_in = jnp.concatenate([g_main_t[:3 * fw], g_f_t[:FOX_H], g_main_t[3 * fw:]], axis=0)
    tok = done("w_in", [dwt_in.reshape(N_DEV, -1, D)])
    small_names = SMALL + ("conv_w",)
    mine = _pack([gs[n] for n in small_names])
    s_send, s_recv, s_src, s_land, tok = _split_start([mine], [land(mine, "small_grads")], tok, _gather_plan, 4,
                                                      "gather_small_start")
    out = finish("ffn", tok)
    out.update(finish("attn", out["w_down"][3]))
    _, s_land = _split_wait(s_send, s_recv, s_src, s_land, out["w_co"][3], _gather_plan, "gather_small_wait")
    gathered = _gather_forward(s_land, "gather_small_forward")[0]

    g_sum = _unpack(_sum_devices(gathered, "sum_small_grads"), [gs[n] for n in small_names])
    g_small = dict(zip(small_names, g_sum))
    cw = conv_w.shape[2]
    g_small["conv_w"] = lax.dynamic_slice_in_dim(g_small["conv_w"], dev_index * cw, cw, axis=1)
    g_small = {n: g_small[n].reshape(w[n].shape) for n in small_names}
    packed = [_pack([d[n] for n in small_names]) for d in (w, g_small, m, v)]
    upd = _adamw_packed(*packed, "adamw_small")
    like = [w[n] for n in small_names]
    d_s, m_s, v_s = (dict(zip(small_names, _unpack(u, like))) for u in upd)
    for n in small_names:
        out[n] = (g_small[n], d_s[n], m_s[n], v_s[n])
    out.update(finish("w_in", upd[0]))

    return (loss, grad_x[None], *[out[n][0] for n in ORDER], *[out[n][1] for n in ORDER],
            *[out[n][2] for n in ORDER], *[out[n][3] for n in ORDER])
```

```python
import functools
import math

import jax
import jax.numpy as jnp
from jax import lax
from jax.experimental import pallas as pl
from jax.experimental.pallas import tpu as pltpu

F32 = jnp.float32
BF16 = jnp.bfloat16
MESH = pl.DeviceIdType.MESH
AXES = ("x", "y", "c")
N_DEV = 8
N_CHIP = 4

HEAD = 128
FOX_H = 8
XATT_H = 4
LRU_NB = 8
CONV_K = 4
LRU_C = 8.0
RMS_EPS = 1e-6
ATT_T = 1024
CUM_T = 256
ROW_T = 256
NEG = -1e30
V7X_VMEM_LIMIT = 48 * 1024 * 1024

ADAM_LR = 0.001
ADAM_B1 = 0.9
ADAM_B2 = 0.999
ADAM_EPS = 1e-08
ADAM_WD = 0.01
ADAM_STEP = 10

NT = (((1,), (1,)), ((), ()))
TN = (((0,), (0,)), ((), ()))
NN = (((1,), (0,)), ((), ()))

ANY = pl.BlockSpec(memory_space=pl.ANY)


def _behind(after):
    return ([], []) if after is None else ([ANY], [after])


def _params(*sem):
    return pltpu.CompilerParams(dimension_semantics=sem or None, vmem_limit_bytes=V7X_VMEM_LIMIT)


def _dot(a, b, dn=NN):
    return lax.dot_general(a, b, dn, preferred_element_type=F32)


def _rms(x, g):
    r = lax.rsqrt(jnp.mean(x * x, axis=-1, keepdims=True) + RMS_EPS)
    return x * r * g


def _rms_grad(x, g, dy):
    r = lax.rsqrt(jnp.mean(x * x, axis=-1, keepdims=True) + RMS_EPS)
    xh = x * r
    dxh = dy * g
    dx = r * (dxh - xh * jnp.mean(dxh * xh, axis=-1, keepdims=True))
    return dx, jnp.sum(dy * xh, axis=0, keepdims=True)


def _gelu(x):
    k = math.sqrt(2.0 / math.pi)
    return 0.5 * x * (1.0 + jnp.tanh(k * (x + 0.044715 * (x * x * x))))


def _gelu_grad(x):
    k = math.sqrt(2.0 / math.pi)
    t = jnp.tanh(k * (x + 0.044715 * (x * x * x)))
    return 0.5 * (1.0 + t) + 0.5 * x * (1.0 - t * t) * k * (1.0 + 3.0 * 0.044715 * x * x)


def _one_minus_exp(z):
    series = -(z + 0.5 * z * z + (1.0 / 6.0) * z * z * z)
    return jnp.where(z > -1e-3, series, 1.0 - jnp.exp(z))


def _row_tile(rows, cap=ROW_T, mult=16):
    t = min(rows, cap)
    while rows % t or (t % mult and t != rows):
        t -= 1
    return t


def _mat_dims(z):
    return (z.shape[-2], z.shape[-1] * (z.shape[0] if z.ndim == 3 else 1))


def _mat_spec(arr, rblk, cblk, rsel, csel):
    if arr.ndim == 2:
        return pl.BlockSpec((rblk, cblk), lambda i, j, k: ((i, j, k)[rsel], (i, j, k)[csel]))
    nw = arr.shape[2]
    assert nw % cblk == 0, (arr.shape, cblk)
    per = nw // cblk
    return pl.BlockSpec((None, rblk, cblk),
                        lambda i, j, k: ((i, j, k)[csel] // per, (i, j, k)[rsel], (i, j, k)[csel] % per))


def _mm(a, b, mode, out_dtype, name, add=None, out_blocks=None, tm=1024, tn=1024, tk=None, after=None):
    ar, ac = _mat_dims(a)
    br, bc = _mat_dims(b)
    if mode == "nn":
        M, K, N = ar, ac, bc
        assert br == K
    elif mode == "nt":
        M, K, N = ar, ac, br
        assert bc == K
    else:
        M, K, N = ac, ar, bc
        assert br == K
    tm, tn = min(tm, M), min(tn, N)
    tk = K if tk is None or mode == "tn" else min(tk, K)
    assert M % tm == 0 and N % tn == 0 and K % tk == 0, (name, M, N, K, tm, tn, tk)
    nk = K // tk
    nj = N // tn
    if mode == "nn":
        specs = [_mat_spec(a, tm, tk, 0, 2), _mat_spec(b, tk, tn, 2, 1)]
        dn = NN
    elif mode == "nt":
        specs = [_mat_spec(a, tm, tk, 0, 2), _mat_spec(b, tn, tk, 1, 2)]
        dn = NT
    else:
        specs = [_mat_spec(a, tk, tm, 2, 0), _mat_spec(b, tk, tn, 2, 1)]
        dn = TN
    args = [a, b]
    if add is not None:
        specs.append(pl.BlockSpec((tm, tn), lambda i, j, k: (i, j)))
        args.append(add)
    specs += _behind(after)[0]
    args += _behind(after)[1]
    n_in = len(args)
    if out_blocks is None:
        out_shape = jax.ShapeDtypeStruct((M, N), out_dtype)
        out_spec = pl.BlockSpec((tm, tn), lambda i, j, k: (i, j))
    else:
        nb = out_blocks
        nw = N // nb
        assert nw % tn == 0
        per = nw // tn
        out_shape = jax.ShapeDtypeStruct((nb, M, nw), out_dtype)
        out_spec = pl.BlockSpec((None, tm, tn), lambda i, j, k: (j // per, i, j % per))
    keep_t = mode == "tn" and nj > 1
    scratch = []
    if nk > 1:
        scratch.append(pltpu.VMEM((tm, tn), F32))
    if keep_t:
        scratch.append(pltpu.VMEM((tm, tk), a.dtype))

    def body(*refs):
        a_ref, b_ref = refs[0], refs[1]
        add_ref = refs[2] if add is not None else None
        o_ref = refs[n_in]

        def finish(r):
            if add_ref is not None:
                r = r + add_ref[...]
            o_ref[...] = r.astype(out_dtype)

        if keep_t:
            at_ref = refs[-1]

            @pl.when(pl.program_id(1) == 0)
            def _():
                at_ref[...] = a_ref[...].T

            finish(_dot(at_ref[...], b_ref[...], NN))
        elif nk == 1:
            finish(_dot(a_ref[...], b_ref[...], dn))
        else:
            acc_ref = refs[n_in + 1]
            k = pl.program_id(2)

            @pl.when(k == 0)
            def _():
                acc_ref[...] = jnp.zeros_like(acc_ref)

            acc_ref[...] += _dot(a_ref[...], b_ref[...], dn)

            @pl.when(k == nk - 1)
            def _():
                finish(acc_ref[...])

    return pl.pallas_call(
        body, name=name, grid=(M // tm, nj, nk), in_specs=specs, out_specs=out_spec, out_shape=out_shape,
        scratch_shapes=scratch,
        compiler_params=_params("parallel", "arbitrary" if keep_t else "parallel", "arbitrary"),
    )(*args)


def _rms_fwd(xs, gs, name, after=None):
    n = len(xs)
    rows = xs[0].shape[0]
    widths = [x.shape[1] for x in xs]
    tr = _row_tile(rows)
    more_specs, more = _behind(after)

    def body(*refs):
        o_ref = refs[2 * n + len(more)]
        off = 0
        for t in range(n):
            o_ref[:, off:off + widths[t]] = _rms(refs[t][...], refs[n + t][...]).astype(BF16)
            off += widths[t]

    return pl.pallas_call(
        body, name=name, grid=(rows // tr,),
        in_specs=[pl.BlockSpec((tr, w), lambda i: (i, 0)) for w in widths]
        + [pl.BlockSpec((1, w), lambda i: (0, 0)) for w in widths] + more_specs,
        out_specs=pl.BlockSpec((tr, sum(widths)), lambda i: (i, 0)),
        out_shape=jax.ShapeDtypeStruct((rows, sum(widths)), BF16),
        compiler_params=_params("parallel"),
    )(*xs, *gs, *more)


def _rms_bwd(x, g, dy, name, dy_col=0, res=None, want_dx=True, after=None):
    rows, w = x.shape
    tr = _row_tile(rows)
    has_res = res is not None
    more_specs, more = _behind(after)

    def body(*refs):
        x_ref, g_ref, dy_ref = refs[:3]
        res_ref = refs[3] if has_res else None
        outs = refs[3 + has_res + len(more):]
        dg_ref = outs[-1]
        dx, dg = _rms_grad(x_ref[...], g_ref[...], dy_ref[...])

        @pl.when(pl.program_id(0) == 0)
        def _():
            dg_ref[...] = jnp.zeros_like(dg_ref)

        dg_ref[...] += dg
        if want_dx:
            if has_res:
                dx = dx + res_ref[...]
            outs[0][...] = dx
            outs[1][...] = dx.astype(BF16)

    row_spec = pl.BlockSpec((tr, w), lambda i: (i, 0))
    in_specs = [row_spec, pl.BlockSpec((1, w), lambda i: (0, 0)), pl.BlockSpec((tr, w), lambda i: (i, dy_col))]
    args = [x, g, dy]
    if has_res:
        in_specs.append(row_spec)
        args.append(res)
    in_specs += more_specs
    args += more
    out_specs, out_shape = [], []
    if want_dx:
        out_specs += [row_spec, row_spec]
        out_shape += [jax.ShapeDtypeStruct((rows, w), F32), jax.ShapeDtypeStruct((rows, w), BF16)]
    out_specs.append(pl.BlockSpec((1, w), lambda i: (0, 0)))
    out_shape.append(jax.ShapeDtypeStruct((1, w), F32))
    return pl.pallas_call(
        body, name=name, grid=(rows // tr,), in_specs=in_specs, out_specs=out_specs, out_shape=out_shape,
        compiler_params=_params("arbitrary"),
    )(*args)


def _loss_head(y, target, name):
    rows, w = y.shape
    tr = _row_tile(rows)

    def body(y_ref, t_ref, dy_ref, dyb_ref, acc_ref):
        e = y_ref[...] - t_ref[...]

        @pl.when(pl.program_id(0) == 0)
        def _():
            acc_ref[...] = jnp.zeros_like(acc_ref)

        acc_ref[...] += jnp.sum(e * e)
        dy = e * (1.0 / w)
        dy_ref[...] = dy
        dyb_ref[...] = dy.astype(BF16)

    row_spec = pl.BlockSpec((tr, w), lambda i: (i, 0))
    return pl.pallas_call(
        body, name=name, grid=(rows // tr,), in_specs=[row_spec, row_spec],
        out_specs=[row_spec, row_spec, pl.BlockSpec((8, 128), lambda i: (0, 0))],
        out_shape=[jax.ShapeDtypeStruct((rows, w), F32), jax.ShapeDtypeStruct((rows, w), BF16),
                   jax.ShapeDtypeStruct((8, 128), F32)],
        compiler_params=_params("arbitrary"),
    )(y, target)


def _qk_fwd(proj, g_q, g_k, name, after=None):
    rows = proj.shape[0]
    w = FOX_H * HEAD
    tr = _row_tile(rows)
    more_specs, more = _behind(after)

    def body(q_ref, k_ref, v_ref, gq_ref, gk_ref, *rest):
        qn_ref, kn_ref, vb_ref = rest[len(more):]
        for h in range(FOX_H):
            sl = slice(h * HEAD, (h + 1) * HEAD)
            qn_ref[:, sl] = _rms(q_ref[:, sl], gq_ref[...]).astype(BF16)
            kn_ref[:, sl] = _rms(k_ref[:, sl], gk_ref[...]).astype(BF16)
        vb_ref[...] = v_ref[...].astype(BF16)

    gain = pl.BlockSpec((1, HEAD), lambda i: (0, 0))
    out = pl.BlockSpec((tr, w), lambda i: (i, 0))
    return pl.pallas_call(
        body, name=name, grid=(rows // tr,),
        in_specs=[pl.BlockSpec((tr, w), lambda i: (i, 0)), pl.BlockSpec((tr, w), lambda i: (i, 1)),
                  pl.BlockSpec((tr, w), lambda i: (i, 2)), gain, gain] + more_specs,
        out_specs=[out, out, out], out_shape=[jax.ShapeDtypeStruct((rows, w), BF16)] * 3,
        compiler_params=_params("parallel"),
    )(proj, proj, proj, g_q, g_k, *more)


def _qk_bwd(proj, g_q, g_k, dqn, dkn, name):
    rows = proj.shape[0]
    w = FOX_H * HEAD
    tr = _row_tile(rows)

    def body(q_ref, k_ref, gq_ref, gk_ref, dqn_ref, dkn_ref, dq_ref, dk_ref, dgq_ref, dgk_ref):
        @pl.when(pl.program_id(0) == 0)
        def _():
            dgq_ref[...] = jnp.zeros_like(dgq_ref)
            dgk_ref[...] = jnp.zeros_like(dgk_ref)

        for h in range(FOX_H):
            sl = slice(h * HEAD, (h + 1) * HEAD)
            dq, dgq = _rms_grad(q_ref[:, sl], gq_ref[...], dqn_ref[:, sl])
            dk, dgk = _rms_grad(k_ref[:, sl], gk_ref[...], dkn_ref[:, sl])
            dq_ref[:, sl] = dq.astype(BF16)
            dk_ref[:, sl] = dk.astype(BF16)
            dgq_ref[...] += dgq
            dgk_ref[...] += dgk

    gain = pl.BlockSpec((1, HEAD), lambda i: (0, 0))
    row = pl.BlockSpec((tr, w), lambda i: (i, 0))
    return pl.pallas_call(
        body, name=name, grid=(rows // tr,),
        in_specs=[row, pl.BlockSpec((tr, w), lambda i: (i, 1)), gain, gain, row, row],
        out_specs=[row, row, gain, gain],
        out_shape=[jax.ShapeDtypeStruct((rows, w), BF16)] * 2 + [jax.ShapeDtypeStruct((1, HEAD), F32)] * 2,
        compiler_params=_params("arbitrary"),
    )(proj, proj, g_q, g_k, dqn, dkn)


def _fgate_fwd(f_logit, b_pad, name):
    S = f_logit.shape[0]
    T = min(CUM_T, S)
    nb = S // T
    RT = min(ATT_T, S)
    nbr = S // RT

    def body(f_ref, b_ref, col_ref, row_ref, c_scr):
        tri = (lax.broadcasted_iota(jnp.int32, (T, T), 0) >= lax.broadcasted_iota(jnp.int32, (T, T), 1)).astype(F32)
        carry = jnp.zeros((1, HEAD), F32)
        for blk in range(nb):
            z = f_ref[blk * T:(blk + 1) * T, :] + b_ref[...]
            lf = jnp.minimum(z, 0.0) - jnp.log1p(jnp.exp(-jnp.abs(z)))
            cb = jnp.dot(tri, lf, precision=lax.Precision.HIGHEST, preferred_element_type=F32) + carry
            c_scr[blk * T:(blk + 1) * T, :] = cb
            carry = cb[T - 1:T, :]
        c = c_scr[...]
        lane = lax.broadcasted_iota(jnp.int32, c.shape, 1)
        ct = c.T
        for h in range(FOX_H):
            col_ref[h] = jnp.sum(jnp.where(lane == h, c, 0.0), axis=1, keepdims=True)
            for jb in range(nbr):
                row_ref[h, jb] = ct[h:h + 1, jb * RT:(jb + 1) * RT]

    return pl.pallas_call(
        body, name=name,
        out_shape=[jax.ShapeDtypeStruct((FOX_H, S, 1), F32), jax.ShapeDtypeStruct((FOX_H, nbr, 1, RT), F32)],
        scratch_shapes=[pltpu.VMEM((S, HEAD), F32)], compiler_params=_params(),
    )(f_logit, b_pad)


def _fgate_bwd(f_logit, b_pad, dc_col, dc_row, name):
    S = f_logit.shape[0]
    T = min(CUM_T, S)
    nb = S // T
    RT = min(ATT_T, S)

    def body(f_ref, b_ref, dcol_ref, drow_ref, df_ref, db_ref, dc_scr, rt_scr):
        lane = lax.broadcasted_iota(jnp.int32, (S, HEAD), 1)
        sub = lax.broadcasted_iota(jnp.int32, (HEAD, RT), 0)
        dc = jnp.zeros((S, HEAD), F32)
        for h in range(FOX_H):
            dc = jnp.where(lane == h, dcol_ref[h], dc)
        for jb in range(S // RT):
            rt = jnp.zeros((HEAD, RT), F32)
            for h in range(FOX_H):
                rt = jnp.where(sub == h, drow_ref[h, jb], rt)
            rt_scr[jb * RT:(jb + 1) * RT, :] = rt.T
        dc_scr[...] = dc - rt_scr[...]
        tri = (lax.broadcasted_iota(jnp.int32, (T, T), 0) <= lax.broadcasted_iota(jnp.int32, (T, T), 1)).astype(F32)
        carry = jnp.zeros((1, HEAD), F32)
        db = jnp.zeros((1, HEAD), F32)
        for blk in reversed(range(nb)):
            rows = slice(blk * T, (blk + 1) * T)
            dlf = jnp.dot(tri, dc_scr[rows, :], precision=lax.Precision.HIGHEST, preferred_element_type=F32) + carry
            carry = dlf[0:1, :]
            z = f_ref[rows, :] + b_ref[...]
            df = dlf * jax.nn.sigmoid(-z)
            df_ref[rows, :] = df.astype(BF16)
            db = db + jnp.sum(df, axis=0, keepdims=True)
        db_ref[...] = db

    return pl.pallas_call(
        body, name=name,
        out_shape=[jax.ShapeDtypeStruct((S, HEAD), BF16), jax.ShapeDtypeStruct((1, HEAD), F32)],
        scratch_shapes=[pltpu.VMEM((S, HEAD), F32), pltpu.VMEM((S, HEAD), F32)], compiler_params=_params(),
    )(f_logit, b_pad, dc_col, dc_row)


def _fox_fwd(qn, kn, vb, c_col, c_row, name, after=None):
    S = qn.shape[0]
    T = min(ATT_T, S)
    nb = S // T
    scale = 1.0 / math.sqrt(HEAD)
    more_specs, more = _behind(after)

    def body(q_ref, k_ref, v_ref, cc_ref, cr_ref, *rest):
        o_ref, lse_ref = rest[len(more):]
        i = pl.program_id(1)
        q = q_ref[...]
        cc = cc_ref[...]

        def step(j, carry, diagonal=False):
            m, l, acc = carry
            sl = pl.ds(pl.multiple_of(j * T, T), T)
            s = _dot(q, k_ref[sl, :], NT) * scale + cc - cr_ref[j]
            if diagonal:
                s = jnp.where(lax.broadcasted_iota(jnp.int32, (T, T), 0) >= lax.broadcasted_iota(jnp.int32, (T, T), 1), s, NEG)
            m2 = jnp.maximum(m, jnp.max(s, axis=1, keepdims=True))
            p = jnp.exp(s - m2)
            al = jnp.exp(m - m2)
            return m2, al * l + jnp.sum(p, axis=1, keepdims=True), al * acc + _dot(p.astype(BF16), v_ref[sl, :])

        init = (jnp.full((T, 1), NEG, F32), jnp.zeros((T, 1), F32), jnp.zeros((T, HEAD), F32))
        m, l, acc = step(i, lax.fori_loop(0, i, step, init), diagonal=True)
        o_ref[...] = acc / l
        lse_ref[...] = m + jnp.log(l)

    head = pl.BlockSpec((S, HEAD), lambda h, i: (0, h))
    return pl.pallas_call(
        body, name=name, grid=(FOX_H, nb),
        in_specs=[pl.BlockSpec((T, HEAD), lambda h, i: (i, h)), head, head,
                  pl.BlockSpec((None, T, 1), lambda h, i: (h, i, 0)),
                  pl.BlockSpec((None, nb, 1, T), lambda h, i: (h, 0, 0, 0))] + more_specs,
        out_specs=[pl.BlockSpec((T, HEAD), lambda h, i: (i, h)), pl.BlockSpec((None, T, 1), lambda h, i: (h, i, 0))],
        out_shape=[jax.ShapeDtypeStruct((S, FOX_H * HEAD), F32), jax.ShapeDtypeStruct((FOX_H, S, 1), F32)],
        compiler_params=_params("parallel", "parallel"),
    )(qn, kn, vb, c_col, c_row, *more)


def _fox_bwd(qn, kn, vb, c_col, c_row, lse, o, do, name):
    S = qn.shape[0]
    T = min(ATT_T, S)
    nb = S // T
    scale = 1.0 / math.sqrt(HEAD)

    def body(q_ref, k_ref, v_ref, cc_ref, cr_ref, lse_ref, o_ref, do_ref, dq_ref, dk_ref, dv_ref, dcc_ref, dcr_ref):
        j = pl.program_id(1)

        @pl.when(j == 0)
        def _():
            dq_ref[...] = jnp.zeros_like(dq_ref)
            dcc_ref[...] = jnp.zeros_like(dcc_ref)

        k = k_ref[...]
        v = v_ref[...]
        cr = cr_ref[...]

        def step(i, carry, diagonal=False):
            dk, dv, dcr = carry
            sl = pl.ds(pl.multiple_of(i * T, T), T)
            q = q_ref[sl, :]
            d_o = do_ref[sl, :]
            s = _dot(q, k, NT) * scale + cc_ref[sl, :] - cr
            if diagonal:
                s = jnp.where(lax.broadcasted_iota(jnp.int32, (T, T), 0) >= lax.broadcasted_iota(jnp.int32, (T, T), 1), s, NEG)
            p = jnp.exp(s - lse_ref[sl, :])
            dob = d_o.astype(BF16)
            dp = _dot(dob, v, NT)
            delta = jnp.sum(d_o * o_ref[sl, :], axis=1, keepdims=True)
            ds = p * (dp - delta)
            dsb = ds.astype(BF16)
            dq_ref[sl, :] += _dot(dsb, k) * scale
            dcc_ref[sl, :] += jnp.sum(ds, axis=1, keepdims=True)
            return (dk + _dot(dsb, q, TN), dv + _dot(p.astype(BF16), dob, TN), dcr + jnp.sum(ds, axis=0, keepdims=True))

        init = (jnp.zeros((T, HEAD), F32), jnp.zeros((T, HEAD), F32), jnp.zeros((1, T), F32))
        dk, dv, dcr = lax.fori_loop(j + 1, nb, step, step(j, init, diagonal=True))
        dk_ref[...] = dk * scale
        dv_ref[...] = dv.astype(BF16)
        dcr_ref[...] = dcr

    head = pl.BlockSpec((S, HEAD), lambda h, j: (0, h))
    tile = pl.BlockSpec((T, HEAD), lambda h, j: (j, h))
    col = pl.BlockSpec((None, S, 1), lambda h, j: (h, 0, 0))
    row = pl.BlockSpec((None, None, 1, T), lambda h, j: (h, j, 0, 0))
    w = FOX_H * HEAD
    return pl.pallas_call(
        body, name=name, grid=(FOX_H, nb),
        in_specs=[head, tile, tile, col, row, col, head, head],
        out_specs=[head, tile, tile, col, row],
        out_shape=[jax.ShapeDtypeStruct((S, w), F32), jax.ShapeDtypeStruct((S, w), F32), jax.ShapeDtypeStruct((S, w), BF16),
                   jax.ShapeDtypeStruct((FOX_H, S, 1), F32), jax.ShapeDtypeStruct((FOX_H, nb, 1, T), F32)],
        compiler_params=_params("arbitrary", "arbitrary"),
    )(qn, kn, vb, c_col, c_row, lse, o, do)


def _lru_gates(uc, wra, bra, wri, bri, lam):
    ucb = uc.astype(BF16)
    r = jax.nn.sigmoid(_dot(ucb, wra.astype(BF16)) + bra)
    ig = jax.nn.sigmoid(_dot(ucb, wri.astype(BF16)) + bri)
    sp = jnp.maximum(-lam, 0.0) + jnp.log1p(jnp.exp(-jnp.abs(lam)))
    log_a = -LRU_C * r * sp
    a = jnp.exp(log_a)
    mult = jnp.sqrt(_one_minus_exp(2.0 * log_a))
    return r, ig, sp, a, mult


def _conv(pad_ref, cw, cb, S):
    uc = cb
    for j in range(CONV_K):
        uc = uc + cw[j:j + 1, :] * pad_ref[5 + j:5 + j + S, :]
    return uc


def _lru_specs(S, n_proj_cols):
    u_col = 3 * FOX_H
    g_col = u_col + LRU_NB
    blk = pl.BlockSpec((S, HEAD), lambda n: (0, n))
    vec = pl.BlockSpec((1, HEAD), lambda n: (0, n))
    mat = pl.BlockSpec((None, HEAD, HEAD), lambda n: (n, 0, 0))
    return dict(
        u=pl.BlockSpec((S, HEAD), lambda n: (0, u_col + n)), gate=pl.BlockSpec((S, HEAD), lambda n: (0, g_col + n)),
        blk=blk, vec=vec, mat=mat, cw=pl.BlockSpec((CONV_K, HEAD), lambda n: (0, n)))


def _lru_fwd(proj, conv_w, conv_b, w_ra, b_ra, w_ri, b_ri, lam, name):
    S = proj.shape[0]
    sp_ = _lru_specs(S, proj.shape[1])
    rows8 = S // 8

    def body(u_ref, gt_ref, cw_ref, cb_ref, wra_ref, bra_ref, wri_ref, bri_ref, lam_ref, y_ref, h_ref, pad, a_scr, b_scr):
        pad[0:8, :] = jnp.zeros((8, HEAD), F32)
        pad[8:S + 8, :] = u_ref[...]
        uc = _conv(pad, cw_ref[...], cb_ref[...], S)
        r, ig, sp, a, mult = _lru_gates(uc, wra_ref[...], bra_ref[...], wri_ref[...], bri_ref[...], lam_ref[...])
        a_scr[...] = a
        b_scr[...] = mult * (ig * uc)
        sub = lax.broadcasted_iota(jnp.int32, (8, HEAD), 0)

        def step(t, carry):
            sl = pl.ds(pl.multiple_of(t * 8, 8), 8)
            A, B = a_scr[sl, :], b_scr[sl, :]
            for d in (1, 2, 4):
                m = sub >= d
                B = jnp.where(m, A * pltpu.roll(B, d, 0) + B, B)
                A = jnp.where(m, A * pltpu.roll(A, d, 0), A)
            h = A * carry + B
            h_ref[sl, :] = h
            return h[7:8, :]

        lax.fori_loop(0, rows8, step, jnp.zeros((1, HEAD), F32))
        y_ref[...] = h_ref[...] * _gelu(gt_ref[...])

    w = LRU_NB * HEAD
    return pl.pallas_call(
        body, name=name, grid=(LRU_NB,),
        in_specs=[sp_["u"], sp_["gate"], sp_["cw"], sp_["vec"], sp_["mat"], sp_["vec"], sp_["mat"], sp_["vec"], sp_["vec"]],
        out_specs=[sp_["blk"], sp_["blk"]],
        out_shape=[jax.ShapeDtypeStruct((S, w), F32)] * 2,
        scratch_shapes=[pltpu.VMEM((S + 8, HEAD), F32), pltpu.VMEM((S, HEAD), F32), pltpu.VMEM((S, HEAD), F32)],
        compiler_params=_params("parallel"),
    )(proj, proj, conv_w, conv_b, w_ra, b_ra, w_ri, b_ri, lam)


def _lru_bwd(proj, h, dy, conv_w, conv_b, w_ra, b_ra, w_ri, b_ri, lam, name):
    S = proj.shape[0]
    sp_ = _lru_specs(S, proj.shape[1])
    rows8 = S // 8

    def body(u_ref, gt_ref, h_ref, dy_ref, cw_ref, cb_ref, wra_ref, bra_ref, wri_ref, bri_ref, lam_ref,
             du_ref, dgt_ref, dcw_ref, dcb_ref, dwra_ref, dbra_ref, dwri_ref, dbri_ref, dlam_ref,
             pad, an_scr, d_scr, g_scr, hp_scr):
        zero8 = jnp.zeros((8, HEAD), F32)
        pad[0:8, :] = zero8
        pad[8:S + 8, :] = u_ref[...]
        cw = cw_ref[...]
        uc = _conv(pad, cw, cb_ref[...], S)
        wra, wri, lam_v = wra_ref[...], wri_ref[...], lam_ref[...]
        r, ig, sp, a, mult = _lru_gates(uc, wra, bra_ref[...], wri, bri_ref[...], lam_v)
        gate = gt_ref[...]
        dy_v = dy_ref[...]
        hv = h_ref[...]
        dgt_ref[...] = (dy_v * hv * _gelu_grad(gate)).astype(BF16)
        d_scr[...] = dy_v * _gelu(gate)
        g_scr[0:S, :] = a
        g_scr[S:S + 8, :] = zero8
        an_scr[...] = g_scr[1:S + 1, :]
        sub = lax.broadcasted_iota(jnp.int32, (8, HEAD), 0)

        def step(t, carry):
            sl = pl.ds(pl.multiple_of((rows8 - 1 - t) * 8, 8), 8)
            A, D = an_scr[sl, :], d_scr[sl, :]
            for d in (1, 2, 4):
                m = sub + d <= 7
                D = jnp.where(m, A * pltpu.roll(D, 8 - d, 0) + D, D)
                A = jnp.where(m, A * pltpu.roll(A, 8 - d, 0), A)
            g = A * carry + D
            g_scr[sl, :] = g
            return g[0:1, :]

        lax.fori_loop(0, rows8, step, jnp.zeros((1, HEAD), F32))
        g = g_scr[0:S, :]
        hp_scr[0:8, :] = zero8
        hp_scr[8:S + 8, :] = hv
        da = g * hp_scr[7:S + 7, :]
        iu = ig * uc
        dmult = g * iu
        diu = g * mult
        dig = diu * uc
        duc = diu * ig
        dlog_a = da * a - dmult * (a * a) / mult
        dr = dlog_a * (-LRU_C * sp)
        dsp = jnp.sum(dlog_a * (-LRU_C * r), axis=0, keepdims=True)
        dlam_ref[...] = -dsp * jax.nn.sigmoid(-lam_v)
        dpr = dr * r * (1.0 - r)
        dpi = dig * ig * (1.0 - ig)
        dbra_ref[...] = jnp.sum(dpr, axis=0, keepdims=True)
        dbri_ref[...] = jnp.sum(dpi, axis=0, keepdims=True)
        ucb = uc.astype(BF16)
        dprb, dpib = dpr.astype(BF16), dpi.astype(BF16)
        dwra_ref[...] = _dot(ucb, dprb, TN)
        dwri_ref[...] = _dot(ucb, dpib, TN)
        duc = duc + _dot(dprb, wra.astype(BF16), NT) + _dot(dpib, wri.astype(BF16), NT)
        dcb_ref[...] = jnp.sum(duc, axis=0, keepdims=True)
        for j in range(CONV_K):
            dcw_ref[j:j + 1, :] = jnp.sum(duc * pad[5 + j:5 + j + S, :], axis=0, keepdims=True)
        g_scr[0:S, :] = duc
        g_scr[S:S + 8, :] = zero8
        du = jnp.zeros((S, HEAD), F32)
        for j in range(CONV_K):
            du = du + cw[j:j + 1, :] * g_scr[3 - j:3 - j + S, :]
        du_ref[...] = du.astype(BF16)

    w = LRU_NB * HEAD
    bf = jax.ShapeDtypeStruct((S, w), BF16)
    vec = jax.ShapeDtypeStruct((1, w), F32)
    mat = jax.ShapeDtypeStruct((LRU_NB, HEAD, HEAD), F32)
    return pl.pallas_call(
        body, name=name, grid=(LRU_NB,),
        in_specs=[sp_["u"], sp_["gate"], sp_["blk"], sp_["blk"], sp_["cw"], sp_["vec"], sp_["mat"], sp_["vec"],
                  sp_["mat"], sp_["vec"], sp_["vec"]],
        out_specs=[sp_["blk"], sp_["blk"], sp_["cw"], sp_["vec"], sp_["mat"], sp_["vec"], sp_["mat"], sp_["vec"], sp_["vec"]],
        out_shape=[bf, bf, jax.ShapeDtypeStruct((CONV_K, w), F32), vec, mat, vec, mat, vec, vec],
        scratch_shapes=[pltpu.VMEM((S + 8, HEAD), F32), pltpu.VMEM((S, HEAD), F32), pltpu.VMEM((S, HEAD), F32),
                        pltpu.VMEM((S + 8, HEAD), F32), pltpu.VMEM((S + 8, HEAD), F32)],
        compiler_params=_params("parallel"),
    )(proj, proj, h, dy, conv_w, conv_b, w_ra, b_ra, w_ri, b_ri, lam)


def _xattn_fwd(cq, ckv, g_cq, g_ck, name):
    S, w = cq.shape
    M = ckv.shape[0]
    tr = _row_tile(S)
    scale = 1.0 / math.sqrt(HEAD)

    def body(cq_ref, ckv_ref, gq_ref, gk_ref, o_ref):
        for h in range(XATT_H):
            sl = slice(h * HEAD, (h + 1) * HEAD)
            qn = _rms(cq_ref[:, sl], gq_ref[...]).astype(BF16)
            kn = _rms(ckv_ref[:, sl], gk_ref[...]).astype(BF16)
            v = ckv_ref[:, w + h * HEAD:w + (h + 1) * HEAD].astype(BF16)
            s = _dot(qn, kn, NT) * scale
            p = jnp.exp(s - jnp.max(s, axis=1, keepdims=True))
            p = p / jnp.sum(p, axis=1, keepdims=True)
            o_ref[:, sl] = _dot(p.astype(BF16), v).astype(BF16)

    gain = pl.BlockSpec((1, HEAD), lambda i: (0, 0))
    return pl.pallas_call(
        body, name=name, grid=(S // tr,),
        in_specs=[pl.BlockSpec((tr, w), lambda i: (i, 0)), pl.BlockSpec((M, 2 * w), lambda i: (0, 0)), gain, gain],
        out_specs=pl.BlockSpec((tr, w), lambda i: (i, 0)), out_shape=jax.ShapeDtypeStruct((S, w), BF16),
        compiler_params=_params("parallel"),
    )(cq, ckv, g_cq, g_ck)


def _xattn_bwd(cq, ckv, g_cq, g_ck, do, name):
    S, w = cq.shape
    M = ckv.shape[0]
    tr = _row_tile(S)
    nsteps = S // tr
    scale = 1.0 / math.sqrt(HEAD)

    def body(cq_ref, ckv_ref, gq_ref, gk_ref, do_ref, dcq_ref, dckv_ref, dgq_ref, dgk_ref, dkn_scr, dv_scr):
        step = pl.program_id(0)

        @pl.when(step == 0)
        def _():
            dkn_scr[...] = jnp.zeros_like(dkn_scr)
            dv_scr[...] = jnp.zeros_like(dv_scr)
            dgq_ref[...] = jnp.zeros_like(dgq_ref)

        for h in range(XATT_H):
            sl = slice(h * HEAD, (h + 1) * HEAD)
            q_raw = cq_ref[:, sl]
            qn = _rms(q_raw, gq_ref[...]).astype(BF16)
            kn = _rms(ckv_ref[:, sl], gk_ref[...]).astype(BF16)
            v = ckv_ref[:, w + h * HEAD:w + (h + 1) * HEAD].astype(BF16)
            s = _dot(qn, kn, NT) * scale
            p = jnp.exp(s - jnp.max(s, axis=1, keepdims=True))
            p = p / jnp.sum(p, axis=1, keepdims=True)
            dob = do_ref[:, sl].astype(BF16)
            dp = _dot(dob, v, NT)
            ds = p * (dp - jnp.sum(p * dp, axis=1, keepdims=True)) * scale
            dsb = ds.astype(BF16)
            dv_scr[:, sl] += _dot(p.astype(BF16), dob, TN)
            dkn_scr[:, sl] += _dot(dsb, qn, TN)
            dq, dgq = _rms_grad(q_raw, gq_ref[...], _dot(dsb, kn))
            dcq_ref[:, sl] = dq.astype(BF16)
            dgq_ref[...] += dgq

        @pl.when(step == nsteps - 1)
        def _():
            dgk = jnp.zeros((1, HEAD), F32)
            for h in range(XATT_H):
                sl = slice(h * HEAD, (h + 1) * HEAD)
                dk, dgk_h = _rms_grad(ckv_ref[:, sl], gk_ref[...], dkn_scr[:, sl])
                dckv_ref[:, sl] = dk.astype(BF16)
                dgk = dgk + dgk_h
            dckv_ref[:, w:2 * w] = dv_scr[...].astype(BF16)
            dgk_ref[...] = dgk

    gain = pl.BlockSpec((1, HEAD), lambda i: (0, 0))
    row = pl.BlockSpec((tr, w), lambda i: (i, 0))
    mem = pl.BlockSpec((M, 2 * w), lambda i: (0, 0))
    return pl.pallas_call(
        body, name=name, grid=(nsteps,), in_specs=[row, mem, gain, gain, row], out_specs=[row, mem, gain, gain],
        out_shape=[jax.ShapeDtypeStruct((S, w), BF16), jax.ShapeDtypeStruct((M, 2 * w), BF16),
                   jax.ShapeDtypeStruct((1, HEAD), F32), jax.ShapeDtypeStruct((1, HEAD), F32)],
        scratch_shapes=[pltpu.VMEM((M, w), F32), pltpu.VMEM((M, w), F32)],
        compiler_params=_params("arbitrary"),
    )(cq, ckv, g_cq, g_ck, do)


def _swiglu_fwd(gu, name, n_ct=4):
    S, w2 = gu.shape
    f = w2 // 2
    tc = f // n_ct
    tr = _row_tile(S)

    def body(g_ref, u_ref, o_ref):
        g = g_ref[...]
        o_ref[...] = (g * jax.nn.sigmoid(g) * u_ref[...]).astype(BF16)

    return pl.pallas_call(
        body, name=name, grid=(S // tr, n_ct),
        in_specs=[pl.BlockSpec((tr, tc), lambda i, j: (i, j)), pl.BlockSpec((tr, tc), lambda i, j: (i, j + n_ct))],
        out_specs=pl.BlockSpec((tr, tc), lambda i, j: (i, j)), out_shape=jax.ShapeDtypeStruct((S, f), BF16),
        compiler_params=_params("parallel", "parallel"),
    )(gu, gu)


def _swiglu_bwd(gu, dact, name, n_ct=4):
    S, w2 = gu.shape
    f = w2 // 2
    tc = f // n_ct
    tr = _row_tile(S)

    steps = (S // tr) * n_ct

    def body(g_ref, u_ref, da_ref, o_hbm, buf, sems):
        i, j = pl.program_id(0), pl.program_id(1)
        step = i * n_ct + j
        slot = step % 2

        def put(half):
            cols = pl.ds(pl.multiple_of(half * f + j * tc, 128), tc)
            return pltpu.make_async_copy(buf.at[slot, half], o_hbm.at[pl.ds(pl.multiple_of(i * tr, 8), tr), cols],
                                         sems.at[slot, half])

        @pl.when(step >= 2)
        def _():
            put(0).wait()
            put(1).wait()

        g = g_ref[...]
        sg = jax.nn.sigmoid(g)
        da = da_ref[...]
        buf[slot, 0] = (da * u_ref[...] * (sg * (1.0 + g * (1.0 - sg)))).astype(BF16)
        buf[slot, 1] = (da * (g * sg)).astype(BF16)
        put(0).start()
        put(1).start()

        @pl.when(step == steps - 1)
        def _():
            put(0).wait()
            put(1).wait()
            if steps > 1:
                for half in (0, 1):
                    pltpu.make_async_copy(buf.at[1 - slot, half], o_hbm.at[pl.ds(0, tr), pl.ds(0, tc)],
                                          sems.at[1 - slot, half]).wait()

    return pl.pallas_call(
        body, name=name, grid=(S // tr, n_ct),
        in_specs=[pl.BlockSpec((tr, tc), lambda i, j: (i, j)), pl.BlockSpec((tr, tc), lambda i, j: (i, j + n_ct)),
                  pl.BlockSpec((tr, tc), lambda i, j: (i, j))],
        out_specs=ANY, out_shape=jax.ShapeDtypeStruct((S, w2), BF16),
        scratch_shapes=[pltpu.VMEM((2, 2, tr, tc), BF16), pltpu.SemaphoreType.DMA((2, 2))],
        compiler_params=_params("arbitrary", "arbitrary"),
    )(gu, gu, dact)


def _adamw_math(w, g, m, v):
    m = ADAM_B1 * m + (1.0 - ADAM_B1) * g
    v = ADAM_B2 * v + (1.0 - ADAM_B2) * (g * g)
    m_hat = m / (1.0 - ADAM_B1 ** ADAM_STEP)
    v_hat = v / (1.0 - ADAM_B2 ** ADAM_STEP)
    delta = -ADAM_LR * (m_hat / (jnp.sqrt(v_hat) + ADAM_EPS) + ADAM_WD * w)
    return delta, m, v


def _tile2(R, C, elems):
    if R % 16 == 0:
        return _row_tile(R, cap=max(16, elems // C // 16 * 16)), C
    tc = 128
    while C % (2 * tc) == 0 and R * 2 * tc <= elems:
        tc *= 2
    return R, tc


def _reduce_adamw(own, land, chip, w, m, v, name):
    _, R, C = own.shape
    tr, tc = _tile2(R, C, 1 << 18)

    def body(s_ref, o_ref, l1_ref, l2_ref, l3_ref, w_ref, m_ref, v_ref, g_ref, d_ref, nm_ref, nv_ref):
        g = o_ref[...].astype(F32) + l1_ref[...].astype(F32) + l2_ref[...].astype(F32) + l3_ref[...].astype(F32)
        g_ref[...] = g
        d_ref[...], nm_ref[...], nv_ref[...] = _adamw_math(w_ref[...], g, m_ref[...], v_ref[...])

    def part(flip):
        return pl.BlockSpec((None, tr, tc), lambda i, j, s: (s[0] ^ flip, i, j))

    blk = pl.BlockSpec((tr, tc), lambda i, j, s: (i, j))
    sds = jax.ShapeDtypeStruct((R, C), F32)
    return pl.pallas_call(
        body, name=name,
        grid_spec=pltpu.PrefetchScalarGridSpec(
            num_scalar_prefetch=1, grid=(R // tr, C // tc),
            in_specs=[part(0), part(1), part(2), part(3), blk, blk, blk], out_specs=[blk] * 4),
        out_shape=[sds] * 4, compiler_params=_params("parallel", "parallel"),
    )(chip, own, land, land, land, w, m, v)


def _pair_sum(g8, recv, core, name):
    _, R, C = g8.shape
    tr, tc = _tile2(R, C, 1 << 19)

    def body(c_ref, a_ref, b_ref, o_ref):
        o_ref[...] = (a_ref[...].astype(F32) + b_ref[...].astype(F32)).astype(BF16)

    return pl.pallas_call(
        body, name=name,
        grid_spec=pltpu.PrefetchScalarGridSpec(
            num_scalar_prefetch=1, grid=(N_CHIP, R // tr, C // tc),
            in_specs=[pl.BlockSpec((None, tr, tc), lambda q, i, j, c: (2 * q + c[0], i, j)),
                      pl.BlockSpec((None, tr, tc), lambda q, i, j, c: (q, i, j))],
            out_specs=pl.BlockSpec((None, tr, tc), lambda q, i, j, c: (q, i, j))),
        out_shape=jax.ShapeDtypeStruct((N_CHIP, R, C), BF16),
        compiler_params=_params("parallel", "parallel", "parallel"),
    )(core, g8, recv)


def _place():
    return lax.axis_index("x"), lax.axis_index("y"), lax.axis_index("c")


def _land_with_own(shard, dev, name):
    R, C = shard.shape
    tr, tc = _tile2(R, C, 1 << 19)

    def body(d_ref, s_ref, o_ref):
        o_ref[...] = s_ref[...]

    return pl.pallas_call(
        body, name=name,
        grid_spec=pltpu.PrefetchScalarGridSpec(
            num_scalar_prefetch=1, grid=(R // tr, C // tc),
            in_specs=[pl.BlockSpec((tr, tc), lambda i, j, d: (i, j))],
            out_specs=pl.BlockSpec((None, tr, tc), lambda i, j, d: (d[0], i, j))),
        out_shape=jax.ShapeDtypeStruct((N_DEV, R, C), shard.dtype),
        compiler_params=_params("parallel", "parallel"),
    )(dev, shard)


def _all_gather(shards, lands, name):
    n = len(shards)

    def body(*refs):
        ins, outs = refs[:n], refs[2 * n:3 * n]
        send_sems, recv_sems = refs[3 * n:]
        x, y, c = _place()
        me, sibling = (x, y, c), (x, y, 1 - c)
        chips = [(1 - x, y), (x, 1 - y), (1 - x, 1 - y)]

        def copy(a, k, block, to, src=None):
            dst = outs[a].at[4 * block[0] + 2 * block[1] + block[2]]
            return pltpu.make_async_remote_copy(
                src_ref=dst if src is None else src, dst_ref=dst, send_sem=send_sems.at[a, k],
                recv_sem=recv_sems.at[a, k], device_id=to, device_id_type=MESH)

        sent = []
        for a in range(n):
            sent.append(copy(a, 0, me, sibling, src=ins[a]))
            sent += [copy(a, 1 + j, me, (*chip, c), src=ins[a]) for j, chip in enumerate(chips)]
        for cp in sent:
            cp.start()
        for j, chip in enumerate(chips):
            for a in range(n):
                copy(a, 1 + j, (*chip, c), me).wait_recv()
                fwd = copy(a, 4 + j, (*chip, c), sibling)
                fwd.start()
                sent.append(fwd)
        for a in range(n):
            copy(a, 0, sibling, me).wait_recv()
        for j, chip in enumerate(chips):
            for a in range(n):
                copy(a, 4 + j, (*chip, 1 - c), me).wait_recv()
        for cp in sent:
            cp.wait_send()

    return pl.pallas_call(
        body, name=name, in_specs=[ANY] * (2 * n), out_specs=[ANY] * n,
        out_shape=[jax.ShapeDtypeStruct(l.shape, l.dtype) for l in lands],
        input_output_aliases={n + a: a for a in range(n)},
        scratch_shapes=[pltpu.SemaphoreType.DMA((n, 7)), pltpu.SemaphoreType.DMA((n, 7))],
    )(*shards, *lands)


def _pair_exchange(g8s, name):
    n = len(g8s)

    def body(*refs):
        ins, outs = refs[:n], refs[n:2 * n]
        send_sems, recv_sems = refs[2 * n:]
        x, y, c = _place()
        copies = []
        for a in range(n):
            for q in range(N_CHIP):
                copies.append(pltpu.make_async_remote_copy(
                    src_ref=ins[a].at[2 * q + 1 - c], dst_ref=outs[a].at[q], send_sem=send_sems.at[a, q],
                    recv_sem=recv_sems.at[a, q], device_id=(x, y, 1 - c), device_id_type=MESH))
        for cp in copies:
            cp.start()
        for cp in copies:
            cp.wait()

    return pl.pallas_call(
        body, name=name, in_specs=[ANY] * n, out_specs=[ANY] * n,
        out_shape=[jax.ShapeDtypeStruct((N_CHIP,) + g.shape[1:], g.dtype) for g in g8s],
        scratch_shapes=[pltpu.SemaphoreType.DMA((n, N_CHIP)), pltpu.SemaphoreType.DMA((n, N_CHIP))],
    )(*g8s)


def _chip_plan(srcs, lands):
    x, y, c = _place()
    mine = 2 * x + y
    plan = []
    for a in range(len(srcs)):
        for px, py in [(1 - x, y), (x, 1 - y), (1 - x, 1 - y)]:
            peer = 2 * px + py
            plan.append((srcs[a].at[peer], lands[a].at[mine], lands[a].at[peer], (px, py, c)))
    return plan


def _gather_plan(srcs, lands):
    x, y, c = _place()
    mine = 4 * x + 2 * y + c
    plan = []
    for a in range(len(srcs)):
        for px, py, pc in [(x, y, 1 - c), (1 - x, y, c), (x, 1 - y, c), (1 - x, 1 - y, c)]:
            plan.append((srcs[a], lands[a].at[mine], lands[a].at[4 * px + 2 * py + pc], (px, py, pc)))
    return plan


def _remote(src, dst, send_sem, recv_sem, peer):
    return pltpu.make_async_remote_copy(src_ref=src, dst_ref=dst, send_sem=send_sem, recv_sem=recv_sem,
                                        device_id=peer, device_id_type=MESH)


HBM = pl.BlockSpec(memory_space=pltpu.HBM)
SEMS = pl.BlockSpec(memory_space=pltpu.SEMAPHORE)
DATAFLOW = pltpu.SideEffectType.DATAFLOW_SIDE_EFFECTING


def _split_start(srcs, lands, after, plan_fn, per_array, name):
    n = len(srcs)
    ncopy = per_array * n

    def body(*refs):
        send_sems, recv_sems = refs[2 * n + 1], refs[2 * n + 2]
        token = refs[-1]
        for k, (src, dst, _, peer) in enumerate(plan_fn(refs[:n], refs[n:2 * n])):
            _remote(src, dst, send_sems.at[k], recv_sems.at[k], peer).start()
        token[...] = jnp.zeros_like(token)

    thru = [pltpu.HBM(a.shape, a.dtype) for a in (*srcs, *lands)]
    out = pl.pallas_call(
        body, name=name,
        out_shape=(pltpu.SemaphoreType.DMA((ncopy,)), pltpu.SemaphoreType.DMA((ncopy,)), *thru,
                   jax.ShapeDtypeStruct((8, 128), F32)),
        in_specs=[HBM] * (2 * n) + [ANY], out_specs=(SEMS, SEMS, *[HBM] * (2 * n), pl.BlockSpec(memory_space=pltpu.VMEM)),
        input_output_aliases={i: 2 + i for i in range(2 * n)},
        compiler_params=pltpu.CompilerParams(has_side_effects=DATAFLOW),
    )(*[pltpu.with_memory_space_constraint(a, pltpu.HBM) for a in (*srcs, *lands)], after)
    return out[0], out[1], list(out[2:2 + n]), list(out[2 + n:2 + 2 * n]), out[-1]


def _split_wait(send_sems, recv_sems, srcs, lands, after, plan_fn, name):
    n = len(srcs)

    def body(*refs):
        send_ref, recv_ref = refs[2 * n], refs[2 * n + 1]
        for k, (src, dst, mine, peer) in enumerate(plan_fn(refs[:n], refs[n:2 * n])):
            _remote(src, dst, send_ref.at[k], recv_ref.at[k], peer).wait_send()
            _remote(src, mine, send_ref.at[k], recv_ref.at[k], peer).wait_recv()

    thru = [pltpu.HBM(a.shape, a.dtype) for a in (*srcs, *lands)]
    out = pl.pallas_call(
        body, name=name, out_shape=tuple(thru),
        in_specs=[HBM] * (2 * n) + [SEMS, SEMS, ANY], out_specs=tuple([HBM] * (2 * n)),
        input_output_aliases={i: i for i in range(2 * n)},
        compiler_params=pltpu.CompilerParams(has_side_effects=DATAFLOW),
    )(*srcs, *lands, send_sems, recv_sems, after)
    return list(out[:n]), list(out[n:])


def _gather_forward(lands, name):
    n = len(lands)

    def body(*refs):
        landed, out = refs[:n], refs[n:2 * n]
        send_sems, recv_sems = refs[2 * n:]
        x, y, c = _place()
        chips = [(1 - x, y), (x, 1 - y), (1 - x, 1 - y)]
        sent = []
        for a in range(n):
            for j, (px, py) in enumerate(chips):
                blk = 4 * px + 2 * py + c
                sent.append(_remote(landed[a].at[blk], out[a].at[blk], send_sems.at[a, j], recv_sems.at[a, j], (x, y, 1 - c)))
        for cp in sent:
            cp.start()
        for a in range(n):
            for j, (px, py) in enumerate(chips):
                blk = 4 * px + 2 * py + 1 - c
                _remote(landed[a].at[blk], out[a].at[blk], send_sems.at[a, j], recv_sems.at[a, j], (x, y, 1 - c)).wait_recv()
        for cp in sent:
            cp.wait_send()

    return pl.pallas_call(
        body, name=name, in_specs=[ANY] * n, out_specs=[ANY] * n,
        out_shape=[jax.ShapeDtypeStruct(l.shape, l.dtype) for l in lands],
        input_output_aliases={a: a for a in range(n)},
        scratch_shapes=[pltpu.SemaphoreType.DMA((n, 3)), pltpu.SemaphoreType.DMA((n, 3))],
    )(*lands)


def _pack(arrs):
    parts = []
    for a in arrs:
        flat = a.reshape(-1).astype(F32)
        padded = -(-flat.shape[0] // 1024) * 1024
        parts.append(jnp.pad(flat, (0, padded - flat.shape[0])).reshape(padded // 128, 128))
    return jnp.concatenate(parts, axis=0)


def _unpack(packed, like):
    out, row = [], 0
    for a in like:
        size = math.prod(a.shape)
        rows = -(-size // 1024) * 8
        out.append(packed[row:row + rows].reshape(-1)[:size].reshape(a.shape))
        row += rows
    return out


def _sum_devices(parts, name):
    P, R, C = parts.shape
    tr = _row_tile(R, cap=512, mult=8)

    def body(p_ref, o_ref):
        g = p_ref[0]
        for q in range(1, P):
            g = g + p_ref[q]
        o_ref[...] = g

    return pl.pallas_call(
        body, name=name, grid=(R // tr,), in_specs=[pl.BlockSpec((P, tr, C), lambda i: (0, i, 0))],
        out_specs=pl.BlockSpec((tr, C), lambda i: (i, 0)), out_shape=jax.ShapeDtypeStruct((R, C), F32),
        compiler_params=_params("parallel"),
    )(parts)


def _adamw_packed(w, g, m, v, name):
    R, C = w.shape
    tr = _row_tile(R, cap=512, mult=8)

    def body(w_ref, g_ref, m_ref, v_ref, d_ref, nm_ref, nv_ref):
        d_ref[...], nm_ref[...], nv_ref[...] = _adamw_math(w_ref[...], g_ref[...], m_ref[...], v_ref[...])

    blk = pl.BlockSpec((tr, C), lambda i: (i, 0))
    sds = jax.ShapeDtypeStruct((R, C), F32)
    return pl.pallas_call(
        body, name=name, grid=(R // tr,), in_specs=[blk] * 4, out_specs=[blk] * 3, out_shape=[sds] * 3,
        compiler_params=_params("parallel"),
    )(w, g, m, v)


def _local_step(x, mem, target, sp, W, want, done):
    h0 = _rms_fwd([x], [sp["g_mix"]], "rms_mix")
    proj = _mm(h0, W["main_t"], "nt", F32, "proj_main", tm=2048, tn=512)
    f_logit = _mm(h0, W["f_t"], "nt", F32, "proj_f", tm=2048)
    qn, kn, vb = _qk_fwd(proj, sp["g_q"], sp["g_k"], "qk_norm")
    b_pad = jnp.pad(sp["b_f"], ((0, 0), (0, HEAD - FOX_H)))
    c_col, c_row = _fgate_fwd(f_logit, b_pad, "forget_cumsum")
    (w_out, w_cq, w_ckv, w_co), tok = want("attn", c_row)
    o_fox, lse = _fox_fwd(qn, kn, vb, c_col, c_row, "fox_fwd", after=tok)
    lru_p = (sp["conv_w"], sp["conv_b"], sp["w_ra"], sp["b_ra"], sp["w_ri"], sp["b_ri"], sp["lam"])
    y_lru, h_lru = _lru_fwd(proj, *lru_p, "lru_fwd")
    mix = _rms_fwd([o_fox, y_lru], [sp["g_fox_out"], sp["g_lru_out"]], "rms_mix_out")
    x1 = _mm(mix, w_out, "nn", F32, "out_proj", add=x)
    hq = _rms_fwd([x1], [sp["g_xattn"]], "rms_xattn")
    mn = _rms_fwd([mem], [sp["g_mem"]], "rms_mem")
    cq = _mm(hq, w_cq, "nn", F32, "xattn_q", tm=2048)
    ckv = _mm(mn, w_ckv, "nn", F32, "xattn_kv")
    o_x = _xattn_fwd(cq, ckv, sp["g_cq"], sp["g_ck"], "xattn_fwd")
    co_w = w_co.shape[2]
    x2 = _mm(o_x, w_co, "nn", F32, "xattn_out", add=x1, tm=2048, tn=co_w)
    hf = _rms_fwd([x2], [sp["g_ffn"]], "rms_ffn")
    (w_gu,), tok = want("gate_up", hf)
    gu_w = w_gu.shape[2]
    gu = _mm(hf, w_gu, "nn", F32, "ffn_up", tn=gu_w, after=tok)
    act = _swiglu_fwd(gu, "swiglu_fwd")
    (w_down,), _ = want("down", act)
    x3 = _mm(act, w_down, "nn", F32, "ffn_down", add=x2, tk=w_down.shape[0] // 2)
    dy, dyb, sq = _loss_head(x3, target, "loss_head")
    dact = _mm(dyb, w_down, "nt", F32, "ffn_down_dx", tm=2048, tn=512)
    g_down = _mm(act, dyb, "tn", BF16, "ffn_down_dw", tm=w_down.shape[0] // 4, tn=2048)
    dgu = _swiglu_bwd(gu, dact, "swiglu_bwd")
    dhf = _mm(dgu, w_gu, "nt", F32, "ffn_up_dx", tn=2048, tk=gu_w)
    g_gu = _mm(hf, dgu, "tn", BF16, "ffn_up_dw", out_blocks=N_DEV, tn=gu_w)
    tok = done("ffn", [g_gu, g_down])
    dx2, dx2b, dg_ffn = _rms_bwd(x2, sp["g_ffn"], dhf, "rms_ffn_bwd", res=dy, after=tok)
    d_ox = _mm(dx2b, w_co, "nt", F32, "xattn_out_dx", tm=2048, tk=co_w)
    g_co = _mm(o_x, dx2b, "tn", BF16, "xattn_out_dw", out_blocks=N_DEV, tn=co_w)
    dcq, dckv, dg_cq, dg_ck = _xattn_bwd(cq, ckv, sp["g_cq"], sp["g_ck"], d_ox, "xattn_bwd")
    dhq = _mm(dcq, w_cq, "nt", F32, "xattn_q_dx", tm=2048)
    g_cq = _mm(hq, dcq, "tn", BF16, "xattn_q_dw")
    dmn = _mm(dckv, w_ckv, "nt", F32, "xattn_kv_dx")
    g_ckv = _mm(mn, dckv, "tn", BF16, "xattn_kv_dw")
    (dg_mem,) = _rms_bwd(mem, sp["g_mem"], dmn, "rms_mem_bwd", want_dx=False)
    dx1, dx1b, dg_xattn = _rms_bwd(x1, sp["g_xattn"], dhq, "rms_xattn_bwd", res=dx2)
    dmix = _mm(dx1b, w_out, "nt", F32, "out_proj_dx")
    g_out = _mm(mix, dx1b, "tn", BF16, "out_proj_dw", tn=2048)
    tok = done("attn", [g_out, g_cq, g_ckv, g_co])
    do_fox, _, dg_fox_out = _rms_bwd(o_fox, sp["g_fox_out"], dmix, "rms_fox_out_bwd", dy_col=0, after=tok)
    dy_lru, _, dg_lru_out = _rms_bwd(y_lru, sp["g_lru_out"], dmix, "rms_lru_out_bwd", dy_col=1)
    du, dgate, dconv_w, dconv_b, dw_ra, db_ra, dw_ri, db_ri, dlam = _lru_bwd(proj, h_lru, dy_lru, *lru_p, "lru_bwd")
    dqn, dkn, dv, dc_col, dc_row = _fox_bwd(qn, kn, vb, c_col, c_row, lse, o_fox, do_fox, "fox_bwd")
    dq, dk, dg_q, dg_k = _qk_bwd(proj, sp["g_q"], sp["g_k"], dqn, dkn, "qk_norm_bwd")
    df, db_f = _fgate_bwd(f_logit, b_pad, dc_col, dc_row, "forget_cumsum_bwd")
    dproj = jnp.concatenate([dq, dk, dv, du, dgate], axis=1)
    dh_f = _mm(df, W["f_t"], "nn", F32, "proj_f_dx", tm=2048)
    dh0 = _mm(dproj, W["main_t"], "nn", F32, "proj_main_dx", add=dh_f, tk=dproj.shape[1] // 2)
    g_main_t = _mm(dproj, h0, "tn", BF16, "proj_main_dw", tn=2048)
    g_f_t = _mm(df, h0, "tn", BF16, "proj_f_dw", tn=2048)
    grad_x, _, dg_mix = _rms_bwd(x, sp["g_mix"], dh0, "rms_mix_bwd", res=dx1)
    small = dict(g_mix=dg_mix, b_f=db_f[:, :FOX_H], g_q=dg_q, g_k=dg_k, conv_w=dconv_w, conv_b=dconv_b, w_ra=dw_ra,
                 b_ra=db_ra, w_ri=dw_ri, b_ri=db_ri, lam=dlam, g_fox_out=dg_fox_out, g_lru_out=dg_lru_out,
                 g_xattn=dg_xattn, g_mem=dg_mem, g_cq=dg_cq, g_ck=dg_ck, g_ffn=dg_ffn)
    return sq, grad_x, g_main_t, g_f_t, small


BIG = ("w_in", "w_out", "w_cq", "w_ckv", "w_co", "w_gate_up", "w_down")
SMALL = ("g_mix", "b_f", "g_q", "g_k", "conv_b", "w_ra", "b_ra", "w_ri", "b_ri", "lam", "g_fox_out", "g_lru_out",
         "g_xattn", "g_mem", "g_cq", "g_ck", "g_ffn")
ORDER = ("g_mix", "w_in", "b_f", "g_q", "g_k", "conv_w", "conv_b", "w_ra", "b_ra", "w_ri", "b_ri", "lam", "g_fox_out",
         "g_lru_out", "w_out", "g_xattn", "g_mem", "w_cq", "w_ckv", "g_cq", "g_ck", "w_co", "g_ffn", "w_gate_up", "w_down")


def kernel(x, mem, g_mix, w_in, b_f, g_q, g_k, conv_w, conv_b, w_ra, b_ra, w_ri, b_ri, lam, g_fox_out, g_lru_out, w_out, g_xattn, g_mem, w_cq, w_ckv, g_cq, g_ck, w_co, g_ffn, w_gate_up, w_down, loss_target, m_g_mix, m_w_in, m_b_f, m_g_q, m_g_k, m_conv_w, m_conv_b, m_w_ra, m_b_ra, m_w_ri, m_b_ri, m_lam, m_g_fox_out, m_g_lru_out, m_w_out, m_g_xattn, m_g_mem, m_w_cq, m_w_ckv, m_g_cq, m_g_ck, m_w_co, m_g_ffn, m_w_gate_up, m_w_down, v_g_mix, v_w_in, v_b_f, v_g_q, v_g_k, v_conv_w, v_conv_b, v_w_ra, v_b_ra, v_w_ri, v_b_ri, v_lam, v_g_fox_out, v_g_lru_out, v_w_out, v_g_xattn, v_g_mem, v_w_cq, v_w_ckv, v_g_cq, v_g_ck, v_w_co, v_g_ffn, v_w_gate_up, v_w_down):
    given = dict(locals())
    w = {n: given[n] for n in ORDER}
    m = {n: given["m_" + n] for n in ORDER}
    v = {n: given["v_" + n] for n in ORDER}
    D = x.shape[2]
    fw = FOX_H * HEAD
    dev_index = 4 * lax.axis_index("x") + 2 * lax.axis_index("y") + lax.axis_index("c")
    dev = jnp.reshape(dev_index, (1,)).astype(jnp.int32)
    core = jnp.reshape(lax.axis_index("c"), (1,)).astype(jnp.int32)
    chip = jnp.reshape(2 * lax.axis_index("x") + lax.axis_index("y"), (1,)).astype(jnp.int32)

    def shard(d, n):
        return jnp.swapaxes(d[n][0], 0, 1) if n == "w_in" else d[n][0]

    def unshard(a, n):
        return (jnp.swapaxes(a, 0, 1) if n == "w_in" else a)[None]

    gather_groups = dict(attn=("w_out", "w_cq", "w_ckv", "w_co"), gate_up=("w_gate_up",), down=("w_down",))
    gather_order = ("attn", "gate_up", "down")
    reduce_groups = dict(ffn=("w_gate_up", "w_down"), attn=("w_out", "w_cq", "w_ckv", "w_co"), w_in=("w_in",))
    column_blocked = ("w_co", "w_gate_up")
    flying = {}

    def land(a, tag):
        return _land_with_own(a, dev, "own_" + tag)

    def launch(group, after):
        shards = [shard(w, n).astype(BF16) for n in gather_groups[group]]
        lands = [land(s, n) for s, n in zip(shards, gather_groups[group])]
        flying[group] = _split_start(shards, lands, after, _gather_plan, 4, "gather_" + group + "_start")
        return flying[group][4]

    def want(group, after):
        send, recv, shards, lands, _ = flying.pop(group)
        _, lands = _split_wait(send, recv, shards, lands, after, _gather_plan, "gather_" + group + "_wait")
        full = _gather_forward(lands, "gather_" + group + "_forward")
        later = gather_order[gather_order.index(group) + 1:]
        tok = launch(later[0], full[0]) if later else None
        return [g if n in column_blocked else g.reshape(-1, g.shape[2]) for n, g in zip(gather_groups[group], full)], tok

    first = [shard(w, "w_in").astype(BF16), conv_w[0]]
    g_in, g_conv = _all_gather(first, [land(first[0], "w_in"), land(first[1], "conv_w")], "gather_w_in")
    tok = launch("attn", g_conv)
    wt_in = g_in.reshape(-1, D)
    W = dict(main_t=jnp.concatenate([wt_in[:3 * fw], wt_in[3 * fw + FOX_H:]], axis=0),
             f_t=jnp.pad(wt_in[3 * fw:3 * fw + FOX_H], ((0, HEAD - FOX_H), (0, 0))))
    sp = {n: w[n] for n in SMALL if n not in ("w_ra", "w_ri")}
    sp["w_ra"], sp["w_ri"] = w_ra[0], w_ri[0]
    sp["conv_w"] = jnp.transpose(g_conv, (1, 0, 2)).reshape(CONV_K, -1)
    sp["g_mix"] = sp["g_mix"] + tok[0, 0]

    reducing = {}

    def done(group, grads):
        names = reduce_groups[group]
        g8 = [g if g.ndim == 3 else g.reshape(N_DEV, -1, g.shape[1]) for g in grads]
        from_sibling = _pair_exchange(g8, "reduce_" + group + "_pair")
        p4 = [_pair_sum(g, r, core, "reduce_pair_sum_" + n) for g, r, n in zip(g8, from_sibling, names)]
        lands = [lax.empty(p.shape, p.dtype) for p in p4]
        reducing[group] = _split_start(p4, lands, chip, _chip_plan, 3, "reduce_" + group + "_start")
        return reducing[group][4]

    def finish(group, after):
        send, recv, p4, lands, _ = reducing.pop(group)
        p4, lands = _split_wait(send, recv, p4, lands, after, _chip_plan, "reduce_" + group + "_wait")
        return {n: tuple(unshard(r, n) for r in _reduce_adamw(p, l, chip, shard(w, n), shard(m, n), shard(v, n), "adamw_" + n))
                for n, p, l in zip(reduce_groups[group], p4, lands)}

    sq, grad_x, g_main_t, g_f_t, gs = _local_step(x[0], mem[0], loss_target[0], sp, W, want, done)
    loss = lax.psum(0.5 * sq[0, 0] / D, AXES)

    dwt_in = jnp.concatenate([g_main_t[:3 * fw], g_f_t[:FOX_H], g_main_t[3 * fw:]], axis=0)
    tok = done("w_in", [dwt_in.reshape(N_DEV, -1, D)])
    small_names = SMALL + ("conv_w",)
    mine = _pack([gs[n] for n in small_names])
    s_send, s_recv, s_src, s_land, tok = _split_start([mine], [land(mine, "small_grads")], tok, _gather_plan, 4,
                                                      "gather_small_start")
    out = finish("ffn", tok)
    out.update(finish("attn", out["w_down"][3]))
    _, s_land = _split_wait(s_send, s_recv, s_src, s_land, out["w_co"][3], _gather_plan, "gather_small_wait")
    gathered = _gather_forward(s_land, "gather_small_forward")[0]

    g_sum = _unpack(_sum_devices(gathered, "sum_small_grads"), [gs[n] for n in small_names])
    g_small = dict(zip(small_names, g_sum))
    cw = conv_w.shape[2]
    g_small["conv_w"] = lax.dynamic_slice_in_dim(g_small["conv_w"], dev_index * cw, cw, axis=1)
    g_small = {n: g_small[n].reshape(w[n].shape) for n in small_names}
    packed = [_pack([d[n] for n in small_names]) for d in (w, g_small, m, v)]
    upd = _adamw_packed(*packed, "adamw_small")
    like = [w[n] for n in small_names]
    d_s, m_s, v_s = (dict(zip(small_names, _unpack(u, like))) for u in upd)
    for n in small_names:
        out[n] = (g_small[n], d_s[n], m_s[n], v_s[n])
    out.update(finish("w_in", upd[0]))

    return (loss, grad_x[None], *[out[n][0] for n in ORDER], *[out[n][1] for n in ORDER],
            *[out[n][2] for n in ORDER], *[out[n][3] for n in ORDER])
```

```python
import functools
import math

import jax
import jax.numpy as jnp
from jax import lax
from jax.experimental import pallas as pl
from jax.experimental.pallas import tpu as pltpu

F32 = jnp.float32
BF16 = jnp.bfloat16
MESH = pl.DeviceIdType.MESH
AXES = ("x", "y", "c")
N_DEV = 8
N_CHIP = 4

HEAD = 128
FOX_H = 8
XATT_H = 4
LRU_NB = 8
CONV_K = 4
LRU_C = 8.0
RMS_EPS = 1e-6
ATT_T = 1024
CUM_T = 256
SCAN_UNROLL = 8
ROW_T = 256
NEG = -1e30
V7X_VMEM_LIMIT = 48 * 1024 * 1024

ADAM_LR = 0.001
ADAM_B1 = 0.9
ADAM_B2 = 0.999
ADAM_EPS = 1e-08
ADAM_WD = 0.01
ADAM_STEP = 10

NT = (((1,), (1,)), ((), ()))
TN = (((0,), (0,)), ((), ()))
NN = (((1,), (0,)), ((), ()))

ANY = pl.BlockSpec(memory_space=pl.ANY)


def _behind(after):
    return ([], []) if after is None else ([ANY], [after])


def _params(*sem):
    return pltpu.CompilerParams(dimension_semantics=sem or None, vmem_limit_bytes=V7X_VMEM_LIMIT)


def _dot(a, b, dn=NN):
    return lax.dot_general(a, b, dn, preferred_element_type=F32)


def _rms(x, g):
    r = lax.rsqrt(jnp.mean(x * x, axis=-1, keepdims=True) + RMS_EPS)
    return x * r * g


def _rms_grad(x, g, dy):
    r = lax.rsqrt(jnp.mean(x * x, axis=-1, keepdims=True) + RMS_EPS)
    xh = x * r
    dxh = dy * g
    dx = r * (dxh - xh * jnp.mean(dxh * xh, axis=-1, keepdims=True))
    return dx, jnp.sum(dy * xh, axis=0, keepdims=True)


def _gelu(x):
    k = math.sqrt(2.0 / math.pi)
    return 0.5 * x * (1.0 + jnp.tanh(k * (x + 0.044715 * (x * x * x))))


def _gelu_grad(x):
    k = math.sqrt(2.0 / math.pi)
    t = jnp.tanh(k * (x + 0.044715 * (x * x * x)))
    return 0.5 * (1.0 + t) + 0.5 * x * (1.0 - t * t) * k * (1.0 + 3.0 * 0.044715 * x * x)


def _one_minus_exp(z):
    series = -(z + 0.5 * z * z + (1.0 / 6.0) * z * z * z)
    return jnp.where(z > -1e-3, series, 1.0 - jnp.exp(z))


def _row_tile(rows, cap=ROW_T, mult=16):
    t = min(rows, cap)
    while rows % t or (t % mult and t != rows):
        t -= 1
    return t


def _mat_dims(z):
    return (z.shape[-2], z.shape[-1] * (z.shape[0] if z.ndim == 3 else 1))


def _mat_spec(arr, rblk, cblk, rsel, csel):
    if arr.ndim == 2:
        return pl.BlockSpec((rblk, cblk), lambda i, j, k: ((i, j, k)[rsel], (i, j, k)[csel]))
    nw = arr.shape[2]
    assert nw % cblk == 0, (arr.shape, cblk)
    per = nw // cblk
    return pl.BlockSpec((None, rblk, cblk),
                        lambda i, j, k: ((i, j, k)[csel] // per, (i, j, k)[rsel], (i, j, k)[csel] % per))


def _mm(a, b, mode, out_dtype, name, add=None, out_blocks=None, tm=1024, tn=1024, tk=None, after=None):
    ar, ac = _mat_dims(a)
    br, bc = _mat_dims(b)
    if mode == "nn":
        M, K, N = ar, ac, bc
        assert br == K
    elif mode == "nt":
        M, K, N = ar, ac, br
        assert bc == K
    else:
        M, K, N = ac, ar, bc
        assert br == K
    tm, tn = min(tm, M), min(tn, N)
    tk = K if tk is None or mode == "tn" else min(tk, K)
    assert M % tm == 0 and N % tn == 0 and K % tk == 0, (name, M, N, K, tm, tn, tk)
    nk = K // tk
    nj = N // tn
    if mode == "nn":
        specs = [_mat_spec(a, tm, tk, 0, 2), _mat_spec(b, tk, tn, 2, 1)]
        dn = NN
    elif mode == "nt":
        specs = [_mat_spec(a, tm, tk, 0, 2), _mat_spec(b, tn, tk, 1, 2)]
        dn = NT
    else:
        specs = [_mat_spec(a, tk, tm, 2, 0), _mat_spec(b, tk, tn, 2, 1)]
        dn = TN
    args = [a, b]
    if add is not None:
        specs.append(pl.BlockSpec((tm, tn), lambda i, j, k: (i, j)))
        args.append(add)
    specs += _behind(after)[0]
    args += _behind(after)[1]
    n_in = len(args)
    if out_blocks is None:
        out_shape = jax.ShapeDtypeStruct((M, N), out_dtype)
        out_spec = pl.BlockSpec((tm, tn), lambda i, j, k: (i, j))
    else:
        nb = out_blocks
        nw = N // nb
        assert nw % tn == 0
        per = nw // tn
        out_shape = jax.ShapeDtypeStruct((nb, M, nw), out_dtype)
        out_spec = pl.BlockSpec((None, tm, tn), lambda i, j, k: (j // per, i, j % per))
    keep_t = mode == "tn" and nj > 1
    scratch = []
    if nk > 1:
        scratch.append(pltpu.VMEM((tm, tn), F32))
    if keep_t:
        scratch.append(pltpu.VMEM((tm, tk), a.dtype))

    def body(*refs):
        a_ref, b_ref = refs[0], refs[1]
        add_ref = refs[2] if add is not None else None
        o_ref = refs[n_in]

        def finish(r):
            if add_ref is not None:
                r = r + add_ref[...]
            o_ref[...] = r.astype(out_dtype)

        if keep_t:
            at_ref = refs[-1]

            @pl.when(pl.program_id(1) == 0)
            def _():
                at_ref[...] = a_ref[...].T

            finish(_dot(at_ref[...], b_ref[...], NN))
        elif nk == 1:
            finish(_dot(a_ref[...], b_ref[...], dn))
        else:
            acc_ref = refs[n_in + 1]
            k = pl.program_id(2)

            @pl.when(k == 0)
            def _():
                acc_ref[...] = jnp.zeros_like(acc_ref)

            acc_ref[...] += _dot(a_ref[...], b_ref[...], dn)

            @pl.when(k == nk - 1)
            def _():
                finish(acc_ref[...])

    return pl.pallas_call(
        body, name=name, grid=(M // tm, nj, nk), in_specs=specs, out_specs=out_spec, out_shape=out_shape,
        scratch_shapes=scratch,
        compiler_params=_params("parallel", "arbitrary" if keep_t else "parallel", "arbitrary"),
    )(*args)


def _rms_fwd(xs, gs, name, after=None):
    n = len(xs)
    rows = xs[0].shape[0]
    widths = [x.shape[1] for x in xs]
    tr = _row_tile(rows)
    more_specs, more = _behind(after)

    def body(*refs):
        o_ref = refs[2 * n + len(more)]
        off = 0
        for t in range(n):
            o_ref[:, off:off + widths[t]] = _rms(refs[t][...], refs[n + t][...]).astype(BF16)
            off += widths[t]

    return pl.pallas_call(
        body, name=name, grid=(rows // tr,),
        in_specs=[pl.BlockSpec((tr, w), lambda i: (i, 0)) for w in widths]
        + [pl.BlockSpec((1, w), lambda i: (0, 0)) for w in widths] + more_specs,
        out_specs=pl.BlockSpec((tr, sum(widths)), lambda i: (i, 0)),
        out_shape=jax.ShapeDtypeStruct((rows, sum(widths)), BF16),
        compiler_params=_params("parallel"),
    )(*xs, *gs, *more)


def _rms_bwd(x, g, dy, name, dy_col=0, res=None, want_dx=True, after=None):
    rows, w = x.shape
    tr = _row_tile(rows)
    has_res = res is not None
    more_specs, more = _behind(after)

    def body(*refs):
        x_ref, g_ref, dy_ref = refs[:3]
        res_ref = refs[3] if has_res else None
        outs = refs[3 + has_res + len(more):]
        dg_ref = outs[-1]
        dx, dg = _rms_grad(x_ref[...], g_ref[...], dy_ref[...])

        @pl.when(pl.program_id(0) == 0)
        def _():
            dg_ref[...] = jnp.zeros_like(dg_ref)

        dg_ref[...] += dg
        if want_dx:
            if has_res:
                dx = dx + res_ref[...]
            outs[0][...] = dx
            outs[1][...] = dx.astype(BF16)

    row_spec = pl.BlockSpec((tr, w), lambda i: (i, 0))
    in_specs = [row_spec, pl.BlockSpec((1, w), lambda i: (0, 0)), pl.BlockSpec((tr, w), lambda i: (i, dy_col))]
    args = [x, g, dy]
    if has_res:
        in_specs.append(row_spec)
        args.append(res)
    in_specs += more_specs
    args += more
    out_specs, out_shape = [], []
    if want_dx:
        out_specs += [row_spec, row_spec]
        out_shape += [jax.ShapeDtypeStruct((rows, w), F32), jax.ShapeDtypeStruct((rows, w), BF16)]
    out_specs.append(pl.BlockSpec((1, w), lambda i: (0, 0)))
    out_shape.append(jax.ShapeDtypeStruct((1, w), F32))
    return pl.pallas_call(
        body, name=name, grid=(rows // tr,), in_specs=in_specs, out_specs=out_specs, out_shape=out_shape,
        compiler_params=_params("arbitrary"),
    )(*args)


def _loss_head(y, target, name):
    rows, w = y.shape
    tr = _row_tile(rows)

    def body(y_ref, t_ref, dy_ref, dyb_ref, acc_ref):
        e = y_ref[...] - t_ref[...]

        @pl.when(pl.program_id(0) == 0)
        def _():
            acc_ref[...] = jnp.zeros_like(acc_ref)

        acc_ref[...] += jnp.sum(e * e)
        dy = e * (1.0 / w)
        dy_ref[...] = dy
        dyb_ref[...] = dy.astype(BF16)

    row_spec = pl.BlockSpec((tr, w), lambda i: (i, 0))
    return pl.pallas_call(
        body, name=name, grid=(rows // tr,), in_specs=[row_spec, row_spec],
        out_specs=[row_spec, row_spec, pl.BlockSpec((8, 128), lambda i: (0, 0))],
        out_shape=[jax.ShapeDtypeStruct((rows, w), F32), jax.ShapeDtypeStruct((rows, w), BF16),
                   jax.ShapeDtypeStruct((8, 128), F32)],
        compiler_params=_params("arbitrary"),
    )(y, target)


def _qk_fwd(proj, g_q, g_k, name, after=None):
    rows = proj.shape[0]
    w = FOX_H * HEAD
    tr = _row_tile(rows)
    more_specs, more = _behind(after)

    def body(q_ref, k_ref, v_ref, gq_ref, gk_ref, *rest):
        qn_ref, kn_ref, vb_ref = rest[len(more):]
        for h in range(FOX_H):
            sl = slice(h * HEAD, (h + 1) * HEAD)
            qn_ref[:, sl] = _rms(q_ref[:, sl], gq_ref[...]).astype(BF16)
            kn_ref[:, sl] = _rms(k_ref[:, sl], gk_ref[...]).astype(BF16)
        vb_ref[...] = v_ref[...].astype(BF16)

    gain = pl.BlockSpec((1, HEAD), lambda i: (0, 0))
    out = pl.BlockSpec((tr, w), lambda i: (i, 0))
    return pl.pallas_call(
        body, name=name, grid=(rows // tr,),
        in_specs=[pl.BlockSpec((tr, w), lambda i: (i, 0)), pl.BlockSpec((tr, w), lambda i: (i, 1)),
                  pl.BlockSpec((tr, w), lambda i: (i, 2)), gain, gain] + more_specs,
        out_specs=[out, out, out], out_shape=[jax.ShapeDtypeStruct((rows, w), BF16)] * 3,
        compiler_params=_params("parallel"),
    )(proj, proj, proj, g_q, g_k, *more)


def _qk_bwd(proj, g_q, g_k, dqn, dkn, name):
    rows = proj.shape[0]
    w = FOX_H * HEAD
    tr = _row_tile(rows)

    def body(q_ref, k_ref, gq_ref, gk_ref, dqn_ref, dkn_ref, dq_ref, dk_ref, dgq_ref, dgk_ref):
        @pl.when(pl.program_id(0) == 0)
        def _():
            dgq_ref[...] = jnp.zeros_like(dgq_ref)
            dgk_ref[...] = jnp.zeros_like(dgk_ref)

        for h in range(FOX_H):
            sl = slice(h * HEAD, (h + 1) * HEAD)
            dq, dgq = _rms_grad(q_ref[:, sl], gq_ref[...], dqn_ref[:, sl])
            dk, dgk = _rms_grad(k_ref[:, sl], gk_ref[...], dkn_ref[:, sl])
            dq_ref[:, sl] = dq.astype(BF16)
            dk_ref[:, sl] = dk.astype(BF16)
            dgq_ref[...] += dgq
            dgk_ref[...] += dgk

    gain = pl.BlockSpec((1, HEAD), lambda i: (0, 0))
    row = pl.BlockSpec((tr, w), lambda i: (i, 0))
    return pl.pallas_call(
        body, name=name, grid=(rows // tr,),
        in_specs=[row, pl.BlockSpec((tr, w), lambda i: (i, 1)), gain, gain, row, row],
        out_specs=[row, row, gain, gain],
        out_shape=[jax.ShapeDtypeStruct((rows, w), BF16)] * 2 + [jax.ShapeDtypeStruct((1, HEAD), F32)] * 2,
        compiler_params=_params("arbitrary"),
    )(proj, proj, g_q, g_k, dqn, dkn)


def _fgate_fwd(f_logit, b_pad, name):
    S = f_logit.shape[0]
    T = min(CUM_T, S)
    nb = S // T
    RT = min(ATT_T, S)
    nbr = S // RT

    def body(f_ref, b_ref, col_ref, row_ref, c_scr):
        tri = (lax.broadcasted_iota(jnp.int32, (T, T), 0) >= lax.broadcasted_iota(jnp.int32, (T, T), 1)).astype(F32)
        carry = jnp.zeros((1, HEAD), F32)
        for blk in range(nb):
            z = f_ref[blk * T:(blk + 1) * T, :] + b_ref[...]
            lf = jnp.minimum(z, 0.0) - jnp.log1p(jnp.exp(-jnp.abs(z)))
            cb = jnp.dot(tri, lf, precision=lax.Precision.HIGHEST, preferred_element_type=F32) + carry
            c_scr[blk * T:(blk + 1) * T, :] = cb
            carry = cb[T - 1:T, :]
        c = c_scr[...]
        lane = lax.broadcasted_iota(jnp.int32, c.shape, 1)
        ct = c.T
        for h in range(FOX_H):
            col_ref[h] = jnp.sum(jnp.where(lane == h, c, 0.0), axis=1, keepdims=True)
            for jb in range(nbr):
                row_ref[h, jb] = ct[h:h + 1, jb * RT:(jb + 1) * RT]

    return pl.pallas_call(
        body, name=name,
        out_shape=[jax.ShapeDtypeStruct((FOX_H, S, 1), F32), jax.ShapeDtypeStruct((FOX_H, nbr, 1, RT), F32)],
        scratch_shapes=[pltpu.VMEM((S, HEAD), F32)], compiler_params=_params(),
    )(f_logit, b_pad)


def _fgate_bwd(f_logit, b_pad, dc_col, dc_row, name):
    S = f_logit.shape[0]
    T = min(CUM_T, S)
    nb = S // T
    RT = min(ATT_T, S)

    def body(f_ref, b_ref, dcol_ref, drow_ref, df_ref, db_ref, dc_scr, rt_scr):
        lane = lax.broadcasted_iota(jnp.int32, (S, HEAD), 1)
        sub = lax.broadcasted_iota(jnp.int32, (HEAD, RT), 0)
        dc = jnp.zeros((S, HEAD), F32)
        for h in range(FOX_H):
            dc = jnp.where(lane == h, dcol_ref[h], dc)
        for jb in range(S // RT):
            rt = jnp.zeros((HEAD, RT), F32)
            for h in range(FOX_H):
                rt = jnp.where(sub == h, drow_ref[h, jb], rt)
            rt_scr[jb * RT:(jb + 1) * RT, :] = rt.T
        dc_scr[...] = dc - rt_scr[...]
        tri = (lax.broadcasted_iota(jnp.int32, (T, T), 0) <= lax.broadcasted_iota(jnp.int32, (T, T), 1)).astype(F32)
        carry = jnp.zeros((1, HEAD), F32)
        db = jnp.zeros((1, HEAD), F32)
        for blk in reversed(range(nb)):
            rows = slice(blk * T, (blk + 1) * T)
            dlf = jnp.dot(tri, dc_scr[rows, :], precision=lax.Precision.HIGHEST, preferred_element_type=F32) + carry
            carry = dlf[0:1, :]
            z = f_ref[rows, :] + b_ref[...]
            df = dlf * jax.nn.sigmoid(-z)
            df_ref[rows, :] = df.astype(BF16)
            db = db + jnp.sum(df, axis=0, keepdims=True)
        db_ref[...] = db

    return pl.pallas_call(
        body, name=name,
        out_shape=[jax.ShapeDtypeStruct((S, HEAD), BF16), jax.ShapeDtypeStruct((1, HEAD), F32)],
        scratch_shapes=[pltpu.VMEM((S, HEAD), F32), pltpu.VMEM((S, HEAD), F32)], compiler_params=_params(),
    )(f_logit, b_pad, dc_col, dc_row)


def _fox_fwd(qn, kn, vb, c_col, c_row, name, after=None):
    S = qn.shape[0]
    T = min(ATT_T, S)
    nb = S // T
    scale = 1.0 / math.sqrt(HEAD)
    more_specs, more = _behind(after)

    def body(q_ref, k_ref, v_ref, cc_ref, cr_ref, *rest):
        o_ref, lse_ref = rest[len(more):]
        i = pl.program_id(1)
        q = q_ref[...]
        cc = cc_ref[...]

        def step(j, carry, diagonal=False):
            m, l, acc = carry
            sl = pl.ds(pl.multiple_of(j * T, T), T)
            s = _dot(q, k_ref[sl, :], NT) * scale + cc - cr_ref[j]
            if diagonal:
                s = jnp.where(lax.broadcasted_iota(jnp.int32, (T, T), 0) >= lax.broadcasted_iota(jnp.int32, (T, T), 1), s, NEG)
            m2 = jnp.maximum(m, jnp.max(s, axis=1, keepdims=True))
            p = jnp.exp(s - m2)
            al = jnp.exp(m - m2)
            return m2, al * l + jnp.sum(p, axis=1, keepdims=True), al * acc + _dot(p.astype(BF16), v_ref[sl, :])

        init = (jnp.full((T, 1), NEG, F32), jnp.zeros((T, 1), F32), jnp.zeros((T, HEAD), F32))
        m, l, acc = step(i, lax.fori_loop(0, i, step, init), diagonal=True)
        o_ref[...] = acc / l
        lse_ref[...] = m + jnp.log(l)

    head = pl.BlockSpec((S, HEAD), lambda h, i: (0, h))
    return pl.pallas_call(
        body, name=name, grid=(FOX_H, nb),
        in_specs=[pl.BlockSpec((T, HEAD), lambda h, i: (i, h)), head, head,
                  pl.BlockSpec((None, T, 1), lambda h, i: (h, i, 0)),
                  pl.BlockSpec((None, nb, 1, T), lambda h, i: (h, 0, 0, 0))] + more_specs,
        out_specs=[pl.BlockSpec((T, HEAD), lambda h, i: (i, h)), pl.BlockSpec((None, T, 1), lambda h, i: (h, i, 0))],
        out_shape=[jax.ShapeDtypeStruct((S, FOX_H * HEAD), F32), jax.ShapeDtypeStruct((FOX_H, S, 1), F32)],
        compiler_params=_params("parallel", "parallel"),
    )(qn, kn, vb, c_col, c_row, *more)


def _fox_bwd(qn, kn, vb, c_col, c_row, lse, o, do, name):
    S = qn.shape[0]
    T = min(ATT_T, S)
    nb = S // T
    scale = 1.0 / math.sqrt(HEAD)

    def body(q_ref, k_ref, v_ref, cc_ref, cr_ref, lse_ref, o_ref, do_ref, dq_ref, dk_ref, dv_ref, dcc_ref, dcr_ref):
        j = pl.program_id(1)

        @pl.when(j == 0)
        def _():
            dq_ref[...] = jnp.zeros_like(dq_ref)
            dcc_ref[...] = jnp.zeros_like(dcc_ref)

        k = k_ref[...]
        v = v_ref[...]
        cr = cr_ref[...]

        def step(i, carry, diagonal=False):
            dk, dv, dcr = carry
            sl = pl.ds(pl.multiple_of(i * T, T), T)
            q = q_ref[sl, :]
            d_o = do_ref[sl, :]
            s = _dot(q, k, NT) * scale + cc_ref[sl, :] - cr
            if diagonal:
                s = jnp.where(lax.broadcasted_iota(jnp.int32, (T, T), 0) >= lax.broadcasted_iota(jnp.int32, (T, T), 1), s, NEG)
            p = jnp.exp(s - lse_ref[sl, :])
            dob = d_o.astype(BF16)
            dp = _dot(dob, v, NT)
            delta = jnp.sum(d_o * o_ref[sl, :], axis=1, keepdims=True)
            ds = p * (dp - delta)
            dsb = ds.astype(BF16)
            dq_ref[sl, :] += _dot(dsb, k) * scale
            dcc_ref[sl, :] += jnp.sum(ds, axis=1, keepdims=True)
            return (dk + _dot(dsb, q, TN), dv + _dot(p.astype(BF16), dob, TN), dcr + jnp.sum(ds, axis=0, keepdims=True))

        init = (jnp.zeros((T, HEAD), F32), jnp.zeros((T, HEAD), F32), jnp.zeros((1, T), F32))
        dk, dv, dcr = lax.fori_loop(j + 1, nb, step, step(j, init, diagonal=True))
        dk_ref[...] = dk * scale
        dv_ref[...] = dv.astype(BF16)
        dcr_ref[...] = dcr

    head = pl.BlockSpec((S, HEAD), lambda h, j: (0, h))
    tile = pl.BlockSpec((T, HEAD), lambda h, j: (j, h))
    col = pl.BlockSpec((None, S, 1), lambda h, j: (h, 0, 0))
    row = pl.BlockSpec((None, None, 1, T), lambda h, j: (h, j, 0, 0))
    w = FOX_H * HEAD
    return pl.pallas_call(
        body, name=name, grid=(FOX_H, nb),
        in_specs=[head, tile, tile, col, row, col, head, head],
        out_specs=[head, tile, tile, col, row],
        out_shape=[jax.ShapeDtypeStruct((S, w), F32), jax.ShapeDtypeStruct((S, w), F32), jax.ShapeDtypeStruct((S, w), BF16),
                   jax.ShapeDtypeStruct((FOX_H, S, 1), F32), jax.ShapeDtypeStruct((FOX_H, nb, 1, T), F32)],
        compiler_params=_params("arbitrary", "arbitrary"),
    )(qn, kn, vb, c_col, c_row, lse, o, do)


def _lru_gates(uc, wra, bra, wri, bri, lam):
    ucb = uc.astype(BF16)
    r = jax.nn.sigmoid(_dot(ucb, wra.astype(BF16)) + bra)
    ig = jax.nn.sigmoid(_dot(ucb, wri.astype(BF16)) + bri)
    sp = jnp.maximum(-lam, 0.0) + jnp.log1p(jnp.exp(-jnp.abs(lam)))
    log_a = -LRU_C * r * sp
    a = jnp.exp(log_a)
    mult = jnp.sqrt(_one_minus_exp(2.0 * log_a))
    return r, ig, sp, a, mult


def _conv(pad_ref, cw, cb, S):
    uc = cb
    for j in range(CONV_K):
        uc = uc + cw[j:j + 1, :] * pad_ref[5 + j:5 + j + S, :]
    return uc


def _lru_specs(S, n_proj_cols):
    u_col = 3 * FOX_H
    g_col = u_col + LRU_NB
    blk = pl.BlockSpec((S, HEAD), lambda n: (0, n))
    vec = pl.BlockSpec((1, HEAD), lambda n: (0, n))
    mat = pl.BlockSpec((None, HEAD, HEAD), lambda n: (n, 0, 0))
    return dict(
        u=pl.BlockSpec((S, HEAD), lambda n: (0, u_col + n)), gate=pl.BlockSpec((S, HEAD), lambda n: (0, g_col + n)),
        blk=blk, vec=vec, mat=mat, cw=pl.BlockSpec((CONV_K, HEAD), lambda n: (0, n)))


def _lru_fwd(proj, conv_w, conv_b, w_ra, b_ra, w_ri, b_ri, lam, name):
    S = proj.shape[0]
    sp_ = _lru_specs(S, proj.shape[1])
    rows8 = S // 8

    def body(u_ref, gt_ref, cw_ref, cb_ref, wra_ref, bra_ref, wri_ref, bri_ref, lam_ref, y_ref, h_ref, pad, a_scr, b_scr):
        pad[0:8, :] = jnp.zeros((8, HEAD), F32)
        pad[8:S + 8, :] = u_ref[...]
        uc = _conv(pad, cw_ref[...], cb_ref[...], S)
        r, ig, sp, a, mult = _lru_gates(uc, wra_ref[...], bra_ref[...], wri_ref[...], bri_ref[...], lam_ref[...])
        a_scr[...] = a
        b_scr[...] = mult * (ig * uc)
        sub = lax.broadcasted_iota(jnp.int32, (8, HEAD), 0)

        def step(t, carry):
            sl = pl.ds(pl.multiple_of(t * 8, 8), 8)
            A, B = a_scr[sl, :], b_scr[sl, :]
            for d in (1, 2, 4):
                m = sub >= d
                B = jnp.where(m, A * pltpu.roll(B, d, 0) + B, B)
                A = jnp.where(m, A * pltpu.roll(A, d, 0), A)
            h = A * carry + B
            h_ref[sl, :] = h
            return h[7:8, :]

        lax.fori_loop(0, rows8, step, jnp.zeros((1, HEAD), F32), unroll=SCAN_UNROLL)
        y_ref[...] = h_ref[...] * _gelu(gt_ref[...])

    w = LRU_NB * HEAD
    return pl.pallas_call(
        body, name=name, grid=(LRU_NB,),
        in_specs=[sp_["u"], sp_["gate"], sp_["cw"], sp_["vec"], sp_["mat"], sp_["vec"], sp_["mat"], sp_["vec"], sp_["vec"]],
        out_specs=[sp_["blk"], sp_["blk"]],
        out_shape=[jax.ShapeDtypeStruct((S, w), F32)] * 2,
        scratch_shapes=[pltpu.VMEM((S + 8, HEAD), F32), pltpu.VMEM((S, HEAD), F32), pltpu.VMEM((S, HEAD), F32)],
        compiler_params=_params("parallel"),
    )(proj, proj, conv_w, conv_b, w_ra, b_ra, w_ri, b_ri, lam)


def _lru_bwd(proj, h, dy, conv_w, conv_b, w_ra, b_ra, w_ri, b_ri, lam, name, after=None):
    S = proj.shape[0]
    sp_ = _lru_specs(S, proj.shape[1])
    rows8 = S // 8
    more_specs, more = _behind(after)

    def body(u_ref, gt_ref, h_ref, dy_ref, cw_ref, cb_ref, wra_ref, bra_ref, wri_ref, bri_ref, lam_ref, *rest):
        (du_ref, dgt_ref, dcw_ref, dcb_ref, dwra_ref, dbra_ref, dwri_ref, dbri_ref, dlam_ref,
         pad, an_scr, d_scr, g_scr, hp_scr) = rest[len(more):]
        zero8 = jnp.zeros((8, HEAD), F32)
        pad[0:8, :] = zero8
        pad[8:S + 8, :] = u_ref[...]
        cw = cw_ref[...]
        uc = _conv(pad, cw, cb_ref[...], S)
        wra, wri, lam_v = wra_ref[...], wri_ref[...], lam_ref[...]
        r, ig, sp, a, mult = _lru_gates(uc, wra, bra_ref[...], wri, bri_ref[...], lam_v)
        gate = gt_ref[...]
        dy_v = dy_ref[...]
        hv = h_ref[...]
        dgt_ref[...] = (dy_v * hv * _gelu_grad(gate)).astype(BF16)
        d_scr[...] = dy_v * _gelu(gate)
        g_scr[0:S, :] = a
        g_scr[S:S + 8, :] = zero8
        an_scr[...] = g_scr[1:S + 1, :]
        sub = lax.broadcasted_iota(jnp.int32, (8, HEAD), 0)

        def step(t, carry):
            sl = pl.ds(pl.multiple_of((rows8 - 1 - t) * 8, 8), 8)
            A, D = an_scr[sl, :], d_scr[sl, :]
            for d in (1, 2, 4):
                m = sub + d <= 7
                D = jnp.where(m, A * pltpu.roll(D, 8 - d, 0) + D, D)
                A = jnp.where(m, A * pltpu.roll(A, 8 - d, 0), A)
            g = A * carry + D
            g_scr[sl, :] = g
            return g[0:1, :]

        lax.fori_loop(0, rows8, step, jnp.zeros((1, HEAD), F32), unroll=SCAN_UNROLL)
        g = g_scr[0:S, :]
        hp_scr[0:8, :] = zero8
        hp_scr[8:S + 8, :] = hv
        da = g * hp_scr[7:S + 7, :]
        iu = ig * uc
        dmult = g * iu
        diu = g * mult
        dig = diu * uc
        duc = diu * ig
        dlog_a = da * a - dmult * (a * a) / mult
        dr = dlog_a * (-LRU_C * sp)
        dsp = jnp.sum(dlog_a * (-LRU_C * r), axis=0, keepdims=True)
        dlam_ref[...] = -dsp * jax.nn.sigmoid(-lam_v)
        dpr = dr * r * (1.0 - r)
        dpi = dig * ig * (1.0 - ig)
        dbra_ref[...] = jnp.sum(dpr, axis=0, keepdims=True)
        dbri_ref[...] = jnp.sum(dpi, axis=0, keepdims=True)
        ucb = uc.astype(BF16)
        dprb, dpib = dpr.astype(BF16), dpi.astype(BF16)
        dwra_ref[...] = _dot(ucb, dprb, TN).astype(BF16)
        dwri_ref[...] = _dot(ucb, dpib, TN).astype(BF16)
        duc = duc + _dot(dprb, wra.astype(BF16), NT) + _dot(dpib, wri.astype(BF16), NT)
        dcb_ref[...] = jnp.sum(duc, axis=0, keepdims=True)
        for j in range(CONV_K):
            dcw_ref[j:j + 1, :] = jnp.sum(duc * pad[5 + j:5 + j + S, :], axis=0, keepdims=True)
        g_scr[0:S, :] = duc
        g_scr[S:S + 8, :] = zero8
        du = jnp.zeros((S, HEAD), F32)
        for j in range(CONV_K):
            du = du + cw[j:j + 1, :] * g_scr[3 - j:3 - j + S, :]
        du_ref[...] = du.astype(BF16)

    w = LRU_NB * HEAD
    bf = jax.ShapeDtypeStruct((S, w), BF16)
    vec = jax.ShapeDtypeStruct((1, w), F32)
    mat = jax.ShapeDtypeStruct((LRU_NB, HEAD, HEAD), BF16)
    return pl.pallas_call(
        body, name=name, grid=(LRU_NB,),
        in_specs=[sp_["u"], sp_["gate"], sp_["blk"], sp_["blk"], sp_["cw"], sp_["vec"], sp_["mat"], sp_["vec"],
                  sp_["mat"], sp_["vec"], sp_["vec"]] + more_specs,
        out_specs=[sp_["blk"], sp_["blk"], sp_["cw"], sp_["vec"], sp_["mat"], sp_["vec"], sp_["mat"], sp_["vec"], sp_["vec"]],
        out_shape=[bf, bf, jax.ShapeDtypeStruct((CONV_K, w), F32), vec, mat, vec, mat, vec, vec],
        scratch_shapes=[pltpu.VMEM((S + 8, HEAD), F32), pltpu.VMEM((S, HEAD), F32), pltpu.VMEM((S, HEAD), F32),
                        pltpu.VMEM((S + 8, HEAD), F32), pltpu.VMEM((S + 8, HEAD), F32)],
        compiler_params=_params("parallel"),
    )(proj, proj, h, dy, conv_w, conv_b, w_ra, b_ra, w_ri, b_ri, lam, *more)


def _xattn_fwd(cq, ckv, g_cq, g_ck, name):
    S, w = cq.shape
    M = ckv.shape[0]
    tr = _row_tile(S)
    scale = 1.0 / math.sqrt(HEAD)

    def body(cq_ref, ckv_ref, gq_ref, gk_ref, o_ref):
        for h in range(XATT_H):
            sl = slice(h * HEAD, (h + 1) * HEAD)
            qn = _rms(cq_ref[:, sl], gq_ref[...]).astype(BF16)
            kn = _rms(ckv_ref[:, sl], gk_ref[...]).astype(BF16)
            v = ckv_ref[:, w + h * HEAD:w + (h + 1) * HEAD].astype(BF16)
            s = _dot(qn, kn, NT) * scale
            p = jnp.exp(s - jnp.max(s, axis=1, keepdims=True))
            p = p / jnp.sum(p, axis=1, keepdims=True)
            o_ref[:, sl] = _dot(p.astype(BF16), v).astype(BF16)

    gain = pl.BlockSpec((1, HEAD), lambda i: (0, 0))
    return pl.pallas_call(
        body, name=name, grid=(S // tr,),
        in_specs=[pl.BlockSpec((tr, w), lambda i: (i, 0)), pl.BlockSpec((M, 2 * w), lambda i: (0, 0)), gain, gain],
        out_specs=pl.BlockSpec((tr, w), lambda i: (i, 0)), out_shape=jax.ShapeDtypeStruct((S, w), BF16),
        compiler_params=_params("parallel"),
    )(cq, ckv, g_cq, g_ck)


def _xattn_bwd(cq, ckv, g_cq, g_ck, do, name):
    S, w = cq.shape
    M = ckv.shape[0]
    tr = _row_tile(S)
    nsteps = S // tr
    scale = 1.0 / math.sqrt(HEAD)

    def body(cq_ref, ckv_ref, gq_ref, gk_ref, do_ref, dcq_ref, dckv_ref, dgq_ref, dgk_ref, dkn_scr, dv_scr):
        step = pl.program_id(0)

        @pl.when(step == 0)
        def _():
            dkn_scr[...] = jnp.zeros_like(dkn_scr)
            dv_scr[...] = jnp.zeros_like(dv_scr)
            dgq_ref[...] = jnp.zeros_like(dgq_ref)

        for h in range(XATT_H):
            sl = slice(h * HEAD, (h + 1) * HEAD)
            q_raw = cq_ref[:, sl]
            qn = _rms(q_raw, gq_ref[...]).astype(BF16)
            kn = _rms(ckv_ref[:, sl], gk_ref[...]).astype(BF16)
            v = ckv_ref[:, w + h * HEAD:w + (h + 1) * HEAD].astype(BF16)
            s = _dot(qn, kn, NT) * scale
            p = jnp.exp(s - jnp.max(s, axis=1, keepdims=True))
            p = p / jnp.sum(p, axis=1, keepdims=True)
            dob = do_ref[:, sl].astype(BF16)
            dp = _dot(dob, v, NT)
            ds = p * (dp - jnp.sum(p * dp, axis=1, keepdims=True)) * scale
            dsb = ds.astype(BF16)
            dv_scr[:, sl] += _dot(p.astype(BF16), dob, TN)
            dkn_scr[:, sl] += _dot(dsb, qn, TN)
            dq, dgq = _rms_grad(q_raw, gq_ref[...], _dot(dsb, kn))
            dcq_ref[:, sl] = dq.astype(BF16)
            dgq_ref[...] += dgq

        @pl.when(step == nsteps - 1)
        def _():
            dgk = jnp.zeros((1, HEAD), F32)
            for h in range(XATT_H):
                sl = slice(h * HEAD, (h + 1) * HEAD)
                dk, dgk_h = _rms_grad(ckv_ref[:, sl], gk_ref[...], dkn_scr[:, sl])
                dckv_ref[:, sl] = dk.astype(BF16)
                dgk = dgk + dgk_h
            dckv_ref[:, w:2 * w] = dv_scr[...].astype(BF16)
            dgk_ref[...] = dgk

    gain = pl.BlockSpec((1, HEAD), lambda i: (0, 0))
    row = pl.BlockSpec((tr, w), lambda i: (i, 0))
    mem = pl.BlockSpec((M, 2 * w), lambda i: (0, 0))
    return pl.pallas_call(
        body, name=name, grid=(nsteps,), in_specs=[row, mem, gain, gain, row], out_specs=[row, mem, gain, gain],
        out_shape=[jax.ShapeDtypeStruct((S, w), BF16), jax.ShapeDtypeStruct((M, 2 * w), BF16),
                   jax.ShapeDtypeStruct((1, HEAD), F32), jax.ShapeDtypeStruct((1, HEAD), F32)],
        scratch_shapes=[pltpu.VMEM((M, w), F32), pltpu.VMEM((M, w), F32)],
        compiler_params=_params("arbitrary"),
    )(cq, ckv, g_cq, g_ck, do)


def _swiglu_fwd(gu, name, n_ct=4):
    S, w2 = gu.shape
    f = w2 // 2
    tc = f // n_ct
    tr = _row_tile(S)

    def body(g_ref, u_ref, o_ref):
        g = g_ref[...]
        o_ref[...] = (g * jax.nn.sigmoid(g) * u_ref[...]).astype(BF16)

    return pl.pallas_call(
        body, name=name, grid=(S // tr, n_ct),
        in_specs=[pl.BlockSpec((tr, tc), lambda i, j: (i, j)), pl.BlockSpec((tr, tc), lambda i, j: (i, j + n_ct))],
        out_specs=pl.BlockSpec((tr, tc), lambda i, j: (i, j)), out_shape=jax.ShapeDtypeStruct((S, f), BF16),
        compiler_params=_params("parallel", "parallel"),
    )(gu, gu)


def _swiglu_bwd(gu, dact, name, n_ct=4):
    S, w2 = gu.shape
    f = w2 // 2
    tc = f // n_ct
    tr = _row_tile(S)

    steps = (S // tr) * n_ct

    def body(g_ref, u_ref, da_ref, o_hbm, buf, sems):
        i, j = pl.program_id(0), pl.program_id(1)
        step = i * n_ct + j
        slot = step % 2

        def put(half):
            cols = pl.ds(pl.multiple_of(half * f + j * tc, 128), tc)
            return pltpu.make_async_copy(buf.at[slot, half], o_hbm.at[pl.ds(pl.multiple_of(i * tr, 8), tr), cols],
                                         sems.at[slot, half])

        @pl.when(step >= 2)
        def _():
            put(0).wait()
            put(1).wait()

        g = g_ref[...]
        sg = jax.nn.sigmoid(g)
        da = da_ref[...]
        buf[slot, 0] = (da * u_ref[...] * (sg * (1.0 + g * (1.0 - sg)))).astype(BF16)
        buf[slot, 1] = (da * (g * sg)).astype(BF16)
        put(0).start()
        put(1).start()

        @pl.when(step == steps - 1)
        def _():
            put(0).wait()
            put(1).wait()
            if steps > 1:
                for half in (0, 1):
                    pltpu.make_async_copy(buf.at[1 - slot, half], o_hbm.at[pl.ds(0, tr), pl.ds(0, tc)],
                                          sems.at[1 - slot, half]).wait()

    return pl.pallas_call(
        body, name=name, grid=(S // tr, n_ct),
        in_specs=[pl.BlockSpec((tr, tc), lambda i, j: (i, j)), pl.BlockSpec((tr, tc), lambda i, j: (i, j + n_ct)),
                  pl.BlockSpec((tr, tc), lambda i, j: (i, j))],
        out_specs=ANY, out_shape=jax.ShapeDtypeStruct((S, w2), BF16),
        scratch_shapes=[pltpu.VMEM((2, 2, tr, tc), BF16), pltpu.SemaphoreType.DMA((2, 2))],
        compiler_params=_params("arbitrary", "arbitrary"),
    )(gu, gu, dact)


def _adamw_math(w, g, m, v):
    m = ADAM_B1 * m + (1.0 - ADAM_B1) * g
    v = ADAM_B2 * v + (1.0 - ADAM_B2) * (g * g)
    m_hat = m / (1.0 - ADAM_B1 ** ADAM_STEP)
    v_hat = v / (1.0 - ADAM_B2 ** ADAM_STEP)
    delta = -ADAM_LR * (m_hat / (jnp.sqrt(v_hat) + ADAM_EPS) + ADAM_WD * w)
    return delta, m, v


def _tile2(R, C, elems):
    if R % 16 == 0:
        return _row_tile(R, cap=max(16, elems // C // 16 * 16)), C
    tc = 128
    while C % (2 * tc) == 0 and R * 2 * tc <= elems:
        tc *= 2
    return R, tc


def _reduce_adamw(own, land, chip, w, m, v, name):
    _, R, C = own.shape
    tr, tc = _tile2(R, C, 1 << 18)

    def body(s_ref, o_ref, l1_ref, l2_ref, l3_ref, w_ref, m_ref, v_ref, g_ref, d_ref, nm_ref, nv_ref):
        g = o_ref[...].astype(F32) + l1_ref[...].astype(F32) + l2_ref[...].astype(F32) + l3_ref[...].astype(F32)
        g_ref[...] = g
        d_ref[...], nm_ref[...], nv_ref[...] = _adamw_math(w_ref[...], g, m_ref[...], v_ref[...])

    def part(flip):
        return pl.BlockSpec((None, tr, tc), lambda i, j, s: (s[0] ^ flip, i, j))

    if w.ndim == 3:
        blk = pl.BlockSpec((tr, None, tc), lambda i, j, s: (i, 0, j))
    else:
        blk = pl.BlockSpec((tr, tc), lambda i, j, s: (i, j))
    sds = jax.ShapeDtypeStruct(w.shape, F32)
    return pl.pallas_call(
        body, name=name,
        grid_spec=pltpu.PrefetchScalarGridSpec(
            num_scalar_prefetch=1, grid=(R // tr, C // tc),
            in_specs=[part(0), part(1), part(2), part(3), blk, blk, blk], out_specs=[blk] * 4),
        out_shape=[sds] * 4, compiler_params=_params("parallel", "parallel"),
    )(chip, own, land, land, land, w, m, v)


def _pair_sum(g8, recv, core, name):
    _, R, C = g8.shape
    tr, tc = _tile2(R, C, 1 << 19)

    def body(c_ref, a_ref, b_ref, o_ref):
        o_ref[...] = (a_ref[...].astype(F32) + b_ref[...].astype(F32)).astype(BF16)

    return pl.pallas_call(
        body, name=name,
        grid_spec=pltpu.PrefetchScalarGridSpec(
            num_scalar_prefetch=1, grid=(N_CHIP, R // tr, C // tc),
            in_specs=[pl.BlockSpec((None, tr, tc), lambda q, i, j, c: (2 * q + c[0], i, j)),
                      pl.BlockSpec((None, tr, tc), lambda q, i, j, c: (q, i, j))],
            out_specs=pl.BlockSpec((None, tr, tc), lambda q, i, j, c: (q, i, j))),
        out_shape=jax.ShapeDtypeStruct((N_CHIP, R, C), BF16),
        compiler_params=_params("parallel", "parallel", "parallel"),
    )(core, g8, recv)


def _place():
    return lax.axis_index("x"), lax.axis_index("y"), lax.axis_index("c")


def _land_with_own(shard, dev, name):
    R, C = shard.shape
    tr, tc = _tile2(R, C, 1 << 19)

    def body(d_ref, s_ref, o_ref):
        o_ref[...] = s_ref[...]

    return pl.pallas_call(
        body, name=name,
        grid_spec=pltpu.PrefetchScalarGridSpec(
            num_scalar_prefetch=1, grid=(R // tr, C // tc),
            in_specs=[pl.BlockSpec((tr, tc), lambda i, j, d: (i, j))],
            out_specs=pl.BlockSpec((None, tr, tc), lambda i, j, d: (d[0], i, j))),
        out_shape=jax.ShapeDtypeStruct((N_DEV, R, C), shard.dtype),
        compiler_params=_params("parallel", "parallel"),
    )(dev, shard)


def _all_gather(shards, lands, name):
    n = len(shards)

    def body(*refs):
        ins, outs = refs[:n], refs[2 * n:3 * n]
        send_sems, recv_sems = refs[3 * n:]
        x, y, c = _place()
        me, sibling = (x, y, c), (x, y, 1 - c)
        chips = [(1 - x, y), (x, 1 - y), (1 - x, 1 - y)]

        def copy(a, k, block, to, src=None):
            dst = outs[a].at[4 * block[0] + 2 * block[1] + block[2]]
            return pltpu.make_async_remote_copy(
                src_ref=dst if src is None else src, dst_ref=dst, send_sem=send_sems.at[a, k],
                recv_sem=recv_sems.at[a, k], device_id=to, device_id_type=MESH)

        sent = []
        for a in range(n):
            sent.append(copy(a, 0, me, sibling, src=ins[a]))
            sent += [copy(a, 1 + j, me, (*chip, c), src=ins[a]) for j, chip in enumerate(chips)]
        for cp in sent:
            cp.start()
        for j, chip in enumerate(chips):
            for a in range(n):
                copy(a, 1 + j, (*chip, c), me).wait_recv()
                fwd = copy(a, 4 + j, (*chip, c), sibling)
                fwd.start()
                sent.append(fwd)
        for a in range(n):
            copy(a, 0, sibling, me).wait_recv()
        for j, chip in enumerate(chips):
            for a in range(n):
                copy(a, 4 + j, (*chip, 1 - c), me).wait_recv()
        for cp in sent:
            cp.wait_send()

    return pl.pallas_call(
        body, name=name, in_specs=[ANY] * (2 * n), out_specs=[ANY] * n,
        out_shape=[jax.ShapeDtypeStruct(l.shape, l.dtype) for l in lands],
        input_output_aliases={n + a: a for a in range(n)},
        scratch_shapes=[pltpu.SemaphoreType.DMA((n, 7)), pltpu.SemaphoreType.DMA((n, 7))],
    )(*shards, *lands)


def _pair_plan(srcs, lands):
    x, y, c = _place()
    plan = []
    for a in range(len(srcs)):
        for q in range(N_CHIP):
            plan.append((srcs[a].at[2 * q + 1 - c], lands[a].at[q], lands[a].at[q], (x, y, 1 - c)))
    return plan


def _chip_plan(srcs, lands):
    x, y, c = _place()
    mine = 2 * x + y
    plan = []
    for a in range(len(srcs)):
        for px, py in [(1 - x, y), (x, 1 - y), (1 - x, 1 - y)]:
            peer = 2 * px + py
            plan.append((srcs[a].at[peer], lands[a].at[mine], lands[a].at[peer], (px, py, c)))
    return plan


def _gather_plan(srcs, lands):
    x, y, c = _place()
    mine = 4 * x + 2 * y + c
    plan = []
    for a in range(len(srcs)):
        for px, py, pc in [(x, y, 1 - c), (1 - x, y, c), (x, 1 - y, c), (1 - x, 1 - y, c)]:
            plan.append((srcs[a], lands[a].at[mine], lands[a].at[4 * px + 2 * py + pc], (px, py, pc)))
    return plan


def _remote(src, dst, send_sem, recv_sem, peer):
    return pltpu.make_async_remote_copy(src_ref=src, dst_ref=dst, send_sem=send_sem, recv_sem=recv_sem,
                                        device_id=peer, device_id_type=MESH)


HBM = pl.BlockSpec(memory_space=pltpu.HBM)
SEMS = pl.BlockSpec(memory_space=pltpu.SEMAPHORE)
DATAFLOW = pltpu.SideEffectType.DATAFLOW_SIDE_EFFECTING


def _split_start(srcs, lands, after, plan_fn, per_array, name):
    n = len(srcs)
    ncopy = per_array * n

    def body(*refs):
        send_sems, recv_sems = refs[2 * n + 1], refs[2 * n + 2]
        token = refs[-1]
        for k, (src, dst, _, peer) in enumerate(plan_fn(refs[:n], refs[n:2 * n])):
            _remote(src, dst, send_sems.at[k], recv_sems.at[k], peer).start()
        token[...] = jnp.zeros_like(token)

    thru = [pltpu.HBM(a.shape, a.dtype) for a in (*srcs, *lands)]
    out = pl.pallas_call(
        body, name=name,
        out_shape=(pltpu.SemaphoreType.DMA((ncopy,)), pltpu.SemaphoreType.DMA((ncopy,)), *thru,
                   jax.ShapeDtypeStruct((8, 128), F32)),
        in_specs=[HBM] * (2 * n) + [ANY], out_specs=(SEMS, SEMS, *[HBM] * (2 * n), pl.BlockSpec(memory_space=pltpu.VMEM)),
        input_output_aliases={i: 2 + i for i in range(2 * n)},
        compiler_params=pltpu.CompilerParams(has_side_effects=DATAFLOW),
    )(*[pltpu.with_memory_space_constraint(a, pltpu.HBM) for a in (*srcs, *lands)], after)
    return out[0], out[1], list(out[2:2 + n]), list(out[2 + n:2 + 2 * n]), out[-1]


def _split_wait(send_sems, recv_sems, srcs, lands, after, plan_fn, name):
    n = len(srcs)

    def body(*refs):
        send_ref, recv_ref = refs[2 * n], refs[2 * n + 1]
        for k, (src, dst, mine, peer) in enumerate(plan_fn(refs[:n], refs[n:2 * n])):
            _remote(src, dst, send_ref.at[k], recv_ref.at[k], peer).wait_send()
            _remote(src, mine, send_ref.at[k], recv_ref.at[k], peer).wait_recv()

    thru = [pltpu.HBM(a.shape, a.dtype) for a in (*srcs, *lands)]
    out = pl.pallas_call(
        body, name=name, out_shape=tuple(thru),
        in_specs=[HBM] * (2 * n) + [SEMS, SEMS, ANY], out_specs=tuple([HBM] * (2 * n)),
        input_output_aliases={i: i for i in range(2 * n)},
        compiler_params=pltpu.CompilerParams(has_side_effects=DATAFLOW),
    )(*srcs, *lands, send_sems, recv_sems, after)
    return list(out[:n]), list(out[n:])


def _gather_forward(lands, name):
    n = len(lands)

    def body(*refs):
        landed, out = refs[:n], refs[n:2 * n]
        send_sems, recv_sems = refs[2 * n:]
        x, y, c = _place()
        chips = [(1 - x, y), (x, 1 - y), (1 - x, 1 - y)]
        sent = []
        for a in range(n):
            for j, (px, py) in enumerate(chips):
                blk = 4 * px + 2 * py + c
                sent.append(_remote(landed[a].at[blk], out[a].at[blk], send_sems.at[a, j], recv_sems.at[a, j], (x, y, 1 - c)))
        for cp in sent:
            cp.start()
        for a in range(n):
            for j, (px, py) in enumerate(chips):
                blk = 4 * px + 2 * py + 1 - c
                _remote(landed[a].at[blk], out[a].at[blk], send_sems.at[a, j], recv_sems.at[a, j], (x, y, 1 - c)).wait_recv()
        for cp in sent:
            cp.wait_send()

    return pl.pallas_call(
        body, name=name, in_specs=[ANY] * n, out_specs=[ANY] * n,
        out_shape=[jax.ShapeDtypeStruct(l.shape, l.dtype) for l in lands],
        input_output_aliases={a: a for a in range(n)},
        scratch_shapes=[pltpu.SemaphoreType.DMA((n, 3)), pltpu.SemaphoreType.DMA((n, 3))],
    )(*lands)


def _pack_rows(groups, name):
    n = len(groups[0])
    rows = -(-n // 8) * 8
    width = max(v.shape[1] for v in groups[0])

    def body(*refs):
        ins, outs = refs[:n * len(groups)], refs[n * len(groups):]
        for gi, o_ref in enumerate(outs):
            o_ref[...] = jnp.zeros_like(o_ref)
            for r in range(n):
                v_ref = ins[gi * n + r]
                o_ref[r:r + 1, 0:v_ref.shape[1]] = v_ref[...]

    return pl.pallas_call(
        body, name=name, out_shape=[jax.ShapeDtypeStruct((rows, width), F32)] * len(groups), compiler_params=_params(),
    )(*[v for g in groups for v in g])


def _unpack_rows(packs, like, name):
    n = len(like)

    def body(*refs):
        ins, outs = refs[:len(packs)], refs[len(packs):]
        for pi, p_ref in enumerate(ins):
            for r in range(n):
                o_ref = outs[pi * n + r]
                o_ref[...] = p_ref[r:r + 1, 0:o_ref.shape[1]]

    flat = pl.pallas_call(
        body, name=name, out_shape=[jax.ShapeDtypeStruct(a.shape, F32) for _ in packs for a in like],
        compiler_params=_params(),
    )(*packs)
    return [flat[pi * n:(pi + 1) * n] for pi in range(len(packs))]


def _sum_adamw(parts, w, m, v, name, col=None):
    P, R, _ = parts.shape
    C = w.shape[1]
    tr = _row_tile(R, cap=512, mult=8)

    def body(c_ref, p_ref, w_ref, m_ref, v_ref, g_ref, d_ref, nm_ref, nv_ref):
        g = p_ref[0].astype(F32)
        for q in range(1, P):
            g = g + p_ref[q].astype(F32)
        g_ref[...] = g
        d_ref[...], nm_ref[...], nv_ref[...] = _adamw_math(w_ref[...], g, m_ref[...], v_ref[...])

    blk = pl.BlockSpec((tr, C), lambda i, c: (i, 0))
    sds = jax.ShapeDtypeStruct((R, C), F32)
    at = jnp.zeros((1,), jnp.int32) if col is None else col
    return pl.pallas_call(
        body, name=name,
        grid_spec=pltpu.PrefetchScalarGridSpec(
            num_scalar_prefetch=1, grid=(R // tr,),
            in_specs=[pl.BlockSpec((P, tr, C), lambda i, c: (0, i, c[0])), blk, blk, blk], out_specs=[blk] * 4),
        out_shape=[sds] * 4, compiler_params=_params("parallel"),
    )(at, parts, w, m, v)


def _local_step(x, mem, target, sp, W, want, done, paired):
    h0 = _rms_fwd([x], [sp["g_mix"]], "rms_mix")
    proj = _mm(h0, W["main_t"], "nt", F32, "proj_main", tm=2048, tn=512)
    f_logit = _mm(h0, W["f_t"], "nt", F32, "proj_f", tm=2048)
    qn, kn, vb = _qk_fwd(proj, sp["g_q"], sp["g_k"], "qk_norm")
    b_pad = jnp.pad(sp["b_f"], ((0, 0), (0, HEAD - FOX_H)))
    c_col, c_row = _fgate_fwd(f_logit, b_pad, "forget_cumsum")
    (w_out, w_cq, w_ckv, w_co), tok = want("attn", qn)
    o_fox, lse = _fox_fwd(qn, kn, vb, c_col, c_row, "fox_fwd", after=tok)
    lru_p = (sp["conv_w"], sp["conv_b"], sp["w_ra"], sp["b_ra"], sp["w_ri"], sp["b_ri"], sp["lam"])
    y_lru, h_lru = _lru_fwd(proj, *lru_p, "lru_fwd")
    mix = _rms_fwd([o_fox, y_lru], [sp["g_fox_out"], sp["g_lru_out"]], "rms_mix_out")
    x1 = _mm(mix, w_out, "nn", F32, "out_proj", add=x)
    hq = _rms_fwd([x1], [sp["g_xattn"]], "rms_xattn")
    mn = _rms_fwd([mem], [sp["g_mem"]], "rms_mem")
    cq = _mm(hq, w_cq, "nn", F32, "xattn_q", tm=2048)
    ckv = _mm(mn, w_ckv, "nn", F32, "xattn_kv")
    o_x = _xattn_fwd(cq, ckv, sp["g_cq"], sp["g_ck"], "xattn_fwd")
    co_w = w_co.shape[2]
    x2 = _mm(o_x, w_co, "nn", F32, "xattn_out", add=x1, tm=2048, tn=co_w)
    hf = _rms_fwd([x2], [sp["g_ffn"]], "rms_ffn")
    (w_gu,), tok = want("gate_up", hf)
    gu_w = w_gu.shape[2]
    gu = _mm(hf, w_gu, "nn", F32, "ffn_up", tn=gu_w, after=tok)
    act = _swiglu_fwd(gu, "swiglu_fwd")
    (w_down,), _ = want("down", act)
    x3 = _mm(act, w_down, "nn", F32, "ffn_down", add=x2, tk=w_down.shape[0] // 2)
    dy, dyb, sq = _loss_head(x3, target, "loss_head")
    dact = _mm(dyb, w_down, "nt", F32, "ffn_down_dx", tm=2048, tn=512)
    g_down = _mm(act, dyb, "tn", BF16, "ffn_down_dw", tm=w_down.shape[0] // 4, tn=2048)
    dgu = _swiglu_bwd(gu, dact, "swiglu_bwd")
    g_gu = _mm(hf, dgu, "tn", BF16, "ffn_up_dw", out_blocks=N_DEV, tn=gu_w)
    tok = done("ffn", [g_gu, g_down])
    dhf = _mm(dgu, w_gu, "nt", F32, "ffn_up_dx", tn=2048, tk=gu_w, after=tok)
    tok = paired("ffn", dhf)
    dx2, dx2b, dg_ffn = _rms_bwd(x2, sp["g_ffn"], dhf, "rms_ffn_bwd", res=dy, after=tok)
    d_ox = _mm(dx2b, w_co, "nt", F32, "xattn_out_dx", tm=2048, tk=co_w)
    g_co = _mm(o_x, dx2b, "tn", BF16, "xattn_out_dw", out_blocks=N_DEV, tn=co_w)
    dcq, dckv, dg_cq, dg_ck = _xattn_bwd(cq, ckv, sp["g_cq"], sp["g_ck"], d_ox, "xattn_bwd")
    dhq = _mm(dcq, w_cq, "nt", F32, "xattn_q_dx", tm=2048)
    g_cq = _mm(hq, dcq, "tn", BF16, "xattn_q_dw")
    dmn = _mm(dckv, w_ckv, "nt", F32, "xattn_kv_dx")
    g_ckv = _mm(mn, dckv, "tn", BF16, "xattn_kv_dw")
    (dg_mem,) = _rms_bwd(mem, sp["g_mem"], dmn, "rms_mem_bwd", want_dx=False)
    dx1, dx1b, dg_xattn = _rms_bwd(x1, sp["g_xattn"], dhq, "rms_xattn_bwd", res=dx2)
    dmix = _mm(dx1b, w_out, "nt", F32, "out_proj_dx")
    g_out = _mm(mix, dx1b, "tn", BF16, "out_proj_dw", tn=2048)
    tok = done("attn", [g_out, g_cq, g_ckv, g_co])
    do_fox, _, dg_fox_out = _rms_bwd(o_fox, sp["g_fox_out"], dmix, "rms_fox_out_bwd", dy_col=0, after=tok)
    dy_lru, _, dg_lru_out = _rms_bwd(y_lru, sp["g_lru_out"], dmix, "rms_lru_out_bwd", dy_col=1)
    tok = paired("attn", dy_lru)
    du, dgate, dconv_w, dconv_b, dw_ra, db_ra, dw_ri, db_ri, dlam = _lru_bwd(proj, h_lru, dy_lru, *lru_p, "lru_bwd", after=tok)
    dqn, dkn, dv, dc_col, dc_row = _fox_bwd(qn, kn, vb, c_col, c_row, lse, o_fox, do_fox, "fox_bwd")
    dq, dk, dg_q, dg_k = _qk_bwd(proj, sp["g_q"], sp["g_k"], dqn, dkn, "qk_norm_bwd")
    df, db_f = _fgate_bwd(f_logit, b_pad, dc_col, dc_row, "forget_cumsum_bwd")
    dproj = jnp.concatenate([dq, dk, dv, du, dgate], axis=1)
    dh_f = _mm(df, W["f_t"], "nn", F32, "proj_f_dx", tm=2048)
    dh0 = _mm(dproj, W["main_t"], "nn", F32, "proj_main_dx", add=dh_f, tk=dproj.shape[1] // 2)
    g_main_t = _mm(dproj, h0, "tn", BF16, "proj_main_dw", tn=2048)
    g_f_t = _mm(df, h0, "tn", BF16, "proj_f_dw", tn=2048)
    grad_x, _, dg_mix = _rms_bwd(x, sp["g_mix"], dh0, "rms_mix_bwd", res=dx1)
    small = dict(g_mix=dg_mix, b_f=db_f[:, :FOX_H], g_q=dg_q, g_k=dg_k, conv_w=dconv_w, conv_b=dconv_b, w_ra=dw_ra,
                 b_ra=db_ra, w_ri=dw_ri, b_ri=db_ri, lam=dlam, g_fox_out=dg_fox_out, g_lru_out=dg_lru_out,
                 g_xattn=dg_xattn, g_mem=dg_mem, g_cq=dg_cq, g_ck=dg_ck, g_ffn=dg_ffn)
    return sq, grad_x, g_main_t, g_f_t, small


BIG = ("w_in", "w_out", "w_cq", "w_ckv", "w_co", "w_gate_up", "w_down")
SMALL = ("g_mix", "b_f", "g_q", "g_k", "conv_b", "w_ra", "b_ra", "w_ri", "b_ri", "lam", "g_fox_out", "g_lru_out",
         "g_xattn", "g_mem", "g_cq", "g_ck", "g_ffn")
ORDER = ("g_mix", "w_in", "b_f", "g_q", "g_k", "conv_w", "conv_b", "w_ra", "b_ra", "w_ri", "b_ri", "lam", "g_fox_out",
         "g_lru_out", "w_out", "g_xattn", "g_mem", "w_cq", "w_ckv", "g_cq", "g_ck", "w_co", "g_ffn", "w_gate_up", "w_down")


def kernel(x, mem, g_mix, w_in, b_f, g_q, g_k, conv_w, conv_b, w_ra, b_ra, w_ri, b_ri, lam, g_fox_out, g_lru_out, w_out, g_xattn, g_mem, w_cq, w_ckv, g_cq, g_ck, w_co, g_ffn, w_gate_up, w_down, loss_target, m_g_mix, m_w_in, m_b_f, m_g_q, m_g_k, m_conv_w, m_conv_b, m_w_ra, m_b_ra, m_w_ri, m_b_ri, m_lam, m_g_fox_out, m_g_lru_out, m_w_out, m_g_xattn, m_g_mem, m_w_cq, m_w_ckv, m_g_cq, m_g_ck, m_w_co, m_g_ffn, m_w_gate_up, m_w_down, v_g_mix, v_w_in, v_b_f, v_g_q, v_g_k, v_conv_w, v_conv_b, v_w_ra, v_b_ra, v_w_ri, v_b_ri, v_lam, v_g_fox_out, v_g_lru_out, v_w_out, v_g_xattn, v_g_mem, v_w_cq, v_w_ckv, v_g_cq, v_g_ck, v_w_co, v_g_ffn, v_w_gate_up, v_w_down):
    given = dict(locals())
    w = {n: given[n] for n in ORDER}
    m = {n: given["m_" + n] for n in ORDER}
    v = {n: given["v_" + n] for n in ORDER}
    D = x.shape[2]
    fw = FOX_H * HEAD
    dev_index = 4 * lax.axis_index("x") + 2 * lax.axis_index("y") + lax.axis_index("c")
    dev = jnp.reshape(dev_index, (1,)).astype(jnp.int32)
    core = jnp.reshape(lax.axis_index("c"), (1,)).astype(jnp.int32)
    chip = jnp.reshape(2 * lax.axis_index("x") + lax.axis_index("y"), (1,)).astype(jnp.int32)

    def shard(d, n):
        return jnp.transpose(d[n], (2, 0, 1)) if n == "w_in" else d[n][0]

    def unshard(a, n):
        return jnp.transpose(a, (1, 2, 0)) if n == "w_in" else a[None]

    gather_groups = dict(attn=("w_out", "w_cq", "w_ckv", "w_co"), gate_up=("w_gate_up",), down=("w_down",))
    gather_order = ("attn", "gate_up", "down")
    reduce_groups = dict(ffn=("w_gate_up", "w_down"), attn=("w_out", "w_cq", "w_ckv", "w_co"), w_in=("w_in",))
    column_blocked = ("w_co", "w_gate_up")
    flying = {}

    def land(a, tag):
        return _land_with_own(a, dev, "own_" + tag)

    def launch(group, after):
        shards = [shard(w, n).astype(BF16) for n in gather_groups[group]]
        lands = [land(s, n) for s, n in zip(shards, gather_groups[group])]
        flying[group] = _split_start(shards, lands, after, _gather_plan, 4, "gather_" + group + "_start")
        return flying[group][4]

    def want(group, after):
        send, recv, shards, lands, _ = flying.pop(group)
        _, lands = _split_wait(send, recv, shards, lands, after, _gather_plan, "gather_" + group + "_wait")
        full = _gather_forward(lands, "gather_" + group + "_forward")
        later = gather_order[gather_order.index(group) + 1:]
        tok = launch(later[0], full[0]) if later else None
        return [g if n in column_blocked else g.reshape(-1, g.shape[2]) for n, g in zip(gather_groups[group], full)], tok

    first = [shard(w, "w_in").astype(BF16).reshape(-1, D), conv_w[0]]
    g_in, g_conv = _all_gather(first, [land(first[0], "w_in"), land(first[1], "conv_w")], "gather_w_in")
    tok = launch("attn", g_conv)
    wt_in = g_in.reshape(-1, D)
    W = dict(main_t=jnp.concatenate([wt_in[:3 * fw], wt_in[3 * fw + FOX_H:]], axis=0),
             f_t=jnp.pad(wt_in[3 * fw:3 * fw + FOX_H], ((0, HEAD - FOX_H), (0, 0))))
    sp = {n: w[n] for n in SMALL if n not in ("w_ra", "w_ri")}
    sp["w_ra"], sp["w_ri"] = w_ra[0], w_ri[0]
    sp["conv_w"] = jnp.transpose(g_conv, (1, 0, 2)).reshape(CONV_K, -1)
    sp["g_mix"] = sp["g_mix"] + tok[0, 0]

    pairing, reducing = {}, {}

    def done(group, grads):
        g8 = [g if g.ndim == 3 else g.reshape(N_DEV, -1, g.shape[1]) for g in grads]
        lands = [lax.empty((N_CHIP,) + g.shape[1:], g.dtype) for g in g8]
        pairing[group] = _split_start(g8, lands, chip, _pair_plan, N_CHIP, "reduce_" + group + "_pair_start")
        return pairing[group][4]

    def paired(group, after):
        send, recv, g8, lands, _ = pairing.pop(group)
        g8, from_sibling = _split_wait(send, recv, g8, lands, after, _pair_plan, "reduce_" + group + "_pair_wait")
        p4 = [_pair_sum(g, r, core, "reduce_pair_sum_" + n) for g, r, n in zip(g8, from_sibling, reduce_groups[group])]
        lands = [lax.empty(p.shape, p.dtype) for p in p4]
        reducing[group] = _split_start(p4, lands, chip, _chip_plan, 3, "reduce_" + group + "_start")
        return reducing[group][4]

    def finish(group, after):
        send, recv, p4, lands, _ = reducing.pop(group)
        p4, lands = _split_wait(send, recv, p4, lands, after, _chip_plan, "reduce_" + group + "_wait")
        return {n: tuple(unshard(r, n) for r in _reduce_adamw(p, l, chip, shard(w, n), shard(m, n), shard(v, n), "adamw_" + n))
                for n, p, l in zip(reduce_groups[group], p4, lands)}

    sq, grad_x, g_main_t, g_f_t, gs = _local_step(x[0], mem[0], loss_target[0], sp, W, want, done, paired)
    loss = lax.psum(0.5 * sq[0, 0] / D, AXES)

    dwt_in = jnp.concatenate([g_main_t[:3 * fw], g_f_t[:FOX_H], g_main_t[3 * fw:]], axis=0)
    tok = done("w_in", [dwt_in.reshape(N_DEV, -1, D)])
    out = finish("ffn", tok)
    tok = paired("w_in", out["w_down"][3])
    vectors = tuple(n for n in SMALL if n not in ("w_ra", "w_ri"))
    mine = [_pack_rows([[gs[n] for n in vectors]], "pack_small_grads")[0], gs["w_ra"].reshape(-1, HEAD),
            gs["w_ri"].reshape(-1, HEAD), gs["conv_w"]]
    lands = [land(a, "small_grads_%d" % k) for k, a in enumerate(mine)]
    s_send, s_recv, s_src, s_land, tok = _split_start(mine, lands, tok, _gather_plan, 4, "gather_small_start")
    out.update(finish("attn", tok))
    _, s_land = _split_wait(s_send, s_recv, s_src, s_land, out["w_co"][3], _gather_plan, "gather_small_wait")
    all_vec, all_ra, all_ri, all_conv = _gather_forward(s_land, "gather_small_forward")

    state = _pack_rows([[d[n] for n in vectors] for d in (w, m, v)], "pack_small_state")
    upd = _sum_adamw(all_vec, *state, "adamw_small_vectors")
    per_vector = _unpack_rows(upd, [w[n] for n in vectors], "unpack_small")
    for r, n in enumerate(vectors):
        out[n] = tuple(per_vector[k][r] for k in range(4))
    for n, parts in (("w_ra", all_ra), ("w_ri", all_ri)):
        res = _sum_adamw(parts, *[d[n].reshape(-1, HEAD) for d in (w, m, v)], "adamw_" + n)
        out[n] = tuple(r.reshape(w[n].shape) for r in res)
    out["conv_w"] = tuple(r[None] for r in _sum_adamw(all_conv, conv_w[0], m_conv_w[0], v_conv_w[0], "adamw_conv_w", col=dev))
    out.update(finish("w_in", upd[1]))

    return (loss, grad_x[None], *[out[n][0] for n in ORDER], *[out[n][1] for n in ORDER],
            *[out[n][2] for n in ORDER], *[out[n][3] for n in ORDER])
```

```python
import functools
import math

import jax
import jax.numpy as jnp
from jax import lax
from jax.experimental import pallas as pl
from jax.experimental.pallas import tpu as pltpu

F32 = jnp.float32
BF16 = jnp.bfloat16
MESH = pl.DeviceIdType.MESH
AXES = ("x", "y", "c")
N_DEV = 8
N_CHIP = 4

HEAD = 128
FOX_H = 8
XATT_H = 4
LRU_NB = 8
CONV_K = 4
LRU_C = 8.0
RMS_EPS = 1e-6
ATT_T = 1024
CUM_T = 256
ROW_T = 256
NEG = -1e30
V7X_VMEM_LIMIT = 48 * 1024 * 1024

ADAM_LR = 0.001
ADAM_B1 = 0.9
ADAM_B2 = 0.999
ADAM_EPS = 1e-08
ADAM_WD = 0.01
ADAM_STEP = 10

NT = (((1,), (1,)), ((), ()))
TN = (((0,), (0,)), ((), ()))
NN = (((1,), (0,)), ((), ()))

ANY = pl.BlockSpec(memory_space=pl.ANY)


def _behind(after):
    return ([], []) if after is None else ([ANY], [after])


def _params(*sem):
    return pltpu.CompilerParams(dimension_semantics=sem or None, vmem_limit_bytes=V7X_VMEM_LIMIT)


def _dot(a, b, dn=NN):
    return lax.dot_general(a, b, dn, preferred_element_type=F32)


def _rms(x, g):
    r = lax.rsqrt(jnp.mean(x * x, axis=-1, keepdims=True) + RMS_EPS)
    return x * r * g


def _rms_grad(x, g, dy):
    r = lax.rsqrt(jnp.mean(x * x, axis=-1, keepdims=True) + RMS_EPS)
    xh = x * r
    dxh = dy * g
    dx = r * (dxh - xh * jnp.mean(dxh * xh, axis=-1, keepdims=True))
    return dx, jnp.sum(dy * xh, axis=0, keepdims=True)


def _gelu(x):
    k = math.sqrt(2.0 / math.pi)
    return 0.5 * x * (1.0 + jnp.tanh(k * (x + 0.044715 * (x * x * x))))


def _gelu_grad(x):
    k = math.sqrt(2.0 / math.pi)
    t = jnp.tanh(k * (x + 0.044715 * (x * x * x)))
    return 0.5 * (1.0 + t) + 0.5 * x * (1.0 - t * t) * k * (1.0 + 3.0 * 0.044715 * x * x)


def _one_minus_exp(z):
    series = -(z + 0.5 * z * z + (1.0 / 6.0) * z * z * z)
    return jnp.where(z > -1e-3, series, 1.0 - jnp.exp(z))


def _row_tile(rows, cap=ROW_T, mult=16):
    t = min(rows, cap)
    while rows % t or (t % mult and t != rows):
        t -= 1
    return t


def _mat_dims(z):
    return (z.shape[-2], z.shape[-1] * (z.shape[0] if z.ndim == 3 else 1))


def _mat_spec(arr, rblk, cblk, rsel, csel):
    if arr.ndim == 2:
        return pl.BlockSpec((rblk, cblk), lambda i, j, k: ((i, j, k)[rsel], (i, j, k)[csel]))
    nw = arr.shape[2]
    assert nw % cblk == 0, (arr.shape, cblk)
    per = nw // cblk
    return pl.BlockSpec((None, rblk, cblk),
                        lambda i, j, k: ((i, j, k)[csel] // per, (i, j, k)[rsel], (i, j, k)[csel] % per))


def _mm(a, b, mode, out_dtype, name, add=None, out_blocks=None, tm=1024, tn=1024, tk=None, after=None):
    ar, ac = _mat_dims(a)
    br, bc = _mat_dims(b)
    if mode == "nn":
        M, K, N = ar, ac, bc
        assert br == K
    elif mode == "nt":
        M, K, N = ar, ac, br
        assert bc == K
    else:
        M, K, N = ac, ar, bc
        assert br == K
    tm, tn = min(tm, M), min(tn, N)
    tk = K if tk is None or mode == "tn" else min(tk, K)
    assert M % tm == 0 and N % tn == 0 and K % tk == 0, (name, M, N, K, tm, tn, tk)
    nk = K // tk
    nj = N // tn
    if mode == "nn":
        specs = [_mat_spec(a, tm, tk, 0, 2), _mat_spec(b, tk, tn, 2, 1)]
        dn = NN
    elif mode == "nt":
        specs = [_mat_spec(a, tm, tk, 0, 2), _mat_spec(b, tn, tk, 1, 2)]
        dn = NT
    else:
        specs = [_mat_spec(a, tk, tm, 2, 0), _mat_spec(b, tk, tn, 2, 1)]
        dn = TN
    args = [a, b]
    if add is not None:
        specs.append(pl.BlockSpec((tm, tn), lambda i, j, k: (i, j)))
        args.append(add)
    specs += _behind(after)[0]
    args += _behind(after)[1]
    n_in = len(args)
    if out_blocks is None:
        out_shape = jax.ShapeDtypeStruct((M, N), out_dtype)
        out_spec = pl.BlockSpec((tm, tn), lambda i, j, k: (i, j))
    else:
        nb = out_blocks
        nw = N // nb
        assert nw % tn == 0
        per = nw // tn
        out_shape = jax.ShapeDtypeStruct((nb, M, nw), out_dtype)
        out_spec = pl.BlockSpec((None, tm, tn), lambda i, j, k: (j // per, i, j % per))
    keep_t = mode == "tn" and nj > 1
    scratch = []
    if nk > 1:
        scratch.append(pltpu.VMEM((tm, tn), F32))
    if keep_t:
        scratch.append(pltpu.VMEM((tm, tk), a.dtype))

    def body(*refs):
        a_ref, b_ref = refs[0], refs[1]
        add_ref = refs[2] if add is not None else None
        o_ref = refs[n_in]

        def finish(r):
            if add_ref is not None:
                r = r + add_ref[...]
            o_ref[...] = r.astype(out_dtype)

        if keep_t:
            at_ref = refs[-1]

            @pl.when(pl.program_id(1) == 0)
            def _():
                at_ref[...] = a_ref[...].T

            finish(_dot(at_ref[...], b_ref[...], NN))
        elif nk == 1:
            finish(_dot(a_ref[...], b_ref[...], dn))
        else:
            acc_ref = refs[n_in + 1]
            k = pl.program_id(2)

            @pl.when(k == 0)
            def _():
                acc_ref[...] = jnp.zeros_like(acc_ref)

            acc_ref[...] += _dot(a_ref[...], b_ref[...], dn)

            @pl.when(k == nk - 1)
            def _():
                finish(acc_ref[...])

    return pl.pallas_call(
        body, name=name, grid=(M // tm, nj, nk), in_specs=specs, out_specs=out_spec, out_shape=out_shape,
        scratch_shapes=scratch,
        compiler_params=_params("parallel", "arbitrary" if keep_t else "parallel", "arbitrary"),
    )(*args)


def _rms_fwd(xs, gs, name, after=None):
    n = len(xs)
    rows = xs[0].shape[0]
    widths = [x.shape[1] for x in xs]
    tr = _row_tile(rows)
    more_specs, more = _behind(after)

    def body(*refs):
        o_ref = refs[2 * n + len(more)]
        off = 0
        for t in range(n):
            o_ref[:, off:off + widths[t]] = _rms(refs[t][...], refs[n + t][...]).astype(BF16)
            off += widths[t]

    return pl.pallas_call(
        body, name=name, grid=(rows // tr,),
        in_specs=[pl.BlockSpec((tr, w), lambda i: (i, 0)) for w in widths]
        + [pl.BlockSpec((1, w), lambda i: (0, 0)) for w in widths] + more_specs,
        out_specs=pl.BlockSpec((tr, sum(widths)), lambda i: (i, 0)),
        out_shape=jax.ShapeDtypeStruct((rows, sum(widths)), BF16),
        compiler_params=_params("parallel"),
    )(*xs, *gs, *more)


def _rms_bwd(x, g, dy, name, dy_col=0, res=None, want_dx=True, after=None):
    rows, w = x.shape
    tr = _row_tile(rows)
    has_res = res is not None
    more_specs, more = _behind(after)

    def body(*refs):
        x_ref, g_ref, dy_ref = refs[:3]
        res_ref = refs[3] if has_res else None
        outs = refs[3 + has_res + len(more):]
        dg_ref = outs[-1]
        dx, dg = _rms_grad(x_ref[...], g_ref[...], dy_ref[...])

        @pl.when(pl.program_id(0) == 0)
        def _():
            dg_ref[...] = jnp.zeros_like(dg_ref)

        dg_ref[...] += dg
        if want_dx:
            if has_res:
                dx = dx + res_ref[...]
            outs[0][...] = dx
            outs[1][...] = dx.astype(BF16)

    row_spec = pl.BlockSpec((tr, w), lambda i: (i, 0))
    in_specs = [row_spec, pl.BlockSpec((1, w), lambda i: (0, 0)), pl.BlockSpec((tr, w), lambda i: (i, dy_col))]
    args = [x, g, dy]
    if has_res:
        in_specs.append(row_spec)
        args.append(res)
    in_specs += more_specs
    args += more
    out_specs, out_shape = [], []
    if want_dx:
        out_specs += [row_spec, row_spec]
        out_shape += [jax.ShapeDtypeStruct((rows, w), F32), jax.ShapeDtypeStruct((rows, w), BF16)]
    out_specs.append(pl.BlockSpec((1, w), lambda i: (0, 0)))
    out_shape.append(jax.ShapeDtypeStruct((1, w), F32))
    return pl.pallas_call(
        body, name=name, grid=(rows // tr,), in_specs=in_specs, out_specs=out_specs, out_shape=out_shape,
        compiler_params=_params("arbitrary"),
    )(*args)


def _loss_head(y, target, name):
    rows, w = y.shape
    tr = _row_tile(rows)

    def body(y_ref, t_ref, dy_ref, dyb_ref, acc_ref):
        e = y_ref[...] - t_ref[...]

        @pl.when(pl.program_id(0) == 0)
        def _():
            acc_ref[...] = jnp.zeros_like(acc_ref)

        acc_ref[...] += jnp.sum(e * e)
        dy = e * (1.0 / w)
        dy_ref[...] = dy
        dyb_ref[...] = dy.astype(BF16)

    row_spec = pl.BlockSpec((tr, w), lambda i: (i, 0))
    return pl.pallas_call(
        body, name=name, grid=(rows // tr,), in_specs=[row_spec, row_spec],
        out_specs=[row_spec, row_spec, pl.BlockSpec((8, 128), lambda i: (0, 0))],
        out_shape=[jax.ShapeDtypeStruct((rows, w), F32), jax.ShapeDtypeStruct((rows, w), BF16),
                   jax.ShapeDtypeStruct((8, 128), F32)],
        compiler_params=_params("arbitrary"),
    )(y, target)


def _qk_fwd(proj, g_q, g_k, name, after=None):
    rows = proj.shape[0]
    w = FOX_H * HEAD
    tr = _row_tile(rows)
    more_specs, more = _behind(after)

    def body(q_ref, k_ref, v_ref, gq_ref, gk_ref, *rest):
        qn_ref, kn_ref, vb_ref = rest[len(more):]
        for h in range(FOX_H):
            sl = slice(h * HEAD, (h + 1) * HEAD)
            qn_ref[:, sl] = _rms(q_ref[:, sl], gq_ref[...]).astype(BF16)
            kn_ref[:, sl] = _rms(k_ref[:, sl], gk_ref[...]).astype(BF16)
        vb_ref[...] = v_ref[...].astype(BF16)

    gain = pl.BlockSpec((1, HEAD), lambda i: (0, 0))
    out = pl.BlockSpec((tr, w), lambda i: (i, 0))
    return pl.pallas_call(
        body, name=name, grid=(rows // tr,),
        in_specs=[pl.BlockSpec((tr, w), lambda i: (i, 0)), pl.BlockSpec((tr, w), lambda i: (i, 1)),
                  pl.BlockSpec((tr, w), lambda i: (i, 2)), gain, gain] + more_specs,
        out_specs=[out, out, out], out_shape=[jax.ShapeDtypeStruct((rows, w), BF16)] * 3,
        compiler_params=_params("parallel"),
    )(proj, proj, proj, g_q, g_k, *more)


def _qk_bwd(proj, g_q, g_k, dqn, dkn, name):
    rows = proj.shape[0]
    w = FOX_H * HEAD
    tr = _row_tile(rows)

    def body(q_ref, k_ref, gq_ref, gk_ref, dqn_ref, dkn_ref, dq_ref, dk_ref, dgq_ref, dgk_ref):
        @pl.when(pl.program_id(0) == 0)
        def _():
            dgq_ref[...] = jnp.zeros_like(dgq_ref)
            dgk_ref[...] = jnp.zeros_like(dgk_ref)

        for h in range(FOX_H):
            sl = slice(h * HEAD, (h + 1) * HEAD)
            dq, dgq = _rms_grad(q_ref[:, sl], gq_ref[...], dqn_ref[:, sl])
            dk, dgk = _rms_grad(k_ref[:, sl], gk_ref[...], dkn_ref[:, sl])
            dq_ref[:, sl] = dq.astype(BF16)
            dk_ref[:, sl] = dk.astype(BF16)
            dgq_ref[...] += dgq
            dgk_ref[...] += dgk

    gain = pl.BlockSpec((1, HEAD), lambda i: (0, 0))
    row = pl.BlockSpec((tr, w), lambda i: (i, 0))
    return pl.pallas_call(
        body, name=name, grid=(rows // tr,),
        in_specs=[row, pl.BlockSpec((tr, w), lambda i: (i, 1)), gain, gain, row, row],
        out_specs=[row, row, gain, gain],
        out_shape=[jax.ShapeDtypeStruct((rows, w), BF16)] * 2 + [jax.ShapeDtypeStruct((1, HEAD), F32)] * 2,
        compiler_params=_params("arbitrary"),
    )(proj, proj, g_q, g_k, dqn, dkn)


def _fgate_fwd(f_logit, b_pad, name):
    S = f_logit.shape[0]
    T = min(CUM_T, S)
    nb = S // T
    RT = min(ATT_T, S)
    nbr = S // RT

    def body(f_ref, b_ref, col_ref, row_ref, c_scr):
        tri = (lax.broadcasted_iota(jnp.int32, (T, T), 0) >= lax.broadcasted_iota(jnp.int32, (T, T), 1)).astype(F32)
        carry = jnp.zeros((1, HEAD), F32)
        for blk in range(nb):
            z = f_ref[blk * T:(blk + 1) * T, :] + b_ref[...]
            lf = jnp.minimum(z, 0.0) - jnp.log1p(jnp.exp(-jnp.abs(z)))
            cb = jnp.dot(tri, lf, precision=lax.Precision.HIGHEST, preferred_element_type=F32) + carry
            c_scr[blk * T:(blk + 1) * T, :] = cb
            carry = cb[T - 1:T, :]
        c = c_scr[...]
        lane = lax.broadcasted_iota(jnp.int32, c.shape, 1)
        ct = c.T
        for h in range(FOX_H):
            col_ref[h] = jnp.sum(jnp.where(lane == h, c, 0.0), axis=1, keepdims=True)
            for jb in range(nbr):
                row_ref[h, jb] = ct[h:h + 1, jb * RT:(jb + 1) * RT]

    return pl.pallas_call(
        body, name=name,
        out_shape=[jax.ShapeDtypeStruct((FOX_H, S, 1), F32), jax.ShapeDtypeStruct((FOX_H, nbr, 1, RT), F32)],
        scratch_shapes=[pltpu.VMEM((S, HEAD), F32)], compiler_params=_params(),
    )(f_logit, b_pad)


def _fgate_bwd(f_logit, b_pad, dc_col, dc_row, name):
    S = f_logit.shape[0]
    T = min(CUM_T, S)
    nb = S // T
    RT = min(ATT_T, S)

    def body(f_ref, b_ref, dcol_ref, drow_ref, df_ref, db_ref, dc_scr, rt_scr):
        lane = lax.broadcasted_iota(jnp.int32, (S, HEAD), 1)
        sub = lax.broadcasted_iota(jnp.int32, (HEAD, RT), 0)
        dc = jnp.zeros((S, HEAD), F32)
        for h in range(FOX_H):
            dc = jnp.where(lane == h, dcol_ref[h], dc)
        for jb in range(S // RT):
            rt = jnp.zeros((HEAD, RT), F32)
            for h in range(FOX_H):
                rt = jnp.where(sub == h, drow_ref[h, jb], rt)
            rt_scr[jb * RT:(jb + 1) * RT, :] = rt.T
        dc_scr[...] = dc - rt_scr[...]
        tri = (lax.broadcasted_iota(jnp.int32, (T, T), 0) <= lax.broadcasted_iota(jnp.int32, (T, T), 1)).astype(F32)
        carry = jnp.zeros((1, HEAD), F32)
        db = jnp.zeros((1, HEAD), F32)
        for blk in reversed(range(nb)):
            rows = slice(blk * T, (blk + 1) * T)
            dlf = jnp.dot(tri, dc_scr[rows, :], precision=lax.Precision.HIGHEST, preferred_element_type=F32) + carry
            carry = dlf[0:1, :]
            z = f_ref[rows, :] + b_ref[...]
            df = dlf * jax.nn.sigmoid(-z)
            df_ref[rows, :] = df.astype(BF16)
            db = db + jnp.sum(df, axis=0, keepdims=True)
        db_ref[...] = db

    return pl.pallas_call(
        body, name=name,
        out_shape=[jax.ShapeDtypeStruct((S, HEAD), BF16), jax.ShapeDtypeStruct((1, HEAD), F32)],
        scratch_shapes=[pltpu.VMEM((S, HEAD), F32), pltpu.VMEM((S, HEAD), F32)], compiler_params=_params(),
    )(f_logit, b_pad, dc_col, dc_row)


def _fox_fwd(qn, kn, vb, c_col, c_row, name, after=None):
    S = qn.shape[0]
    T = min(ATT_T, S)
    nb = S // T
    scale = 1.0 / math.sqrt(HEAD)
    more_specs, more = _behind(after)

    def body(q_ref, k_ref, v_ref, cc_ref, cr_ref, *rest):
        o_ref, lse_ref = rest[len(more):]
        i = pl.program_id(1)
        q = q_ref[...]
        cc = cc_ref[...]

        def step(j, carry, diagonal=False):
            m, l, acc = carry
            sl = pl.ds(pl.multiple_of(j * T, T), T)
            s = _dot(q, k_ref[sl, :], NT) * scale + cc - cr_ref[j]
            if diagonal:
                s = jnp.where(lax.broadcasted_iota(jnp.int32, (T, T), 0) >= lax.broadcasted_iota(jnp.int32, (T, T), 1), s, NEG)
            m2 = jnp.maximum(m, jnp.max(s, axis=1, keepdims=True))
            p = jnp.exp(s - m2)
            al = jnp.exp(m - m2)
            return m2, al * l + jnp.sum(p, axis=1, keepdims=True), al * acc + _dot(p.astype(BF16), v_ref[sl, :])

        init = (jnp.full((T, 1), NEG, F32), jnp.zeros((T, 1), F32), jnp.zeros((T, HEAD), F32))
        m, l, acc = step(i, lax.fori_loop(0, i, step, init), diagonal=True)
        o_ref[...] = acc / l
        lse_ref[...] = m + jnp.log(l)

    head = pl.BlockSpec((S, HEAD), lambda h, i: (0, h))
    return pl.pallas_call(
        body, name=name, grid=(FOX_H, nb),
        in_specs=[pl.BlockSpec((T, HEAD), lambda h, i: (i, h)), head, head,
                  pl.BlockSpec((None, T, 1), lambda h, i: (h, i, 0)),
                  pl.BlockSpec((None, nb, 1, T), lambda h, i: (h, 0, 0, 0))] + more_specs,
        out_specs=[pl.BlockSpec((T, HEAD), lambda h, i: (i, h)), pl.BlockSpec((None, T, 1), lambda h, i: (h, i, 0))],
        out_shape=[jax.ShapeDtypeStruct((S, FOX_H * HEAD), F32), jax.ShapeDtypeStruct((FOX_H, S, 1), F32)],
        compiler_params=_params("parallel", "parallel"),
    )(qn, kn, vb, c_col, c_row, *more)


def _fox_bwd(qn, kn, vb, c_col, c_row, lse, o, do, name):
    S = qn.shape[0]
    T = min(ATT_T, S)
    nb = S // T
    scale = 1.0 / math.sqrt(HEAD)

    def body(q_ref, k_ref, v_ref, cc_ref, cr_ref, lse_ref, o_ref, do_ref, dq_ref, dk_ref, dv_ref, dcc_ref, dcr_ref):
        j = pl.program_id(1)

        @pl.when(j == 0)
        def _():
            dq_ref[...] = jnp.zeros_like(dq_ref)
            dcc_ref[...] = jnp.zeros_like(dcc_ref)

        k = k_ref[...]
        v = v_ref[...]
        cr = cr_ref[...]

        def step(i, carry, diagonal=False):
            dk, dv, dcr = carry
            sl = pl.ds(pl.multiple_of(i * T, T), T)
            q = q_ref[sl, :]
            d_o = do_ref[sl, :]
            s = _dot(q, k, NT) * scale + cc_ref[sl, :] - cr
            if diagonal:
                s = jnp.where(lax.broadcasted_iota(jnp.int32, (T, T), 0) >= lax.broadcasted_iota(jnp.int32, (T, T), 1), s, NEG)
            p = jnp.exp(s - lse_ref[sl, :])
            dob = d_o.astype(BF16)
            dp = _dot(dob, v, NT)
            delta = jnp.sum(d_o * o_ref[sl, :], axis=1, keepdims=True)
            ds = p * (dp - delta)
            dsb = ds.astype(BF16)
            dq_ref[sl, :] += _dot(dsb, k) * scale
            dcc_ref[sl, :] += jnp.sum(ds, axis=1, keepdims=True)
            return (dk + _dot(dsb, q, TN), dv + _dot(p.astype(BF16), dob, TN), dcr + jnp.sum(ds, axis=0, keepdims=True))

        init = (jnp.zeros((T, HEAD), F32), jnp.zeros((T, HEAD), F32), jnp.zeros((1, T), F32))
        dk, dv, dcr = lax.fori_loop(j + 1, nb, step, step(j, init, diagonal=True))
        dk_ref[...] = dk * scale
        dv_ref[...] = dv.astype(BF16)
        dcr_ref[...] = dcr

    head = pl.BlockSpec((S, HEAD), lambda h, j: (0, h))
    tile = pl.BlockSpec((T, HEAD), lambda h, j: (j, h))
    col = pl.BlockSpec((None, S, 1), lambda h, j: (h, 0, 0))
    row = pl.BlockSpec((None, None, 1, T), lambda h, j: (h, j, 0, 0))
    w = FOX_H * HEAD
    return pl.pallas_call(
        body, name=name, grid=(FOX_H, nb),
        in_specs=[head, tile, tile, col, row, col, head, head],
        out_specs=[head, tile, tile, col, row],
        out_shape=[jax.ShapeDtypeStruct((S, w), F32), jax.ShapeDtypeStruct((S, w), F32), jax.ShapeDtypeStruct((S, w), BF16),
                   jax.ShapeDtypeStruct((FOX_H, S, 1), F32), jax.ShapeDtypeStruct((FOX_H, nb, 1, T), F32)],
        compiler_params=_params("arbitrary", "arbitrary"),
    )(qn, kn, vb, c_col, c_row, lse, o, do)


def _lru_gates(uc, wra, bra, wri, bri, lam):
    ucb = uc.astype(BF16)
    r = jax.nn.sigmoid(_dot(ucb, wra.astype(BF16)) + bra)
    ig = jax.nn.sigmoid(_dot(ucb, wri.astype(BF16)) + bri)
    sp = jnp.maximum(-lam, 0.0) + jnp.log1p(jnp.exp(-jnp.abs(lam)))
    log_a = -LRU_C * r * sp
    a = jnp.exp(log_a)
    mult = jnp.sqrt(_one_minus_exp(2.0 * log_a))
    return r, ig, sp, a, mult


def _conv(pad_ref, cw, cb, S):
    uc = cb
    for j in range(CONV_K):
        uc = uc + cw[j:j + 1, :] * pad_ref[5 + j:5 + j + S, :]
    return uc


def _lru_specs(S, n_proj_cols):
    u_col = 3 * FOX_H
    g_col = u_col + LRU_NB
    blk = pl.BlockSpec((S, HEAD), lambda n: (0, n))
    vec = pl.BlockSpec((1, HEAD), lambda n: (0, n))
    mat = pl.BlockSpec((None, HEAD, HEAD), lambda n: (n, 0, 0))
    return dict(
        u=pl.BlockSpec((S, HEAD), lambda n: (0, u_col + n)), gate=pl.BlockSpec((S, HEAD), lambda n: (0, g_col + n)),
        blk=blk, vec=vec, mat=mat, cw=pl.BlockSpec((CONV_K, HEAD), lambda n: (0, n)))


def _lru_fwd(proj, conv_w, conv_b, w_ra, b_ra, w_ri, b_ri, lam, name):
    S = proj.shape[0]
    sp_ = _lru_specs(S, proj.shape[1])
    rows8 = S // 8

    def body(u_ref, gt_ref, cw_ref, cb_ref, wra_ref, bra_ref, wri_ref, bri_ref, lam_ref, y_ref, h_ref, pad, a_scr, b_scr):
        pad[0:8, :] = jnp.zeros((8, HEAD), F32)
        pad[8:S + 8, :] = u_ref[...]
        uc = _conv(pad, cw_ref[...], cb_ref[...], S)
        r, ig, sp, a, mult = _lru_gates(uc, wra_ref[...], bra_ref[...], wri_ref[...], bri_ref[...], lam_ref[...])
        a_scr[...] = a
        b_scr[...] = mult * (ig * uc)
        sub = lax.broadcasted_iota(jnp.int32, (8, HEAD), 0)

        def step(t, carry):
            sl = pl.ds(pl.multiple_of(t * 8, 8), 8)
            A, B = a_scr[sl, :], b_scr[sl, :]
            for d in (1, 2, 4):
                m = sub >= d
                B = jnp.where(m, A * pltpu.roll(B, d, 0) + B, B)
                A = jnp.where(m, A * pltpu.roll(A, d, 0), A)
            h = A * carry + B
            h_ref[sl, :] = h
            return h[7:8, :]

        lax.fori_loop(0, rows8, step, jnp.zeros((1, HEAD), F32))
        y_ref[...] = h_ref[...] * _gelu(gt_ref[...])

    w = LRU_NB * HEAD
    return pl.pallas_call(
        body, name=name, grid=(LRU_NB,),
        in_specs=[sp_["u"], sp_["gate"], sp_["cw"], sp_["vec"], sp_["mat"], sp_["vec"], sp_["mat"], sp_["vec"], sp_["vec"]],
        out_specs=[sp_["blk"], sp_["blk"]],
        out_shape=[jax.ShapeDtypeStruct((S, w), F32)] * 2,
        scratch_shapes=[pltpu.VMEM((S + 8, HEAD), F32), pltpu.VMEM((S, HEAD), F32), pltpu.VMEM((S, HEAD), F32)],
        compiler_params=_params("parallel"),
    )(proj, proj, conv_w, conv_b, w_ra, b_ra, w_ri, b_ri, lam)


def _lru_bwd(proj, h, dy, conv_w, conv_b, w_ra, b_ra, w_ri, b_ri, lam, name, after=None):
    S = proj.shape[0]
    sp_ = _lru_specs(S, proj.shape[1])
    rows8 = S // 8
    more_specs, more = _behind(after)

    def body(u_ref, gt_ref, h_ref, dy_ref, cw_ref, cb_ref, wra_ref, bra_ref, wri_ref, bri_ref, lam_ref, *rest):
        (du_ref, dgt_ref, dcw_ref, dcb_ref, dwra_ref, dbra_ref, dwri_ref, dbri_ref, dlam_ref,
         pad, an_scr, d_scr, g_scr, hp_scr) = rest[len(more):]
        zero8 = jnp.zeros((8, HEAD), F32)
        pad[0:8, :] = zero8
        pad[8:S + 8, :] = u_ref[...]
        cw = cw_ref[...]
        uc = _conv(pad, cw, cb_ref[...], S)
        wra, wri, lam_v = wra_ref[...], wri_ref[...], lam_ref[...]
        r, ig, sp, a, mult = _lru_gates(uc, wra, bra_ref[...], wri, bri_ref[...], lam_v)
        gate = gt_ref[...]
        dy_v = dy_ref[...]
        hv = h_ref[...]
        dgt_ref[...] = (dy_v * hv * _gelu_grad(gate)).astype(BF16)
        d_scr[...] = dy_v * _gelu(gate)
        g_scr[0:S, :] = a
        g_scr[S:S + 8, :] = zero8
        an_scr[...] = g_scr[1:S + 1, :]
        sub = lax.broadcasted_iota(jnp.int32, (8, HEAD), 0)

        def step(t, carry):
            sl = pl.ds(pl.multiple_of((rows8 - 1 - t) * 8, 8), 8)
            A, D = an_scr[sl, :], d_scr[sl, :]
            for d in (1, 2, 4):
                m = sub + d <= 7
                D = jnp.where(m, A * pltpu.roll(D, 8 - d, 0) + D, D)
                A = jnp.where(m, A * pltpu.roll(A, 8 - d, 0), A)
            g = A * carry + D
            g_scr[sl, :] = g
            return g[0:1, :]

        lax.fori_loop(0, rows8, step, jnp.zeros((1, HEAD), F32))
        g = g_scr[0:S, :]
        hp_scr[0:8, :] = zero8
        hp_scr[8:S + 8, :] = hv
        da = g * hp_scr[7:S + 7, :]
        iu = ig * uc
        dmult = g * iu
        diu = g * mult
        dig = diu * uc
        duc = diu * ig
        dlog_a = da * a - dmult * (a * a) / mult
        dr = dlog_a * (-LRU_C * sp)
        dsp = jnp.sum(dlog_a * (-LRU_C * r), axis=0, keepdims=True)
        dlam_ref[...] = -dsp * jax.nn.sigmoid(-lam_v)
        dpr = dr * r * (1.0 - r)
        dpi = dig * ig * (1.0 - ig)
        dbra_ref[...] = jnp.sum(dpr, axis=0, keepdims=True)
        dbri_ref[...] = jnp.sum(dpi, axis=0, keepdims=True)
        ucb = uc.astype(BF16)
        dprb, dpib = dpr.astype(BF16), dpi.astype(BF16)
        dwra_ref[...] = _dot(ucb, dprb, TN).astype(BF16)
        dwri_ref[...] = _dot(ucb, dpib, TN).astype(BF16)
        duc = duc + _dot(dprb, wra.astype(BF16), NT) + _dot(dpib, wri.astype(BF16), NT)
        dcb_ref[...] = jnp.sum(duc, axis=0, keepdims=True)
        for j in range(CONV_K):
            dcw_ref[j:j + 1, :] = jnp.sum(duc * pad[5 + j:5 + j + S, :], axis=0, keepdims=True)
        g_scr[0:S, :] = duc
        g_scr[S:S + 8, :] = zero8
        du = jnp.zeros((S, HEAD), F32)
        for j in range(CONV_K):
            du = du + cw[j:j + 1, :] * g_scr[3 - j:3 - j + S, :]
        du_ref[...] = du.astype(BF16)

    w = LRU_NB * HEAD
    bf = jax.ShapeDtypeStruct((S, w), BF16)
    vec = jax.ShapeDtypeStruct((1, w), F32)
    mat = jax.ShapeDtypeStruct((LRU_NB, HEAD, HEAD), BF16)
    return pl.pallas_call(
        body, name=name, grid=(LRU_NB,),
        in_specs=[sp_["u"], sp_["gate"], sp_["blk"], sp_["blk"], sp_["cw"], sp_["vec"], sp_["mat"], sp_["vec"],
                  sp_["mat"], sp_["vec"], sp_["vec"]] + more_specs,
        out_specs=[sp_["blk"], sp_["blk"], sp_["cw"], sp_["vec"], sp_["mat"], sp_["vec"], sp_["mat"], sp_["vec"], sp_["vec"]],
        out_shape=[bf, bf, jax.ShapeDtypeStruct((CONV_K, w), F32), vec, mat, vec, mat, vec, vec],
        scratch_shapes=[pltpu.VMEM((S + 8, HEAD), F32), pltpu.VMEM((S, HEAD), F32), pltpu.VMEM((S, HEAD), F32),
                        pltpu.VMEM((S + 8, HEAD), F32), pltpu.VMEM((S + 8, HEAD), F32)],
        compiler_params=_params("parallel"),
    )(proj, proj, h, dy, conv_w, conv_b, w_ra, b_ra, w_ri, b_ri, lam, *more)


def _xattn_fwd(cq, ckv, g_cq, g_ck, name):
    S, w = cq.shape
    M = ckv.shape[0]
    tr = _row_tile(S)
    scale = 1.0 / math.sqrt(HEAD)

    def body(cq_ref, ckv_ref, gq_ref, gk_ref, o_ref):
        for h in range(XATT_H):
            sl = slice(h * HEAD, (h + 1) * HEAD)
            qn = _rms(cq_ref[:, sl], gq_ref[...]).astype(BF16)
            kn = _rms(ckv_ref[:, sl], gk_ref[...]).astype(BF16)
            v = ckv_ref[:, w + h * HEAD:w + (h + 1) * HEAD].astype(BF16)
            s = _dot(qn, kn, NT) * scale
            p = jnp.exp(s - jnp.max(s, axis=1, keepdims=True))
            p = p / jnp.sum(p, axis=1, keepdims=True)
            o_ref[:, sl] = _dot(p.astype(BF16), v).astype(BF16)

    gain = pl.BlockSpec((1, HEAD), lambda i: (0, 0))
    return pl.pallas_call(
        body, name=name, grid=(S // tr,),
        in_specs=[pl.BlockSpec((tr, w), lambda i: (i, 0)), pl.BlockSpec((M, 2 * w), lambda i: (0, 0)), gain, gain],
        out_specs=pl.BlockSpec((tr, w), lambda i: (i, 0)), out_shape=jax.ShapeDtypeStruct((S, w), BF16),
        compiler_params=_params("parallel"),
    )(cq, ckv, g_cq, g_ck)


def _xattn_bwd(cq, ckv, g_cq, g_ck, do, name):
    S, w = cq.shape
    M = ckv.shape[0]
    tr = _row_tile(S)
    nsteps = S // tr
    scale = 1.0 / math.sqrt(HEAD)

    def body(cq_ref, ckv_ref, gq_ref, gk_ref, do_ref, dcq_ref, dckv_ref, dgq_ref, dgk_ref, dkn_scr, dv_scr):
        step = pl.program_id(0)

        @pl.when(step == 0)
        def _():
            dkn_scr[...] = jnp.zeros_like(dkn_scr)
            dv_scr[...] = jnp.zeros_like(dv_scr)
            dgq_ref[...] = jnp.zeros_like(dgq_ref)

        for h in range(XATT_H):
            sl = slice(h * HEAD, (h + 1) * HEAD)
            q_raw = cq_ref[:, sl]
            qn = _rms(q_raw, gq_ref[...]).astype(BF16)
            kn = _rms(ckv_ref[:, sl], gk_ref[...]).astype(BF16)
            v = ckv_ref[:, w + h * HEAD:w + (h + 1) * HEAD].astype(BF16)
            s = _dot(qn, kn, NT) * scale
            p = jnp.exp(s - jnp.max(s, axis=1, keepdims=True))
            p = p / jnp.sum(p, axis=1, keepdims=True)
            dob = do_ref[:, sl].astype(BF16)
            dp = _dot(dob, v, NT)
            ds = p * (dp - jnp.sum(p * dp, axis=1, keepdims=True)) * scale
            dsb = ds.astype(BF16)
            dv_scr[:, sl] += _dot(p.astype(BF16), dob, TN)
            dkn_scr[:, sl] += _dot(dsb, qn, TN)
            dq, dgq = _rms_grad(q_raw, gq_ref[...], _dot(dsb, kn))
            dcq_ref[:, sl] = dq.astype(BF16)
            dgq_ref[...] += dgq

        @pl.when(step == nsteps - 1)
        def _():
            dgk = jnp.zeros((1, HEAD), F32)
            for h in range(XATT_H):
                sl = slice(h * HEAD, (h + 1) * HEAD)
                dk, dgk_h = _rms_grad(ckv_ref[:, sl], gk_ref[...], dkn_scr[:, sl])
                dckv_ref[:, sl] = dk.astype(BF16)
                dgk = dgk + dgk_h
            dckv_ref[:, w:2 * w] = dv_scr[...].astype(BF16)
            dgk_ref[...] = dgk

    gain = pl.BlockSpec((1, HEAD), lambda i: (0, 0))
    row = pl.BlockSpec((tr, w), lambda i: (i, 0))
    mem = pl.BlockSpec((M, 2 * w), lambda i: (0, 0))
    return pl.pallas_call(
        body, name=name, grid=(nsteps,), in_specs=[row, mem, gain, gain, row], out_specs=[row, mem, gain, gain],
        out_shape=[jax.ShapeDtypeStruct((S, w), BF16), jax.ShapeDtypeStruct((M, 2 * w), BF16),
                   jax.ShapeDtypeStruct((1, HEAD), F32), jax.ShapeDtypeStruct((1, HEAD), F32)],
        scratch_shapes=[pltpu.VMEM((M, w), F32), pltpu.VMEM((M, w), F32)],
        compiler_params=_params("arbitrary"),
    )(cq, ckv, g_cq, g_ck, do)


def _swiglu_fwd(gu, name, n_ct=4):
    S, w2 = gu.shape
    f = w2 // 2
    tc = f // n_ct
    tr = _row_tile(S)

    def body(g_ref, u_ref, o_ref):
        g = g_ref[...]
        o_ref[...] = (g * jax.nn.sigmoid(g) * u_ref[...]).astype(BF16)

    return pl.pallas_call(
        body, name=name, grid=(S // tr, n_ct),
        in_specs=[pl.BlockSpec((tr, tc), lambda i, j: (i, j)), pl.BlockSpec((tr, tc), lambda i, j: (i, j + n_ct))],
        out_specs=pl.BlockSpec((tr, tc), lambda i, j: (i, j)), out_shape=jax.ShapeDtypeStruct((S, f), BF16),
        compiler_params=_params("parallel", "parallel"),
    )(gu, gu)


def _swiglu_bwd(gu, dact, name, n_ct=4):
    S, w2 = gu.shape
    f = w2 // 2
    tc = f // n_ct
    tr = _row_tile(S)

    steps = (S // tr) * n_ct

    def body(g_ref, u_ref, da_ref, o_hbm, buf, sems):
        i, j = pl.program_id(0), pl.program_id(1)
        step = i * n_ct + j
        slot = step % 2

        def put(half):
            cols = pl.ds(pl.multiple_of(half * f + j * tc, 128), tc)
            return pltpu.make_async_copy(buf.at[slot, half], o_hbm.at[pl.ds(pl.multiple_of(i * tr, 8), tr), cols],
                                         sems.at[slot, half])

        @pl.when(step >= 2)
        def _():
            put(0).wait()
            put(1).wait()

        g = g_ref[...]
        sg = jax.nn.sigmoid(g)
        da = da_ref[...]
        buf[slot, 0] = (da * u_ref[...] * (sg * (1.0 + g * (1.0 - sg)))).astype(BF16)
        buf[slot, 1] = (da * (g * sg)).astype(BF16)
        put(0).start()
        put(1).start()

        @pl.when(step == steps - 1)
        def _():
            put(0).wait()
            put(1).wait()
            if steps > 1:
                for half in (0, 1):
                    pltpu.make_async_copy(buf.at[1 - slot, half], o_hbm.at[pl.ds(0, tr), pl.ds(0, tc)],
                                          sems.at[1 - slot, half]).wait()

    return pl.pallas_call(
        body, name=name, grid=(S // tr, n_ct),
        in_specs=[pl.BlockSpec((tr, tc), lambda i, j: (i, j)), pl.BlockSpec((tr, tc), lambda i, j: (i, j + n_ct)),
                  pl.BlockSpec((tr, tc), lambda i, j: (i, j))],
        out_specs=ANY, out_shape=jax.ShapeDtypeStruct((S, w2), BF16),
        scratch_shapes=[pltpu.VMEM((2, 2, tr, tc), BF16), pltpu.SemaphoreType.DMA((2, 2))],
        compiler_params=_params("arbitrary", "arbitrary"),
    )(gu, gu, dact)


def _adamw_math(w, g, m, v):
    m = ADAM_B1 * m + (1.0 - ADAM_B1) * g
    v = ADAM_B2 * v + (1.0 - ADAM_B2) * (g * g)
    m_hat = m / (1.0 - ADAM_B1 ** ADAM_STEP)
    v_hat = v / (1.0 - ADAM_B2 ** ADAM_STEP)
    delta = -ADAM_LR * (m_hat / (jnp.sqrt(v_hat) + ADAM_EPS) + ADAM_WD * w)
    return delta, m, v


def _tile2(R, C, elems):
    if R % 16 == 0:
        return _row_tile(R, cap=max(16, elems // C // 16 * 16)), C
    tc = 128
    while C % (2 * tc) == 0 and R * 2 * tc <= elems:
        tc *= 2
    return R, tc


def _reduce_adamw(own, land, chip, w, m, v, name):
    _, R, C = own.shape
    tr, tc = _tile2(R, C, 1 << 18)

    def body(s_ref, o_ref, l1_ref, l2_ref, l3_ref, w_ref, m_ref, v_ref, g_ref, d_ref, nm_ref, nv_ref):
        g = o_ref[...].astype(F32) + l1_ref[...].astype(F32) + l2_ref[...].astype(F32) + l3_ref[...].astype(F32)
        g_ref[...] = g
        d_ref[...], nm_ref[...], nv_ref[...] = _adamw_math(w_ref[...], g, m_ref[...], v_ref[...])

    def part(flip):
        return pl.BlockSpec((None, tr, tc), lambda i, j, s: (s[0] ^ flip, i, j))

    if w.ndim == 3:
        blk = pl.BlockSpec((tr, None, tc), lambda i, j, s: (i, 0, j))
    else:
        blk = pl.BlockSpec((tr, tc), lambda i, j, s: (i, j))
    sds = jax.ShapeDtypeStruct(w.shape, F32)
    return pl.pallas_call(
        body, name=name,
        grid_spec=pltpu.PrefetchScalarGridSpec(
            num_scalar_prefetch=1, grid=(R // tr, C // tc),
            in_specs=[part(0), part(1), part(2), part(3), blk, blk, blk], out_specs=[blk] * 4),
        out_shape=[sds] * 4, compiler_params=_params("parallel", "parallel"),
    )(chip, own, land, land, land, w, m, v)


def _pair_sum(g8, recv, core, name):
    _, R, C = g8.shape
    tr, tc = _tile2(R, C, 1 << 19)

    def body(c_ref, a_ref, b_ref, o_ref):
        o_ref[...] = (a_ref[...].astype(F32) + b_ref[...].astype(F32)).astype(BF16)

    return pl.pallas_call(
        body, name=name,
        grid_spec=pltpu.PrefetchScalarGridSpec(
            num_scalar_prefetch=1, grid=(N_CHIP, R // tr, C // tc),
            in_specs=[pl.BlockSpec((None, tr, tc), lambda q, i, j, c: (2 * q + c[0], i, j)),
                      pl.BlockSpec((None, tr, tc), lambda q, i, j, c: (q, i, j))],
            out_specs=pl.BlockSpec((None, tr, tc), lambda q, i, j, c: (q, i, j))),
        out_shape=jax.ShapeDtypeStruct((N_CHIP, R, C), BF16),
        compiler_params=_params("parallel", "parallel", "parallel"),
    )(core, g8, recv)


def _place():
    return lax.axis_index("x"), lax.axis_index("y"), lax.axis_index("c")


def _land_with_own(shard, dev, name):
    R, C = shard.shape
    tr, tc = _tile2(R, C, 1 << 19)

    def body(d_ref, s_ref, o_ref):
        o_ref[...] = s_ref[...]

    return pl.pallas_call(
        body, name=name,
        grid_spec=pltpu.PrefetchScalarGridSpec(
            num_scalar_prefetch=1, grid=(R // tr, C // tc),
            in_specs=[pl.BlockSpec((tr, tc), lambda i, j, d: (i, j))],
            out_specs=pl.BlockSpec((None, tr, tc), lambda i, j, d: (d[0], i, j))),
        out_shape=jax.ShapeDtypeStruct((N_DEV, R, C), shard.dtype),
        compiler_params=_params("parallel", "parallel"),
    )(dev, shard)


def _all_gather(shards, lands, name):
    n = len(shards)

    def body(*refs):
        ins, outs = refs[:n], refs[2 * n:3 * n]
        send_sems, recv_sems = refs[3 * n:]
        x, y, c = _place()
        me, sibling = (x, y, c), (x, y, 1 - c)
        chips = [(1 - x, y), (x, 1 - y), (1 - x, 1 - y)]

        def copy(a, k, block, to, src=None):
            dst = outs[a].at[4 * block[0] + 2 * block[1] + block[2]]
            return pltpu.make_async_remote_copy(
                src_ref=dst if src is None else src, dst_ref=dst, send_sem=send_sems.at[a, k],
                recv_sem=recv_sems.at[a, k], device_id=to, device_id_type=MESH)

        sent = []
        for a in range(n):
            sent.append(copy(a, 0, me, sibling, src=ins[a]))
            sent += [copy(a, 1 + j, me, (*chip, c), src=ins[a]) for j, chip in enumerate(chips)]
        for cp in sent:
            cp.start()
        for j, chip in enumerate(chips):
            for a in range(n):
                copy(a, 1 + j, (*chip, c), me).wait_recv()
                fwd = copy(a, 4 + j, (*chip, c), sibling)
                fwd.start()
                sent.append(fwd)
        for a in range(n):
            copy(a, 0, sibling, me).wait_recv()
        for j, chip in enumerate(chips):
            for a in range(n):
                copy(a, 4 + j, (*chip, 1 - c), me).wait_recv()
        for cp in sent:
            cp.wait_send()

    return pl.pallas_call(
        body, name=name, in_specs=[ANY] * (2 * n), out_specs=[ANY] * n,
        out_shape=[jax.ShapeDtypeStruct(l.shape, l.dtype) for l in lands],
        input_output_aliases={n + a: a for a in range(n)},
        scratch_shapes=[pltpu.SemaphoreType.DMA((n, 7)), pltpu.SemaphoreType.DMA((n, 7))],
    )(*shards, *lands)


def _pair_plan(srcs, lands):
    x, y, c = _place()
    plan = []
    for a in range(len(srcs)):
        for q in range(N_CHIP):
            plan.append((srcs[a].at[2 * q + 1 - c], lands[a].at[q], lands[a].at[q], (x, y, 1 - c)))
    return plan


def _chip_plan(srcs, lands):
    x, y, c = _place()
    mine = 2 * x + y
    plan = []
    for a in range(len(srcs)):
        for px, py in [(1 - x, y), (x, 1 - y), (1 - x, 1 - y)]:
            peer = 2 * px + py
            plan.append((srcs[a].at[peer], lands[a].at[mine], lands[a].at[peer], (px, py, c)))
    return plan


def _gather_plan(srcs, lands):
    x, y, c = _place()
    mine = 4 * x + 2 * y + c
    plan = []
    for a in range(len(srcs)):
        for px, py, pc in [(x, y, 1 - c), (1 - x, y, c), (x, 1 - y, c), (1 - x, 1 - y, c)]:
            plan.append((srcs[a], lands[a].at[mine], lands[a].at[4 * px + 2 * py + pc], (px, py, pc)))
    return plan


def _remote(src, dst, send_sem, recv_sem, peer):
    return pltpu.make_async_remote_copy(src_ref=src, dst_ref=dst, send_sem=send_sem, recv_sem=recv_sem,
                                        device_id=peer, device_id_type=MESH)


HBM = pl.BlockSpec(memory_space=pltpu.HBM)
SEMS = pl.BlockSpec(memory_space=pltpu.SEMAPHORE)
DATAFLOW = pltpu.SideEffectType.DATAFLOW_SIDE_EFFECTING


def _split_start(srcs, lands, after, plan_fn, per_array, name):
    n = len(srcs)
    ncopy = per_array * n
    after = list(after) if isinstance(after, (list, tuple)) else [after]

    def body(*refs):
        send_sems, recv_sems = refs[2 * n + len(after)], refs[2 * n + len(after) + 1]
        token = refs[-1]
        for k, (src, dst, _, peer) in enumerate(plan_fn(refs[:n], refs[n:2 * n])):
            _remote(src, dst, send_sems.at[k], recv_sems.at[k], peer).start()
        token[...] = jnp.zeros_like(token)

    thru = [pltpu.HBM(a.shape, a.dtype) for a in (*srcs, *lands)]
    out = pl.pallas_call(
        body, name=name,
        out_shape=(pltpu.SemaphoreType.DMA((ncopy,)), pltpu.SemaphoreType.DMA((ncopy,)), *thru,
                   jax.ShapeDtypeStruct((8, 128), F32)),
        in_specs=[HBM] * (2 * n) + [ANY] * len(after),
        out_specs=(SEMS, SEMS, *[HBM] * (2 * n), pl.BlockSpec(memory_space=pltpu.VMEM)),
        input_output_aliases={i: 2 + i for i in range(2 * n)},
        compiler_params=pltpu.CompilerParams(has_side_effects=DATAFLOW),
    )(*[pltpu.with_memory_space_constraint(a, pltpu.HBM) for a in (*srcs, *lands)], *after)
    return out[0], out[1], list(out[2:2 + n]), list(out[2 + n:2 + 2 * n]), out[-1]


def _split_wait(send_sems, recv_sems, srcs, lands, after, plan_fn, name):
    n = len(srcs)
    after = list(after) if isinstance(after, (list, tuple)) else [after]

    def body(*refs):
        send_ref, recv_ref = refs[2 * n], refs[2 * n + 1]
        for k, (src, dst, mine, peer) in enumerate(plan_fn(refs[:n], refs[n:2 * n])):
            _remote(src, dst, send_ref.at[k], recv_ref.at[k], peer).wait_send()
            _remote(src, mine, send_ref.at[k], recv_ref.at[k], peer).wait_recv()

    thru = [pltpu.HBM(a.shape, a.dtype) for a in (*srcs, *lands)]
    out = pl.pallas_call(
        body, name=name, out_shape=tuple(thru),
        in_specs=[HBM] * (2 * n) + [SEMS, SEMS] + [ANY] * len(after), out_specs=tuple([HBM] * (2 * n)),
        input_output_aliases={i: i for i in range(2 * n)},
        compiler_params=pltpu.CompilerParams(has_side_effects=DATAFLOW),
    )(*srcs, *lands, send_sems, recv_sems, *after)
    return list(out[:n]), list(out[n:])


def _gather_forward(lands, name):
    n = len(lands)

    def body(*refs):
        landed, out = refs[:n], refs[n:2 * n]
        send_sems, recv_sems = refs[2 * n:]
        x, y, c = _place()
        chips = [(1 - x, y), (x, 1 - y), (1 - x, 1 - y)]
        sent = []
        for a in range(n):
            for j, (px, py) in enumerate(chips):
                blk = 4 * px + 2 * py + c
                sent.append(_remote(landed[a].at[blk], out[a].at[blk], send_sems.at[a, j], recv_sems.at[a, j], (x, y, 1 - c)))
        for cp in sent:
            cp.start()
        for a in range(n):
            for j, (px, py) in enumerate(chips):
                blk = 4 * px + 2 * py + 1 - c
                _remote(landed[a].at[blk], out[a].at[blk], send_sems.at[a, j], recv_sems.at[a, j], (x, y, 1 - c)).wait_recv()
        for cp in sent:
            cp.wait_send()

    return pl.pallas_call(
        body, name=name, in_specs=[ANY] * n, out_specs=[ANY] * n,
        out_shape=[jax.ShapeDtypeStruct(l.shape, l.dtype) for l in lands],
        input_output_aliases={a: a for a in range(n)},
        scratch_shapes=[pltpu.SemaphoreType.DMA((n, 3)), pltpu.SemaphoreType.DMA((n, 3))],
    )(*lands)


def _pack_rows(groups, name):
    n = len(groups[0])
    rows = -(-n // 8) * 8
    width = max(v.shape[1] for v in groups[0])

    def body(*refs):
        ins, outs = refs[:n * len(groups)], refs[n * len(groups):]
        for gi, o_ref in enumerate(outs):
            o_ref[...] = jnp.zeros_like(o_ref)
            for r in range(n):
                v_ref = ins[gi * n + r]
                o_ref[r:r + 1, 0:v_ref.shape[1]] = v_ref[...]

    return pl.pallas_call(
        body, name=name, out_shape=[jax.ShapeDtypeStruct((rows, width), F32)] * len(groups), compiler_params=_params(),
    )(*[v for g in groups for v in g])


def _unpack_rows(packs, like, name):
    n = len(like)

    def body(*refs):
        ins, outs = refs[:len(packs)], refs[len(packs):]
        for pi, p_ref in enumerate(ins):
            for r in range(n):
                o_ref = outs[pi * n + r]
                o_ref[...] = p_ref[r:r + 1, 0:o_ref.shape[1]]

    flat = pl.pallas_call(
        body, name=name, out_shape=[jax.ShapeDtypeStruct(a.shape, F32) for _ in packs for a in like],
        compiler_params=_params(),
    )(*packs)
    return [flat[pi * n:(pi + 1) * n] for pi in range(len(packs))]


def _sum_adamw(parts, w, m, v, name, col=None):
    P, R, _ = parts.shape
    C = w.shape[1]
    tr = _row_tile(R, cap=512, mult=8)

    def body(c_ref, p_ref, w_ref, m_ref, v_ref, g_ref, d_ref, nm_ref, nv_ref):
        g = p_ref[0].astype(F32)
        for q in range(1, P):
            g = g + p_ref[q].astype(F32)
        g_ref[...] = g
        d_ref[...], nm_ref[...], nv_ref[...] = _adamw_math(w_ref[...], g, m_ref[...], v_ref[...])

    blk = pl.BlockSpec((tr, C), lambda i, c: (i, 0))
    sds = jax.ShapeDtypeStruct((R, C), F32)
    at = jnp.zeros((1,), jnp.int32) if col is None else col
    return pl.pallas_call(
        body, name=name,
        grid_spec=pltpu.PrefetchScalarGridSpec(
            num_scalar_prefetch=1, grid=(R // tr,),
            in_specs=[pl.BlockSpec((P, tr, C), lambda i, c: (0, i, c[0])), blk, blk, blk], out_specs=[blk] * 4),
        out_shape=[sds] * 4, compiler_params=_params("parallel"),
    )(at, parts, w, m, v)


def _local_step(x, mem, target, sp, W, want, done, paired):
    h0 = _rms_fwd([x], [sp["g_mix"]], "rms_mix")
    proj = _mm(h0, W["main_t"], "nt", F32, "proj_main", tm=2048, tn=512)
    f_logit = _mm(h0, W["f_t"], "nt", F32, "proj_f", tm=2048)
    qn, kn, vb = _qk_fwd(proj, sp["g_q"], sp["g_k"], "qk_norm")
    b_pad = jnp.pad(sp["b_f"], ((0, 0), (0, HEAD - FOX_H)))
    c_col, c_row = _fgate_fwd(f_logit, b_pad, "forget_cumsum")
    o_fox, lse = _fox_fwd(qn, kn, vb, c_col, c_row, "fox_fwd")
    lru_p = (sp["conv_w"], sp["conv_b"], sp["w_ra"], sp["b_ra"], sp["w_ri"], sp["b_ri"], sp["lam"])
    y_lru, h_lru = _lru_fwd(proj, *lru_p, "lru_fwd")
    (w_out, w_cq, w_ckv, w_co), tok = want("attn", y_lru)
    mix = _rms_fwd([o_fox, y_lru], [sp["g_fox_out"], sp["g_lru_out"]], "rms_mix_out", after=tok)
    x1 = _mm(mix, w_out, "nn", F32, "out_proj", add=x)
    hq = _rms_fwd([x1], [sp["g_xattn"]], "rms_xattn")
    mn = _rms_fwd([mem], [sp["g_mem"]], "rms_mem")
    cq = _mm(hq, w_cq, "nn", F32, "xattn_q", tm=2048)
    ckv = _mm(mn, w_ckv, "nn", F32, "xattn_kv")
    o_x = _xattn_fwd(cq, ckv, sp["g_cq"], sp["g_ck"], "xattn_fwd")
    co_w = w_co.shape[2]
    x2 = _mm(o_x, w_co, "nn", F32, "xattn_out", add=x1, tm=2048, tn=co_w)
    hf = _rms_fwd([x2], [sp["g_ffn"]], "rms_ffn")
    (w_gu,), _ = want("gate_up", hf)
    gu_w = w_gu.shape[2]
    gu = _mm(hf, w_gu, "nn", F32, "ffn_up", tn=gu_w)
    act = _swiglu_fwd(gu, "swiglu_fwd")
    (w_down,), _ = want("down", act)
    x3 = _mm(act, w_down, "nn", F32, "ffn_down", add=x2, tk=w_down.shape[0] // 2)
    dy, dyb, sq = _loss_head(x3, target, "loss_head")
    dact = _mm(dyb, w_down, "nt", F32, "ffn_down_dx", tm=2048, tn=512)
    g_down = _mm(act, dyb, "tn", BF16, "ffn_down_dw", tm=w_down.shape[0] // 4, tn=2048)
    dgu = _swiglu_bwd(gu, dact, "swiglu_bwd")
    g_gu = _mm(hf, dgu, "tn", BF16, "ffn_up_dw", out_blocks=N_DEV, tn=gu_w)
    tok = done("ffn", [g_gu, g_down])
    dhf = _mm(dgu, w_gu, "nt", F32, "ffn_up_dx", tn=2048, tk=gu_w, after=tok)
    tok = paired("ffn", dhf)
    dx2, dx2b, dg_ffn = _rms_bwd(x2, sp["g_ffn"], dhf, "rms_ffn_bwd", res=dy, after=tok)
    d_ox = _mm(dx2b, w_co, "nt", F32, "xattn_out_dx", tm=2048, tk=co_w)
    g_co = _mm(o_x, dx2b, "tn", BF16, "xattn_out_dw", out_blocks=N_DEV, tn=co_w)
    dcq, dckv, dg_cq, dg_ck = _xattn_bwd(cq, ckv, sp["g_cq"], sp["g_ck"], d_ox, "xattn_bwd")
    dhq = _mm(dcq, w_cq, "nt", F32, "xattn_q_dx", tm=2048)
    g_cq = _mm(hq, dcq, "tn", BF16, "xattn_q_dw")
    dmn = _mm(dckv, w_ckv, "nt", F32, "xattn_kv_dx")
    g_ckv = _mm(mn, dckv, "tn", BF16, "xattn_kv_dw")
    (dg_mem,) = _rms_bwd(mem, sp["g_mem"], dmn, "rms_mem_bwd", want_dx=False)
    dx1, dx1b, dg_xattn = _rms_bwd(x1, sp["g_xattn"], dhq, "rms_xattn_bwd", res=dx2)
    dmix = _mm(dx1b, w_out, "nt", F32, "out_proj_dx")
    g_out = _mm(mix, dx1b, "tn", BF16, "out_proj_dw", tn=2048)
    tok = done("attn", [g_out, g_cq, g_ckv, g_co])
    do_fox, _, dg_fox_out = _rms_bwd(o_fox, sp["g_fox_out"], dmix, "rms_fox_out_bwd", dy_col=0, after=tok)
    dy_lru, _, dg_lru_out = _rms_bwd(y_lru, sp["g_lru_out"], dmix, "rms_lru_out_bwd", dy_col=1)
    tok = paired("attn", dy_lru)
    du, dgate, dconv_w, dconv_b, dw_ra, db_ra, dw_ri, db_ri, dlam = _lru_bwd(proj, h_lru, dy_lru, *lru_p, "lru_bwd", after=tok)
    dqn, dkn, dv, dc_col, dc_row = _fox_bwd(qn, kn, vb, c_col, c_row, lse, o_fox, do_fox, "fox_bwd")
    dq, dk, dg_q, dg_k = _qk_bwd(proj, sp["g_q"], sp["g_k"], dqn, dkn, "qk_norm_bwd")
    df, db_f = _fgate_bwd(f_logit, b_pad, dc_col, dc_row, "forget_cumsum_bwd")
    dproj = jnp.concatenate([dq, dk, dv, du, dgate], axis=1)
    g_main_t = _mm(dproj, h0, "tn", BF16, "proj_main_dw", tn=2048)
    g_f_t = _mm(df, h0, "tn", BF16, "proj_f_dw", tn=2048)
    tok = done("w_in", [g_main_t, g_f_t])
    dh_f = _mm(df, W["f_t"], "nn", F32, "proj_f_dx", tm=2048)
    dh0 = _mm(dproj, W["main_t"], "nn", F32, "proj_main_dx", add=dh_f, tk=dproj.shape[1] // 2, after=tok)
    tok = paired("w_in", dh0)
    grad_x, _, dg_mix = _rms_bwd(x, sp["g_mix"], dh0, "rms_mix_bwd", res=dx1, after=tok)
    small = dict(g_mix=dg_mix, b_f=db_f[:, :FOX_H], g_q=dg_q, g_k=dg_k, conv_w=dconv_w, conv_b=dconv_b, w_ra=dw_ra,
                 b_ra=db_ra, w_ri=dw_ri, b_ri=db_ri, lam=dlam, g_fox_out=dg_fox_out, g_lru_out=dg_lru_out,
                 g_xattn=dg_xattn, g_mem=dg_mem, g_cq=dg_cq, g_ck=dg_ck, g_ffn=dg_ffn)
    return sq, grad_x, small


BIG = ("w_in", "w_out", "w_cq", "w_ckv", "w_co", "w_gate_up", "w_down")
SMALL = ("g_mix", "b_f", "g_q", "g_k", "conv_b", "w_ra", "b_ra", "w_ri", "b_ri", "lam", "g_fox_out", "g_lru_out",
         "g_xattn", "g_mem", "g_cq", "g_ck", "g_ffn")
ORDER = ("g_mix", "w_in", "b_f", "g_q", "g_k", "conv_w", "conv_b", "w_ra", "b_ra", "w_ri", "b_ri", "lam", "g_fox_out",
         "g_lru_out", "w_out", "g_xattn", "g_mem", "w_cq", "w_ckv", "g_cq", "g_ck", "w_co", "g_ffn", "w_gate_up", "w_down")


def kernel(x, mem, g_mix, w_in, b_f, g_q, g_k, conv_w, conv_b, w_ra, b_ra, w_ri, b_ri, lam, g_fox_out, g_lru_out, w_out, g_xattn, g_mem, w_cq, w_ckv, g_cq, g_ck, w_co, g_ffn, w_gate_up, w_down, loss_target, m_g_mix, m_w_in, m_b_f, m_g_q, m_g_k, m_conv_w, m_conv_b, m_w_ra, m_b_ra, m_w_ri, m_b_ri, m_lam, m_g_fox_out, m_g_lru_out, m_w_out, m_g_xattn, m_g_mem, m_w_cq, m_w_ckv, m_g_cq, m_g_ck, m_w_co, m_g_ffn, m_w_gate_up, m_w_down, v_g_mix, v_w_in, v_b_f, v_g_q, v_g_k, v_conv_w, v_conv_b, v_w_ra, v_b_ra, v_w_ri, v_b_ri, v_lam, v_g_fox_out, v_g_lru_out, v_w_out, v_g_xattn, v_g_mem, v_w_cq, v_w_ckv, v_g_cq, v_g_ck, v_w_co, v_g_ffn, v_w_gate_up, v_w_down):
    given = dict(locals())
    w = {n: given[n] for n in ORDER}
    m = {n: given["m_" + n] for n in ORDER}
    v = {n: given["v_" + n] for n in ORDER}
    D = x.shape[2]
    fw = FOX_H * HEAD
    dev_index = 4 * lax.axis_index("x") + 2 * lax.axis_index("y") + lax.axis_index("c")
    dev = jnp.reshape(dev_index, (1,)).astype(jnp.int32)
    core = jnp.reshape(lax.axis_index("c"), (1,)).astype(jnp.int32)
    chip = jnp.reshape(2 * lax.axis_index("x") + lax.axis_index("y"), (1,)).astype(jnp.int32)

    def shard(d, n):
        return jnp.transpose(d[n], (2, 0, 1)) if n == "w_in" else d[n][0]

    def unshard(a, n):
        return jnp.transpose(a, (1, 2, 0)) if n == "w_in" else a[None]

    gather_groups = dict(attn=("w_out", "w_cq", "w_ckv", "w_co"), gate_up=("w_gate_up",), down=("w_down",))
    reduce_groups = dict(ffn=("w_gate_up", "w_down"), attn=("w_out", "w_cq", "w_ckv", "w_co"), w_in=("w_in",))
    column_blocked = ("w_co", "w_gate_up")
    flying = {}

    def land(a, tag):
        return _land_with_own(a, dev, "own_" + tag)

    def launch(group, after):
        shards = [shard(w, n).astype(BF16) for n in gather_groups[group]]
        lands = [land(s, n) for s, n in zip(shards, gather_groups[group])]
        flying[group] = _split_start(shards, lands, after, _gather_plan, 4, "gather_" + group + "_start")
        return flying[group][4]

    def want(group, after):
        send, recv, shards, lands, _ = flying.pop(group)
        _, lands = _split_wait(send, recv, shards, lands, after, _gather_plan, "gather_" + group + "_wait")
        full = _gather_forward(lands, "gather_" + group + "_forward")
        tok = launch("down", full[0]) if group == "attn" else None
        return [g if n in column_blocked else g.reshape(-1, g.shape[2]) for n, g in zip(gather_groups[group], full)], tok

    first = [shard(w, "w_in").astype(BF16).reshape(-1, D), conv_w[0]]
    g_in, g_conv = _all_gather(first, [land(first[0], "w_in"), land(first[1], "conv_w")], "gather_w_in")
    tok = launch("gate_up", launch("attn", g_conv))
    wt_in = g_in.reshape(-1, D)
    W = dict(main_t=jnp.concatenate([wt_in[:3 * fw], wt_in[3 * fw + FOX_H:]], axis=0),
             f_t=jnp.pad(wt_in[3 * fw:3 * fw + FOX_H], ((0, HEAD - FOX_H), (0, 0))))
    sp = {n: w[n] for n in SMALL if n not in ("w_ra", "w_ri")}
    sp["w_ra"], sp["w_ri"] = w_ra[0], w_ri[0]
    sp["conv_w"] = jnp.transpose(g_conv, (1, 0, 2)).reshape(CONV_K, -1)
    sp["g_mix"] = sp["g_mix"] + tok[0, 0]

    pairing, reducing = {}, {}

    def done(group, grads):
        if group == "w_in":
            g_main_t, g_f_t = grads
            grads = [jnp.concatenate([g_main_t[:3 * fw], g_f_t[:FOX_H], g_main_t[3 * fw:]], axis=0)]
        g8 = [g if g.ndim == 3 else g.reshape(N_DEV, -1, g.shape[1]) for g in grads]
        lands = [lax.empty((N_CHIP,) + g.shape[1:], g.dtype) for g in g8]
        pairing[group] = _split_start(g8, lands, chip, _pair_plan, N_CHIP, "reduce_" + group + "_pair_start")
        return pairing[group][4]

    def paired(group, after):
        send, recv, g8, lands, _ = pairing.pop(group)
        g8, from_sibling = _split_wait(send, recv, g8, lands, after, _pair_plan, "reduce_" + group + "_pair_wait")
        p4 = [_pair_sum(g, r, core, "reduce_pair_sum_" + n) for g, r, n in zip(g8, from_sibling, reduce_groups[group])]
        lands = [lax.empty(p.shape, p.dtype) for p in p4]
        reducing[group] = _split_start(p4, lands, chip, _chip_plan, 3, "reduce_" + group + "_start")
        return reducing[group][4]

    def finish(group, after):
        send, recv, p4, lands, _ = reducing.pop(group)
        p4, lands = _split_wait(send, recv, p4, lands, after, _chip_plan, "reduce_" + group + "_wait")
        return {n: tuple(unshard(r, n) for r in _reduce_adamw(p, l, chip, shard(w, n), shard(m, n), shard(v, n), "adamw_" + n))
                for n, p, l in zip(reduce_groups[group], p4, lands)}

    sq, grad_x, gs = _local_step(x[0], mem[0], loss_target[0], sp, W, want, done, paired)
    loss = lax.psum(0.5 * sq[0, 0] / D, AXES)

    vectors = tuple(n for n in SMALL if n not in ("w_ra", "w_ri"))
    mine = [_pack_rows([[gs[n] for n in vectors]], "pack_small_grads")[0], gs["w_ra"].reshape(-1, HEAD),
            gs["w_ri"].reshape(-1, HEAD), gs["conv_w"]]
    lands = [land(a, "small_grads_%d" % k) for k, a in enumerate(mine)]
    s_send, s_recv, s_src, s_land, tok = _split_start(mine, lands, grad_x, _gather_plan, 4, "gather_small_start")
    out = finish("ffn", tok)
    out.update(finish("attn", tok))
    updated = [r for n in reduce_groups["ffn"] + reduce_groups["attn"] for r in out[n]]
    _, s_land = _split_wait(s_send, s_recv, s_src, s_land, updated, _gather_plan, "gather_small_wait")
    all_vec, all_ra, all_ri, all_conv = _gather_forward(s_land, "gather_small_forward")

    state = _pack_rows([[d[n] for n in vectors] for d in (w, m, v)], "pack_small_state")
    upd = _sum_adamw(all_vec, *state, "adamw_small_vectors")
    per_vector = _unpack_rows(upd, [w[n] for n in vectors], "unpack_small")
    for r, n in enumerate(vectors):
        out[n] = tuple(per_vector[k][r] for k in range(4))
    for n, parts in (("w_ra", all_ra), ("w_ri", all_ri)):
        res = _sum_adamw(parts, *[d[n].reshape(-1, HEAD) for d in (w, m, v)], "adamw_" + n)
        out[n] = tuple(r.reshape(w[n].shape) for r in res)
    out["conv_w"] = tuple(r[None] for r in _sum_adamw(all_conv, conv_w[0], m_conv_w[0], v_conv_w[0], "adamw_conv_w", col=dev))
    out.update(finish("w_in", upd[1]))

    return (loss, grad_x[None], *[out[n][0] for n in ORDER], *[out[n][1] for n in ORDER],
            *[out[n][2] for n in ORDER], *[out[n][3] for n in ORDER])
```

```python
import functools
import math

import jax
import jax.numpy as jnp
from jax import lax
from jax.experimental import pallas as pl
from jax.experimental.pallas import tpu as pltpu

F32 = jnp.float32
BF16 = jnp.bfloat16
MESH = pl.DeviceIdType.MESH
AXES = ("x", "y", "c")
N_DEV = 8
N_CHIP = 4

HEAD = 128
FOX_H = 8
XATT_H = 4
LRU_NB = 8
CONV_K = 4
LRU_C = 8.0
RMS_EPS = 1e-6
ATT_T = 1024
CUM_T = 256
ROW_T = 256
NEG = -1e30
V7X_VMEM_LIMIT = 48 * 1024 * 1024

ADAM_LR = 0.001
ADAM_B1 = 0.9
ADAM_B2 = 0.999
ADAM_EPS = 1e-08
ADAM_WD = 0.01
ADAM_STEP = 10

NT = (((1,), (1,)), ((), ()))
TN = (((0,), (0,)), ((), ()))
NN = (((1,), (0,)), ((), ()))

ANY = pl.BlockSpec(memory_space=pl.ANY)


def _behind(after):
    return ([], []) if after is None else ([ANY], [after])


def _params(*sem):
    return pltpu.CompilerParams(dimension_semantics=sem or None, vmem_limit_bytes=V7X_VMEM_LIMIT)


def _dot(a, b, dn=NN):
    return lax.dot_general(a, b, dn, preferred_element_type=F32)


def _rms(x, g):
    r = lax.rsqrt(jnp.mean(x * x, axis=-1, keepdims=True) + RMS_EPS)
    return x * r * g


def _rms_grad(x, g, dy):
    r = lax.rsqrt(jnp.mean(x * x, axis=-1, keepdims=True) + RMS_EPS)
    xh = x * r
    dxh = dy * g
    dx = r * (dxh - xh * jnp.mean(dxh * xh, axis=-1, keepdims=True))
    return dx, jnp.sum(dy * xh, axis=0, keepdims=True)


def _gelu(x):
    k = math.sqrt(2.0 / math.pi)
    return 0.5 * x * (1.0 + jnp.tanh(k * (x + 0.044715 * (x * x * x))))


def _gelu_grad(x):
    k = math.sqrt(2.0 / math.pi)
    t = jnp.tanh(k * (x + 0.044715 * (x * x * x)))
    return 0.5 * (1.0 + t) + 0.5 * x * (1.0 - t * t) * k * (1.0 + 3.0 * 0.044715 * x * x)


def _one_minus_exp(z):
    series = -(z + 0.5 * z * z + (1.0 / 6.0) * z * z * z)
    return jnp.where(z > -1e-3, series, 1.0 - jnp.exp(z))


def _row_tile(rows, cap=ROW_T, mult=16):
    t = min(rows, cap)
    while rows % t or (t % mult and t != rows):
        t -= 1
    return t


def _mat_dims(z):
    return (z.shape[-2], z.shape[-1] * (z.shape[0] if z.ndim == 3 else 1))


def _mat_spec(arr, rblk, cblk, rsel, csel):
    if arr.ndim == 2:
        return pl.BlockSpec((rblk, cblk), lambda i, j, k: ((i, j, k)[rsel], (i, j, k)[csel]))
    nw = arr.shape[2]
    assert nw % cblk == 0, (arr.shape, cblk)
    per = nw // cblk
    return pl.BlockSpec((None, rblk, cblk),
                        lambda i, j, k: ((i, j, k)[csel] // per, (i, j, k)[rsel], (i, j, k)[csel] % per))


def _mm(a, b, mode, out_dtype, name, add=None, out_blocks=None, tm=1024, tn=1024, tk=None, after=None):
    ar, ac = _mat_dims(a)
    br, bc = _mat_dims(b)
    if mode == "nn":
        M, K, N = ar, ac, bc
        assert br == K
    elif mode == "nt":
        M, K, N = ar, ac, br
        assert bc == K
    else:
        M, K, N = ac, ar, bc
        assert br == K
    tm, tn = min(tm, M), min(tn, N)
    tk = K if tk is None or mode == "tn" else min(tk, K)
    assert M % tm == 0 and N % tn == 0 and K % tk == 0, (name, M, N, K, tm, tn, tk)
    nk = K // tk
    nj = N // tn
    if mode == "nn":
        specs = [_mat_spec(a, tm, tk, 0, 2), _mat_spec(b, tk, tn, 2, 1)]
        dn = NN
    elif mode == "nt":
        specs = [_mat_spec(a, tm, tk, 0, 2), _mat_spec(b, tn, tk, 1, 2)]
        dn = NT
    else:
        specs = [_mat_spec(a, tk, tm, 2, 0), _mat_spec(b, tk, tn, 2, 1)]
        dn = TN
    args = [a, b]
    if add is not None:
        specs.append(pl.BlockSpec((tm, tn), lambda i, j, k: (i, j)))
        args.append(add)
    specs += _behind(after)[0]
    args += _behind(after)[1]
    n_in = len(args)
    if out_blocks is None:
        out_shape = jax.ShapeDtypeStruct((M, N), out_dtype)
        out_spec = pl.BlockSpec((tm, tn), lambda i, j, k: (i, j))
    else:
        nb = out_blocks
        nw = N // nb
        assert nw % tn == 0
        per = nw // tn
        out_shape = jax.ShapeDtypeStruct((nb, M, nw), out_dtype)
        out_spec = pl.BlockSpec((None, tm, tn), lambda i, j, k: (j // per, i, j % per))
    keep_t = mode == "tn" and nj > 1
    scratch = []
    if nk > 1:
        scratch.append(pltpu.VMEM((tm, tn), F32))
    if keep_t:
        scratch.append(pltpu.VMEM((tm, tk), a.dtype))

    def body(*refs):
        a_ref, b_ref = refs[0], refs[1]
        add_ref = refs[2] if add is not None else None
        o_ref = refs[n_in]

        def finish(r):
            if add_ref is not None:
                r = r + add_ref[...]
            o_ref[...] = r.astype(out_dtype)

        if keep_t:
            at_ref = refs[-1]

            @pl.when(pl.program_id(1) == 0)
            def _():
                at_ref[...] = a_ref[...].T

            finish(_dot(at_ref[...], b_ref[...], NN))
        elif nk == 1:
            finish(_dot(a_ref[...], b_ref[...], dn))
        else:
            acc_ref = refs[n_in + 1]
            k = pl.program_id(2)

            @pl.when(k == 0)
            def _():
                acc_ref[...] = jnp.zeros_like(acc_ref)

            acc_ref[...] += _dot(a_ref[...], b_ref[...], dn)

            @pl.when(k == nk - 1)
            def _():
                finish(acc_ref[...])

    return pl.pallas_call(
        body, name=name, grid=(M // tm, nj, nk), in_specs=specs, out_specs=out_spec, out_shape=out_shape,
        scratch_shapes=scratch,
        compiler_params=_params("parallel", "arbitrary" if keep_t else "parallel", "arbitrary"),
    )(*args)


def _rms_fwd(xs, gs, name, after=None):
    n = len(xs)
    rows = xs[0].shape[0]
    widths = [x.shape[1] for x in xs]
    tr = _row_tile(rows)
    more_specs, more = _behind(after)

    def body(*refs):
        o_ref = refs[2 * n + len(more)]
        off = 0
        for t in range(n):
            o_ref[:, off:off + widths[t]] = _rms(refs[t][...], refs[n + t][...]).astype(BF16)
            off += widths[t]

    return pl.pallas_call(
        body, name=name, grid=(rows // tr,),
        in_specs=[pl.BlockSpec((tr, w), lambda i: (i, 0)) for w in widths]
        + [pl.BlockSpec((1, w), lambda i: (0, 0)) for w in widths] + more_specs,
        out_specs=pl.BlockSpec((tr, sum(widths)), lambda i: (i, 0)),
        out_shape=jax.ShapeDtypeStruct((rows, sum(widths)), BF16),
        compiler_params=_params("parallel"),
    )(*xs, *gs, *more)


def _rms_bwd(x, g, dy, name, dy_col=0, res=None, want_dx=True, after=None):
    rows, w = x.shape
    tr = _row_tile(rows)
    has_res = res is not None
    more_specs, more = _behind(after)

    def body(*refs):
        x_ref, g_ref, dy_ref = refs[:3]
        res_ref = refs[3] if has_res else None
        outs = refs[3 + has_res + len(more):]
        dg_ref = outs[-1]
        dx, dg = _rms_grad(x_ref[...], g_ref[...], dy_ref[...])

        @pl.when(pl.program_id(0) == 0)
        def _():
            dg_ref[...] = jnp.zeros_like(dg_ref)

        dg_ref[...] += dg
        if want_dx:
            if has_res:
                dx = dx + res_ref[...]
            outs[0][...] = dx
            outs[1][...] = dx.astype(BF16)

    row_spec = pl.BlockSpec((tr, w), lambda i: (i, 0))
    in_specs = [row_spec, pl.BlockSpec((1, w), lambda i: (0, 0)), pl.BlockSpec((tr, w), lambda i: (i, dy_col))]
    args = [x, g, dy]
    if has_res:
        in_specs.append(row_spec)
        args.append(res)
    in_specs += more_specs
    args += more
    out_specs, out_shape = [], []
    if want_dx:
        out_specs += [row_spec, row_spec]
        out_shape += [jax.ShapeDtypeStruct((rows, w), F32), jax.ShapeDtypeStruct((rows, w), BF16)]
    out_specs.append(pl.BlockSpec((1, w), lambda i: (0, 0)))
    out_shape.append(jax.ShapeDtypeStruct((1, w), F32))
    return pl.pallas_call(
        body, name=name, grid=(rows // tr,), in_specs=in_specs, out_specs=out_specs, out_shape=out_shape,
        compiler_params=_params("arbitrary"),
    )(*args)


def _loss_head(y, target, name):
    rows, w = y.shape
    tr = _row_tile(rows)

    def body(y_ref, t_ref, dy_ref, dyb_ref, acc_ref):
        e = y_ref[...] - t_ref[...]

        @pl.when(pl.program_id(0) == 0)
        def _():
            acc_ref[...] = jnp.zeros_like(acc_ref)

        acc_ref[...] += jnp.sum(e * e)
        dy = e * (1.0 / w)
        dy_ref[...] = dy
        dyb_ref[...] = dy.astype(BF16)

    row_spec = pl.BlockSpec((tr, w), lambda i: (i, 0))
    return pl.pallas_call(
        body, name=name, grid=(rows // tr,), in_specs=[row_spec, row_spec],
        out_specs=[row_spec, row_spec, pl.BlockSpec((8, 128), lambda i: (0, 0))],
        out_shape=[jax.ShapeDtypeStruct((rows, w), F32), jax.ShapeDtypeStruct((rows, w), BF16),
                   jax.ShapeDtypeStruct((8, 128), F32)],
        compiler_params=_params("arbitrary"),
    )(y, target)


def _qk_fwd(proj, g_q, g_k, name, after=None):
    rows = proj.shape[0]
    w = FOX_H * HEAD
    tr = _row_tile(rows)
    more_specs, more = _behind(after)

    def body(q_ref, k_ref, v_ref, gq_ref, gk_ref, *rest):
        qn_ref, kn_ref, vb_ref = rest[len(more):]
        for h in range(FOX_H):
            sl = slice(h * HEAD, (h + 1) * HEAD)
            qn_ref[:, sl] = _rms(q_ref[:, sl], gq_ref[...]).astype(BF16)
            kn_ref[:, sl] = _rms(k_ref[:, sl], gk_ref[...]).astype(BF16)
        vb_ref[...] = v_ref[...].astype(BF16)

    gain = pl.BlockSpec((1, HEAD), lambda i: (0, 0))
    out = pl.BlockSpec((tr, w), lambda i: (i, 0))
    return pl.pallas_call(
        body, name=name, grid=(rows // tr,),
        in_specs=[pl.BlockSpec((tr, w), lambda i: (i, 0)), pl.BlockSpec((tr, w), lambda i: (i, 1)),
                  pl.BlockSpec((tr, w), lambda i: (i, 2)), gain, gain] + more_specs,
        out_specs=[out, out, out], out_shape=[jax.ShapeDtypeStruct((rows, w), BF16)] * 3,
        compiler_params=_params("parallel"),
    )(proj, proj, proj, g_q, g_k, *more)


def _qk_bwd(proj, g_q, g_k, dqn, dkn, name):
    rows = proj.shape[0]
    w = FOX_H * HEAD
    tr = _row_tile(rows)

    def body(q_ref, k_ref, gq_ref, gk_ref, dqn_ref, dkn_ref, dq_ref, dk_ref, dgq_ref, dgk_ref):
        @pl.when(pl.program_id(0) == 0)
        def _():
            dgq_ref[...] = jnp.zeros_like(dgq_ref)
            dgk_ref[...] = jnp.zeros_like(dgk_ref)

        for h in range(FOX_H):
            sl = slice(h * HEAD, (h + 1) * HEAD)
            dq, dgq = _rms_grad(q_ref[:, sl], gq_ref[...], dqn_ref[:, sl])
            dk, dgk = _rms_grad(k_ref[:, sl], gk_ref[...], dkn_ref[:, sl])
            dq_ref[:, sl] = dq.astype(BF16)
            dk_ref[:, sl] = dk.astype(BF16)
            dgq_ref[...] += dgq
            dgk_ref[...] += dgk

    gain = pl.BlockSpec((1, HEAD), lambda i: (0, 0))
    row = pl.BlockSpec((tr, w), lambda i: (i, 0))
    return pl.pallas_call(
        body, name=name, grid=(rows // tr,),
        in_specs=[row, pl.BlockSpec((tr, w), lambda i: (i, 1)), gain, gain, row, row],
        out_specs=[row, row, gain, gain],
        out_shape=[jax.ShapeDtypeStruct((rows, w), BF16)] * 2 + [jax.ShapeDtypeStruct((1, HEAD), F32)] * 2,
        compiler_params=_params("arbitrary"),
    )(proj, proj, g_q, g_k, dqn, dkn)


def _fgate_fwd(f_logit, b_pad, name):
    S = f_logit.shape[0]
    T = min(CUM_T, S)
    nb = S // T
    RT = min(ATT_T, S)
    nbr = S // RT

    def body(f_ref, b_ref, col_ref, row_ref, c_scr):
        tri = (lax.broadcasted_iota(jnp.int32, (T, T), 0) >= lax.broadcasted_iota(jnp.int32, (T, T), 1)).astype(F32)
        carry = jnp.zeros((1, HEAD), F32)
        for blk in range(nb):
            z = f_ref[blk * T:(blk + 1) * T, :] + b_ref[...]
            lf = jnp.minimum(z, 0.0) - jnp.log1p(jnp.exp(-jnp.abs(z)))
            cb = jnp.dot(tri, lf, precision=lax.Precision.HIGHEST, preferred_element_type=F32) + carry
            c_scr[blk * T:(blk + 1) * T, :] = cb
            carry = cb[T - 1:T, :]
        c = c_scr[...]
        lane = lax.broadcasted_iota(jnp.int32, c.shape, 1)
        ct = c.T
        for h in range(FOX_H):
            col_ref[h] = jnp.sum(jnp.where(lane == h, c, 0.0), axis=1, keepdims=True)
            for jb in range(nbr):
                row_ref[h, jb] = ct[h:h + 1, jb * RT:(jb + 1) * RT]

    return pl.pallas_call(
        body, name=name,
        out_shape=[jax.ShapeDtypeStruct((FOX_H, S, 1), F32), jax.ShapeDtypeStruct((FOX_H, nbr, 1, RT), F32)],
        scratch_shapes=[pltpu.VMEM((S, HEAD), F32)], compiler_params=_params(),
    )(f_logit, b_pad)


def _fgate_bwd(f_logit, b_pad, dc_col, dc_row, name):
    S = f_logit.shape[0]
    T = min(CUM_T, S)
    nb = S // T
    RT = min(ATT_T, S)

    def body(f_ref, b_ref, dcol_ref, drow_ref, df_ref, db_ref, dc_scr, rt_scr):
        lane = lax.broadcasted_iota(jnp.int32, (S, HEAD), 1)
        sub = lax.broadcasted_iota(jnp.int32, (HEAD, RT), 0)
        dc = jnp.zeros((S, HEAD), F32)
        for h in range(FOX_H):
            dc = jnp.where(lane == h, dcol_ref[h], dc)
        for jb in range(S // RT):
            rt = jnp.zeros((HEAD, RT), F32)
            for h in range(FOX_H):
                rt = jnp.where(sub == h, drow_ref[h, jb], rt)
            rt_scr[jb * RT:(jb + 1) * RT, :] = rt.T
        dc_scr[...] = dc - rt_scr[...]
        tri = (lax.broadcasted_iota(jnp.int32, (T, T), 0) <= lax.broadcasted_iota(jnp.int32, (T, T), 1)).astype(F32)
        carry = jnp.zeros((1, HEAD), F32)
        db = jnp.zeros((1, HEAD), F32)
        for blk in reversed(range(nb)):
            rows = slice(blk * T, (blk + 1) * T)
            dlf = jnp.dot(tri, dc_scr[rows, :], precision=lax.Precision.HIGHEST, preferred_element_type=F32) + carry
            carry = dlf[0:1, :]
            z = f_ref[rows, :] + b_ref[...]
            df = dlf * jax.nn.sigmoid(-z)
            df_ref[rows, :] = df.astype(BF16)
            db = db + jnp.sum(df, axis=0, keepdims=True)
        db_ref[...] = db

    return pl.pallas_call(
        body, name=name,
        out_shape=[jax.ShapeDtypeStruct((S, HEAD), BF16), jax.ShapeDtypeStruct((1, HEAD), F32)],
        scratch_shapes=[pltpu.VMEM((S, HEAD), F32), pltpu.VMEM((S, HEAD), F32)], compiler_params=_params(),
    )(f_logit, b_pad, dc_col, dc_row)


def _fox_fwd(qn, kn, vb, c_col, c_row, name, after=None):
    S = qn.shape[0]
    T = min(ATT_T, S)
    nb = S // T
    scale = 1.0 / math.sqrt(HEAD)
    more_specs, more = _behind(after)

    def body(q_ref, k_ref, v_ref, cc_ref, cr_ref, *rest):
        o_ref, lse_ref = rest[len(more):]
        i = pl.program_id(1)
        q = q_ref[...]
        cc = cc_ref[...]

        def step(j, carry, diagonal=False):
            m, l, acc = carry
            sl = pl.ds(pl.multiple_of(j * T, T), T)
            s = _dot(q, k_ref[sl, :], NT) * scale + cc - cr_ref[j]
            if diagonal:
                s = jnp.where(lax.broadcasted_iota(jnp.int32, (T, T), 0) >= lax.broadcasted_iota(jnp.int32, (T, T), 1), s, NEG)
            m2 = jnp.maximum(m, jnp.max(s, axis=1, keepdims=True))
            p = jnp.exp(s - m2)
            al = jnp.exp(m - m2)
            return m2, al * l + jnp.sum(p, axis=1, keepdims=True), al * acc + _dot(p.astype(BF16), v_ref[sl, :])

        init = (jnp.full((T, 1), NEG, F32), jnp.zeros((T, 1), F32), jnp.zeros((T, HEAD), F32))
        m, l, acc = step(i, lax.fori_loop(0, i, step, init), diagonal=True)
        o_ref[...] = acc / l
        lse_ref[...] = m + jnp.log(l)

    head = pl.BlockSpec((S, HEAD), lambda h, i: (0, h))
    return pl.pallas_call(
        body, name=name, grid=(FOX_H, nb),
        in_specs=[pl.BlockSpec((T, HEAD), lambda h, i: (i, h)), head, head,
                  pl.BlockSpec((None, T, 1), lambda h, i: (h, i, 0)),
                  pl.BlockSpec((None, nb, 1, T), lambda h, i: (h, 0, 0, 0))] + more_specs,
        out_specs=[pl.BlockSpec((T, HEAD), lambda h, i: (i, h)), pl.BlockSpec((None, T, 1), lambda h, i: (h, i, 0))],
        out_shape=[jax.ShapeDtypeStruct((S, FOX_H * HEAD), F32), jax.ShapeDtypeStruct((FOX_H, S, 1), F32)],
        compiler_params=_params("parallel", "parallel"),
    )(qn, kn, vb, c_col, c_row, *more)


def _fox_bwd(qn, kn, vb, c_col, c_row, lse, o, do, name):
    S = qn.shape[0]
    T = min(ATT_T, S)
    nb = S // T
    scale = 1.0 / math.sqrt(HEAD)

    def body(q_ref, k_ref, v_ref, cc_ref, cr_ref, lse_ref, o_ref, do_ref, dq_ref, dk_ref, dv_ref, dcc_ref, dcr_ref):
        j = pl.program_id(1)

        @pl.when(j == 0)
        def _():
            dq_ref[...] = jnp.zeros_like(dq_ref)
            dcc_ref[...] = jnp.zeros_like(dcc_ref)

        k = k_ref[...]
        v = v_ref[...]
        cr = cr_ref[...]

        def step(i, carry, diagonal=False):
            dk, dv, dcr = carry
            sl = pl.ds(pl.multiple_of(i * T, T), T)
            q = q_ref[sl, :]
            d_o = do_ref[sl, :]
            s = _dot(q, k, NT) * scale + cc_ref[sl, :] - cr
            if diagonal:
                s = jnp.where(lax.broadcasted_iota(jnp.int32, (T, T), 0) >= lax.broadcasted_iota(jnp.int32, (T, T), 1), s, NEG)
            p = jnp.exp(s - lse_ref[sl, :])
            dob = d_o.astype(BF16)
            dp = _dot(dob, v, NT)
            delta = jnp.sum(d_o * o_ref[sl, :], axis=1, keepdims=True)
            ds = p * (dp - delta)
            dsb = ds.astype(BF16)
            dq_ref[sl, :] += _dot(dsb, k) * scale
            dcc_ref[sl, :] += jnp.sum(ds, axis=1, keepdims=True)
            return (dk + _dot(dsb, q, TN), dv + _dot(p.astype(BF16), dob, TN), dcr + jnp.sum(ds, axis=0, keepdims=True))

        init = (jnp.zeros((T, HEAD), F32), jnp.zeros((T, HEAD), F32), jnp.zeros((1, T), F32))
        dk, dv, dcr = lax.fori_loop(j + 1, nb, step, step(j, init, diagonal=True))
        dk_ref[...] = dk * scale
        dv_ref[...] = dv.astype(BF16)
        dcr_ref[...] = dcr

    head = pl.BlockSpec((S, HEAD), lambda h, j: (0, h))
    tile = pl.BlockSpec((T, HEAD), lambda h, j: (j, h))
    col = pl.BlockSpec((None, S, 1), lambda h, j: (h, 0, 0))
    row = pl.BlockSpec((None, None, 1, T), lambda h, j: (h, j, 0, 0))
    w = FOX_H * HEAD
    return pl.pallas_call(
        body, name=name, grid=(FOX_H, nb),
        in_specs=[head, tile, tile, col, row, col, head, head],
        out_specs=[head, tile, tile, col, row],
        out_shape=[jax.ShapeDtypeStruct((S, w), F32), jax.ShapeDtypeStruct((S, w), F32), jax.ShapeDtypeStruct((S, w), BF16),
                   jax.ShapeDtypeStruct((FOX_H, S, 1), F32), jax.ShapeDtypeStruct((FOX_H, nb, 1, T), F32)],
        compiler_params=_params("arbitrary", "arbitrary"),
    )(qn, kn, vb, c_col, c_row, lse, o, do)


def _lru_gates(uc, wra, bra, wri, bri, lam):
    ucb = uc.astype(BF16)
    r = jax.nn.sigmoid(_dot(ucb, wra.astype(BF16)) + bra)
    ig = jax.nn.sigmoid(_dot(ucb, wri.astype(BF16)) + bri)
    sp = jnp.maximum(-lam, 0.0) + jnp.log1p(jnp.exp(-jnp.abs(lam)))
    log_a = -LRU_C * r * sp
    a = jnp.exp(log_a)
    mult = jnp.sqrt(_one_minus_exp(2.0 * log_a))
    return r, ig, sp, a, mult


def _conv(pad_ref, cw, cb, S):
    uc = cb
    for j in range(CONV_K):
        uc = uc + cw[j:j + 1, :] * pad_ref[5 + j:5 + j + S, :]
    return uc


def _lru_specs(S, n_proj_cols):
    u_col = 3 * FOX_H
    g_col = u_col + LRU_NB
    blk = pl.BlockSpec((S, HEAD), lambda n: (0, n))
    vec = pl.BlockSpec((1, HEAD), lambda n: (0, n))
    mat = pl.BlockSpec((None, HEAD, HEAD), lambda n: (n, 0, 0))
    return dict(
        u=pl.BlockSpec((S, HEAD), lambda n: (0, u_col + n)), gate=pl.BlockSpec((S, HEAD), lambda n: (0, g_col + n)),
        blk=blk, vec=vec, mat=mat, cw=pl.BlockSpec((CONV_K, HEAD), lambda n: (0, n)))


def _lru_fwd(proj, conv_w, conv_b, w_ra, b_ra, w_ri, b_ri, lam, name, after=None):
    S = proj.shape[0]
    sp_ = _lru_specs(S, proj.shape[1])
    rows8 = S // 8
    more_specs, more = _behind(after)

    def body(u_ref, gt_ref, cw_ref, cb_ref, wra_ref, bra_ref, wri_ref, bri_ref, lam_ref, *rest):
        y_ref, h_ref, pad, a_scr, b_scr = rest[len(more):]
        pad[0:8, :] = jnp.zeros((8, HEAD), F32)
        pad[8:S + 8, :] = u_ref[...]
        uc = _conv(pad, cw_ref[...], cb_ref[...], S)
        r, ig, sp, a, mult = _lru_gates(uc, wra_ref[...], bra_ref[...], wri_ref[...], bri_ref[...], lam_ref[...])
        a_scr[...] = a
        b_scr[...] = mult * (ig * uc)
        sub = lax.broadcasted_iota(jnp.int32, (8, HEAD), 0)

        def step(t, carry):
            sl = pl.ds(pl.multiple_of(t * 8, 8), 8)
            A, B = a_scr[sl, :], b_scr[sl, :]
            for d in (1, 2, 4):
                m = sub >= d
                B = jnp.where(m, A * pltpu.roll(B, d, 0) + B, B)
                A = jnp.where(m, A * pltpu.roll(A, d, 0), A)
            h = A * carry + B
            h_ref[sl, :] = h
            return h[7:8, :]

        lax.fori_loop(0, rows8, step, jnp.zeros((1, HEAD), F32))
        y_ref[...] = h_ref[...] * _gelu(gt_ref[...])

    w = LRU_NB * HEAD
    return pl.pallas_call(
        body, name=name, grid=(LRU_NB,),
        in_specs=[sp_["u"], sp_["gate"], sp_["cw"], sp_["vec"], sp_["mat"], sp_["vec"], sp_["mat"], sp_["vec"], sp_["vec"]]
        + more_specs,
        out_specs=[sp_["blk"], sp_["blk"]],
        out_shape=[jax.ShapeDtypeStruct((S, w), F32)] * 2,
        scratch_shapes=[pltpu.VMEM((S + 8, HEAD), F32), pltpu.VMEM((S, HEAD), F32), pltpu.VMEM((S, HEAD), F32)],
        compiler_params=_params("parallel"),
    )(proj, proj, conv_w, conv_b, w_ra, b_ra, w_ri, b_ri, lam, *more)


def _lru_bwd(proj, h, dy, conv_w, conv_b, w_ra, b_ra, w_ri, b_ri, lam, name, after=None):
    S = proj.shape[0]
    sp_ = _lru_specs(S, proj.shape[1])
    rows8 = S // 8
    more_specs, more = _behind(after)

    def body(u_ref, gt_ref, h_ref, dy_ref, cw_ref, cb_ref, wra_ref, bra_ref, wri_ref, bri_ref, lam_ref, *rest):
        (du_ref, dgt_ref, dcw_ref, dcb_ref, dwra_ref, dbra_ref, dwri_ref, dbri_ref, dlam_ref,
         pad, an_scr, d_scr, g_scr, hp_scr) = rest[len(more):]
        zero8 = jnp.zeros((8, HEAD), F32)
        pad[0:8, :] = zero8
        pad[8:S + 8, :] = u_ref[...]
        cw = cw_ref[...]
        uc = _conv(pad, cw, cb_ref[...], S)
        wra, wri, lam_v = wra_ref[...], wri_ref[...], lam_ref[...]
        r, ig, sp, a, mult = _lru_gates(uc, wra, bra_ref[...], wri, bri_ref[...], lam_v)
        gate = gt_ref[...]
        dy_v = dy_ref[...]
        hv = h_ref[...]
        dgt_ref[...] = (dy_v * hv * _gelu_grad(gate)).astype(BF16)
        d_scr[...] = dy_v * _gelu(gate)
        g_scr[0:S, :] = a
        g_scr[S:S + 8, :] = zero8
        an_scr[...] = g_scr[1:S + 1, :]
        sub = lax.broadcasted_iota(jnp.int32, (8, HEAD), 0)

        def step(t, carry):
            sl = pl.ds(pl.multiple_of((rows8 - 1 - t) * 8, 8), 8)
            A, D = an_scr[sl, :], d_scr[sl, :]
            for d in (1, 2, 4):
                m = sub + d <= 7
                D = jnp.where(m, A * pltpu.roll(D, 8 - d, 0) + D, D)
                A = jnp.where(m, A * pltpu.roll(A, 8 - d, 0), A)
            g = A * carry + D
            g_scr[sl, :] = g
            return g[0:1, :]

        lax.fori_loop(0, rows8, step, jnp.zeros((1, HEAD), F32))
        g = g_scr[0:S, :]
        hp_scr[0:8, :] = zero8
        hp_scr[8:S + 8, :] = hv
        da = g * hp_scr[7:S + 7, :]
        iu = ig * uc
        dmult = g * iu
        diu = g * mult
        dig = diu * uc
        duc = diu * ig
        dlog_a = da * a - dmult * (a * a) / mult
        dr = dlog_a * (-LRU_C * sp)
        dsp = jnp.sum(dlog_a * (-LRU_C * r), axis=0, keepdims=True)
        dlam_ref[...] = -dsp * jax.nn.sigmoid(-lam_v)
        dpr = dr * r * (1.0 - r)
        dpi = dig * ig * (1.0 - ig)
        dbra_ref[...] = jnp.sum(dpr, axis=0, keepdims=True)
        dbri_ref[...] = jnp.sum(dpi, axis=0, keepdims=True)
        ucb = uc.astype(BF16)
        dprb, dpib = dpr.astype(BF16), dpi.astype(BF16)
        dwra_ref[...] = _dot(ucb, dprb, TN).astype(BF16)
        dwri_ref[...] = _dot(ucb, dpib, TN).astype(BF16)
        duc = duc + _dot(dprb, wra.astype(BF16), NT) + _dot(dpib, wri.astype(BF16), NT)
        dcb_ref[...] = jnp.sum(duc, axis=0, keepdims=True)
        for j in range(CONV_K):
            dcw_ref[j:j + 1, :] = jnp.sum(duc * pad[5 + j:5 + j + S, :], axis=0, keepdims=True)
        g_scr[0:S, :] = duc
        g_scr[S:S + 8, :] = zero8
        du = jnp.zeros((S, HEAD), F32)
        for j in range(CONV_K):
            du = du + cw[j:j + 1, :] * g_scr[3 - j:3 - j + S, :]
        du_ref[...] = du.astype(BF16)

    w = LRU_NB * HEAD
    bf = jax.ShapeDtypeStruct((S, w), BF16)
    vec = jax.ShapeDtypeStruct((1, w), F32)
    mat = jax.ShapeDtypeStruct((LRU_NB, HEAD, HEAD), BF16)
    return pl.pallas_call(
        body, name=name, grid=(LRU_NB,),
        in_specs=[sp_["u"], sp_["gate"], sp_["blk"], sp_["blk"], sp_["cw"], sp_["vec"], sp_["mat"], sp_["vec"],
                  sp_["mat"], sp_["vec"], sp_["vec"]] + more_specs,
        out_specs=[sp_["blk"], sp_["blk"], sp_["cw"], sp_["vec"], sp_["mat"], sp_["vec"], sp_["mat"], sp_["vec"], sp_["vec"]],
        out_shape=[bf, bf, jax.ShapeDtypeStruct((CONV_K, w), F32), vec, mat, vec, mat, vec, vec],
        scratch_shapes=[pltpu.VMEM((S + 8, HEAD), F32), pltpu.VMEM((S, HEAD), F32), pltpu.VMEM((S, HEAD), F32),
                        pltpu.VMEM((S + 8, HEAD), F32), pltpu.VMEM((S + 8, HEAD), F32)],
        compiler_params=_params("parallel"),
    )(proj, proj, h, dy, conv_w, conv_b, w_ra, b_ra, w_ri, b_ri, lam, *more)


def _xattn_fwd(cq, ckv, g_cq, g_ck, name):
    S, w = cq.shape
    M = ckv.shape[0]
    tr = _row_tile(S)
    scale = 1.0 / math.sqrt(HEAD)

    def body(cq_ref, ckv_ref, gq_ref, gk_ref, o_ref):
        for h in range(XATT_H):
            sl = slice(h * HEAD, (h + 1) * HEAD)
            qn = _rms(cq_ref[:, sl], gq_ref[...]).astype(BF16)
            kn = _rms(ckv_ref[:, sl], gk_ref[...]).astype(BF16)
            v = ckv_ref[:, w + h * HEAD:w + (h + 1) * HEAD].astype(BF16)
            s = _dot(qn, kn, NT) * scale
            p = jnp.exp(s - jnp.max(s, axis=1, keepdims=True))
            p = p / jnp.sum(p, axis=1, keepdims=True)
            o_ref[:, sl] = _dot(p.astype(BF16), v).astype(BF16)

    gain = pl.BlockSpec((1, HEAD), lambda i: (0, 0))
    return pl.pallas_call(
        body, name=name, grid=(S // tr,),
        in_specs=[pl.BlockSpec((tr, w), lambda i: (i, 0)), pl.BlockSpec((M, 2 * w), lambda i: (0, 0)), gain, gain],
        out_specs=pl.BlockSpec((tr, w), lambda i: (i, 0)), out_shape=jax.ShapeDtypeStruct((S, w), BF16),
        compiler_params=_params("parallel"),
    )(cq, ckv, g_cq, g_ck)


def _xattn_bwd(cq, ckv, g_cq, g_ck, do, name):
    S, w = cq.shape
    M = ckv.shape[0]
    tr = _row_tile(S)
    nsteps = S // tr
    scale = 1.0 / math.sqrt(HEAD)

    def body(cq_ref, ckv_ref, gq_ref, gk_ref, do_ref, dcq_ref, dckv_ref, dgq_ref, dgk_ref, dkn_scr, dv_scr):
        step = pl.program_id(0)

        @pl.when(step == 0)
        def _():
            dkn_scr[...] = jnp.zeros_like(dkn_scr)
            dv_scr[...] = jnp.zeros_like(dv_scr)
            dgq_ref[...] = jnp.zeros_like(dgq_ref)

        for h in range(XATT_H):
            sl = slice(h * HEAD, (h + 1) * HEAD)
            q_raw = cq_ref[:, sl]
            qn = _rms(q_raw, gq_ref[...]).astype(BF16)
            kn = _rms(ckv_ref[:, sl], gk_ref[...]).astype(BF16)
            v = ckv_ref[:, w + h * HEAD:w + (h + 1) * HEAD].astype(BF16)
            s = _dot(qn, kn, NT) * scale
            p = jnp.exp(s - jnp.max(s, axis=1, keepdims=True))
            p = p / jnp.sum(p, axis=1, keepdims=True)
            dob = do_ref[:, sl].astype(BF16)
            dp = _dot(dob, v, NT)
            ds = p * (dp - jnp.sum(p * dp, axis=1, keepdims=True)) * scale
            dsb = ds.astype(BF16)
            dv_scr[:, sl] += _dot(p.astype(BF16), dob, TN)
            dkn_scr[:, sl] += _dot(dsb, qn, TN)
            dq, dgq = _rms_grad(q_raw, gq_ref[...], _dot(dsb, kn))
            dcq_ref[:, sl] = dq.astype(BF16)
            dgq_ref[...] += dgq

        @pl.when(step == nsteps - 1)
        def _():
            dgk = jnp.zeros((1, HEAD), F32)
            for h in range(XATT_H):
                sl = slice(h * HEAD, (h + 1) * HEAD)
                dk, dgk_h = _rms_grad(ckv_ref[:, sl], gk_ref[...], dkn_scr[:, sl])
                dckv_ref[:, sl] = dk.astype(BF16)
                dgk = dgk + dgk_h
            dckv_ref[:, w:2 * w] = dv_scr[...].astype(BF16)
            dgk_ref[...] = dgk

    gain = pl.BlockSpec((1, HEAD), lambda i: (0, 0))
    row = pl.BlockSpec((tr, w), lambda i: (i, 0))
    mem = pl.BlockSpec((M, 2 * w), lambda i: (0, 0))
    return pl.pallas_call(
        body, name=name, grid=(nsteps,), in_specs=[row, mem, gain, gain, row], out_specs=[row, mem, gain, gain],
        out_shape=[jax.ShapeDtypeStruct((S, w), BF16), jax.ShapeDtypeStruct((M, 2 * w), BF16),
                   jax.ShapeDtypeStruct((1, HEAD), F32), jax.ShapeDtypeStruct((1, HEAD), F32)],
        scratch_shapes=[pltpu.VMEM((M, w), F32), pltpu.VMEM((M, w), F32)],
        compiler_params=_params("arbitrary"),
    )(cq, ckv, g_cq, g_ck, do)


def _swiglu_fwd(gu, name, n_ct=4, after=None):
    S, w2 = gu.shape
    f = w2 // 2
    tc = f // n_ct
    tr = _row_tile(S)
    more_specs, more = _behind(after)

    def body(g_ref, u_ref, *rest):
        g = g_ref[...]
        rest[-1][...] = (g * jax.nn.sigmoid(g) * u_ref[...]).astype(BF16)

    return pl.pallas_call(
        body, name=name, grid=(S // tr, n_ct),
        in_specs=[pl.BlockSpec((tr, tc), lambda i, j: (i, j)), pl.BlockSpec((tr, tc), lambda i, j: (i, j + n_ct))] + more_specs,
        out_specs=pl.BlockSpec((tr, tc), lambda i, j: (i, j)), out_shape=jax.ShapeDtypeStruct((S, f), BF16),
        compiler_params=_params("parallel", "parallel"),
    )(gu, gu, *more)


def _swiglu_bwd(gu, dact, name, n_ct=4):
    S, w2 = gu.shape
    f = w2 // 2
    tc = f // n_ct
    tr = _row_tile(S)

    steps = (S // tr) * n_ct

    def body(g_ref, u_ref, da_ref, o_hbm, buf, sems):
        i, j = pl.program_id(0), pl.program_id(1)
        step = i * n_ct + j
        slot = step % 2

        def put(half):
            cols = pl.ds(pl.multiple_of(half * f + j * tc, 128), tc)
            return pltpu.make_async_copy(buf.at[slot, half], o_hbm.at[pl.ds(pl.multiple_of(i * tr, 8), tr), cols],
                                         sems.at[slot, half])

        @pl.when(step >= 2)
        def _():
            put(0).wait()
            put(1).wait()

        g = g_ref[...]
        sg = jax.nn.sigmoid(g)
        da = da_ref[...]
        buf[slot, 0] = (da * u_ref[...] * (sg * (1.0 + g * (1.0 - sg)))).astype(BF16)
        buf[slot, 1] = (da * (g * sg)).astype(BF16)
        put(0).start()
        put(1).start()

        @pl.when(step == steps - 1)
        def _():
            put(0).wait()
            put(1).wait()
            if steps > 1:
                for half in (0, 1):
                    pltpu.make_async_copy(buf.at[1 - slot, half], o_hbm.at[pl.ds(0, tr), pl.ds(0, tc)],
                                          sems.at[1 - slot, half]).wait()

    return pl.pallas_call(
        body, name=name, grid=(S // tr, n_ct),
        in_specs=[pl.BlockSpec((tr, tc), lambda i, j: (i, j)), pl.BlockSpec((tr, tc), lambda i, j: (i, j + n_ct)),
                  pl.BlockSpec((tr, tc), lambda i, j: (i, j))],
        out_specs=ANY, out_shape=jax.ShapeDtypeStruct((S, w2), BF16),
        scratch_shapes=[pltpu.VMEM((2, 2, tr, tc), BF16), pltpu.SemaphoreType.DMA((2, 2))],
        compiler_params=_params("arbitrary", "arbitrary"),
    )(gu, gu, dact)


def _adamw_math(w, g, m, v):
    m = ADAM_B1 * m + (1.0 - ADAM_B1) * g
    v = ADAM_B2 * v + (1.0 - ADAM_B2) * (g * g)
    m_hat = m / (1.0 - ADAM_B1 ** ADAM_STEP)
    v_hat = v / (1.0 - ADAM_B2 ** ADAM_STEP)
    delta = -ADAM_LR * (m_hat / (jnp.sqrt(v_hat) + ADAM_EPS) + ADAM_WD * w)
    return delta, m, v


def _tile2(R, C, elems):
    if R % 16 == 0:
        return _row_tile(R, cap=max(16, elems // C // 16 * 16)), C
    tc = 128
    while C % (2 * tc) == 0 and R * 2 * tc <= elems:
        tc *= 2
    return R, tc


def _reduce_adamw(own, land, chip, w, m, v, name):
    _, R, C = own.shape
    tr, tc = _tile2(R, C, 1 << 18)

    def body(s_ref, o_ref, l1_ref, l2_ref, l3_ref, w_ref, m_ref, v_ref, g_ref, d_ref, nm_ref, nv_ref):
        g = o_ref[...].astype(F32) + l1_ref[...].astype(F32) + l2_ref[...].astype(F32) + l3_ref[...].astype(F32)
        g_ref[...] = g
        d_ref[...], nm_ref[...], nv_ref[...] = _adamw_math(w_ref[...], g, m_ref[...], v_ref[...])

    def part(flip):
        return pl.BlockSpec((None, tr, tc), lambda i, j, s: (s[0] ^ flip, i, j))

    if w.ndim == 3:
        blk = pl.BlockSpec((tr, None, tc), lambda i, j, s: (i, 0, j))
    else:
        blk = pl.BlockSpec((tr, tc), lambda i, j, s: (i, j))
    sds = jax.ShapeDtypeStruct(w.shape, F32)
    return pl.pallas_call(
        body, name=name,
        grid_spec=pltpu.PrefetchScalarGridSpec(
            num_scalar_prefetch=1, grid=(R // tr, C // tc),
            in_specs=[part(0), part(1), part(2), part(3), blk, blk, blk], out_specs=[blk] * 4),
        out_shape=[sds] * 4, compiler_params=_params("parallel", "parallel"),
    )(chip, own, land, land, land, w, m, v)


def _pair_sum(g8, recv, core, name):
    _, R, C = g8.shape
    tr, tc = _tile2(R, C, 1 << 21)

    def body(c_ref, a_ref, b_ref, o_ref):
        o_ref[...] = (a_ref[...].astype(F32) + b_ref[...].astype(F32)).astype(BF16)

    return pl.pallas_call(
        body, name=name,
        grid_spec=pltpu.PrefetchScalarGridSpec(
            num_scalar_prefetch=1, grid=(N_CHIP, R // tr, C // tc),
            in_specs=[pl.BlockSpec((None, tr, tc), lambda q, i, j, c: (2 * q + c[0], i, j)),
                      pl.BlockSpec((None, tr, tc), lambda q, i, j, c: (q, i, j))],
            out_specs=pl.BlockSpec((None, tr, tc), lambda q, i, j, c: (q, i, j))),
        out_shape=jax.ShapeDtypeStruct((N_CHIP, R, C), BF16),
        compiler_params=_params("parallel", "parallel", "parallel"),
    )(core, g8, recv)


def _place():
    return lax.axis_index("x"), lax.axis_index("y"), lax.axis_index("c")


def _land_with_own(shard, dev, name):
    R, C = shard.shape
    tr, tc = _tile2(R, C, 1 << 19)

    def body(d_ref, s_ref, o_ref):
        o_ref[...] = s_ref[...]

    return pl.pallas_call(
        body, name=name,
        grid_spec=pltpu.PrefetchScalarGridSpec(
            num_scalar_prefetch=1, grid=(R // tr, C // tc),
            in_specs=[pl.BlockSpec((tr, tc), lambda i, j, d: (i, j))],
            out_specs=pl.BlockSpec((None, tr, tc), lambda i, j, d: (d[0], i, j))),
        out_shape=jax.ShapeDtypeStruct((N_DEV, R, C), shard.dtype),
        compiler_params=_params("parallel", "parallel"),
    )(dev, shard)


def _all_gather(shards, lands, name):
    n = len(shards)

    def body(*refs):
        ins, outs = refs[:n], refs[2 * n:3 * n]
        send_sems, recv_sems = refs[3 * n:]
        x, y, c = _place()
        me, sibling = (x, y, c), (x, y, 1 - c)
        chips = [(1 - x, y), (x, 1 - y), (1 - x, 1 - y)]

        def copy(a, k, block, to, src=None):
            dst = outs[a].at[4 * block[0] + 2 * block[1] + block[2]]
            return pltpu.make_async_remote_copy(
                src_ref=dst if src is None else src, dst_ref=dst, send_sem=send_sems.at[a, k],
                recv_sem=recv_sems.at[a, k], device_id=to, device_id_type=MESH)

        sent = []
        for a in range(n):
            sent.append(copy(a, 0, me, sibling, src=ins[a]))
            sent += [copy(a, 1 + j, me, (*chip, c), src=ins[a]) for j, chip in enumerate(chips)]
        for cp in sent:
            cp.start()
        for j, chip in enumerate(chips):
            for a in range(n):
                copy(a, 1 + j, (*chip, c), me).wait_recv()
                fwd = copy(a, 4 + j, (*chip, c), sibling)
                fwd.start()
                sent.append(fwd)
        for a in range(n):
            copy(a, 0, sibling, me).wait_recv()
        for j, chip in enumerate(chips):
            for a in range(n):
                copy(a, 4 + j, (*chip, 1 - c), me).wait_recv()
        for cp in sent:
            cp.wait_send()

    return pl.pallas_call(
        body, name=name, in_specs=[ANY] * (2 * n), out_specs=[ANY] * n,
        out_shape=[jax.ShapeDtypeStruct(l.shape, l.dtype) for l in lands],
        input_output_aliases={n + a: a for a in range(n)},
        scratch_shapes=[pltpu.SemaphoreType.DMA((n, 7)), pltpu.SemaphoreType.DMA((n, 7))],
    )(*shards, *lands)


def _pair_plan(srcs, lands):
    x, y, c = _place()
    plan = []
    for a in range(len(srcs)):
        for q in range(N_CHIP):
            plan.append((srcs[a].at[2 * q + 1 - c], lands[a].at[q], lands[a].at[q], (x, y, 1 - c)))
    return plan


def _chip_plan(srcs, lands):
    x, y, c = _place()
    mine = 2 * x + y
    plan = []
    for a in range(len(srcs)):
        for px, py in [(1 - x, y), (x, 1 - y), (1 - x, 1 - y)]:
            peer = 2 * px + py
            plan.append((srcs[a].at[peer], lands[a].at[mine], lands[a].at[peer], (px, py, c)))
    return plan


def _gather_plan(srcs, lands):
    x, y, c = _place()
    mine = 4 * x + 2 * y + c
    plan = []
    for a in range(len(srcs)):
        for px, py, pc in [(x, y, 1 - c), (1 - x, y, c), (x, 1 - y, c), (1 - x, 1 - y, c)]:
            plan.append((srcs[a], lands[a].at[mine], lands[a].at[4 * px + 2 * py + pc], (px, py, pc)))
    return plan


def _forward_plan(srcs, lands):
    x, y, c = _place()
    plan = []
    for a in range(len(lands)):
        for px, py in [(1 - x, y), (x, 1 - y), (1 - x, 1 - y)]:
            mine, theirs = 4 * px + 2 * py + c, 4 * px + 2 * py + 1 - c
            plan.append((lands[a].at[mine], lands[a].at[mine], lands[a].at[theirs], (x, y, 1 - c)))
    return plan


def _remote(src, dst, send_sem, recv_sem, peer):
    return pltpu.make_async_remote_copy(src_ref=src, dst_ref=dst, send_sem=send_sem, recv_sem=recv_sem,
                                        device_id=peer, device_id_type=MESH)


HBM = pl.BlockSpec(memory_space=pltpu.HBM)
SEMS = pl.BlockSpec(memory_space=pltpu.SEMAPHORE)
DATAFLOW = pltpu.SideEffectType.DATAFLOW_SIDE_EFFECTING


def _split_start(srcs, lands, after, plan_fn, per_array, name):
    ns, nb = len(srcs), len(srcs) + len(lands)
    ncopy = per_array * len(lands)
    after = list(after) if isinstance(after, (list, tuple)) else [after]

    def body(*refs):
        send_sems, recv_sems = refs[nb + len(after)], refs[nb + len(after) + 1]
        token = refs[-1]
        for k, (src, dst, _, peer) in enumerate(plan_fn(refs[:ns], refs[ns:nb])):
            _remote(src, dst, send_sems.at[k], recv_sems.at[k], peer).start()
        token[...] = jnp.zeros_like(token)

    thru = [pltpu.HBM(a.shape, a.dtype) for a in (*srcs, *lands)]
    out = pl.pallas_call(
        body, name=name,
        out_shape=(pltpu.SemaphoreType.DMA((ncopy,)), pltpu.SemaphoreType.DMA((ncopy,)), *thru,
                   jax.ShapeDtypeStruct((8, 128), F32)),
        in_specs=[HBM] * nb + [ANY] * len(after),
        out_specs=(SEMS, SEMS, *[HBM] * nb, pl.BlockSpec(memory_space=pltpu.VMEM)),
        input_output_aliases={i: 2 + i for i in range(nb)},
        compiler_params=pltpu.CompilerParams(has_side_effects=DATAFLOW),
    )(*[pltpu.with_memory_space_constraint(a, pltpu.HBM) for a in (*srcs, *lands)], *after)
    return out[0], out[1], list(out[2:2 + ns]), list(out[2 + ns:2 + nb]), out[-1]


def _split_wait(send_sems, recv_sems, srcs, lands, after, plan_fn, name):
    ns, nb = len(srcs), len(srcs) + len(lands)
    after = list(after) if isinstance(after, (list, tuple)) else [after]

    def body(*refs):
        send_ref, recv_ref = refs[nb], refs[nb + 1]
        for k, (src, dst, mine, peer) in enumerate(plan_fn(refs[:ns], refs[ns:nb])):
            _remote(src, dst, send_ref.at[k], recv_ref.at[k], peer).wait_send()
            _remote(src, mine, send_ref.at[k], recv_ref.at[k], peer).wait_recv()

    thru = [pltpu.HBM(a.shape, a.dtype) for a in (*srcs, *lands)]
    out = pl.pallas_call(
        body, name=name, out_shape=tuple(thru),
        in_specs=[HBM] * nb + [SEMS, SEMS] + [ANY] * len(after), out_specs=tuple([HBM] * nb),
        input_output_aliases={i: i for i in range(nb)},
        compiler_params=pltpu.CompilerParams(has_side_effects=DATAFLOW),
    )(*srcs, *lands, send_sems, recv_sems, *after)
    return list(out[:ns]), list(out[ns:])


def _gather_forward(lands, name):
    n = len(lands)

    def body(*refs):
        landed, out = refs[:n], refs[n:2 * n]
        send_sems, recv_sems = refs[2 * n:]
        x, y, c = _place()
        chips = [(1 - x, y), (x, 1 - y), (1 - x, 1 - y)]
        sent = []
        for a in range(n):
            for j, (px, py) in enumerate(chips):
                blk = 4 * px + 2 * py + c
                sent.append(_remote(landed[a].at[blk], out[a].at[blk], send_sems.at[a, j], recv_sems.at[a, j], (x, y, 1 - c)))
        for cp in sent:
            cp.start()
        for a in range(n):
            for j, (px, py) in enumerate(chips):
                blk = 4 * px + 2 * py + 1 - c
                _remote(landed[a].at[blk], out[a].at[blk], send_sems.at[a, j], recv_sems.at[a, j], (x, y, 1 - c)).wait_recv()
        for cp in sent:
            cp.wait_send()

    return pl.pallas_call(
        body, name=name, in_specs=[ANY] * n, out_specs=[ANY] * n,
        out_shape=[jax.ShapeDtypeStruct(l.shape, l.dtype) for l in lands],
        input_output_aliases={a: a for a in range(n)},
        scratch_shapes=[pltpu.SemaphoreType.DMA((n, 3)), pltpu.SemaphoreType.DMA((n, 3))],
    )(*lands)


def _pack_rows(groups, name):
    n = len(groups[0])
    rows = -(-n // 8) * 8
    width = max(v.shape[1] for v in groups[0])

    def body(*refs):
        ins, outs = refs[:n * len(groups)], refs[n * len(groups):]
        for gi, o_ref in enumerate(outs):
            o_ref[...] = jnp.zeros_like(o_ref)
            for r in range(n):
                v_ref = ins[gi * n + r]
                o_ref[r:r + 1, 0:v_ref.shape[1]] = v_ref[...]

    return pl.pallas_call(
        body, name=name, out_shape=[jax.ShapeDtypeStruct((rows, width), F32)] * len(groups), compiler_params=_params(),
    )(*[v for g in groups for v in g])


def _unpack_rows(packs, like, name):
    n = len(like)

    def body(*refs):
        ins, outs = refs[:len(packs)], refs[len(packs):]
        for pi, p_ref in enumerate(ins):
            for r in range(n):
                o_ref = outs[pi * n + r]
                o_ref[...] = p_ref[r:r + 1, 0:o_ref.shape[1]]

    flat = pl.pallas_call(
        body, name=name, out_shape=[jax.ShapeDtypeStruct(a.shape, F32) for _ in packs for a in like],
        compiler_params=_params(),
    )(*packs)
    return [flat[pi * n:(pi + 1) * n] for pi in range(len(packs))]


def _sum_adamw(parts, w, m, v, name, col=None):
    P, R, _ = parts.shape
    C = w.shape[1]
    tr = _row_tile(R, cap=512, mult=8)

    def body(c_ref, p_ref, w_ref, m_ref, v_ref, g_ref, d_ref, nm_ref, nv_ref):
        g = p_ref[0].astype(F32)
        for q in range(1, P):
            g = g + p_ref[q].astype(F32)
        g_ref[...] = g
        d_ref[...], nm_ref[...], nv_ref[...] = _adamw_math(w_ref[...], g, m_ref[...], v_ref[...])

    blk = pl.BlockSpec((tr, C), lambda i, c: (i, 0))
    sds = jax.ShapeDtypeStruct((R, C), F32)
    at = jnp.zeros((1,), jnp.int32) if col is None else col
    return pl.pallas_call(
        body, name=name,
        grid_spec=pltpu.PrefetchScalarGridSpec(
            num_scalar_prefetch=1, grid=(R // tr,),
            in_specs=[pl.BlockSpec((P, tr, C), lambda i, c: (0, i, c[0])), blk, blk, blk], out_specs=[blk] * 4),
        out_shape=[sds] * 4, compiler_params=_params("parallel"),
    )(at, parts, w, m, v)


def _local_step(x, mem, target, sp, W, arrived, want, done, paired):
    h0 = _rms_fwd([x], [sp["g_mix"]], "rms_mix")
    proj = _mm(h0, W["main_t"], "nt", F32, "proj_main", tm=2048, tn=512)
    f_logit = _mm(h0, W["f_t"], "nt", F32, "proj_f", tm=2048)
    qn, kn, vb = _qk_fwd(proj, sp["g_q"], sp["g_k"], "qk_norm")
    b_pad = jnp.pad(sp["b_f"], ((0, 0), (0, HEAD - FOX_H)))
    c_col, c_row = _fgate_fwd(f_logit, b_pad, "forget_cumsum")
    o_fox, lse = _fox_fwd(qn, kn, vb, c_col, c_row, "fox_fwd")
    lru_p = (sp["conv_w"], sp["conv_b"], sp["w_ra"], sp["b_ra"], sp["w_ri"], sp["b_ri"], sp["lam"])
    y_lru, h_lru = _lru_fwd(proj, *lru_p, "lru_fwd", after=arrived("attn", o_fox))
    w_out, w_cq, w_ckv, w_co = want("attn", y_lru)
    mix = _rms_fwd([o_fox, y_lru], [sp["g_fox_out"], sp["g_lru_out"]], "rms_mix_out")
    x1 = _mm(mix, w_out, "nn", F32, "out_proj", add=x)
    hq = _rms_fwd([x1], [sp["g_xattn"]], "rms_xattn")
    mn = _rms_fwd([mem], [sp["g_mem"]], "rms_mem")
    cq = _mm(hq, w_cq, "nn", F32, "xattn_q", tm=2048)
    ckv = _mm(mn, w_ckv, "nn", F32, "xattn_kv")
    o_x = _xattn_fwd(cq, ckv, sp["g_cq"], sp["g_ck"], "xattn_fwd")
    co_w = w_co.shape[2]
    x2 = _mm(o_x, w_co, "nn", F32, "xattn_out", add=x1, tm=2048, tn=co_w, after=arrived("gate_up", o_x))
    hf = _rms_fwd([x2], [sp["g_ffn"]], "rms_ffn")
    (w_gu,) = want("gate_up", hf)
    gu_w = w_gu.shape[2]
    gu = _mm(hf, w_gu, "nn", F32, "ffn_up", tn=gu_w)
    act = _swiglu_fwd(gu, "swiglu_fwd", after=arrived("down", gu))
    (w_down,) = want("down", act)
    x3 = _mm(act, w_down, "nn", F32, "ffn_down", add=x2, tk=w_down.shape[0] // 2)
    dy, dyb, sq = _loss_head(x3, target, "loss_head")
    dact = _mm(dyb, w_down, "nt", F32, "ffn_down_dx", tm=2048, tn=512)
    g_down = _mm(act, dyb, "tn", BF16, "ffn_down_dw", tm=w_down.shape[0] // 4, tn=2048)
    dgu = _swiglu_bwd(gu, dact, "swiglu_bwd")
    g_gu = _mm(hf, dgu, "tn", BF16, "ffn_up_dw", out_blocks=N_DEV, tn=gu_w)
    tok = done("ffn", [g_gu, g_down])
    dhf = _mm(dgu, w_gu, "nt", F32, "ffn_up_dx", tn=2048, tk=gu_w, after=tok)
    tok = paired("ffn", dhf)
    dx2, dx2b, dg_ffn = _rms_bwd(x2, sp["g_ffn"], dhf, "rms_ffn_bwd", res=dy, after=tok)
    d_ox = _mm(dx2b, w_co, "nt", F32, "xattn_out_dx", tm=2048, tk=co_w)
    g_co = _mm(o_x, dx2b, "tn", BF16, "xattn_out_dw", out_blocks=N_DEV, tn=co_w)
    dcq, dckv, dg_cq, dg_ck = _xattn_bwd(cq, ckv, sp["g_cq"], sp["g_ck"], d_ox, "xattn_bwd")
    dhq = _mm(dcq, w_cq, "nt", F32, "xattn_q_dx", tm=2048)
    g_cq = _mm(hq, dcq, "tn", BF16, "xattn_q_dw")
    dmn = _mm(dckv, w_ckv, "nt", F32, "xattn_kv_dx")
    g_ckv = _mm(mn, dckv, "tn", BF16, "xattn_kv_dw")
    (dg_mem,) = _rms_bwd(mem, sp["g_mem"], dmn, "rms_mem_bwd", want_dx=False)
    dx1, dx1b, dg_xattn = _rms_bwd(x1, sp["g_xattn"], dhq, "rms_xattn_bwd", res=dx2)
    dmix = _mm(dx1b, w_out, "nt", F32, "out_proj_dx")
    g_out = _mm(mix, dx1b, "tn", BF16, "out_proj_dw", tn=2048)
    tok = done("attn", [g_out, g_cq, g_ckv, g_co])
    do_fox, _, dg_fox_out = _rms_bwd(o_fox, sp["g_fox_out"], dmix, "rms_fox_out_bwd", dy_col=0, after=tok)
    dy_lru, _, dg_lru_out = _rms_bwd(y_lru, sp["g_lru_out"], dmix, "rms_lru_out_bwd", dy_col=1)
    tok = paired("attn", dy_lru)
    du, dgate, dconv_w, dconv_b, dw_ra, db_ra, dw_ri, db_ri, dlam = _lru_bwd(proj, h_lru, dy_lru, *lru_p, "lru_bwd", after=tok)
    dqn, dkn, dv, dc_col, dc_row = _fox_bwd(qn, kn, vb, c_col, c_row, lse, o_fox, do_fox, "fox_bwd")
    dq, dk, dg_q, dg_k = _qk_bwd(proj, sp["g_q"], sp["g_k"], dqn, dkn, "qk_norm_bwd")
    df, db_f = _fgate_bwd(f_logit, b_pad, dc_col, dc_row, "forget_cumsum_bwd")
    dproj = jnp.concatenate([dq, dk, dv, du, dgate], axis=1)
    g_main_t = _mm(dproj, h0, "tn", BF16, "proj_main_dw", tn=2048)
    g_f_t = _mm(df, h0, "tn", BF16, "proj_f_dw", tn=2048)
    tok = done("w_in", [g_main_t, g_f_t])
    dh_f = _mm(df, W["f_t"], "nn", F32, "proj_f_dx", tm=2048)
    dh0 = _mm(dproj, W["main_t"], "nn", F32, "proj_main_dx", add=dh_f, tk=dproj.shape[1] // 2, after=tok)
    tok = paired("w_in", dh0)
    grad_x, _, dg_mix = _rms_bwd(x, sp["g_mix"], dh0, "rms_mix_bwd", res=dx1, after=tok)
    small = dict(g_mix=dg_mix, b_f=db_f[:, :FOX_H], g_q=dg_q, g_k=dg_k, conv_w=dconv_w, conv_b=dconv_b, w_ra=dw_ra,
                 b_ra=db_ra, w_ri=dw_ri, b_ri=db_ri, lam=dlam, g_fox_out=dg_fox_out, g_lru_out=dg_lru_out,
                 g_xattn=dg_xattn, g_mem=dg_mem, g_cq=dg_cq, g_ck=dg_ck, g_ffn=dg_ffn)
    return sq, grad_x, small


BIG = ("w_in", "w_out", "w_cq", "w_ckv", "w_co", "w_gate_up", "w_down")
SMALL = ("g_mix", "b_f", "g_q", "g_k", "conv_b", "w_ra", "b_ra", "w_ri", "b_ri", "lam", "g_fox_out", "g_lru_out",
         "g_xattn", "g_mem", "g_cq", "g_ck", "g_ffn")
ORDER = ("g_mix", "w_in", "b_f", "g_q", "g_k", "conv_w", "conv_b", "w_ra", "b_ra", "w_ri", "b_ri", "lam", "g_fox_out",
         "g_lru_out", "w_out", "g_xattn", "g_mem", "w_cq", "w_ckv", "g_cq", "g_ck", "w_co", "g_ffn", "w_gate_up", "w_down")


def kernel(x, mem, g_mix, w_in, b_f, g_q, g_k, conv_w, conv_b, w_ra, b_ra, w_ri, b_ri, lam, g_fox_out, g_lru_out, w_out, g_xattn, g_mem, w_cq, w_ckv, g_cq, g_ck, w_co, g_ffn, w_gate_up, w_down, loss_target, m_g_mix, m_w_in, m_b_f, m_g_q, m_g_k, m_conv_w, m_conv_b, m_w_ra, m_b_ra, m_w_ri, m_b_ri, m_lam, m_g_fox_out, m_g_lru_out, m_w_out, m_g_xattn, m_g_mem, m_w_cq, m_w_ckv, m_g_cq, m_g_ck, m_w_co, m_g_ffn, m_w_gate_up, m_w_down, v_g_mix, v_w_in, v_b_f, v_g_q, v_g_k, v_conv_w, v_conv_b, v_w_ra, v_b_ra, v_w_ri, v_b_ri, v_lam, v_g_fox_out, v_g_lru_out, v_w_out, v_g_xattn, v_g_mem, v_w_cq, v_w_ckv, v_g_cq, v_g_ck, v_w_co, v_g_ffn, v_w_gate_up, v_w_down):
    given = dict(locals())
    w = {n: given[n] for n in ORDER}
    m = {n: given["m_" + n] for n in ORDER}
    v = {n: given["v_" + n] for n in ORDER}
    D = x.shape[2]
    fw = FOX_H * HEAD
    dev_index = 4 * lax.axis_index("x") + 2 * lax.axis_index("y") + lax.axis_index("c")
    dev = jnp.reshape(dev_index, (1,)).astype(jnp.int32)
    core = jnp.reshape(lax.axis_index("c"), (1,)).astype(jnp.int32)
    chip = jnp.reshape(2 * lax.axis_index("x") + lax.axis_index("y"), (1,)).astype(jnp.int32)

    def shard(d, n):
        return jnp.transpose(d[n], (2, 0, 1)) if n == "w_in" else d[n][0]

    def unshard(a, n):
        return jnp.transpose(a, (1, 2, 0)) if n == "w_in" else a[None]

    gather_groups = dict(attn=("w_out", "w_cq", "w_ckv", "w_co"), gate_up=("w_gate_up",), down=("w_down",))
    reduce_groups = dict(ffn=("w_gate_up", "w_down"), attn=("w_out", "w_cq", "w_ckv", "w_co"), w_in=("w_in",))
    column_blocked = ("w_co", "w_gate_up")
    flying = {}

    def land(a, tag):
        return _land_with_own(a, dev, "own_" + tag)

    def launch(group, after):
        shards = [shard(w, n).astype(BF16) for n in gather_groups[group]]
        lands = [land(s, n) for s, n in zip(shards, gather_groups[group])]
        flying[group] = _split_start(shards, lands, after, _gather_plan, 4, "gather_" + group + "_start")
        return flying[group][4]

    def arrived(group, after):
        send, recv, shards, lands, _ = flying.pop(group)
        _, lands = _split_wait(send, recv, shards, lands, after, _gather_plan, "gather_" + group + "_wait")
        flying[group] = _split_start([], lands, after, _forward_plan, 3, "gather_" + group + "_forward_start")
        if group == "attn":
            return launch("down", flying[group][4])
        return flying[group][4]

    def want(group, after):
        send, recv, _, lands, _ = flying.pop(group)
        _, full = _split_wait(send, recv, [], lands, after, _forward_plan, "gather_" + group + "_forward_wait")
        return [g if n in column_blocked else g.reshape(-1, g.shape[2]) for n, g in zip(gather_groups[group], full)]

    first = [shard(w, "w_in").astype(BF16).reshape(-1, D), conv_w[0]]
    g_in, g_conv = _all_gather(first, [land(first[0], "w_in"), land(first[1], "conv_w")], "gather_w_in")
    tok = launch("gate_up", launch("attn", g_conv))
    per = g_in.shape[1]
    f_k, f_lo = divmod(3 * fw, per)
    assert f_lo + FOX_H <= per
    W = dict(main_t=jnp.concatenate([g_in[k] for k in range(f_k)] + [g_in[f_k][:f_lo], g_in[f_k][f_lo + FOX_H:]]
                                    + [g_in[k] for k in range(f_k + 1, N_DEV)], axis=0),
             f_t=jnp.pad(g_in[f_k][f_lo:f_lo + FOX_H], ((0, HEAD - FOX_H), (0, 0))))
    sp = {n: w[n] for n in SMALL if n not in ("w_ra", "w_ri")}
    sp["w_ra"], sp["w_ri"] = w_ra[0], w_ri[0]
    sp["conv_w"] = jnp.transpose(g_conv, (1, 0, 2)).reshape(CONV_K, -1)
    sp["g_mix"] = sp["g_mix"] + tok[0, 0]

    pairing, reducing = {}, {}

    def done(group, grads):
        if group == "w_in":
            g_main_t, g_f_t = grads
            shards = [g_main_t[k * per:(k + 1) * per] for k in range(f_k)]
            shards.append(jnp.concatenate([g_main_t[f_k * per:3 * fw], g_f_t[:FOX_H],
                                           g_main_t[3 * fw:(f_k + 1) * per - FOX_H]], axis=0))
            shards += [g_main_t[k * per - FOX_H:(k + 1) * per - FOX_H] for k in range(f_k + 1, N_DEV)]
            grads = [jnp.stack(shards)]
        g8 = [g if g.ndim == 3 else g.reshape(N_DEV, -1, g.shape[1]) for g in grads]
        lands = [lax.empty((N_CHIP,) + g.shape[1:], g.dtype) for g in g8]
        pairing[group] = _split_start(g8, lands, chip, _pair_plan, N_CHIP, "reduce_" + group + "_pair_start")
        return pairing[group][4]

    def paired(group, after):
        send, recv, g8, lands, _ = pairing.pop(group)
        g8, from_sibling = _split_wait(send, recv, g8, lands, after, _pair_plan, "reduce_" + group + "_pair_wait")
        p4 = [_pair_sum(g, r, core, "reduce_pair_sum_" + n) for g, r, n in zip(g8, from_sibling, reduce_groups[group])]
        lands = [lax.empty(p.shape, p.dtype) for p in p4]
        reducing[group] = _split_start(p4, lands, chip, _chip_plan, 3, "reduce_" + group + "_start")
        return reducing[group][4]

    def finish(group, after):
        send, recv, p4, lands, _ = reducing.pop(group)
        p4, lands = _split_wait(send, recv, p4, lands, after, _chip_plan, "reduce_" + group + "_wait")
        return {n: tuple(unshard(r, n) for r in _reduce_adamw(p, l, chip, shard(w, n), shard(m, n), shard(v, n), "adamw_" + n))
                for n, p, l in zip(reduce_groups[group], p4, lands)}

    sq, grad_x, gs = _local_step(x[0], mem[0], loss_target[0], sp, W, arrived, want, done, paired)
    loss = lax.psum(0.5 * sq[0, 0] / D, AXES)

    vectors = tuple(n for n in SMALL if n not in ("w_ra", "w_ri"))
    mine = [_pack_rows([[gs[n] for n in vectors]], "pack_small_grads")[0], gs["w_ra"].reshape(-1, HEAD),
            gs["w_ri"].reshape(-1, HEAD), gs["conv_w"]]
    lands = [land(a, "small_grads_%d" % k) for k, a in enumerate(mine)]
    s_send, s_recv, s_src, s_land, tok = _split_start(mine, lands, grad_x, _gather_plan, 4, "gather_small_start")
    out = finish("ffn", tok)
    out.update(finish("attn", tok))
    updated = [r for n in reduce_groups["ffn"] + reduce_groups["attn"] for r in out[n]]
    _, s_land = _split_wait(s_send, s_recv, s_src, s_land, updated, _gather_plan, "gather_small_wait")
    all_vec, all_ra, all_ri, all_conv = _gather_forward(s_land, "gather_small_forward")

    state = _pack_rows([[d[n] for n in vectors] for d in (w, m, v)], "pack_small_state")
    upd = _sum_adamw(all_vec, *state, "adamw_small_vectors")
    per_vector = _unpack_rows(upd, [w[n] for n in vectors], "unpack_small")
    for r, n in enumerate(vectors):
        out[n] = tuple(per_vector[k][r] for k in range(4))
    for n, parts in (("w_ra", all_ra), ("w_ri", all_ri)):
        res = _sum_adamw(parts, *[d[n].reshape(-1, HEAD) for d in (w, m, v)], "adamw_" + n)
        out[n] = tuple(r.reshape(w[n].shape) for r in res)
    out["conv_w"] = tuple(r[None] for r in _sum_adamw(all_conv, conv_w[0], m_conv_w[0], v_conv_w[0], "adamw_conv_w", col=dev))
    out.update(finish("w_in", upd[1]))

    return (loss, grad_x[None], *[out[n][0] for n in ORDER], *[out[n][1] for n in ORDER],
            *[out[n][2] for n in ORDER], *[out[n][3] for n in ORDER])
```

```python
import functools
import math

import jax
import jax.numpy as jnp
from jax import lax
from jax.experimental import pallas as pl
from jax.experimental.pallas import tpu as pltpu

F32 = jnp.float32
BF16 = jnp.bfloat16
MESH = pl.DeviceIdType.MESH
N_DEV = 8
N_CHIP = 4

HEAD = 128
FOX_H = 8
XATT_H = 4
LRU_NB = 8
CONV_K = 4
LRU_C = 8.0
RMS_EPS = 1e-6
ATT_T = 1024
CUM_T = 256
ROW_T = 256
NEG = -1e30
V7X_VMEM_LIMIT = 48 * 1024 * 1024

ADAM_LR = 0.001
ADAM_B1 = 0.9
ADAM_B2 = 0.999
ADAM_EPS = 1e-08
ADAM_WD = 0.01
ADAM_STEP = 10

NT = (((1,), (1,)), ((), ()))
TN = (((0,), (0,)), ((), ()))
NN = (((1,), (0,)), ((), ()))

ANY = pl.BlockSpec(memory_space=pl.ANY)


def _behind(after):
    return ([], []) if after is None else ([ANY], [after])


def _params(*sem):
    return pltpu.CompilerParams(dimension_semantics=sem or None, vmem_limit_bytes=V7X_VMEM_LIMIT)


def _dot(a, b, dn=NN):
    return lax.dot_general(a, b, dn, preferred_element_type=F32)


def _rms(x, g):
    r = lax.rsqrt(jnp.mean(x * x, axis=-1, keepdims=True) + RMS_EPS)
    return x * r * g


def _rms_grad(x, g, dy):
    r = lax.rsqrt(jnp.mean(x * x, axis=-1, keepdims=True) + RMS_EPS)
    xh = x * r
    dxh = dy * g
    dx = r * (dxh - xh * jnp.mean(dxh * xh, axis=-1, keepdims=True))
    return dx, jnp.sum(dy * xh, axis=0, keepdims=True)


def _gelu(x):
    k = math.sqrt(2.0 / math.pi)
    return 0.5 * x * (1.0 + jnp.tanh(k * (x + 0.044715 * (x * x * x))))


def _gelu_grad(x):
    k = math.sqrt(2.0 / math.pi)
    t = jnp.tanh(k * (x + 0.044715 * (x * x * x)))
    return 0.5 * (1.0 + t) + 0.5 * x * (1.0 - t * t) * k * (1.0 + 3.0 * 0.044715 * x * x)


def _one_minus_exp(z):
    series = -(z + 0.5 * z * z + (1.0 / 6.0) * z * z * z)
    return jnp.where(z > -1e-3, series, 1.0 - jnp.exp(z))


def _row_tile(rows, cap=ROW_T, mult=16):
    t = min(rows, cap)
    while rows % t or (t % mult and t != rows):
        t -= 1
    return t


def _mat_dims(z):
    return (z.shape[-2], z.shape[-1] * (z.shape[0] if z.ndim == 3 else 1))


def _mat_spec(arr, rblk, cblk, rsel, csel):
    if arr.ndim == 2:
        return pl.BlockSpec((rblk, cblk), lambda i, j, k: ((i, j, k)[rsel], (i, j, k)[csel]))
    nw = arr.shape[2]
    assert nw % cblk == 0, (arr.shape, cblk)
    per = nw // cblk
    return pl.BlockSpec((None, rblk, cblk),
                        lambda i, j, k: ((i, j, k)[csel] // per, (i, j, k)[rsel], (i, j, k)[csel] % per))


def _mm(a, b, mode, out_dtype, name, add=None, out_blocks=None, tm=1024, tn=1024, tk=None, after=None):
    ar, ac = _mat_dims(a)
    br, bc = _mat_dims(b)
    if mode == "nn":
        M, K, N = ar, ac, bc
        assert br == K
    elif mode == "nt":
        M, K, N = ar, ac, br
        assert bc == K
    else:
        M, K, N = ac, ar, bc
        assert br == K
    tm, tn = min(tm, M), min(tn, N)
    tk = K if tk is None or mode == "tn" else min(tk, K)
    assert M % tm == 0 and N % tn == 0 and K % tk == 0, (name, M, N, K, tm, tn, tk)
    nk = K // tk
    nj = N // tn
    if mode == "nn":
        specs = [_mat_spec(a, tm, tk, 0, 2), _mat_spec(b, tk, tn, 2, 1)]
        dn = NN
    elif mode == "nt":
        specs = [_mat_spec(a, tm, tk, 0, 2), _mat_spec(b, tn, tk, 1, 2)]
        dn = NT
    else:
        specs = [_mat_spec(a, tk, tm, 2, 0), _mat_spec(b, tk, tn, 2, 1)]
        dn = TN
    args = [a, b]
    if add is not None:
        specs.append(pl.BlockSpec((tm, tn), lambda i, j, k: (i, j)))
        args.append(add)
    specs += _behind(after)[0]
    args += _behind(after)[1]
    n_in = len(args)
    if out_blocks is None:
        out_shape = jax.ShapeDtypeStruct((M, N), out_dtype)
        out_spec = pl.BlockSpec((tm, tn), lambda i, j, k: (i, j))
    else:
        nb = out_blocks
        nw = N // nb
        assert nw % tn == 0
        per = nw // tn
        out_shape = jax.ShapeDtypeStruct((nb, M, nw), out_dtype)
        out_spec = pl.BlockSpec((None, tm, tn), lambda i, j, k: (j // per, i, j % per))
    keep_t = mode == "tn" and nj > 1
    scratch = []
    if nk > 1:
        scratch.append(pltpu.VMEM((tm, tn), F32))
    if keep_t:
        scratch.append(pltpu.VMEM((tm, tk), a.dtype))

    def body(*refs):
        a_ref, b_ref = refs[0], refs[1]
        add_ref = refs[2] if add is not None else None
        o_ref = refs[n_in]

        def finish(r):
            if add_ref is not None:
                r = r + add_ref[...]
            o_ref[...] = r.astype(out_dtype)

        if keep_t:
            at_ref = refs[-1]

            @pl.when(pl.program_id(1) == 0)
            def _():
                at_ref[...] = a_ref[...].T

            finish(_dot(at_ref[...], b_ref[...], NN))
        elif nk == 1:
            finish(_dot(a_ref[...], b_ref[...], dn))
        else:
            acc_ref = refs[n_in + 1]
            k = pl.program_id(2)

            @pl.when(k == 0)
            def _():
                acc_ref[...] = jnp.zeros_like(acc_ref)

            acc_ref[...] += _dot(a_ref[...], b_ref[...], dn)

            @pl.when(k == nk - 1)
            def _():
                finish(acc_ref[...])

    return pl.pallas_call(
        body, name=name, grid=(M // tm, nj, nk), in_specs=specs, out_specs=out_spec, out_shape=out_shape,
        scratch_shapes=scratch,
        compiler_params=_params("parallel", "arbitrary" if keep_t else "parallel", "arbitrary"),
    )(*args)


def _rms_fwd(xs, gs, name, after=None):
    n = len(xs)
    rows = xs[0].shape[0]
    widths = [x.shape[1] for x in xs]
    tr = _row_tile(rows)
    more_specs, more = _behind(after)

    def body(*refs):
        o_ref = refs[2 * n + len(more)]
        off = 0
        for t in range(n):
            o_ref[:, off:off + widths[t]] = _rms(refs[t][...], refs[n + t][...]).astype(BF16)
            off += widths[t]

    return pl.pallas_call(
        body, name=name, grid=(rows // tr,),
        in_specs=[pl.BlockSpec((tr, w), lambda i: (i, 0)) for w in widths]
        + [pl.BlockSpec((1, w), lambda i: (0, 0)) for w in widths] + more_specs,
        out_specs=pl.BlockSpec((tr, sum(widths)), lambda i: (i, 0)),
        out_shape=jax.ShapeDtypeStruct((rows, sum(widths)), BF16),
        compiler_params=_params("parallel"),
    )(*xs, *gs, *more)


def _rms_bwd(x, g, dy, name, dy_col=0, res=None, want_dx=True, after=None):
    rows, w = x.shape
    tr = _row_tile(rows)
    has_res = res is not None
    more_specs, more = _behind(after)

    def body(*refs):
        x_ref, g_ref, dy_ref = refs[:3]
        res_ref = refs[3] if has_res else None
        outs = refs[3 + has_res + len(more):]
        dg_ref = outs[-1]
        dx, dg = _rms_grad(x_ref[...], g_ref[...], dy_ref[...])

        @pl.when(pl.program_id(0) == 0)
        def _():
            dg_ref[...] = jnp.zeros_like(dg_ref)

        dg_ref[...] += dg
        if want_dx:
            if has_res:
                dx = dx + res_ref[...]
            outs[0][...] = dx
            outs[1][...] = dx.astype(BF16)

    row_spec = pl.BlockSpec((tr, w), lambda i: (i, 0))
    in_specs = [row_spec, pl.BlockSpec((1, w), lambda i: (0, 0)), pl.BlockSpec((tr, w), lambda i: (i, dy_col))]
    args = [x, g, dy]
    if has_res:
        in_specs.append(row_spec)
        args.append(res)
    in_specs += more_specs
    args += more
    out_specs, out_shape = [], []
    if want_dx:
        out_specs += [row_spec, row_spec]
        out_shape += [jax.ShapeDtypeStruct((rows, w), F32), jax.ShapeDtypeStruct((rows, w), BF16)]
    out_specs.append(pl.BlockSpec((1, w), lambda i: (0, 0)))
    out_shape.append(jax.ShapeDtypeStruct((1, w), F32))
    return pl.pallas_call(
        body, name=name, grid=(rows // tr,), in_specs=in_specs, out_specs=out_specs, out_shape=out_shape,
        compiler_params=_params("arbitrary"),
    )(*args)


def _loss_head(y, target, name):
    rows, w = y.shape
    tr = _row_tile(rows)

    def body(y_ref, t_ref, dy_ref, dyb_ref, acc_ref):
        e = y_ref[...] - t_ref[...]

        @pl.when(pl.program_id(0) == 0)
        def _():
            acc_ref[...] = jnp.zeros_like(acc_ref)

        acc_ref[...] += jnp.sum(e * e)
        dy = e * (1.0 / w)
        dy_ref[...] = dy
        dyb_ref[...] = dy.astype(BF16)

    row_spec = pl.BlockSpec((tr, w), lambda i: (i, 0))
    return pl.pallas_call(
        body, name=name, grid=(rows // tr,), in_specs=[row_spec, row_spec],
        out_specs=[row_spec, row_spec, pl.BlockSpec((8, 128), lambda i: (0, 0))],
        out_shape=[jax.ShapeDtypeStruct((rows, w), F32), jax.ShapeDtypeStruct((rows, w), BF16),
                   jax.ShapeDtypeStruct((8, 128), F32)],
        compiler_params=_params("arbitrary"),
    )(y, target)


def _qk_fwd(proj, g_q, g_k, name, after=None):
    rows = proj.shape[0]
    w = FOX_H * HEAD
    tr = _row_tile(rows)
    more_specs, more = _behind(after)

    def body(q_ref, k_ref, v_ref, gq_ref, gk_ref, *rest):
        qn_ref, kn_ref, vb_ref = rest[len(more):]
        for h in range(FOX_H):
            sl = slice(h * HEAD, (h + 1) * HEAD)
            qn_ref[:, sl] = _rms(q_ref[:, sl], gq_ref[...]).astype(BF16)
            kn_ref[:, sl] = _rms(k_ref[:, sl], gk_ref[...]).astype(BF16)
        vb_ref[...] = v_ref[...].astype(BF16)

    gain = pl.BlockSpec((1, HEAD), lambda i: (0, 0))
    out = pl.BlockSpec((tr, w), lambda i: (i, 0))
    return pl.pallas_call(
        body, name=name, grid=(rows // tr,),
        in_specs=[pl.BlockSpec((tr, w), lambda i: (i, 0)), pl.BlockSpec((tr, w), lambda i: (i, 1)),
                  pl.BlockSpec((tr, w), lambda i: (i, 2)), gain, gain] + more_specs,
        out_specs=[out, out, out], out_shape=[jax.ShapeDtypeStruct((rows, w), BF16)] * 3,
        compiler_params=_params("parallel"),
    )(proj, proj, proj, g_q, g_k, *more)


def _qk_bwd(proj, g_q, g_k, dqn, dkn, name):
    rows = proj.shape[0]
    w = FOX_H * HEAD
    tr = _row_tile(rows)

    def body(q_ref, k_ref, gq_ref, gk_ref, dqn_ref, dkn_ref, dq_ref, dk_ref, dgq_ref, dgk_ref):
        @pl.when(pl.program_id(0) == 0)
        def _():
            dgq_ref[...] = jnp.zeros_like(dgq_ref)
            dgk_ref[...] = jnp.zeros_like(dgk_ref)

        for h in range(FOX_H):
            sl = slice(h * HEAD, (h + 1) * HEAD)
            dq, dgq = _rms_grad(q_ref[:, sl], gq_ref[...], dqn_ref[:, sl])
            dk, dgk = _rms_grad(k_ref[:, sl], gk_ref[...], dkn_ref[:, sl])
            dq_ref[:, sl] = dq.astype(BF16)
            dk_ref[:, sl] = dk.astype(BF16)
            dgq_ref[...] += dgq
            dgk_ref[...] += dgk

    gain = pl.BlockSpec((1, HEAD), lambda i: (0, 0))
    row = pl.BlockSpec((tr, w), lambda i: (i, 0))
    return pl.pallas_call(
        body, name=name, grid=(rows // tr,),
        in_specs=[row, pl.BlockSpec((tr, w), lambda i: (i, 1)), gain, gain, row, row],
        out_specs=[row, row, gain, gain],
        out_shape=[jax.ShapeDtypeStruct((rows, w), BF16)] * 2 + [jax.ShapeDtypeStruct((1, HEAD), F32)] * 2,
        compiler_params=_params("arbitrary"),
    )(proj, proj, g_q, g_k, dqn, dkn)


def _fgate_fwd(f_logit, b_pad, name):
    S = f_logit.shape[0]
    T = min(CUM_T, S)
    nb = S // T
    RT = min(ATT_T, S)
    nbr = S // RT

    def body(f_ref, b_ref, col_ref, row_ref, c_scr):
        tri = (lax.broadcasted_iota(jnp.int32, (T, T), 0) >= lax.broadcasted_iota(jnp.int32, (T, T), 1)).astype(F32)
        carry = jnp.zeros((1, HEAD), F32)
        for blk in range(nb):
            z = f_ref[blk * T:(blk + 1) * T, :] + b_ref[...]
            lf = jnp.minimum(z, 0.0) - jnp.log1p(jnp.exp(-jnp.abs(z)))
            cb = jnp.dot(tri, lf, precision=lax.Precision.HIGHEST, preferred_element_type=F32) + carry
            c_scr[blk * T:(blk + 1) * T, :] = cb
            carry = cb[T - 1:T, :]
        c = c_scr[...]
        lane = lax.broadcasted_iota(jnp.int32, c.shape, 1)
        ct = c.T
        for h in range(FOX_H):
            col_ref[h] = jnp.sum(jnp.where(lane == h, c, 0.0), axis=1, keepdims=True)
            for jb in range(nbr):
                row_ref[h, jb] = ct[h:h + 1, jb * RT:(jb + 1) * RT]

    return pl.pallas_call(
        body, name=name,
        out_shape=[jax.ShapeDtypeStruct((FOX_H, S, 1), F32), jax.ShapeDtypeStruct((FOX_H, nbr, 1, RT), F32)],
        scratch_shapes=[pltpu.VMEM((S, HEAD), F32)], compiler_params=_params(),
    )(f_logit, b_pad)


def _fgate_bwd(f_logit, b_pad, dc_col, dc_row, name):
    S = f_logit.shape[0]
    T = min(CUM_T, S)
    nb = S // T
    RT = min(ATT_T, S)

    def body(f_ref, b_ref, dcol_ref, drow_ref, df_ref, db_ref, dc_scr, rt_scr):
        lane = lax.broadcasted_iota(jnp.int32, (S, HEAD), 1)
        sub = lax.broadcasted_iota(jnp.int32, (HEAD, RT), 0)
        dc = jnp.zeros((S, HEAD), F32)
        for h in range(FOX_H):
            dc = jnp.where(lane == h, dcol_ref[h], dc)
        for jb in range(S // RT):
            rt = jnp.zeros((HEAD, RT), F32)
            for h in range(FOX_H):
                rt = jnp.where(sub == h, drow_ref[h, jb], rt)
            rt_scr[jb * RT:(jb + 1) * RT, :] = rt.T
        dc_scr[...] = dc - rt_scr[...]
        tri = (lax.broadcasted_iota(jnp.int32, (T, T), 0) <= lax.broadcasted_iota(jnp.int32, (T, T), 1)).astype(F32)
        carry = jnp.zeros((1, HEAD), F32)
        db = jnp.zeros((1, HEAD), F32)
        for blk in reversed(range(nb)):
            rows = slice(blk * T, (blk + 1) * T)
            dlf = jnp.dot(tri, dc_scr[rows, :], precision=lax.Precision.HIGHEST, preferred_element_type=F32) + carry
            carry = dlf[0:1, :]
            z = f_ref[rows, :] + b_ref[...]
            df = dlf * jax.nn.sigmoid(-z)
            df_ref[rows, :] = df.astype(BF16)
            db = db + jnp.sum(df, axis=0, keepdims=True)
        db_ref[...] = db

    return pl.pallas_call(
        body, name=name,
        out_shape=[jax.ShapeDtypeStruct((S, HEAD), BF16), jax.ShapeDtypeStruct((1, HEAD), F32)],
        scratch_shapes=[pltpu.VMEM((S, HEAD), F32), pltpu.VMEM((S, HEAD), F32)], compiler_params=_params(),
    )(f_logit, b_pad, dc_col, dc_row)


def _fox_fwd(qn, kn, vb, c_col, c_row, name, after=None):
    S = qn.shape[0]
    T = min(ATT_T, S)
    nb = S // T
    scale = 1.0 / math.sqrt(HEAD)
    more_specs, more = _behind(after)

    def body(q_ref, k_ref, v_ref, cc_ref, cr_ref, *rest):
        o_ref, lse_ref = rest[len(more):]
        i = pl.program_id(1)
        q = q_ref[...]
        cc = cc_ref[...]

        def step(j, carry, diagonal=False):
            m, l, acc = carry
            sl = pl.ds(pl.multiple_of(j * T, T), T)
            s = _dot(q, k_ref[sl, :], NT) * scale + cc - cr_ref[j]
            if diagonal:
                s = jnp.where(lax.broadcasted_iota(jnp.int32, (T, T), 0) >= lax.broadcasted_iota(jnp.int32, (T, T), 1), s, NEG)
            m2 = jnp.maximum(m, jnp.max(s, axis=1, keepdims=True))
            p = jnp.exp(s - m2)
            al = jnp.exp(m - m2)
            return m2, al * l + jnp.sum(p, axis=1, keepdims=True), al * acc + _dot(p.astype(BF16), v_ref[sl, :])

        init = (jnp.full((T, 1), NEG, F32), jnp.zeros((T, 1), F32), jnp.zeros((T, HEAD), F32))
        m, l, acc = step(i, lax.fori_loop(0, i, step, init), diagonal=True)
        o_ref[...] = acc / l
        lse_ref[...] = m + jnp.log(l)

    head = pl.BlockSpec((S, HEAD), lambda h, i: (0, h))
    return pl.pallas_call(
        body, name=name, grid=(FOX_H, nb),
        in_specs=[pl.BlockSpec((T, HEAD), lambda h, i: (i, h)), head, head,
                  pl.BlockSpec((None, T, 1), lambda h, i: (h, i, 0)),
                  pl.BlockSpec((None, nb, 1, T), lambda h, i: (h, 0, 0, 0))] + more_specs,
        out_specs=[pl.BlockSpec((T, HEAD), lambda h, i: (i, h)), pl.BlockSpec((None, T, 1), lambda h, i: (h, i, 0))],
        out_shape=[jax.ShapeDtypeStruct((S, FOX_H * HEAD), F32), jax.ShapeDtypeStruct((FOX_H, S, 1), F32)],
        compiler_params=_params("parallel", "parallel"),
    )(qn, kn, vb, c_col, c_row, *more)


def _fox_bwd(qn, kn, vb, c_col, c_row, lse, o, do, name):
    S = qn.shape[0]
    T = min(ATT_T, S)
    nb = S // T
    scale = 1.0 / math.sqrt(HEAD)

    def body(q_ref, k_ref, v_ref, cc_ref, cr_ref, lse_ref, o_ref, do_ref, dq_ref, dk_ref, dv_ref, dcc_ref, dcr_ref):
        j = pl.program_id(1)

        @pl.when(j == 0)
        def _():
            dq_ref[...] = jnp.zeros_like(dq_ref)
            dcc_ref[...] = jnp.zeros_like(dcc_ref)

        k = k_ref[...]
        v = v_ref[...]
        cr = cr_ref[...]

        def step(i, carry, diagonal=False):
            dk, dv, dcr = carry
            sl = pl.ds(pl.multiple_of(i * T, T), T)
            q = q_ref[sl, :]
            d_o = do_ref[sl, :]
            s = _dot(q, k, NT) * scale + cc_ref[sl, :] - cr
            if diagonal:
                s = jnp.where(lax.broadcasted_iota(jnp.int32, (T, T), 0) >= lax.broadcasted_iota(jnp.int32, (T, T), 1), s, NEG)
            p = jnp.exp(s - lse_ref[sl, :])
            dob = d_o.astype(BF16)
            dp = _dot(dob, v, NT)
            delta = jnp.sum(d_o * o_ref[sl, :], axis=1, keepdims=True)
            ds = p * (dp - delta)
            dsb = ds.astype(BF16)
            dq_ref[sl, :] += _dot(dsb, k) * scale
            dcc_ref[sl, :] += jnp.sum(ds, axis=1, keepdims=True)
            return (dk + _dot(dsb, q, TN), dv + _dot(p.astype(BF16), dob, TN), dcr + jnp.sum(ds, axis=0, keepdims=True))

        init = (jnp.zeros((T, HEAD), F32), jnp.zeros((T, HEAD), F32), jnp.zeros((1, T), F32))
        dk, dv, dcr = lax.fori_loop(j + 1, nb, step, step(j, init, diagonal=True))
        dk_ref[...] = dk * scale
        dv_ref[...] = dv.astype(BF16)
        dcr_ref[...] = dcr

    head = pl.BlockSpec((S, HEAD), lambda h, j: (0, h))
    tile = pl.BlockSpec((T, HEAD), lambda h, j: (j, h))
    col = pl.BlockSpec((None, S, 1), lambda h, j: (h, 0, 0))
    row = pl.BlockSpec((None, None, 1, T), lambda h, j: (h, j, 0, 0))
    w = FOX_H * HEAD
    return pl.pallas_call(
        body, name=name, grid=(FOX_H, nb),
        in_specs=[head, tile, tile, col, row, col, head, head],
        out_specs=[head, tile, tile, col, row],
        out_shape=[jax.ShapeDtypeStruct((S, w), F32), jax.ShapeDtypeStruct((S, w), F32), jax.ShapeDtypeStruct((S, w), BF16),
                   jax.ShapeDtypeStruct((FOX_H, S, 1), F32), jax.ShapeDtypeStruct((FOX_H, nb, 1, T), F32)],
        compiler_params=_params("arbitrary", "arbitrary"),
    )(qn, kn, vb, c_col, c_row, lse, o, do)


def _lru_gates(uc, wra, bra, wri, bri, lam):
    ucb = uc.astype(BF16)
    r = jax.nn.sigmoid(_dot(ucb, wra.astype(BF16)) + bra)
    ig = jax.nn.sigmoid(_dot(ucb, wri.astype(BF16)) + bri)
    sp = jnp.maximum(-lam, 0.0) + jnp.log1p(jnp.exp(-jnp.abs(lam)))
    log_a = -LRU_C * r * sp
    a = jnp.exp(log_a)
    mult = jnp.sqrt(_one_minus_exp(2.0 * log_a))
    return r, ig, sp, a, mult


def _conv(pad_ref, cw, cb, S):
    uc = cb
    for j in range(CONV_K):
        uc = uc + cw[j:j + 1, :] * pad_ref[5 + j:5 + j + S, :]
    return uc


def _lru_specs(S, n_proj_cols):
    u_col = 3 * FOX_H
    g_col = u_col + LRU_NB
    blk = pl.BlockSpec((S, HEAD), lambda n: (0, n))
    vec = pl.BlockSpec((1, HEAD), lambda n: (0, n))
    mat = pl.BlockSpec((None, HEAD, HEAD), lambda n: (n, 0, 0))
    return dict(
        u=pl.BlockSpec((S, HEAD), lambda n: (0, u_col + n)), gate=pl.BlockSpec((S, HEAD), lambda n: (0, g_col + n)),
        blk=blk, vec=vec, mat=mat, cw=pl.BlockSpec((CONV_K, HEAD), lambda n: (0, n)))


def _lru_fwd(proj, conv_w, conv_b, w_ra, b_ra, w_ri, b_ri, lam, name, after=None):
    S = proj.shape[0]
    sp_ = _lru_specs(S, proj.shape[1])
    rows8 = S // 8
    more_specs, more = _behind(after)

    def body(u_ref, gt_ref, cw_ref, cb_ref, wra_ref, bra_ref, wri_ref, bri_ref, lam_ref, *rest):
        y_ref, h_ref, pad, a_scr, b_scr = rest[len(more):]
        pad[0:8, :] = jnp.zeros((8, HEAD), F32)
        pad[8:S + 8, :] = u_ref[...]
        uc = _conv(pad, cw_ref[...], cb_ref[...], S)
        r, ig, sp, a, mult = _lru_gates(uc, wra_ref[...], bra_ref[...], wri_ref[...], bri_ref[...], lam_ref[...])
        a_scr[...] = a
        b_scr[...] = mult * (ig * uc)
        sub = lax.broadcasted_iota(jnp.int32, (8, HEAD), 0)

        def step(t, carry):
            sl = pl.ds(pl.multiple_of(t * 8, 8), 8)
            A, B = a_scr[sl, :], b_scr[sl, :]
            for d in (1, 2, 4):
                m = sub >= d
                B = jnp.where(m, A * pltpu.roll(B, d, 0) + B, B)
                A = jnp.where(m, A * pltpu.roll(A, d, 0), A)
            h = A * carry + B
            h_ref[sl, :] = h
            return h[7:8, :]

        lax.fori_loop(0, rows8, step, jnp.zeros((1, HEAD), F32))
        y_ref[...] = h_ref[...] * _gelu(gt_ref[...])

    w = LRU_NB * HEAD
    return pl.pallas_call(
        body, name=name, grid=(LRU_NB,),
        in_specs=[sp_["u"], sp_["gate"], sp_["cw"], sp_["vec"], sp_["mat"], sp_["vec"], sp_["mat"], sp_["vec"], sp_["vec"]]
        + more_specs,
        out_specs=[sp_["blk"], sp_["blk"]],
        out_shape=[jax.ShapeDtypeStruct((S, w), F32)] * 2,
        scratch_shapes=[pltpu.VMEM((S + 8, HEAD), F32), pltpu.VMEM((S, HEAD), F32), pltpu.VMEM((S, HEAD), F32)],
        compiler_params=_params("parallel"),
    )(proj, proj, conv_w, conv_b, w_ra, b_ra, w_ri, b_ri, lam, *more)


def _lru_bwd(proj, h, dy, conv_w, conv_b, w_ra, b_ra, w_ri, b_ri, lam, name, after=None):
    S = proj.shape[0]
    sp_ = _lru_specs(S, proj.shape[1])
    rows8 = S // 8
    more_specs, more = _behind(after)

    def body(u_ref, gt_ref, h_ref, dy_ref, cw_ref, cb_ref, wra_ref, bra_ref, wri_ref, bri_ref, lam_ref, *rest):
        (du_ref, dgt_ref, dcw_ref, dcb_ref, dwra_ref, dbra_ref, dwri_ref, dbri_ref, dlam_ref,
         pad, an_scr, d_scr, g_scr, hp_scr) = rest[len(more):]
        zero8 = jnp.zeros((8, HEAD), F32)
        pad[0:8, :] = zero8
        pad[8:S + 8, :] = u_ref[...]
        cw = cw_ref[...]
        uc = _conv(pad, cw, cb_ref[...], S)
        wra, wri, lam_v = wra_ref[...], wri_ref[...], lam_ref[...]
        r, ig, sp, a, mult = _lru_gates(uc, wra, bra_ref[...], wri, bri_ref[...], lam_v)
        gate = gt_ref[...]
        dy_v = dy_ref[...]
        hv = h_ref[...]
        dgt_ref[...] = (dy_v * hv * _gelu_grad(gate)).astype(BF16)
        d_scr[...] = dy_v * _gelu(gate)
        g_scr[0:S, :] = a
        g_scr[S:S + 8, :] = zero8
        an_scr[...] = g_scr[1:S + 1, :]
        sub = lax.broadcasted_iota(jnp.int32, (8, HEAD), 0)

        def step(t, carry):
            sl = pl.ds(pl.multiple_of((rows8 - 1 - t) * 8, 8), 8)
            A, D = an_scr[sl, :], d_scr[sl, :]
            for d in (1, 2, 4):
                m = sub + d <= 7
                D = jnp.where(m, A * pltpu.roll(D, 8 - d, 0) + D, D)
                A = jnp.where(m, A * pltpu.roll(A, 8 - d, 0), A)
            g = A * carry + D
            g_scr[sl, :] = g
            return g[0:1, :]

        lax.fori_loop(0, rows8, step, jnp.zeros((1, HEAD), F32))
        g = g_scr[0:S, :]
        hp_scr[0:8, :] = zero8
        hp_scr[8:S + 8, :] = hv
        da = g * hp_scr[7:S + 7, :]
        iu = ig * uc
        dmult = g * iu
        diu = g * mult
        dig = diu * uc
        duc = diu * ig
        dlog_a = da * a - dmult * (a * a) / mult
        dr = dlog_a * (-LRU_C * sp)
        dsp = jnp.sum(dlog_a * (-LRU_C * r), axis=0, keepdims=True)
        dlam_ref[...] = -dsp * jax.nn.sigmoid(-lam_v)
        dpr = dr * r * (1.0 - r)
        dpi = dig * ig * (1.0 - ig)
        dbra_ref[...] = jnp.sum(dpr, axis=0, keepdims=True)
        dbri_ref[...] = jnp.sum(dpi, axis=0, keepdims=True)
        ucb = uc.astype(BF16)
        dprb, dpib = dpr.astype(BF16), dpi.astype(BF16)
        dwra_ref[...] = _dot(ucb, dprb, TN).astype(BF16)
        dwri_ref[...] = _dot(ucb, dpib, TN).astype(BF16)
        duc = duc + _dot(dprb, wra.astype(BF16), NT) + _dot(dpib, wri.astype(BF16), NT)
        dcb_ref[...] = jnp.sum(duc, axis=0, keepdims=True)
        for j in range(CONV_K):
            dcw_ref[j:j + 1, :] = jnp.sum(duc * pad[5 + j:5 + j + S, :], axis=0, keepdims=True)
        g_scr[0:S, :] = duc
        g_scr[S:S + 8, :] = zero8
        du = jnp.zeros((S, HEAD), F32)
        for j in range(CONV_K):
            du = du + cw[j:j + 1, :] * g_scr[3 - j:3 - j + S, :]
        du_ref[...] = du.astype(BF16)

    w = LRU_NB * HEAD
    bf = jax.ShapeDtypeStruct((S, w), BF16)
    vec = jax.ShapeDtypeStruct((1, w), F32)
    mat = jax.ShapeDtypeStruct((LRU_NB, HEAD, HEAD), BF16)
    return pl.pallas_call(
        body, name=name, grid=(LRU_NB,),
        in_specs=[sp_["u"], sp_["gate"], sp_["blk"], sp_["blk"], sp_["cw"], sp_["vec"], sp_["mat"], sp_["vec"],
                  sp_["mat"], sp_["vec"], sp_["vec"]] + more_specs,
        out_specs=[sp_["blk"], sp_["blk"], sp_["cw"], sp_["vec"], sp_["mat"], sp_["vec"], sp_["mat"], sp_["vec"], sp_["vec"]],
        out_shape=[bf, bf, jax.ShapeDtypeStruct((CONV_K, w), F32), vec, mat, vec, mat, vec, vec],
        scratch_shapes=[pltpu.VMEM((S + 8, HEAD), F32), pltpu.VMEM((S, HEAD), F32), pltpu.VMEM((S, HEAD), F32),
                        pltpu.VMEM((S + 8, HEAD), F32), pltpu.VMEM((S + 8, HEAD), F32)],
        compiler_params=_params("parallel"),
    )(proj, proj, h, dy, conv_w, conv_b, w_ra, b_ra, w_ri, b_ri, lam, *more)


def _xattn_fwd(cq, ckv, g_cq, g_ck, name):
    S, w = cq.shape
    M = ckv.shape[0]
    tr = _row_tile(S)
    scale = 1.0 / math.sqrt(HEAD)

    def body(cq_ref, ckv_ref, gq_ref, gk_ref, o_ref):
        for h in range(XATT_H):
            sl = slice(h * HEAD, (h + 1) * HEAD)
            qn = _rms(cq_ref[:, sl], gq_ref[...]).astype(BF16)
            kn = _rms(ckv_ref[:, sl], gk_ref[...]).astype(BF16)
            v = ckv_ref[:, w + h * HEAD:w + (h + 1) * HEAD].astype(BF16)
            s = _dot(qn, kn, NT) * scale
            p = jnp.exp(s - jnp.max(s, axis=1, keepdims=True))
            p = p / jnp.sum(p, axis=1, keepdims=True)
            o_ref[:, sl] = _dot(p.astype(BF16), v).astype(BF16)

    gain = pl.BlockSpec((1, HEAD), lambda i: (0, 0))
    return pl.pallas_call(
        body, name=name, grid=(S // tr,),
        in_specs=[pl.BlockSpec((tr, w), lambda i: (i, 0)), pl.BlockSpec((M, 2 * w), lambda i: (0, 0)), gain, gain],
        out_specs=pl.BlockSpec((tr, w), lambda i: (i, 0)), out_shape=jax.ShapeDtypeStruct((S, w), BF16),
        compiler_params=_params("parallel"),
    )(cq, ckv, g_cq, g_ck)


def _xattn_bwd(cq, ckv, g_cq, g_ck, do, name):
    S, w = cq.shape
    M = ckv.shape[0]
    tr = _row_tile(S)
    nsteps = S // tr
    scale = 1.0 / math.sqrt(HEAD)

    def body(cq_ref, ckv_ref, gq_ref, gk_ref, do_ref, dcq_ref, dckv_ref, dgq_ref, dgk_ref, dkn_scr, dv_scr):
        step = pl.program_id(0)

        @pl.when(step == 0)
        def _():
            dkn_scr[...] = jnp.zeros_like(dkn_scr)
            dv_scr[...] = jnp.zeros_like(dv_scr)
            dgq_ref[...] = jnp.zeros_like(dgq_ref)

        for h in range(XATT_H):
            sl = slice(h * HEAD, (h + 1) * HEAD)
            q_raw = cq_ref[:, sl]
            qn = _rms(q_raw, gq_ref[...]).astype(BF16)
            kn = _rms(ckv_ref[:, sl], gk_ref[...]).astype(BF16)
            v = ckv_ref[:, w + h * HEAD:w + (h + 1) * HEAD].astype(BF16)
            s = _dot(qn, kn, NT) * scale
            p = jnp.exp(s - jnp.max(s, axis=1, keepdims=True))
            p = p / jnp.sum(p, axis=1, keepdims=True)
            dob = do_ref[:, sl].astype(BF16)
            dp = _dot(dob, v, NT)
            ds = p * (dp - jnp.sum(p * dp, axis=1, keepdims=True)) * scale
            dsb = ds.astype(BF16)
            dv_scr[:, sl] += _dot(p.astype(BF16), dob, TN)
            dkn_scr[:, sl] += _dot(dsb, qn, TN)
            dq, dgq = _rms_grad(q_raw, gq_ref[...], _dot(dsb, kn))
            dcq_ref[:, sl] = dq.astype(BF16)
            dgq_ref[...] += dgq

        @pl.when(step == nsteps - 1)
        def _():
            dgk = jnp.zeros((1, HEAD), F32)
            for h in range(XATT_H):
                sl = slice(h * HEAD, (h + 1) * HEAD)
                dk, dgk_h = _rms_grad(ckv_ref[:, sl], gk_ref[...], dkn_scr[:, sl])
                dckv_ref[:, sl] = dk.astype(BF16)
                dgk = dgk + dgk_h
            dckv_ref[:, w:2 * w] = dv_scr[...].astype(BF16)
            dgk_ref[...] = dgk

    gain = pl.BlockSpec((1, HEAD), lambda i: (0, 0))
    row = pl.BlockSpec((tr, w), lambda i: (i, 0))
    mem = pl.BlockSpec((M, 2 * w), lambda i: (0, 0))
    return pl.pallas_call(
        body, name=name, grid=(nsteps,), in_specs=[row, mem, gain, gain, row], out_specs=[row, mem, gain, gain],
        out_shape=[jax.ShapeDtypeStruct((S, w), BF16), jax.ShapeDtypeStruct((M, 2 * w), BF16),
                   jax.ShapeDtypeStruct((1, HEAD), F32), jax.ShapeDtypeStruct((1, HEAD), F32)],
        scratch_shapes=[pltpu.VMEM((M, w), F32), pltpu.VMEM((M, w), F32)],
        compiler_params=_params("arbitrary"),
    )(cq, ckv, g_cq, g_ck, do)


def _swiglu_fwd(gu, name, n_ct=4, after=None):
    S, w2 = gu.shape
    f = w2 // 2
    tc = f // n_ct
    tr = _row_tile(S)
    more_specs, more = _behind(after)

    def body(g_ref, u_ref, *rest):
        g = g_ref[...]
        rest[-1][...] = (g * jax.nn.sigmoid(g) * u_ref[...]).astype(BF16)

    return pl.pallas_call(
        body, name=name, grid=(S // tr, n_ct),
        in_specs=[pl.BlockSpec((tr, tc), lambda i, j: (i, j)), pl.BlockSpec((tr, tc), lambda i, j: (i, j + n_ct))] + more_specs,
        out_specs=pl.BlockSpec((tr, tc), lambda i, j: (i, j)), out_shape=jax.ShapeDtypeStruct((S, f), BF16),
        compiler_params=_params("parallel", "parallel"),
    )(gu, gu, *more)


def _swiglu_bwd(gu, dact, name, n_ct=4):
    S, w2 = gu.shape
    f = w2 // 2
    tc = f // n_ct
    tr = _row_tile(S)

    steps = (S // tr) * n_ct

    def body(g_ref, u_ref, da_ref, o_hbm, buf, sems):
        i, j = pl.program_id(0), pl.program_id(1)
        step = i * n_ct + j
        slot = step % 2

        def put(half):
            cols = pl.ds(pl.multiple_of(half * f + j * tc, 128), tc)
            return pltpu.make_async_copy(buf.at[slot, half], o_hbm.at[pl.ds(pl.multiple_of(i * tr, 8), tr), cols],
                                         sems.at[slot, half])

        @pl.when(step >= 2)
        def _():
            put(0).wait()
            put(1).wait()

        g = g_ref[...]
        sg = jax.nn.sigmoid(g)
        da = da_ref[...]
        buf[slot, 0] = (da * u_ref[...] * (sg * (1.0 + g * (1.0 - sg)))).astype(BF16)
        buf[slot, 1] = (da * (g * sg)).astype(BF16)
        put(0).start()
        put(1).start()

        @pl.when(step == steps - 1)
        def _():
            put(0).wait()
            put(1).wait()
            if steps > 1:
                for half in (0, 1):
                    pltpu.make_async_copy(buf.at[1 - slot, half], o_hbm.at[pl.ds(0, tr), pl.ds(0, tc)],
                                          sems.at[1 - slot, half]).wait()

    return pl.pallas_call(
        body, name=name, grid=(S // tr, n_ct),
        in_specs=[pl.BlockSpec((tr, tc), lambda i, j: (i, j)), pl.BlockSpec((tr, tc), lambda i, j: (i, j + n_ct)),
                  pl.BlockSpec((tr, tc), lambda i, j: (i, j))],
        out_specs=ANY, out_shape=jax.ShapeDtypeStruct((S, w2), BF16),
        scratch_shapes=[pltpu.VMEM((2, 2, tr, tc), BF16), pltpu.SemaphoreType.DMA((2, 2))],
        compiler_params=_params("arbitrary", "arbitrary"),
    )(gu, gu, dact)


def _adamw_math(w, g, m, v):
    m = ADAM_B1 * m + (1.0 - ADAM_B1) * g
    v = ADAM_B2 * v + (1.0 - ADAM_B2) * (g * g)
    m_hat = m / (1.0 - ADAM_B1 ** ADAM_STEP)
    v_hat = v / (1.0 - ADAM_B2 ** ADAM_STEP)
    delta = -ADAM_LR * (m_hat / (jnp.sqrt(v_hat) + ADAM_EPS) + ADAM_WD * w)
    return delta, m, v


def _tile2(R, C, elems):
    if R % 16 == 0:
        return _row_tile(R, cap=max(16, elems // C // 16 * 16)), C
    tc = 128
    while C % (2 * tc) == 0 and R * 2 * tc <= elems:
        tc *= 2
    return R, tc


def _reduce_adamw(own, land, chip, w, m, v, name):
    _, R, C = own.shape
    tr, tc = _tile2(R, C, 1 << (19 if w.ndim == 3 else 18))

    def body(s_ref, o_ref, l1_ref, l2_ref, l3_ref, w_ref, m_ref, v_ref, g_ref, d_ref, nm_ref, nv_ref):
        g = o_ref[...].astype(F32) + l1_ref[...].astype(F32) + l2_ref[...].astype(F32) + l3_ref[...].astype(F32)
        g_ref[...] = g
        d_ref[...], nm_ref[...], nv_ref[...] = _adamw_math(w_ref[...], g, m_ref[...], v_ref[...])

    def part(flip):
        return pl.BlockSpec((None, tr, tc), lambda i, j, s: (s[0] ^ flip, i, j))

    if w.ndim == 3:
        blk = pl.BlockSpec((tr, None, tc), lambda i, j, s: (i, 0, j))
    else:
        blk = pl.BlockSpec((tr, tc), lambda i, j, s: (i, j))
    sds = jax.ShapeDtypeStruct(w.shape, F32)
    return pl.pallas_call(
        body, name=name,
        grid_spec=pltpu.PrefetchScalarGridSpec(
            num_scalar_prefetch=1, grid=(R // tr, C // tc),
            in_specs=[part(0), part(1), part(2), part(3), blk, blk, blk], out_specs=[blk] * 4),
        out_shape=[sds] * 4, compiler_params=_params("parallel", "parallel"),
    )(chip, own, land, land, land, w, m, v)


def _pair_sum(g8, recv, core, name):
    _, R, C = g8.shape
    tr, tc = _tile2(R, C, 1 << 21)

    def body(c_ref, a_ref, b_ref, o_ref):
        o_ref[...] = (a_ref[...].astype(F32) + b_ref[...].astype(F32)).astype(BF16)

    return pl.pallas_call(
        body, name=name,
        grid_spec=pltpu.PrefetchScalarGridSpec(
            num_scalar_prefetch=1, grid=(N_CHIP, R // tr, C // tc),
            in_specs=[pl.BlockSpec((None, tr, tc), lambda q, i, j, c: (2 * q + c[0], i, j)),
                      pl.BlockSpec((None, tr, tc), lambda q, i, j, c: (q, i, j))],
            out_specs=pl.BlockSpec((None, tr, tc), lambda q, i, j, c: (q, i, j))),
        out_shape=jax.ShapeDtypeStruct((N_CHIP, R, C), BF16),
        compiler_params=_params("parallel", "parallel", "parallel"),
    )(core, g8, recv)


def _place():
    return lax.axis_index("x"), lax.axis_index("y"), lax.axis_index("c")


def _land_with_own(shard, dev, name):
    R, C = shard.shape
    tr, tc = _tile2(R, C, 1 << 19)

    def body(d_ref, s_ref, o_ref):
        o_ref[...] = s_ref[...]

    return pl.pallas_call(
        body, name=name,
        grid_spec=pltpu.PrefetchScalarGridSpec(
            num_scalar_prefetch=1, grid=(R // tr, C // tc),
            in_specs=[pl.BlockSpec((tr, tc), lambda i, j, d: (i, j))],
            out_specs=pl.BlockSpec((None, tr, tc), lambda i, j, d: (d[0], i, j))),
        out_shape=jax.ShapeDtypeStruct((N_DEV, R, C), shard.dtype),
        compiler_params=_params("parallel", "parallel"),
    )(dev, shard)


def _all_gather(shards, lands, name):
    n = len(shards)

    def body(*refs):
        ins, outs = refs[:n], refs[2 * n:3 * n]
        send_sems, recv_sems = refs[3 * n:]
        x, y, c = _place()
        me, sibling = (x, y, c), (x, y, 1 - c)
        chips = [(1 - x, y), (x, 1 - y), (1 - x, 1 - y)]

        def copy(a, k, block, to, src=None):
            dst = outs[a].at[4 * block[0] + 2 * block[1] + block[2]]
            return pltpu.make_async_remote_copy(
                src_ref=dst if src is None else src, dst_ref=dst, send_sem=send_sems.at[a, k],
                recv_sem=recv_sems.at[a, k], device_id=to, device_id_type=MESH)

        sent = []
        for a in range(n):
            sent.append(copy(a, 0, me, sibling, src=ins[a]))
            sent += [copy(a, 1 + j, me, (*chip, c), src=ins[a]) for j, chip in enumerate(chips)]
        for cp in sent:
            cp.start()
        for j, chip in enumerate(chips):
            for a in range(n):
                copy(a, 1 + j, (*chip, c), me).wait_recv()
                fwd = copy(a, 4 + j, (*chip, c), sibling)
                fwd.start()
                sent.append(fwd)
        for a in range(n):
            copy(a, 0, sibling, me).wait_recv()
        for j, chip in enumerate(chips):
            for a in range(n):
                copy(a, 4 + j, (*chip, 1 - c), me).wait_recv()
        for cp in sent:
            cp.wait_send()

    return pl.pallas_call(
        body, name=name, in_specs=[ANY] * (2 * n), out_specs=[ANY] * n,
        out_shape=[jax.ShapeDtypeStruct(l.shape, l.dtype) for l in lands],
        input_output_aliases={n + a: a for a in range(n)},
        scratch_shapes=[pltpu.SemaphoreType.DMA((n, 7)), pltpu.SemaphoreType.DMA((n, 7))],
    )(*shards, *lands)


def _pair_plan(srcs, lands):
    x, y, c = _place()
    plan = []
    for a in range(len(srcs)):
        for q in range(N_CHIP):
            plan.append((srcs[a].at[2 * q + 1 - c], lands[a].at[q], lands[a].at[q], (x, y, 1 - c)))
    return plan


def _chip_plan(srcs, lands):
    x, y, c = _place()
    mine = 2 * x + y
    plan = []
    for a in range(len(srcs)):
        for px, py in [(1 - x, y), (x, 1 - y), (1 - x, 1 - y)]:
            peer = 2 * px + py
            plan.append((srcs[a].at[peer], lands[a].at[mine], lands[a].at[peer], (px, py, c)))
    return plan


def _gather_plan(srcs, lands):
    x, y, c = _place()
    mine = 4 * x + 2 * y + c
    plan = []
    for a in range(len(srcs)):
        for px, py, pc in [(x, y, 1 - c), (1 - x, y, c), (x, 1 - y, c), (1 - x, 1 - y, c)]:
            plan.append((srcs[a], lands[a].at[mine], lands[a].at[4 * px + 2 * py + pc], (px, py, pc)))
    return plan


def _forward_plan(srcs, lands):
    x, y, c = _place()
    plan = []
    for a in range(len(lands)):
        for px, py in [(1 - x, y), (x, 1 - y), (1 - x, 1 - y)]:
            mine, theirs = 4 * px + 2 * py + c, 4 * px + 2 * py + 1 - c
            plan.append((lands[a].at[mine], lands[a].at[mine], lands[a].at[theirs], (x, y, 1 - c)))
    return plan


def _remote(src, dst, send_sem, recv_sem, peer):
    return pltpu.make_async_remote_copy(src_ref=src, dst_ref=dst, send_sem=send_sem, recv_sem=recv_sem,
                                        device_id=peer, device_id_type=MESH)


HBM = pl.BlockSpec(memory_space=pltpu.HBM)
SEMS = pl.BlockSpec(memory_space=pltpu.SEMAPHORE)
DATAFLOW = pltpu.SideEffectType.DATAFLOW_SIDE_EFFECTING


def _split_start(srcs, lands, after, plan_fn, per_array, name):
    ns, nb = len(srcs), len(srcs) + len(lands)
    ncopy = per_array * len(lands)
    after = list(after) if isinstance(after, (list, tuple)) else [after]

    def body(*refs):
        send_sems, recv_sems = refs[nb + len(after)], refs[nb + len(after) + 1]
        token = refs[-1]
        for k, (src, dst, _, peer) in enumerate(plan_fn(refs[:ns], refs[ns:nb])):
            _remote(src, dst, send_sems.at[k], recv_sems.at[k], peer).start()
        token[...] = jnp.zeros_like(token)

    thru = [pltpu.HBM(a.shape, a.dtype) for a in (*srcs, *lands)]
    out = pl.pallas_call(
        body, name=name,
        out_shape=(pltpu.SemaphoreType.DMA((ncopy,)), pltpu.SemaphoreType.DMA((ncopy,)), *thru,
                   jax.ShapeDtypeStruct((8, 128), F32)),
        in_specs=[HBM] * nb + [ANY] * len(after),
        out_specs=(SEMS, SEMS, *[HBM] * nb, pl.BlockSpec(memory_space=pltpu.VMEM)),
        input_output_aliases={i: 2 + i for i in range(nb)},
        compiler_params=pltpu.CompilerParams(has_side_effects=DATAFLOW),
    )(*[pltpu.with_memory_space_constraint(a, pltpu.HBM) for a in (*srcs, *lands)], *after)
    return out[0], out[1], list(out[2:2 + ns]), list(out[2 + ns:2 + nb]), out[-1]


def _split_wait(send_sems, recv_sems, srcs, lands, after, plan_fn, name):
    ns, nb = len(srcs), len(srcs) + len(lands)
    after = list(after) if isinstance(after, (list, tuple)) else [after]

    def body(*refs):
        send_ref, recv_ref = refs[nb], refs[nb + 1]
        for k, (src, dst, mine, peer) in enumerate(plan_fn(refs[:ns], refs[ns:nb])):
            _remote(src, dst, send_ref.at[k], recv_ref.at[k], peer).wait_send()
            _remote(src, mine, send_ref.at[k], recv_ref.at[k], peer).wait_recv()

    thru = [pltpu.HBM(a.shape, a.dtype) for a in (*srcs, *lands)]
    out = pl.pallas_call(
        body, name=name, out_shape=tuple(thru),
        in_specs=[HBM] * nb + [SEMS, SEMS] + [ANY] * len(after), out_specs=tuple([HBM] * nb),
        input_output_aliases={i: i for i in range(nb)},
        compiler_params=pltpu.CompilerParams(has_side_effects=DATAFLOW),
    )(*srcs, *lands, send_sems, recv_sems, *after)
    return list(out[:ns]), list(out[ns:])


def _gather_forward(lands, name):
    n = len(lands)

    def body(*refs):
        landed, out = refs[:n], refs[n:2 * n]
        send_sems, recv_sems = refs[2 * n:]
        x, y, c = _place()
        chips = [(1 - x, y), (x, 1 - y), (1 - x, 1 - y)]
        sent = []
        for a in range(n):
            for j, (px, py) in enumerate(chips):
                blk = 4 * px + 2 * py + c
                sent.append(_remote(landed[a].at[blk], out[a].at[blk], send_sems.at[a, j], recv_sems.at[a, j], (x, y, 1 - c)))
        for cp in sent:
            cp.start()
        for a in range(n):
            for j, (px, py) in enumerate(chips):
                blk = 4 * px + 2 * py + 1 - c
                _remote(landed[a].at[blk], out[a].at[blk], send_sems.at[a, j], recv_sems.at[a, j], (x, y, 1 - c)).wait_recv()
        for cp in sent:
            cp.wait_send()

    return pl.pallas_call(
        body, name=name, in_specs=[ANY] * n, out_specs=[ANY] * n,
        out_shape=[jax.ShapeDtypeStruct(l.shape, l.dtype) for l in lands],
        input_output_aliases={a: a for a in range(n)},
        scratch_shapes=[pltpu.SemaphoreType.DMA((n, 3)), pltpu.SemaphoreType.DMA((n, 3))],
    )(*lands)


def _pack_rows(groups, name):
    n = len(groups[0])
    rows = -(-n // 8) * 8
    width = max(v.shape[1] for v in groups[0])

    def body(*refs):
        ins, outs = refs[:n * len(groups)], refs[n * len(groups):]
        for gi, o_ref in enumerate(outs):
            o_ref[...] = jnp.zeros_like(o_ref)
            for r in range(n):
                v_ref = ins[gi * n + r]
                o_ref[r:r + 1, 0:v_ref.shape[1]] = v_ref[...]

    return pl.pallas_call(
        body, name=name, out_shape=[jax.ShapeDtypeStruct((rows, width), F32)] * len(groups), compiler_params=_params(),
    )(*[v for g in groups for v in g])


def _unpack_rows(packs, like, name):
    n = len(like)

    def body(*refs):
        ins, outs = refs[:len(packs)], refs[len(packs):]
        for pi, p_ref in enumerate(ins):
            for r in range(n):
                o_ref = outs[pi * n + r]
                o_ref[...] = p_ref[r:r + 1, 0:o_ref.shape[1]]

    flat = pl.pallas_call(
        body, name=name, out_shape=[jax.ShapeDtypeStruct(a.shape, F32) for _ in packs for a in like],
        compiler_params=_params(),
    )(*packs)
    return [flat[pi * n:(pi + 1) * n] for pi in range(len(packs))]


def _sum_adamw(parts, w, m, v, name, col=None):
    P, R, _ = parts.shape
    C = w.shape[1]
    tr = _row_tile(R, cap=512, mult=8)

    def body(c_ref, p_ref, w_ref, m_ref, v_ref, g_ref, d_ref, nm_ref, nv_ref):
        g = p_ref[0].astype(F32)
        for q in range(1, P):
            g = g + p_ref[q].astype(F32)
        g_ref[...] = g
        d_ref[...], nm_ref[...], nv_ref[...] = _adamw_math(w_ref[...], g, m_ref[...], v_ref[...])

    blk = pl.BlockSpec((tr, C), lambda i, c: (i, 0))
    sds = jax.ShapeDtypeStruct((R, C), F32)
    at = jnp.zeros((1,), jnp.int32) if col is None else col
    return pl.pallas_call(
        body, name=name,
        grid_spec=pltpu.PrefetchScalarGridSpec(
            num_scalar_prefetch=1, grid=(R // tr,),
            in_specs=[pl.BlockSpec((P, tr, C), lambda i, c: (0, i, c[0])), blk, blk, blk], out_specs=[blk] * 4),
        out_shape=[sds] * 4, compiler_params=_params("parallel"),
    )(at, parts, w, m, v)


def _local_step(x, mem, target, sp, W, arrived, want, done, paired):
    h0 = _rms_fwd([x], [sp["g_mix"]], "rms_mix")
    proj = _mm(h0, W["main_t"], "nt", F32, "proj_main", tm=2048, tn=512)
    f_logit = _mm(h0, W["f_t"], "nt", F32, "proj_f", tm=2048)
    qn, kn, vb = _qk_fwd(proj, sp["g_q"], sp["g_k"], "qk_norm")
    b_pad = jnp.pad(sp["b_f"], ((0, 0), (0, HEAD - FOX_H)))
    c_col, c_row = _fgate_fwd(f_logit, b_pad, "forget_cumsum")
    o_fox, lse = _fox_fwd(qn, kn, vb, c_col, c_row, "fox_fwd")
    lru_p = (sp["conv_w"], sp["conv_b"], sp["w_ra"], sp["b_ra"], sp["w_ri"], sp["b_ri"], sp["lam"])
    y_lru, h_lru = _lru_fwd(proj, *lru_p, "lru_fwd", after=arrived("attn", o_fox))
    w_out, w_cq, w_ckv, w_co = want("attn", y_lru)
    mix = _rms_fwd([o_fox, y_lru], [sp["g_fox_out"], sp["g_lru_out"]], "rms_mix_out")
    x1 = _mm(mix, w_out, "nn", F32, "out_proj", add=x)
    hq = _rms_fwd([x1], [sp["g_xattn"]], "rms_xattn")
    mn = _rms_fwd([mem], [sp["g_mem"]], "rms_mem")
    cq = _mm(hq, w_cq, "nn", F32, "xattn_q", tm=2048)
    ckv = _mm(mn, w_ckv, "nn", F32, "xattn_kv")
    o_x = _xattn_fwd(cq, ckv, sp["g_cq"], sp["g_ck"], "xattn_fwd")
    co_w = w_co.shape[2]
    x2 = _mm(o_x, w_co, "nn", F32, "xattn_out", add=x1, tm=2048, tn=co_w, after=arrived("gate_up", o_x))
    hf = _rms_fwd([x2], [sp["g_ffn"]], "rms_ffn")
    (w_gu,) = want("gate_up", hf)
    gu_w = w_gu.shape[2]
    gu = _mm(hf, w_gu, "nn", F32, "ffn_up", tn=gu_w)
    act = _swiglu_fwd(gu, "swiglu_fwd", after=arrived("down", gu))
    (w_down,) = want("down", act)
    x3 = _mm(act, w_down, "nn", F32, "ffn_down", add=x2, tk=w_down.shape[0] // 2)
    dy, dyb, sq = _loss_head(x3, target, "loss_head")
    dact = _mm(dyb, w_down, "nt", F32, "ffn_down_dx", tm=2048, tn=512)
    g_down = _mm(act, dyb, "tn", BF16, "ffn_down_dw", tm=w_down.shape[0] // 4, tn=2048)
    dgu = _swiglu_bwd(gu, dact, "swiglu_bwd")
    g_gu = _mm(hf, dgu, "tn", BF16, "ffn_up_dw", out_blocks=N_DEV, tn=gu_w)
    tok = done("ffn", [g_gu, g_down])
    dhf = _mm(dgu, w_gu, "nt", F32, "ffn_up_dx", tn=2048, tk=gu_w, after=tok)
    tok = paired("ffn", dhf)
    dx2, dx2b, dg_ffn = _rms_bwd(x2, sp["g_ffn"], dhf, "rms_ffn_bwd", res=dy, after=tok)
    d_ox = _mm(dx2b, w_co, "nt", F32, "xattn_out_dx", tm=2048, tk=co_w)
    g_co = _mm(o_x, dx2b, "tn", BF16, "xattn_out_dw", out_blocks=N_DEV, tn=co_w)
    dcq, dckv, dg_cq, dg_ck = _xattn_bwd(cq, ckv, sp["g_cq"], sp["g_ck"], d_ox, "xattn_bwd")
    dhq = _mm(dcq, w_cq, "nt", F32, "xattn_q_dx", tm=2048)
    g_cq = _mm(hq, dcq, "tn", BF16, "xattn_q_dw")
    dmn = _mm(dckv, w_ckv, "nt", F32, "xattn_kv_dx")
    g_ckv = _mm(mn, dckv, "tn", BF16, "xattn_kv_dw")
    (dg_mem,) = _rms_bwd(mem, sp["g_mem"], dmn, "rms_mem_bwd", want_dx=False)
    dx1, dx1b, dg_xattn = _rms_bwd(x1, sp["g_xattn"], dhq, "rms_xattn_bwd", res=dx2)
    dmix = _mm(dx1b, w_out, "nt", F32, "out_proj_dx")
    g_out = _mm(mix, dx1b, "tn", BF16, "out_proj_dw", tn=2048)
    tok = done("attn", [g_out, g_cq, g_ckv, g_co])
    do_fox, _, dg_fox_out = _rms_bwd(o_fox, sp["g_fox_out"], dmix, "rms_fox_out_bwd", dy_col=0, after=tok)
    dy_lru, _, dg_lru_out = _rms_bwd(y_lru, sp["g_lru_out"], dmix, "rms_lru_out_bwd", dy_col=1)
    tok = paired("attn", dy_lru)
    du, dgate, dconv_w, dconv_b, dw_ra, db_ra, dw_ri, db_ri, dlam = _lru_bwd(proj, h_lru, dy_lru, *lru_p, "lru_bwd", after=tok)
    dqn, dkn, dv, dc_col, dc_row = _fox_bwd(qn, kn, vb, c_col, c_row, lse, o_fox, do_fox, "fox_bwd")
    dq, dk, dg_q, dg_k = _qk_bwd(proj, sp["g_q"], sp["g_k"], dqn, dkn, "qk_norm_bwd")
    df, db_f = _fgate_bwd(f_logit, b_pad, dc_col, dc_row, "forget_cumsum_bwd")
    dproj = jnp.concatenate([dq, dk, dv, du, dgate], axis=1)
    g_main_t = _mm(dproj, h0, "tn", BF16, "proj_main_dw", tn=2048)
    g_f_t = _mm(df, h0, "tn", BF16, "proj_f_dw", tn=2048)
    tok = done("w_in", [g_main_t, g_f_t])
    dh_f = _mm(df, W["f_t"], "nn", F32, "proj_f_dx", tm=2048)
    dh0 = _mm(dproj, W["main_t"], "nn", F32, "proj_main_dx", add=dh_f, tk=dproj.shape[1] // 2, after=tok)
    tok = paired("w_in", dh0)
    grad_x, _, dg_mix = _rms_bwd(x, sp["g_mix"], dh0, "rms_mix_bwd", res=dx1, after=tok)
    small = dict(g_mix=dg_mix, b_f=db_f[:, :FOX_H], g_q=dg_q, g_k=dg_k, conv_w=dconv_w, conv_b=dconv_b, w_ra=dw_ra,
                 b_ra=db_ra, w_ri=dw_ri, b_ri=db_ri, lam=dlam, g_fox_out=dg_fox_out, g_lru_out=dg_lru_out,
                 g_xattn=dg_xattn, g_mem=dg_mem, g_cq=dg_cq, g_ck=dg_ck, g_ffn=dg_ffn)
    return sq, grad_x, small


BIG = ("w_in", "w_out", "w_cq", "w_ckv", "w_co", "w_gate_up", "w_down")
SMALL = ("g_mix", "b_f", "g_q", "g_k", "conv_b", "w_ra", "b_ra", "w_ri", "b_ri", "lam", "g_fox_out", "g_lru_out",
         "g_xattn", "g_mem", "g_cq", "g_ck", "g_ffn")
ORDER = ("g_mix", "w_in", "b_f", "g_q", "g_k", "conv_w", "conv_b", "w_ra", "b_ra", "w_ri", "b_ri", "lam", "g_fox_out",
         "g_lru_out", "w_out", "g_xattn", "g_mem", "w_cq", "w_ckv", "g_cq", "g_ck", "w_co", "g_ffn", "w_gate_up", "w_down")


def kernel(x, mem, g_mix, w_in, b_f, g_q, g_k, conv_w, conv_b, w_ra, b_ra, w_ri, b_ri, lam, g_fox_out, g_lru_out, w_out, g_xattn, g_mem, w_cq, w_ckv, g_cq, g_ck, w_co, g_ffn, w_gate_up, w_down, loss_target, m_g_mix, m_w_in, m_b_f, m_g_q, m_g_k, m_conv_w, m_conv_b, m_w_ra, m_b_ra, m_w_ri, m_b_ri, m_lam, m_g_fox_out, m_g_lru_out, m_w_out, m_g_xattn, m_g_mem, m_w_cq, m_w_ckv, m_g_cq, m_g_ck, m_w_co, m_g_ffn, m_w_gate_up, m_w_down, v_g_mix, v_w_in, v_b_f, v_g_q, v_g_k, v_conv_w, v_conv_b, v_w_ra, v_b_ra, v_w_ri, v_b_ri, v_lam, v_g_fox_out, v_g_lru_out, v_w_out, v_g_xattn, v_g_mem, v_w_cq, v_w_ckv, v_g_cq, v_g_ck, v_w_co, v_g_ffn, v_w_gate_up, v_w_down):
    given = dict(locals())
    w = {n: given[n] for n in ORDER}
    m = {n: given["m_" + n] for n in ORDER}
    v = {n: given["v_" + n] for n in ORDER}
    D = x.shape[2]
    fw = FOX_H * HEAD
    dev_index = 4 * lax.axis_index("x") + 2 * lax.axis_index("y") + lax.axis_index("c")
    dev = jnp.reshape(dev_index, (1,)).astype(jnp.int32)
    core = jnp.reshape(lax.axis_index("c"), (1,)).astype(jnp.int32)
    chip = jnp.reshape(2 * lax.axis_index("x") + lax.axis_index("y"), (1,)).astype(jnp.int32)

    def shard(d, n):
        return jnp.transpose(d[n], (2, 0, 1)) if n == "w_in" else d[n][0]

    def unshard(a, n):
        return jnp.transpose(a, (1, 2, 0)) if n == "w_in" else a[None]

    gather_groups = dict(attn=("w_out", "w_cq", "w_ckv", "w_co"), gate_up=("w_gate_up",), down=("w_down",))
    reduce_groups = dict(ffn=("w_gate_up", "w_down"), attn=("w_out", "w_cq", "w_ckv", "w_co"), w_in=("w_in",))
    column_blocked = ("w_co", "w_gate_up")
    flying = {}

    def land(a, tag):
        return _land_with_own(a, dev, "own_" + tag)

    def launch(group, after):
        shards = [shard(w, n).astype(BF16) for n in gather_groups[group]]
        lands = [land(s, n) for s, n in zip(shards, gather_groups[group])]
        flying[group] = _split_start(shards, lands, after, _gather_plan, 4, "gather_" + group + "_start")
        return flying[group][4]

    def arrived(group, after):
        send, recv, shards, lands, _ = flying.pop(group)
        _, lands = _split_wait(send, recv, shards, lands, after, _gather_plan, "gather_" + group + "_wait")
        flying[group] = _split_start([], lands, after, _forward_plan, 3, "gather_" + group + "_forward_start")
        if group == "attn":
            return launch("down", flying[group][4])
        return flying[group][4]

    def want(group, after):
        send, recv, _, lands, _ = flying.pop(group)
        _, full = _split_wait(send, recv, [], lands, after, _forward_plan, "gather_" + group + "_forward_wait")
        return [g if n in column_blocked else g.reshape(-1, g.shape[2]) for n, g in zip(gather_groups[group], full)]

    first = [shard(w, "w_in").astype(BF16).reshape(-1, D), conv_w[0]]
    g_in, g_conv = _all_gather(first, [land(first[0], "w_in"), land(first[1], "conv_w")], "gather_w_in")
    tok = launch("gate_up", launch("attn", g_conv))
    per = g_in.shape[1]
    f_k, f_lo = divmod(3 * fw, per)
    assert f_lo + FOX_H <= per
    wt_in = g_in.reshape(-1, D)
    W = dict(main_t=jnp.concatenate([wt_in[:3 * fw], wt_in[3 * fw + FOX_H:]], axis=0),
             f_t=jnp.pad(wt_in[3 * fw:3 * fw + FOX_H], ((0, HEAD - FOX_H), (0, 0))))
    sp = {n: w[n] for n in SMALL if n not in ("w_ra", "w_ri")}
    sp["w_ra"], sp["w_ri"] = w_ra[0], w_ri[0]
    sp["conv_w"] = jnp.transpose(g_conv, (1, 0, 2)).reshape(CONV_K, -1)
    sp["g_mix"] = sp["g_mix"] + tok[0, 0]

    pairing, reducing = {}, {}

    def done(group, grads):
        if group == "w_in":
            g_main_t, g_f_t = grads
            shards = [g_main_t[k * per:(k + 1) * per] for k in range(f_k)]
            shards.append(jnp.concatenate([g_main_t[f_k * per:3 * fw], g_f_t[:FOX_H],
                                           g_main_t[3 * fw:(f_k + 1) * per - FOX_H]], axis=0))
            shards += [g_main_t[k * per - FOX_H:(k + 1) * per - FOX_H] for k in range(f_k + 1, N_DEV)]
            grads = [jnp.stack(shards)]
        g8 = [g if g.ndim == 3 else g.reshape(N_DEV, -1, g.shape[1]) for g in grads]
        lands = [lax.empty((N_CHIP,) + g.shape[1:], g.dtype) for g in g8]
        pairing[group] = _split_start(g8, lands, chip, _pair_plan, N_CHIP, "reduce_" + group + "_pair_start")
        return pairing[group][4]

    def paired(group, after):
        send, recv, g8, lands, _ = pairing.pop(group)
        g8, from_sibling = _split_wait(send, recv, g8, lands, after, _pair_plan, "reduce_" + group + "_pair_wait")
        p4 = [_pair_sum(g, r, core, "reduce_pair_sum_" + n) for g, r, n in zip(g8, from_sibling, reduce_groups[group])]
        lands = [lax.empty(p.shape, p.dtype) for p in p4]
        reducing[group] = _split_start(p4, lands, chip, _chip_plan, 3, "reduce_" + group + "_start")
        return reducing[group][4]

    def finish(group, after):
        send, recv, p4, lands, _ = reducing.pop(group)
        p4, lands = _split_wait(send, recv, p4, lands, after, _chip_plan, "reduce_" + group + "_wait")
        return {n: tuple(unshard(r, n) for r in _reduce_adamw(p, l, chip, shard(w, n), shard(m, n), shard(v, n), "adamw_" + n))
                for n, p, l in zip(reduce_groups[group], p4, lands)}

    sq, grad_x, gs = _local_step(x[0], mem[0], loss_target[0], sp, W, arrived, want, done, paired)

    vectors = tuple(n for n in SMALL if n not in ("w_ra", "w_ri"))
    mine = [_pack_rows([[gs[n] for n in vectors] + [sq[0:1]]], "pack_small_grads")[0], gs["w_ra"].reshape(-1, HEAD),
            gs["w_ri"].reshape(-1, HEAD), gs["conv_w"]]
    lands = [land(a, "small_grads_%d" % k) for k, a in enumerate(mine)]
    s_send, s_recv, s_src, s_land, tok = _split_start(mine, lands, grad_x, _gather_plan, 4, "gather_small_start")
    out = finish("ffn", tok)
    out.update(finish("attn", tok))
    updated = [r for n in reduce_groups["ffn"] + reduce_groups["attn"] for r in out[n]]
    _, s_land = _split_wait(s_send, s_recv, s_src, s_land, updated, _gather_plan, "gather_small_wait")
    all_vec, all_ra, all_ri, all_conv = _gather_forward(s_land, "gather_small_forward")

    nothing = jnp.zeros((1, HEAD), F32)
    state = _pack_rows([[d[n] for n in vectors] + [nothing] for d in (w, m, v)], "pack_small_state")
    upd = _sum_adamw(all_vec, *state, "adamw_small_vectors")
    per_vector = _unpack_rows(upd, [w[n] for n in vectors] + [nothing], "unpack_small")
    for r, n in enumerate(vectors):
        out[n] = tuple(per_vector[k][r] for k in range(4))
    loss = 0.5 * per_vector[0][len(vectors)][0, 0] / D
    for n, parts in (("w_ra", all_ra), ("w_ri", all_ri)):
        res = _sum_adamw(parts, *[d[n].reshape(-1, HEAD) for d in (w, m, v)], "adamw_" + n)
        out[n] = tuple(r.reshape(w[n].shape) for r in res)
    out["conv_w"] = tuple(r[None] for r in _sum_adamw(all_conv, conv_w[0], m_conv_w[0], v_conv_w[0], "adamw_conv_w", col=dev))
    out.update(finish("w_in", upd[1]))

    return (loss, grad_x[None], *[out[n][0] for n in ORDER], *[out[n][1] for n in ORDER],
            *[out[n][2] for n in ORDER], *[out[n][3] for n in ORDER])
```

```python
import functools
import math

import jax
import jax.numpy as jnp
from jax import lax
from jax.experimental import pallas as pl
from jax.experimental.pallas import tpu as pltpu

F32 = jnp.float32
BF16 = jnp.bfloat16
MESH = pl.DeviceIdType.MESH
N_DEV = 8
N_CHIP = 4

HEAD = 128
FOX_H = 8
XATT_H = 4
LRU_NB = 8
CONV_K = 4
LRU_C = 8.0
RMS_EPS = 1e-6
ATT_T = 1024
CUM_T = 256
ROW_T = 256
NEG = -1e30
V7X_VMEM_LIMIT = 48 * 1024 * 1024

ADAM_LR = 0.001
ADAM_B1 = 0.9
ADAM_B2 = 0.999
ADAM_EPS = 1e-08
ADAM_WD = 0.01
ADAM_STEP = 10

NT = (((1,), (1,)), ((), ()))
TN = (((0,), (0,)), ((), ()))
NN = (((1,), (0,)), ((), ()))

ANY = pl.BlockSpec(memory_space=pl.ANY)


def _behind(after):
    return ([], []) if after is None else ([ANY], [after])


def _params(*sem):
    return pltpu.CompilerParams(dimension_semantics=sem or None, vmem_limit_bytes=V7X_VMEM_LIMIT)


def _dot(a, b, dn=NN):
    return lax.dot_general(a, b, dn, preferred_element_type=F32)


def _rms(x, g):
    r = lax.rsqrt(jnp.mean(x * x, axis=-1, keepdims=True) + RMS_EPS)
    return x * r * g


def _rms_grad(x, g, dy):
    r = lax.rsqrt(jnp.mean(x * x, axis=-1, keepdims=True) + RMS_EPS)
    xh = x * r
    dxh = dy * g
    dx = r * (dxh - xh * jnp.mean(dxh * xh, axis=-1, keepdims=True))
    return dx, jnp.sum(dy * xh, axis=0, keepdims=True)


def _gelu(x):
    k = math.sqrt(2.0 / math.pi)
    return 0.5 * x * (1.0 + jnp.tanh(k * (x + 0.044715 * (x * x * x))))


def _gelu_grad(x):
    k = math.sqrt(2.0 / math.pi)
    t = jnp.tanh(k * (x + 0.044715 * (x * x * x)))
    return 0.5 * (1.0 + t) + 0.5 * x * (1.0 - t * t) * k * (1.0 + 3.0 * 0.044715 * x * x)


def _one_minus_exp(z):
    series = -(z + 0.5 * z * z + (1.0 / 6.0) * z * z * z)
    return jnp.where(z > -1e-3, series, 1.0 - jnp.exp(z))


def _row_tile(rows, cap=ROW_T, mult=16):
    t = min(rows, cap)
    while rows % t or (t % mult and t != rows):
        t -= 1
    return t


def _mat_dims(z):
    return (z.shape[-2], z.shape[-1] * (z.shape[0] if z.ndim == 3 else 1))


def _mat_spec(arr, rblk, cblk, rsel, csel):
    if arr.ndim == 2:
        return pl.BlockSpec((rblk, cblk), lambda i, j, k: ((i, j, k)[rsel], (i, j, k)[csel]))
    nw = arr.shape[2]
    assert nw % cblk == 0, (arr.shape, cblk)
    per = nw // cblk
    return pl.BlockSpec((None, rblk, cblk),
                        lambda i, j, k: ((i, j, k)[csel] // per, (i, j, k)[rsel], (i, j, k)[csel] % per))


def _mm(a, b, mode, out_dtype, name, add=None, out_blocks=None, tm=1024, tn=1024, tk=None, after=None):
    ar, ac = _mat_dims(a)
    br, bc = _mat_dims(b)
    if mode == "nn":
        M, K, N = ar, ac, bc
        assert br == K
    elif mode == "nt":
        M, K, N = ar, ac, br
        assert bc == K
    else:
        M, K, N = ac, ar, bc
        assert br == K
    tm, tn = min(tm, M), min(tn, N)
    tk = K if tk is None or mode == "tn" else min(tk, K)
    assert M % tm == 0 and N % tn == 0 and K % tk == 0, (name, M, N, K, tm, tn, tk)
    nk = K // tk
    nj = N // tn
    if mode == "nn":
        specs = [_mat_spec(a, tm, tk, 0, 2), _mat_spec(b, tk, tn, 2, 1)]
        dn = NN
    elif mode == "nt":
        specs = [_mat_spec(a, tm, tk, 0, 2), _mat_spec(b, tn, tk, 1, 2)]
        dn = NT
    else:
        specs = [_mat_spec(a, tk, tm, 2, 0), _mat_spec(b, tk, tn, 2, 1)]
        dn = TN
    args = [a, b]
    if add is not None:
        specs.append(pl.BlockSpec((tm, tn), lambda i, j, k: (i, j)))
        args.append(add)
    specs += _behind(after)[0]
    args += _behind(after)[1]
    n_in = len(args)
    if out_blocks is None:
        out_shape = jax.ShapeDtypeStruct((M, N), out_dtype)
        out_spec = pl.BlockSpec((tm, tn), lambda i, j, k: (i, j))
    else:
        nb = out_blocks
        nw = N // nb
        assert nw % tn == 0
        per = nw // tn
        out_shape = jax.ShapeDtypeStruct((nb, M, nw), out_dtype)
        out_spec = pl.BlockSpec((None, tm, tn), lambda i, j, k: (j // per, i, j % per))
    keep_t = mode == "tn" and nj > 1
    scratch = []
    if nk > 1:
        scratch.append(pltpu.VMEM((tm, tn), F32))
    if keep_t:
        scratch.append(pltpu.VMEM((tm, tk), a.dtype))

    def body(*refs):
        a_ref, b_ref = refs[0], refs[1]
        add_ref = refs[2] if add is not None else None
        o_ref = refs[n_in]

        def finish(r):
            if add_ref is not None:
                r = r + add_ref[...]
            o_ref[...] = r.astype(out_dtype)

        if keep_t:
            at_ref = refs[-1]

            @pl.when(pl.program_id(1) == 0)
            def _():
                at_ref[...] = a_ref[...].T

            finish(_dot(at_ref[...], b_ref[...], NN))
        elif nk == 1:
            finish(_dot(a_ref[...], b_ref[...], dn))
        else:
            acc_ref = refs[n_in + 1]
            k = pl.program_id(2)

            @pl.when(k == 0)
            def _():
                acc_ref[...] = jnp.zeros_like(acc_ref)

            acc_ref[...] += _dot(a_ref[...], b_ref[...], dn)

            @pl.when(k == nk - 1)
            def _():
                finish(acc_ref[...])

    return pl.pallas_call(
        body, name=name, grid=(M // tm, nj, nk), in_specs=specs, out_specs=out_spec, out_shape=out_shape,
        scratch_shapes=scratch,
        compiler_params=_params("parallel", "arbitrary" if keep_t else "parallel", "arbitrary"),
    )(*args)


def _rms_fwd(xs, gs, name, after=None):
    n = len(xs)
    rows = xs[0].shape[0]
    widths = [x.shape[1] for x in xs]
    tr = _row_tile(rows)
    more_specs, more = _behind(after)

    def body(*refs):
        o_ref = refs[2 * n + len(more)]
        off = 0
        for t in range(n):
            o_ref[:, off:off + widths[t]] = _rms(refs[t][...], refs[n + t][...]).astype(BF16)
            off += widths[t]

    return pl.pallas_call(
        body, name=name, grid=(rows // tr,),
        in_specs=[pl.BlockSpec((tr, w), lambda i: (i, 0)) for w in widths]
        + [pl.BlockSpec((1, w), lambda i: (0, 0)) for w in widths] + more_specs,
        out_specs=pl.BlockSpec((tr, sum(widths)), lambda i: (i, 0)),
        out_shape=jax.ShapeDtypeStruct((rows, sum(widths)), BF16),
        compiler_params=_params("parallel"),
    )(*xs, *gs, *more)


def _rms_bwd(x, g, dy, name, dy_col=0, res=None, want_dx=True, after=None):
    rows, w = x.shape
    tr = _row_tile(rows)
    has_res = res is not None
    more_specs, more = _behind(after)

    def body(*refs):
        x_ref, g_ref, dy_ref = refs[:3]
        res_ref = refs[3] if has_res else None
        outs = refs[3 + has_res + len(more):]
        dg_ref = outs[-1]
        dx, dg = _rms_grad(x_ref[...], g_ref[...], dy_ref[...])

        @pl.when(pl.program_id(0) == 0)
        def _():
            dg_ref[...] = jnp.zeros_like(dg_ref)

        dg_ref[...] += dg
        if want_dx:
            if has_res:
                dx = dx + res_ref[...]
            outs[0][...] = dx
            outs[1][...] = dx.astype(BF16)

    row_spec = pl.BlockSpec((tr, w), lambda i: (i, 0))
    in_specs = [row_spec, pl.BlockSpec((1, w), lambda i: (0, 0)), pl.BlockSpec((tr, w), lambda i: (i, dy_col))]
    args = [x, g, dy]
    if has_res:
        in_specs.append(row_spec)
        args.append(res)
    in_specs += more_specs
    args += more
    out_specs, out_shape = [], []
    if want_dx:
        out_specs += [row_spec, row_spec]
        out_shape += [jax.ShapeDtypeStruct((rows, w), F32), jax.ShapeDtypeStruct((rows, w), BF16)]
    out_specs.append(pl.BlockSpec((1, w), lambda i: (0, 0)))
    out_shape.append(jax.ShapeDtypeStruct((1, w), F32))
    return pl.pallas_call(
        body, name=name, grid=(rows // tr,), in_specs=in_specs, out_specs=out_specs, out_shape=out_shape,
        compiler_params=_params("arbitrary"),
    )(*args)


def _loss_head(y, target, name):
    rows, w = y.shape
    tr = _row_tile(rows)

    def body(y_ref, t_ref, dy_ref, dyb_ref, acc_ref):
        e = y_ref[...] - t_ref[...]

        @pl.when(pl.program_id(0) == 0)
        def _():
            acc_ref[...] = jnp.zeros_like(acc_ref)

        acc_ref[...] += jnp.sum(e * e)
        dy = e * (1.0 / w)
        dy_ref[...] = dy
        dyb_ref[...] = dy.astype(BF16)

    row_spec = pl.BlockSpec((tr, w), lambda i: (i, 0))
    return pl.pallas_call(
        body, name=name, grid=(rows // tr,), in_specs=[row_spec, row_spec],
        out_specs=[row_spec, row_spec, pl.BlockSpec((8, 128), lambda i: (0, 0))],
        out_shape=[jax.ShapeDtypeStruct((rows, w), F32), jax.ShapeDtypeStruct((rows, w), BF16),
                   jax.ShapeDtypeStruct((8, 128), F32)],
        compiler_params=_params("arbitrary"),
    )(y, target)


def _qk_fwd(proj, g_q, g_k, name, after=None):
    rows = proj.shape[0]
    w = FOX_H * HEAD
    tr = _row_tile(rows)
    more_specs, more = _behind(after)

    def body(q_ref, k_ref, v_ref, gq_ref, gk_ref, *rest):
        qn_ref, kn_ref, vb_ref = rest[len(more):]
        for h in range(FOX_H):
            sl = slice(h * HEAD, (h + 1) * HEAD)
            qn_ref[:, sl] = _rms(q_ref[:, sl], gq_ref[...]).astype(BF16)
            kn_ref[:, sl] = _rms(k_ref[:, sl], gk_ref[...]).astype(BF16)
        vb_ref[...] = v_ref[...].astype(BF16)

    gain = pl.BlockSpec((1, HEAD), lambda i: (0, 0))
    out = pl.BlockSpec((tr, w), lambda i: (i, 0))
    return pl.pallas_call(
        body, name=name, grid=(rows // tr,),
        in_specs=[pl.BlockSpec((tr, w), lambda i: (i, 0)), pl.BlockSpec((tr, w), lambda i: (i, 1)),
                  pl.BlockSpec((tr, w), lambda i: (i, 2)), gain, gain] + more_specs,
        out_specs=[out, out, out], out_shape=[jax.ShapeDtypeStruct((rows, w), BF16)] * 3,
        compiler_params=_params("parallel"),
    )(proj, proj, proj, g_q, g_k, *more)


def _qk_bwd(proj, g_q, g_k, dqn, dkn, name):
    rows = proj.shape[0]
    w = FOX_H * HEAD
    tr = _row_tile(rows)

    def body(q_ref, k_ref, gq_ref, gk_ref, dqn_ref, dkn_ref, dq_ref, dk_ref, dgq_ref, dgk_ref):
        @pl.when(pl.program_id(0) == 0)
        def _():
            dgq_ref[...] = jnp.zeros_like(dgq_ref)
            dgk_ref[...] = jnp.zeros_like(dgk_ref)

        for h in range(FOX_H):
            sl = slice(h * HEAD, (h + 1) * HEAD)
            dq, dgq = _rms_grad(q_ref[:, sl], gq_ref[...], dqn_ref[:, sl])
            dk, dgk = _rms_grad(k_ref[:, sl], gk_ref[...], dkn_ref[:, sl])
            dq_ref[:, sl] = dq.astype(BF16)
            dk_ref[:, sl] = dk.astype(BF16)
            dgq_ref[...] += dgq
            dgk_ref[...] += dgk

    gain = pl.BlockSpec((1, HEAD), lambda i: (0, 0))
    row = pl.BlockSpec((tr, w), lambda i: (i, 0))
    return pl.pallas_call(
        body, name=name, grid=(rows // tr,),
        in_specs=[row, pl.BlockSpec((tr, w), lambda i: (i, 1)), gain, gain, row, row],
        out_specs=[row, row, gain, gain],
        out_shape=[jax.ShapeDtypeStruct((rows, w), BF16)] * 2 + [jax.ShapeDtypeStruct((1, HEAD), F32)] * 2,
        compiler_params=_params("arbitrary"),
    )(proj, proj, g_q, g_k, dqn, dkn)


def _fgate_fwd(f_logit, b_pad, name):
    S = f_logit.shape[0]
    T = min(CUM_T, S)
    nb = S // T
    RT = min(ATT_T, S)
    nbr = S // RT

    def body(f_ref, b_ref, col_ref, row_ref, c_scr):
        tri = (lax.broadcasted_iota(jnp.int32, (T, T), 0) >= lax.broadcasted_iota(jnp.int32, (T, T), 1)).astype(F32)
        carry = jnp.zeros((1, HEAD), F32)
        for blk in range(nb):
            z = f_ref[blk * T:(blk + 1) * T, :] + b_ref[...]
            lf = jnp.minimum(z, 0.0) - jnp.log1p(jnp.exp(-jnp.abs(z)))
            cb = jnp.dot(tri, lf, precision=lax.Precision.HIGHEST, preferred_element_type=F32) + carry
            c_scr[blk * T:(blk + 1) * T, :] = cb
            carry = cb[T - 1:T, :]
        c = c_scr[...]
        lane = lax.broadcasted_iota(jnp.int32, c.shape, 1)
        ct = c.T
        for h in range(FOX_H):
            col_ref[h] = jnp.sum(jnp.where(lane == h, c, 0.0), axis=1, keepdims=True)
            for jb in range(nbr):
                row_ref[h, jb] = ct[h:h + 1, jb * RT:(jb + 1) * RT]

    return pl.pallas_call(
        body, name=name,
        out_shape=[jax.ShapeDtypeStruct((FOX_H, S, 1), F32), jax.ShapeDtypeStruct((FOX_H, nbr, 1, RT), F32)],
        scratch_shapes=[pltpu.VMEM((S, HEAD), F32)], compiler_params=_params(),
    )(f_logit, b_pad)


def _fgate_bwd(f_logit, b_pad, dc_col, dc_row, name):
    S = f_logit.shape[0]
    T = min(CUM_T, S)
    nb = S // T
    RT = min(ATT_T, S)

    def body(f_ref, b_ref, dcol_ref, drow_ref, df_ref, db_ref, dc_scr, rt_scr):
        lane = lax.broadcasted_iota(jnp.int32, (S, HEAD), 1)
        sub = lax.broadcasted_iota(jnp.int32, (HEAD, RT), 0)
        dc = jnp.zeros((S, HEAD), F32)
        for h in range(FOX_H):
            dc = jnp.where(lane == h, dcol_ref[h], dc)
        for jb in range(S // RT):
            rt = jnp.zeros((HEAD, RT), F32)
            for h in range(FOX_H):
                rt = jnp.where(sub == h, drow_ref[h, jb], rt)
            rt_scr[jb * RT:(jb + 1) * RT, :] = rt.T
        dc_scr[...] = dc - rt_scr[...]
        tri = (lax.broadcasted_iota(jnp.int32, (T, T), 0) <= lax.broadcasted_iota(jnp.int32, (T, T), 1)).astype(F32)
        carry = jnp.zeros((1, HEAD), F32)
        db = jnp.zeros((1, HEAD), F32)
        for blk in reversed(range(nb)):
            rows = slice(blk * T, (blk + 1) * T)
            dlf = jnp.dot(tri, dc_scr[rows, :], precision=lax.Precision.HIGHEST, preferred_element_type=F32) + carry
            carry = dlf[0:1, :]
            z = f_ref[rows, :] + b_ref[...]
            df = dlf * jax.nn.sigmoid(-z)
            df_ref[rows, :] = df.astype(BF16)
            db = db + jnp.sum(df, axis=0, keepdims=True)
        db_ref[...] = db

    return pl.pallas_call(
        body, name=name,
        out_shape=[jax.ShapeDtypeStruct((S, HEAD), BF16), jax.ShapeDtypeStruct((1, HEAD), F32)],
        scratch_shapes=[pltpu.VMEM((S, HEAD), F32), pltpu.VMEM((S, HEAD), F32)], compiler_params=_params(),
    )(f_logit, b_pad, dc_col, dc_row)


def _fox_fwd(qn, kn, vb, c_col, c_row, name, after=None):
    S = qn.shape[0]
    T = min(ATT_T, S)
    nb = S // T
    scale = 1.0 / math.sqrt(HEAD)
    more_specs, more = _behind(after)

    def body(q_ref, k_ref, v_ref, cc_ref, cr_ref, *rest):
        o_ref, lse_ref = rest[len(more):]
        i = pl.program_id(1)
        q = q_ref[...]
        cc = cc_ref[...]

        def step(j, carry, diagonal=False):
            m, l, acc = carry
            sl = pl.ds(pl.multiple_of(j * T, T), T)
            s = _dot(q, k_ref[sl, :], NT) * scale + cc - cr_ref[j]
            if diagonal:
                s = jnp.where(lax.broadcasted_iota(jnp.int32, (T, T), 0) >= lax.broadcasted_iota(jnp.int32, (T, T), 1), s, NEG)
            m2 = jnp.maximum(m, jnp.max(s, axis=1, keepdims=True))
            p = jnp.exp(s - m2)
            al = jnp.exp(m - m2)
            return m2, al * l + jnp.sum(p, axis=1, keepdims=True), al * acc + _dot(p.astype(BF16), v_ref[sl, :])

        init = (jnp.full((T, 1), NEG, F32), jnp.zeros((T, 1), F32), jnp.zeros((T, HEAD), F32))
        m, l, acc = step(i, lax.fori_loop(0, i, step, init), diagonal=True)
        o_ref[...] = acc / l
        lse_ref[...] = m + jnp.log(l)

    head = pl.BlockSpec((S, HEAD), lambda h, i: (0, h))
    return pl.pallas_call(
        body, name=name, grid=(FOX_H, nb),
        in_specs=[pl.BlockSpec((T, HEAD), lambda h, i: (i, h)), head, head,
                  pl.BlockSpec((None, T, 1), lambda h, i: (h, i, 0)),
                  pl.BlockSpec((None, nb, 1, T), lambda h, i: (h, 0, 0, 0))] + more_specs,
        out_specs=[pl.BlockSpec((T, HEAD), lambda h, i: (i, h)), pl.BlockSpec((None, T, 1), lambda h, i: (h, i, 0))],
        out_shape=[jax.ShapeDtypeStruct((S, FOX_H * HEAD), F32), jax.ShapeDtypeStruct((FOX_H, S, 1), F32)],
        compiler_params=_params("parallel", "parallel"),
    )(qn, kn, vb, c_col, c_row, *more)


def _fox_bwd(qn, kn, vb, c_col, c_row, lse, o, do, name):
    S = qn.shape[0]
    T = min(ATT_T, S)
    nb = S // T
    scale = 1.0 / math.sqrt(HEAD)

    def body(q_ref, k_ref, v_ref, cc_ref, cr_ref, lse_ref, o_ref, do_ref, dq_ref, dk_ref, dv_ref, dcc_ref, dcr_ref):
        j = pl.program_id(1)

        @pl.when(j == 0)
        def _():
            dq_ref[...] = jnp.zeros_like(dq_ref)
            dcc_ref[...] = jnp.zeros_like(dcc_ref)

        k = k_ref[...]
        v = v_ref[...]
        cr = cr_ref[...]

        def step(i, carry, diagonal=False):
            dk, dv, dcr = carry
            sl = pl.ds(pl.multiple_of(i * T, T), T)
            q = q_ref[sl, :]
            d_o = do_ref[sl, :]
            s = _dot(q, k, NT) * scale + cc_ref[sl, :] - cr
            if diagonal:
                s = jnp.where(lax.broadcasted_iota(jnp.int32, (T, T), 0) >= lax.broadcasted_iota(jnp.int32, (T, T), 1), s, NEG)
            p = jnp.exp(s - lse_ref[sl, :])
            dob = d_o.astype(BF16)
            dp = _dot(dob, v, NT)
            delta = jnp.sum(d_o * o_ref[sl, :], axis=1, keepdims=True)
            ds = p * (dp - delta)
            dsb = ds.astype(BF16)
            dq_ref[sl, :] += _dot(dsb, k) * scale
            dcc_ref[sl, :] += jnp.sum(ds, axis=1, keepdims=True)
            return (dk + _dot(dsb, q, TN), dv + _dot(p.astype(BF16), dob, TN), dcr + jnp.sum(ds, axis=0, keepdims=True))

        init = (jnp.zeros((T, HEAD), F32), jnp.zeros((T, HEAD), F32), jnp.zeros((1, T), F32))
        dk, dv, dcr = lax.fori_loop(j + 1, nb, step, step(j, init, diagonal=True))
        dk_ref[...] = dk * scale
        dv_ref[...] = dv.astype(BF16)
        dcr_ref[...] = dcr

    head = pl.BlockSpec((S, HEAD), lambda h, j: (0, h))
    tile = pl.BlockSpec((T, HEAD), lambda h, j: (j, h))
    col = pl.BlockSpec((None, S, 1), lambda h, j: (h, 0, 0))
    row = pl.BlockSpec((None, None, 1, T), lambda h, j: (h, j, 0, 0))
    w = FOX_H * HEAD
    return pl.pallas_call(
        body, name=name, grid=(FOX_H, nb),
        in_specs=[head, tile, tile, col, row, col, head, head],
        out_specs=[head, tile, tile, col, row],
        out_shape=[jax.ShapeDtypeStruct((S, w), F32), jax.ShapeDtypeStruct((S, w), F32), jax.ShapeDtypeStruct((S, w), BF16),
                   jax.ShapeDtypeStruct((FOX_H, S, 1), F32), jax.ShapeDtypeStruct((FOX_H, nb, 1, T), F32)],
        compiler_params=_params("arbitrary", "arbitrary"),
    )(qn, kn, vb, c_col, c_row, lse, o, do)


def _lru_gates(uc, wra, bra, wri, bri, lam):
    ucb = uc.astype(BF16)
    r = jax.nn.sigmoid(_dot(ucb, wra.astype(BF16)) + bra)
    ig = jax.nn.sigmoid(_dot(ucb, wri.astype(BF16)) + bri)
    sp = jnp.maximum(-lam, 0.0) + jnp.log1p(jnp.exp(-jnp.abs(lam)))
    log_a = -LRU_C * r * sp
    a = jnp.exp(log_a)
    mult = jnp.sqrt(_one_minus_exp(2.0 * log_a))
    return r, ig, sp, a, mult


def _conv(pad_ref, cw, cb, S):
    uc = cb
    for j in range(CONV_K):
        uc = uc + cw[j:j + 1, :] * pad_ref[5 + j:5 + j + S, :]
    return uc


def _lru_specs(S, n_proj_cols):
    u_col = 3 * FOX_H
    g_col = u_col + LRU_NB
    blk = pl.BlockSpec((S, HEAD), lambda n: (0, n))
    vec = pl.BlockSpec((1, HEAD), lambda n: (0, n))
    mat = pl.BlockSpec((None, HEAD, HEAD), lambda n: (n, 0, 0))
    return dict(
        u=pl.BlockSpec((S, HEAD), lambda n: (0, u_col + n)), gate=pl.BlockSpec((S, HEAD), lambda n: (0, g_col + n)),
        blk=blk, vec=vec, mat=mat, cw=pl.BlockSpec((CONV_K, HEAD), lambda n: (0, n)))


def _lru_fwd(proj, conv_w, conv_b, w_ra, b_ra, w_ri, b_ri, lam, name, after=None):
    S = proj.shape[0]
    sp_ = _lru_specs(S, proj.shape[1])
    rows8 = S // 8
    more_specs, more = _behind(after)

    def body(u_ref, gt_ref, cw_ref, cb_ref, wra_ref, bra_ref, wri_ref, bri_ref, lam_ref, *rest):
        y_ref, h_ref, pad, a_scr, b_scr = rest[len(more):]
        pad[0:8, :] = jnp.zeros((8, HEAD), F32)
        pad[8:S + 8, :] = u_ref[...]
        uc = _conv(pad, cw_ref[...], cb_ref[...], S)
        r, ig, sp, a, mult = _lru_gates(uc, wra_ref[...], bra_ref[...], wri_ref[...], bri_ref[...], lam_ref[...])
        a_scr[...] = a
        b_scr[...] = mult * (ig * uc)
        sub = lax.broadcasted_iota(jnp.int32, (8, HEAD), 0)

        def step(t, carry):
            sl = pl.ds(pl.multiple_of(t * 8, 8), 8)
            A, B = a_scr[sl, :], b_scr[sl, :]
            for d in (1, 2, 4):
                m = sub >= d
                B = jnp.where(m, A * pltpu.roll(B, d, 0) + B, B)
                A = jnp.where(m, A * pltpu.roll(A, d, 0), A)
            h = A * carry + B
            h_ref[sl, :] = h
            return h[7:8, :]

        lax.fori_loop(0, rows8, step, jnp.zeros((1, HEAD), F32))
        y_ref[...] = h_ref[...] * _gelu(gt_ref[...])

    w = LRU_NB * HEAD
    return pl.pallas_call(
        body, name=name, grid=(LRU_NB,),
        in_specs=[sp_["u"], sp_["gate"], sp_["cw"], sp_["vec"], sp_["mat"], sp_["vec"], sp_["mat"], sp_["vec"], sp_["vec"]]
        + more_specs,
        out_specs=[sp_["blk"], sp_["blk"]],
        out_shape=[jax.ShapeDtypeStruct((S, w), F32)] * 2,
        scratch_shapes=[pltpu.VMEM((S + 8, HEAD), F32), pltpu.VMEM((S, HEAD), F32), pltpu.VMEM((S, HEAD), F32)],
        compiler_params=_params("parallel"),
    )(proj, proj, conv_w, conv_b, w_ra, b_ra, w_ri, b_ri, lam, *more)


def _lru_bwd(proj, h, dy, conv_w, conv_b, w_ra, b_ra, w_ri, b_ri, lam, name, after=None):
    S = proj.shape[0]
    sp_ = _lru_specs(S, proj.shape[1])
    rows8 = S // 8
    more_specs, more = _behind(after)

    def body(u_ref, gt_ref, h_ref, dy_ref, cw_ref, cb_ref, wra_ref, bra_ref, wri_ref, bri_ref, lam_ref, *rest):
        (du_ref, dgt_ref, dcw_ref, dcb_ref, dwra_ref, dbra_ref, dwri_ref, dbri_ref, dlam_ref,
         pad, an_scr, d_scr, g_scr, hp_scr) = rest[len(more):]
        zero8 = jnp.zeros((8, HEAD), F32)
        pad[0:8, :] = zero8
        pad[8:S + 8, :] = u_ref[...]
        cw = cw_ref[...]
        uc = _conv(pad, cw, cb_ref[...], S)
        wra, wri, lam_v = wra_ref[...], wri_ref[...], lam_ref[...]
        r, ig, sp, a, mult = _lru_gates(uc, wra, bra_ref[...], wri, bri_ref[...], lam_v)
        gate = gt_ref[...]
        dy_v = dy_ref[...]
        hv = h_ref[...]
        dgt_ref[...] = (dy_v * hv * _gelu_grad(gate)).astype(BF16)
        d_scr[...] = dy_v * _gelu(gate)
        g_scr[0:S, :] = a
        g_scr[S:S + 8, :] = zero8
        an_scr[...] = g_scr[1:S + 1, :]
        sub = lax.broadcasted_iota(jnp.int32, (8, HEAD), 0)

        def step(t, carry):
            sl = pl.ds(pl.multiple_of((rows8 - 1 - t) * 8, 8), 8)
            A, D = an_scr[sl, :], d_scr[sl, :]
            for d in (1, 2, 4):
                m = sub + d <= 7
                D = jnp.where(m, A * pltpu.roll(D, 8 - d, 0) + D, D)
                A = jnp.where(m, A * pltpu.roll(A, 8 - d, 0), A)
            g = A * carry + D
            g_scr[sl, :] = g
            return g[0:1, :]

        lax.fori_loop(0, rows8, step, jnp.zeros((1, HEAD), F32))
        g = g_scr[0:S, :]
        hp_scr[0:8, :] = zero8
        hp_scr[8:S + 8, :] = hv
        da = g * hp_scr[7:S + 7, :]
        iu = ig * uc
        dmult = g * iu
        diu = g * mult
        dig = diu * uc
        duc = diu * ig
        dlog_a = da * a - dmult * (a * a) / mult
        dr = dlog_a * (-LRU_C * sp)
        dsp = jnp.sum(dlog_a * (-LRU_C * r), axis=0, keepdims=True)
        dlam_ref[...] = -dsp * jax.nn.sigmoid(-lam_v)
        dpr = dr * r * (1.0 - r)
        dpi = dig * ig * (1.0 - ig)
        dbra_ref[...] = jnp.sum(dpr, axis=0, keepdims=True)
        dbri_ref[...] = jnp.sum(dpi, axis=0, keepdims=True)
        ucb = uc.astype(BF16)
        dprb, dpib = dpr.astype(BF16), dpi.astype(BF16)
        dwra_ref[...] = _dot(ucb, dprb, TN).astype(BF16)
        dwri_ref[...] = _dot(ucb, dpib, TN).astype(BF16)
        duc = duc + _dot(dprb, wra.astype(BF16), NT) + _dot(dpib, wri.astype(BF16), NT)
        dcb_ref[...] = jnp.sum(duc, axis=0, keepdims=True)
        for j in range(CONV_K):
            dcw_ref[j:j + 1, :] = jnp.sum(duc * pad[5 + j:5 + j + S, :], axis=0, keepdims=True)
        g_scr[0:S, :] = duc
        g_scr[S:S + 8, :] = zero8
        du = jnp.zeros((S, HEAD), F32)
        for j in range(CONV_K):
            du = du + cw[j:j + 1, :] * g_scr[3 - j:3 - j + S, :]
        du_ref[...] = du.astype(BF16)

    w = LRU_NB * HEAD
    bf = jax.ShapeDtypeStruct((S, w), BF16)
    vec = jax.ShapeDtypeStruct((1, w), F32)
    mat = jax.ShapeDtypeStruct((LRU_NB, HEAD, HEAD), BF16)
    return pl.pallas_call(
        body, name=name, grid=(LRU_NB,),
        in_specs=[sp_["u"], sp_["gate"], sp_["blk"], sp_["blk"], sp_["cw"], sp_["vec"], sp_["mat"], sp_["vec"],
                  sp_["mat"], sp_["vec"], sp_["vec"]] + more_specs,
        out_specs=[sp_["blk"], sp_["blk"], sp_["cw"], sp_["vec"], sp_["mat"], sp_["vec"], sp_["mat"], sp_["vec"], sp_["vec"]],
        out_shape=[bf, bf, jax.ShapeDtypeStruct((CONV_K, w), F32), vec, mat, vec, mat, vec, vec],
        scratch_shapes=[pltpu.VMEM((S + 8, HEAD), F32), pltpu.VMEM((S, HEAD), F32), pltpu.VMEM((S, HEAD), F32),
                        pltpu.VMEM((S + 8, HEAD), F32), pltpu.VMEM((S + 8, HEAD), F32)],
        compiler_params=_params("parallel"),
    )(proj, proj, h, dy, conv_w, conv_b, w_ra, b_ra, w_ri, b_ri, lam, *more)


def _xattn_fwd(cq, ckv, g_cq, g_ck, name):
    S, w = cq.shape
    M = ckv.shape[0]
    tr = _row_tile(S)
    scale = 1.0 / math.sqrt(HEAD)

    def body(cq_ref, ckv_ref, gq_ref, gk_ref, o_ref):
        for h in range(XATT_H):
            sl = slice(h * HEAD, (h + 1) * HEAD)
            qn = _rms(cq_ref[:, sl], gq_ref[...]).astype(BF16)
            kn = _rms(ckv_ref[:, sl], gk_ref[...]).astype(BF16)
            v = ckv_ref[:, w + h * HEAD:w + (h + 1) * HEAD].astype(BF16)
            s = _dot(qn, kn, NT) * scale
            p = jnp.exp(s - jnp.max(s, axis=1, keepdims=True))
            p = p / jnp.sum(p, axis=1, keepdims=True)
            o_ref[:, sl] = _dot(p.astype(BF16), v).astype(BF16)

    gain = pl.BlockSpec((1, HEAD), lambda i: (0, 0))
    return pl.pallas_call(
        body, name=name, grid=(S // tr,),
        in_specs=[pl.BlockSpec((tr, w), lambda i: (i, 0)), pl.BlockSpec((M, 2 * w), lambda i: (0, 0)), gain, gain],
        out_specs=pl.BlockSpec((tr, w), lambda i: (i, 0)), out_shape=jax.ShapeDtypeStruct((S, w), BF16),
        compiler_params=_params("parallel"),
    )(cq, ckv, g_cq, g_ck)


def _xattn_bwd(cq, ckv, g_cq, g_ck, do, name):
    S, w = cq.shape
    M = ckv.shape[0]
    tr = _row_tile(S)
    nsteps = S // tr
    scale = 1.0 / math.sqrt(HEAD)

    def body(cq_ref, ckv_ref, gq_ref, gk_ref, do_ref, dcq_ref, dckv_ref, dgq_ref, dgk_ref, dkn_scr, dv_scr):
        step = pl.program_id(0)

        @pl.when(step == 0)
        def _():
            dkn_scr[...] = jnp.zeros_like(dkn_scr)
            dv_scr[...] = jnp.zeros_like(dv_scr)
            dgq_ref[...] = jnp.zeros_like(dgq_ref)

        for h in range(XATT_H):
            sl = slice(h * HEAD, (h + 1) * HEAD)
            q_raw = cq_ref[:, sl]
            qn = _rms(q_raw, gq_ref[...]).astype(BF16)
            kn = _rms(ckv_ref[:, sl], gk_ref[...]).astype(BF16)
            v = ckv_ref[:, w + h * HEAD:w + (h + 1) * HEAD].astype(BF16)
            s = _dot(qn, kn, NT) * scale
            p = jnp.exp(s - jnp.max(s, axis=1, keepdims=True))
            p = p / jnp.sum(p, axis=1, keepdims=True)
            dob = do_ref[:, sl].astype(BF16)
            dp = _dot(dob, v, NT)
            ds = p * (dp - jnp.sum(p * dp, axis=1, keepdims=True)) * scale
            dsb = ds.astype(BF16)
            dv_scr[:, sl] += _dot(p.astype(BF16), dob, TN)
            dkn_scr[:, sl] += _dot(dsb, qn, TN)
            dq, dgq = _rms_grad(q_raw, gq_ref[...], _dot(dsb, kn))
            dcq_ref[:, sl] = dq.astype(BF16)
            dgq_ref[...] += dgq

        @pl.when(step == nsteps - 1)
        def _():
            dgk = jnp.zeros((1, HEAD), F32)
            for h in range(XATT_H):
                sl = slice(h * HEAD, (h + 1) * HEAD)
                dk, dgk_h = _rms_grad(ckv_ref[:, sl], gk_ref[...], dkn_scr[:, sl])
                dckv_ref[:, sl] = dk.astype(BF16)
                dgk = dgk + dgk_h
            dckv_ref[:, w:2 * w] = dv_scr[...].astype(BF16)
            dgk_ref[...] = dgk

    gain = pl.BlockSpec((1, HEAD), lambda i: (0, 0))
    row = pl.BlockSpec((tr, w), lambda i: (i, 0))
    mem = pl.BlockSpec((M, 2 * w), lambda i: (0, 0))
    return pl.pallas_call(
        body, name=name, grid=(nsteps,), in_specs=[row, mem, gain, gain, row], out_specs=[row, mem, gain, gain],
        out_shape=[jax.ShapeDtypeStruct((S, w), BF16), jax.ShapeDtypeStruct((M, 2 * w), BF16),
                   jax.ShapeDtypeStruct((1, HEAD), F32), jax.ShapeDtypeStruct((1, HEAD), F32)],
        scratch_shapes=[pltpu.VMEM((M, w), F32), pltpu.VMEM((M, w), F32)],
        compiler_params=_params("arbitrary"),
    )(cq, ckv, g_cq, g_ck, do)


def _ffn_up(h, w_gu, name, tm=512, after=None):
    S, D = h.shape
    nb, _, bw = w_gu.shape
    half = nb // 2
    tm = min(tm, S)
    more_specs, more = _behind(after)

    def body(h_ref, wg_ref, wu_ref, *rest):
        g_ref, u_ref, act_ref = rest[len(more):]
        hv = h_ref[...]
        g = _dot(hv, wg_ref[...])
        u = _dot(hv, wu_ref[...])
        g_ref[...] = g
        u_ref[...] = u
        act_ref[...] = (g * jax.nn.sigmoid(g) * u).astype(BF16)

    blk = pl.BlockSpec((tm, bw), lambda j, i: (i, j))
    return pl.pallas_call(
        body, name=name, grid=(half, S // tm),
        in_specs=[pl.BlockSpec((tm, D), lambda j, i: (i, 0)), pl.BlockSpec((None, D, bw), lambda j, i: (j, 0, 0)),
                  pl.BlockSpec((None, D, bw), lambda j, i: (j + half, 0, 0))] + more_specs,
        out_specs=[blk, blk, blk],
        out_shape=[jax.ShapeDtypeStruct((S, half * bw), F32)] * 2 + [jax.ShapeDtypeStruct((S, half * bw), BF16)],
        compiler_params=_params("parallel", "parallel"),
    )(h, w_gu, w_gu, *more)


def _ffn_down_dx(dy, w_down, g, u, name, tr=1024, tc=512):
    S, f = g.shape
    w2 = 2 * f
    tr, tc = min(tr, S), min(tc, f)
    assert S % tr == 0 and f % tc == 0
    n_ct = f // tc
    steps = (S // tr) * n_ct

    def body(dy_ref, w_ref, g_ref, u_ref, o_hbm, buf, sems):
        i, j = pl.program_id(0), pl.program_id(1)
        step = i * n_ct + j
        slot = step % 2

        def put(half):
            cols = pl.ds(pl.multiple_of(half * f + j * tc, 128), tc)
            return pltpu.make_async_copy(buf.at[slot, half], o_hbm.at[pl.ds(pl.multiple_of(i * tr, 8), tr), cols],
                                         sems.at[slot, half])

        @pl.when(step >= 2)
        def _():
            put(0).wait()
            put(1).wait()

        g = g_ref[...]
        sg = jax.nn.sigmoid(g)
        da = _dot(dy_ref[...], w_ref[...], NT)
        buf[slot, 0] = (da * u_ref[...] * (sg * (1.0 + g * (1.0 - sg)))).astype(BF16)
        buf[slot, 1] = (da * (g * sg)).astype(BF16)
        put(0).start()
        put(1).start()

        @pl.when(step == steps - 1)
        def _():
            put(0).wait()
            put(1).wait()
            if steps > 1:
                for half in (0, 1):
                    pltpu.make_async_copy(buf.at[1 - slot, half], o_hbm.at[pl.ds(0, tr), pl.ds(0, tc)],
                                          sems.at[1 - slot, half]).wait()

    D = dy.shape[1]
    return pl.pallas_call(
        body, name=name, grid=(S // tr, n_ct),
        in_specs=[pl.BlockSpec((tr, D), lambda i, j: (i, 0)), pl.BlockSpec((tc, D), lambda i, j: (j, 0)),
                  pl.BlockSpec((tr, tc), lambda i, j: (i, j)), pl.BlockSpec((tr, tc), lambda i, j: (i, j))],
        out_specs=ANY, out_shape=jax.ShapeDtypeStruct((S, w2), BF16),
        scratch_shapes=[pltpu.VMEM((2, 2, tr, tc), BF16), pltpu.SemaphoreType.DMA((2, 2))],
        compiler_params=_params("arbitrary", "arbitrary"),
    )(dy, w_down, g, u)


def _adamw_math(w, g, m, v):
    m = ADAM_B1 * m + (1.0 - ADAM_B1) * g
    v = ADAM_B2 * v + (1.0 - ADAM_B2) * (g * g)
    m_hat = m / (1.0 - ADAM_B1 ** ADAM_STEP)
    v_hat = v / (1.0 - ADAM_B2 ** ADAM_STEP)
    delta = -ADAM_LR * (m_hat / (jnp.sqrt(v_hat) + ADAM_EPS) + ADAM_WD * w)
    return delta, m, v


def _tile2(R, C, elems):
    if R % 16 == 0:
        return _row_tile(R, cap=max(16, elems // C // 16 * 16)), C
    tc = 128
    while C % (2 * tc) == 0 and R * 2 * tc <= elems:
        tc *= 2
    return R, tc


def _reduce_adamw(own, land, chip, w, m, v, name):
    _, R, C = own.shape
    tr, tc = _tile2(R, C, 1 << (19 if w.ndim == 3 else 18))

    def body(s_ref, o_ref, l1_ref, l2_ref, l3_ref, w_ref, m_ref, v_ref, g_ref, d_ref, nm_ref, nv_ref):
        g = o_ref[...].astype(F32) + l1_ref[...].astype(F32) + l2_ref[...].astype(F32) + l3_ref[...].astype(F32)
        g_ref[...] = g
        d_ref[...], nm_ref[...], nv_ref[...] = _adamw_math(w_ref[...], g, m_ref[...], v_ref[...])

    def part(flip):
        return pl.BlockSpec((None, tr, tc), lambda i, j, s: (s[0] ^ flip, i, j))

    if w.ndim == 3:
        blk = pl.BlockSpec((tr, None, tc), lambda i, j, s: (i, 0, j))
    else:
        blk = pl.BlockSpec((tr, tc), lambda i, j, s: (i, j))
    sds = jax.ShapeDtypeStruct(w.shape, F32)
    return pl.pallas_call(
        body, name=name,
        grid_spec=pltpu.PrefetchScalarGridSpec(
            num_scalar_prefetch=1, grid=(R // tr, C // tc),
            in_specs=[part(0), part(1), part(2), part(3), blk, blk, blk], out_specs=[blk] * 4),
        out_shape=[sds] * 4, compiler_params=_params("parallel", "parallel"),
    )(chip, own, land, land, land, w, m, v)


def _pair_sum(g8, recv, core, name):
    _, R, C = g8.shape
    tr, tc = _tile2(R, C, 1 << 21)

    def body(c_ref, a_ref, b_ref, o_ref):
        o_ref[...] = (a_ref[...].astype(F32) + b_ref[...].astype(F32)).astype(BF16)

    return pl.pallas_call(
        body, name=name,
        grid_spec=pltpu.PrefetchScalarGridSpec(
            num_scalar_prefetch=1, grid=(N_CHIP, R // tr, C // tc),
            in_specs=[pl.BlockSpec((None, tr, tc), lambda q, i, j, c: (2 * q + c[0], i, j)),
                      pl.BlockSpec((None, tr, tc), lambda q, i, j, c: (q, i, j))],
            out_specs=pl.BlockSpec((None, tr, tc), lambda q, i, j, c: (q, i, j))),
        out_shape=jax.ShapeDtypeStruct((N_CHIP, R, C), BF16),
        compiler_params=_params("parallel", "parallel", "parallel"),
    )(core, g8, recv)


def _place():
    return lax.axis_index("x"), lax.axis_index("y"), lax.axis_index("c")


def _land_with_own(shard, dev, name):
    R, C = shard.shape
    tr, tc = _tile2(R, C, 1 << 19)

    def body(d_ref, s_ref, o_ref):
        o_ref[...] = s_ref[...]

    return pl.pallas_call(
        body, name=name,
        grid_spec=pltpu.PrefetchScalarGridSpec(
            num_scalar_prefetch=1, grid=(R // tr, C // tc),
            in_specs=[pl.BlockSpec((tr, tc), lambda i, j, d: (i, j))],
            out_specs=pl.BlockSpec((None, tr, tc), lambda i, j, d: (d[0], i, j))),
        out_shape=jax.ShapeDtypeStruct((N_DEV, R, C), shard.dtype),
        compiler_params=_params("parallel", "parallel"),
    )(dev, shard)


def _all_gather(shards, lands, name):
    n = len(shards)

    def body(*refs):
        ins, outs = refs[:n], refs[2 * n:3 * n]
        send_sems, recv_sems = refs[3 * n:]
        x, y, c = _place()
        me, sibling = (x, y, c), (x, y, 1 - c)
        chips = [(1 - x, y), (x, 1 - y), (1 - x, 1 - y)]

        def copy(a, k, block, to, src=None):
            dst = outs[a].at[4 * block[0] + 2 * block[1] + block[2]]
            return pltpu.make_async_remote_copy(
                src_ref=dst if src is None else src, dst_ref=dst, send_sem=send_sems.at[a, k],
                recv_sem=recv_sems.at[a, k], device_id=to, device_id_type=MESH)

        sent = []
        for a in range(n):
            sent.append(copy(a, 0, me, sibling, src=ins[a]))
            sent += [copy(a, 1 + j, me, (*chip, c), src=ins[a]) for j, chip in enumerate(chips)]
        for cp in sent:
            cp.start()
        for j, chip in enumerate(chips):
            for a in range(n):
                copy(a, 1 + j, (*chip, c), me).wait_recv()
                fwd = copy(a, 4 + j, (*chip, c), sibling)
                fwd.start()
                sent.append(fwd)
        for a in range(n):
            copy(a, 0, sibling, me).wait_recv()
        for j, chip in enumerate(chips):
            for a in range(n):
                copy(a, 4 + j, (*chip, 1 - c), me).wait_recv()
        for cp in sent:
            cp.wait_send()

    return pl.pallas_call(
        body, name=name, in_specs=[ANY] * (2 * n), out_specs=[ANY] * n,
        out_shape=[jax.ShapeDtypeStruct(l.shape, l.dtype) for l in lands],
        input_output_aliases={n + a: a for a in range(n)},
        scratch_shapes=[pltpu.SemaphoreType.DMA((n, 7)), pltpu.SemaphoreType.DMA((n, 7))],
    )(*shards, *lands)


def _pair_plan(srcs, lands):
    x, y, c = _place()
    plan = []
    for a in range(len(srcs)):
        for q in range(N_CHIP):
            plan.append((srcs[a].at[2 * q + 1 - c], lands[a].at[q], lands[a].at[q], (x, y, 1 - c)))
    return plan


def _chip_plan(srcs, lands):
    x, y, c = _place()
    mine = 2 * x + y
    plan = []
    for a in range(len(srcs)):
        for px, py in [(1 - x, y), (x, 1 - y), (1 - x, 1 - y)]:
            peer = 2 * px + py
            plan.append((srcs[a].at[peer], lands[a].at[mine], lands[a].at[peer], (px, py, c)))
    return plan


def _gather_plan(srcs, lands):
    x, y, c = _place()
    mine = 4 * x + 2 * y + c
    plan = []
    for a in range(len(srcs)):
        for px, py, pc in [(x, y, 1 - c), (1 - x, y, c), (x, 1 - y, c), (1 - x, 1 - y, c)]:
            plan.append((srcs[a], lands[a].at[mine], lands[a].at[4 * px + 2 * py + pc], (px, py, pc)))
    return plan


def _forward_plan(srcs, lands):
    x, y, c = _place()
    plan = []
    for a in range(len(lands)):
        for px, py in [(1 - x, y), (x, 1 - y), (1 - x, 1 - y)]:
            mine, theirs = 4 * px + 2 * py + c, 4 * px + 2 * py + 1 - c
            plan.append((lands[a].at[mine], lands[a].at[mine], lands[a].at[theirs], (x, y, 1 - c)))
    return plan


def _remote(src, dst, send_sem, recv_sem, peer):
    return pltpu.make_async_remote_copy(src_ref=src, dst_ref=dst, send_sem=send_sem, recv_sem=recv_sem,
                                        device_id=peer, device_id_type=MESH)


HBM = pl.BlockSpec(memory_space=pltpu.HBM)
SEMS = pl.BlockSpec(memory_space=pltpu.SEMAPHORE)
DATAFLOW = pltpu.SideEffectType.DATAFLOW_SIDE_EFFECTING


def _split_start(srcs, lands, after, plan_fn, per_array, name):
    ns, nb = len(srcs), len(srcs) + len(lands)
    ncopy = per_array * len(lands)
    after = list(after) if isinstance(after, (list, tuple)) else [after]

    def body(*refs):
        send_sems, recv_sems = refs[nb + len(after)], refs[nb + len(after) + 1]
        token = refs[-1]
        for k, (src, dst, _, peer) in enumerate(plan_fn(refs[:ns], refs[ns:nb])):
            _remote(src, dst, send_sems.at[k], recv_sems.at[k], peer).start()
        token[...] = jnp.zeros_like(token)

    thru = [pltpu.HBM(a.shape, a.dtype) for a in (*srcs, *lands)]
    out = pl.pallas_call(
        body, name=name,
        out_shape=(pltpu.SemaphoreType.DMA((ncopy,)), pltpu.SemaphoreType.DMA((ncopy,)), *thru,
                   jax.ShapeDtypeStruct((8, 128), F32)),
        in_specs=[HBM] * nb + [ANY] * len(after),
        out_specs=(SEMS, SEMS, *[HBM] * nb, pl.BlockSpec(memory_space=pltpu.VMEM)),
        input_output_aliases={i: 2 + i for i in range(nb)},
        compiler_params=pltpu.CompilerParams(has_side_effects=DATAFLOW),
    )(*[pltpu.with_memory_space_constraint(a, pltpu.HBM) for a in (*srcs, *lands)], *after)
    return out[0], out[1], list(out[2:2 + ns]), list(out[2 + ns:2 + nb]), out[-1]


def _split_wait(send_sems, recv_sems, srcs, lands, after, plan_fn, name):
    ns, nb = len(srcs), len(srcs) + len(lands)
    after = list(after) if isinstance(after, (list, tuple)) else [after]

    def body(*refs):
        send_ref, recv_ref = refs[nb], refs[nb + 1]
        for k, (src, dst, mine, peer) in enumerate(plan_fn(refs[:ns], refs[ns:nb])):
            _remote(src, dst, send_ref.at[k], recv_ref.at[k], peer).wait_send()
            _remote(src, mine, send_ref.at[k], recv_ref.at[k], peer).wait_recv()

    thru = [pltpu.HBM(a.shape, a.dtype) for a in (*srcs, *lands)]
    out = pl.pallas_call(
        body, name=name, out_shape=tuple(thru),
        in_specs=[HBM] * nb + [SEMS, SEMS] + [ANY] * len(after), out_specs=tuple([HBM] * nb),
        input_output_aliases={i: i for i in range(nb)},
        compiler_params=pltpu.CompilerParams(has_side_effects=DATAFLOW),
    )(*srcs, *lands, send_sems, recv_sems, *after)
    return list(out[:ns]), list(out[ns:])


def _gather_forward(lands, name):
    n = len(lands)

    def body(*refs):
        landed, out = refs[:n], refs[n:2 * n]
        send_sems, recv_sems = refs[2 * n:]
        x, y, c = _place()
        chips = [(1 - x, y), (x, 1 - y), (1 - x, 1 - y)]
        sent = []
        for a in range(n):
            for j, (px, py) in enumerate(chips):
                blk = 4 * px + 2 * py + c
                sent.append(_remote(landed[a].at[blk], out[a].at[blk], send_sems.at[a, j], recv_sems.at[a, j], (x, y, 1 - c)))
        for cp in sent:
            cp.start()
        for a in range(n):
            for j, (px, py) in enumerate(chips):
                blk = 4 * px + 2 * py + 1 - c
                _remote(landed[a].at[blk], out[a].at[blk], send_sems.at[a, j], recv_sems.at[a, j], (x, y, 1 - c)).wait_recv()
        for cp in sent:
            cp.wait_send()

    return pl.pallas_call(
        body, name=name, in_specs=[ANY] * n, out_specs=[ANY] * n,
        out_shape=[jax.ShapeDtypeStruct(l.shape, l.dtype) for l in lands],
        input_output_aliases={a: a for a in range(n)},
        scratch_shapes=[pltpu.SemaphoreType.DMA((n, 3)), pltpu.SemaphoreType.DMA((n, 3))],
    )(*lands)


def _pack_rows(groups, name):
    n = len(groups[0])
    rows = -(-n // 8) * 8
    width = max(v.shape[1] for v in groups[0])

    def body(*refs):
        ins, outs = refs[:n * len(groups)], refs[n * len(groups):]
        for gi, o_ref in enumerate(outs):
            o_ref[...] = jnp.zeros_like(o_ref)
            for r in range(n):
                v_ref = ins[gi * n + r]
                o_ref[r:r + 1, 0:v_ref.shape[1]] = v_ref[...]

    return pl.pallas_call(
        body, name=name, out_shape=[jax.ShapeDtypeStruct((rows, width), F32)] * len(groups), compiler_params=_params(),
    )(*[v for g in groups for v in g])


def _unpack_rows(packs, like, name):
    n = len(like)

    def body(*refs):
        ins, outs = refs[:len(packs)], refs[len(packs):]
        for pi, p_ref in enumerate(ins):
            for r in range(n):
                o_ref = outs[pi * n + r]
                o_ref[...] = p_ref[r:r + 1, 0:o_ref.shape[1]]

    flat = pl.pallas_call(
        body, name=name, out_shape=[jax.ShapeDtypeStruct(a.shape, F32) for _ in packs for a in like],
        compiler_params=_params(),
    )(*packs)
    return [flat[pi * n:(pi + 1) * n] for pi in range(len(packs))]


def _sum_adamw(parts, w, m, v, name, col=None):
    P, R, _ = parts.shape
    C = w.shape[1]
    tr = _row_tile(R, cap=512, mult=8)

    def body(c_ref, p_ref, w_ref, m_ref, v_ref, g_ref, d_ref, nm_ref, nv_ref):
        g = p_ref[0].astype(F32)
        for q in range(1, P):
            g = g + p_ref[q].astype(F32)
        g_ref[...] = g
        d_ref[...], nm_ref[...], nv_ref[...] = _adamw_math(w_ref[...], g, m_ref[...], v_ref[...])

    blk = pl.BlockSpec((tr, C), lambda i, c: (i, 0))
    sds = jax.ShapeDtypeStruct((R, C), F32)
    at = jnp.zeros((1,), jnp.int32) if col is None else col
    return pl.pallas_call(
        body, name=name,
        grid_spec=pltpu.PrefetchScalarGridSpec(
            num_scalar_prefetch=1, grid=(R // tr,),
            in_specs=[pl.BlockSpec((P, tr, C), lambda i, c: (0, i, c[0])), blk, blk, blk], out_specs=[blk] * 4),
        out_shape=[sds] * 4, compiler_params=_params("parallel"),
    )(at, parts, w, m, v)


def _local_step(x, mem, target, sp, W, arrived, want, done, paired):
    h0 = _rms_fwd([x], [sp["g_mix"]], "rms_mix")
    proj = _mm(h0, W["main_t"], "nt", F32, "proj_main", tm=2048, tn=512)
    f_logit = _mm(h0, W["f_t"], "nt", F32, "proj_f", tm=2048)
    qn, kn, vb = _qk_fwd(proj, sp["g_q"], sp["g_k"], "qk_norm")
    b_pad = jnp.pad(sp["b_f"], ((0, 0), (0, HEAD - FOX_H)))
    c_col, c_row = _fgate_fwd(f_logit, b_pad, "forget_cumsum")
    o_fox, lse = _fox_fwd(qn, kn, vb, c_col, c_row, "fox_fwd")
    lru_p = (sp["conv_w"], sp["conv_b"], sp["w_ra"], sp["b_ra"], sp["w_ri"], sp["b_ri"], sp["lam"])
    y_lru, h_lru = _lru_fwd(proj, *lru_p, "lru_fwd", after=arrived("attn", o_fox))
    w_out, w_cq, w_ckv, w_co = want("attn", y_lru)
    mix = _rms_fwd([o_fox, y_lru], [sp["g_fox_out"], sp["g_lru_out"]], "rms_mix_out")
    x1 = _mm(mix, w_out, "nn", F32, "out_proj", add=x)
    hq = _rms_fwd([x1], [sp["g_xattn"]], "rms_xattn")
    mn = _rms_fwd([mem], [sp["g_mem"]], "rms_mem")
    cq = _mm(hq, w_cq, "nn", F32, "xattn_q", tm=2048)
    ckv = _mm(mn, w_ckv, "nn", F32, "xattn_kv")
    o_x = _xattn_fwd(cq, ckv, sp["g_cq"], sp["g_ck"], "xattn_fwd")
    co_w = w_co.shape[2]
    x2 = _mm(o_x, w_co, "nn", F32, "xattn_out", add=x1, tm=2048, tn=co_w, after=arrived("gate_up", o_x))
    hf = _rms_fwd([x2], [sp["g_ffn"]], "rms_ffn")
    (w_gu,) = want("gate_up", hf)
    gu_w = w_gu.shape[2]
    f_gate, f_up, act = _ffn_up(hf, w_gu, "ffn_up", after=arrived("down", hf))
    (w_down,) = want("down", act)
    x3 = _mm(act, w_down, "nn", F32, "ffn_down", add=x2, tk=w_down.shape[0] // 2)
    dy, dyb, sq = _loss_head(x3, target, "loss_head")
    g_down = _mm(act, dyb, "tn", BF16, "ffn_down_dw", tm=w_down.shape[0] // 4, tn=2048)
    dgu = _ffn_down_dx(dyb, w_down, f_gate, f_up, "ffn_down_dx")
    g_gu = _mm(hf, dgu, "tn", BF16, "ffn_up_dw", out_blocks=N_DEV, tn=gu_w)
    tok = done("ffn", [g_gu, g_down])
    dhf = _mm(dgu, w_gu, "nt", F32, "ffn_up_dx", tn=2048, tk=gu_w, after=tok)
    tok = paired("ffn", dhf)
    dx2, dx2b, dg_ffn = _rms_bwd(x2, sp["g_ffn"], dhf, "rms_ffn_bwd", res=dy, after=tok)
    d_ox = _mm(dx2b, w_co, "nt", F32, "xattn_out_dx", tm=2048, tk=co_w)
    g_co = _mm(o_x, dx2b, "tn", BF16, "xattn_out_dw", out_blocks=N_DEV, tn=co_w)
    dcq, dckv, dg_cq, dg_ck = _xattn_bwd(cq, ckv, sp["g_cq"], sp["g_ck"], d_ox, "xattn_bwd")
    dhq = _mm(dcq, w_cq, "nt", F32, "xattn_q_dx", tm=2048)
    g_cq = _mm(hq, dcq, "tn", BF16, "xattn_q_dw")
    dmn = _mm(dckv, w_ckv, "nt", F32, "xattn_kv_dx")
    g_ckv = _mm(mn, dckv, "tn", BF16, "xattn_kv_dw")
    (dg_mem,) = _rms_bwd(mem, sp["g_mem"], dmn, "rms_mem_bwd", want_dx=False)
    dx1, dx1b, dg_xattn = _rms_bwd(x1, sp["g_xattn"], dhq, "rms_xattn_bwd", res=dx2)
    dmix = _mm(dx1b, w_out, "nt", F32, "out_proj_dx")
    g_out = _mm(mix, dx1b, "tn", BF16, "out_proj_dw", tn=2048)
    tok = done("attn", [g_out, g_cq, g_ckv, g_co])
    do_fox, _, dg_fox_out = _rms_bwd(o_fox, sp["g_fox_out"], dmix, "rms_fox_out_bwd", dy_col=0, after=tok)
    dy_lru, _, dg_lru_out = _rms_bwd(y_lru, sp["g_lru_out"], dmix, "rms_lru_out_bwd", dy_col=1)
    tok = paired("attn", dy_lru)
    du, dgate, dconv_w, dconv_b, dw_ra, db_ra, dw_ri, db_ri, dlam = _lru_bwd(proj, h_lru, dy_lru, *lru_p, "lru_bwd", after=tok)
    dqn, dkn, dv, dc_col, dc_row = _fox_bwd(qn, kn, vb, c_col, c_row, lse, o_fox, do_fox, "fox_bwd")
    dq, dk, dg_q, dg_k = _qk_bwd(proj, sp["g_q"], sp["g_k"], dqn, dkn, "qk_norm_bwd")
    df, db_f = _fgate_bwd(f_logit, b_pad, dc_col, dc_row, "forget_cumsum_bwd")
    dproj = jnp.concatenate([dq, dk, dv, du, dgate], axis=1)
    g_main_t = _mm(dproj, h0, "tn", BF16, "proj_main_dw", tn=2048)
    g_f_t = _mm(df, h0, "tn", BF16, "proj_f_dw", tn=2048)
    tok = done("w_in", [g_main_t, g_f_t])
    dh_f = _mm(df, W["f_t"], "nn", F32, "proj_f_dx", tm=2048)
    dh0 = _mm(dproj, W["main_t"], "nn", F32, "proj_main_dx", add=dh_f, tk=dproj.shape[1] // 2, after=tok)
    tok = paired("w_in", dh0)
    grad_x, _, dg_mix = _rms_bwd(x, sp["g_mix"], dh0, "rms_mix_bwd", res=dx1, after=tok)
    small = dict(g_mix=dg_mix, b_f=db_f[:, :FOX_H], g_q=dg_q, g_k=dg_k, conv_w=dconv_w, conv_b=dconv_b, w_ra=dw_ra,
                 b_ra=db_ra, w_ri=dw_ri, b_ri=db_ri, lam=dlam, g_fox_out=dg_fox_out, g_lru_out=dg_lru_out,
                 g_xattn=dg_xattn, g_mem=dg_mem, g_cq=dg_cq, g_ck=dg_ck, g_ffn=dg_ffn)
    return sq, grad_x, small


BIG = ("w_in", "w_out", "w_cq", "w_ckv", "w_co", "w_gate_up", "w_down")
SMALL = ("g_mix", "b_f", "g_q", "g_k", "conv_b", "w_ra", "b_ra", "w_ri", "b_ri", "lam", "g_fox_out", "g_lru_out",
         "g_xattn", "g_mem", "g_cq", "g_ck", "g_ffn")
ORDER = ("g_mix", "w_in", "b_f", "g_q", "g_k", "conv_w", "conv_b", "w_ra", "b_ra", "w_ri", "b_ri", "lam", "g_fox_out",
         "g_lru_out", "w_out", "g_xattn", "g_mem", "w_cq", "w_ckv", "g_cq", "g_ck", "w_co", "g_ffn", "w_gate_up", "w_down")


def kernel(x, mem, g_mix, w_in, b_f, g_q, g_k, conv_w, conv_b, w_ra, b_ra, w_ri, b_ri, lam, g_fox_out, g_lru_out, w_out, g_xattn, g_mem, w_cq, w_ckv, g_cq, g_ck, w_co, g_ffn, w_gate_up, w_down, loss_target, m_g_mix, m_w_in, m_b_f, m_g_q, m_g_k, m_conv_w, m_conv_b, m_w_ra, m_b_ra, m_w_ri, m_b_ri, m_lam, m_g_fox_out, m_g_lru_out, m_w_out, m_g_xattn, m_g_mem, m_w_cq, m_w_ckv, m_g_cq, m_g_ck, m_w_co, m_g_ffn, m_w_gate_up, m_w_down, v_g_mix, v_w_in, v_b_f, v_g_q, v_g_k, v_conv_w, v_conv_b, v_w_ra, v_b_ra, v_w_ri, v_b_ri, v_lam, v_g_fox_out, v_g_lru_out, v_w_out, v_g_xattn, v_g_mem, v_w_cq, v_w_ckv, v_g_cq, v_g_ck, v_w_co, v_g_ffn, v_w_gate_up, v_w_down):
    given = dict(locals())
    w = {n: given[n] for n in ORDER}
    m = {n: given["m_" + n] for n in ORDER}
    v = {n: given["v_" + n] for n in ORDER}
    D = x.shape[2]
    fw = FOX_H * HEAD
    dev_index = 4 * lax.axis_index("x") + 2 * lax.axis_index("y") + lax.axis_index("c")
    dev = jnp.reshape(dev_index, (1,)).astype(jnp.int32)
    core = jnp.reshape(lax.axis_index("c"), (1,)).astype(jnp.int32)
    chip = jnp.reshape(2 * lax.axis_index("x") + lax.axis_index("y"), (1,)).astype(jnp.int32)

    def shard(d, n):
        return jnp.transpose(d[n], (2, 0, 1)) if n == "w_in" else d[n][0]

    def unshard(a, n):
        return jnp.transpose(a, (1, 2, 0)) if n == "w_in" else a[None]

    gather_groups = dict(attn=("w_out", "w_cq", "w_ckv", "w_co"), gate_up=("w_gate_up",), down=("w_down",))
    reduce_groups = dict(ffn=("w_gate_up", "w_down"), attn=("w_out", "w_cq", "w_ckv", "w_co"), w_in=("w_in",))
    column_blocked = ("w_co", "w_gate_up")
    flying = {}

    def land(a, tag):
        return _land_with_own(a, dev, "own_" + tag)

    def launch(group, after):
        shards = [shard(w, n).astype(BF16) for n in gather_groups[group]]
        lands = [land(s, n) for s, n in zip(shards, gather_groups[group])]
        flying[group] = _split_start(shards, lands, after, _gather_plan, 4, "gather_" + group + "_start")
        return flying[group][4]

    def arrived(group, after):
        send, recv, shards, lands, _ = flying.pop(group)
        _, lands = _split_wait(send, recv, shards, lands, after, _gather_plan, "gather_" + group + "_wait")
        flying[group] = _split_start([], lands, after, _forward_plan, 3, "gather_" + group + "_forward_start")
        if group == "attn":
            return launch("down", flying[group][4])
        return flying[group][4]

    def want(group, after):
        send, recv, _, lands, _ = flying.pop(group)
        _, full = _split_wait(send, recv, [], lands, after, _forward_plan, "gather_" + group + "_forward_wait")
        return [g if n in column_blocked else g.reshape(-1, g.shape[2]) for n, g in zip(gather_groups[group], full)]

    first = [shard(w, "w_in").astype(BF16).reshape(-1, D), conv_w[0]]
    g_in, g_conv = _all_gather(first, [land(first[0], "w_in"), land(first[1], "conv_w")], "gather_w_in")
    tok = launch("gate_up", launch("attn", g_conv))
    per = g_in.shape[1]
    f_k, f_lo = divmod(3 * fw, per)
    assert f_lo + FOX_H <= per
    wt_in = g_in.reshape(-1, D)
    W = dict(main_t=jnp.concatenate([wt_in[:3 * fw], wt_in[3 * fw + FOX_H:]], axis=0),
             f_t=jnp.pad(wt_in[3 * fw:3 * fw + FOX_H], ((0, HEAD - FOX_H), (0, 0))))
    sp = {n: w[n] for n in SMALL if n not in ("w_ra", "w_ri")}
    sp["w_ra"], sp["w_ri"] = w_ra[0], w_ri[0]
    sp["conv_w"] = jnp.transpose(g_conv, (1, 0, 2)).reshape(CONV_K, -1)
    sp["g_mix"] = sp["g_mix"] + tok[0, 0]

    pairing, reducing = {}, {}

    def done(group, grads):
        if group == "w_in":
            g_main_t, g_f_t = grads
            shards = [g_main_t[k * per:(k + 1) * per] for k in range(f_k)]
            shards.append(jnp.concatenate([g_main_t[f_k * per:3 * fw], g_f_t[:FOX_H],
                                           g_main_t[3 * fw:(f_k + 1) * per - FOX_H]], axis=0))
            shards += [g_main_t[k * per - FOX_H:(k + 1) * per - FOX_H] for k in range(f_k + 1, N_DEV)]
            grads = [jnp.stack(shards)]
        g8 = [g if g.ndim == 3 else g.reshape(N_DEV, -1, g.shape[1]) for g in grads]
        lands = [lax.empty((N_CHIP,) + g.shape[1:], g.dtype) for g in g8]
        pairing[group] = _split_start(g8, lands, chip, _pair_plan, N_CHIP, "reduce_" + group + "_pair_start")
        return pairing[group][4]

    def paired(group, after):
        send, recv, g8, lands, _ = pairing.pop(group)
        g8, from_sibling = _split_wait(send, recv, g8, lands, after, _pair_plan, "reduce_" + group + "_pair_wait")
        p4 = [_pair_sum(g, r, core, "reduce_pair_sum_" + n) for g, r, n in zip(g8, from_sibling, reduce_groups[group])]
        lands = [lax.empty(p.shape, p.dtype) for p in p4]
        reducing[group] = _split_start(p4, lands, chip, _chip_plan, 3, "reduce_" + group + "_start")
        return reducing[group][4]

    def finish(group, after):
        send, recv, p4, lands, _ = reducing.pop(group)
        p4, lands = _split_wait(send, recv, p4, lands, after, _chip_plan, "reduce_" + group + "_wait")
        return {n: tuple(unshard(r, n) for r in _reduce_adamw(p, l, chip, shard(w, n), shard(m, n), shard(v, n), "adamw_" + n))
                for n, p, l in zip(reduce_groups[group], p4, lands)}

    sq, grad_x, gs = _local_step(x[0], mem[0], loss_target[0], sp, W, arrived, want, done, paired)

    vectors = tuple(n for n in SMALL if n not in ("w_ra", "w_ri"))
    mine = [_pack_rows([[gs[n] for n in vectors] + [sq[0:1]]], "pack_small_grads")[0], gs["w_ra"].reshape(-1, HEAD),
            gs["w_ri"].reshape(-1, HEAD), gs["conv_w"]]
    lands = [land(a, "small_grads_%d" % k) for k, a in enumerate(mine)]
    s_send, s_recv, s_src, s_land, tok = _split_start(mine, lands, grad_x, _gather_plan, 4, "gather_small_start")
    out = finish("ffn", tok)
    out.update(finish("attn", tok))
    updated = [r for n in reduce_groups["ffn"] + reduce_groups["attn"] for r in out[n]]
    _, s_land = _split_wait(s_send, s_recv, s_src, s_land, updated, _gather_plan, "gather_small_wait")
    all_vec, all_ra, all_ri, all_conv = _gather_forward(s_land, "gather_small_forward")

    nothing = jnp.zeros((1, HEAD), F32)
    state = _pack_rows([[d[n] for n in vectors] + [nothing] for d in (w, m, v)], "pack_small_state")
    upd = _sum_adamw(all_vec, *state, "adamw_small_vectors")
    per_vector = _unpack_rows(upd, [w[n] for n in vectors] + [nothing], "unpack_small")
    for r, n in enumerate(vectors):
        out[n] = tuple(per_vector[k][r] for k in range(4))
    loss = 0.5 * per_vector[0][len(vectors)][0, 0] / D
    for n, parts in (("w_ra", all_ra), ("w_ri", all_ri)):
        res = _sum_adamw(parts, *[d[n].reshape(-1, HEAD) for d in (w, m, v)], "adamw_" + n)
        out[n] = tuple(r.reshape(w[n].shape) for r in res)
    out["conv_w"] = tuple(r[None] for r in _sum_adamw(all_conv, conv_w[0], m_conv_w[0], v_conv_w[0], "adamw_conv_w", col=dev))
    out.update(finish("w_in", upd[1]))

    return (loss, grad_x[None], *[out[n][0] for n in ORDER], *[out[n][1] for n in ORDER],
            *[out[n][2] for n in ORDER], *[out[n][3] for n in ORDER])
```

```python
import functools
import math

import jax
import jax.numpy as jnp
from jax import lax
from jax.experimental import pallas as pl
from jax.experimental.pallas import tpu as pltpu

F32 = jnp.float32
BF16 = jnp.bfloat16
MESH = pl.DeviceIdType.MESH
N_DEV = 8
N_CHIP = 4

HEAD = 128
FOX_H = 8
XATT_H = 4
LRU_NB = 8
CONV_K = 4
LRU_C = 8.0
RMS_EPS = 1e-6
ATT_T = 1024
CUM_T = 256
ROW_T = 256
NEG = -1e30
V7X_VMEM_LIMIT = 48 * 1024 * 1024

ADAM_LR = 0.001
ADAM_B1 = 0.9
ADAM_B2 = 0.999
ADAM_EPS = 1e-08
ADAM_WD = 0.01
ADAM_STEP = 10

NT = (((1,), (1,)), ((), ()))
TN = (((0,), (0,)), ((), ()))
NN = (((1,), (0,)), ((), ()))

ANY = pl.BlockSpec(memory_space=pl.ANY)


def _behind(after):
    return ([], []) if after is None else ([ANY], [after])


def _params(*sem):
    return pltpu.CompilerParams(dimension_semantics=sem or None, vmem_limit_bytes=V7X_VMEM_LIMIT)


def _dot(a, b, dn=NN):
    return lax.dot_general(a, b, dn, preferred_element_type=F32)


def _rms(x, g):
    r = lax.rsqrt(jnp.mean(x * x, axis=-1, keepdims=True) + RMS_EPS)
    return x * r * g


def _rms_grad(x, g, dy):
    r = lax.rsqrt(jnp.mean(x * x, axis=-1, keepdims=True) + RMS_EPS)
    xh = x * r
    dxh = dy * g
    dx = r * (dxh - xh * jnp.mean(dxh * xh, axis=-1, keepdims=True))
    return dx, jnp.sum(dy * xh, axis=0, keepdims=True)


def _gelu(x):
    k = math.sqrt(2.0 / math.pi)
    return 0.5 * x * (1.0 + jnp.tanh(k * (x + 0.044715 * (x * x * x))))


def _gelu_grad(x):
    k = math.sqrt(2.0 / math.pi)
    t = jnp.tanh(k * (x + 0.044715 * (x * x * x)))
    return 0.5 * (1.0 + t) + 0.5 * x * (1.0 - t * t) * k * (1.0 + 3.0 * 0.044715 * x * x)


def _one_minus_exp(z):
    series = -(z + 0.5 * z * z + (1.0 / 6.0) * z * z * z)
    return jnp.where(z > -1e-3, series, 1.0 - jnp.exp(z))


def _row_tile(rows, cap=ROW_T, mult=16):
    t = min(rows, cap)
    while rows % t or (t % mult and t != rows):
        t -= 1
    return t


def _mat_dims(z):
    return (z.shape[-2], z.shape[-1] * (z.shape[0] if z.ndim == 3 else 1))


def _mat_spec(arr, rblk, cblk, rsel, csel):
    if arr.ndim == 2:
        return pl.BlockSpec((rblk, cblk), lambda i, j, k: ((i, j, k)[rsel], (i, j, k)[csel]))
    nw = arr.shape[2]
    assert nw % cblk == 0, (arr.shape, cblk)
    per = nw // cblk
    return pl.BlockSpec((None, rblk, cblk),
                        lambda i, j, k: ((i, j, k)[csel] // per, (i, j, k)[rsel], (i, j, k)[csel] % per))


def _mm(a, b, mode, out_dtype, name, add=None, out_blocks=None, tm=1024, tn=1024, tk=None, after=None):
    ar, ac = _mat_dims(a)
    br, bc = _mat_dims(b)
    if mode == "nn":
        M, K, N = ar, ac, bc
        assert br == K
    elif mode == "nt":
        M, K, N = ar, ac, br
        assert bc == K
    else:
        M, K, N = ac, ar, bc
        assert br == K
    tm, tn = min(tm, M), min(tn, N)
    tk = K if tk is None or mode == "tn" else min(tk, K)
    assert M % tm == 0 and N % tn == 0 and K % tk == 0, (name, M, N, K, tm, tn, tk)
    nk = K // tk
    nj = N // tn
    if mode == "nn":
        specs = [_mat_spec(a, tm, tk, 0, 2), _mat_spec(b, tk, tn, 2, 1)]
        dn = NN
    elif mode == "nt":
        specs = [_mat_spec(a, tm, tk, 0, 2), _mat_spec(b, tn, tk, 1, 2)]
        dn = NT
    else:
        specs = [_mat_spec(a, tk, tm, 2, 0), _mat_spec(b, tk, tn, 2, 1)]
        dn = TN
    args = [a, b]
    if add is not None:
        specs.append(pl.BlockSpec((tm, tn), lambda i, j, k: (i, j)))
        args.append(add)
    specs += _behind(after)[0]
    args += _behind(after)[1]
    n_in = len(args)
    if out_blocks is None:
        out_shape = jax.ShapeDtypeStruct((M, N), out_dtype)
        out_spec = pl.BlockSpec((tm, tn), lambda i, j, k: (i, j))
    else:
        nb = out_blocks
        nw = N // nb
        assert nw % tn == 0
        per = nw // tn
        out_shape = jax.ShapeDtypeStruct((nb, M, nw), out_dtype)
        out_spec = pl.BlockSpec((None, tm, tn), lambda i, j, k: (j // per, i, j % per))
    keep_t = mode == "tn" and nj > 1
    scratch = []
    if nk > 1:
        scratch.append(pltpu.VMEM((tm, tn), F32))
    if keep_t:
        scratch.append(pltpu.VMEM((tm, tk), a.dtype))

    def body(*refs):
        a_ref, b_ref = refs[0], refs[1]
        add_ref = refs[2] if add is not None else None
        o_ref = refs[n_in]

        def finish(r):
            if add_ref is not None:
                r = r + add_ref[...]
            o_ref[...] = r.astype(out_dtype)

        if keep_t:
            at_ref = refs[-1]

            @pl.when(pl.program_id(1) == 0)
            def _():
                at_ref[...] = a_ref[...].T

            finish(_dot(at_ref[...], b_ref[...], NN))
        elif nk == 1:
            finish(_dot(a_ref[...], b_ref[...], dn))
        else:
            acc_ref = refs[n_in + 1]
            k = pl.program_id(2)

            @pl.when(k == 0)
            def _():
                acc_ref[...] = jnp.zeros_like(acc_ref)

            acc_ref[...] += _dot(a_ref[...], b_ref[...], dn)

            @pl.when(k == nk - 1)
            def _():
                finish(acc_ref[...])

    return pl.pallas_call(
        body, name=name, grid=(M // tm, nj, nk), in_specs=specs, out_specs=out_spec, out_shape=out_shape,
        scratch_shapes=scratch,
        compiler_params=_params("parallel", "arbitrary" if keep_t else "parallel", "arbitrary"),
    )(*args)


def _rms_fwd(xs, gs, name, after=None):
    n = len(xs)
    rows = xs[0].shape[0]
    widths = [x.shape[1] for x in xs]
    tr = _row_tile(rows)
    more_specs, more = _behind(after)

    def body(*refs):
        o_ref = refs[2 * n + len(more)]
        off = 0
        for t in range(n):
            o_ref[:, off:off + widths[t]] = _rms(refs[t][...], refs[n + t][...]).astype(BF16)
            off += widths[t]

    return pl.pallas_call(
        body, name=name, grid=(rows // tr,),
        in_specs=[pl.BlockSpec((tr, w), lambda i: (i, 0)) for w in widths]
        + [pl.BlockSpec((1, w), lambda i: (0, 0)) for w in widths] + more_specs,
        out_specs=pl.BlockSpec((tr, sum(widths)), lambda i: (i, 0)),
        out_shape=jax.ShapeDtypeStruct((rows, sum(widths)), BF16),
        compiler_params=_params("parallel"),
    )(*xs, *gs, *more)


def _rms_bwd(x, g, dy, name, dy_col=0, res=None, want_dx=True, after=None):
    rows, w = x.shape
    tr = _row_tile(rows)
    has_res = res is not None
    more_specs, more = _behind(after)

    def body(*refs):
        x_ref, g_ref, dy_ref = refs[:3]
        res_ref = refs[3] if has_res else None
        outs = refs[3 + has_res + len(more):]
        dg_ref = outs[-1]
        dx, dg = _rms_grad(x_ref[...], g_ref[...], dy_ref[...])

        @pl.when(pl.program_id(0) == 0)
        def _():
            dg_ref[...] = jnp.zeros_like(dg_ref)

        dg_ref[...] += dg
        if want_dx:
            if has_res:
                dx = dx + res_ref[...]
            outs[0][...] = dx
            outs[1][...] = dx.astype(BF16)

    row_spec = pl.BlockSpec((tr, w), lambda i: (i, 0))
    in_specs = [row_spec, pl.BlockSpec((1, w), lambda i: (0, 0)), pl.BlockSpec((tr, w), lambda i: (i, dy_col))]
    args = [x, g, dy]
    if has_res:
        in_specs.append(row_spec)
        args.append(res)
    in_specs += more_specs
    args += more
    out_specs, out_shape = [], []
    if want_dx:
        out_specs += [row_spec, row_spec]
        out_shape += [jax.ShapeDtypeStruct((rows, w), F32), jax.ShapeDtypeStruct((rows, w), BF16)]
    out_specs.append(pl.BlockSpec((1, w), lambda i: (0, 0)))
    out_shape.append(jax.ShapeDtypeStruct((1, w), F32))
    return pl.pallas_call(
        body, name=name, grid=(rows // tr,), in_specs=in_specs, out_specs=out_specs, out_shape=out_shape,
        compiler_params=_params("arbitrary"),
    )(*args)


def _loss_head(y, target, name):
    rows, w = y.shape
    tr = _row_tile(rows)

    def body(y_ref, t_ref, dy_ref, dyb_ref, acc_ref):
        e = y_ref[...] - t_ref[...]

        @pl.when(pl.program_id(0) == 0)
        def _():
            acc_ref[...] = jnp.zeros_like(acc_ref)

        acc_ref[...] += jnp.sum(e * e)
        dy = e * (1.0 / w)
        dy_ref[...] = dy
        dyb_ref[...] = dy.astype(BF16)

    row_spec = pl.BlockSpec((tr, w), lambda i: (i, 0))
    return pl.pallas_call(
        body, name=name, grid=(rows // tr,), in_specs=[row_spec, row_spec],
        out_specs=[row_spec, row_spec, pl.BlockSpec((8, 128), lambda i: (0, 0))],
        out_shape=[jax.ShapeDtypeStruct((rows, w), F32), jax.ShapeDtypeStruct((rows, w), BF16),
                   jax.ShapeDtypeStruct((8, 128), F32)],
        compiler_params=_params("arbitrary"),
    )(y, target)


def _qk_fwd(proj, g_q, g_k, name, after=None):
    rows = proj.shape[0]
    w = FOX_H * HEAD
    tr = _row_tile(rows)
    more_specs, more = _behind(after)

    def body(q_ref, k_ref, v_ref, gq_ref, gk_ref, *rest):
        qn_ref, kn_ref, vb_ref = rest[len(more):]
        for h in range(FOX_H):
            sl = slice(h * HEAD, (h + 1) * HEAD)
            qn_ref[:, sl] = _rms(q_ref[:, sl], gq_ref[...]).astype(BF16)
            kn_ref[:, sl] = _rms(k_ref[:, sl], gk_ref[...]).astype(BF16)
        vb_ref[...] = v_ref[...].astype(BF16)

    gain = pl.BlockSpec((1, HEAD), lambda i: (0, 0))
    out = pl.BlockSpec((tr, w), lambda i: (i, 0))
    return pl.pallas_call(
        body, name=name, grid=(rows // tr,),
        in_specs=[pl.BlockSpec((tr, w), lambda i: (i, 0)), pl.BlockSpec((tr, w), lambda i: (i, 1)),
                  pl.BlockSpec((tr, w), lambda i: (i, 2)), gain, gain] + more_specs,
        out_specs=[out, out, out], out_shape=[jax.ShapeDtypeStruct((rows, w), BF16)] * 3,
        compiler_params=_params("parallel"),
    )(proj, proj, proj, g_q, g_k, *more)


def _qk_bwd(proj, g_q, g_k, dqn, dkn, name):
    rows = proj.shape[0]
    w = FOX_H * HEAD
    tr = _row_tile(rows)

    def body(q_ref, k_ref, gq_ref, gk_ref, dqn_ref, dkn_ref, dq_ref, dk_ref, dgq_ref, dgk_ref):
        @pl.when(pl.program_id(0) == 0)
        def _():
            dgq_ref[...] = jnp.zeros_like(dgq_ref)
            dgk_ref[...] = jnp.zeros_like(dgk_ref)

        for h in range(FOX_H):
            sl = slice(h * HEAD, (h + 1) * HEAD)
            dq, dgq = _rms_grad(q_ref[:, sl], gq_ref[...], dqn_ref[:, sl])
            dk, dgk = _rms_grad(k_ref[:, sl], gk_ref[...], dkn_ref[:, sl])
            dq_ref[:, sl] = dq.astype(BF16)
            dk_ref[:, sl] = dk.astype(BF16)
            dgq_ref[...] += dgq
            dgk_ref[...] += dgk

    gain = pl.BlockSpec((1, HEAD), lambda i: (0, 0))
    row = pl.BlockSpec((tr, w), lambda i: (i, 0))
    return pl.pallas_call(
        body, name=name, grid=(rows // tr,),
        in_specs=[row, pl.BlockSpec((tr, w), lambda i: (i, 1)), gain, gain, row, row],
        out_specs=[row, row, gain, gain],
        out_shape=[jax.ShapeDtypeStruct((rows, w), BF16)] * 2 + [jax.ShapeDtypeStruct((1, HEAD), F32)] * 2,
        compiler_params=_params("arbitrary"),
    )(proj, proj, g_q, g_k, dqn, dkn)


def _fgate_fwd(f_logit, b_pad, name):
    S = f_logit.shape[0]
    T = min(CUM_T, S)
    nb = S // T
    RT = min(ATT_T, S)
    nbr = S // RT

    def body(f_ref, b_ref, col_ref, row_ref, c_scr):
        tri = (lax.broadcasted_iota(jnp.int32, (T, T), 0) >= lax.broadcasted_iota(jnp.int32, (T, T), 1)).astype(F32)
        carry = jnp.zeros((1, HEAD), F32)
        for blk in range(nb):
            z = f_ref[blk * T:(blk + 1) * T, :] + b_ref[...]
            lf = jnp.minimum(z, 0.0) - jnp.log1p(jnp.exp(-jnp.abs(z)))
            cb = jnp.dot(tri, lf, precision=lax.Precision.HIGHEST, preferred_element_type=F32) + carry
            c_scr[blk * T:(blk + 1) * T, :] = cb
            carry = cb[T - 1:T, :]
        c = c_scr[...]
        lane = lax.broadcasted_iota(jnp.int32, c.shape, 1)
        ct = c.T
        for h in range(FOX_H):
            col_ref[h] = jnp.sum(jnp.where(lane == h, c, 0.0), axis=1, keepdims=True)
            for jb in range(nbr):
                row_ref[h, jb] = ct[h:h + 1, jb * RT:(jb + 1) * RT]

    return pl.pallas_call(
        body, name=name,
        out_shape=[jax.ShapeDtypeStruct((FOX_H, S, 1), F32), jax.ShapeDtypeStruct((FOX_H, nbr, 1, RT), F32)],
        scratch_shapes=[pltpu.VMEM((S, HEAD), F32)], compiler_params=_params(),
    )(f_logit, b_pad)


def _fgate_bwd(f_logit, b_pad, dc_col, dc_row, name):
    S = f_logit.shape[0]
    T = min(CUM_T, S)
    nb = S // T
    RT = min(ATT_T, S)

    def body(f_ref, b_ref, dcol_ref, drow_ref, df_ref, db_ref, dc_scr, rt_scr):
        lane = lax.broadcasted_iota(jnp.int32, (S, HEAD), 1)
        sub = lax.broadcasted_iota(jnp.int32, (HEAD, RT), 0)
        dc = jnp.zeros((S, HEAD), F32)
        for h in range(FOX_H):
            dc = jnp.where(lane == h, dcol_ref[h], dc)
        for jb in range(S // RT):
            rt = jnp.zeros((HEAD, RT), F32)
            for h in range(FOX_H):
                rt = jnp.where(sub == h, drow_ref[h, jb], rt)
            rt_scr[jb * RT:(jb + 1) * RT, :] = rt.T
        dc_scr[...] = dc - rt_scr[...]
        tri = (lax.broadcasted_iota(jnp.int32, (T, T), 0) <= lax.broadcasted_iota(jnp.int32, (T, T), 1)).astype(F32)
        carry = jnp.zeros((1, HEAD), F32)
        db = jnp.zeros((1, HEAD), F32)
        for blk in reversed(range(nb)):
            rows = slice(blk * T, (blk + 1) * T)
            dlf = jnp.dot(tri, dc_scr[rows, :], precision=lax.Precision.HIGHEST, preferred_element_type=F32) + carry
            carry = dlf[0:1, :]
            z = f_ref[rows, :] + b_ref[...]
            df = dlf * jax.nn.sigmoid(-z)
            df_ref[rows, :] = df.astype(BF16)
            db = db + jnp.sum(df, axis=0, keepdims=True)
        db_ref[...] = db

    return pl.pallas_call(
        body, name=name,
        out_shape=[jax.ShapeDtypeStruct((S, HEAD), BF16), jax.ShapeDtypeStruct((1, HEAD), F32)],
        scratch_shapes=[pltpu.VMEM((S, HEAD), F32), pltpu.VMEM((S, HEAD), F32)], compiler_params=_params(),
    )(f_logit, b_pad, dc_col, dc_row)


def _fox_fwd(qn, kn, vb, c_col, c_row, name, after=None):
    S = qn.shape[0]
    T = min(ATT_T, S)
    Th = T // 2
    nb = S // T
    scale = 1.0 / math.sqrt(HEAD)
    more_specs, more = _behind(after)

    def body(q_ref, k_ref, v_ref, cc_ref, cr_ref, *rest):
        o_ref, lse_ref = rest[len(more):]
        i = pl.program_id(1)
        q = q_ref[...]
        cc = cc_ref[...]

        def update(carry, s, v):
            m, l, acc = carry
            m2 = jnp.maximum(m, jnp.max(s, axis=1, keepdims=True))
            p = jnp.exp(s - m2)
            al = jnp.exp(m - m2)
            return m2, al * l + jnp.sum(p, axis=1, keepdims=True), al * acc + _dot(p.astype(BF16), v)

        def step(j, carry):
            sl = pl.ds(pl.multiple_of(j * T, T), T)
            return update(carry, _dot(q, k_ref[sl, :], NT) * scale + cc - cr_ref[j], v_ref[sl, :])

        def corner(r0, c0, carry, masked):
            sl = pl.ds(pl.multiple_of(i * T + c0, Th), Th)
            s = _dot(q[r0:r0 + Th], k_ref[sl, :], NT) * scale + cc[r0:r0 + Th] - cr_ref[i][:, c0:c0 + Th]
            if masked:
                s = jnp.where(lax.broadcasted_iota(jnp.int32, (Th, Th), 0) >= lax.broadcasted_iota(jnp.int32, (Th, Th), 1), s, NEG)
            return update(carry, s, v_ref[sl, :])

        init = (jnp.full((T, 1), NEG, F32), jnp.zeros((T, 1), F32), jnp.zeros((T, HEAD), F32))
        before = lax.fori_loop(0, i, step, init)
        upper = corner(0, 0, tuple(a[:Th] for a in before), True)
        lower = corner(Th, Th, corner(Th, 0, tuple(a[Th:] for a in before), False), True)
        for r0, (m, l, acc) in ((0, upper), (Th, lower)):
            o_ref[r0:r0 + Th, :] = acc / l
            lse_ref[r0:r0 + Th, :] = m + jnp.log(l)

    head = pl.BlockSpec((S, HEAD), lambda h, i: (0, h))
    return pl.pallas_call(
        body, name=name, grid=(FOX_H, nb),
        in_specs=[pl.BlockSpec((T, HEAD), lambda h, i: (i, h)), head, head,
                  pl.BlockSpec((None, T, 1), lambda h, i: (h, i, 0)),
                  pl.BlockSpec((None, nb, 1, T), lambda h, i: (h, 0, 0, 0))] + more_specs,
        out_specs=[pl.BlockSpec((T, HEAD), lambda h, i: (i, h)), pl.BlockSpec((None, T, 1), lambda h, i: (h, i, 0))],
        out_shape=[jax.ShapeDtypeStruct((S, FOX_H * HEAD), F32), jax.ShapeDtypeStruct((FOX_H, S, 1), F32)],
        compiler_params=_params("parallel", "parallel"),
    )(qn, kn, vb, c_col, c_row, *more)


def _fox_bwd(qn, kn, vb, c_col, c_row, lse, o, do, name):
    S = qn.shape[0]
    T = min(ATT_T, S)
    Th = T // 2
    nb = S // T
    scale = 1.0 / math.sqrt(HEAD)

    def body(q_ref, k_ref, v_ref, cc_ref, cr_ref, lse_ref, o_ref, do_ref, dq_ref, dk_ref, dv_ref, dcc_ref, dcr_ref):
        j = pl.program_id(1)

        @pl.when(j == 0)
        def _():
            dq_ref[...] = jnp.zeros_like(dq_ref)
            dcc_ref[...] = jnp.zeros_like(dcc_ref)

        k = k_ref[...]
        v = v_ref[...]
        cr = cr_ref[...]

        def tile(sl, kk, vv, crr, carry, masked):
            dk, dv, dcr = carry
            q = q_ref[sl, :]
            d_o = do_ref[sl, :]
            s = _dot(q, kk, NT) * scale + cc_ref[sl, :] - crr
            if masked:
                s = jnp.where(lax.broadcasted_iota(jnp.int32, s.shape, 0) >= lax.broadcasted_iota(jnp.int32, s.shape, 1), s, NEG)
            p = jnp.exp(s - lse_ref[sl, :])
            dob = d_o.astype(BF16)
            dp = _dot(dob, vv, NT)
            delta = jnp.sum(d_o * o_ref[sl, :], axis=1, keepdims=True)
            ds = p * (dp - delta)
            dsb = ds.astype(BF16)
            dq_ref[sl, :] += _dot(dsb, kk) * scale
            dcc_ref[sl, :] += jnp.sum(ds, axis=1, keepdims=True)
            return (dk + _dot(dsb, q, TN), dv + _dot(p.astype(BF16), dob, TN), dcr + jnp.sum(ds, axis=0, keepdims=True))

        def step(i, carry):
            return tile(pl.ds(pl.multiple_of(i * T, T), T), k, v, cr, carry, False)

        def rows(r0):
            return pl.ds(pl.multiple_of(j * T + r0, Th), Th)

        zero = (jnp.zeros((Th, HEAD), F32), jnp.zeros((Th, HEAD), F32), jnp.zeros((1, Th), F32))
        left = tile(rows(Th), k[:Th], v[:Th], cr[:, :Th], tile(rows(0), k[:Th], v[:Th], cr[:, :Th], zero, True), False)
        right = tile(rows(Th), k[Th:], v[Th:], cr[:, Th:], zero, True)
        diagonal = (jnp.concatenate([left[0], right[0]], axis=0), jnp.concatenate([left[1], right[1]], axis=0),
                    jnp.concatenate([left[2], right[2]], axis=1))
        dk, dv, dcr = lax.fori_loop(j + 1, nb, step, diagonal)
        dk_ref[...] = dk * scale
        dv_ref[...] = dv.astype(BF16)
        dcr_ref[...] = dcr

    head = pl.BlockSpec((S, HEAD), lambda h, j: (0, h))
    tile = pl.BlockSpec((T, HEAD), lambda h, j: (j, h))
    col = pl.BlockSpec((None, S, 1), lambda h, j: (h, 0, 0))
    row = pl.BlockSpec((None, None, 1, T), lambda h, j: (h, j, 0, 0))
    w = FOX_H * HEAD
    return pl.pallas_call(
        body, name=name, grid=(FOX_H, nb),
        in_specs=[head, tile, tile, col, row, col, head, head],
        out_specs=[head, tile, tile, col, row],
        out_shape=[jax.ShapeDtypeStruct((S, w), F32), jax.ShapeDtypeStruct((S, w), F32), jax.ShapeDtypeStruct((S, w), BF16),
                   jax.ShapeDtypeStruct((FOX_H, S, 1), F32), jax.ShapeDtypeStruct((FOX_H, nb, 1, T), F32)],
        compiler_params=_params("arbitrary", "arbitrary"),
    )(qn, kn, vb, c_col, c_row, lse, o, do)


def _lru_gates(uc, wra, bra, wri, bri, lam):
    ucb = uc.astype(BF16)
    r = jax.nn.sigmoid(_dot(ucb, wra.astype(BF16)) + bra)
    ig = jax.nn.sigmoid(_dot(ucb, wri.astype(BF16)) + bri)
    sp = jnp.maximum(-lam, 0.0) + jnp.log1p(jnp.exp(-jnp.abs(lam)))
    log_a = -LRU_C * r * sp
    a = jnp.exp(log_a)
    mult = jnp.sqrt(_one_minus_exp(2.0 * log_a))
    return r, ig, sp, a, mult


def _conv(pad_ref, cw, cb, S):
    uc = cb
    for j in range(CONV_K):
        uc = uc + cw[j:j + 1, :] * pad_ref[5 + j:5 + j + S, :]
    return uc


def _lru_specs(S, n_proj_cols):
    u_col = 3 * FOX_H
    g_col = u_col + LRU_NB
    blk = pl.BlockSpec((S, HEAD), lambda n: (0, n))
    vec = pl.BlockSpec((1, HEAD), lambda n: (0, n))
    mat = pl.BlockSpec((None, HEAD, HEAD), lambda n: (n, 0, 0))
    return dict(
        u=pl.BlockSpec((S, HEAD), lambda n: (0, u_col + n)), gate=pl.BlockSpec((S, HEAD), lambda n: (0, g_col + n)),
        blk=blk, vec=vec, mat=mat, cw=pl.BlockSpec((CONV_K, HEAD), lambda n: (0, n)))


def _lru_fwd(proj, conv_w, conv_b, w_ra, b_ra, w_ri, b_ri, lam, name, after=None):
    S = proj.shape[0]
    sp_ = _lru_specs(S, proj.shape[1])
    rows8 = S // 8
    more_specs, more = _behind(after)

    def body(u_ref, gt_ref, cw_ref, cb_ref, wra_ref, bra_ref, wri_ref, bri_ref, lam_ref, *rest):
        y_ref, h_ref, pad, a_scr, b_scr = rest[len(more):]
        pad[0:8, :] = jnp.zeros((8, HEAD), F32)
        pad[8:S + 8, :] = u_ref[...]
        uc = _conv(pad, cw_ref[...], cb_ref[...], S)
        r, ig, sp, a, mult = _lru_gates(uc, wra_ref[...], bra_ref[...], wri_ref[...], bri_ref[...], lam_ref[...])
        a_scr[...] = a
        b_scr[...] = mult * (ig * uc)
        sub = lax.broadcasted_iota(jnp.int32, (8, HEAD), 0)

        def step(t, carry):
            sl = pl.ds(pl.multiple_of(t * 8, 8), 8)
            A, B = a_scr[sl, :], b_scr[sl, :]
            for d in (1, 2, 4):
                m = sub >= d
                B = jnp.where(m, A * pltpu.roll(B, d, 0) + B, B)
                A = jnp.where(m, A * pltpu.roll(A, d, 0), A)
            h = A * carry + B
            h_ref[sl, :] = h
            return h[7:8, :]

        lax.fori_loop(0, rows8, step, jnp.zeros((1, HEAD), F32))
        y_ref[...] = h_ref[...] * _gelu(gt_ref[...])

    w = LRU_NB * HEAD
    return pl.pallas_call(
        body, name=name, grid=(LRU_NB,),
        in_specs=[sp_["u"], sp_["gate"], sp_["cw"], sp_["vec"], sp_["mat"], sp_["vec"], sp_["mat"], sp_["vec"], sp_["vec"]]
        + more_specs,
        out_specs=[sp_["blk"], sp_["blk"]],
        out_shape=[jax.ShapeDtypeStruct((S, w), F32)] * 2,
        scratch_shapes=[pltpu.VMEM((S + 8, HEAD), F32), pltpu.VMEM((S, HEAD), F32), pltpu.VMEM((S, HEAD), F32)],
        compiler_params=_params("parallel"),
    )(proj, proj, conv_w, conv_b, w_ra, b_ra, w_ri, b_ri, lam, *more)


def _lru_bwd(proj, h, dy, conv_w, conv_b, w_ra, b_ra, w_ri, b_ri, lam, name, after=None):
    S = proj.shape[0]
    sp_ = _lru_specs(S, proj.shape[1])
    rows8 = S // 8
    more_specs, more = _behind(after)

    def body(u_ref, gt_ref, h_ref, dy_ref, cw_ref, cb_ref, wra_ref, bra_ref, wri_ref, bri_ref, lam_ref, *rest):
        (du_ref, dgt_ref, dcw_ref, dcb_ref, dwra_ref, dbra_ref, dwri_ref, dbri_ref, dlam_ref,
         pad, an_scr, d_scr, g_scr, hp_scr) = rest[len(more):]
        zero8 = jnp.zeros((8, HEAD), F32)
        pad[0:8, :] = zero8
        pad[8:S + 8, :] = u_ref[...]
        cw = cw_ref[...]
        uc = _conv(pad, cw, cb_ref[...], S)
        wra, wri, lam_v = wra_ref[...], wri_ref[...], lam_ref[...]
        r, ig, sp, a, mult = _lru_gates(uc, wra, bra_ref[...], wri, bri_ref[...], lam_v)
        gate = gt_ref[...]
        dy_v = dy_ref[...]
        hv = h_ref[...]
        dgt_ref[...] = (dy_v * hv * _gelu_grad(gate)).astype(BF16)
        d_scr[...] = dy_v * _gelu(gate)
        g_scr[0:S, :] = a
        g_scr[S:S + 8, :] = zero8
        an_scr[...] = g_scr[1:S + 1, :]
        sub = lax.broadcasted_iota(jnp.int32, (8, HEAD), 0)

        def step(t, carry):
            sl = pl.ds(pl.multiple_of((rows8 - 1 - t) * 8, 8), 8)
            A, D = an_scr[sl, :], d_scr[sl, :]
            for d in (1, 2, 4):
                m = sub + d <= 7
                D = jnp.where(m, A * pltpu.roll(D, 8 - d, 0) + D, D)
                A = jnp.where(m, A * pltpu.roll(A, 8 - d, 0), A)
            g = A * carry + D
            g_scr[sl, :] = g
            return g[0:1, :]

        lax.fori_loop(0, rows8, step, jnp.zeros((1, HEAD), F32))
        g = g_scr[0:S, :]
        hp_scr[0:8, :] = zero8
        hp_scr[8:S + 8, :] = hv
        da = g * hp_scr[7:S + 7, :]
        iu = ig * uc
        dmult = g * iu
        diu = g * mult
        dig = diu * uc
        duc = diu * ig
        dlog_a = da * a - dmult * (a * a) / mult
        dr = dlog_a * (-LRU_C * sp)
        dsp = jnp.sum(dlog_a * (-LRU_C * r), axis=0, keepdims=True)
        dlam_ref[...] = -dsp * jax.nn.sigmoid(-lam_v)
        dpr = dr * r * (1.0 - r)
        dpi = dig * ig * (1.0 - ig)
        dbra_ref[...] = jnp.sum(dpr, axis=0, keepdims=True)
        dbri_ref[...] = jnp.sum(dpi, axis=0, keepdims=True)
        ucb = uc.astype(BF16)
        dprb, dpib = dpr.astype(BF16), dpi.astype(BF16)
        dwra_ref[...] = _dot(ucb, dprb, TN).astype(BF16)
        dwri_ref[...] = _dot(ucb, dpib, TN).astype(BF16)
        duc = duc + _dot(dprb, wra.astype(BF16), NT) + _dot(dpib, wri.astype(BF16), NT)
        dcb_ref[...] = jnp.sum(duc, axis=0, keepdims=True)
        for j in range(CONV_K):
            dcw_ref[j:j + 1, :] = jnp.sum(duc * pad[5 + j:5 + j + S, :], axis=0, keepdims=True)
        g_scr[0:S, :] = duc
        g_scr[S:S + 8, :] = zero8
        du = jnp.zeros((S, HEAD), F32)
        for j in range(CONV_K):
            du = du + cw[j:j + 1, :] * g_scr[3 - j:3 - j + S, :]
        du_ref[...] = du.astype(BF16)

    w = LRU_NB * HEAD
    bf = jax.ShapeDtypeStruct((S, w), BF16)
    vec = jax.ShapeDtypeStruct((1, w), F32)
    mat = jax.ShapeDtypeStruct((LRU_NB, HEAD, HEAD), BF16)
    return pl.pallas_call(
        body, name=name, grid=(LRU_NB,),
        in_specs=[sp_["u"], sp_["gate"], sp_["blk"], sp_["blk"], sp_["cw"], sp_["vec"], sp_["mat"], sp_["vec"],
                  sp_["mat"], sp_["vec"], sp_["vec"]] + more_specs,
        out_specs=[sp_["blk"], sp_["blk"], sp_["cw"], sp_["vec"], sp_["mat"], sp_["vec"], sp_["mat"], sp_["vec"], sp_["vec"]],
        out_shape=[bf, bf, jax.ShapeDtypeStruct((CONV_K, w), F32), vec, mat, vec, mat, vec, vec],
        scratch_shapes=[pltpu.VMEM((S + 8, HEAD), F32), pltpu.VMEM((S, HEAD), F32), pltpu.VMEM((S, HEAD), F32),
                        pltpu.VMEM((S + 8, HEAD), F32), pltpu.VMEM((S + 8, HEAD), F32)],
        compiler_params=_params("parallel"),
    )(proj, proj, h, dy, conv_w, conv_b, w_ra, b_ra, w_ri, b_ri, lam, *more)


def _xattn_fwd(cq, ckv, g_cq, g_ck, name):
    S, w = cq.shape
    M = ckv.shape[0]
    tr = _row_tile(S)
    scale = 1.0 / math.sqrt(HEAD)

    def body(cq_ref, ckv_ref, gq_ref, gk_ref, o_ref):
        for h in range(XATT_H):
            sl = slice(h * HEAD, (h + 1) * HEAD)
            qn = _rms(cq_ref[:, sl], gq_ref[...]).astype(BF16)
            kn = _rms(ckv_ref[:, sl], gk_ref[...]).astype(BF16)
            v = ckv_ref[:, w + h * HEAD:w + (h + 1) * HEAD].astype(BF16)
            s = _dot(qn, kn, NT) * scale
            p = jnp.exp(s - jnp.max(s, axis=1, keepdims=True))
            p = p / jnp.sum(p, axis=1, keepdims=True)
            o_ref[:, sl] = _dot(p.astype(BF16), v).astype(BF16)

    gain = pl.BlockSpec((1, HEAD), lambda i: (0, 0))
    return pl.pallas_call(
        body, name=name, grid=(S // tr,),
        in_specs=[pl.BlockSpec((tr, w), lambda i: (i, 0)), pl.BlockSpec((M, 2 * w), lambda i: (0, 0)), gain, gain],
        out_specs=pl.BlockSpec((tr, w), lambda i: (i, 0)), out_shape=jax.ShapeDtypeStruct((S, w), BF16),
        compiler_params=_params("parallel"),
    )(cq, ckv, g_cq, g_ck)


def _xattn_bwd(cq, ckv, g_cq, g_ck, do, name):
    S, w = cq.shape
    M = ckv.shape[0]
    tr = _row_tile(S)
    nsteps = S // tr
    scale = 1.0 / math.sqrt(HEAD)

    def body(cq_ref, ckv_ref, gq_ref, gk_ref, do_ref, dcq_ref, dckv_ref, dgq_ref, dgk_ref, dkn_scr, dv_scr):
        step = pl.program_id(0)

        @pl.when(step == 0)
        def _():
            dkn_scr[...] = jnp.zeros_like(dkn_scr)
            dv_scr[...] = jnp.zeros_like(dv_scr)
            dgq_ref[...] = jnp.zeros_like(dgq_ref)

        for h in range(XATT_H):
            sl = slice(h * HEAD, (h + 1) * HEAD)
            q_raw = cq_ref[:, sl]
            qn = _rms(q_raw, gq_ref[...]).astype(BF16)
            kn = _rms(ckv_ref[:, sl], gk_ref[...]).astype(BF16)
            v = ckv_ref[:, w + h * HEAD:w + (h + 1) * HEAD].astype(BF16)
            s = _dot(qn, kn, NT) * scale
            p = jnp.exp(s - jnp.max(s, axis=1, keepdims=True))
            p = p / jnp.sum(p, axis=1, keepdims=True)
            dob = do_ref[:, sl].astype(BF16)
            dp = _dot(dob, v, NT)
            ds = p * (dp - jnp.sum(p * dp, axis=1, keepdims=True)) * scale
            dsb = ds.astype(BF16)
            dv_scr[:, sl] += _dot(p.astype(BF16), dob, TN)
            dkn_scr[:, sl] += _dot(dsb, qn, TN)
            dq, dgq = _rms_grad(q_raw, gq_ref[...], _dot(dsb, kn))
            dcq_ref[:, sl] = dq.astype(BF16)
            dgq_ref[...] += dgq

        @pl.when(step == nsteps - 1)
        def _():
            dgk = jnp.zeros((1, HEAD), F32)
            for h in range(XATT_H):
                sl = slice(h * HEAD, (h + 1) * HEAD)
                dk, dgk_h = _rms_grad(ckv_ref[:, sl], gk_ref[...], dkn_scr[:, sl])
                dckv_ref[:, sl] = dk.astype(BF16)
                dgk = dgk + dgk_h
            dckv_ref[:, w:2 * w] = dv_scr[...].astype(BF16)
            dgk_ref[...] = dgk

    gain = pl.BlockSpec((1, HEAD), lambda i: (0, 0))
    row = pl.BlockSpec((tr, w), lambda i: (i, 0))
    mem = pl.BlockSpec((M, 2 * w), lambda i: (0, 0))
    return pl.pallas_call(
        body, name=name, grid=(nsteps,), in_specs=[row, mem, gain, gain, row], out_specs=[row, mem, gain, gain],
        out_shape=[jax.ShapeDtypeStruct((S, w), BF16), jax.ShapeDtypeStruct((M, 2 * w), BF16),
                   jax.ShapeDtypeStruct((1, HEAD), F32), jax.ShapeDtypeStruct((1, HEAD), F32)],
        scratch_shapes=[pltpu.VMEM((M, w), F32), pltpu.VMEM((M, w), F32)],
        compiler_params=_params("arbitrary"),
    )(cq, ckv, g_cq, g_ck, do)


def _ffn_up(h, w_gu, name, tm=512):
    S, D = h.shape
    nb, _, bw = w_gu.shape
    half = nb // 2
    tm = min(tm, S)

    def body(h_ref, wg_ref, wu_ref, g_ref, u_ref, act_ref):
        hv = h_ref[...]
        g = _dot(hv, wg_ref[...])
        u = _dot(hv, wu_ref[...])
        g_ref[...] = g
        u_ref[...] = u
        act_ref[...] = (g * jax.nn.sigmoid(g) * u).astype(BF16)

    blk = pl.BlockSpec((tm, bw), lambda j, i: (i, j))
    return pl.pallas_call(
        body, name=name, grid=(half, S // tm),
        in_specs=[pl.BlockSpec((tm, D), lambda j, i: (i, 0)), pl.BlockSpec((None, D, bw), lambda j, i: (j, 0, 0)),
                  pl.BlockSpec((None, D, bw), lambda j, i: (j + half, 0, 0))],
        out_specs=[blk, blk, blk],
        out_shape=[jax.ShapeDtypeStruct((S, half * bw), F32)] * 2 + [jax.ShapeDtypeStruct((S, half * bw), BF16)],
        compiler_params=_params("parallel", "parallel"),
    )(h, w_gu, w_gu)


def _ffn_down_dx(dy, w_down, g, u, name, tr=1024, tc=512):
    S, f = g.shape
    w2 = 2 * f
    tr, tc = min(tr, S), min(tc, f)
    assert S % tr == 0 and f % tc == 0
    n_ct = f // tc
    steps = (S // tr) * n_ct

    def body(dy_ref, w_ref, g_ref, u_ref, o_hbm, buf, sems):
        i, j = pl.program_id(0), pl.program_id(1)
        step = i * n_ct + j
        slot = step % 2

        def put(half):
            cols = pl.ds(pl.multiple_of(half * f + j * tc, 128), tc)
            return pltpu.make_async_copy(buf.at[slot, half], o_hbm.at[pl.ds(pl.multiple_of(i * tr, 8), tr), cols],
                                         sems.at[slot, half])

        @pl.when(step >= 2)
        def _():
            put(0).wait()
            put(1).wait()

        g = g_ref[...]
        sg = jax.nn.sigmoid(g)
        da = _dot(dy_ref[...], w_ref[...], NT)
        buf[slot, 0] = (da * u_ref[...] * (sg * (1.0 + g * (1.0 - sg)))).astype(BF16)
        buf[slot, 1] = (da * (g * sg)).astype(BF16)
        put(0).start()
        put(1).start()

        @pl.when(step == steps - 1)
        def _():
            put(0).wait()
            put(1).wait()
            if steps > 1:
                for half in (0, 1):
                    pltpu.make_async_copy(buf.at[1 - slot, half], o_hbm.at[pl.ds(0, tr), pl.ds(0, tc)],
                                          sems.at[1 - slot, half]).wait()

    D = dy.shape[1]
    return pl.pallas_call(
        body, name=name, grid=(S // tr, n_ct),
        in_specs=[pl.BlockSpec((tr, D), lambda i, j: (i, 0)), pl.BlockSpec((tc, D), lambda i, j: (j, 0)),
                  pl.BlockSpec((tr, tc), lambda i, j: (i, j)), pl.BlockSpec((tr, tc), lambda i, j: (i, j))],
        out_specs=ANY, out_shape=jax.ShapeDtypeStruct((S, w2), BF16),
        scratch_shapes=[pltpu.VMEM((2, 2, tr, tc), BF16), pltpu.SemaphoreType.DMA((2, 2))],
        compiler_params=_params("arbitrary", "arbitrary"),
    )(dy, w_down, g, u)


def _adamw_math(w, g, m, v):
    m = ADAM_B1 * m + (1.0 - ADAM_B1) * g
    v = ADAM_B2 * v + (1.0 - ADAM_B2) * (g * g)
    m_hat = m / (1.0 - ADAM_B1 ** ADAM_STEP)
    v_hat = v / (1.0 - ADAM_B2 ** ADAM_STEP)
    delta = -ADAM_LR * (m_hat / (jnp.sqrt(v_hat) + ADAM_EPS) + ADAM_WD * w)
    return delta, m, v


def _tile2(R, C, elems):
    if R % 16 == 0:
        return _row_tile(R, cap=max(16, elems // C // 16 * 16)), C
    tc = 128
    while C % (2 * tc) == 0 and R * 2 * tc <= elems:
        tc *= 2
    return R, tc


def _reduce_adamw(own, land, chip, w, m, v, name):
    _, R, C = own.shape
    tr, tc = _tile2(R, C, 1 << (19 if w.ndim == 3 else 18))

    def body(s_ref, o_ref, l1_ref, l2_ref, l3_ref, w_ref, m_ref, v_ref, g_ref, d_ref, nm_ref, nv_ref):
        g = o_ref[...].astype(F32) + l1_ref[...].astype(F32) + l2_ref[...].astype(F32) + l3_ref[...].astype(F32)
        g_ref[...] = g
        d_ref[...], nm_ref[...], nv_ref[...] = _adamw_math(w_ref[...], g, m_ref[...], v_ref[...])

    def part(flip):
        return pl.BlockSpec((None, tr, tc), lambda i, j, s: (s[0] ^ flip, i, j))

    if w.ndim == 3:
        blk = pl.BlockSpec((tr, None, tc), lambda i, j, s: (i, 0, j))
    else:
        blk = pl.BlockSpec((tr, tc), lambda i, j, s: (i, j))
    sds = jax.ShapeDtypeStruct(w.shape, F32)
    return pl.pallas_call(
        body, name=name,
        grid_spec=pltpu.PrefetchScalarGridSpec(
            num_scalar_prefetch=1, grid=(R // tr, C // tc),
            in_specs=[part(0), part(1), part(2), part(3), blk, blk, blk], out_specs=[blk] * 4),
        out_shape=[sds] * 4, compiler_params=_params("parallel", "parallel"),
    )(chip, own, land, land, land, w, m, v)


def _pair_sum(g8, recv, core, name):
    _, R, C = g8.shape
    tr, tc = _tile2(R, C, 1 << 21)

    def body(c_ref, a_ref, b_ref, o_ref):
        o_ref[...] = (a_ref[...].astype(F32) + b_ref[...].astype(F32)).astype(BF16)

    return pl.pallas_call(
        body, name=name,
        grid_spec=pltpu.PrefetchScalarGridSpec(
            num_scalar_prefetch=1, grid=(N_CHIP, R // tr, C // tc),
            in_specs=[pl.BlockSpec((None, tr, tc), lambda q, i, j, c: (2 * q + c[0], i, j)),
                      pl.BlockSpec((None, tr, tc), lambda q, i, j, c: (q, i, j))],
            out_specs=pl.BlockSpec((None, tr, tc), lambda q, i, j, c: (q, i, j))),
        out_shape=jax.ShapeDtypeStruct((N_CHIP, R, C), BF16),
        compiler_params=_params("parallel", "parallel", "parallel"),
    )(core, g8, recv)


def _place():
    return lax.axis_index("x"), lax.axis_index("y"), lax.axis_index("c")


def _land_with_own(shard, dev, name):
    R, C = shard.shape
    tr, tc = _tile2(R, C, 1 << 19)

    def body(d_ref, s_ref, o_ref):
        o_ref[...] = s_ref[...]

    return pl.pallas_call(
        body, name=name,
        grid_spec=pltpu.PrefetchScalarGridSpec(
            num_scalar_prefetch=1, grid=(R // tr, C // tc),
            in_specs=[pl.BlockSpec((tr, tc), lambda i, j, d: (i, j))],
            out_specs=pl.BlockSpec((None, tr, tc), lambda i, j, d: (d[0], i, j))),
        out_shape=jax.ShapeDtypeStruct((N_DEV, R, C), shard.dtype),
        compiler_params=_params("parallel", "parallel"),
    )(dev, shard)


def _all_gather(shards, lands, name):
    n = len(shards)

    def body(*refs):
        ins, outs = refs[:n], refs[2 * n:3 * n]
        send_sems, recv_sems = refs[3 * n:]
        x, y, c = _place()
        me, sibling = (x, y, c), (x, y, 1 - c)
        chips = [(1 - x, y), (x, 1 - y), (1 - x, 1 - y)]

        def copy(a, k, block, to, src=None):
            dst = outs[a].at[4 * block[0] + 2 * block[1] + block[2]]
            return pltpu.make_async_remote_copy(
                src_ref=dst if src is None else src, dst_ref=dst, send_sem=send_sems.at[a, k],
                recv_sem=recv_sems.at[a, k], device_id=to, device_id_type=MESH)

        sent = []
        for a in range(n):
            sent.append(copy(a, 0, me, sibling, src=ins[a]))
            sent += [copy(a, 1 + j, me, (*chip, c), src=ins[a]) for j, chip in enumerate(chips)]
        for cp in sent:
            cp.start()
        for j, chip in enumerate(chips):
            for a in range(n):
                copy(a, 1 + j, (*chip, c), me).wait_recv()
                fwd = copy(a, 4 + j, (*chip, c), sibling)
                fwd.start()
                sent.append(fwd)
        for a in range(n):
            copy(a, 0, sibling, me).wait_recv()
        for j, chip in enumerate(chips):
            for a in range(n):
                copy(a, 4 + j, (*chip, 1 - c), me).wait_recv()
        for cp in sent:
            cp.wait_send()

    return pl.pallas_call(
        body, name=name, in_specs=[ANY] * (2 * n), out_specs=[ANY] * n,
        out_shape=[jax.ShapeDtypeStruct(l.shape, l.dtype) for l in lands],
        input_output_aliases={n + a: a for a in range(n)},
        scratch_shapes=[pltpu.SemaphoreType.DMA((n, 7)), pltpu.SemaphoreType.DMA((n, 7))],
    )(*shards, *lands)


def _pair_plan(srcs, lands):
    x, y, c = _place()
    plan = []
    for a in range(len(srcs)):
        for q in range(N_CHIP):
            plan.append((srcs[a].at[2 * q + 1 - c], lands[a].at[q], lands[a].at[q], (x, y, 1 - c)))
    return plan


def _chip_plan(srcs, lands):
    x, y, c = _place()
    mine = 2 * x + y
    plan = []
    for a in range(len(srcs)):
        for px, py in [(1 - x, y), (x, 1 - y), (1 - x, 1 - y)]:
            peer = 2 * px + py
            plan.append((srcs[a].at[peer], lands[a].at[mine], lands[a].at[peer], (px, py, c)))
    return plan


def _gather_plan(srcs, lands):
    x, y, c = _place()
    mine = 4 * x + 2 * y + c
    plan = []
    for a in range(len(srcs)):
        for px, py, pc in [(x, y, 1 - c), (1 - x, y, c), (x, 1 - y, c), (1 - x, 1 - y, c)]:
            plan.append((srcs[a], lands[a].at[mine], lands[a].at[4 * px + 2 * py + pc], (px, py, pc)))
    return plan


def _forward_plan(srcs, lands):
    x, y, c = _place()
    plan = []
    for a in range(len(lands)):
        for px, py in [(1 - x, y), (x, 1 - y), (1 - x, 1 - y)]:
            mine, theirs = 4 * px + 2 * py + c, 4 * px + 2 * py + 1 - c
            plan.append((lands[a].at[mine], lands[a].at[mine], lands[a].at[theirs], (x, y, 1 - c)))
    return plan


def _remote(src, dst, send_sem, recv_sem, peer):
    return pltpu.make_async_remote_copy(src_ref=src, dst_ref=dst, send_sem=send_sem, recv_sem=recv_sem,
                                        device_id=peer, device_id_type=MESH)


HBM = pl.BlockSpec(memory_space=pltpu.HBM)
SEMS = pl.BlockSpec(memory_space=pltpu.SEMAPHORE)
DATAFLOW = pltpu.SideEffectType.DATAFLOW_SIDE_EFFECTING


def _split_start(srcs, lands, after, plan_fn, per_array, name):
    ns, nb = len(srcs), len(srcs) + len(lands)
    ncopy = per_array * len(lands)
    after = list(after) if isinstance(after, (list, tuple)) else [after]

    def body(*refs):
        send_sems, recv_sems = refs[nb + len(after)], refs[nb + len(after) + 1]
        token = refs[-1]
        for k, (src, dst, _, peer) in enumerate(plan_fn(refs[:ns], refs[ns:nb])):
            _remote(src, dst, send_sems.at[k], recv_sems.at[k], peer).start()
        token[...] = jnp.zeros_like(token)

    thru = [pltpu.HBM(a.shape, a.dtype) for a in (*srcs, *lands)]
    out = pl.pallas_call(
        body, name=name,
        out_shape=(pltpu.SemaphoreType.DMA((ncopy,)), pltpu.SemaphoreType.DMA((ncopy,)), *thru,
                   jax.ShapeDtypeStruct((8, 128), F32)),
        in_specs=[HBM] * nb + [ANY] * len(after),
        out_specs=(SEMS, SEMS, *[HBM] * nb, pl.BlockSpec(memory_space=pltpu.VMEM)),
        input_output_aliases={i: 2 + i for i in range(nb)},
        compiler_params=pltpu.CompilerParams(has_side_effects=DATAFLOW),
    )(*[pltpu.with_memory_space_constraint(a, pltpu.HBM) for a in (*srcs, *lands)], *after)
    return out[0], out[1], list(out[2:2 + ns]), list(out[2 + ns:2 + nb]), out[-1]


def _split_wait(send_sems, recv_sems, srcs, lands, after, plan_fn, name):
    ns, nb = len(srcs), len(srcs) + len(lands)
    after = list(after) if isinstance(after, (list, tuple)) else [after]

    def body(*refs):
        send_ref, recv_ref = refs[nb], refs[nb + 1]
        for k, (src, dst, mine, peer) in enumerate(plan_fn(refs[:ns], refs[ns:nb])):
            _remote(src, dst, send_ref.at[k], recv_ref.at[k], peer).wait_send()
            _remote(src, mine, send_ref.at[k], recv_ref.at[k], peer).wait_recv()

    thru = [pltpu.HBM(a.shape, a.dtype) for a in (*srcs, *lands)]
    out = pl.pallas_call(
        body, name=name, out_shape=tuple(thru),
        in_specs=[HBM] * nb + [SEMS, SEMS] + [ANY] * len(after), out_specs=tuple([HBM] * nb),
        input_output_aliases={i: i for i in range(nb)},
        compiler_params=pltpu.CompilerParams(has_side_effects=DATAFLOW),
    )(*srcs, *lands, send_sems, recv_sems, *after)
    return list(out[:ns]), list(out[ns:])


def _gather_forward(lands, name):
    n = len(lands)

    def body(*refs):
        landed, out = refs[:n], refs[n:2 * n]
        send_sems, recv_sems = refs[2 * n:]
        x, y, c = _place()
        chips = [(1 - x, y), (x, 1 - y), (1 - x, 1 - y)]
        sent = []
        for a in range(n):
            for j, (px, py) in enumerate(chips):
                blk = 4 * px + 2 * py + c
                sent.append(_remote(landed[a].at[blk], out[a].at[blk], send_sems.at[a, j], recv_sems.at[a, j], (x, y, 1 - c)))
        for cp in sent:
            cp.start()
        for a in range(n):
            for j, (px, py) in enumerate(chips):
                blk = 4 * px + 2 * py + 1 - c
                _remote(landed[a].at[blk], out[a].at[blk], send_sems.at[a, j], recv_sems.at[a, j], (x, y, 1 - c)).wait_recv()
        for cp in sent:
            cp.wait_send()

    return pl.pallas_call(
        body, name=name, in_specs=[ANY] * n, out_specs=[ANY] * n,
        out_shape=[jax.ShapeDtypeStruct(l.shape, l.dtype) for l in lands],
        input_output_aliases={a: a for a in range(n)},
        scratch_shapes=[pltpu.SemaphoreType.DMA((n, 3)), pltpu.SemaphoreType.DMA((n, 3))],
    )(*lands)


def _pack_rows(groups, name):
    n = len(groups[0])
    rows = -(-n // 8) * 8
    width = max(v.shape[1] for v in groups[0])

    def body(*refs):
        ins, outs = refs[:n * len(groups)], refs[n * len(groups):]
        for gi, o_ref in enumerate(outs):
            o_ref[...] = jnp.zeros_like(o_ref)
            for r in range(n):
                v_ref = ins[gi * n + r]
                o_ref[r:r + 1, 0:v_ref.shape[1]] = v_ref[...]

    return pl.pallas_call(
        body, name=name, out_shape=[jax.ShapeDtypeStruct((rows, width), F32)] * len(groups), compiler_params=_params(),
    )(*[v for g in groups for v in g])


def _unpack_rows(packs, like, name):
    n = len(like)

    def body(*refs):
        ins, outs = refs[:len(packs)], refs[len(packs):]
        for pi, p_ref in enumerate(ins):
            for r in range(n):
                o_ref = outs[pi * n + r]
                o_ref[...] = p_ref[r:r + 1, 0:o_ref.shape[1]]

    flat = pl.pallas_call(
        body, name=name, out_shape=[jax.ShapeDtypeStruct(a.shape, F32) for _ in packs for a in like],
        compiler_params=_params(),
    )(*packs)
    return [flat[pi * n:(pi + 1) * n] for pi in range(len(packs))]


def _sum_adamw(parts, w, m, v, name, col=None):
    P, R, _ = parts.shape
    C = w.shape[1]
    tr = _row_tile(R, cap=512, mult=8)

    def body(c_ref, p_ref, w_ref, m_ref, v_ref, g_ref, d_ref, nm_ref, nv_ref):
        g = p_ref[0].astype(F32)
        for q in range(1, P):
            g = g + p_ref[q].astype(F32)
        g_ref[...] = g
        d_ref[...], nm_ref[...], nv_ref[...] = _adamw_math(w_ref[...], g, m_ref[...], v_ref[...])

    blk = pl.BlockSpec((tr, C), lambda i, c: (i, 0))
    sds = jax.ShapeDtypeStruct((R, C), F32)
    at = jnp.zeros((1,), jnp.int32) if col is None else col
    return pl.pallas_call(
        body, name=name,
        grid_spec=pltpu.PrefetchScalarGridSpec(
            num_scalar_prefetch=1, grid=(R // tr,),
            in_specs=[pl.BlockSpec((P, tr, C), lambda i, c: (0, i, c[0])), blk, blk, blk], out_specs=[blk] * 4),
        out_shape=[sds] * 4, compiler_params=_params("parallel"),
    )(at, parts, w, m, v)


def _local_step(x, mem, target, sp, W, arrived, want, done, paired):
    h0 = _rms_fwd([x], [sp["g_mix"]], "rms_mix")
    proj = _mm(h0, W["main_t"], "nt", F32, "proj_main", tm=2048, tn=512)
    f_logit = _mm(h0, W["f_t"], "nt", F32, "proj_f", tm=2048)
    qn, kn, vb = _qk_fwd(proj, sp["g_q"], sp["g_k"], "qk_norm")
    b_pad = jnp.pad(sp["b_f"], ((0, 0), (0, HEAD - FOX_H)))
    c_col, c_row = _fgate_fwd(f_logit, b_pad, "forget_cumsum")
    o_fox, lse = _fox_fwd(qn, kn, vb, c_col, c_row, "fox_fwd")
    lru_p = (sp["conv_w"], sp["conv_b"], sp["w_ra"], sp["b_ra"], sp["w_ri"], sp["b_ri"], sp["lam"])
    y_lru, h_lru = _lru_fwd(proj, *lru_p, "lru_fwd", after=arrived("attn", o_fox))
    w_out, w_cq, w_ckv, w_co = want("attn", y_lru)
    mix = _rms_fwd([o_fox, y_lru], [sp["g_fox_out"], sp["g_lru_out"]], "rms_mix_out")
    x1 = _mm(mix, w_out, "nn", F32, "out_proj", add=x)
    hq = _rms_fwd([x1], [sp["g_xattn"]], "rms_xattn")
    mn = _rms_fwd([mem], [sp["g_mem"]], "rms_mem")
    cq = _mm(hq, w_cq, "nn", F32, "xattn_q", tm=2048)
    ckv = _mm(mn, w_ckv, "nn", F32, "xattn_kv")
    o_x = _xattn_fwd(cq, ckv, sp["g_cq"], sp["g_ck"], "xattn_fwd")
    co_w = w_co.shape[2]
    x2 = _mm(o_x, w_co, "nn", F32, "xattn_out", add=x1, tm=2048, tn=co_w, after=arrived("gate_up", o_x))
    hf = _rms_fwd([x2], [sp["g_ffn"]], "rms_ffn")
    (w_gu,) = want("gate_up", hf)
    gu_w = w_gu.shape[2]
    f_gate, f_up, act = _ffn_up(hf, w_gu, "ffn_up")
    (w_down,) = want("down", arrived("down", act))
    x3 = _mm(act, w_down, "nn", F32, "ffn_down", add=x2, tk=w_down.shape[0] // 2)
    dy, dyb, sq = _loss_head(x3, target, "loss_head")
    g_down = _mm(act, dyb, "tn", BF16, "ffn_down_dw", tm=w_down.shape[0] // 4, tn=2048)
    dgu = _ffn_down_dx(dyb, w_down, f_gate, f_up, "ffn_down_dx")
    g_gu = _mm(hf, dgu, "tn", BF16, "ffn_up_dw", out_blocks=N_DEV, tn=gu_w)
    tok = done("ffn", [g_gu, g_down])
    dhf = _mm(dgu, w_gu, "nt", F32, "ffn_up_dx", tn=2048, tk=gu_w, after=tok)
    tok = paired("ffn", dhf)
    dx2, dx2b, dg_ffn = _rms_bwd(x2, sp["g_ffn"], dhf, "rms_ffn_bwd", res=dy, after=tok)
    d_ox = _mm(dx2b, w_co, "nt", F32, "xattn_out_dx", tm=2048, tk=co_w)
    g_co = _mm(o_x, dx2b, "tn", BF16, "xattn_out_dw", out_blocks=N_DEV, tn=co_w)
    dcq, dckv, dg_cq, dg_ck = _xattn_bwd(cq, ckv, sp["g_cq"], sp["g_ck"], d_ox, "xattn_bwd")
    dhq = _mm(dcq, w_cq, "nt", F32, "xattn_q_dx", tm=2048)
    g_cq = _mm(hq, dcq, "tn", BF16, "xattn_q_dw")
    dmn = _mm(dckv, w_ckv, "nt", F32, "xattn_kv_dx")
    g_ckv = _mm(mn, dckv, "tn", BF16, "xattn_kv_dw")
    (dg_mem,) = _rms_bwd(mem, sp["g_mem"], dmn, "rms_mem_bwd", want_dx=False)
    dx1, dx1b, dg_xattn = _rms_bwd(x1, sp["g_xattn"], dhq, "rms_xattn_bwd", res=dx2)
    dmix = _mm(dx1b, w_out, "nt", F32, "out_proj_dx")
    g_out = _mm(mix, dx1b, "tn", BF16, "out_proj_dw", tn=2048)
    tok = done("attn", [g_out, g_cq, g_ckv, g_co])
    do_fox, _, dg_fox_out = _rms_bwd(o_fox, sp["g_fox_out"], dmix, "rms_fox_out_bwd", dy_col=0, after=tok)
    dy_lru, _, dg_lru_out = _rms_bwd(y_lru, sp["g_lru_out"], dmix, "rms_lru_out_bwd", dy_col=1)
    tok = paired("attn", dy_lru)
    du, dgate, dconv_w, dconv_b, dw_ra, db_ra, dw_ri, db_ri, dlam = _lru_bwd(proj, h_lru, dy_lru, *lru_p, "lru_bwd", after=tok)
    dqn, dkn, dv, dc_col, dc_row = _fox_bwd(qn, kn, vb, c_col, c_row, lse, o_fox, do_fox, "fox_bwd")
    dq, dk, dg_q, dg_k = _qk_bwd(proj, sp["g_q"], sp["g_k"], dqn, dkn, "qk_norm_bwd")
    df, db_f = _fgate_bwd(f_logit, b_pad, dc_col, dc_row, "forget_cumsum_bwd")
    dproj = jnp.concatenate([dq, dk, dv, du, dgate], axis=1)
    g_main_t = _mm(dproj, h0, "tn", BF16, "proj_main_dw", tn=2048)
    g_f_t = _mm(df, h0, "tn", BF16, "proj_f_dw", tn=2048)
    tok = done("w_in", [g_main_t, g_f_t])
    dh_f = _mm(df, W["f_t"], "nn", F32, "proj_f_dx", tm=2048)
    dh0 = _mm(dproj, W["main_t"], "nn", F32, "proj_main_dx", add=dh_f, tk=dproj.shape[1] // 2, after=tok)
    tok = paired("w_in", dh0)
    grad_x, _, dg_mix = _rms_bwd(x, sp["g_mix"], dh0, "rms_mix_bwd", res=dx1, after=tok)
    small = dict(g_mix=dg_mix, b_f=db_f[:, :FOX_H], g_q=dg_q, g_k=dg_k, conv_w=dconv_w, conv_b=dconv_b, w_ra=dw_ra,
                 b_ra=db_ra, w_ri=dw_ri, b_ri=db_ri, lam=dlam, g_fox_out=dg_fox_out, g_lru_out=dg_lru_out,
                 g_xattn=dg_xattn, g_mem=dg_mem, g_cq=dg_cq, g_ck=dg_ck, g_ffn=dg_ffn)
    return sq, grad_x, small


BIG = ("w_in", "w_out", "w_cq", "w_ckv", "w_co", "w_gate_up", "w_down")
SMALL = ("g_mix", "b_f", "g_q", "g_k", "conv_b", "w_ra", "b_ra", "w_ri", "b_ri", "lam", "g_fox_out", "g_lru_out",
         "g_xattn", "g_mem", "g_cq", "g_ck", "g_ffn")
ORDER = ("g_mix", "w_in", "b_f", "g_q", "g_k", "conv_w", "conv_b", "w_ra", "b_ra", "w_ri", "b_ri", "lam", "g_fox_out",
         "g_lru_out", "w_out", "g_xattn", "g_mem", "w_cq", "w_ckv", "g_cq", "g_ck", "w_co", "g_ffn", "w_gate_up", "w_down")


def kernel(x, mem, g_mix, w_in, b_f, g_q, g_k, conv_w, conv_b, w_ra, b_ra, w_ri, b_ri, lam, g_fox_out, g_lru_out, w_out, g_xattn, g_mem, w_cq, w_ckv, g_cq, g_ck, w_co, g_ffn, w_gate_up, w_down, loss_target, m_g_mix, m_w_in, m_b_f, m_g_q, m_g_k, m_conv_w, m_conv_b, m_w_ra, m_b_ra, m_w_ri, m_b_ri, m_lam, m_g_fox_out, m_g_lru_out, m_w_out, m_g_xattn, m_g_mem, m_w_cq, m_w_ckv, m_g_cq, m_g_ck, m_w_co, m_g_ffn, m_w_gate_up, m_w_down, v_g_mix, v_w_in, v_b_f, v_g_q, v_g_k, v_conv_w, v_conv_b, v_w_ra, v_b_ra, v_w_ri, v_b_ri, v_lam, v_g_fox_out, v_g_lru_out, v_w_out, v_g_xattn, v_g_mem, v_w_cq, v_w_ckv, v_g_cq, v_g_ck, v_w_co, v_g_ffn, v_w_gate_up, v_w_down):
    given = dict(locals())
    w = {n: given[n] for n in ORDER}
    m = {n: given["m_" + n] for n in ORDER}
    v = {n: given["v_" + n] for n in ORDER}
    D = x.shape[2]
    fw = FOX_H * HEAD
    dev_index = 4 * lax.axis_index("x") + 2 * lax.axis_index("y") + lax.axis_index("c")
    dev = jnp.reshape(dev_index, (1,)).astype(jnp.int32)
    core = jnp.reshape(lax.axis_index("c"), (1,)).astype(jnp.int32)
    chip = jnp.reshape(2 * lax.axis_index("x") + lax.axis_index("y"), (1,)).astype(jnp.int32)

    def shard(d, n):
        return jnp.transpose(d[n], (2, 0, 1)) if n == "w_in" else d[n][0]

    def unshard(a, n):
        return jnp.transpose(a, (1, 2, 0)) if n == "w_in" else a[None]

    gather_groups = dict(attn=("w_out", "w_cq", "w_ckv", "w_co"), gate_up=("w_gate_up",), down=("w_down",))
    reduce_groups = dict(ffn=("w_gate_up", "w_down"), attn=("w_out", "w_cq", "w_ckv", "w_co"), w_in=("w_in",))
    column_blocked = ("w_co", "w_gate_up")
    flying = {}

    def land(a, tag):
        return _land_with_own(a, dev, "own_" + tag)

    def launch(group, after):
        shards = [shard(w, n).astype(BF16) for n in gather_groups[group]]
        lands = [land(s, n) for s, n in zip(shards, gather_groups[group])]
        flying[group] = _split_start(shards, lands, after, _gather_plan, 4, "gather_" + group + "_start")
        return flying[group][4]

    def arrived(group, after):
        send, recv, shards, lands, _ = flying.pop(group)
        _, lands = _split_wait(send, recv, shards, lands, after, _gather_plan, "gather_" + group + "_wait")
        flying[group] = _split_start([], lands, after, _forward_plan, 3, "gather_" + group + "_forward_start")
        if group == "attn":
            return launch("down", flying[group][4])
        return flying[group][4]

    def want(group, after):
        send, recv, _, lands, _ = flying.pop(group)
        _, full = _split_wait(send, recv, [], lands, after, _forward_plan, "gather_" + group + "_forward_wait")
        return [g if n in column_blocked else g.reshape(-1, g.shape[2]) for n, g in zip(gather_groups[group], full)]

    first = [shard(w, "w_in").astype(BF16).reshape(-1, D), conv_w[0]]
    g_in, g_conv = _all_gather(first, [land(first[0], "w_in"), land(first[1], "conv_w")], "gather_w_in")
    tok = launch("gate_up", launch("attn", g_conv))
    per = g_in.shape[1]
    f_k, f_lo = divmod(3 * fw, per)
    assert f_lo + FOX_H <= per
    wt_in = g_in.reshape(-1, D)
    W = dict(main_t=jnp.concatenate([wt_in[:3 * fw], wt_in[3 * fw + FOX_H:]], axis=0),
             f_t=jnp.pad(wt_in[3 * fw:3 * fw + FOX_H], ((0, HEAD - FOX_H), (0, 0))))
    sp = {n: w[n] for n in SMALL if n not in ("w_ra", "w_ri")}
    sp["w_ra"], sp["w_ri"] = w_ra[0], w_ri[0]
    sp["conv_w"] = jnp.transpose(g_conv, (1, 0, 2)).reshape(CONV_K, -1)
    sp["g_mix"] = sp["g_mix"] + tok[0, 0]

    pairing, reducing = {}, {}

    def done(group, grads):
        if group == "w_in":
            g_main_t, g_f_t = grads
            shards = [g_main_t[k * per:(k + 1) * per] for k in range(f_k)]
            shards.append(jnp.concatenate([g_main_t[f_k * per:3 * fw], g_f_t[:FOX_H],
                                           g_main_t[3 * fw:(f_k + 1) * per - FOX_H]], axis=0))
            shards += [g_main_t[k * per - FOX_H:(k + 1) * per - FOX_H] for k in range(f_k + 1, N_DEV)]
            grads = [jnp.stack(shards)]
        g8 = [g if g.ndim == 3 else g.reshape(N_DEV, -1, g.shape[1]) for g in grads]
        lands = [lax.empty((N_CHIP,) + g.shape[1:], g.dtype) for g in g8]
        pairing[group] = _split_start(g8, lands, chip, _pair_plan, N_CHIP, "reduce_" + group + "_pair_start")
        return pairing[group][4]

    def paired(group, after):
        send, recv, g8, lands, _ = pairing.pop(group)
        g8, from_sibling = _split_wait(send, recv, g8, lands, after, _pair_plan, "reduce_" + group + "_pair_wait")
        p4 = [_pair_sum(g, r, core, "reduce_pair_sum_" + n) for g, r, n in zip(g8, from_sibling, reduce_groups[group])]
        lands = [lax.empty(p.shape, p.dtype) for p in p4]
        reducing[group] = _split_start(p4, lands, chip, _chip_plan, 3, "reduce_" + group + "_start")
        return reducing[group][4]

    def finish(group, after):
        send, recv, p4, lands, _ = reducing.pop(group)
        p4, lands = _split_wait(send, recv, p4, lands, after, _chip_plan, "reduce_" + group + "_wait")
        return {n: tuple(unshard(r, n) for r in _reduce_adamw(p, l, chip, shard(w, n), shard(m, n), shard(v, n), "adamw_" + n))
                for n, p, l in zip(reduce_groups[group], p4, lands)}

    sq, grad_x, gs = _local_step(x[0], mem[0], loss_target[0], sp, W, arrived, want, done, paired)

    vectors = tuple(n for n in SMALL if n not in ("w_ra", "w_ri"))
    mine = [_pack_rows([[gs[n] for n in vectors] + [sq[0:1]]], "pack_small_grads")[0], gs["w_ra"].reshape(-1, HEAD),
            gs["w_ri"].reshape(-1, HEAD), gs["conv_w"]]
    lands = [land(a, "small_grads_%d" % k) for k, a in enumerate(mine)]
    s_send, s_recv, s_src, s_land, tok = _split_start(mine, lands, grad_x, _gather_plan, 4, "gather_small_start")
    out = finish("ffn", tok)
    out.update(finish("attn", tok))
    updated = [r for n in reduce_groups["ffn"] + reduce_groups["attn"] for r in out[n]]
    _, s_land = _split_wait(s_send, s_recv, s_src, s_land, updated, _gather_plan, "gather_small_wait")
    all_vec, all_ra, all_ri, all_conv = _gather_forward(s_land, "gather_small_forward")

    nothing = jnp.zeros((1, HEAD), F32)
    state = _pack_rows([[d[n] for n in vectors] + [nothing] for d in (w, m, v)], "pack_small_state")
    upd = _sum_adamw(all_vec, *state, "adamw_small_vectors")
    per_vector = _unpack_rows(upd, [w[n] for n in vectors] + [nothing], "unpack_small")
    for r, n in enumerate(vectors):
        out[n] = tuple(per_vector[k][r] for k in range(4))
    loss = 0.5 * per_vector[0][len(vectors)][0, 0] / D
    for n, parts in (("w_ra", all_ra), ("w_ri", all_ri)):
        res = _sum_adamw(parts, *[d[n].reshape(-1, HEAD) for d in (w, m, v)], "adamw_" + n)
        out[n] = tuple(r.reshape(w[n].shape) for r in res)
    out["conv_w"] = tuple(r[None] for r in _sum_adamw(all_conv, conv_w[0], m_conv_w[0], v_conv_w[0], "adamw_conv_w", col=dev))
    out.update(finish("w_in", upd[1]))

    return (loss, grad_x[None], *[out[n][0] for n in ORDER], *[out[n][1] for n in ORDER],
            *[out[n][2] for n in ORDER], *[out[n][3] for n in ORDER])
```

```python
import functools
import math

import jax
import jax.numpy as jnp
from jax import lax
from jax.experimental import pallas as pl
from jax.experimental.pallas import tpu as pltpu

F32 = jnp.float32
BF16 = jnp.bfloat16
MESH = pl.DeviceIdType.MESH
N_DEV = 8
N_CHIP = 4

HEAD = 128
FOX_H = 8
XATT_H = 4
LRU_NB = 8
CONV_K = 4
LRU_C = 8.0
RMS_EPS = 1e-6
ATT_T = 1024
CUM_T = 256
ROW_T = 256
NEG = -1e30
V7X_VMEM_LIMIT = 48 * 1024 * 1024

ADAM_LR = 0.001
ADAM_B1 = 0.9
ADAM_B2 = 0.999
ADAM_EPS = 1e-08
ADAM_WD = 0.01
ADAM_STEP = 10

NT = (((1,), (1,)), ((), ()))
TN = (((0,), (0,)), ((), ()))
NN = (((1,), (0,)), ((), ()))

ANY = pl.BlockSpec(memory_space=pl.ANY)


def _behind(after):
    return ([], []) if after is None else ([ANY], [after])


def _params(*sem):
    return pltpu.CompilerParams(dimension_semantics=sem or None, vmem_limit_bytes=V7X_VMEM_LIMIT)


def _dot(a, b, dn=NN):
    return lax.dot_general(a, b, dn, preferred_element_type=F32)


def _rms(x, g):
    r = lax.rsqrt(jnp.mean(x * x, axis=-1, keepdims=True) + RMS_EPS)
    return x * r * g


def _rms_grad(x, g, dy):
    r = lax.rsqrt(jnp.mean(x * x, axis=-1, keepdims=True) + RMS_EPS)
    xh = x * r
    dxh = dy * g
    dx = r * (dxh - xh * jnp.mean(dxh * xh, axis=-1, keepdims=True))
    return dx, jnp.sum(dy * xh, axis=0, keepdims=True)


def _gelu(x):
    k = math.sqrt(2.0 / math.pi)
    return 0.5 * x * (1.0 + jnp.tanh(k * (x + 0.044715 * (x * x * x))))


def _gelu_grad(x):
    k = math.sqrt(2.0 / math.pi)
    t = jnp.tanh(k * (x + 0.044715 * (x * x * x)))
    return 0.5 * (1.0 + t) + 0.5 * x * (1.0 - t * t) * k * (1.0 + 3.0 * 0.044715 * x * x)


def _one_minus_exp(z):
    series = -(z + 0.5 * z * z + (1.0 / 6.0) * z * z * z)
    return jnp.where(z > -1e-3, series, 1.0 - jnp.exp(z))


def _row_tile(rows, cap=ROW_T, mult=16):
    t = min(rows, cap)
    while rows % t or (t % mult and t != rows):
        t -= 1
    return t


def _mat_dims(z):
    return (z.shape[-2], z.shape[-1] * (z.shape[0] if z.ndim == 3 else 1))


def _mat_spec(arr, rblk, cblk, rsel, csel):
    if arr.ndim == 2:
        return pl.BlockSpec((rblk, cblk), lambda i, j, k: ((i, j, k)[rsel], (i, j, k)[csel]))
    nw = arr.shape[2]
    assert nw % cblk == 0, (arr.shape, cblk)
    per = nw // cblk
    return pl.BlockSpec((None, rblk, cblk),
                        lambda i, j, k: ((i, j, k)[csel] // per, (i, j, k)[rsel], (i, j, k)[csel] % per))


def _mm(a, b, mode, out_dtype, name, add=None, out_blocks=None, tm=1024, tn=1024, tk=None, after=None):
    ar, ac = _mat_dims(a)
    br, bc = _mat_dims(b)
    if mode == "nn":
        M, K, N = ar, ac, bc
        assert br == K
    elif mode == "nt":
        M, K, N = ar, ac, br
        assert bc == K
    else:
        M, K, N = ac, ar, bc
        assert br == K
    tm, tn = min(tm, M), min(tn, N)
    tk = K if tk is None or mode == "tn" else min(tk, K)
    assert M % tm == 0 and N % tn == 0 and K % tk == 0, (name, M, N, K, tm, tn, tk)
    nk = K // tk
    nj = N // tn
    if mode == "nn":
        specs = [_mat_spec(a, tm, tk, 0, 2), _mat_spec(b, tk, tn, 2, 1)]
        dn = NN
    elif mode == "nt":
        specs = [_mat_spec(a, tm, tk, 0, 2), _mat_spec(b, tn, tk, 1, 2)]
        dn = NT
    else:
        specs = [_mat_spec(a, tk, tm, 2, 0), _mat_spec(b, tk, tn, 2, 1)]
        dn = TN
    args = [a, b]
    if add is not None:
        specs.append(pl.BlockSpec((tm, tn), lambda i, j, k: (i, j)))
        args.append(add)
    specs += _behind(after)[0]
    args += _behind(after)[1]
    n_in = len(args)
    if out_blocks is None:
        out_shape = jax.ShapeDtypeStruct((M, N), out_dtype)
        out_spec = pl.BlockSpec((tm, tn), lambda i, j, k: (i, j))
    else:
        nb = out_blocks
        nw = N // nb
        assert nw % tn == 0
        per = nw // tn
        out_shape = jax.ShapeDtypeStruct((nb, M, nw), out_dtype)
        out_spec = pl.BlockSpec((None, tm, tn), lambda i, j, k: (j // per, i, j % per))
    keep_t = mode == "tn" and nj > 1
    scratch = []
    if nk > 1:
        scratch.append(pltpu.VMEM((tm, tn), F32))
    if keep_t:
        scratch.append(pltpu.VMEM((tm, tk), a.dtype))

    def body(*refs):
        a_ref, b_ref = refs[0], refs[1]
        add_ref = refs[2] if add is not None else None
        o_ref = refs[n_in]

        def finish(r):
            if add_ref is not None:
                r = r + add_ref[...]
            o_ref[...] = r.astype(out_dtype)

        if keep_t:
            at_ref = refs[-1]

            @pl.when(pl.program_id(1) == 0)
            def _():
                at_ref[...] = a_ref[...].T

            finish(_dot(at_ref[...], b_ref[...], NN))
        elif nk == 1:
            finish(_dot(a_ref[...], b_ref[...], dn))
        else:
            acc_ref = refs[n_in + 1]
            k = pl.program_id(2)

            @pl.when(k == 0)
            def _():
                acc_ref[...] = jnp.zeros_like(acc_ref)

            acc_ref[...] += _dot(a_ref[...], b_ref[...], dn)

            @pl.when(k == nk - 1)
            def _():
                finish(acc_ref[...])

    return pl.pallas_call(
        body, name=name, grid=(M // tm, nj, nk), in_specs=specs, out_specs=out_spec, out_shape=out_shape,
        scratch_shapes=scratch,
        compiler_params=_params("parallel", "arbitrary" if keep_t else "parallel", "arbitrary"),
    )(*args)


def _rms_fwd(xs, gs, name, after=None):
    n = len(xs)
    rows = xs[0].shape[0]
    widths = [x.shape[1] for x in xs]
    tr = _row_tile(rows)
    more_specs, more = _behind(after)

    def body(*refs):
        o_ref = refs[2 * n + len(more)]
        off = 0
        for t in range(n):
            o_ref[:, off:off + widths[t]] = _rms(refs[t][...], refs[n + t][...]).astype(BF16)
            off += widths[t]

    return pl.pallas_call(
        body, name=name, grid=(rows // tr,),
        in_specs=[pl.BlockSpec((tr, w), lambda i: (i, 0)) for w in widths]
        + [pl.BlockSpec((1, w), lambda i: (0, 0)) for w in widths] + more_specs,
        out_specs=pl.BlockSpec((tr, sum(widths)), lambda i: (i, 0)),
        out_shape=jax.ShapeDtypeStruct((rows, sum(widths)), BF16),
        compiler_params=_params("parallel"),
    )(*xs, *gs, *more)


def _rms_bwd(x, g, dy, name, dy_col=0, res=None, want_dx=True, after=None):
    rows, w = x.shape
    tr = _row_tile(rows)
    has_res = res is not None
    more_specs, more = _behind(after)

    def body(*refs):
        x_ref, g_ref, dy_ref = refs[:3]
        res_ref = refs[3] if has_res else None
        outs = refs[3 + has_res + len(more):]
        dg_ref = outs[-1]
        dx, dg = _rms_grad(x_ref[...], g_ref[...], dy_ref[...])

        @pl.when(pl.program_id(0) == 0)
        def _():
            dg_ref[...] = jnp.zeros_like(dg_ref)

        dg_ref[...] += dg
        if want_dx:
            if has_res:
                dx = dx + res_ref[...]
            outs[0][...] = dx
            outs[1][...] = dx.astype(BF16)

    row_spec = pl.BlockSpec((tr, w), lambda i: (i, 0))
    in_specs = [row_spec, pl.BlockSpec((1, w), lambda i: (0, 0)), pl.BlockSpec((tr, w), lambda i: (i, dy_col))]
    args = [x, g, dy]
    if has_res:
        in_specs.append(row_spec)
        args.append(res)
    in_specs += more_specs
    args += more
    out_specs, out_shape = [], []
    if want_dx:
        out_specs += [row_spec, row_spec]
        out_shape += [jax.ShapeDtypeStruct((rows, w), F32), jax.ShapeDtypeStruct((rows, w), BF16)]
    out_specs.append(pl.BlockSpec((1, w), lambda i: (0, 0)))
    out_shape.append(jax.ShapeDtypeStruct((1, w), F32))
    return pl.pallas_call(
        body, name=name, grid=(rows // tr,), in_specs=in_specs, out_specs=out_specs, out_shape=out_shape,
        compiler_params=_params("arbitrary"),
    )(*args)


def _loss_head(y, target, name):
    rows, w = y.shape
    tr = _row_tile(rows)

    def body(y_ref, t_ref, dy_ref, dyb_ref, acc_ref):
        e = y_ref[...] - t_ref[...]

        @pl.when(pl.program_id(0) == 0)
        def _():
            acc_ref[...] = jnp.zeros_like(acc_ref)

        acc_ref[...] += jnp.sum(e * e)
        dy = e * (1.0 / w)
        dy_ref[...] = dy
        dyb_ref[...] = dy.astype(BF16)

    row_spec = pl.BlockSpec((tr, w), lambda i: (i, 0))
    return pl.pallas_call(
        body, name=name, grid=(rows // tr,), in_specs=[row_spec, row_spec],
        out_specs=[row_spec, row_spec, pl.BlockSpec((8, 128), lambda i: (0, 0))],
        out_shape=[jax.ShapeDtypeStruct((rows, w), F32), jax.ShapeDtypeStruct((rows, w), BF16),
                   jax.ShapeDtypeStruct((8, 128), F32)],
        compiler_params=_params("arbitrary"),
    )(y, target)


def _qk_fwd(proj, g_q, g_k, name, after=None):
    rows = proj.shape[0]
    w = FOX_H * HEAD
    tr = _row_tile(rows)
    more_specs, more = _behind(after)

    def body(q_ref, k_ref, v_ref, gq_ref, gk_ref, *rest):
        qn_ref, kn_ref, vb_ref = rest[len(more):]
        for h in range(FOX_H):
            sl = slice(h * HEAD, (h + 1) * HEAD)
            qn_ref[:, sl] = _rms(q_ref[:, sl], gq_ref[...]).astype(BF16)
            kn_ref[:, sl] = _rms(k_ref[:, sl], gk_ref[...]).astype(BF16)
        vb_ref[...] = v_ref[...].astype(BF16)

    gain = pl.BlockSpec((1, HEAD), lambda i: (0, 0))
    out = pl.BlockSpec((tr, w), lambda i: (i, 0))
    return pl.pallas_call(
        body, name=name, grid=(rows // tr,),
        in_specs=[pl.BlockSpec((tr, w), lambda i: (i, 0)), pl.BlockSpec((tr, w), lambda i: (i, 1)),
                  pl.BlockSpec((tr, w), lambda i: (i, 2)), gain, gain] + more_specs,
        out_specs=[out, out, out], out_shape=[jax.ShapeDtypeStruct((rows, w), BF16)] * 3,
        compiler_params=_params("parallel"),
    )(proj, proj, proj, g_q, g_k, *more)


def _qk_bwd(proj, g_q, g_k, dqn, dkn, name):
    rows = proj.shape[0]
    w = FOX_H * HEAD
    tr = _row_tile(rows)

    def body(q_ref, k_ref, gq_ref, gk_ref, dqn_ref, dkn_ref, dq_ref, dk_ref, dgq_ref, dgk_ref):
        @pl.when(pl.program_id(0) == 0)
        def _():
            dgq_ref[...] = jnp.zeros_like(dgq_ref)
            dgk_ref[...] = jnp.zeros_like(dgk_ref)

        for h in range(FOX_H):
            sl = slice(h * HEAD, (h + 1) * HEAD)
            dq, dgq = _rms_grad(q_ref[:, sl], gq_ref[...], dqn_ref[:, sl])
            dk, dgk = _rms_grad(k_ref[:, sl], gk_ref[...], dkn_ref[:, sl])
            dq_ref[:, sl] = dq.astype(BF16)
            dk_ref[:, sl] = dk.astype(BF16)
            dgq_ref[...] += dgq
            dgk_ref[...] += dgk

    gain = pl.BlockSpec((1, HEAD), lambda i: (0, 0))
    row = pl.BlockSpec((tr, w), lambda i: (i, 0))
    return pl.pallas_call(
        body, name=name, grid=(rows // tr,),
        in_specs=[row, pl.BlockSpec((tr, w), lambda i: (i, 1)), gain, gain, row, row],
        out_specs=[row, row, gain, gain],
        out_shape=[jax.ShapeDtypeStruct((rows, w), BF16)] * 2 + [jax.ShapeDtypeStruct((1, HEAD), F32)] * 2,
        compiler_params=_params("arbitrary"),
    )(proj, proj, g_q, g_k, dqn, dkn)


def _fgate_fwd(f_logit, b_pad, name):
    S = f_logit.shape[0]
    T = min(CUM_T, S)
    nb = S // T
    RT = min(ATT_T, S)
    nbr = S // RT

    def body(f_ref, b_ref, col_ref, row_ref, c_scr):
        tri = (lax.broadcasted_iota(jnp.int32, (T, T), 0) >= lax.broadcasted_iota(jnp.int32, (T, T), 1)).astype(F32)
        carry = jnp.zeros((1, HEAD), F32)
        for blk in range(nb):
            z = f_ref[blk * T:(blk + 1) * T, :] + b_ref[...]
            lf = jnp.minimum(z, 0.0) - jnp.log1p(jnp.exp(-jnp.abs(z)))
            cb = jnp.dot(tri, lf, precision=lax.Precision.HIGHEST, preferred_element_type=F32) + carry
            c_scr[blk * T:(blk + 1) * T, :] = cb
            carry = cb[T - 1:T, :]
        c = c_scr[...]
        lane = lax.broadcasted_iota(jnp.int32, c.shape, 1)
        ct = c.T
        for h in range(FOX_H):
            col_ref[h] = jnp.sum(jnp.where(lane == h, c, 0.0), axis=1, keepdims=True)
            for jb in range(nbr):
                row_ref[h, jb] = ct[h:h + 1, jb * RT:(jb + 1) * RT]

    return pl.pallas_call(
        body, name=name,
        out_shape=[jax.ShapeDtypeStruct((FOX_H, S, 1), F32), jax.ShapeDtypeStruct((FOX_H, nbr, 1, RT), F32)],
        scratch_shapes=[pltpu.VMEM((S, HEAD), F32)], compiler_params=_params(),
    )(f_logit, b_pad)


def _fgate_bwd(f_logit, b_pad, dc_col, dc_row, name):
    S = f_logit.shape[0]
    T = min(CUM_T, S)
    nb = S // T
    RT = min(ATT_T, S)

    def body(f_ref, b_ref, dcol_ref, drow_ref, df_ref, db_ref, dc_scr, rt_scr):
        lane = lax.broadcasted_iota(jnp.int32, (S, HEAD), 1)
        sub = lax.broadcasted_iota(jnp.int32, (HEAD, RT), 0)
        dc = jnp.zeros((S, HEAD), F32)
        for h in range(FOX_H):
            dc = jnp.where(lane == h, dcol_ref[h], dc)
        for jb in range(S // RT):
            rt = jnp.zeros((HEAD, RT), F32)
            for h in range(FOX_H):
                rt = jnp.where(sub == h, drow_ref[h, jb], rt)
            rt_scr[jb * RT:(jb + 1) * RT, :] = rt.T
        dc_scr[...] = dc - rt_scr[...]
        tri = (lax.broadcasted_iota(jnp.int32, (T, T), 0) <= lax.broadcasted_iota(jnp.int32, (T, T), 1)).astype(F32)
        carry = jnp.zeros((1, HEAD), F32)
        db = jnp.zeros((1, HEAD), F32)
        for blk in reversed(range(nb)):
            rows = slice(blk * T, (blk + 1) * T)
            dlf = jnp.dot(tri, dc_scr[rows, :], precision=lax.Precision.HIGHEST, preferred_element_type=F32) + carry
            carry = dlf[0:1, :]
            z = f_ref[rows, :] + b_ref[...]
            df = dlf * jax.nn.sigmoid(-z)
            df_ref[rows, :] = df.astype(BF16)
            db = db + jnp.sum(df, axis=0, keepdims=True)
        db_ref[...] = db

    return pl.pallas_call(
        body, name=name,
        out_shape=[jax.ShapeDtypeStruct((S, HEAD), BF16), jax.ShapeDtypeStruct((1, HEAD), F32)],
        scratch_shapes=[pltpu.VMEM((S, HEAD), F32), pltpu.VMEM((S, HEAD), F32)], compiler_params=_params(),
    )(f_logit, b_pad, dc_col, dc_row)


def _fox_fwd(qn, kn, vb, c_col, c_row, name, after=None):
    S = qn.shape[0]
    T = min(ATT_T, S)
    nb = S // T
    scale = 1.0 / math.sqrt(HEAD)
    more_specs, more = _behind(after)

    def body(q_ref, k_ref, v_ref, cc_ref, cr_ref, *rest):
        o_ref, lse_ref = rest[len(more):]
        i = pl.program_id(1)
        q = q_ref[...]
        cc = cc_ref[...]

        def step(j, carry, diagonal=False):
            m, l, acc = carry
            sl = pl.ds(pl.multiple_of(j * T, T), T)
            s = _dot(q, k_ref[sl, :], NT) * scale + cc - cr_ref[j]
            if diagonal:
                s = jnp.where(lax.broadcasted_iota(jnp.int32, (T, T), 0) >= lax.broadcasted_iota(jnp.int32, (T, T), 1), s, NEG)
            m2 = jnp.maximum(m, jnp.max(s, axis=1, keepdims=True))
            p = jnp.exp(s - m2)
            al = jnp.exp(m - m2)
            return m2, al * l + jnp.sum(p, axis=1, keepdims=True), al * acc + _dot(p.astype(BF16), v_ref[sl, :])

        init = (jnp.full((T, 1), NEG, F32), jnp.zeros((T, 1), F32), jnp.zeros((T, HEAD), F32))
        m, l, acc = step(i, lax.fori_loop(0, i, step, init), diagonal=True)
        o_ref[...] = acc / l
        lse_ref[...] = m + jnp.log(l)

    head = pl.BlockSpec((S, HEAD), lambda h, i: (0, h))
    return pl.pallas_call(
        body, name=name, grid=(FOX_H, nb),
        in_specs=[pl.BlockSpec((T, HEAD), lambda h, i: (i, h)), head, head,
                  pl.BlockSpec((None, T, 1), lambda h, i: (h, i, 0)),
                  pl.BlockSpec((None, nb, 1, T), lambda h, i: (h, 0, 0, 0))] + more_specs,
        out_specs=[pl.BlockSpec((T, HEAD), lambda h, i: (i, h)), pl.BlockSpec((None, T, 1), lambda h, i: (h, i, 0))],
        out_shape=[jax.ShapeDtypeStruct((S, FOX_H * HEAD), F32), jax.ShapeDtypeStruct((FOX_H, S, 1), F32)],
        compiler_params=_params("parallel", "parallel"),
    )(qn, kn, vb, c_col, c_row, *more)


def _fox_bwd(qn, kn, vb, c_col, c_row, lse, o, do, name):
    S = qn.shape[0]
    T = min(ATT_T, S)
    Th = T // 2
    nb = S // T
    scale = 1.0 / math.sqrt(HEAD)

    def body(q_ref, k_ref, v_ref, cc_ref, cr_ref, lse_ref, o_ref, do_ref, dq_ref, dk_ref, dv_ref, dcc_ref, dcr_ref):
        j = pl.program_id(1)

        @pl.when(j == 0)
        def _():
            dq_ref[...] = jnp.zeros_like(dq_ref)
            dcc_ref[...] = jnp.zeros_like(dcc_ref)

        k = k_ref[...]
        v = v_ref[...]
        cr = cr_ref[...]

        def tile(sl, kk, vv, crr, carry, masked):
            dk, dv, dcr = carry
            q = q_ref[sl, :]
            d_o = do_ref[sl, :]
            s = _dot(q, kk, NT) * scale + cc_ref[sl, :] - crr
            if masked:
                s = jnp.where(lax.broadcasted_iota(jnp.int32, s.shape, 0) >= lax.broadcasted_iota(jnp.int32, s.shape, 1), s, NEG)
            p = jnp.exp(s - lse_ref[sl, :])
            dob = d_o.astype(BF16)
            dp = _dot(dob, vv, NT)
            delta = jnp.sum(d_o * o_ref[sl, :], axis=1, keepdims=True)
            ds = p * (dp - delta)
            dsb = ds.astype(BF16)
            dq_ref[sl, :] += _dot(dsb, kk) * scale
            dcc_ref[sl, :] += jnp.sum(ds, axis=1, keepdims=True)
            return (dk + _dot(dsb, q, TN), dv + _dot(p.astype(BF16), dob, TN), dcr + jnp.sum(ds, axis=0, keepdims=True))

        def step(i, carry):
            return tile(pl.ds(pl.multiple_of(i * T, T), T), k, v, cr, carry, False)

        def rows(r0):
            return pl.ds(pl.multiple_of(j * T + r0, Th), Th)

        zero = (jnp.zeros((Th, HEAD), F32), jnp.zeros((Th, HEAD), F32), jnp.zeros((1, Th), F32))
        left = tile(rows(Th), k[:Th], v[:Th], cr[:, :Th], tile(rows(0), k[:Th], v[:Th], cr[:, :Th], zero, True), False)
        right = tile(rows(Th), k[Th:], v[Th:], cr[:, Th:], zero, True)
        diagonal = (jnp.concatenate([left[0], right[0]], axis=0), jnp.concatenate([left[1], right[1]], axis=0),
                    jnp.concatenate([left[2], right[2]], axis=1))
        dk, dv, dcr = lax.fori_loop(j + 1, nb, step, diagonal)
        dk_ref[...] = dk * scale
        dv_ref[...] = dv.astype(BF16)
        dcr_ref[...] = dcr

    head = pl.BlockSpec((S, HEAD), lambda h, j: (0, h))
    tile = pl.BlockSpec((T, HEAD), lambda h, j: (j, h))
    col = pl.BlockSpec((None, S, 1), lambda h, j: (h, 0, 0))
    row = pl.BlockSpec((None, None, 1, T), lambda h, j: (h, j, 0, 0))
    w = FOX_H * HEAD
    return pl.pallas_call(
        body, name=name, grid=(FOX_H, nb),
        in_specs=[head, tile, tile, col, row, col, head, head],
        out_specs=[head, tile, tile, col, row],
        out_shape=[jax.ShapeDtypeStruct((S, w), F32), jax.ShapeDtypeStruct((S, w), F32), jax.ShapeDtypeStruct((S, w), BF16),
                   jax.ShapeDtypeStruct((FOX_H, S, 1), F32), jax.ShapeDtypeStruct((FOX_H, nb, 1, T), F32)],
        compiler_params=_params("arbitrary", "arbitrary"),
    )(qn, kn, vb, c_col, c_row, lse, o, do)


def _lru_gates(uc, wra, bra, wri, bri, lam):
    ucb = uc.astype(BF16)
    r = jax.nn.sigmoid(_dot(ucb, wra.astype(BF16)) + bra)
    ig = jax.nn.sigmoid(_dot(ucb, wri.astype(BF16)) + bri)
    sp = jnp.maximum(-lam, 0.0) + jnp.log1p(jnp.exp(-jnp.abs(lam)))
    log_a = -LRU_C * r * sp
    a = jnp.exp(log_a)
    mult = jnp.sqrt(_one_minus_exp(2.0 * log_a))
    return r, ig, sp, a, mult


def _conv(pad_ref, cw, cb, S):
    uc = cb
    for j in range(CONV_K):
        uc = uc + cw[j:j + 1, :] * pad_ref[5 + j:5 + j + S, :]
    return uc


def _lru_specs(S, n_proj_cols):
    u_col = 3 * FOX_H
    g_col = u_col + LRU_NB
    blk = pl.BlockSpec((S, HEAD), lambda n: (0, n))
    vec = pl.BlockSpec((1, HEAD), lambda n: (0, n))
    mat = pl.BlockSpec((None, HEAD, HEAD), lambda n: (n, 0, 0))
    return dict(
        u=pl.BlockSpec((S, HEAD), lambda n: (0, u_col + n)), gate=pl.BlockSpec((S, HEAD), lambda n: (0, g_col + n)),
        blk=blk, vec=vec, mat=mat, cw=pl.BlockSpec((CONV_K, HEAD), lambda n: (0, n)))


def _lru_fwd(proj, conv_w, conv_b, w_ra, b_ra, w_ri, b_ri, lam, name, after=None):
    S = proj.shape[0]
    sp_ = _lru_specs(S, proj.shape[1])
    rows8 = S // 8
    more_specs, more = _behind(after)

    def body(u_ref, gt_ref, cw_ref, cb_ref, wra_ref, bra_ref, wri_ref, bri_ref, lam_ref, *rest):
        y_ref, h_ref, pad, a_scr, b_scr = rest[len(more):]
        pad[0:8, :] = jnp.zeros((8, HEAD), F32)
        pad[8:S + 8, :] = u_ref[...]
        uc = _conv(pad, cw_ref[...], cb_ref[...], S)
        r, ig, sp, a, mult = _lru_gates(uc, wra_ref[...], bra_ref[...], wri_ref[...], bri_ref[...], lam_ref[...])
        a_scr[...] = a
        b_scr[...] = mult * (ig * uc)
        sub = lax.broadcasted_iota(jnp.int32, (8, HEAD), 0)

        def step(t, carry):
            sl = pl.ds(pl.multiple_of(t * 8, 8), 8)
            A, B = a_scr[sl, :], b_scr[sl, :]
            for d in (1, 2, 4):
                m = sub >= d
                B = jnp.where(m, A * pltpu.roll(B, d, 0) + B, B)
                A = jnp.where(m, A * pltpu.roll(A, d, 0), A)
            h = A * carry + B
            h_ref[sl, :] = h
            return h[7:8, :]

        lax.fori_loop(0, rows8, step, jnp.zeros((1, HEAD), F32))
        y_ref[...] = h_ref[...] * _gelu(gt_ref[...])

    w = LRU_NB * HEAD
    return pl.pallas_call(
        body, name=name, grid=(LRU_NB,),
        in_specs=[sp_["u"], sp_["gate"], sp_["cw"], sp_["vec"], sp_["mat"], sp_["vec"], sp_["mat"], sp_["vec"], sp_["vec"]]
        + more_specs,
        out_specs=[sp_["blk"], sp_["blk"]],
        out_shape=[jax.ShapeDtypeStruct((S, w), F32)] * 2,
        scratch_shapes=[pltpu.VMEM((S + 8, HEAD), F32), pltpu.VMEM((S, HEAD), F32), pltpu.VMEM((S, HEAD), F32)],
        compiler_params=_params("parallel"),
    )(proj, proj, conv_w, conv_b, w_ra, b_ra, w_ri, b_ri, lam, *more)


def _lru_bwd(proj, h, dy, conv_w, conv_b, w_ra, b_ra, w_ri, b_ri, lam, name, after=None):
    S = proj.shape[0]
    sp_ = _lru_specs(S, proj.shape[1])
    rows8 = S // 8
    more_specs, more = _behind(after)

    def body(u_ref, gt_ref, h_ref, dy_ref, cw_ref, cb_ref, wra_ref, bra_ref, wri_ref, bri_ref, lam_ref, *rest):
        (du_ref, dgt_ref, dcw_ref, dcb_ref, dwra_ref, dbra_ref, dwri_ref, dbri_ref, dlam_ref,
         pad, an_scr, d_scr, g_scr, hp_scr) = rest[len(more):]
        zero8 = jnp.zeros((8, HEAD), F32)
        pad[0:8, :] = zero8
        pad[8:S + 8, :] = u_ref[...]
        cw = cw_ref[...]
        uc = _conv(pad, cw, cb_ref[...], S)
        wra, wri, lam_v = wra_ref[...], wri_ref[...], lam_ref[...]
        r, ig, sp, a, mult = _lru_gates(uc, wra, bra_ref[...], wri, bri_ref[...], lam_v)
        gate = gt_ref[...]
        dy_v = dy_ref[...]
        hv = h_ref[...]
        dgt_ref[...] = (dy_v * hv * _gelu_grad(gate)).astype(BF16)
        d_scr[...] = dy_v * _gelu(gate)
        g_scr[0:S, :] = a
        g_scr[S:S + 8, :] = zero8
        an_scr[...] = g_scr[1:S + 1, :]
        sub = lax.broadcasted_iota(jnp.int32, (8, HEAD), 0)

        def step(t, carry):
            sl = pl.ds(pl.multiple_of((rows8 - 1 - t) * 8, 8), 8)
            A, D = an_scr[sl, :], d_scr[sl, :]
            for d in (1, 2, 4):
                m = sub + d <= 7
                D = jnp.where(m, A * pltpu.roll(D, 8 - d, 0) + D, D)
                A = jnp.where(m, A * pltpu.roll(A, 8 - d, 0), A)
            g = A * carry + D
            g_scr[sl, :] = g
            return g[0:1, :]

        lax.fori_loop(0, rows8, step, jnp.zeros((1, HEAD), F32))
        g = g_scr[0:S, :]
        hp_scr[0:8, :] = zero8
        hp_scr[8:S + 8, :] = hv
        da = g * hp_scr[7:S + 7, :]
        iu = ig * uc
        dmult = g * iu
        diu = g * mult
        dig = diu * uc
        duc = diu * ig
        dlog_a = da * a - dmult * (a * a) / mult
        dr = dlog_a * (-LRU_C * sp)
        dsp = jnp.sum(dlog_a * (-LRU_C * r), axis=0, keepdims=True)
        dlam_ref[...] = -dsp * jax.nn.sigmoid(-lam_v)
        dpr = dr * r * (1.0 - r)
        dpi = dig * ig * (1.0 - ig)
        dbra_ref[...] = jnp.sum(dpr, axis=0, keepdims=True)
        dbri_ref[...] = jnp.sum(dpi, axis=0, keepdims=True)
        ucb = uc.astype(BF16)
        dprb, dpib = dpr.astype(BF16), dpi.astype(BF16)
        dwra_ref[...] = _dot(ucb, dprb, TN).astype(BF16)
        dwri_ref[...] = _dot(ucb, dpib, TN).astype(BF16)
        duc = duc + _dot(dprb, wra.astype(BF16), NT) + _dot(dpib, wri.astype(BF16), NT)
        dcb_ref[...] = jnp.sum(duc, axis=0, keepdims=True)
        for j in range(CONV_K):
            dcw_ref[j:j + 1, :] = jnp.sum(duc * pad[5 + j:5 + j + S, :], axis=0, keepdims=True)
        g_scr[0:S, :] = duc
        g_scr[S:S + 8, :] = zero8
        du = jnp.zeros((S, HEAD), F32)
        for j in range(CONV_K):
            du = du + cw[j:j + 1, :] * g_scr[3 - j:3 - j + S, :]
        du_ref[...] = du.astype(BF16)

    w = LRU_NB * HEAD
    bf = jax.ShapeDtypeStruct((S, w), BF16)
    vec = jax.ShapeDtypeStruct((1, w), F32)
    mat = jax.ShapeDtypeStruct((LRU_NB, HEAD, HEAD), BF16)
    return pl.pallas_call(
        body, name=name, grid=(LRU_NB,),
        in_specs=[sp_["u"], sp_["gate"], sp_["blk"], sp_["blk"], sp_["cw"], sp_["vec"], sp_["mat"], sp_["vec"],
                  sp_["mat"], sp_["vec"], sp_["vec"]] + more_specs,
        out_specs=[sp_["blk"], sp_["blk"], sp_["cw"], sp_["vec"], sp_["mat"], sp_["vec"], sp_["mat"], sp_["vec"], sp_["vec"]],
        out_shape=[bf, bf, jax.ShapeDtypeStruct((CONV_K, w), F32), vec, mat, vec, mat, vec, vec],
        scratch_shapes=[pltpu.VMEM((S + 8, HEAD), F32), pltpu.VMEM((S, HEAD), F32), pltpu.VMEM((S, HEAD), F32),
                        pltpu.VMEM((S + 8, HEAD), F32), pltpu.VMEM((S + 8, HEAD), F32)],
        compiler_params=_params("parallel"),
    )(proj, proj, h, dy, conv_w, conv_b, w_ra, b_ra, w_ri, b_ri, lam, *more)


def _xattn_fwd(cq, ckv, g_cq, g_ck, name):
    S, w = cq.shape
    M = ckv.shape[0]
    tr = _row_tile(S)
    scale = 1.0 / math.sqrt(HEAD)

    def body(cq_ref, ckv_ref, gq_ref, gk_ref, o_ref):
        for h in range(XATT_H):
            sl = slice(h * HEAD, (h + 1) * HEAD)
            qn = _rms(cq_ref[:, sl], gq_ref[...]).astype(BF16)
            kn = _rms(ckv_ref[:, sl], gk_ref[...]).astype(BF16)
            v = ckv_ref[:, w + h * HEAD:w + (h + 1) * HEAD].astype(BF16)
            s = _dot(qn, kn, NT) * scale
            p = jnp.exp(s - jnp.max(s, axis=1, keepdims=True))
            p = p / jnp.sum(p, axis=1, keepdims=True)
            o_ref[:, sl] = _dot(p.astype(BF16), v).astype(BF16)

    gain = pl.BlockSpec((1, HEAD), lambda i: (0, 0))
    return pl.pallas_call(
        body, name=name, grid=(S // tr,),
        in_specs=[pl.BlockSpec((tr, w), lambda i: (i, 0)), pl.BlockSpec((M, 2 * w), lambda i: (0, 0)), gain, gain],
        out_specs=pl.BlockSpec((tr, w), lambda i: (i, 0)), out_shape=jax.ShapeDtypeStruct((S, w), BF16),
        compiler_params=_params("parallel"),
    )(cq, ckv, g_cq, g_ck)


def _xattn_bwd(cq, ckv, g_cq, g_ck, do, name):
    S, w = cq.shape
    M = ckv.shape[0]
    tr = _row_tile(S)
    nsteps = S // tr
    scale = 1.0 / math.sqrt(HEAD)

    def body(cq_ref, ckv_ref, gq_ref, gk_ref, do_ref, dcq_ref, dckv_ref, dgq_ref, dgk_ref, dkn_scr, dv_scr):
        step = pl.program_id(0)

        @pl.when(step == 0)
        def _():
            dkn_scr[...] = jnp.zeros_like(dkn_scr)
            dv_scr[...] = jnp.zeros_like(dv_scr)
            dgq_ref[...] = jnp.zeros_like(dgq_ref)

        for h in range(XATT_H):
            sl = slice(h * HEAD, (h + 1) * HEAD)
            q_raw = cq_ref[:, sl]
            qn = _rms(q_raw, gq_ref[...]).astype(BF16)
            kn = _rms(ckv_ref[:, sl], gk_ref[...]).astype(BF16)
            v = ckv_ref[:, w + h * HEAD:w + (h + 1) * HEAD].astype(BF16)
            s = _dot(qn, kn, NT) * scale
            p = jnp.exp(s - jnp.max(s, axis=1, keepdims=True))
            p = p / jnp.sum(p, axis=1, keepdims=True)
            dob = do_ref[:, sl].astype(BF16)
            dp = _dot(dob, v, NT)
            ds = p * (dp - jnp.sum(p * dp, axis=1, keepdims=True)) * scale
            dsb = ds.astype(BF16)
            dv_scr[:, sl] += _dot(p.astype(BF16), dob, TN)
            dkn_scr[:, sl] += _dot(dsb, qn, TN)
            dq, dgq = _rms_grad(q_raw, gq_ref[...], _dot(dsb, kn))
            dcq_ref[:, sl] = dq.astype(BF16)
            dgq_ref[...] += dgq

        @pl.when(step == nsteps - 1)
        def _():
            dgk = jnp.zeros((1, HEAD), F32)
            for h in range(XATT_H):
                sl = slice(h * HEAD, (h + 1) * HEAD)
                dk, dgk_h = _rms_grad(ckv_ref[:, sl], gk_ref[...], dkn_scr[:, sl])
                dckv_ref[:, sl] = dk.astype(BF16)
                dgk = dgk + dgk_h
            dckv_ref[:, w:2 * w] = dv_scr[...].astype(BF16)
            dgk_ref[...] = dgk

    gain = pl.BlockSpec((1, HEAD), lambda i: (0, 0))
    row = pl.BlockSpec((tr, w), lambda i: (i, 0))
    mem = pl.BlockSpec((M, 2 * w), lambda i: (0, 0))
    return pl.pallas_call(
        body, name=name, grid=(nsteps,), in_specs=[row, mem, gain, gain, row], out_specs=[row, mem, gain, gain],
        out_shape=[jax.ShapeDtypeStruct((S, w), BF16), jax.ShapeDtypeStruct((M, 2 * w), BF16),
                   jax.ShapeDtypeStruct((1, HEAD), F32), jax.ShapeDtypeStruct((1, HEAD), F32)],
        scratch_shapes=[pltpu.VMEM((M, w), F32), pltpu.VMEM((M, w), F32)],
        compiler_params=_params("arbitrary"),
    )(cq, ckv, g_cq, g_ck, do)


def _ffn_up(h, w_gu, name, tm=512):
    S, D = h.shape
    nb, _, bw = w_gu.shape
    half = nb // 2
    tm = min(tm, S)

    def body(h_ref, wg_ref, wu_ref, g_ref, u_ref, act_ref):
        hv = h_ref[...]
        g = _dot(hv, wg_ref[...])
        u = _dot(hv, wu_ref[...])
        g_ref[...] = g
        u_ref[...] = u
        act_ref[...] = (g * jax.nn.sigmoid(g) * u).astype(BF16)

    blk = pl.BlockSpec((tm, bw), lambda j, i: (i, j))
    return pl.pallas_call(
        body, name=name, grid=(half, S // tm),
        in_specs=[pl.BlockSpec((tm, D), lambda j, i: (i, 0)), pl.BlockSpec((None, D, bw), lambda j, i: (j, 0, 0)),
                  pl.BlockSpec((None, D, bw), lambda j, i: (j + half, 0, 0))],
        out_specs=[blk, blk, blk],
        out_shape=[jax.ShapeDtypeStruct((S, half * bw), F32)] * 2 + [jax.ShapeDtypeStruct((S, half * bw), BF16)],
        compiler_params=_params("parallel", "parallel"),
    )(h, w_gu, w_gu)


def _ffn_down_dx(dy, w_down, g, u, name, tr=1024, tc=512):
    S, f = g.shape
    w2 = 2 * f
    tr, tc = min(tr, S), min(tc, f)
    assert S % tr == 0 and f % tc == 0
    n_ct = f // tc
    steps = (S // tr) * n_ct

    def body(dy_ref, w_ref, g_ref, u_ref, o_hbm, buf, sems):
        i, j = pl.program_id(0), pl.program_id(1)
        step = i * n_ct + j
        slot = step % 2

        def put(half):
            cols = pl.ds(pl.multiple_of(half * f + j * tc, 128), tc)
            return pltpu.make_async_copy(buf.at[slot, half], o_hbm.at[pl.ds(pl.multiple_of(i * tr, 8), tr), cols],
                                         sems.at[slot, half])

        @pl.when(step >= 2)
        def _():
            put(0).wait()
            put(1).wait()

        dy_v = dy_ref[...]
        for c0 in range(0, tc, 256):
            cs = slice(c0, c0 + 256)
            g = g_ref[:, cs]
            sg = jax.nn.sigmoid(g)
            da = _dot(dy_v, w_ref[cs, :], NT)
            buf[slot, 0, :, cs] = (da * u_ref[:, cs] * (sg * (1.0 + g * (1.0 - sg)))).astype(BF16)
            buf[slot, 1, :, cs] = (da * (g * sg)).astype(BF16)
        put(0).start()
        put(1).start()

        @pl.when(step == steps - 1)
        def _():
            put(0).wait()
            put(1).wait()
            if steps > 1:
                for half in (0, 1):
                    pltpu.make_async_copy(buf.at[1 - slot, half], o_hbm.at[pl.ds(0, tr), pl.ds(0, tc)],
                                          sems.at[1 - slot, half]).wait()

    D = dy.shape[1]
    return pl.pallas_call(
        body, name=name, grid=(S // tr, n_ct),
        in_specs=[pl.BlockSpec((tr, D), lambda i, j: (i, 0)), pl.BlockSpec((tc, D), lambda i, j: (j, 0)),
                  pl.BlockSpec((tr, tc), lambda i, j: (i, j)), pl.BlockSpec((tr, tc), lambda i, j: (i, j))],
        out_specs=ANY, out_shape=jax.ShapeDtypeStruct((S, w2), BF16),
        scratch_shapes=[pltpu.VMEM((2, 2, tr, tc), BF16), pltpu.SemaphoreType.DMA((2, 2))],
        compiler_params=_params("arbitrary", "arbitrary"),
    )(dy, w_down, g, u)


def _adamw_math(w, g, m, v):
    m = ADAM_B1 * m + (1.0 - ADAM_B1) * g
    v = ADAM_B2 * v + (1.0 - ADAM_B2) * (g * g)
    m_hat = m / (1.0 - ADAM_B1 ** ADAM_STEP)
    v_hat = v / (1.0 - ADAM_B2 ** ADAM_STEP)
    delta = -ADAM_LR * (m_hat / (jnp.sqrt(v_hat) + ADAM_EPS) + ADAM_WD * w)
    return delta, m, v


def _tile2(R, C, elems):
    if R % 16 == 0:
        return _row_tile(R, cap=max(16, elems // C // 16 * 16)), C
    tc = 128
    while C % (2 * tc) == 0 and R * 2 * tc <= elems:
        tc *= 2
    return R, tc


def _reduce_adamw(own, land, chip, w, m, v, name):
    _, R, C = own.shape
    tr, tc = _tile2(R, C, 1 << (19 if w.ndim == 3 else 18))

    def body(s_ref, o_ref, l1_ref, l2_ref, l3_ref, w_ref, m_ref, v_ref, g_ref, d_ref, nm_ref, nv_ref):
        g = o_ref[...].astype(F32) + l1_ref[...].astype(F32) + l2_ref[...].astype(F32) + l3_ref[...].astype(F32)
        g_ref[...] = g
        d_ref[...], nm_ref[...], nv_ref[...] = _adamw_math(w_ref[...], g, m_ref[...], v_ref[...])

    def part(flip):
        return pl.BlockSpec((None, tr, tc), lambda i, j, s: (s[0] ^ flip, i, j))

    if w.ndim == 3:
        blk = pl.BlockSpec((tr, None, tc), lambda i, j, s: (i, 0, j))
    else:
        blk = pl.BlockSpec((tr, tc), lambda i, j, s: (i, j))
    sds = jax.ShapeDtypeStruct(w.shape, F32)
    return pl.pallas_call(
        body, name=name,
        grid_spec=pltpu.PrefetchScalarGridSpec(
            num_scalar_prefetch=1, grid=(R // tr, C // tc),
            in_specs=[part(0), part(1), part(2), part(3), blk, blk, blk], out_specs=[blk] * 4),
        out_shape=[sds] * 4, compiler_params=_params("parallel", "parallel"),
    )(chip, own, land, land, land, w, m, v)


def _pair_sum(g8, recv, core, name):
    _, R, C = g8.shape
    tr, tc = _tile2(R, C, 1 << 21)

    def body(c_ref, a_ref, b_ref, o_ref):
        o_ref[...] = (a_ref[...].astype(F32) + b_ref[...].astype(F32)).astype(BF16)

    return pl.pallas_call(
        body, name=name,
        grid_spec=pltpu.PrefetchScalarGridSpec(
            num_scalar_prefetch=1, grid=(N_CHIP, R // tr, C // tc),
            in_specs=[pl.BlockSpec((None, tr, tc), lambda q, i, j, c: (2 * q + c[0], i, j)),
                      pl.BlockSpec((None, tr, tc), lambda q, i, j, c: (q, i, j))],
            out_specs=pl.BlockSpec((None, tr, tc), lambda q, i, j, c: (q, i, j))),
        out_shape=jax.ShapeDtypeStruct((N_CHIP, R, C), BF16),
        compiler_params=_params("parallel", "parallel", "parallel"),
    )(core, g8, recv)


def _place():
    return lax.axis_index("x"), lax.axis_index("y"), lax.axis_index("c")


def _land_with_own(shard, dev, name):
    R, C = shard.shape
    tr, tc = _tile2(R, C, 1 << 19)

    def body(d_ref, s_ref, o_ref):
        o_ref[...] = s_ref[...]

    return pl.pallas_call(
        body, name=name,
        grid_spec=pltpu.PrefetchScalarGridSpec(
            num_scalar_prefetch=1, grid=(R // tr, C // tc),
            in_specs=[pl.BlockSpec((tr, tc), lambda i, j, d: (i, j))],
            out_specs=pl.BlockSpec((None, tr, tc), lambda i, j, d: (d[0], i, j))),
        out_shape=jax.ShapeDtypeStruct((N_DEV, R, C), shard.dtype),
        compiler_params=_params("parallel", "parallel"),
    )(dev, shard)


def _all_gather(shards, lands, name):
    n = len(shards)

    def body(*refs):
        ins, outs = refs[:n], refs[2 * n:3 * n]
        send_sems, recv_sems = refs[3 * n:]
        x, y, c = _place()
        me, sibling = (x, y, c), (x, y, 1 - c)
        chips = [(1 - x, y), (x, 1 - y), (1 - x, 1 - y)]

        def copy(a, k, block, to, src=None):
            dst = outs[a].at[4 * block[0] + 2 * block[1] + block[2]]
            return pltpu.make_async_remote_copy(
                src_ref=dst if src is None else src, dst_ref=dst, send_sem=send_sems.at[a, k],
                recv_sem=recv_sems.at[a, k], device_id=to, device_id_type=MESH)

        sent = []
        for a in range(n):
            sent.append(copy(a, 0, me, sibling, src=ins[a]))
            sent += [copy(a, 1 + j, me, (*chip, c), src=ins[a]) for j, chip in enumerate(chips)]
        for cp in sent:
            cp.start()
        for j, chip in enumerate(chips):
            for a in range(n):
                copy(a, 1 + j, (*chip, c), me).wait_recv()
                fwd = copy(a, 4 + j, (*chip, c), sibling)
                fwd.start()
                sent.append(fwd)
        for a in range(n):
            copy(a, 0, sibling, me).wait_recv()
        for j, chip in enumerate(chips):
            for a in range(n):
                copy(a, 4 + j, (*chip, 1 - c), me).wait_recv()
        for cp in sent:
            cp.wait_send()

    return pl.pallas_call(
        body, name=name, in_specs=[ANY] * (2 * n), out_specs=[ANY] * n,
        out_shape=[jax.ShapeDtypeStruct(l.shape, l.dtype) for l in lands],
        input_output_aliases={n + a: a for a in range(n)},
        scratch_shapes=[pltpu.SemaphoreType.DMA((n, 7)), pltpu.SemaphoreType.DMA((n, 7))],
    )(*shards, *lands)


def _pair_plan(srcs, lands):
    x, y, c = _place()
    plan = []
    for a in range(len(srcs)):
        for q in range(N_CHIP):
            plan.append((srcs[a].at[2 * q + 1 - c], lands[a].at[q], lands[a].at[q], (x, y, 1 - c)))
    return plan


def _chip_plan(srcs, lands):
    x, y, c = _place()
    mine = 2 * x + y
    plan = []
    for a in range(len(srcs)):
        for px, py in [(1 - x, y), (x, 1 - y), (1 - x, 1 - y)]:
            peer = 2 * px + py
            plan.append((srcs[a].at[peer], lands[a].at[mine], lands[a].at[peer], (px, py, c)))
    return plan


def _gather_plan(srcs, lands):
    x, y, c = _place()
    mine = 4 * x + 2 * y + c
    plan = []
    for a in range(len(srcs)):
        for px, py, pc in [(x, y, 1 - c), (1 - x, y, c), (x, 1 - y, c), (1 - x, 1 - y, c)]:
            plan.append((srcs[a], lands[a].at[mine], lands[a].at[4 * px + 2 * py + pc], (px, py, pc)))
    return plan


def _forward_plan(srcs, lands):
    x, y, c = _place()
    plan = []
    for a in range(len(lands)):
        for px, py in [(1 - x, y), (x, 1 - y), (1 - x, 1 - y)]:
            mine, theirs = 4 * px + 2 * py + c, 4 * px + 2 * py + 1 - c
            plan.append((lands[a].at[mine], lands[a].at[mine], lands[a].at[theirs], (x, y, 1 - c)))
    return plan


def _remote(src, dst, send_sem, recv_sem, peer):
    return pltpu.make_async_remote_copy(src_ref=src, dst_ref=dst, send_sem=send_sem, recv_sem=recv_sem,
                                        device_id=peer, device_id_type=MESH)


HBM = pl.BlockSpec(memory_space=pltpu.HBM)
SEMS = pl.BlockSpec(memory_space=pltpu.SEMAPHORE)
DATAFLOW = pltpu.SideEffectType.DATAFLOW_SIDE_EFFECTING


def _split_start(srcs, lands, after, plan_fn, per_array, name):
    ns, nb = len(srcs), len(srcs) + len(lands)
    ncopy = per_array * len(lands)
    after = list(after) if isinstance(after, (list, tuple)) else [after]

    def body(*refs):
        send_sems, recv_sems = refs[nb + len(after)], refs[nb + len(after) + 1]
        token = refs[-1]
        for k, (src, dst, _, peer) in enumerate(plan_fn(refs[:ns], refs[ns:nb])):
            _remote(src, dst, send_sems.at[k], recv_sems.at[k], peer).start()
        token[...] = jnp.zeros_like(token)

    thru = [pltpu.HBM(a.shape, a.dtype) for a in (*srcs, *lands)]
    out = pl.pallas_call(
        body, name=name,
        out_shape=(pltpu.SemaphoreType.DMA((ncopy,)), pltpu.SemaphoreType.DMA((ncopy,)), *thru,
                   jax.ShapeDtypeStruct((8, 128), F32)),
        in_specs=[HBM] * nb + [ANY] * len(after),
        out_specs=(SEMS, SEMS, *[HBM] * nb, pl.BlockSpec(memory_space=pltpu.VMEM)),
        input_output_aliases={i: 2 + i for i in range(nb)},
        compiler_params=pltpu.CompilerParams(has_side_effects=DATAFLOW),
    )(*[pltpu.with_memory_space_constraint(a, pltpu.HBM) for a in (*srcs, *lands)], *after)
    return out[0], out[1], list(out[2:2 + ns]), list(out[2 + ns:2 + nb]), out[-1]


def _split_wait(send_sems, recv_sems, srcs, lands, after, plan_fn, name):
    ns, nb = len(srcs), len(srcs) + len(lands)
    after = list(after) if isinstance(after, (list, tuple)) else [after]

    def body(*refs):
        send_ref, recv_ref = refs[nb], refs[nb + 1]
        for k, (src, dst, mine, peer) in enumerate(plan_fn(refs[:ns], refs[ns:nb])):
            _remote(src, dst, send_ref.at[k], recv_ref.at[k], peer).wait_send()
            _remote(src, mine, send_ref.at[k], recv_ref.at[k], peer).wait_recv()

    thru = [pltpu.HBM(a.shape, a.dtype) for a in (*srcs, *lands)]
    out = pl.pallas_call(
        body, name=name, out_shape=tuple(thru),
        in_specs=[HBM] * nb + [SEMS, SEMS] + [ANY] * len(after), out_specs=tuple([HBM] * nb),
        input_output_aliases={i: i for i in range(nb)},
        compiler_params=pltpu.CompilerParams(has_side_effects=DATAFLOW),
    )(*srcs, *lands, send_sems, recv_sems, *after)
    return list(out[:ns]), list(out[ns:])


def _gather_forward(lands, name):
    n = len(lands)

    def body(*refs):
        landed, out = refs[:n], refs[n:2 * n]
        send_sems, recv_sems = refs[2 * n:]
        x, y, c = _place()
        chips = [(1 - x, y), (x, 1 - y), (1 - x, 1 - y)]
        sent = []
        for a in range(n):
            for j, (px, py) in enumerate(chips):
                blk = 4 * px + 2 * py + c
                sent.append(_remote(landed[a].at[blk], out[a].at[blk], send_sems.at[a, j], recv_sems.at[a, j], (x, y, 1 - c)))
        for cp in sent:
            cp.start()
        for a in range(n):
            for j, (px, py) in enumerate(chips):
                blk = 4 * px + 2 * py + 1 - c
                _remote(landed[a].at[blk], out[a].at[blk], send_sems.at[a, j], recv_sems.at[a, j], (x, y, 1 - c)).wait_recv()
        for cp in sent:
            cp.wait_send()

    return pl.pallas_call(
        body, name=name, in_specs=[ANY] * n, out_specs=[ANY] * n,
        out_shape=[jax.ShapeDtypeStruct(l.shape, l.dtype) for l in lands],
        input_output_aliases={a: a for a in range(n)},
        scratch_shapes=[pltpu.SemaphoreType.DMA((n, 3)), pltpu.SemaphoreType.DMA((n, 3))],
    )(*lands)


def _pack_rows(groups, name):
    n = len(groups[0])
    rows = -(-n // 8) * 8
    width = max(v.shape[1] for v in groups[0])

    def body(*refs):
        ins, outs = refs[:n * len(groups)], refs[n * len(groups):]
        for gi, o_ref in enumerate(outs):
            o_ref[...] = jnp.zeros_like(o_ref)
            for r in range(n):
                v_ref = ins[gi * n + r]
                o_ref[r:r + 1, 0:v_ref.shape[1]] = v_ref[...]

    return pl.pallas_call(
        body, name=name, out_shape=[jax.ShapeDtypeStruct((rows, width), F32)] * len(groups), compiler_params=_params(),
    )(*[v for g in groups for v in g])


def _unpack_rows(packs, like, name):
    n = len(like)

    def body(*refs):
        ins, outs = refs[:len(packs)], refs[len(packs):]
        for pi, p_ref in enumerate(ins):
            for r in range(n):
                o_ref = outs[pi * n + r]
                o_ref[...] = p_ref[r:r + 1, 0:o_ref.shape[1]]

    flat = pl.pallas_call(
        body, name=name, out_shape=[jax.ShapeDtypeStruct(a.shape, F32) for _ in packs for a in like],
        compiler_params=_params(),
    )(*packs)
    return [flat[pi * n:(pi + 1) * n] for pi in range(len(packs))]


def _sum_adamw(parts, w, m, v, name, col=None):
    P, R, _ = parts.shape
    C = w.shape[1]
    tr = _row_tile(R, cap=512, mult=8)

    def body(c_ref, p_ref, w_ref, m_ref, v_ref, g_ref, d_ref, nm_ref, nv_ref):
        g = p_ref[0].astype(F32)
        for q in range(1, P):
            g = g + p_ref[q].astype(F32)
        g_ref[...] = g
        d_ref[...], nm_ref[...], nv_ref[...] = _adamw_math(w_ref[...], g, m_ref[...], v_ref[...])

    blk = pl.BlockSpec((tr, C), lambda i, c: (i, 0))
    sds = jax.ShapeDtypeStruct((R, C), F32)
    at = jnp.zeros((1,), jnp.int32) if col is None else col
    return pl.pallas_call(
        body, name=name,
        grid_spec=pltpu.PrefetchScalarGridSpec(
            num_scalar_prefetch=1, grid=(R // tr,),
            in_specs=[pl.BlockSpec((P, tr, C), lambda i, c: (0, i, c[0])), blk, blk, blk], out_specs=[blk] * 4),
        out_shape=[sds] * 4, compiler_params=_params("parallel"),
    )(at, parts, w, m, v)


def _local_step(x, mem, target, sp, W, arrived, want, done, paired):
    h0 = _rms_fwd([x], [sp["g_mix"]], "rms_mix")
    proj = _mm(h0, W["main_t"], "nt", F32, "proj_main", tm=2048, tn=512)
    f_logit = _mm(h0, W["f_t"], "nt", F32, "proj_f", tm=2048)
    qn, kn, vb = _qk_fwd(proj, sp["g_q"], sp["g_k"], "qk_norm")
    b_pad = jnp.pad(sp["b_f"], ((0, 0), (0, HEAD - FOX_H)))
    c_col, c_row = _fgate_fwd(f_logit, b_pad, "forget_cumsum")
    o_fox, lse = _fox_fwd(qn, kn, vb, c_col, c_row, "fox_fwd")
    lru_p = (sp["conv_w"], sp["conv_b"], sp["w_ra"], sp["b_ra"], sp["w_ri"], sp["b_ri"], sp["lam"])
    y_lru, h_lru = _lru_fwd(proj, *lru_p, "lru_fwd", after=arrived("attn", o_fox))
    w_out, w_cq, w_ckv, w_co = want("attn", y_lru)
    mix = _rms_fwd([o_fox, y_lru], [sp["g_fox_out"], sp["g_lru_out"]], "rms_mix_out")
    x1 = _mm(mix, w_out, "nn", F32, "out_proj", add=x)
    hq = _rms_fwd([x1], [sp["g_xattn"]], "rms_xattn")
    mn = _rms_fwd([mem], [sp["g_mem"]], "rms_mem")
    cq = _mm(hq, w_cq, "nn", F32, "xattn_q", tm=2048)
    ckv = _mm(mn, w_ckv, "nn", F32, "xattn_kv")
    o_x = _xattn_fwd(cq, ckv, sp["g_cq"], sp["g_ck"], "xattn_fwd")
    co_w = w_co.shape[2]
    x2 = _mm(o_x, w_co, "nn", F32, "xattn_out", add=x1, tm=2048, tn=co_w, after=arrived("gate_up", o_x))
    hf = _rms_fwd([x2], [sp["g_ffn"]], "rms_ffn")
    (w_gu,) = want("gate_up", hf)
    gu_w = w_gu.shape[2]
    f_gate, f_up, act = _ffn_up(hf, w_gu, "ffn_up")
    (w_down,) = want("down", arrived("down", act))
    x3 = _mm(act, w_down, "nn", F32, "ffn_down", add=x2, tk=w_down.shape[0] // 2)
    dy, dyb, sq = _loss_head(x3, target, "loss_head")
    g_down = _mm(act, dyb, "tn", BF16, "ffn_down_dw", tm=w_down.shape[0] // 4, tn=2048)
    dgu = _ffn_down_dx(dyb, w_down, f_gate, f_up, "ffn_down_dx")
    g_gu = _mm(hf, dgu, "tn", BF16, "ffn_up_dw", out_blocks=N_DEV, tn=gu_w)
    tok = done("ffn", [g_gu, g_down])
    dhf = _mm(dgu, w_gu, "nt", F32, "ffn_up_dx", tn=2048, tk=gu_w, after=tok)
    tok = paired("ffn", dhf)
    dx2, dx2b, dg_ffn = _rms_bwd(x2, sp["g_ffn"], dhf, "rms_ffn_bwd", res=dy, after=tok)
    d_ox = _mm(dx2b, w_co, "nt", F32, "xattn_out_dx", tm=2048, tk=co_w)
    g_co = _mm(o_x, dx2b, "tn", BF16, "xattn_out_dw", out_blocks=N_DEV, tn=co_w)
    dcq, dckv, dg_cq, dg_ck = _xattn_bwd(cq, ckv, sp["g_cq"], sp["g_ck"], d_ox, "xattn_bwd")
    dhq = _mm(dcq, w_cq, "nt", F32, "xattn_q_dx", tm=2048)
    g_cq = _mm(hq, dcq, "tn", BF16, "xattn_q_dw")
    dmn = _mm(dckv, w_ckv, "nt", F32, "xattn_kv_dx")
    g_ckv = _mm(mn, dckv, "tn", BF16, "xattn_kv_dw")
    (dg_mem,) = _rms_bwd(mem, sp["g_mem"], dmn, "rms_mem_bwd", want_dx=False)
    dx1, dx1b, dg_xattn = _rms_bwd(x1, sp["g_xattn"], dhq, "rms_xattn_bwd", res=dx2)
    dmix = _mm(dx1b, w_out, "nt", F32, "out_proj_dx")
    g_out = _mm(mix, dx1b, "tn", BF16, "out_proj_dw", tn=2048)
    tok = done("attn", [g_out, g_cq, g_ckv, g_co])
    do_fox, _, dg_fox_out = _rms_bwd(o_fox, sp["g_fox_out"], dmix, "rms_fox_out_bwd", dy_col=0, after=tok)
    dy_lru, _, dg_lru_out = _rms_bwd(y_lru, sp["g_lru_out"], dmix, "rms_lru_out_bwd", dy_col=1)
    tok = paired("attn", dy_lru)
    du, dgate, dconv_w, dconv_b, dw_ra, db_ra, dw_ri, db_ri, dlam = _lru_bwd(proj, h_lru, dy_lru, *lru_p, "lru_bwd", after=tok)
    dqn, dkn, dv, dc_col, dc_row = _fox_bwd(qn, kn, vb, c_col, c_row, lse, o_fox, do_fox, "fox_bwd")
    dq, dk, dg_q, dg_k = _qk_bwd(proj, sp["g_q"], sp["g_k"], dqn, dkn, "qk_norm_bwd")
    df, db_f = _fgate_bwd(f_logit, b_pad, dc_col, dc_row, "forget_cumsum_bwd")
    dproj = jnp.concatenate([dq, dk, dv, du, dgate], axis=1)
    g_main_t = _mm(dproj, h0, "tn", BF16, "proj_main_dw", tn=2048)
    g_f_t = _mm(df, h0, "tn", BF16, "proj_f_dw", tn=2048)
    tok = done("w_in", [g_main_t, g_f_t])
    dh_f = _mm(df, W["f_t"], "nn", F32, "proj_f_dx", tm=2048)
    dh0 = _mm(dproj, W["main_t"], "nn", F32, "proj_main_dx", add=dh_f, tk=dproj.shape[1] // 2, after=tok)
    tok = paired("w_in", dh0)
    grad_x, _, dg_mix = _rms_bwd(x, sp["g_mix"], dh0, "rms_mix_bwd", res=dx1, after=tok)
    small = dict(g_mix=dg_mix, b_f=db_f[:, :FOX_H], g_q=dg_q, g_k=dg_k, conv_w=dconv_w, conv_b=dconv_b, w_ra=dw_ra,
                 b_ra=db_ra, w_ri=dw_ri, b_ri=db_ri, lam=dlam, g_fox_out=dg_fox_out, g_lru_out=dg_lru_out,
                 g_xattn=dg_xattn, g_mem=dg_mem, g_cq=dg_cq, g_ck=dg_ck, g_ffn=dg_ffn)
    return sq, grad_x, small


BIG = ("w_in", "w_out", "w_cq", "w_ckv", "w_co", "w_gate_up", "w_down")
SMALL = ("g_mix", "b_f", "g_q", "g_k", "conv_b", "w_ra", "b_ra", "w_ri", "b_ri", "lam", "g_fox_out", "g_lru_out",
         "g_xattn", "g_mem", "g_cq", "g_ck", "g_ffn")
ORDER = ("g_mix", "w_in", "b_f", "g_q", "g_k", "conv_w", "conv_b", "w_ra", "b_ra", "w_ri", "b_ri", "lam", "g_fox_out",
         "g_lru_out", "w_out", "g_xattn", "g_mem", "w_cq", "w_ckv", "g_cq", "g_ck", "w_co", "g_ffn", "w_gate_up", "w_down")


def kernel(x, mem, g_mix, w_in, b_f, g_q, g_k, conv_w, conv_b, w_ra, b_ra, w_ri, b_ri, lam, g_fox_out, g_lru_out, w_out, g_xattn, g_mem, w_cq, w_ckv, g_cq, g_ck, w_co, g_ffn, w_gate_up, w_down, loss_target, m_g_mix, m_w_in, m_b_f, m_g_q, m_g_k, m_conv_w, m_conv_b, m_w_ra, m_b_ra, m_w_ri, m_b_ri, m_lam, m_g_fox_out, m_g_lru_out, m_w_out, m_g_xattn, m_g_mem, m_w_cq, m_w_ckv, m_g_cq, m_g_ck, m_w_co, m_g_ffn, m_w_gate_up, m_w_down, v_g_mix, v_w_in, v_b_f, v_g_q, v_g_k, v_conv_w, v_conv_b, v_w_ra, v_b_ra, v_w_ri, v_b_ri, v_lam, v_g_fox_out, v_g_lru_out, v_w_out, v_g_xattn, v_g_mem, v_w_cq, v_w_ckv, v_g_cq, v_g_ck, v_w_co, v_g_ffn, v_w_gate_up, v_w_down):
    given = dict(locals())
    w = {n: given[n] for n in ORDER}
    m = {n: given["m_" + n] for n in ORDER}
    v = {n: given["v_" + n] for n in ORDER}
    D = x.shape[2]
    fw = FOX_H * HEAD
    dev_index = 4 * lax.axis_index("x") + 2 * lax.axis_index("y") + lax.axis_index("c")
    dev = jnp.reshape(dev_index, (1,)).astype(jnp.int32)
    core = jnp.reshape(lax.axis_index("c"), (1,)).astype(jnp.int32)
    chip = jnp.reshape(2 * lax.axis_index("x") + lax.axis_index("y"), (1,)).astype(jnp.int32)

    def shard(d, n):
        return jnp.transpose(d[n], (2, 0, 1)) if n == "w_in" else d[n][0]

    def unshard(a, n):
        return jnp.transpose(a, (1, 2, 0)) if n == "w_in" else a[None]

    gather_groups = dict(attn=("w_out", "w_cq", "w_ckv", "w_co"), gate_up=("w_gate_up",), down=("w_down",))
    reduce_groups = dict(ffn=("w_gate_up", "w_down"), attn=("w_out", "w_cq", "w_ckv", "w_co"), w_in=("w_in",))
    column_blocked = ("w_co", "w_gate_up")
    flying = {}

    def land(a, tag):
        return _land_with_own(a, dev, "own_" + tag)

    def launch(group, after):
        shards = [shard(w, n).astype(BF16) for n in gather_groups[group]]
        lands = [land(s, n) for s, n in zip(shards, gather_groups[group])]
        flying[group] = _split_start(shards, lands, after, _gather_plan, 4, "gather_" + group + "_start")
        return flying[group][4]

    def arrived(group, after):
        send, recv, shards, lands, _ = flying.pop(group)
        _, lands = _split_wait(send, recv, shards, lands, after, _gather_plan, "gather_" + group + "_wait")
        flying[group] = _split_start([], lands, after, _forward_plan, 3, "gather_" + group + "_forward_start")
        if group == "attn":
            return launch("down", flying[group][4])
        return flying[group][4]

    def want(group, after):
        send, recv, _, lands, _ = flying.pop(group)
        _, full = _split_wait(send, recv, [], lands, after, _forward_plan, "gather_" + group + "_forward_wait")
        return [g if n in column_blocked else g.reshape(-1, g.shape[2]) for n, g in zip(gather_groups[group], full)]

    first = [shard(w, "w_in").astype(BF16).reshape(-1, D), conv_w[0]]
    g_in, g_conv = _all_gather(first, [land(first[0], "w_in"), land(first[1], "conv_w")], "gather_w_in")
    tok = launch("gate_up", launch("attn", g_conv))
    per = g_in.shape[1]
    f_k, f_lo = divmod(3 * fw, per)
    assert f_lo + FOX_H <= per
    wt_in = g_in.reshape(-1, D)
    W = dict(main_t=jnp.concatenate([wt_in[:3 * fw], wt_in[3 * fw + FOX_H:]], axis=0),
             f_t=jnp.pad(wt_in[3 * fw:3 * fw + FOX_H], ((0, HEAD - FOX_H), (0, 0))))
    sp = {n: w[n] for n in SMALL if n not in ("w_ra", "w_ri")}
    sp["w_ra"], sp["w_ri"] = w_ra[0], w_ri[0]
    sp["conv_w"] = jnp.transpose(g_conv, (1, 0, 2)).reshape(CONV_K, -1)
    sp["g_mix"] = sp["g_mix"] + tok[0, 0]

    pairing, reducing = {}, {}

    def done(group, grads):
        if group == "w_in":
            g_main_t, g_f_t = grads
            shards = [g_main_t[k * per:(k + 1) * per] for k in range(f_k)]
            shards.append(jnp.concatenate([g_main_t[f_k * per:3 * fw], g_f_t[:FOX_H],
                                           g_main_t[3 * fw:(f_k + 1) * per - FOX_H]], axis=0))
            shards += [g_main_t[k * per - FOX_H:(k + 1) * per - FOX_H] for k in range(f_k + 1, N_DEV)]
            grads = [jnp.stack(shards)]
        g8 = [g if g.ndim == 3 else g.reshape(N_DEV, -1, g.shape[1]) for g in grads]
        lands = [lax.empty((N_CHIP,) + g.shape[1:], g.dtype) for g in g8]
        pairing[group] = _split_start(g8, lands, chip, _pair_plan, N_CHIP, "reduce_" + group + "_pair_start")
        return pairing[group][4]

    def paired(group, after):
        send, recv, g8, lands, _ = pairing.pop(group)
        g8, from_sibling = _split_wait(send, recv, g8, lands, after, _pair_plan, "reduce_" + group + "_pair_wait")
        p4 = [_pair_sum(g, r, core, "reduce_pair_sum_" + n) for g, r, n in zip(g8, from_sibling, reduce_groups[group])]
        lands = [lax.empty(p.shape, p.dtype) for p in p4]
        reducing[group] = _split_start(p4, lands, chip, _chip_plan, 3, "reduce_" + group + "_start")
        return reducing[group][4]

    def finish(group, after):
        send, recv, p4, lands, _ = reducing.pop(group)
        p4, lands = _split_wait(send, recv, p4, lands, after, _chip_plan, "reduce_" + group + "_wait")
        return {n: tuple(unshard(r, n) for r in _reduce_adamw(p, l, chip, shard(w, n), shard(m, n), shard(v, n), "adamw_" + n))
                for n, p, l in zip(reduce_groups[group], p4, lands)}

    sq, grad_x, gs = _local_step(x[0], mem[0], loss_target[0], sp, W, arrived, want, done, paired)

    vectors = tuple(n for n in SMALL if n not in ("w_ra", "w_ri"))
    mine = [_pack_rows([[gs[n] for n in vectors] + [sq[0:1]]], "pack_small_grads")[0], gs["w_ra"].reshape(-1, HEAD),
            gs["w_ri"].reshape(-1, HEAD), gs["conv_w"]]
    lands = [land(a, "small_grads_%d" % k) for k, a in enumerate(mine)]
    s_send, s_recv, s_src, s_land, tok = _split_start(mine, lands, grad_x, _gather_plan, 4, "gather_small_start")
    out = finish("ffn", tok)
    out.update(finish("attn", tok))
    updated = [r for n in reduce_groups["ffn"] + reduce_groups["attn"] for r in out[n]]
    _, s_land = _split_wait(s_send, s_recv, s_src, s_land, updated, _gather_plan, "gather_small_wait")
    all_vec, all_ra, all_ri, all_conv = _gather_forward(s_land, "gather_small_forward")

    nothing = jnp.zeros((1, HEAD), F32)
    state = _pack_rows([[d[n] for n in vectors] + [nothing] for d in (w, m, v)], "pack_small_state")
    upd = _sum_adamw(all_vec, *state, "adamw_small_vectors")
    per_vector = _unpack_rows(upd, [w[n] for n in vectors] + [nothing], "unpack_small")
    for r, n in enumerate(vectors):
        out[n] = tuple(per_vector[k][r] for k in range(4))
    loss = 0.5 * per_vector[0][len(vectors)][0, 0] / D
    for n, parts in (("w_ra", all_ra), ("w_ri", all_ri)):
        res = _sum_adamw(parts, *[d[n].reshape(-1, HEAD) for d in (w, m, v)], "adamw_" + n)
        out[n] = tuple(r.reshape(w[n].shape) for r in res)
    out["conv_w"] = tuple(r[None] for r in _sum_adamw(all_conv, conv_w[0], m_conv_w[0], v_conv_w[0], "adamw_conv_w", col=dev))
    out.update(finish("w_in", upd[1]))

    return (loss, grad_x[None], *[out[n][0] for n in ORDER], *[out[n][1] for n in ORDER],
            *[out[n][2] for n in ORDER], *[out[n][3] for n in ORDER])
```

```python
import functools
import math

import jax
import jax.numpy as jnp
from jax import lax
from jax.experimental import pallas as pl
from jax.experimental.pallas import tpu as pltpu

F32 = jnp.float32
BF16 = jnp.bfloat16
MESH = pl.DeviceIdType.MESH
N_DEV = 8
N_CHIP = 4

HEAD = 128
FOX_H = 8
XATT_H = 4
LRU_NB = 8
CONV_K = 4
LRU_C = 8.0
RMS_EPS = 1e-6
ATT_T = 1024
CUM_T = 256
ROW_T = 256
NEG = -1e30
V7X_VMEM_LIMIT = 48 * 1024 * 1024

ADAM_LR = 0.001
ADAM_B1 = 0.9
ADAM_B2 = 0.999
ADAM_EPS = 1e-08
ADAM_WD = 0.01
ADAM_STEP = 10

NT = (((1,), (1,)), ((), ()))
TN = (((0,), (0,)), ((), ()))
NN = (((1,), (0,)), ((), ()))

ANY = pl.BlockSpec(memory_space=pl.ANY)


def _behind(after):
    return ([], []) if after is None else ([ANY], [after])


def _params(*sem):
    return pltpu.CompilerParams(dimension_semantics=sem or None, vmem_limit_bytes=V7X_VMEM_LIMIT)


def _dot(a, b, dn=NN):
    return lax.dot_general(a, b, dn, preferred_element_type=F32)


def _rms(x, g):
    r = lax.rsqrt(jnp.mean(x * x, axis=-1, keepdims=True) + RMS_EPS)
    return x * r * g


def _rms_grad(x, g, dy):
    r = lax.rsqrt(jnp.mean(x * x, axis=-1, keepdims=True) + RMS_EPS)
    xh = x * r
    dxh = dy * g
    dx = r * (dxh - xh * jnp.mean(dxh * xh, axis=-1, keepdims=True))
    return dx, jnp.sum(dy * xh, axis=0, keepdims=True)


def _gelu(x):
    k = math.sqrt(2.0 / math.pi)
    return 0.5 * x * (1.0 + jnp.tanh(k * (x + 0.044715 * (x * x * x))))


def _gelu_grad(x):
    k = math.sqrt(2.0 / math.pi)
    t = jnp.tanh(k * (x + 0.044715 * (x * x * x)))
    return 0.5 * (1.0 + t) + 0.5 * x * (1.0 - t * t) * k * (1.0 + 3.0 * 0.044715 * x * x)


def _one_minus_exp(z):
    series = -(z + 0.5 * z * z + (1.0 / 6.0) * z * z * z)
    return jnp.where(z > -1e-3, series, 1.0 - jnp.exp(z))


def _row_tile(rows, cap=ROW_T, mult=16):
    t = min(rows, cap)
    while rows % t or (t % mult and t != rows):
        t -= 1
    return t


def _mat_dims(z):
    return (z.shape[-2], z.shape[-1] * (z.shape[0] if z.ndim == 3 else 1))


def _mat_spec(arr, rblk, cblk, rsel, csel):
    if arr.ndim == 2:
        return pl.BlockSpec((rblk, cblk), lambda i, j, k: ((i, j, k)[rsel], (i, j, k)[csel]))
    nw = arr.shape[2]
    assert nw % cblk == 0, (arr.shape, cblk)
    per = nw // cblk
    return pl.BlockSpec((None, rblk, cblk),
                        lambda i, j, k: ((i, j, k)[csel] // per, (i, j, k)[rsel], (i, j, k)[csel] % per))


def _mm(a, b, mode, out_dtype, name, add=None, out_blocks=None, tm=1024, tn=1024, tk=None, after=None):
    ar, ac = _mat_dims(a)
    br, bc = _mat_dims(b)
    if mode == "nn":
        M, K, N = ar, ac, bc
        assert br == K
    elif mode == "nt":
        M, K, N = ar, ac, br
        assert bc == K
    else:
        M, K, N = ac, ar, bc
        assert br == K
    tm, tn = min(tm, M), min(tn, N)
    tk = K if tk is None or mode == "tn" else min(tk, K)
    assert M % tm == 0 and N % tn == 0 and K % tk == 0, (name, M, N, K, tm, tn, tk)
    nk = K // tk
    nj = N // tn
    if mode == "nn":
        specs = [_mat_spec(a, tm, tk, 0, 2), _mat_spec(b, tk, tn, 2, 1)]
        dn = NN
    elif mode == "nt":
        specs = [_mat_spec(a, tm, tk, 0, 2), _mat_spec(b, tn, tk, 1, 2)]
        dn = NT
    else:
        specs = [_mat_spec(a, tk, tm, 2, 0), _mat_spec(b, tk, tn, 2, 1)]
        dn = TN
    args = [a, b]
    if add is not None:
        specs.append(pl.BlockSpec((tm, tn), lambda i, j, k: (i, j)))
        args.append(add)
    specs += _behind(after)[0]
    args += _behind(after)[1]
    n_in = len(args)
    if out_blocks is None:
        out_shape = jax.ShapeDtypeStruct((M, N), out_dtype)
        out_spec = pl.BlockSpec((tm, tn), lambda i, j, k: (i, j))
    else:
        nb = out_blocks
        nw = N // nb
        assert nw % tn == 0
        per = nw // tn
        out_shape = jax.ShapeDtypeStruct((nb, M, nw), out_dtype)
        out_spec = pl.BlockSpec((None, tm, tn), lambda i, j, k: (j // per, i, j % per))
    keep_t = mode == "tn" and nj > 1
    scratch = []
    if nk > 1:
        scratch.append(pltpu.VMEM((tm, tn), F32))
    if keep_t:
        scratch.append(pltpu.VMEM((tm, tk), a.dtype))

    def body(*refs):
        a_ref, b_ref = refs[0], refs[1]
        add_ref = refs[2] if add is not None else None
        o_ref = refs[n_in]

        def finish(r):
            if add_ref is not None:
                r = r + add_ref[...]
            o_ref[...] = r.astype(out_dtype)

        if keep_t:
            at_ref = refs[-1]

            @pl.when(pl.program_id(1) == 0)
            def _():
                at_ref[...] = a_ref[...].T

            finish(_dot(at_ref[...], b_ref[...], NN))
        elif nk == 1:
            finish(_dot(a_ref[...], b_ref[...], dn))
        else:
            acc_ref = refs[n_in + 1]
            k = pl.program_id(2)

            @pl.when(k == 0)
            def _():
                acc_ref[...] = jnp.zeros_like(acc_ref)

            acc_ref[...] += _dot(a_ref[...], b_ref[...], dn)

            @pl.when(k == nk - 1)
            def _():
                finish(acc_ref[...])

    return pl.pallas_call(
        body, name=name, grid=(M // tm, nj, nk), in_specs=specs, out_specs=out_spec, out_shape=out_shape,
        scratch_shapes=scratch,
        compiler_params=_params("parallel", "arbitrary" if keep_t else "parallel", "arbitrary"),
    )(*args)


def _rms_fwd(xs, gs, name, after=None):
    n = len(xs)
    rows = xs[0].shape[0]
    widths = [x.shape[1] for x in xs]
    tr = _row_tile(rows)
    more_specs, more = _behind(after)

    def body(*refs):
        o_ref = refs[2 * n + len(more)]
        off = 0
        for t in range(n):
            o_ref[:, off:off + widths[t]] = _rms(refs[t][...], refs[n + t][...]).astype(BF16)
            off += widths[t]

    return pl.pallas_call(
        body, name=name, grid=(rows // tr,),
        in_specs=[pl.BlockSpec((tr, w), lambda i: (i, 0)) for w in widths]
        + [pl.BlockSpec((1, w), lambda i: (0, 0)) for w in widths] + more_specs,
        out_specs=pl.BlockSpec((tr, sum(widths)), lambda i: (i, 0)),
        out_shape=jax.ShapeDtypeStruct((rows, sum(widths)), BF16),
        compiler_params=_params("parallel"),
    )(*xs, *gs, *more)


def _rms_bwd(x, g, dy, name, dy_col=0, res=None, want_dx=True, after=None):
    rows, w = x.shape
    tr = _row_tile(rows)
    has_res = res is not None
    more_specs, more = _behind(after)

    def body(*refs):
        x_ref, g_ref, dy_ref = refs[:3]
        res_ref = refs[3] if has_res else None
        outs = refs[3 + has_res + len(more):]
        dg_ref = outs[-1]
        dx, dg = _rms_grad(x_ref[...], g_ref[...], dy_ref[...])

        @pl.when(pl.program_id(0) == 0)
        def _():
            dg_ref[...] = jnp.zeros_like(dg_ref)

        dg_ref[...] += dg
        if want_dx:
            if has_res:
                dx = dx + res_ref[...]
            outs[0][...] = dx
            outs[1][...] = dx.astype(BF16)

    row_spec = pl.BlockSpec((tr, w), lambda i: (i, 0))
    in_specs = [row_spec, pl.BlockSpec((1, w), lambda i: (0, 0)), pl.BlockSpec((tr, w), lambda i: (i, dy_col))]
    args = [x, g, dy]
    if has_res:
        in_specs.append(row_spec)
        args.append(res)
    in_specs += more_specs
    args += more
    out_specs, out_shape = [], []
    if want_dx:
        out_specs += [row_spec, row_spec]
        out_shape += [jax.ShapeDtypeStruct((rows, w), F32), jax.ShapeDtypeStruct((rows, w), BF16)]
    out_specs.append(pl.BlockSpec((1, w), lambda i: (0, 0)))
    out_shape.append(jax.ShapeDtypeStruct((1, w), F32))
    return pl.pallas_call(
        body, name=name, grid=(rows // tr,), in_specs=in_specs, out_specs=out_specs, out_shape=out_shape,
        compiler_params=_params("arbitrary"),
    )(*args)


def _loss_head(y, target, name):
    rows, w = y.shape
    tr = _row_tile(rows)

    def body(y_ref, t_ref, dy_ref, dyb_ref, acc_ref):
        e = y_ref[...] - t_ref[...]

        @pl.when(pl.program_id(0) == 0)
        def _():
            acc_ref[...] = jnp.zeros_like(acc_ref)

        acc_ref[...] += jnp.sum(e * e)
        dy = e * (1.0 / w)
        dy_ref[...] = dy
        dyb_ref[...] = dy.astype(BF16)

    row_spec = pl.BlockSpec((tr, w), lambda i: (i, 0))
    return pl.pallas_call(
        body, name=name, grid=(rows // tr,), in_specs=[row_spec, row_spec],
        out_specs=[row_spec, row_spec, pl.BlockSpec((8, 128), lambda i: (0, 0))],
        out_shape=[jax.ShapeDtypeStruct((rows, w), F32), jax.ShapeDtypeStruct((rows, w), BF16),
                   jax.ShapeDtypeStruct((8, 128), F32)],
        compiler_params=_params("arbitrary"),
    )(y, target)


def _qk_fwd(proj, g_q, g_k, name, after=None):
    rows = proj.shape[0]
    w = FOX_H * HEAD
    tr = _row_tile(rows)
    more_specs, more = _behind(after)

    def body(q_ref, k_ref, v_ref, gq_ref, gk_ref, *rest):
        qn_ref, kn_ref, vb_ref = rest[len(more):]
        for h in range(FOX_H):
            sl = slice(h * HEAD, (h + 1) * HEAD)
            qn_ref[:, sl] = _rms(q_ref[:, sl], gq_ref[...]).astype(BF16)
            kn_ref[:, sl] = _rms(k_ref[:, sl], gk_ref[...]).astype(BF16)
        vb_ref[...] = v_ref[...].astype(BF16)

    gain = pl.BlockSpec((1, HEAD), lambda i: (0, 0))
    out = pl.BlockSpec((tr, w), lambda i: (i, 0))
    return pl.pallas_call(
        body, name=name, grid=(rows // tr,),
        in_specs=[pl.BlockSpec((tr, w), lambda i: (i, 0)), pl.BlockSpec((tr, w), lambda i: (i, 1)),
                  pl.BlockSpec((tr, w), lambda i: (i, 2)), gain, gain] + more_specs,
        out_specs=[out, out, out], out_shape=[jax.ShapeDtypeStruct((rows, w), BF16)] * 3,
        compiler_params=_params("parallel"),
    )(proj, proj, proj, g_q, g_k, *more)


def _qk_bwd(proj, g_q, g_k, dqn, dkn, name):
    rows = proj.shape[0]
    w = FOX_H * HEAD
    tr = _row_tile(rows)

    def body(q_ref, k_ref, gq_ref, gk_ref, dqn_ref, dkn_ref, dq_ref, dk_ref, dgq_ref, dgk_ref):
        @pl.when(pl.program_id(0) == 0)
        def _():
            dgq_ref[...] = jnp.zeros_like(dgq_ref)
            dgk_ref[...] = jnp.zeros_like(dgk_ref)

        for h in range(FOX_H):
            sl = slice(h * HEAD, (h + 1) * HEAD)
            dq, dgq = _rms_grad(q_ref[:, sl], gq_ref[...], dqn_ref[:, sl])
            dk, dgk = _rms_grad(k_ref[:, sl], gk_ref[...], dkn_ref[:, sl])
            dq_ref[:, sl] = dq.astype(BF16)
            dk_ref[:, sl] = dk.astype(BF16)
            dgq_ref[...] += dgq
            dgk_ref[...] += dgk

    gain = pl.BlockSpec((1, HEAD), lambda i: (0, 0))
    row = pl.BlockSpec((tr, w), lambda i: (i, 0))
    return pl.pallas_call(
        body, name=name, grid=(rows // tr,),
        in_specs=[row, pl.BlockSpec((tr, w), lambda i: (i, 1)), gain, gain, row, row],
        out_specs=[row, row, gain, gain],
        out_shape=[jax.ShapeDtypeStruct((rows, w), BF16)] * 2 + [jax.ShapeDtypeStruct((1, HEAD), F32)] * 2,
        compiler_params=_params("arbitrary"),
    )(proj, proj, g_q, g_k, dqn, dkn)


def _fgate_fwd(f_logit, b_pad, name):
    S = f_logit.shape[0]
    T = min(CUM_T, S)
    nb = S // T
    RT = min(ATT_T, S)
    nbr = S // RT

    def body(f_ref, b_ref, col_ref, row_ref, c_scr):
        tri = (lax.broadcasted_iota(jnp.int32, (T, T), 0) >= lax.broadcasted_iota(jnp.int32, (T, T), 1)).astype(F32)
        carry = jnp.zeros((1, HEAD), F32)
        for blk in range(nb):
            z = f_ref[blk * T:(blk + 1) * T, :] + b_ref[...]
            lf = jnp.minimum(z, 0.0) - jnp.log1p(jnp.exp(-jnp.abs(z)))
            cb = jnp.dot(tri, lf, precision=lax.Precision.HIGHEST, preferred_element_type=F32) + carry
            c_scr[blk * T:(blk + 1) * T, :] = cb
            carry = cb[T - 1:T, :]
        c = c_scr[...]
        lane = lax.broadcasted_iota(jnp.int32, c.shape, 1)
        ct = c.T
        for h in range(FOX_H):
            col_ref[h] = jnp.sum(jnp.where(lane == h, c, 0.0), axis=1, keepdims=True)
            for jb in range(nbr):
                row_ref[h, jb] = ct[h:h + 1, jb * RT:(jb + 1) * RT]

    return pl.pallas_call(
        body, name=name,
        out_shape=[jax.ShapeDtypeStruct((FOX_H, S, 1), F32), jax.ShapeDtypeStruct((FOX_H, nbr, 1, RT), F32)],
        scratch_shapes=[pltpu.VMEM((S, HEAD), F32)], compiler_params=_params(),
    )(f_logit, b_pad)


def _fgate_bwd(f_logit, b_pad, dc_col, dc_row, name):
    S = f_logit.shape[0]
    T = min(CUM_T, S)
    nb = S // T
    RT = min(ATT_T, S)

    def body(f_ref, b_ref, dcol_ref, drow_ref, df_ref, db_ref, dc_scr, rt_scr):
        lane = lax.broadcasted_iota(jnp.int32, (S, HEAD), 1)
        sub = lax.broadcasted_iota(jnp.int32, (HEAD, RT), 0)
        dc = jnp.zeros((S, HEAD), F32)
        for h in range(FOX_H):
            dc = jnp.where(lane == h, dcol_ref[h], dc)
        for jb in range(S // RT):
            rt = jnp.zeros((HEAD, RT), F32)
            for h in range(FOX_H):
                rt = jnp.where(sub == h, drow_ref[h, jb], rt)
            rt_scr[jb * RT:(jb + 1) * RT, :] = rt.T
        dc_scr[...] = dc - rt_scr[...]
        tri = (lax.broadcasted_iota(jnp.int32, (T, T), 0) <= lax.broadcasted_iota(jnp.int32, (T, T), 1)).astype(F32)
        carry = jnp.zeros((1, HEAD), F32)
        db = jnp.zeros((1, HEAD), F32)
        for blk in reversed(range(nb)):
            rows = slice(blk * T, (blk + 1) * T)
            dlf = jnp.dot(tri, dc_scr[rows, :], precision=lax.Precision.HIGHEST, preferred_element_type=F32) + carry
            carry = dlf[0:1, :]
            z = f_ref[rows, :] + b_ref[...]
            df = dlf * jax.nn.sigmoid(-z)
            df_ref[rows, :] = df.astype(BF16)
            db = db + jnp.sum(df, axis=0, keepdims=True)
        db_ref[...] = db

    return pl.pallas_call(
        body, name=name,
        out_shape=[jax.ShapeDtypeStruct((S, HEAD), BF16), jax.ShapeDtypeStruct((1, HEAD), F32)],
        scratch_shapes=[pltpu.VMEM((S, HEAD), F32), pltpu.VMEM((S, HEAD), F32)], compiler_params=_params(),
    )(f_logit, b_pad, dc_col, dc_row)


def _fox_fwd(qn, kn, vb, c_col, c_row, name, after=None):
    S = qn.shape[0]
    T = min(ATT_T, S)
    nb = S // T
    scale = 1.0 / math.sqrt(HEAD)
    more_specs, more = _behind(after)

    def body(q_ref, k_ref, v_ref, cc_ref, cr_ref, *rest):
        o_ref, lse_ref = rest[len(more):]
        i = pl.program_id(1)
        q = q_ref[...]
        cc = cc_ref[...]

        def step(j, carry, diagonal=False):
            m, l, acc = carry
            sl = pl.ds(pl.multiple_of(j * T, T), T)
            s = _dot(q, k_ref[sl, :], NT) * scale + cc - cr_ref[j]
            if diagonal:
                s = jnp.where(lax.broadcasted_iota(jnp.int32, (T, T), 0) >= lax.broadcasted_iota(jnp.int32, (T, T), 1), s, NEG)
            m2 = jnp.maximum(m, jnp.max(s, axis=1, keepdims=True))
            p = jnp.exp(s - m2)
            al = jnp.exp(m - m2)
            return m2, al * l + jnp.sum(p, axis=1, keepdims=True), al * acc + _dot(p.astype(BF16), v_ref[sl, :])

        init = (jnp.full((T, 1), NEG, F32), jnp.zeros((T, 1), F32), jnp.zeros((T, HEAD), F32))
        m, l, acc = step(i, lax.fori_loop(0, i, step, init), diagonal=True)
        o_ref[...] = acc / l
        lse_ref[...] = m + jnp.log(l)

    head = pl.BlockSpec((S, HEAD), lambda h, i: (0, h))
    return pl.pallas_call(
        body, name=name, grid=(FOX_H, nb),
        in_specs=[pl.BlockSpec((T, HEAD), lambda h, i: (i, h)), head, head,
                  pl.BlockSpec((None, T, 1), lambda h, i: (h, i, 0)),
                  pl.BlockSpec((None, nb, 1, T), lambda h, i: (h, 0, 0, 0))] + more_specs,
        out_specs=[pl.BlockSpec((T, HEAD), lambda h, i: (i, h)), pl.BlockSpec((None, T, 1), lambda h, i: (h, i, 0))],
        out_shape=[jax.ShapeDtypeStruct((S, FOX_H * HEAD), F32), jax.ShapeDtypeStruct((FOX_H, S, 1), F32)],
        compiler_params=_params("parallel", "parallel"),
    )(qn, kn, vb, c_col, c_row, *more)


def _fox_bwd(qn, kn, vb, c_col, c_row, lse, o, do, name):
    S = qn.shape[0]
    T = min(ATT_T, S)
    Th = T // 2
    nb = S // T
    scale = 1.0 / math.sqrt(HEAD)

    def body(q_ref, k_ref, v_ref, cc_ref, cr_ref, lse_ref, o_ref, do_ref, dq_ref, dk_ref, dv_ref, dcc_ref, dcr_ref):
        j = pl.program_id(1)

        @pl.when(j == 0)
        def _():
            dq_ref[...] = jnp.zeros_like(dq_ref)
            dcc_ref[...] = jnp.zeros_like(dcc_ref)

        k = k_ref[...]
        v = v_ref[...]
        cr = cr_ref[...]

        def tile(sl, kk, vv, crr, carry, masked):
            dk, dv, dcr = carry
            q = q_ref[sl, :]
            d_o = do_ref[sl, :]
            s = _dot(q, kk, NT) * scale + cc_ref[sl, :] - crr
            if masked:
                s = jnp.where(lax.broadcasted_iota(jnp.int32, s.shape, 0) >= lax.broadcasted_iota(jnp.int32, s.shape, 1), s, NEG)
            p = jnp.exp(s - lse_ref[sl, :])
            dob = d_o.astype(BF16)
            dp = _dot(dob, vv, NT)
            delta = jnp.sum(d_o * o_ref[sl, :], axis=1, keepdims=True)
            ds = p * (dp - delta)
            dsb = ds.astype(BF16)
            dq_ref[sl, :] += _dot(dsb, kk) * scale
            dcc_ref[sl, :] += jnp.sum(ds, axis=1, keepdims=True)
            return (dk + _dot(dsb, q, TN), dv + _dot(p.astype(BF16), dob, TN), dcr + jnp.sum(ds, axis=0, keepdims=True))

        def step(i, carry):
            return tile(pl.ds(pl.multiple_of(i * T, T), T), k, v, cr, carry, False)

        def rows(r0):
            return pl.ds(pl.multiple_of(j * T + r0, Th), Th)

        zero = (jnp.zeros((Th, HEAD), F32), jnp.zeros((Th, HEAD), F32), jnp.zeros((1, Th), F32))
        left = tile(rows(Th), k[:Th], v[:Th], cr[:, :Th], tile(rows(0), k[:Th], v[:Th], cr[:, :Th], zero, True), False)
        right = tile(rows(Th), k[Th:], v[Th:], cr[:, Th:], zero, True)
        diagonal = (jnp.concatenate([left[0], right[0]], axis=0), jnp.concatenate([left[1], right[1]], axis=0),
                    jnp.concatenate([left[2], right[2]], axis=1))
        dk, dv, dcr = lax.fori_loop(j + 1, nb, step, diagonal)
        dk_ref[...] = dk * scale
        dv_ref[...] = dv.astype(BF16)
        dcr_ref[...] = dcr

    head = pl.BlockSpec((S, HEAD), lambda h, j: (0, h))
    tile = pl.BlockSpec((T, HEAD), lambda h, j: (j, h))
    col = pl.BlockSpec((None, S, 1), lambda h, j: (h, 0, 0))
    row = pl.BlockSpec((None, None, 1, T), lambda h, j: (h, j, 0, 0))
    w = FOX_H * HEAD
    return pl.pallas_call(
        body, name=name, grid=(FOX_H, nb),
        in_specs=[head, tile, tile, col, row, col, head, head],
        out_specs=[head, tile, tile, col, row],
        out_shape=[jax.ShapeDtypeStruct((S, w), F32), jax.ShapeDtypeStruct((S, w), F32), jax.ShapeDtypeStruct((S, w), BF16),
                   jax.ShapeDtypeStruct((FOX_H, S, 1), F32), jax.ShapeDtypeStruct((FOX_H, nb, 1, T), F32)],
        compiler_params=_params("arbitrary", "arbitrary"),
    )(qn, kn, vb, c_col, c_row, lse, o, do)


LRU_FWD_BLOCKS = 2


def _block_dot(x, w, dn=NN):
    return jnp.concatenate([_dot(x[:, b * HEAD:(b + 1) * HEAD], w[b], dn) for b in range(w.shape[0])], axis=1)


def _lru_gates(uc, wra, bra, wri, bri, lam):
    ucb = uc.astype(BF16)
    r = jax.nn.sigmoid(_block_dot(ucb, wra.astype(BF16)) + bra)
    ig = jax.nn.sigmoid(_block_dot(ucb, wri.astype(BF16)) + bri)
    sp = jnp.maximum(-lam, 0.0) + jnp.log1p(jnp.exp(-jnp.abs(lam)))
    log_a = -LRU_C * r * sp
    a = jnp.exp(log_a)
    mult = jnp.sqrt(_one_minus_exp(2.0 * log_a))
    return r, ig, sp, a, mult


def _conv(pad_ref, cw, cb, S):
    uc = cb
    for j in range(CONV_K):
        uc = uc + cw[j:j + 1, :] * pad_ref[5 + j:5 + j + S, :]
    return uc


def _lru_specs(S, G):
    bw = G * HEAD
    u_col = 3 * FOX_H // G
    g_col = u_col + LRU_NB // G
    blk = pl.BlockSpec((S, bw), lambda n: (0, n))
    vec = pl.BlockSpec((1, bw), lambda n: (0, n))
    mat = pl.BlockSpec((G, HEAD, HEAD), lambda n: (n, 0, 0))
    return dict(
        u=pl.BlockSpec((S, bw), lambda n: (0, u_col + n)), gate=pl.BlockSpec((S, bw), lambda n: (0, g_col + n)),
        blk=blk, vec=vec, mat=mat, cw=pl.BlockSpec((CONV_K, bw), lambda n: (0, n)))


def _lru_fwd(proj, conv_w, conv_b, w_ra, b_ra, w_ri, b_ri, lam, name, after=None):
    S = proj.shape[0]
    G = LRU_FWD_BLOCKS
    bw = G * HEAD
    sp_ = _lru_specs(S, G)
    rows8 = S // 8
    more_specs, more = _behind(after)

    def body(u_ref, gt_ref, cw_ref, cb_ref, wra_ref, bra_ref, wri_ref, bri_ref, lam_ref, *rest):
        y_ref, h_ref, pad, a_scr, b_scr = rest[len(more):]
        pad[0:8, :] = jnp.zeros((8, bw), F32)
        pad[8:S + 8, :] = u_ref[...]
        uc = _conv(pad, cw_ref[...], cb_ref[...], S)
        r, ig, sp, a, mult = _lru_gates(uc, wra_ref[...], bra_ref[...], wri_ref[...], bri_ref[...], lam_ref[...])
        a_scr[...] = a
        b_scr[...] = mult * (ig * uc)
        sub = lax.broadcasted_iota(jnp.int32, (8, bw), 0)

        def step(t, carry):
            sl = pl.ds(pl.multiple_of(t * 8, 8), 8)
            A, B = a_scr[sl, :], b_scr[sl, :]
            for d in (1, 2, 4):
                m = sub >= d
                B = jnp.where(m, A * pltpu.roll(B, d, 0) + B, B)
                A = jnp.where(m, A * pltpu.roll(A, d, 0), A)
            h = A * carry + B
            h_ref[sl, :] = h
            return h[7:8, :]

        lax.fori_loop(0, rows8, step, jnp.zeros((1, bw), F32))
        y_ref[...] = h_ref[...] * _gelu(gt_ref[...])

    w = LRU_NB * HEAD
    return pl.pallas_call(
        body, name=name, grid=(LRU_NB // G,),
        in_specs=[sp_["u"], sp_["gate"], sp_["cw"], sp_["vec"], sp_["mat"], sp_["vec"], sp_["mat"], sp_["vec"], sp_["vec"]]
        + more_specs,
        out_specs=[sp_["blk"], sp_["blk"]],
        out_shape=[jax.ShapeDtypeStruct((S, w), F32)] * 2,
        scratch_shapes=[pltpu.VMEM((S + 8, bw), F32), pltpu.VMEM((S, bw), F32), pltpu.VMEM((S, bw), F32)],
        compiler_params=_params("parallel"),
    )(proj, proj, conv_w, conv_b, w_ra, b_ra, w_ri, b_ri, lam, *more)


def _lru_bwd(proj, h, dy, conv_w, conv_b, w_ra, b_ra, w_ri, b_ri, lam, name, after=None):
    S = proj.shape[0]
    G = 1
    bw = G * HEAD
    sp_ = _lru_specs(S, G)
    rows8 = S // 8
    more_specs, more = _behind(after)

    def body(u_ref, gt_ref, h_ref, dy_ref, cw_ref, cb_ref, wra_ref, bra_ref, wri_ref, bri_ref, lam_ref, *rest):
        (du_ref, dgt_ref, dcw_ref, dcb_ref, dwra_ref, dbra_ref, dwri_ref, dbri_ref, dlam_ref,
         pad, an_scr, d_scr, g_scr, hp_scr) = rest[len(more):]
        zero8 = jnp.zeros((8, bw), F32)
        pad[0:8, :] = zero8
        pad[8:S + 8, :] = u_ref[...]
        cw = cw_ref[...]
        uc = _conv(pad, cw, cb_ref[...], S)
        wra, wri, lam_v = wra_ref[...], wri_ref[...], lam_ref[...]
        r, ig, sp, a, mult = _lru_gates(uc, wra, bra_ref[...], wri, bri_ref[...], lam_v)
        gate = gt_ref[...]
        dy_v = dy_ref[...]
        hv = h_ref[...]
        dgt_ref[...] = (dy_v * hv * _gelu_grad(gate)).astype(BF16)
        d_scr[...] = dy_v * _gelu(gate)
        g_scr[0:S, :] = a
        g_scr[S:S + 8, :] = zero8
        an_scr[...] = g_scr[1:S + 1, :]
        sub = lax.broadcasted_iota(jnp.int32, (8, bw), 0)

        def step(t, carry):
            sl = pl.ds(pl.multiple_of((rows8 - 1 - t) * 8, 8), 8)
            A, D = an_scr[sl, :], d_scr[sl, :]
            for d in (1, 2, 4):
                m = sub + d <= 7
                D = jnp.where(m, A * pltpu.roll(D, 8 - d, 0) + D, D)
                A = jnp.where(m, A * pltpu.roll(A, 8 - d, 0), A)
            g = A * carry + D
            g_scr[sl, :] = g
            return g[0:1, :]

        lax.fori_loop(0, rows8, step, jnp.zeros((1, bw), F32))
        g = g_scr[0:S, :]
        hp_scr[0:8, :] = zero8
        hp_scr[8:S + 8, :] = hv
        da = g * hp_scr[7:S + 7, :]
        iu = ig * uc
        dmult = g * iu
        diu = g * mult
        dig = diu * uc
        duc = diu * ig
        dlog_a = da * a - dmult * (a * a) / mult
        dr = dlog_a * (-LRU_C * sp)
        dsp = jnp.sum(dlog_a * (-LRU_C * r), axis=0, keepdims=True)
        dlam_ref[...] = -dsp * jax.nn.sigmoid(-lam_v)
        dpr = dr * r * (1.0 - r)
        dpi = dig * ig * (1.0 - ig)
        dbra_ref[...] = jnp.sum(dpr, axis=0, keepdims=True)
        dbri_ref[...] = jnp.sum(dpi, axis=0, keepdims=True)
        ucb = uc.astype(BF16)
        dprb, dpib = dpr.astype(BF16), dpi.astype(BF16)
        for b in range(G):
            cols = slice(b * HEAD, (b + 1) * HEAD)
            dwra_ref[b] = _dot(ucb[:, cols], dprb[:, cols], TN).astype(BF16)
            dwri_ref[b] = _dot(ucb[:, cols], dpib[:, cols], TN).astype(BF16)
        duc = duc + _block_dot(dprb, wra.astype(BF16), NT) + _block_dot(dpib, wri.astype(BF16), NT)
        dcb_ref[...] = jnp.sum(duc, axis=0, keepdims=True)
        for j in range(CONV_K):
            dcw_ref[j:j + 1, :] = jnp.sum(duc * pad[5 + j:5 + j + S, :], axis=0, keepdims=True)
        g_scr[0:S, :] = duc
        g_scr[S:S + 8, :] = zero8
        du = jnp.zeros((S, bw), F32)
        for j in range(CONV_K):
            du = du + cw[j:j + 1, :] * g_scr[3 - j:3 - j + S, :]
        du_ref[...] = du.astype(BF16)

    w = LRU_NB * HEAD
    bf = jax.ShapeDtypeStruct((S, w), BF16)
    vec = jax.ShapeDtypeStruct((1, w), F32)
    mat = jax.ShapeDtypeStruct((LRU_NB, HEAD, HEAD), BF16)
    return pl.pallas_call(
        body, name=name, grid=(LRU_NB // G,),
        in_specs=[sp_["u"], sp_["gate"], sp_["blk"], sp_["blk"], sp_["cw"], sp_["vec"], sp_["mat"], sp_["vec"],
                  sp_["mat"], sp_["vec"], sp_["vec"]] + more_specs,
        out_specs=[sp_["blk"], sp_["blk"], sp_["cw"], sp_["vec"], sp_["mat"], sp_["vec"], sp_["mat"], sp_["vec"], sp_["vec"]],
        out_shape=[bf, bf, jax.ShapeDtypeStruct((CONV_K, w), F32), vec, mat, vec, mat, vec, vec],
        scratch_shapes=[pltpu.VMEM((S + 8, bw), F32), pltpu.VMEM((S, bw), F32), pltpu.VMEM((S, bw), F32),
                        pltpu.VMEM((S + 8, bw), F32), pltpu.VMEM((S + 8, bw), F32)],
        compiler_params=_params("parallel"),
    )(proj, proj, h, dy, conv_w, conv_b, w_ra, b_ra, w_ri, b_ri, lam, *more)


def _xattn_fwd(cq, ckv, g_cq, g_ck, name):
    S, w = cq.shape
    M = ckv.shape[0]
    tr = _row_tile(S)
    scale = 1.0 / math.sqrt(HEAD)

    def body(cq_ref, ckv_ref, gq_ref, gk_ref, o_ref):
        for h in range(XATT_H):
            sl = slice(h * HEAD, (h + 1) * HEAD)
            qn = _rms(cq_ref[:, sl], gq_ref[...]).astype(BF16)
            kn = _rms(ckv_ref[:, sl], gk_ref[...]).astype(BF16)
            v = ckv_ref[:, w + h * HEAD:w + (h + 1) * HEAD].astype(BF16)
            s = _dot(qn, kn, NT) * scale
            p = jnp.exp(s - jnp.max(s, axis=1, keepdims=True))
            p = p / jnp.sum(p, axis=1, keepdims=True)
            o_ref[:, sl] = _dot(p.astype(BF16), v).astype(BF16)

    gain = pl.BlockSpec((1, HEAD), lambda i: (0, 0))
    return pl.pallas_call(
        body, name=name, grid=(S // tr,),
        in_specs=[pl.BlockSpec((tr, w), lambda i: (i, 0)), pl.BlockSpec((M, 2 * w), lambda i: (0, 0)), gain, gain],
        out_specs=pl.BlockSpec((tr, w), lambda i: (i, 0)), out_shape=jax.ShapeDtypeStruct((S, w), BF16),
        compiler_params=_params("parallel"),
    )(cq, ckv, g_cq, g_ck)


def _xattn_bwd(cq, ckv, g_cq, g_ck, do, name):
    S, w = cq.shape
    M = ckv.shape[0]
    tr = _row_tile(S)
    nsteps = S // tr
    scale = 1.0 / math.sqrt(HEAD)

    def body(cq_ref, ckv_ref, gq_ref, gk_ref, do_ref, dcq_ref, dckv_ref, dgq_ref, dgk_ref, dkn_scr, dv_scr):
        step = pl.program_id(0)

        @pl.when(step == 0)
        def _():
            dkn_scr[...] = jnp.zeros_like(dkn_scr)
            dv_scr[...] = jnp.zeros_like(dv_scr)
            dgq_ref[...] = jnp.zeros_like(dgq_ref)

        for h in range(XATT_H):
            sl = slice(h * HEAD, (h + 1) * HEAD)
            q_raw = cq_ref[:, sl]
            qn = _rms(q_raw, gq_ref[...]).astype(BF16)
            kn = _rms(ckv_ref[:, sl], gk_ref[...]).astype(BF16)
            v = ckv_ref[:, w + h * HEAD:w + (h + 1) * HEAD].astype(BF16)
            s = _dot(qn, kn, NT) * scale
            p = jnp.exp(s - jnp.max(s, axis=1, keepdims=True))
            p = p / jnp.sum(p, axis=1, keepdims=True)
            dob = do_ref[:, sl].astype(BF16)
            dp = _dot(dob, v, NT)
            ds = p * (dp - jnp.sum(p * dp, axis=1, keepdims=True)) * scale
            dsb = ds.astype(BF16)
            dv_scr[:, sl] += _dot(p.astype(BF16), dob, TN)
            dkn_scr[:, sl] += _dot(dsb, qn, TN)
            dq, dgq = _rms_grad(q_raw, gq_ref[...], _dot(dsb, kn))
            dcq_ref[:, sl] = dq.astype(BF16)
            dgq_ref[...] += dgq

        @pl.when(step == nsteps - 1)
        def _():
            dgk = jnp.zeros((1, HEAD), F32)
            for h in range(XATT_H):
                sl = slice(h * HEAD, (h + 1) * HEAD)
                dk, dgk_h = _rms_grad(ckv_ref[:, sl], gk_ref[...], dkn_scr[:, sl])
                dckv_ref[:, sl] = dk.astype(BF16)
                dgk = dgk + dgk_h
            dckv_ref[:, w:2 * w] = dv_scr[...].astype(BF16)
            dgk_ref[...] = dgk

    gain = pl.BlockSpec((1, HEAD), lambda i: (0, 0))
    row = pl.BlockSpec((tr, w), lambda i: (i, 0))
    mem = pl.BlockSpec((M, 2 * w), lambda i: (0, 0))
    return pl.pallas_call(
        body, name=name, grid=(nsteps,), in_specs=[row, mem, gain, gain, row], out_specs=[row, mem, gain, gain],
        out_shape=[jax.ShapeDtypeStruct((S, w), BF16), jax.ShapeDtypeStruct((M, 2 * w), BF16),
                   jax.ShapeDtypeStruct((1, HEAD), F32), jax.ShapeDtypeStruct((1, HEAD), F32)],
        scratch_shapes=[pltpu.VMEM((M, w), F32), pltpu.VMEM((M, w), F32)],
        compiler_params=_params("arbitrary"),
    )(cq, ckv, g_cq, g_ck, do)


def _ffn_up(h, w_gu, name, tm=512):
    S, D = h.shape
    nb, _, bw = w_gu.shape
    half = nb // 2
    tm = min(tm, S)

    def body(h_ref, wg_ref, wu_ref, g_ref, u_ref, act_ref):
        hv = h_ref[...]
        g = _dot(hv, wg_ref[...])
        u = _dot(hv, wu_ref[...])
        g_ref[...] = g
        u_ref[...] = u
        act_ref[...] = (g * jax.nn.sigmoid(g) * u).astype(BF16)

    blk = pl.BlockSpec((tm, bw), lambda j, i: (i, j))
    return pl.pallas_call(
        body, name=name, grid=(half, S // tm),
        in_specs=[pl.BlockSpec((tm, D), lambda j, i: (i, 0)), pl.BlockSpec((None, D, bw), lambda j, i: (j, 0, 0)),
                  pl.BlockSpec((None, D, bw), lambda j, i: (j + half, 0, 0))],
        out_specs=[blk, blk, blk],
        out_shape=[jax.ShapeDtypeStruct((S, half * bw), F32)] * 2 + [jax.ShapeDtypeStruct((S, half * bw), BF16)],
        compiler_params=_params("parallel", "parallel"),
    )(h, w_gu, w_gu)


def _ffn_down_dx(dy, w_down, g, u, name, tr=1024, tc=512):
    S, f = g.shape
    w2 = 2 * f
    tr, tc = min(tr, S), min(tc, f)
    assert S % tr == 0 and f % tc == 0
    n_ct = f // tc
    steps = (S // tr) * n_ct

    def body(dy_ref, w_ref, g_ref, u_ref, o_hbm, buf, sems):
        i, j = pl.program_id(0), pl.program_id(1)
        step = i * n_ct + j
        slot = step % 2

        def put(half):
            cols = pl.ds(pl.multiple_of(half * f + j * tc, 128), tc)
            return pltpu.make_async_copy(buf.at[slot, half], o_hbm.at[pl.ds(pl.multiple_of(i * tr, 8), tr), cols],
                                         sems.at[slot, half])

        @pl.when(step >= 2)
        def _():
            put(0).wait()
            put(1).wait()

        dy_v = dy_ref[...]
        for c0 in range(0, tc, 256):
            cs = slice(c0, c0 + 256)
            g = g_ref[:, cs]
            sg = jax.nn.sigmoid(g)
            da = _dot(dy_v, w_ref[cs, :], NT)
            buf[slot, 0, :, cs] = (da * u_ref[:, cs] * (sg * (1.0 + g * (1.0 - sg)))).astype(BF16)
            buf[slot, 1, :, cs] = (da * (g * sg)).astype(BF16)
        put(0).start()
        put(1).start()

        @pl.when(step == steps - 1)
        def _():
            put(0).wait()
            put(1).wait()
            if steps > 1:
                for half in (0, 1):
                    pltpu.make_async_copy(buf.at[1 - slot, half], o_hbm.at[pl.ds(0, tr), pl.ds(0, tc)],
                                          sems.at[1 - slot, half]).wait()

    D = dy.shape[1]
    return pl.pallas_call(
        body, name=name, grid=(S // tr, n_ct),
        in_specs=[pl.BlockSpec((tr, D), lambda i, j: (i, 0)), pl.BlockSpec((tc, D), lambda i, j: (j, 0)),
                  pl.BlockSpec((tr, tc), lambda i, j: (i, j)), pl.BlockSpec((tr, tc), lambda i, j: (i, j))],
        out_specs=ANY, out_shape=jax.ShapeDtypeStruct((S, w2), BF16),
        scratch_shapes=[pltpu.VMEM((2, 2, tr, tc), BF16), pltpu.SemaphoreType.DMA((2, 2))],
        compiler_params=_params("arbitrary", "arbitrary"),
    )(dy, w_down, g, u)


def _adamw_math(w, g, m, v):
    m = ADAM_B1 * m + (1.0 - ADAM_B1) * g
    v = ADAM_B2 * v + (1.0 - ADAM_B2) * (g * g)
    m_hat = m / (1.0 - ADAM_B1 ** ADAM_STEP)
    v_hat = v / (1.0 - ADAM_B2 ** ADAM_STEP)
    delta = -ADAM_LR * (m_hat / (jnp.sqrt(v_hat) + ADAM_EPS) + ADAM_WD * w)
    return delta, m, v


def _tile2(R, C, elems):
    if R % 16 == 0:
        return _row_tile(R, cap=max(16, elems // C // 16 * 16)), C
    tc = 128
    while C % (2 * tc) == 0 and R * 2 * tc <= elems:
        tc *= 2
    return R, tc


def _reduce_adamw(own, land, chip, w, m, v, name):
    _, R, C = own.shape
    tr, tc = _tile2(R, C, 1 << (19 if w.ndim == 3 else 18))

    def body(s_ref, o_ref, l1_ref, l2_ref, l3_ref, w_ref, m_ref, v_ref, g_ref, d_ref, nm_ref, nv_ref):
        g = o_ref[...].astype(F32) + l1_ref[...].astype(F32) + l2_ref[...].astype(F32) + l3_ref[...].astype(F32)
        g_ref[...] = g
        d_ref[...], nm_ref[...], nv_ref[...] = _adamw_math(w_ref[...], g, m_ref[...], v_ref[...])

    def part(flip):
        return pl.BlockSpec((None, tr, tc), lambda i, j, s: (s[0] ^ flip, i, j))

    if w.ndim == 3:
        blk = pl.BlockSpec((tr, None, tc), lambda i, j, s: (i, 0, j))
    else:
        blk = pl.BlockSpec((tr, tc), lambda i, j, s: (i, j))
    sds = jax.ShapeDtypeStruct(w.shape, F32)
    return pl.pallas_call(
        body, name=name,
        grid_spec=pltpu.PrefetchScalarGridSpec(
            num_scalar_prefetch=1, grid=(R // tr, C // tc),
            in_specs=[part(0), part(1), part(2), part(3), blk, blk, blk], out_specs=[blk] * 4),
        out_shape=[sds] * 4, compiler_params=_params("parallel", "parallel"),
    )(chip, own, land, land, land, w, m, v)


def _pair_sum(g8, recv, core, name):
    _, R, C = g8.shape
    tr, tc = _tile2(R, C, 1 << 21)

    def body(c_ref, a_ref, b_ref, o_ref):
        o_ref[...] = (a_ref[...].astype(F32) + b_ref[...].astype(F32)).astype(BF16)

    return pl.pallas_call(
        body, name=name,
        grid_spec=pltpu.PrefetchScalarGridSpec(
            num_scalar_prefetch=1, grid=(N_CHIP, R // tr, C // tc),
            in_specs=[pl.BlockSpec((None, tr, tc), lambda q, i, j, c: (2 * q + c[0], i, j)),
                      pl.BlockSpec((None, tr, tc), lambda q, i, j, c: (q, i, j))],
            out_specs=pl.BlockSpec((None, tr, tc), lambda q, i, j, c: (q, i, j))),
        out_shape=jax.ShapeDtypeStruct((N_CHIP, R, C), BF16),
        compiler_params=_params("parallel", "parallel", "parallel"),
    )(core, g8, recv)


def _place():
    return lax.axis_index("x"), lax.axis_index("y"), lax.axis_index("c")


def _land_with_own(shard, dev, name):
    R, C = shard.shape
    tr, tc = _tile2(R, C, 1 << 19)

    def body(d_ref, s_ref, o_ref):
        o_ref[...] = s_ref[...]

    return pl.pallas_call(
        body, name=name,
        grid_spec=pltpu.PrefetchScalarGridSpec(
            num_scalar_prefetch=1, grid=(R // tr, C // tc),
            in_specs=[pl.BlockSpec((tr, tc), lambda i, j, d: (i, j))],
            out_specs=pl.BlockSpec((None, tr, tc), lambda i, j, d: (d[0], i, j))),
        out_shape=jax.ShapeDtypeStruct((N_DEV, R, C), shard.dtype),
        compiler_params=_params("parallel", "parallel"),
    )(dev, shard)


def _all_gather(shards, lands, name):
    n = len(shards)

    def body(*refs):
        ins, outs = refs[:n], refs[2 * n:3 * n]
        send_sems, recv_sems = refs[3 * n:]
        x, y, c = _place()
        me, sibling = (x, y, c), (x, y, 1 - c)
        chips = [(1 - x, y), (x, 1 - y), (1 - x, 1 - y)]

        def copy(a, k, block, to, src=None):
            dst = outs[a].at[4 * block[0] + 2 * block[1] + block[2]]
            return pltpu.make_async_remote_copy(
                src_ref=dst if src is None else src, dst_ref=dst, send_sem=send_sems.at[a, k],
                recv_sem=recv_sems.at[a, k], device_id=to, device_id_type=MESH)

        sent = []
        for a in range(n):
            sent.append(copy(a, 0, me, sibling, src=ins[a]))
            sent += [copy(a, 1 + j, me, (*chip, c), src=ins[a]) for j, chip in enumerate(chips)]
        for cp in sent:
            cp.start()
        for j, chip in enumerate(chips):
            for a in range(n):
                copy(a, 1 + j, (*chip, c), me).wait_recv()
                fwd = copy(a, 4 + j, (*chip, c), sibling)
                fwd.start()
                sent.append(fwd)
        for a in range(n):
            copy(a, 0, sibling, me).wait_recv()
        for j, chip in enumerate(chips):
            for a in range(n):
                copy(a, 4 + j, (*chip, 1 - c), me).wait_recv()
        for cp in sent:
            cp.wait_send()

    return pl.pallas_call(
        body, name=name, in_specs=[ANY] * (2 * n), out_specs=[ANY] * n,
        out_shape=[jax.ShapeDtypeStruct(l.shape, l.dtype) for l in lands],
        input_output_aliases={n + a: a for a in range(n)},
        scratch_shapes=[pltpu.SemaphoreType.DMA((n, 7)), pltpu.SemaphoreType.DMA((n, 7))],
    )(*shards, *lands)


def _pair_plan(srcs, lands):
    x, y, c = _place()
    plan = []
    for a in range(len(srcs)):
        for q in range(N_CHIP):
            plan.append((srcs[a].at[2 * q + 1 - c], lands[a].at[q], lands[a].at[q], (x, y, 1 - c)))
    return plan


def _chip_plan(srcs, lands):
    x, y, c = _place()
    mine = 2 * x + y
    plan = []
    for a in range(len(srcs)):
        for px, py in [(1 - x, y), (x, 1 - y), (1 - x, 1 - y)]:
            peer = 2 * px + py
            plan.append((srcs[a].at[peer], lands[a].at[mine], lands[a].at[peer], (px, py, c)))
    return plan


def _gather_plan(srcs, lands):
    x, y, c = _place()
    mine = 4 * x + 2 * y + c
    plan = []
    for a in range(len(srcs)):
        for px, py, pc in [(x, y, 1 - c), (1 - x, y, c), (x, 1 - y, c), (1 - x, 1 - y, c)]:
            plan.append((srcs[a], lands[a].at[mine], lands[a].at[4 * px + 2 * py + pc], (px, py, pc)))
    return plan


def _forward_plan(srcs, lands):
    x, y, c = _place()
    plan = []
    for a in range(len(lands)):
        for px, py in [(1 - x, y), (x, 1 - y), (1 - x, 1 - y)]:
            mine, theirs = 4 * px + 2 * py + c, 4 * px + 2 * py + 1 - c
            plan.append((lands[a].at[mine], lands[a].at[mine], lands[a].at[theirs], (x, y, 1 - c)))
    return plan


def _remote(src, dst, send_sem, recv_sem, peer):
    return pltpu.make_async_remote_copy(src_ref=src, dst_ref=dst, send_sem=send_sem, recv_sem=recv_sem,
                                        device_id=peer, device_id_type=MESH)


HBM = pl.BlockSpec(memory_space=pltpu.HBM)
SEMS = pl.BlockSpec(memory_space=pltpu.SEMAPHORE)
DATAFLOW = pltpu.SideEffectType.DATAFLOW_SIDE_EFFECTING


def _split_start(srcs, lands, after, plan_fn, per_array, name):
    ns, nb = len(srcs), len(srcs) + len(lands)
    ncopy = per_array * len(lands)
    after = list(after) if isinstance(after, (list, tuple)) else [after]

    def body(*refs):
        send_sems, recv_sems = refs[nb + len(after)], refs[nb + len(after) + 1]
        token = refs[-1]
        for k, (src, dst, _, peer) in enumerate(plan_fn(refs[:ns], refs[ns:nb])):
            _remote(src, dst, send_sems.at[k], recv_sems.at[k], peer).start()
        token[...] = jnp.zeros_like(token)

    thru = [pltpu.HBM(a.shape, a.dtype) for a in (*srcs, *lands)]
    out = pl.pallas_call(
        body, name=name,
        out_shape=(pltpu.SemaphoreType.DMA((ncopy,)), pltpu.SemaphoreType.DMA((ncopy,)), *thru,
                   jax.ShapeDtypeStruct((8, 128), F32)),
        in_specs=[HBM] * nb + [ANY] * len(after),
        out_specs=(SEMS, SEMS, *[HBM] * nb, pl.BlockSpec(memory_space=pltpu.VMEM)),
        input_output_aliases={i: 2 + i for i in range(nb)},
        compiler_params=pltpu.CompilerParams(has_side_effects=DATAFLOW),
    )(*[pltpu.with_memory_space_constraint(a, pltpu.HBM) for a in (*srcs, *lands)], *after)
    return out[0], out[1], list(out[2:2 + ns]), list(out[2 + ns:2 + nb]), out[-1]


def _split_wait(send_sems, recv_sems, srcs, lands, after, plan_fn, name):
    ns, nb = len(srcs), len(srcs) + len(lands)
    after = list(after) if isinstance(after, (list, tuple)) else [after]

    def body(*refs):
        send_ref, recv_ref = refs[nb], refs[nb + 1]
        for k, (src, dst, mine, peer) in enumerate(plan_fn(refs[:ns], refs[ns:nb])):
            _remote(src, dst, send_ref.at[k], recv_ref.at[k], peer).wait_send()
            _remote(src, mine, send_ref.at[k], recv_ref.at[k], peer).wait_recv()

    thru = [pltpu.HBM(a.shape, a.dtype) for a in (*srcs, *lands)]
    out = pl.pallas_call(
        body, name=name, out_shape=tuple(thru),
        in_specs=[HBM] * nb + [SEMS, SEMS] + [ANY] * len(after), out_specs=tuple([HBM] * nb),
        input_output_aliases={i: i for i in range(nb)},
        compiler_params=pltpu.CompilerParams(has_side_effects=DATAFLOW),
    )(*srcs, *lands, send_sems, recv_sems, *after)
    return list(out[:ns]), list(out[ns:])


def _gather_forward(lands, name):
    n = len(lands)

    def body(*refs):
        landed, out = refs[:n], refs[n:2 * n]
        send_sems, recv_sems = refs[2 * n:]
        x, y, c = _place()
        chips = [(1 - x, y), (x, 1 - y), (1 - x, 1 - y)]
        sent = []
        for a in range(n):
            for j, (px, py) in enumerate(chips):
                blk = 4 * px + 2 * py + c
                sent.append(_remote(landed[a].at[blk], out[a].at[blk], send_sems.at[a, j], recv_sems.at[a, j], (x, y, 1 - c)))
        for cp in sent:
            cp.start()
        for a in range(n):
            for j, (px, py) in enumerate(chips):
                blk = 4 * px + 2 * py + 1 - c
                _remote(landed[a].at[blk], out[a].at[blk], send_sems.at[a, j], recv_sems.at[a, j], (x, y, 1 - c)).wait_recv()
        for cp in sent:
            cp.wait_send()

    return pl.pallas_call(
        body, name=name, in_specs=[ANY] * n, out_specs=[ANY] * n,
        out_shape=[jax.ShapeDtypeStruct(l.shape, l.dtype) for l in lands],
        input_output_aliases={a: a for a in range(n)},
        scratch_shapes=[pltpu.SemaphoreType.DMA((n, 3)), pltpu.SemaphoreType.DMA((n, 3))],
    )(*lands)


def _pack_rows(groups, name):
    n = len(groups[0])
    rows = -(-n // 8) * 8
    width = max(v.shape[1] for v in groups[0])

    def body(*refs):
        ins, outs = refs[:n * len(groups)], refs[n * len(groups):]
        for gi, o_ref in enumerate(outs):
            o_ref[...] = jnp.zeros_like(o_ref)
            for r in range(n):
                v_ref = ins[gi * n + r]
                o_ref[r:r + 1, 0:v_ref.shape[1]] = v_ref[...]

    return pl.pallas_call(
        body, name=name, out_shape=[jax.ShapeDtypeStruct((rows, width), F32)] * len(groups), compiler_params=_params(),
    )(*[v for g in groups for v in g])


def _unpack_rows(packs, like, name):
    n = len(like)

    def body(*refs):
        ins, outs = refs[:len(packs)], refs[len(packs):]
        for pi, p_ref in enumerate(ins):
            for r in range(n):
                o_ref = outs[pi * n + r]
                o_ref[...] = p_ref[r:r + 1, 0:o_ref.shape[1]]

    flat = pl.pallas_call(
        body, name=name, out_shape=[jax.ShapeDtypeStruct(a.shape, F32) for _ in packs for a in like],
        compiler_params=_params(),
    )(*packs)
    return [flat[pi * n:(pi + 1) * n] for pi in range(len(packs))]


def _sum_adamw(parts, w, m, v, name, col=None):
    P, R, _ = parts.shape
    C = w.shape[1]
    tr = _row_tile(R, cap=512, mult=8)

    def body(c_ref, p_ref, w_ref, m_ref, v_ref, g_ref, d_ref, nm_ref, nv_ref):
        g = p_ref[0].astype(F32)
        for q in range(1, P):
            g = g + p_ref[q].astype(F32)
        g_ref[...] = g
        d_ref[...], nm_ref[...], nv_ref[...] = _adamw_math(w_ref[...], g, m_ref[...], v_ref[...])

    blk = pl.BlockSpec((tr, C), lambda i, c: (i, 0))
    sds = jax.ShapeDtypeStruct((R, C), F32)
    at = jnp.zeros((1,), jnp.int32) if col is None else col
    return pl.pallas_call(
        body, name=name,
        grid_spec=pltpu.PrefetchScalarGridSpec(
            num_scalar_prefetch=1, grid=(R // tr,),
            in_specs=[pl.BlockSpec((P, tr, C), lambda i, c: (0, i, c[0])), blk, blk, blk], out_specs=[blk] * 4),
        out_shape=[sds] * 4, compiler_params=_params("parallel"),
    )(at, parts, w, m, v)


def _local_step(x, mem, target, sp, W, arrived, want, done, paired):
    h0 = _rms_fwd([x], [sp["g_mix"]], "rms_mix")
    proj = _mm(h0, W["main_t"], "nt", F32, "proj_main", tm=2048, tn=512)
    f_logit = _mm(h0, W["f_t"], "nt", F32, "proj_f", tm=2048)
    qn, kn, vb = _qk_fwd(proj, sp["g_q"], sp["g_k"], "qk_norm")
    b_pad = jnp.pad(sp["b_f"], ((0, 0), (0, HEAD - FOX_H)))
    c_col, c_row = _fgate_fwd(f_logit, b_pad, "forget_cumsum")
    o_fox, lse = _fox_fwd(qn, kn, vb, c_col, c_row, "fox_fwd")
    lru_p = (sp["conv_w"], sp["conv_b"], sp["w_ra"], sp["b_ra"], sp["w_ri"], sp["b_ri"], sp["lam"])
    y_lru, h_lru = _lru_fwd(proj, *lru_p, "lru_fwd", after=arrived("attn", o_fox))
    w_out, w_cq, w_ckv, w_co = want("attn", y_lru)
    mix = _rms_fwd([o_fox, y_lru], [sp["g_fox_out"], sp["g_lru_out"]], "rms_mix_out")
    x1 = _mm(mix, w_out, "nn", F32, "out_proj", add=x)
    hq = _rms_fwd([x1], [sp["g_xattn"]], "rms_xattn")
    mn = _rms_fwd([mem], [sp["g_mem"]], "rms_mem")
    cq = _mm(hq, w_cq, "nn", F32, "xattn_q", tm=2048)
    ckv = _mm(mn, w_ckv, "nn", F32, "xattn_kv")
    o_x = _xattn_fwd(cq, ckv, sp["g_cq"], sp["g_ck"], "xattn_fwd")
    co_w = w_co.shape[2]
    x2 = _mm(o_x, w_co, "nn", F32, "xattn_out", add=x1, tm=2048, tn=co_w, after=arrived("gate_up", o_x))
    hf = _rms_fwd([x2], [sp["g_ffn"]], "rms_ffn")
    (w_gu,) = want("gate_up", hf)
    gu_w = w_gu.shape[2]
    f_gate, f_up, act = _ffn_up(hf, w_gu, "ffn_up")
    (w_down,) = want("down", arrived("down", act))
    x3 = _mm(act, w_down, "nn", F32, "ffn_down", add=x2, tk=w_down.shape[0] // 2)
    dy, dyb, sq = _loss_head(x3, target, "loss_head")
    g_down = _mm(act, dyb, "tn", BF16, "ffn_down_dw", tm=w_down.shape[0] // 4, tn=2048)
    dgu = _ffn_down_dx(dyb, w_down, f_gate, f_up, "ffn_down_dx")
    g_gu = _mm(hf, dgu, "tn", BF16, "ffn_up_dw", out_blocks=N_DEV, tn=gu_w)
    tok = done("ffn", [g_gu, g_down])
    dhf = _mm(dgu, w_gu, "nt", F32, "ffn_up_dx", tn=2048, tk=gu_w, after=tok)
    tok = paired("ffn", dhf)
    dx2, dx2b, dg_ffn = _rms_bwd(x2, sp["g_ffn"], dhf, "rms_ffn_bwd", res=dy, after=tok)
    d_ox = _mm(dx2b, w_co, "nt", F32, "xattn_out_dx", tm=2048, tk=co_w)
    g_co = _mm(o_x, dx2b, "tn", BF16, "xattn_out_dw", out_blocks=N_DEV, tn=co_w)
    dcq, dckv, dg_cq, dg_ck = _xattn_bwd(cq, ckv, sp["g_cq"], sp["g_ck"], d_ox, "xattn_bwd")
    dhq = _mm(dcq, w_cq, "nt", F32, "xattn_q_dx", tm=2048)
    g_cq = _mm(hq, dcq, "tn", BF16, "xattn_q_dw")
    dmn = _mm(dckv, w_ckv, "nt", F32, "xattn_kv_dx")
    g_ckv = _mm(mn, dckv, "tn", BF16, "xattn_kv_dw")
    (dg_mem,) = _rms_bwd(mem, sp["g_mem"], dmn, "rms_mem_bwd", want_dx=False)
    dx1, dx1b, dg_xattn = _rms_bwd(x1, sp["g_xattn"], dhq, "rms_xattn_bwd", res=dx2)
    dmix = _mm(dx1b, w_out, "nt", F32, "out_proj_dx")
    g_out = _mm(mix, dx1b, "tn", BF16, "out_proj_dw", tn=2048)
    tok = done("attn", [g_out, g_cq, g_ckv, g_co])
    do_fox, _, dg_fox_out = _rms_bwd(o_fox, sp["g_fox_out"], dmix, "rms_fox_out_bwd", dy_col=0, after=tok)
    dy_lru, _, dg_lru_out = _rms_bwd(y_lru, sp["g_lru_out"], dmix, "rms_lru_out_bwd", dy_col=1)
    tok = paired("attn", dy_lru)
    du, dgate, dconv_w, dconv_b, dw_ra, db_ra, dw_ri, db_ri, dlam = _lru_bwd(proj, h_lru, dy_lru, *lru_p, "lru_bwd", after=tok)
    dqn, dkn, dv, dc_col, dc_row = _fox_bwd(qn, kn, vb, c_col, c_row, lse, o_fox, do_fox, "fox_bwd")
    dq, dk, dg_q, dg_k = _qk_bwd(proj, sp["g_q"], sp["g_k"], dqn, dkn, "qk_norm_bwd")
    df, db_f = _fgate_bwd(f_logit, b_pad, dc_col, dc_row, "forget_cumsum_bwd")
    dproj = jnp.concatenate([dq, dk, dv, du, dgate], axis=1)
    g_main_t = _mm(dproj, h0, "tn", BF16, "proj_main_dw", tn=2048)
    g_f_t = _mm(df, h0, "tn", BF16, "proj_f_dw", tn=2048)
    tok = done("w_in", [g_main_t, g_f_t])
    dh_f = _mm(df, W["f_t"], "nn", F32, "proj_f_dx", tm=2048)
    dh0 = _mm(dproj, W["main_t"], "nn", F32, "proj_main_dx", add=dh_f, tk=dproj.shape[1] // 2, after=tok)
    tok = paired("w_in", dh0)
    grad_x, _, dg_mix = _rms_bwd(x, sp["g_mix"], dh0, "rms_mix_bwd", res=dx1, after=tok)
    small = dict(g_mix=dg_mix, b_f=db_f[:, :FOX_H], g_q=dg_q, g_k=dg_k, conv_w=dconv_w, conv_b=dconv_b, w_ra=dw_ra,
                 b_ra=db_ra, w_ri=dw_ri, b_ri=db_ri, lam=dlam, g_fox_out=dg_fox_out, g_lru_out=dg_lru_out,
                 g_xattn=dg_xattn, g_mem=dg_mem, g_cq=dg_cq, g_ck=dg_ck, g_ffn=dg_ffn)
    return sq, grad_x, small


BIG = ("w_in", "w_out", "w_cq", "w_ckv", "w_co", "w_gate_up", "w_down")
SMALL = ("g_mix", "b_f", "g_q", "g_k", "conv_b", "w_ra", "b_ra", "w_ri", "b_ri", "lam", "g_fox_out", "g_lru_out",
         "g_xattn", "g_mem", "g_cq", "g_ck", "g_ffn")
ORDER = ("g_mix", "w_in", "b_f", "g_q", "g_k", "conv_w", "conv_b", "w_ra", "b_ra", "w_ri", "b_ri", "lam", "g_fox_out",
         "g_lru_out", "w_out", "g_xattn", "g_mem", "w_cq", "w_ckv", "g_cq", "g_ck", "w_co", "g_ffn", "w_gate_up", "w_down")


def kernel(x, mem, g_mix, w_in, b_f, g_q, g_k, conv_w, conv_b, w_ra, b_ra, w_ri, b_ri, lam, g_fox_out, g_lru_out, w_out, g_xattn, g_mem, w_cq, w_ckv, g_cq, g_ck, w_co, g_ffn, w_gate_up, w_down, loss_target, m_g_mix, m_w_in, m_b_f, m_g_q, m_g_k, m_conv_w, m_conv_b, m_w_ra, m_b_ra, m_w_ri, m_b_ri, m_lam, m_g_fox_out, m_g_lru_out, m_w_out, m_g_xattn, m_g_mem, m_w_cq, m_w_ckv, m_g_cq, m_g_ck, m_w_co, m_g_ffn, m_w_gate_up, m_w_down, v_g_mix, v_w_in, v_b_f, v_g_q, v_g_k, v_conv_w, v_conv_b, v_w_ra, v_b_ra, v_w_ri, v_b_ri, v_lam, v_g_fox_out, v_g_lru_out, v_w_out, v_g_xattn, v_g_mem, v_w_cq, v_w_ckv, v_g_cq, v_g_ck, v_w_co, v_g_ffn, v_w_gate_up, v_w_down):
    given = dict(locals())
    w = {n: given[n] for n in ORDER}
    m = {n: given["m_" + n] for n in ORDER}
    v = {n: given["v_" + n] for n in ORDER}
    D = x.shape[2]
    fw = FOX_H * HEAD
    dev_index = 4 * lax.axis_index("x") + 2 * lax.axis_index("y") + lax.axis_index("c")
    dev = jnp.reshape(dev_index, (1,)).astype(jnp.int32)
    core = jnp.reshape(lax.axis_index("c"), (1,)).astype(jnp.int32)
    chip = jnp.reshape(2 * lax.axis_index("x") + lax.axis_index("y"), (1,)).astype(jnp.int32)

    def shard(d, n):
        return jnp.transpose(d[n], (2, 0, 1)) if n == "w_in" else d[n][0]

    def unshard(a, n):
        return jnp.transpose(a, (1, 2, 0)) if n == "w_in" else a[None]

    gather_groups = dict(attn=("w_out", "w_cq", "w_ckv", "w_co"), gate_up=("w_gate_up",), down=("w_down",))
    reduce_groups = dict(ffn=("w_gate_up", "w_down"), attn=("w_out", "w_cq", "w_ckv", "w_co"), w_in=("w_in",))
    column_blocked = ("w_co", "w_gate_up")
    flying = {}

    def land(a, tag):
        return _land_with_own(a, dev, "own_" + tag)

    def launch(group, after):
        shards = [shard(w, n).astype(BF16) for n in gather_groups[group]]
        lands = [land(s, n) for s, n in zip(shards, gather_groups[group])]
        flying[group] = _split_start(shards, lands, after, _gather_plan, 4, "gather_" + group + "_start")
        return flying[group][4]

    def arrived(group, after):
        send, recv, shards, lands, _ = flying.pop(group)
        _, lands = _split_wait(send, recv, shards, lands, after, _gather_plan, "gather_" + group + "_wait")
        flying[group] = _split_start([], lands, after, _forward_plan, 3, "gather_" + group + "_forward_start")
        if group == "attn":
            return launch("down", flying[group][4])
        return flying[group][4]

    def want(group, after):
        send, recv, _, lands, _ = flying.pop(group)
        _, full = _split_wait(send, recv, [], lands, after, _forward_plan, "gather_" + group + "_forward_wait")
        return [g if n in column_blocked else g.reshape(-1, g.shape[2]) for n, g in zip(gather_groups[group], full)]

    first = [shard(w, "w_in").astype(BF16).reshape(-1, D), conv_w[0]]
    g_in, g_conv = _all_gather(first, [land(first[0], "w_in"), land(first[1], "conv_w")], "gather_w_in")
    tok = launch("gate_up", launch("attn", g_conv))
    per = g_in.shape[1]
    f_k, f_lo = divmod(3 * fw, per)
    assert f_lo + FOX_H <= per
    wt_in = g_in.reshape(-1, D)
    W = dict(main_t=jnp.concatenate([wt_in[:3 * fw], wt_in[3 * fw + FOX_H:]], axis=0),
             f_t=jnp.pad(wt_in[3 * fw:3 * fw + FOX_H], ((0, HEAD - FOX_H), (0, 0))))
    sp = {n: w[n] for n in SMALL if n not in ("w_ra", "w_ri")}
    sp["w_ra"], sp["w_ri"] = w_ra[0], w_ri[0]
    sp["conv_w"] = jnp.transpose(g_conv, (1, 0, 2)).reshape(CONV_K, -1)
    sp["g_mix"] = sp["g_mix"] + tok[0, 0]

    pairing, reducing = {}, {}

    def done(group, grads):
        if group == "w_in":
            g_main_t, g_f_t = grads
            shards = [g_main_t[k * per:(k + 1) * per] for k in range(f_k)]
            shards.append(jnp.concatenate([g_main_t[f_k * per:3 * fw], g_f_t[:FOX_H],
                                           g_main_t[3 * fw:(f_k + 1) * per - FOX_H]], axis=0))
            shards += [g_main_t[k * per - FOX_H:(k + 1) * per - FOX_H] for k in range(f_k + 1, N_DEV)]
            grads = [jnp.stack(shards)]
        g8 = [g if g.ndim == 3 else g.reshape(N_DEV, -1, g.shape[1]) for g in grads]
        lands = [lax.empty((N_CHIP,) + g.shape[1:], g.dtype) for g in g8]
        pairing[group] = _split_start(g8, lands, chip, _pair_plan, N_CHIP, "reduce_" + group + "_pair_start")
        return pairing[group][4]

    def paired(group, after):
        send, recv, g8, lands, _ = pairing.pop(group)
        g8, from_sibling = _split_wait(send, recv, g8, lands, after, _pair_plan, "reduce_" + group + "_pair_wait")
        p4 = [_pair_sum(g, r, core, "reduce_pair_sum_" + n) for g, r, n in zip(g8, from_sibling, reduce_groups[group])]
        lands = [lax.empty(p.shape, p.dtype) for p in p4]
        reducing[group] = _split_start(p4, lands, chip, _chip_plan, 3, "reduce_" + group + "_start")
        return reducing[group][4]

    def finish(group, after):
        send, recv, p4, lands, _ = reducing.pop(group)
        p4, lands = _split_wait(send, recv, p4, lands, after, _chip_plan, "reduce_" + group + "_wait")
        return {n: tuple(unshard(r, n) for r in _reduce_adamw(p, l, chip, shard(w, n), shard(m, n), shard(v, n), "adamw_" + n))
                for n, p, l in zip(reduce_groups[group], p4, lands)}

    sq, grad_x, gs = _local_step(x[0], mem[0], loss_target[0], sp, W, arrived, want, done, paired)

    vectors = tuple(n for n in SMALL if n not in ("w_ra", "w_ri"))
    mine = [_pack_rows([[gs[n] for n in vectors] + [sq[0:1]]], "pack_small_grads")[0], gs["w_ra"].reshape(-1, HEAD),
            gs["w_ri"].reshape(-1, HEAD), gs["conv_w"]]
    lands = [land(a, "small_grads_%d" % k) for k, a in enumerate(mine)]
    s_send, s_recv, s_src, s_land, tok = _split_start(mine, lands, grad_x, _gather_plan, 4, "gather_small_start")
    out = finish("ffn", tok)
    out.update(finish("attn", tok))
    updated = [r for n in reduce_groups["ffn"] + reduce_groups["attn"] for r in out[n]]
    _, s_land = _split_wait(s_send, s_recv, s_src, s_land, updated, _gather_plan, "gather_small_wait")
    all_vec, all_ra, all_ri, all_conv = _gather_forward(s_land, "gather_small_forward")

    nothing = jnp.zeros((1, HEAD), F32)
    state = _pack_rows([[d[n] for n in vectors] + [nothing] for d in (w, m, v)], "pack_small_state")
    upd = _sum_adamw(all_vec, *state, "adamw_small_vectors")
    per_vector = _unpack_rows(upd, [w[n] for n in vectors] + [nothing], "unpack_small")
    for r, n in enumerate(vectors):
        out[n] = tuple(per_vector[k][r] for k in range(4))
    loss = 0.5 * per_vector[0][len(vectors)][0, 0] / D
    for n, parts in (("w_ra", all_ra), ("w_ri", all_ri)):
        res = _sum_adamw(parts, *[d[n].reshape(-1, HEAD) for d in (w, m, v)], "adamw_" + n)
        out[n] = tuple(r.reshape(w[n].shape) for r in res)
    out["conv_w"] = tuple(r[None] for r in _sum_adamw(all_conv, conv_w[0], m_conv_w[0], v_conv_w[0], "adamw_conv_w", col=dev))
    out.update(finish("w_in", upd[1]))

    return (loss, grad_x[None], *[out[n][0] for n in ORDER], *[out[n][1] for n in ORDER],
            *[out[n][2] for n in ORDER], *[out[n][3] for n in ORDER])
```

```python
import functools
import math

import jax
import jax.numpy as jnp
from jax import lax
from jax.experimental import pallas as pl
from jax.experimental.pallas import tpu as pltpu

F32 = jnp.float32
BF16 = jnp.bfloat16
MESH = pl.DeviceIdType.MESH
N_DEV = 8
N_CHIP = 4

HEAD = 128
FOX_H = 8
XATT_H = 4
LRU_NB = 8
CONV_K = 4
LRU_C = 8.0
RMS_EPS = 1e-6
ATT_T = 1024
CUM_T = 256
ROW_T = 256
NEG = -1e30
V7X_VMEM_LIMIT = 48 * 1024 * 1024
V7X_VMEM_LIMIT_HIGH = 60 * 1024 * 1024

ADAM_LR = 0.001
ADAM_B1 = 0.9
ADAM_B2 = 0.999
ADAM_EPS = 1e-08
ADAM_WD = 0.01
ADAM_STEP = 10

NT = (((1,), (1,)), ((), ()))
TN = (((0,), (0,)), ((), ()))
NN = (((1,), (0,)), ((), ()))

ANY = pl.BlockSpec(memory_space=pl.ANY)


def _behind(after):
    return ([], []) if after is None else ([ANY], [after])


def _params(*sem):
    return pltpu.CompilerParams(dimension_semantics=sem or None, vmem_limit_bytes=V7X_VMEM_LIMIT)


def _dot(a, b, dn=NN):
    return lax.dot_general(a, b, dn, preferred_element_type=F32)


def _rms(x, g):
    r = lax.rsqrt(jnp.mean(x * x, axis=-1, keepdims=True) + RMS_EPS)
    return x * r * g


def _rms_grad(x, g, dy):
    r = lax.rsqrt(jnp.mean(x * x, axis=-1, keepdims=True) + RMS_EPS)
    xh = x * r
    dxh = dy * g
    dx = r * (dxh - xh * jnp.mean(dxh * xh, axis=-1, keepdims=True))
    return dx, jnp.sum(dy * xh, axis=0, keepdims=True)


def _gelu(x):
    k = math.sqrt(2.0 / math.pi)
    return 0.5 * x * (1.0 + jnp.tanh(k * (x + 0.044715 * (x * x * x))))


def _gelu_grad(x):
    k = math.sqrt(2.0 / math.pi)
    t = jnp.tanh(k * (x + 0.044715 * (x * x * x)))
    return 0.5 * (1.0 + t) + 0.5 * x * (1.0 - t * t) * k * (1.0 + 3.0 * 0.044715 * x * x)


def _one_minus_exp(z):
    series = -(z + 0.5 * z * z + (1.0 / 6.0) * z * z * z)
    return jnp.where(z > -1e-3, series, 1.0 - jnp.exp(z))


def _row_tile(rows, cap=ROW_T, mult=16):
    t = min(rows, cap)
    while rows % t or (t % mult and t != rows):
        t -= 1
    return t


def _mat_dims(z):
    return (z.shape[-2], z.shape[-1] * (z.shape[0] if z.ndim == 3 else 1))


def _mat_spec(arr, rblk, cblk, rsel, csel):
    if arr.ndim == 2:
        return pl.BlockSpec((rblk, cblk), lambda i, j, k: ((i, j, k)[rsel], (i, j, k)[csel]))
    nw = arr.shape[2]
    assert nw % cblk == 0, (arr.shape, cblk)
    per = nw // cblk
    return pl.BlockSpec((None, rblk, cblk),
                        lambda i, j, k: ((i, j, k)[csel] // per, (i, j, k)[rsel], (i, j, k)[csel] % per))


def _mm(a, b, mode, out_dtype, name, add=None, out_blocks=None, tm=1024, tn=1024, tk=None, after=None):
    ar, ac = _mat_dims(a)
    br, bc = _mat_dims(b)
    if mode == "nn":
        M, K, N = ar, ac, bc
        assert br == K
    elif mode == "nt":
        M, K, N = ar, ac, br
        assert bc == K
    else:
        M, K, N = ac, ar, bc
        assert br == K
    tm, tn = min(tm, M), min(tn, N)
    tk = K if tk is None or mode == "tn" else min(tk, K)
    assert M % tm == 0 and N % tn == 0 and K % tk == 0, (name, M, N, K, tm, tn, tk)
    nk = K // tk
    nj = N // tn
    if mode == "nn":
        specs = [_mat_spec(a, tm, tk, 0, 2), _mat_spec(b, tk, tn, 2, 1)]
        dn = NN
    elif mode == "nt":
        specs = [_mat_spec(a, tm, tk, 0, 2), _mat_spec(b, tn, tk, 1, 2)]
        dn = NT
    else:
        specs = [_mat_spec(a, tk, tm, 2, 0), _mat_spec(b, tk, tn, 2, 1)]
        dn = TN
    args = [a, b]
    if add is not None:
        specs.append(pl.BlockSpec((tm, tn), lambda i, j, k: (i, j)))
        args.append(add)
    specs += _behind(after)[0]
    args += _behind(after)[1]
    n_in = len(args)
    if out_blocks is None:
        out_shape = jax.ShapeDtypeStruct((M, N), out_dtype)
        out_spec = pl.BlockSpec((tm, tn), lambda i, j, k: (i, j))
    else:
        nb = out_blocks
        nw = N // nb
        assert nw % tn == 0
        per = nw // tn
        out_shape = jax.ShapeDtypeStruct((nb, M, nw), out_dtype)
        out_spec = pl.BlockSpec((None, tm, tn), lambda i, j, k: (j // per, i, j % per))
    keep_t = mode == "tn" and nj > 1
    scratch = []
    if nk > 1:
        scratch.append(pltpu.VMEM((tm, tn), F32))
    if keep_t:
        scratch.append(pltpu.VMEM((tm, tk), a.dtype))

    def body(*refs):
        a_ref, b_ref = refs[0], refs[1]
        add_ref = refs[2] if add is not None else None
        o_ref = refs[n_in]

        def finish(r):
            if add_ref is not None:
                r = r + add_ref[...]
            o_ref[...] = r.astype(out_dtype)

        if keep_t:
            at_ref = refs[-1]

            @pl.when(pl.program_id(1) == 0)
            def _():
                at_ref[...] = a_ref[...].T

            finish(_dot(at_ref[...], b_ref[...], NN))
        elif nk == 1:
            finish(_dot(a_ref[...], b_ref[...], dn))
        else:
            acc_ref = refs[n_in + 1]
            k = pl.program_id(2)

            @pl.when(k == 0)
            def _():
                acc_ref[...] = jnp.zeros_like(acc_ref)

            acc_ref[...] += _dot(a_ref[...], b_ref[...], dn)

            @pl.when(k == nk - 1)
            def _():
                finish(acc_ref[...])

    return pl.pallas_call(
        body, name=name, grid=(M // tm, nj, nk), in_specs=specs, out_specs=out_spec, out_shape=out_shape,
        scratch_shapes=scratch,
        compiler_params=_params("parallel", "arbitrary" if keep_t else "parallel", "arbitrary"),
    )(*args)


def _rms_fwd(xs, gs, name, after=None):
    n = len(xs)
    rows = xs[0].shape[0]
    widths = [x.shape[1] for x in xs]
    tr = _row_tile(rows)
    more_specs, more = _behind(after)

    def body(*refs):
        o_ref = refs[2 * n + len(more)]
        off = 0
        for t in range(n):
            o_ref[:, off:off + widths[t]] = _rms(refs[t][...], refs[n + t][...]).astype(BF16)
            off += widths[t]

    return pl.pallas_call(
        body, name=name, grid=(rows // tr,),
        in_specs=[pl.BlockSpec((tr, w), lambda i: (i, 0)) for w in widths]
        + [pl.BlockSpec((1, w), lambda i: (0, 0)) for w in widths] + more_specs,
        out_specs=pl.BlockSpec((tr, sum(widths)), lambda i: (i, 0)),
        out_shape=jax.ShapeDtypeStruct((rows, sum(widths)), BF16),
        compiler_params=_params("parallel"),
    )(*xs, *gs, *more)


def _rms_bwd(x, g, dy, name, dy_col=0, res=None, want_dx=True, after=None):
    rows, w = x.shape
    tr = _row_tile(rows)
    has_res = res is not None
    more_specs, more = _behind(after)

    def body(*refs):
        x_ref, g_ref, dy_ref = refs[:3]
        res_ref = refs[3] if has_res else None
        outs = refs[3 + has_res + len(more):]
        dg_ref = outs[-1]
        dx, dg = _rms_grad(x_ref[...], g_ref[...], dy_ref[...])

        @pl.when(pl.program_id(0) == 0)
        def _():
            dg_ref[...] = jnp.zeros_like(dg_ref)

        dg_ref[...] += dg
        if want_dx:
            if has_res:
                dx = dx + res_ref[...]
            outs[0][...] = dx
            outs[1][...] = dx.astype(BF16)

    row_spec = pl.BlockSpec((tr, w), lambda i: (i, 0))
    in_specs = [row_spec, pl.BlockSpec((1, w), lambda i: (0, 0)), pl.BlockSpec((tr, w), lambda i: (i, dy_col))]
    args = [x, g, dy]
    if has_res:
        in_specs.append(row_spec)
        args.append(res)
    in_specs += more_specs
    args += more
    out_specs, out_shape = [], []
    if want_dx:
        out_specs += [row_spec, row_spec]
        out_shape += [jax.ShapeDtypeStruct((rows, w), F32), jax.ShapeDtypeStruct((rows, w), BF16)]
    out_specs.append(pl.BlockSpec((1, w), lambda i: (0, 0)))
    out_shape.append(jax.ShapeDtypeStruct((1, w), F32))
    return pl.pallas_call(
        body, name=name, grid=(rows // tr,), in_specs=in_specs, out_specs=out_specs, out_shape=out_shape,
        compiler_params=_params("arbitrary"),
    )(*args)


def _loss_head(y, target, name):
    rows, w = y.shape
    tr = _row_tile(rows)

    def body(y_ref, t_ref, dy_ref, dyb_ref, acc_ref):
        e = y_ref[...] - t_ref[...]

        @pl.when(pl.program_id(0) == 0)
        def _():
            acc_ref[...] = jnp.zeros_like(acc_ref)

        acc_ref[...] += jnp.sum(e * e)
        dy = e * (1.0 / w)
        dy_ref[...] = dy
        dyb_ref[...] = dy.astype(BF16)

    row_spec = pl.BlockSpec((tr, w), lambda i: (i, 0))
    return pl.pallas_call(
        body, name=name, grid=(rows // tr,), in_specs=[row_spec, row_spec],
        out_specs=[row_spec, row_spec, pl.BlockSpec((8, 128), lambda i: (0, 0))],
        out_shape=[jax.ShapeDtypeStruct((rows, w), F32), jax.ShapeDtypeStruct((rows, w), BF16),
                   jax.ShapeDtypeStruct((8, 128), F32)],
        compiler_params=_params("arbitrary"),
    )(y, target)


def _qk_fwd(proj, g_q, g_k, name, after=None):
    rows = proj.shape[0]
    w = FOX_H * HEAD
    tr = _row_tile(rows)
    more_specs, more = _behind(after)

    def body(q_ref, k_ref, v_ref, gq_ref, gk_ref, *rest):
        qn_ref, kn_ref, vb_ref = rest[len(more):]
        for h in range(FOX_H):
            sl = slice(h * HEAD, (h + 1) * HEAD)
            qn_ref[:, sl] = _rms(q_ref[:, sl], gq_ref[...]).astype(BF16)
            kn_ref[:, sl] = _rms(k_ref[:, sl], gk_ref[...]).astype(BF16)
        vb_ref[...] = v_ref[...].astype(BF16)

    gain = pl.BlockSpec((1, HEAD), lambda i: (0, 0))
    out = pl.BlockSpec((tr, w), lambda i: (i, 0))
    return pl.pallas_call(
        body, name=name, grid=(rows // tr,),
        in_specs=[pl.BlockSpec((tr, w), lambda i: (i, 0)), pl.BlockSpec((tr, w), lambda i: (i, 1)),
                  pl.BlockSpec((tr, w), lambda i: (i, 2)), gain, gain] + more_specs,
        out_specs=[out, out, out], out_shape=[jax.ShapeDtypeStruct((rows, w), BF16)] * 3,
        compiler_params=_params("parallel"),
    )(proj, proj, proj, g_q, g_k, *more)


def _qk_bwd(proj, g_q, g_k, dqn, dkn, name):
    rows = proj.shape[0]
    w = FOX_H * HEAD
    tr = _row_tile(rows)

    def body(q_ref, k_ref, gq_ref, gk_ref, dqn_ref, dkn_ref, dq_ref, dk_ref, dgq_ref, dgk_ref):
        @pl.when(pl.program_id(0) == 0)
        def _():
            dgq_ref[...] = jnp.zeros_like(dgq_ref)
            dgk_ref[...] = jnp.zeros_like(dgk_ref)

        for h in range(FOX_H):
            sl = slice(h * HEAD, (h + 1) * HEAD)
            dq, dgq = _rms_grad(q_ref[:, sl], gq_ref[...], dqn_ref[:, sl])
            dk, dgk = _rms_grad(k_ref[:, sl], gk_ref[...], dkn_ref[:, sl])
            dq_ref[:, sl] = dq.astype(BF16)
            dk_ref[:, sl] = dk.astype(BF16)
            dgq_ref[...] += dgq
            dgk_ref[...] += dgk

    gain = pl.BlockSpec((1, HEAD), lambda i: (0, 0))
    row = pl.BlockSpec((tr, w), lambda i: (i, 0))
    return pl.pallas_call(
        body, name=name, grid=(rows // tr,),
        in_specs=[row, pl.BlockSpec((tr, w), lambda i: (i, 1)), gain, gain, row, row],
        out_specs=[row, row, gain, gain],
        out_shape=[jax.ShapeDtypeStruct((rows, w), BF16)] * 2 + [jax.ShapeDtypeStruct((1, HEAD), F32)] * 2,
        compiler_params=_params("arbitrary"),
    )(proj, proj, g_q, g_k, dqn, dkn)


def _fgate_fwd(f_logit, b_pad, name):
    S = f_logit.shape[0]
    T = min(CUM_T, S)
    nb = S // T
    RT = min(ATT_T, S)
    nbr = S // RT

    def body(f_ref, b_ref, col_ref, row_ref, c_scr):
        tri = (lax.broadcasted_iota(jnp.int32, (T, T), 0) >= lax.broadcasted_iota(jnp.int32, (T, T), 1)).astype(F32)
        carry = jnp.zeros((1, HEAD), F32)
        for blk in range(nb):
            z = f_ref[blk * T:(blk + 1) * T, :] + b_ref[...]
            lf = jnp.minimum(z, 0.0) - jnp.log1p(jnp.exp(-jnp.abs(z)))
            cb = jnp.dot(tri, lf, precision=lax.Precision.HIGHEST, preferred_element_type=F32) + carry
            c_scr[blk * T:(blk + 1) * T, :] = cb
            carry = cb[T - 1:T, :]
        c = c_scr[...]
        lane = lax.broadcasted_iota(jnp.int32, c.shape, 1)
        ct = c.T
        for h in range(FOX_H):
            col_ref[h] = jnp.sum(jnp.where(lane == h, c, 0.0), axis=1, keepdims=True)
            for jb in range(nbr):
                row_ref[h, jb] = ct[h:h + 1, jb * RT:(jb + 1) * RT]

    return pl.pallas_call(
        body, name=name,
        out_shape=[jax.ShapeDtypeStruct((FOX_H, S, 1), F32), jax.ShapeDtypeStruct((FOX_H, nbr, 1, RT), F32)],
        scratch_shapes=[pltpu.VMEM((S, HEAD), F32)], compiler_params=_params(),
    )(f_logit, b_pad)


def _fgate_bwd(f_logit, b_pad, dc_col, dc_row, name):
    S = f_logit.shape[0]
    T = min(CUM_T, S)
    nb = S // T
    RT = min(ATT_T, S)

    def body(f_ref, b_ref, dcol_ref, drow_ref, df_ref, db_ref, dc_scr, rt_scr):
        lane = lax.broadcasted_iota(jnp.int32, (S, HEAD), 1)
        sub = lax.broadcasted_iota(jnp.int32, (HEAD, RT), 0)
        dc = jnp.zeros((S, HEAD), F32)
        for h in range(FOX_H):
            dc = jnp.where(lane == h, dcol_ref[h], dc)
        for jb in range(S // RT):
            rt = jnp.zeros((HEAD, RT), F32)
            for h in range(FOX_H):
                rt = jnp.where(sub == h, drow_ref[h, jb], rt)
            rt_scr[jb * RT:(jb + 1) * RT, :] = rt.T
        dc_scr[...] = dc - rt_scr[...]
        tri = (lax.broadcasted_iota(jnp.int32, (T, T), 0) <= lax.broadcasted_iota(jnp.int32, (T, T), 1)).astype(F32)
        carry = jnp.zeros((1, HEAD), F32)
        db = jnp.zeros((1, HEAD), F32)
        for blk in reversed(range(nb)):
            rows = slice(blk * T, (blk + 1) * T)
            dlf = jnp.dot(tri, dc_scr[rows, :], precision=lax.Precision.HIGHEST, preferred_element_type=F32) + carry
            carry = dlf[0:1, :]
            z = f_ref[rows, :] + b_ref[...]
            df = dlf * jax.nn.sigmoid(-z)
            df_ref[rows, :] = df.astype(BF16)
            db = db + jnp.sum(df, axis=0, keepdims=True)
        db_ref[...] = db

    return pl.pallas_call(
        body, name=name,
        out_shape=[jax.ShapeDtypeStruct((S, HEAD), BF16), jax.ShapeDtypeStruct((1, HEAD), F32)],
        scratch_shapes=[pltpu.VMEM((S, HEAD), F32), pltpu.VMEM((S, HEAD), F32)], compiler_params=_params(),
    )(f_logit, b_pad, dc_col, dc_row)


def _fox_fwd(qn, kn, vb, c_col, c_row, name, after=None):
    S = qn.shape[0]
    T = min(ATT_T, S)
    nb = S // T
    scale = 1.0 / math.sqrt(HEAD)
    more_specs, more = _behind(after)

    def body(q_ref, k_ref, v_ref, cc_ref, cr_ref, *rest):
        o_ref, lse_ref = rest[len(more):]
        i = pl.program_id(1)
        q = q_ref[...]
        cc = cc_ref[...]

        def step(j, carry, diagonal=False):
            m, l, acc = carry
            sl = pl.ds(pl.multiple_of(j * T, T), T)
            s = _dot(q, k_ref[sl, :], NT) * scale + cc - cr_ref[j]
            if diagonal:
                s = jnp.where(lax.broadcasted_iota(jnp.int32, (T, T), 0) >= lax.broadcasted_iota(jnp.int32, (T, T), 1), s, NEG)
            m2 = jnp.maximum(m, jnp.max(s, axis=1, keepdims=True))
            p = jnp.exp(s - m2)
            al = jnp.exp(m - m2)
            return m2, al * l + jnp.sum(p, axis=1, keepdims=True), al * acc + _dot(p.astype(BF16), v_ref[sl, :])

        init = (jnp.full((T, 1), NEG, F32), jnp.zeros((T, 1), F32), jnp.zeros((T, HEAD), F32))
        m, l, acc = step(i, lax.fori_loop(0, i, step, init), diagonal=True)
        o_ref[...] = acc / l
        lse_ref[...] = m + jnp.log(l)

    head = pl.BlockSpec((S, HEAD), lambda h, i: (0, h))
    return pl.pallas_call(
        body, name=name, grid=(FOX_H, nb),
        in_specs=[pl.BlockSpec((T, HEAD), lambda h, i: (i, h)), head, head,
                  pl.BlockSpec((None, T, 1), lambda h, i: (h, i, 0)),
                  pl.BlockSpec((None, nb, 1, T), lambda h, i: (h, 0, 0, 0))] + more_specs,
        out_specs=[pl.BlockSpec((T, HEAD), lambda h, i: (i, h)), pl.BlockSpec((None, T, 1), lambda h, i: (h, i, 0))],
        out_shape=[jax.ShapeDtypeStruct((S, FOX_H * HEAD), F32), jax.ShapeDtypeStruct((FOX_H, S, 1), F32)],
        compiler_params=_params("parallel", "parallel"),
    )(qn, kn, vb, c_col, c_row, *more)


def _fox_bwd(qn, kn, vb, c_col, c_row, lse, o, do, name):
    S = qn.shape[0]
    T = min(ATT_T, S)
    Th = T // 2
    nb = S // T
    scale = 1.0 / math.sqrt(HEAD)

    def body(q_ref, k_ref, v_ref, cc_ref, cr_ref, lse_ref, o_ref, do_ref, dq_ref, dk_ref, dv_ref, dcc_ref, dcr_ref):
        j = pl.program_id(1)

        @pl.when(j == 0)
        def _():
            dq_ref[...] = jnp.zeros_like(dq_ref)
            dcc_ref[...] = jnp.zeros_like(dcc_ref)

        k = k_ref[...]
        v = v_ref[...]
        cr = cr_ref[...]

        def tile(sl, kk, vv, crr, carry, masked):
            dk, dv, dcr = carry
            q = q_ref[sl, :]
            d_o = do_ref[sl, :]
            s = _dot(q, kk, NT) * scale + cc_ref[sl, :] - crr
            if masked:
                s = jnp.where(lax.broadcasted_iota(jnp.int32, s.shape, 0) >= lax.broadcasted_iota(jnp.int32, s.shape, 1), s, NEG)
            p = jnp.exp(s - lse_ref[sl, :])
            dob = d_o.astype(BF16)
            dp = _dot(dob, vv, NT)
            delta = jnp.sum(d_o * o_ref[sl, :], axis=1, keepdims=True)
            ds = p * (dp - delta)
            dsb = ds.astype(BF16)
            dq_ref[sl, :] += _dot(dsb, kk) * scale
            dcc_ref[sl, :] += jnp.sum(ds, axis=1, keepdims=True)
            return (dk + _dot(dsb, q, TN), dv + _dot(p.astype(BF16), dob, TN), dcr + jnp.sum(ds, axis=0, keepdims=True))

        def step(i, carry):
            return tile(pl.ds(pl.multiple_of(i * T, T), T), k, v, cr, carry, False)

        def rows(r0):
            return pl.ds(pl.multiple_of(j * T + r0, Th), Th)

        zero = (jnp.zeros((Th, HEAD), F32), jnp.zeros((Th, HEAD), F32), jnp.zeros((1, Th), F32))
        left = tile(rows(Th), k[:Th], v[:Th], cr[:, :Th], tile(rows(0), k[:Th], v[:Th], cr[:, :Th], zero, True), False)
        right = tile(rows(Th), k[Th:], v[Th:], cr[:, Th:], zero, True)
        diagonal = (jnp.concatenate([left[0], right[0]], axis=0), jnp.concatenate([left[1], right[1]], axis=0),
                    jnp.concatenate([left[2], right[2]], axis=1))
        dk, dv, dcr = lax.fori_loop(j + 1, nb, step, diagonal)
        dk_ref[...] = dk * scale
        dv_ref[...] = dv.astype(BF16)
        dcr_ref[...] = dcr

    head = pl.BlockSpec((S, HEAD), lambda h, j: (0, h))
    tile = pl.BlockSpec((T, HEAD), lambda h, j: (j, h))
    col = pl.BlockSpec((None, S, 1), lambda h, j: (h, 0, 0))
    row = pl.BlockSpec((None, None, 1, T), lambda h, j: (h, j, 0, 0))
    w = FOX_H * HEAD
    return pl.pallas_call(
        body, name=name, grid=(FOX_H, nb),
        in_specs=[head, tile, tile, col, row, col, head, head],
        out_specs=[head, tile, tile, col, row],
        out_shape=[jax.ShapeDtypeStruct((S, w), F32), jax.ShapeDtypeStruct((S, w), F32), jax.ShapeDtypeStruct((S, w), BF16),
                   jax.ShapeDtypeStruct((FOX_H, S, 1), F32), jax.ShapeDtypeStruct((FOX_H, nb, 1, T), F32)],
        compiler_params=_params("arbitrary", "arbitrary"),
    )(qn, kn, vb, c_col, c_row, lse, o, do)


LRU_STEP_BLOCKS = 2


def _block_dot(x, w, dn=NN):
    return jnp.concatenate([_dot(x[:, b * HEAD:(b + 1) * HEAD], w[b], dn) for b in range(w.shape[0])], axis=1)


def _lru_gates(uc, wra, bra, wri, bri, lam):
    ucb = uc.astype(BF16)
    r = jax.nn.sigmoid(_block_dot(ucb, wra.astype(BF16)) + bra)
    ig = jax.nn.sigmoid(_block_dot(ucb, wri.astype(BF16)) + bri)
    sp = jnp.maximum(-lam, 0.0) + jnp.log1p(jnp.exp(-jnp.abs(lam)))
    log_a = -LRU_C * r * sp
    a = jnp.exp(log_a)
    mult = jnp.sqrt(_one_minus_exp(2.0 * log_a))
    return r, ig, sp, a, mult


def _conv(pad_ref, cw, cb, S):
    uc = cb
    for j in range(CONV_K):
        uc = uc + cw[j:j + 1, :] * pad_ref[5 + j:5 + j + S, :]
    return uc


def _lru_specs(S, G):
    bw = G * HEAD
    u_col = 3 * FOX_H // G
    g_col = u_col + LRU_NB // G
    blk = pl.BlockSpec((S, bw), lambda n: (0, n))
    vec = pl.BlockSpec((1, bw), lambda n: (0, n))
    mat = pl.BlockSpec((G, HEAD, HEAD), lambda n: (n, 0, 0))
    return dict(
        u=pl.BlockSpec((S, bw), lambda n: (0, u_col + n)), gate=pl.BlockSpec((S, bw), lambda n: (0, g_col + n)),
        blk=blk, vec=vec, mat=mat, cw=pl.BlockSpec((CONV_K, bw), lambda n: (0, n)))


def _lru_fwd(proj, conv_w, conv_b, w_ra, b_ra, w_ri, b_ri, lam, name, after=None):
    S = proj.shape[0]
    G = LRU_STEP_BLOCKS
    bw = G * HEAD
    sp_ = _lru_specs(S, G)
    rows8 = S // 8
    more_specs, more = _behind(after)

    def body(u_ref, gt_ref, cw_ref, cb_ref, wra_ref, bra_ref, wri_ref, bri_ref, lam_ref, *rest):
        y_ref, h_ref, pad, a_scr, b_scr = rest[len(more):]
        pad[0:8, :] = jnp.zeros((8, bw), F32)
        pad[8:S + 8, :] = u_ref[...]
        uc = _conv(pad, cw_ref[...], cb_ref[...], S)
        r, ig, sp, a, mult = _lru_gates(uc, wra_ref[...], bra_ref[...], wri_ref[...], bri_ref[...], lam_ref[...])
        a_scr[...] = a
        b_scr[...] = mult * (ig * uc)
        sub = lax.broadcasted_iota(jnp.int32, (8, bw), 0)

        def step(t, carry):
            sl = pl.ds(pl.multiple_of(t * 8, 8), 8)
            A, B = a_scr[sl, :], b_scr[sl, :]
            for d in (1, 2, 4):
                m = sub >= d
                B = jnp.where(m, A * pltpu.roll(B, d, 0) + B, B)
                A = jnp.where(m, A * pltpu.roll(A, d, 0), A)
            h = A * carry + B
            h_ref[sl, :] = h
            return h[7:8, :]

        lax.fori_loop(0, rows8, step, jnp.zeros((1, bw), F32))
        y_ref[...] = h_ref[...] * _gelu(gt_ref[...])

    w = LRU_NB * HEAD
    return pl.pallas_call(
        body, name=name, grid=(LRU_NB // G,),
        in_specs=[sp_["u"], sp_["gate"], sp_["cw"], sp_["vec"], sp_["mat"], sp_["vec"], sp_["mat"], sp_["vec"], sp_["vec"]]
        + more_specs,
        out_specs=[sp_["blk"], sp_["blk"]],
        out_shape=[jax.ShapeDtypeStruct((S, w), F32)] * 2,
        scratch_shapes=[pltpu.VMEM((S + 8, bw), F32), pltpu.VMEM((S, bw), F32), pltpu.VMEM((S, bw), F32)],
        compiler_params=_params("parallel"),
    )(proj, proj, conv_w, conv_b, w_ra, b_ra, w_ri, b_ri, lam, *more)


def _lru_bwd(proj, h, dy, conv_w, conv_b, w_ra, b_ra, w_ri, b_ri, lam, name, after=None):
    S = proj.shape[0]
    G = LRU_STEP_BLOCKS
    bw = G * HEAD
    sp_ = _lru_specs(S, G)
    rows8 = S // 8
    more_specs, more = _behind(after)

    def body(u_ref, gt_ref, h_ref, dy_ref, cw_ref, cb_ref, wra_ref, bra_ref, wri_ref, bri_ref, lam_ref, *rest):
        (du_ref, dgt_ref, dcw_ref, dcb_ref, dwra_ref, dbra_ref, dwri_ref, dbri_ref, dlam_ref,
         pad, an_scr, d_scr, g_scr, hp_scr) = rest[len(more):]
        zero8 = jnp.zeros((8, bw), F32)
        pad[0:8, :] = zero8
        pad[8:S + 8, :] = u_ref[...]
        cw = cw_ref[...]
        uc = _conv(pad, cw, cb_ref[...], S)
        wra, wri, lam_v = wra_ref[...], wri_ref[...], lam_ref[...]
        r, ig, sp, a, mult = _lru_gates(uc, wra, bra_ref[...], wri, bri_ref[...], lam_v)
        gate = gt_ref[...]
        dy_v = dy_ref[...]
        hv = h_ref[...]
        dgt_ref[...] = (dy_v * hv * _gelu_grad(gate)).astype(BF16)
        d_scr[...] = dy_v * _gelu(gate)
        g_scr[0:S, :] = a
        g_scr[S:S + 8, :] = zero8
        an_scr[...] = g_scr[1:S + 1, :]
        sub = lax.broadcasted_iota(jnp.int32, (8, bw), 0)

        def step(t, carry):
            sl = pl.ds(pl.multiple_of((rows8 - 1 - t) * 8, 8), 8)
            A, D = an_scr[sl, :], d_scr[sl, :]
            for d in (1, 2, 4):
                m = sub + d <= 7
                D = jnp.where(m, A * pltpu.roll(D, 8 - d, 0) + D, D)
                A = jnp.where(m, A * pltpu.roll(A, 8 - d, 0), A)
            g = A * carry + D
            g_scr[sl, :] = g
            return g[0:1, :]

        lax.fori_loop(0, rows8, step, jnp.zeros((1, bw), F32))
        g = g_scr[0:S, :]
        hp_scr[0:8, :] = zero8
        hp_scr[8:S + 8, :] = hv
        da = g * hp_scr[7:S + 7, :]
        iu = ig * uc
        dmult = g * iu
        diu = g * mult
        dig = diu * uc
        duc = diu * ig
        dlog_a = da * a - dmult * (a * a) / mult
        dr = dlog_a * (-LRU_C * sp)
        dsp = jnp.sum(dlog_a * (-LRU_C * r), axis=0, keepdims=True)
        dlam_ref[...] = -dsp * jax.nn.sigmoid(-lam_v)
        dpr = dr * r * (1.0 - r)
        dpi = dig * ig * (1.0 - ig)
        dbra_ref[...] = jnp.sum(dpr, axis=0, keepdims=True)
        dbri_ref[...] = jnp.sum(dpi, axis=0, keepdims=True)
        ucb = uc.astype(BF16)
        dprb, dpib = dpr.astype(BF16), dpi.astype(BF16)
        for b in range(G):
            cols = slice(b * HEAD, (b + 1) * HEAD)
            dwra_ref[b] = _dot(ucb[:, cols], dprb[:, cols], TN).astype(BF16)
            dwri_ref[b] = _dot(ucb[:, cols], dpib[:, cols], TN).astype(BF16)
        duc = duc + _block_dot(dprb, wra.astype(BF16), NT) + _block_dot(dpib, wri.astype(BF16), NT)
        dcb_ref[...] = jnp.sum(duc, axis=0, keepdims=True)
        for j in range(CONV_K):
            dcw_ref[j:j + 1, :] = jnp.sum(duc * pad[5 + j:5 + j + S, :], axis=0, keepdims=True)
        g_scr[0:S, :] = duc
        g_scr[S:S + 8, :] = zero8
        du = jnp.zeros((S, bw), F32)
        for j in range(CONV_K):
            du = du + cw[j:j + 1, :] * g_scr[3 - j:3 - j + S, :]
        du_ref[...] = du.astype(BF16)

    w = LRU_NB * HEAD
    bf = jax.ShapeDtypeStruct((S, w), BF16)
    vec = jax.ShapeDtypeStruct((1, w), F32)
    mat = jax.ShapeDtypeStruct((LRU_NB, HEAD, HEAD), BF16)
    return pl.pallas_call(
        body, name=name, grid=(LRU_NB // G,),
        in_specs=[sp_["u"], sp_["gate"], sp_["blk"], sp_["blk"], sp_["cw"], sp_["vec"], sp_["mat"], sp_["vec"],
                  sp_["mat"], sp_["vec"], sp_["vec"]] + more_specs,
        out_specs=[sp_["blk"], sp_["blk"], sp_["cw"], sp_["vec"], sp_["mat"], sp_["vec"], sp_["mat"], sp_["vec"], sp_["vec"]],
        out_shape=[bf, bf, jax.ShapeDtypeStruct((CONV_K, w), F32), vec, mat, vec, mat, vec, vec],
        scratch_shapes=[pltpu.VMEM((S + 8, bw), F32), pltpu.VMEM((S, bw), F32), pltpu.VMEM((S, bw), F32),
                        pltpu.VMEM((S + 8, bw), F32), pltpu.VMEM((S + 8, bw), F32)],
        compiler_params=pltpu.CompilerParams(dimension_semantics=("parallel",), vmem_limit_bytes=V7X_VMEM_LIMIT_HIGH),
    )(proj, proj, h, dy, conv_w, conv_b, w_ra, b_ra, w_ri, b_ri, lam, *more)


def _xattn_fwd(cq, ckv, g_cq, g_ck, name):
    S, w = cq.shape
    M = ckv.shape[0]
    tr = _row_tile(S)
    scale = 1.0 / math.sqrt(HEAD)

    def body(cq_ref, ckv_ref, gq_ref, gk_ref, o_ref):
        for h in range(XATT_H):
            sl = slice(h * HEAD, (h + 1) * HEAD)
            qn = _rms(cq_ref[:, sl], gq_ref[...]).astype(BF16)
            kn = _rms(ckv_ref[:, sl], gk_ref[...]).astype(BF16)
            v = ckv_ref[:, w + h * HEAD:w + (h + 1) * HEAD].astype(BF16)
            s = _dot(qn, kn, NT) * scale
            p = jnp.exp(s - jnp.max(s, axis=1, keepdims=True))
            p = p / jnp.sum(p, axis=1, keepdims=True)
            o_ref[:, sl] = _dot(p.astype(BF16), v).astype(BF16)

    gain = pl.BlockSpec((1, HEAD), lambda i: (0, 0))
    return pl.pallas_call(
        body, name=name, grid=(S // tr,),
        in_specs=[pl.BlockSpec((tr, w), lambda i: (i, 0)), pl.BlockSpec((M, 2 * w), lambda i: (0, 0)), gain, gain],
        out_specs=pl.BlockSpec((tr, w), lambda i: (i, 0)), out_shape=jax.ShapeDtypeStruct((S, w), BF16),
        compiler_params=_params("parallel"),
    )(cq, ckv, g_cq, g_ck)


def _xattn_bwd(cq, ckv, g_cq, g_ck, do, name):
    S, w = cq.shape
    M = ckv.shape[0]
    tr = _row_tile(S)
    nsteps = S // tr
    scale = 1.0 / math.sqrt(HEAD)

    def body(cq_ref, ckv_ref, gq_ref, gk_ref, do_ref, dcq_ref, dckv_ref, dgq_ref, dgk_ref, dkn_scr, dv_scr):
        step = pl.program_id(0)

        @pl.when(step == 0)
        def _():
            dkn_scr[...] = jnp.zeros_like(dkn_scr)
            dv_scr[...] = jnp.zeros_like(dv_scr)
            dgq_ref[...] = jnp.zeros_like(dgq_ref)

        for h in range(XATT_H):
            sl = slice(h * HEAD, (h + 1) * HEAD)
            q_raw = cq_ref[:, sl]
            qn = _rms(q_raw, gq_ref[...]).astype(BF16)
            kn = _rms(ckv_ref[:, sl], gk_ref[...]).astype(BF16)
            v = ckv_ref[:, w + h * HEAD:w + (h + 1) * HEAD].astype(BF16)
            s = _dot(qn, kn, NT) * scale
            p = jnp.exp(s - jnp.max(s, axis=1, keepdims=True))
            p = p / jnp.sum(p, axis=1, keepdims=True)
            dob = do_ref[:, sl].astype(BF16)
            dp = _dot(dob, v, NT)
            ds = p * (dp - jnp.sum(p * dp, axis=1, keepdims=True)) * scale
            dsb = ds.astype(BF16)
            dv_scr[:, sl] += _dot(p.astype(BF16), dob, TN)
            dkn_scr[:, sl] += _dot(dsb, qn, TN)
            dq, dgq = _rms_grad(q_raw, gq_ref[...], _dot(dsb, kn))
            dcq_ref[:, sl] = dq.astype(BF16)
            dgq_ref[...] += dgq

        @pl.when(step == nsteps - 1)
        def _():
            dgk = jnp.zeros((1, HEAD), F32)
            for h in range(XATT_H):
                sl = slice(h * HEAD, (h + 1) * HEAD)
                dk, dgk_h = _rms_grad(ckv_ref[:, sl], gk_ref[...], dkn_scr[:, sl])
                dckv_ref[:, sl] = dk.astype(BF16)
                dgk = dgk + dgk_h
            dckv_ref[:, w:2 * w] = dv_scr[...].astype(BF16)
            dgk_ref[...] = dgk

    gain = pl.BlockSpec((1, HEAD), lambda i: (0, 0))
    row = pl.BlockSpec((tr, w), lambda i: (i, 0))
    mem = pl.BlockSpec((M, 2 * w), lambda i: (0, 0))
    return pl.pallas_call(
        body, name=name, grid=(nsteps,), in_specs=[row, mem, gain, gain, row], out_specs=[row, mem, gain, gain],
        out_shape=[jax.ShapeDtypeStruct((S, w), BF16), jax.ShapeDtypeStruct((M, 2 * w), BF16),
                   jax.ShapeDtypeStruct((1, HEAD), F32), jax.ShapeDtypeStruct((1, HEAD), F32)],
        scratch_shapes=[pltpu.VMEM((M, w), F32), pltpu.VMEM((M, w), F32)],
        compiler_params=_params("arbitrary"),
    )(cq, ckv, g_cq, g_ck, do)


def _ffn_up(h, w_gu, name, tm=512):
    S, D = h.shape
    nb, _, bw = w_gu.shape
    half = nb // 2
    tm = min(tm, S)

    def body(h_ref, wg_ref, wu_ref, g_ref, u_ref, act_ref):
        hv = h_ref[...]
        g = _dot(hv, wg_ref[...])
        u = _dot(hv, wu_ref[...])
        g_ref[...] = g
        u_ref[...] = u
        act_ref[...] = (g * jax.nn.sigmoid(g) * u).astype(BF16)

    blk = pl.BlockSpec((tm, bw), lambda j, i: (i, j))
    return pl.pallas_call(
        body, name=name, grid=(half, S // tm),
        in_specs=[pl.BlockSpec((tm, D), lambda j, i: (i, 0)), pl.BlockSpec((None, D, bw), lambda j, i: (j, 0, 0)),
                  pl.BlockSpec((None, D, bw), lambda j, i: (j + half, 0, 0))],
        out_specs=[blk, blk, blk],
        out_shape=[jax.ShapeDtypeStruct((S, half * bw), F32)] * 2 + [jax.ShapeDtypeStruct((S, half * bw), BF16)],
        compiler_params=_params("parallel", "parallel"),
    )(h, w_gu, w_gu)


def _ffn_down_dx(dy, w_down, g, u, name, tr=1024, tc=512):
    S, f = g.shape
    w2 = 2 * f
    tr, tc = min(tr, S), min(tc, f)
    assert S % tr == 0 and f % tc == 0
    n_ct = f // tc
    steps = (S // tr) * n_ct

    def body(dy_ref, w_ref, g_ref, u_ref, o_hbm, buf, sems):
        i, j = pl.program_id(0), pl.program_id(1)
        step = i * n_ct + j
        slot = step % 2

        def put(half):
            cols = pl.ds(pl.multiple_of(half * f + j * tc, 128), tc)
            return pltpu.make_async_copy(buf.at[slot, half], o_hbm.at[pl.ds(pl.multiple_of(i * tr, 8), tr), cols],
                                         sems.at[slot, half])

        @pl.when(step >= 2)
        def _():
            put(0).wait()
            put(1).wait()

        dy_v = dy_ref[...]
        for c0 in range(0, tc, 256):
            cs = slice(c0, c0 + 256)
            g = g_ref[:, cs]
            sg = jax.nn.sigmoid(g)
            da = _dot(dy_v, w_ref[cs, :], NT)
            buf[slot, 0, :, cs] = (da * u_ref[:, cs] * (sg * (1.0 + g * (1.0 - sg)))).astype(BF16)
            buf[slot, 1, :, cs] = (da * (g * sg)).astype(BF16)
        put(0).start()
        put(1).start()

        @pl.when(step == steps - 1)
        def _():
            put(0).wait()
            put(1).wait()
            if steps > 1:
                for half in (0, 1):
                    pltpu.make_async_copy(buf.at[1 - slot, half], o_hbm.at[pl.ds(0, tr), pl.ds(0, tc)],
                                          sems.at[1 - slot, half]).wait()

    D = dy.shape[1]
    return pl.pallas_call(
        body, name=name, grid=(S // tr, n_ct),
        in_specs=[pl.BlockSpec((tr, D), lambda i, j: (i, 0)), pl.BlockSpec((tc, D), lambda i, j: (j, 0)),
                  pl.BlockSpec((tr, tc), lambda i, j: (i, j)), pl.BlockSpec((tr, tc), lambda i, j: (i, j))],
        out_specs=ANY, out_shape=jax.ShapeDtypeStruct((S, w2), BF16),
        scratch_shapes=[pltpu.VMEM((2, 2, tr, tc), BF16), pltpu.SemaphoreType.DMA((2, 2))],
        compiler_params=_params("arbitrary", "arbitrary"),
    )(dy, w_down, g, u)


def _adamw_math(w, g, m, v):
    m = ADAM_B1 * m + (1.0 - ADAM_B1) * g
    v = ADAM_B2 * v + (1.0 - ADAM_B2) * (g * g)
    m_hat = m / (1.0 - ADAM_B1 ** ADAM_STEP)
    v_hat = v / (1.0 - ADAM_B2 ** ADAM_STEP)
    delta = -ADAM_LR * (m_hat / (jnp.sqrt(v_hat) + ADAM_EPS) + ADAM_WD * w)
    return delta, m, v


def _tile2(R, C, elems):
    if R % 16 == 0:
        return _row_tile(R, cap=max(16, elems // C // 16 * 16)), C
    tc = 128
    while C % (2 * tc) == 0 and R * 2 * tc <= elems:
        tc *= 2
    return R, tc


def _reduce_adamw(own, land, chip, w, m, v, name):
    _, R, C = own.shape
    tr, tc = _tile2(R, C, 1 << (19 if w.ndim == 3 else 18))

    def body(s_ref, o_ref, l1_ref, l2_ref, l3_ref, w_ref, m_ref, v_ref, g_ref, d_ref, nm_ref, nv_ref):
        g = o_ref[...].astype(F32) + l1_ref[...].astype(F32) + l2_ref[...].astype(F32) + l3_ref[...].astype(F32)
        g_ref[...] = g
        d_ref[...], nm_ref[...], nv_ref[...] = _adamw_math(w_ref[...], g, m_ref[...], v_ref[...])

    def part(flip):
        return pl.BlockSpec((None, tr, tc), lambda i, j, s: (s[0] ^ flip, i, j))

    if w.ndim == 3:
        blk = pl.BlockSpec((tr, None, tc), lambda i, j, s: (i, 0, j))
    else:
        blk = pl.BlockSpec((tr, tc), lambda i, j, s: (i, j))
    sds = jax.ShapeDtypeStruct(w.shape, F32)
    return pl.pallas_call(
        body, name=name,
        grid_spec=pltpu.PrefetchScalarGridSpec(
            num_scalar_prefetch=1, grid=(R // tr, C // tc),
            in_specs=[part(0), part(1), part(2), part(3), blk, blk, blk], out_specs=[blk] * 4),
        out_shape=[sds] * 4, compiler_params=_params("parallel", "parallel"),
    )(chip, own, land, land, land, w, m, v)


def _pair_sum(g8, recv, core, name):
    _, R, C = g8.shape
    tr, tc = _tile2(R, C, 1 << 21)

    def body(c_ref, a_ref, b_ref, o_ref):
        o_ref[...] = (a_ref[...].astype(F32) + b_ref[...].astype(F32)).astype(BF16)

    return pl.pallas_call(
        body, name=name,
        grid_spec=pltpu.PrefetchScalarGridSpec(
            num_scalar_prefetch=1, grid=(N_CHIP, R // tr, C // tc),
            in_specs=[pl.BlockSpec((None, tr, tc), lambda q, i, j, c: (2 * q + c[0], i, j)),
                      pl.BlockSpec((None, tr, tc), lambda q, i, j, c: (q, i, j))],
            out_specs=pl.BlockSpec((None, tr, tc), lambda q, i, j, c: (q, i, j))),
        out_shape=jax.ShapeDtypeStruct((N_CHIP, R, C), BF16),
        compiler_params=_params("parallel", "parallel", "parallel"),
    )(core, g8, recv)


def _place():
    return lax.axis_index("x"), lax.axis_index("y"), lax.axis_index("c")


def _land_with_own(shard, dev, name):
    R, C = shard.shape
    tr, tc = _tile2(R, C, 1 << 19)

    def body(d_ref, s_ref, o_ref):
        o_ref[...] = s_ref[...]

    return pl.pallas_call(
        body, name=name,
        grid_spec=pltpu.PrefetchScalarGridSpec(
            num_scalar_prefetch=1, grid=(R // tr, C // tc),
            in_specs=[pl.BlockSpec((tr, tc), lambda i, j, d: (i, j))],
            out_specs=pl.BlockSpec((None, tr, tc), lambda i, j, d: (d[0], i, j))),
        out_shape=jax.ShapeDtypeStruct((N_DEV, R, C), shard.dtype),
        compiler_params=_params("parallel", "parallel"),
    )(dev, shard)


def _all_gather(shards, lands, name):
    n = len(shards)

    def body(*refs):
        ins, outs = refs[:n], refs[2 * n:3 * n]
        send_sems, recv_sems = refs[3 * n:]
        x, y, c = _place()
        me, sibling = (x, y, c), (x, y, 1 - c)
        chips = [(1 - x, y), (x, 1 - y), (1 - x, 1 - y)]

        def copy(a, k, block, to, src=None):
            dst = outs[a].at[4 * block[0] + 2 * block[1] + block[2]]
            return pltpu.make_async_remote_copy(
                src_ref=dst if src is None else src, dst_ref=dst, send_sem=send_sems.at[a, k],
                recv_sem=recv_sems.at[a, k], device_id=to, device_id_type=MESH)

        sent = []
        for a in range(n):
            sent.append(copy(a, 0, me, sibling, src=ins[a]))
            sent += [copy(a, 1 + j, me, (*chip, c), src=ins[a]) for j, chip in enumerate(chips)]
        for cp in sent:
            cp.start()
        for j, chip in enumerate(chips):
            for a in range(n):
                copy(a, 1 + j, (*chip, c), me).wait_recv()
                fwd = copy(a, 4 + j, (*chip, c), sibling)
                fwd.start()
                sent.append(fwd)
        for a in range(n):
            copy(a, 0, sibling, me).wait_recv()
        for j, chip in enumerate(chips):
            for a in range(n):
                copy(a, 4 + j, (*chip, 1 - c), me).wait_recv()
        for cp in sent:
            cp.wait_send()

    return pl.pallas_call(
        body, name=name, in_specs=[ANY] * (2 * n), out_specs=[ANY] * n,
        out_shape=[jax.ShapeDtypeStruct(l.shape, l.dtype) for l in lands],
        input_output_aliases={n + a: a for a in range(n)},
        scratch_shapes=[pltpu.SemaphoreType.DMA((n, 7)), pltpu.SemaphoreType.DMA((n, 7))],
    )(*shards, *lands)


def _pair_plan(srcs, lands):
    x, y, c = _place()
    plan = []
    for a in range(len(srcs)):
        for q in range(N_CHIP):
            plan.append((srcs[a].at[2 * q + 1 - c], lands[a].at[q], lands[a].at[q], (x, y, 1 - c)))
    return plan


def _chip_plan(srcs, lands):
    x, y, c = _place()
    mine = 2 * x + y
    plan = []
    for a in range(len(srcs)):
        for px, py in [(1 - x, y), (x, 1 - y), (1 - x, 1 - y)]:
            peer = 2 * px + py
            plan.append((srcs[a].at[peer], lands[a].at[mine], lands[a].at[peer], (px, py, c)))
    return plan


def _gather_plan(srcs, lands):
    x, y, c = _place()
    mine = 4 * x + 2 * y + c
    plan = []
    for a in range(len(srcs)):
        for px, py, pc in [(x, y, 1 - c), (1 - x, y, c), (x, 1 - y, c), (1 - x, 1 - y, c)]:
            plan.append((srcs[a], lands[a].at[mine], lands[a].at[4 * px + 2 * py + pc], (px, py, pc)))
    return plan


def _forward_plan(srcs, lands):
    x, y, c = _place()
    plan = []
    for a in range(len(lands)):
        for px, py in [(1 - x, y), (x, 1 - y), (1 - x, 1 - y)]:
            mine, theirs = 4 * px + 2 * py + c, 4 * px + 2 * py + 1 - c
            plan.append((lands[a].at[mine], lands[a].at[mine], lands[a].at[theirs], (x, y, 1 - c)))
    return plan


def _remote(src, dst, send_sem, recv_sem, peer):
    return pltpu.make_async_remote_copy(src_ref=src, dst_ref=dst, send_sem=send_sem, recv_sem=recv_sem,
                                        device_id=peer, device_id_type=MESH)


HBM = pl.BlockSpec(memory_space=pltpu.HBM)
SEMS = pl.BlockSpec(memory_space=pltpu.SEMAPHORE)
DATAFLOW = pltpu.SideEffectType.DATAFLOW_SIDE_EFFECTING


def _split_start(srcs, lands, after, plan_fn, per_array, name):
    ns, nb = len(srcs), len(srcs) + len(lands)
    ncopy = per_array * len(lands)
    after = list(after) if isinstance(after, (list, tuple)) else [after]

    def body(*refs):
        send_sems, recv_sems = refs[nb + len(after)], refs[nb + len(after) + 1]
        token = refs[-1]
        for k, (src, dst, _, peer) in enumerate(plan_fn(refs[:ns], refs[ns:nb])):
            _remote(src, dst, send_sems.at[k], recv_sems.at[k], peer).start()
        token[...] = jnp.zeros_like(token)

    thru = [pltpu.HBM(a.shape, a.dtype) for a in (*srcs, *lands)]
    out = pl.pallas_call(
        body, name=name,
        out_shape=(pltpu.SemaphoreType.DMA((ncopy,)), pltpu.SemaphoreType.DMA((ncopy,)), *thru,
                   jax.ShapeDtypeStruct((8, 128), F32)),
        in_specs=[HBM] * nb + [ANY] * len(after),
        out_specs=(SEMS, SEMS, *[HBM] * nb, pl.BlockSpec(memory_space=pltpu.VMEM)),
        input_output_aliases={i: 2 + i for i in range(nb)},
        compiler_params=pltpu.CompilerParams(has_side_effects=DATAFLOW),
    )(*[pltpu.with_memory_space_constraint(a, pltpu.HBM) for a in (*srcs, *lands)], *after)
    return out[0], out[1], list(out[2:2 + ns]), list(out[2 + ns:2 + nb]), out[-1]


def _split_wait(send_sems, recv_sems, srcs, lands, after, plan_fn, name):
    ns, nb = len(srcs), len(srcs) + len(lands)
    after = list(after) if isinstance(after, (list, tuple)) else [after]

    def body(*refs):
        send_ref, recv_ref = refs[nb], refs[nb + 1]
        for k, (src, dst, mine, peer) in enumerate(plan_fn(refs[:ns], refs[ns:nb])):
            _remote(src, dst, send_ref.at[k], recv_ref.at[k], peer).wait_send()
            _remote(src, mine, send_ref.at[k], recv_ref.at[k], peer).wait_recv()

    thru = [pltpu.HBM(a.shape, a.dtype) for a in (*srcs, *lands)]
    out = pl.pallas_call(
        body, name=name, out_shape=tuple(thru),
        in_specs=[HBM] * nb + [SEMS, SEMS] + [ANY] * len(after), out_specs=tuple([HBM] * nb),
        input_output_aliases={i: i for i in range(nb)},
        compiler_params=pltpu.CompilerParams(has_side_effects=DATAFLOW),
    )(*srcs, *lands, send_sems, recv_sems, *after)
    return list(out[:ns]), list(out[ns:])


def _gather_forward(lands, name):
    n = len(lands)

    def body(*refs):
        landed, out = refs[:n], refs[n:2 * n]
        send_sems, recv_sems = refs[2 * n:]
        x, y, c = _place()
        chips = [(1 - x, y), (x, 1 - y), (1 - x, 1 - y)]
        sent = []
        for a in range(n):
            for j, (px, py) in enumerate(chips):
                blk = 4 * px + 2 * py + c
                sent.append(_remote(landed[a].at[blk], out[a].at[blk], send_sems.at[a, j], recv_sems.at[a, j], (x, y, 1 - c)))
        for cp in sent:
            cp.start()
        for a in range(n):
            for j, (px, py) in enumerate(chips):
                blk = 4 * px + 2 * py + 1 - c
                _remote(landed[a].at[blk], out[a].at[blk], send_sems.at[a, j], recv_sems.at[a, j], (x, y, 1 - c)).wait_recv()
        for cp in sent:
            cp.wait_send()

    return pl.pallas_call(
        body, name=name, in_specs=[ANY] * n, out_specs=[ANY] * n,
        out_shape=[jax.ShapeDtypeStruct(l.shape, l.dtype) for l in lands],
        input_output_aliases={a: a for a in range(n)},
        scratch_shapes=[pltpu.SemaphoreType.DMA((n, 3)), pltpu.SemaphoreType.DMA((n, 3))],
    )(*lands)


def _pack_rows(groups, name):
    n = len(groups[0])
    rows = -(-n // 8) * 8
    width = max(v.shape[1] for v in groups[0])

    def body(*refs):
        ins, outs = refs[:n * len(groups)], refs[n * len(groups):]
        for gi, o_ref in enumerate(outs):
            o_ref[...] = jnp.zeros_like(o_ref)
            for r in range(n):
                v_ref = ins[gi * n + r]
                o_ref[r:r + 1, 0:v_ref.shape[1]] = v_ref[...]

    return pl.pallas_call(
        body, name=name, out_shape=[jax.ShapeDtypeStruct((rows, width), F32)] * len(groups), compiler_params=_params(),
    )(*[v for g in groups for v in g])


def _unpack_rows(packs, like, name):
    n = len(like)

    def body(*refs):
        ins, outs = refs[:len(packs)], refs[len(packs):]
        for pi, p_ref in enumerate(ins):
            for r in range(n):
                o_ref = outs[pi * n + r]
                o_ref[...] = p_ref[r:r + 1, 0:o_ref.shape[1]]

    flat = pl.pallas_call(
        body, name=name, out_shape=[jax.ShapeDtypeStruct(a.shape, F32) for _ in packs for a in like],
        compiler_params=_params(),
    )(*packs)
    return [flat[pi * n:(pi + 1) * n] for pi in range(len(packs))]


def _sum_adamw(parts, w, m, v, name, col=None):
    P, R, _ = parts.shape
    C = w.shape[1]
    tr = _row_tile(R, cap=512, mult=8)

    def body(c_ref, p_ref, w_ref, m_ref, v_ref, g_ref, d_ref, nm_ref, nv_ref):
        g = p_ref[0].astype(F32)
        for q in range(1, P):
            g = g + p_ref[q].astype(F32)
        g_ref[...] = g
        d_ref[...], nm_ref[...], nv_ref[...] = _adamw_math(w_ref[...], g, m_ref[...], v_ref[...])

    blk = pl.BlockSpec((tr, C), lambda i, c: (i, 0))
    sds = jax.ShapeDtypeStruct((R, C), F32)
    at = jnp.zeros((1,), jnp.int32) if col is None else col
    return pl.pallas_call(
        body, name=name,
        grid_spec=pltpu.PrefetchScalarGridSpec(
            num_scalar_prefetch=1, grid=(R // tr,),
            in_specs=[pl.BlockSpec((P, tr, C), lambda i, c: (0, i, c[0])), blk, blk, blk], out_specs=[blk] * 4),
        out_shape=[sds] * 4, compiler_params=_params("parallel"),
    )(at, parts, w, m, v)


def _local_step(x, mem, target, sp, W, arrived, want, done, paired):
    h0 = _rms_fwd([x], [sp["g_mix"]], "rms_mix")
    proj = _mm(h0, W["main_t"], "nt", F32, "proj_main", tm=2048, tn=512)
    f_logit = _mm(h0, W["f_t"], "nt", F32, "proj_f", tm=2048)
    qn, kn, vb = _qk_fwd(proj, sp["g_q"], sp["g_k"], "qk_norm")
    b_pad = jnp.pad(sp["b_f"], ((0, 0), (0, HEAD - FOX_H)))
    c_col, c_row = _fgate_fwd(f_logit, b_pad, "forget_cumsum")
    o_fox, lse = _fox_fwd(qn, kn, vb, c_col, c_row, "fox_fwd")
    lru_p = (sp["conv_w"], sp["conv_b"], sp["w_ra"], sp["b_ra"], sp["w_ri"], sp["b_ri"], sp["lam"])
    y_lru, h_lru = _lru_fwd(proj, *lru_p, "lru_fwd", after=arrived("attn", o_fox))
    w_out, w_cq, w_ckv, w_co = want("attn", y_lru)
    mix = _rms_fwd([o_fox, y_lru], [sp["g_fox_out"], sp["g_lru_out"]], "rms_mix_out")
    x1 = _mm(mix, w_out, "nn", F32, "out_proj", add=x)
    hq = _rms_fwd([x1], [sp["g_xattn"]], "rms_xattn")
    mn = _rms_fwd([mem], [sp["g_mem"]], "rms_mem")
    cq = _mm(hq, w_cq, "nn", F32, "xattn_q", tm=2048)
    ckv = _mm(mn, w_ckv, "nn", F32, "xattn_kv")
    o_x = _xattn_fwd(cq, ckv, sp["g_cq"], sp["g_ck"], "xattn_fwd")
    co_w = w_co.shape[2]
    x2 = _mm(o_x, w_co, "nn", F32, "xattn_out", add=x1, tm=2048, tn=co_w, after=arrived("gate_up", o_x))
    hf = _rms_fwd([x2], [sp["g_ffn"]], "rms_ffn")
    (w_gu,) = want("gate_up", hf)
    gu_w = w_gu.shape[2]
    f_gate, f_up, act = _ffn_up(hf, w_gu, "ffn_up")
    (w_down,) = want("down", arrived("down", act))
    x3 = _mm(act, w_down, "nn", F32, "ffn_down", add=x2, tk=w_down.shape[0] // 2)
    dy, dyb, sq = _loss_head(x3, target, "loss_head")
    g_down = _mm(act, dyb, "tn", BF16, "ffn_down_dw", tm=w_down.shape[0] // 4, tn=2048)
    dgu = _ffn_down_dx(dyb, w_down, f_gate, f_up, "ffn_down_dx")
    g_gu = _mm(hf, dgu, "tn", BF16, "ffn_up_dw", out_blocks=N_DEV, tn=gu_w)
    tok = done("ffn", [g_gu, g_down])
    dhf = _mm(dgu, w_gu, "nt", F32, "ffn_up_dx", tn=2048, tk=gu_w, after=tok)
    tok = paired("ffn", dhf)
    dx2, dx2b, dg_ffn = _rms_bwd(x2, sp["g_ffn"], dhf, "rms_ffn_bwd", res=dy, after=tok)
    d_ox = _mm(dx2b, w_co, "nt", F32, "xattn_out_dx", tm=2048, tk=co_w)
    g_co = _mm(o_x, dx2b, "tn", BF16, "xattn_out_dw", out_blocks=N_DEV, tn=co_w)
    dcq, dckv, dg_cq, dg_ck = _xattn_bwd(cq, ckv, sp["g_cq"], sp["g_ck"], d_ox, "xattn_bwd")
    dhq = _mm(dcq, w_cq, "nt", F32, "xattn_q_dx", tm=2048)
    g_cq = _mm(hq, dcq, "tn", BF16, "xattn_q_dw")
    dmn = _mm(dckv, w_ckv, "nt", F32, "xattn_kv_dx")
    g_ckv = _mm(mn, dckv, "tn", BF16, "xattn_kv_dw")
    (dg_mem,) = _rms_bwd(mem, sp["g_mem"], dmn, "rms_mem_bwd", want_dx=False)
    dx1, dx1b, dg_xattn = _rms_bwd(x1, sp["g_xattn"], dhq, "rms_xattn_bwd", res=dx2)
    dmix = _mm(dx1b, w_out, "nt", F32, "out_proj_dx")
    g_out = _mm(mix, dx1b, "tn", BF16, "out_proj_dw", tn=2048)
    tok = done("attn", [g_out, g_cq, g_ckv, g_co])
    do_fox, _, dg_fox_out = _rms_bwd(o_fox, sp["g_fox_out"], dmix, "rms_fox_out_bwd", dy_col=0, after=tok)
    dy_lru, _, dg_lru_out = _rms_bwd(y_lru, sp["g_lru_out"], dmix, "rms_lru_out_bwd", dy_col=1)
    tok = paired("attn", dy_lru)
    du, dgate, dconv_w, dconv_b, dw_ra, db_ra, dw_ri, db_ri, dlam = _lru_bwd(proj, h_lru, dy_lru, *lru_p, "lru_bwd", after=tok)
    dqn, dkn, dv, dc_col, dc_row = _fox_bwd(qn, kn, vb, c_col, c_row, lse, o_fox, do_fox, "fox_bwd")
    dq, dk, dg_q, dg_k = _qk_bwd(proj, sp["g_q"], sp["g_k"], dqn, dkn, "qk_norm_bwd")
    df, db_f = _fgate_bwd(f_logit, b_pad, dc_col, dc_row, "forget_cumsum_bwd")
    dproj = jnp.concatenate([dq, dk, dv, du, dgate], axis=1)
    g_main_t = _mm(dproj, h0, "tn", BF16, "proj_main_dw", tn=2048)
    g_f_t = _mm(df, h0, "tn", BF16, "proj_f_dw", tn=2048)
    tok = done("w_in", [g_main_t, g_f_t])
    dh_f = _mm(df, W["f_t"], "nn", F32, "proj_f_dx", tm=2048)
    dh0 = _mm(dproj, W["main_t"], "nn", F32, "proj_main_dx", add=dh_f, tk=dproj.shape[1] // 2, after=tok)
    tok = paired("w_in", dh0)
    grad_x, _, dg_mix = _rms_bwd(x, sp["g_mix"], dh0, "rms_mix_bwd", res=dx1, after=tok)
    small = dict(g_mix=dg_mix, b_f=db_f[:, :FOX_H], g_q=dg_q, g_k=dg_k, conv_w=dconv_w, conv_b=dconv_b, w_ra=dw_ra,
                 b_ra=db_ra, w_ri=dw_ri, b_ri=db_ri, lam=dlam, g_fox_out=dg_fox_out, g_lru_out=dg_lru_out,
                 g_xattn=dg_xattn, g_mem=dg_mem, g_cq=dg_cq, g_ck=dg_ck, g_ffn=dg_ffn)
    return sq, grad_x, small


BIG = ("w_in", "w_out", "w_cq", "w_ckv", "w_co", "w_gate_up", "w_down")
SMALL = ("g_mix", "b_f", "g_q", "g_k", "conv_b", "w_ra", "b_ra", "w_ri", "b_ri", "lam", "g_fox_out", "g_lru_out",
         "g_xattn", "g_mem", "g_cq", "g_ck", "g_ffn")
ORDER = ("g_mix", "w_in", "b_f", "g_q", "g_k", "conv_w", "conv_b", "w_ra", "b_ra", "w_ri", "b_ri", "lam", "g_fox_out",
         "g_lru_out", "w_out", "g_xattn", "g_mem", "w_cq", "w_ckv", "g_cq", "g_ck", "w_co", "g_ffn", "w_gate_up", "w_down")


def kernel(x, mem, g_mix, w_in, b_f, g_q, g_k, conv_w, conv_b, w_ra, b_ra, w_ri, b_ri, lam, g_fox_out, g_lru_out, w_out, g_xattn, g_mem, w_cq, w_ckv, g_cq, g_ck, w_co, g_ffn, w_gate_up, w_down, loss_target, m_g_mix, m_w_in, m_b_f, m_g_q, m_g_k, m_conv_w, m_conv_b, m_w_ra, m_b_ra, m_w_ri, m_b_ri, m_lam, m_g_fox_out, m_g_lru_out, m_w_out, m_g_xattn, m_g_mem, m_w_cq, m_w_ckv, m_g_cq, m_g_ck, m_w_co, m_g_ffn, m_w_gate_up, m_w_down, v_g_mix, v_w_in, v_b_f, v_g_q, v_g_k, v_conv_w, v_conv_b, v_w_ra, v_b_ra, v_w_ri, v_b_ri, v_lam, v_g_fox_out, v_g_lru_out, v_w_out, v_g_xattn, v_g_mem, v_w_cq, v_w_ckv, v_g_cq, v_g_ck, v_w_co, v_g_ffn, v_w_gate_up, v_w_down):
    given = dict(locals())
    w = {n: given[n] for n in ORDER}
    m = {n: given["m_" + n] for n in ORDER}
    v = {n: given["v_" + n] for n in ORDER}
    D = x.shape[2]
    fw = FOX_H * HEAD
    dev_index = 4 * lax.axis_index("x") + 2 * lax.axis_index("y") + lax.axis_index("c")
    dev = jnp.reshape(dev_index, (1,)).astype(jnp.int32)
    core = jnp.reshape(lax.axis_index("c"), (1,)).astype(jnp.int32)
    chip = jnp.reshape(2 * lax.axis_index("x") + lax.axis_index("y"), (1,)).astype(jnp.int32)

    def shard(d, n):
        return jnp.transpose(d[n], (2, 0, 1)) if n == "w_in" else d[n][0]

    def unshard(a, n):
        return jnp.transpose(a, (1, 2, 0)) if n == "w_in" else a[None]

    gather_groups = dict(attn=("w_out", "w_cq", "w_ckv", "w_co"), gate_up=("w_gate_up",), down=("w_down",))
    reduce_groups = dict(ffn=("w_gate_up", "w_down"), attn=("w_out", "w_cq", "w_ckv", "w_co"), w_in=("w_in",))
    column_blocked = ("w_co", "w_gate_up")
    flying = {}

    def land(a, tag):
        return _land_with_own(a, dev, "own_" + tag)

    def launch(group, after):
        shards = [shard(w, n).astype(BF16) for n in gather_groups[group]]
        lands = [land(s, n) for s, n in zip(shards, gather_groups[group])]
        flying[group] = _split_start(shards, lands, after, _gather_plan, 4, "gather_" + group + "_start")
        return flying[group][4]

    def arrived(group, after):
        send, recv, shards, lands, _ = flying.pop(group)
        _, lands = _split_wait(send, recv, shards, lands, after, _gather_plan, "gather_" + group + "_wait")
        flying[group] = _split_start([], lands, after, _forward_plan, 3, "gather_" + group + "_forward_start")
        if group == "attn":
            return launch("down", flying[group][4])
        return flying[group][4]

    def want(group, after):
        send, recv, _, lands, _ = flying.pop(group)
        _, full = _split_wait(send, recv, [], lands, after, _forward_plan, "gather_" + group + "_forward_wait")
        return [g if n in column_blocked else g.reshape(-1, g.shape[2]) for n, g in zip(gather_groups[group], full)]

    first = [shard(w, "w_in").astype(BF16).reshape(-1, D), conv_w[0]]
    g_in, g_conv = _all_gather(first, [land(first[0], "w_in"), land(first[1], "conv_w")], "gather_w_in")
    tok = launch("gate_up", launch("attn", g_conv))
    per = g_in.shape[1]
    f_k, f_lo = divmod(3 * fw, per)
    assert f_lo + FOX_H <= per
    wt_in = g_in.reshape(-1, D)
    W = dict(main_t=jnp.concatenate([wt_in[:3 * fw], wt_in[3 * fw + FOX_H:]], axis=0),
             f_t=jnp.pad(wt_in[3 * fw:3 * fw + FOX_H], ((0, HEAD - FOX_H), (0, 0))))
    sp = {n: w[n] for n in SMALL if n not in ("w_ra", "w_ri")}
    sp["w_ra"], sp["w_ri"] = w_ra[0], w_ri[0]
    sp["conv_w"] = jnp.transpose(g_conv, (1, 0, 2)).reshape(CONV_K, -1)
    sp["g_mix"] = sp["g_mix"] + tok[0, 0]

    pairing, reducing = {}, {}

    def done(group, grads):
        if group == "w_in":
            g_main_t, g_f_t = grads
            shards = [g_main_t[k * per:(k + 1) * per] for k in range(f_k)]
            shards.append(jnp.concatenate([g_main_t[f_k * per:3 * fw], g_f_t[:FOX_H],
                                           g_main_t[3 * fw:(f_k + 1) * per - FOX_H]], axis=0))
            shards += [g_main_t[k * per - FOX_H:(k + 1) * per - FOX_H] for k in range(f_k + 1, N_DEV)]
            grads = [jnp.stack(shards)]
        g8 = [g if g.ndim == 3 else g.reshape(N_DEV, -1, g.shape[1]) for g in grads]
        lands = [lax.empty((N_CHIP,) + g.shape[1:], g.dtype) for g in g8]
        pairing[group] = _split_start(g8, lands, chip, _pair_plan, N_CHIP, "reduce_" + group + "_pair_start")
        return pairing[group][4]

    def paired(group, after):
        send, recv, g8, lands, _ = pairing.pop(group)
        g8, from_sibling = _split_wait(send, recv, g8, lands, after, _pair_plan, "reduce_" + group + "_pair_wait")
        p4 = [_pair_sum(g, r, core, "reduce_pair_sum_" + n) for g, r, n in zip(g8, from_sibling, reduce_groups[group])]
        lands = [lax.empty(p.shape, p.dtype) for p in p4]
        reducing[group] = _split_start(p4, lands, chip, _chip_plan, 3, "reduce_" + group + "_start")
        return reducing[group][4]

    def finish(group, after):
        send, recv, p4, lands, _ = reducing.pop(group)
        p4, lands = _split_wait(send, recv, p4, lands, after, _chip_plan, "reduce_" + group + "_wait")
        return {n: tuple(unshard(r, n) for r in _reduce_adamw(p, l, chip, shard(w, n), shard(m, n), shard(v, n), "adamw_" + n))
                for n, p, l in zip(reduce_groups[group], p4, lands)}

    sq, grad_x, gs = _local_step(x[0], mem[0], loss_target[0], sp, W, arrived, want, done, paired)

    vectors = tuple(n for n in SMALL if n not in ("w_ra", "w_ri"))
    mine = [_pack_rows([[gs[n] for n in vectors] + [sq[0:1]]], "pack_small_grads")[0], gs["w_ra"].reshape(-1, HEAD),
            gs["w_ri"].reshape(-1, HEAD), gs["conv_w"]]
    lands = [land(a, "small_grads_%d" % k) for k, a in enumerate(mine)]
    s_send, s_recv, s_src, s_land, tok = _split_start(mine, lands, grad_x, _gather_plan, 4, "gather_small_start")
    out = finish("ffn", tok)
    out.update(finish("attn", tok))
    updated = [r for n in reduce_groups["ffn"] + reduce_groups["attn"] for r in out[n]]
    _, s_land = _split_wait(s_send, s_recv, s_src, s_land, updated, _gather_plan, "gather_small_wait")
    all_vec, all_ra, all_ri, all_conv = _gather_forward(s_land, "gather_small_forward")

    nothing = jnp.zeros((1, HEAD), F32)
    state = _pack_rows([[d[n] for n in vectors] + [nothing] for d in (w, m, v)], "pack_small_state")
    upd = _sum_adamw(all_vec, *state, "adamw_small_vectors")
    per_vector = _unpack_rows(upd, [w[n] for n in vectors] + [nothing], "unpack_small")
    for r, n in enumerate(vectors):
        out[n] = tuple(per_vector[k][r] for k in range(4))
    loss = 0.5 * per_vector[0][len(vectors)][0, 0] / D
    for n, parts in (("w_ra", all_ra), ("w_ri", all_ri)):
        res = _sum_adamw(parts, *[d[n].reshape(-1, HEAD) for d in (w, m, v)], "adamw_" + n)
        out[n] = tuple(r.reshape(w[n].shape) for r in res)
    out["conv_w"] = tuple(r[None] for r in _sum_adamw(all_conv, conv_w[0], m_conv_w[0], v_conv_w[0], "adamw_conv_w", col=dev))
    out.update(finish("w_in", upd[1]))

    return (loss, grad_x[None], *[out[n][0] for n in ORDER], *[out[n][1] for n in ORDER],
            *[out[n][2] for n in ORDER], *[out[n][3] for n in ORDER])
```

```python
import functools
import math

import jax
import jax.numpy as jnp
from jax import lax
from jax.experimental import pallas as pl
from jax.experimental.pallas import tpu as pltpu

F32 = jnp.float32
BF16 = jnp.bfloat16
MESH = pl.DeviceIdType.MESH
N_DEV = 8
N_CHIP = 4

HEAD = 128
FOX_H = 8
XATT_H = 4
LRU_NB = 8
CONV_K = 4
LRU_C = 8.0
RMS_EPS = 1e-6
ATT_T = 1024
CUM_T = 256
ROW_T = 256
NEG = -1e30
V7X_VMEM_LIMIT = 48 * 1024 * 1024
V7X_VMEM_LIMIT_HIGH = 60 * 1024 * 1024

ADAM_LR = 0.001
ADAM_B1 = 0.9
ADAM_B2 = 0.999
ADAM_EPS = 1e-08
ADAM_WD = 0.01
ADAM_STEP = 10

NT = (((1,), (1,)), ((), ()))
TN = (((0,), (0,)), ((), ()))
NN = (((1,), (0,)), ((), ()))

ANY = pl.BlockSpec(memory_space=pl.ANY)


def _behind(after):
    return ([], []) if after is None else ([ANY], [after])


def _params(*sem):
    return pltpu.CompilerParams(dimension_semantics=sem or None, vmem_limit_bytes=V7X_VMEM_LIMIT)


def _dot(a, b, dn=NN):
    return lax.dot_general(a, b, dn, preferred_element_type=F32)


def _rms(x, g):
    r = lax.rsqrt(jnp.mean(x * x, axis=-1, keepdims=True) + RMS_EPS)
    return x * r * g


def _rms_grad(x, g, dy):
    r = lax.rsqrt(jnp.mean(x * x, axis=-1, keepdims=True) + RMS_EPS)
    xh = x * r
    dxh = dy * g
    dx = r * (dxh - xh * jnp.mean(dxh * xh, axis=-1, keepdims=True))
    return dx, jnp.sum(dy * xh, axis=0, keepdims=True)


def _gelu(x):
    k = math.sqrt(2.0 / math.pi)
    return 0.5 * x * (1.0 + jnp.tanh(k * (x + 0.044715 * (x * x * x))))


def _gelu_grad(x):
    k = math.sqrt(2.0 / math.pi)
    t = jnp.tanh(k * (x + 0.044715 * (x * x * x)))
    return 0.5 * (1.0 + t) + 0.5 * x * (1.0 - t * t) * k * (1.0 + 3.0 * 0.044715 * x * x)


def _one_minus_exp(z):
    series = -(z + 0.5 * z * z + (1.0 / 6.0) * z * z * z)
    return jnp.where(z > -1e-3, series, 1.0 - jnp.exp(z))


def _row_tile(rows, cap=ROW_T, mult=16):
    t = min(rows, cap)
    while rows % t or (t % mult and t != rows):
        t -= 1
    return t


def _mat_dims(z):
    return (z.shape[-2], z.shape[-1] * (z.shape[0] if z.ndim == 3 else 1))


def _mat_spec(arr, rblk, cblk, rsel, csel):
    if arr.ndim == 2:
        return pl.BlockSpec((rblk, cblk), lambda i, j, k: ((i, j, k)[rsel], (i, j, k)[csel]))
    nw = arr.shape[2]
    assert nw % cblk == 0, (arr.shape, cblk)
    per = nw // cblk
    return pl.BlockSpec((None, rblk, cblk),
                        lambda i, j, k: ((i, j, k)[csel] // per, (i, j, k)[rsel], (i, j, k)[csel] % per))


def _mm(a, b, mode, out_dtype, name, add=None, out_blocks=None, tm=1024, tn=1024, tk=None, after=None):
    ar, ac = _mat_dims(a)
    br, bc = _mat_dims(b)
    if mode == "nn":
        M, K, N = ar, ac, bc
        assert br == K
    elif mode == "nt":
        M, K, N = ar, ac, br
        assert bc == K
    else:
        M, K, N = ac, ar, bc
        assert br == K
    tm, tn = min(tm, M), min(tn, N)
    tk = K if tk is None or mode == "tn" else min(tk, K)
    assert M % tm == 0 and N % tn == 0 and K % tk == 0, (name, M, N, K, tm, tn, tk)
    nk = K // tk
    nj = N // tn
    if mode == "nn":
        specs = [_mat_spec(a, tm, tk, 0, 2), _mat_spec(b, tk, tn, 2, 1)]
        dn = NN
    elif mode == "nt":
        specs = [_mat_spec(a, tm, tk, 0, 2), _mat_spec(b, tn, tk, 1, 2)]
        dn = NT
    else:
        specs = [_mat_spec(a, tk, tm, 2, 0), _mat_spec(b, tk, tn, 2, 1)]
        dn = TN
    args = [a, b]
    if add is not None:
        specs.append(pl.BlockSpec((tm, tn), lambda i, j, k: (i, j)))
        args.append(add)
    specs += _behind(after)[0]
    args += _behind(after)[1]
    n_in = len(args)
    if out_blocks is None:
        out_shape = jax.ShapeDtypeStruct((M, N), out_dtype)
        out_spec = pl.BlockSpec((tm, tn), lambda i, j, k: (i, j))
    else:
        nb = out_blocks
        nw = N // nb
        assert nw % tn == 0
        per = nw // tn
        out_shape = jax.ShapeDtypeStruct((nb, M, nw), out_dtype)
        out_spec = pl.BlockSpec((None, tm, tn), lambda i, j, k: (j // per, i, j % per))
    keep_t = mode == "tn" and nj > 1
    scratch = []
    if nk > 1:
        scratch.append(pltpu.VMEM((tm, tn), F32))
    if keep_t:
        scratch.append(pltpu.VMEM((tm, tk), a.dtype))

    def body(*refs):
        a_ref, b_ref = refs[0], refs[1]
        add_ref = refs[2] if add is not None else None
        o_ref = refs[n_in]

        def finish(r):
            if add_ref is not None:
                r = r + add_ref[...]
            o_ref[...] = r.astype(out_dtype)

        if keep_t:
            at_ref = refs[-1]

            @pl.when(pl.program_id(1) == 0)
            def _():
                at_ref[...] = a_ref[...].T

            finish(_dot(at_ref[...], b_ref[...], NN))
        elif nk == 1:
            finish(_dot(a_ref[...], b_ref[...], dn))
        else:
            acc_ref = refs[n_in + 1]
            k = pl.program_id(2)

            @pl.when(k == 0)
            def _():
                acc_ref[...] = jnp.zeros_like(acc_ref)

            acc_ref[...] += _dot(a_ref[...], b_ref[...], dn)

            @pl.when(k == nk - 1)
            def _():
                finish(acc_ref[...])

    return pl.pallas_call(
        body, name=name, grid=(M // tm, nj, nk), in_specs=specs, out_specs=out_spec, out_shape=out_shape,
        scratch_shapes=scratch,
        compiler_params=_params("parallel", "arbitrary" if keep_t else "parallel", "arbitrary"),
    )(*args)


def _rms_fwd(xs, gs, name, after=None):
    n = len(xs)
    rows = xs[0].shape[0]
    widths = [x.shape[1] for x in xs]
    tr = _row_tile(rows)
    more_specs, more = _behind(after)

    def body(*refs):
        o_ref = refs[2 * n + len(more)]
        off = 0
        for t in range(n):
            o_ref[:, off:off + widths[t]] = _rms(refs[t][...], refs[n + t][...]).astype(BF16)
            off += widths[t]

    return pl.pallas_call(
        body, name=name, grid=(rows // tr,),
        in_specs=[pl.BlockSpec((tr, w), lambda i: (i, 0)) for w in widths]
        + [pl.BlockSpec((1, w), lambda i: (0, 0)) for w in widths] + more_specs,
        out_specs=pl.BlockSpec((tr, sum(widths)), lambda i: (i, 0)),
        out_shape=jax.ShapeDtypeStruct((rows, sum(widths)), BF16),
        compiler_params=_params("parallel"),
    )(*xs, *gs, *more)


def _rms_bwd(x, g, dy, name, dy_col=0, res=None, want_dx=True, want_bf16=True, after=None):
    rows, w = x.shape
    tr = _row_tile(rows)
    has_res = res is not None
    more_specs, more = _behind(after)

    def body(*refs):
        x_ref, g_ref, dy_ref = refs[:3]
        res_ref = refs[3] if has_res else None
        outs = refs[3 + has_res + len(more):]
        dg_ref = outs[-1]
        dx, dg = _rms_grad(x_ref[...], g_ref[...], dy_ref[...])

        @pl.when(pl.program_id(0) == 0)
        def _():
            dg_ref[...] = jnp.zeros_like(dg_ref)

        dg_ref[...] += dg
        if want_dx:
            if has_res:
                dx = dx + res_ref[...]
            outs[0][...] = dx
            if want_bf16:
                outs[1][...] = dx.astype(BF16)

    row_spec = pl.BlockSpec((tr, w), lambda i: (i, 0))
    in_specs = [row_spec, pl.BlockSpec((1, w), lambda i: (0, 0)), pl.BlockSpec((tr, w), lambda i: (i, dy_col))]
    args = [x, g, dy]
    if has_res:
        in_specs.append(row_spec)
        args.append(res)
    in_specs += more_specs
    args += more
    out_specs, out_shape = [], []
    if want_dx:
        out_specs += [row_spec] * (1 + want_bf16)
        out_shape += [jax.ShapeDtypeStruct((rows, w), F32)] + [jax.ShapeDtypeStruct((rows, w), BF16)] * want_bf16
    out_specs.append(pl.BlockSpec((1, w), lambda i: (0, 0)))
    out_shape.append(jax.ShapeDtypeStruct((1, w), F32))
    return pl.pallas_call(
        body, name=name, grid=(rows // tr,), in_specs=in_specs, out_specs=out_specs, out_shape=out_shape,
        compiler_params=_params("arbitrary"),
    )(*args)


def _loss_head(y, target, name):
    rows, w = y.shape
    tr = _row_tile(rows)

    def body(y_ref, t_ref, dy_ref, dyb_ref, acc_ref):
        e = y_ref[...] - t_ref[...]

        @pl.when(pl.program_id(0) == 0)
        def _():
            acc_ref[...] = jnp.zeros_like(acc_ref)

        acc_ref[...] += jnp.sum(e * e)
        dy = e * (1.0 / w)
        dy_ref[...] = dy
        dyb_ref[...] = dy.astype(BF16)

    row_spec = pl.BlockSpec((tr, w), lambda i: (i, 0))
    return pl.pallas_call(
        body, name=name, grid=(rows // tr,), in_specs=[row_spec, row_spec],
        out_specs=[row_spec, row_spec, pl.BlockSpec((8, 128), lambda i: (0, 0))],
        out_shape=[jax.ShapeDtypeStruct((rows, w), F32), jax.ShapeDtypeStruct((rows, w), BF16),
                   jax.ShapeDtypeStruct((8, 128), F32)],
        compiler_params=_params("arbitrary"),
    )(y, target)


def _qk_fwd(proj, g_q, g_k, name, after=None):
    rows = proj.shape[0]
    w = FOX_H * HEAD
    tr = _row_tile(rows)
    more_specs, more = _behind(after)

    def body(q_ref, k_ref, v_ref, gq_ref, gk_ref, *rest):
        qn_ref, kn_ref, vb_ref = rest[len(more):]
        for h in range(FOX_H):
            sl = slice(h * HEAD, (h + 1) * HEAD)
            qn_ref[:, sl] = _rms(q_ref[:, sl], gq_ref[...]).astype(BF16)
            kn_ref[:, sl] = _rms(k_ref[:, sl], gk_ref[...]).astype(BF16)
        vb_ref[...] = v_ref[...].astype(BF16)

    gain = pl.BlockSpec((1, HEAD), lambda i: (0, 0))
    out = pl.BlockSpec((tr, w), lambda i: (i, 0))
    return pl.pallas_call(
        body, name=name, grid=(rows // tr,),
        in_specs=[pl.BlockSpec((tr, w), lambda i: (i, 0)), pl.BlockSpec((tr, w), lambda i: (i, 1)),
                  pl.BlockSpec((tr, w), lambda i: (i, 2)), gain, gain] + more_specs,
        out_specs=[out, out, out], out_shape=[jax.ShapeDtypeStruct((rows, w), BF16)] * 3,
        compiler_params=_params("parallel"),
    )(proj, proj, proj, g_q, g_k, *more)


def _qk_bwd(proj, g_q, g_k, dqn, dkn, name):
    rows = proj.shape[0]
    w = FOX_H * HEAD
    tr = _row_tile(rows)

    def body(q_ref, k_ref, gq_ref, gk_ref, dqn_ref, dkn_ref, dq_ref, dk_ref, dgq_ref, dgk_ref):
        @pl.when(pl.program_id(0) == 0)
        def _():
            dgq_ref[...] = jnp.zeros_like(dgq_ref)
            dgk_ref[...] = jnp.zeros_like(dgk_ref)

        for h in range(FOX_H):
            sl = slice(h * HEAD, (h + 1) * HEAD)
            dq, dgq = _rms_grad(q_ref[:, sl], gq_ref[...], dqn_ref[:, sl])
            dk, dgk = _rms_grad(k_ref[:, sl], gk_ref[...], dkn_ref[:, sl])
            dq_ref[:, sl] = dq.astype(BF16)
            dk_ref[:, sl] = dk.astype(BF16)
            dgq_ref[...] += dgq
            dgk_ref[...] += dgk

    gain = pl.BlockSpec((1, HEAD), lambda i: (0, 0))
    row = pl.BlockSpec((tr, w), lambda i: (i, 0))
    return pl.pallas_call(
        body, name=name, grid=(rows // tr,),
        in_specs=[row, pl.BlockSpec((tr, w), lambda i: (i, 1)), gain, gain, row, row],
        out_specs=[row, row, gain, gain],
        out_shape=[jax.ShapeDtypeStruct((rows, w), BF16)] * 2 + [jax.ShapeDtypeStruct((1, HEAD), F32)] * 2,
        compiler_params=_params("arbitrary"),
    )(proj, proj, g_q, g_k, dqn, dkn)


def _fgate_fwd(f_logit, b_pad, name):
    S = f_logit.shape[0]
    T = min(CUM_T, S)
    nb = S // T
    RT = min(ATT_T, S)
    nbr = S // RT

    def body(f_ref, b_ref, col_ref, row_ref, c_scr):
        tri = (lax.broadcasted_iota(jnp.int32, (T, T), 0) >= lax.broadcasted_iota(jnp.int32, (T, T), 1)).astype(F32)
        carry = jnp.zeros((1, HEAD), F32)
        for blk in range(nb):
            z = f_ref[blk * T:(blk + 1) * T, :] + b_ref[...]
            lf = jnp.minimum(z, 0.0) - jnp.log1p(jnp.exp(-jnp.abs(z)))
            cb = jnp.dot(tri, lf, precision=lax.Precision.HIGHEST, preferred_element_type=F32) + carry
            c_scr[blk * T:(blk + 1) * T, :] = cb
            carry = cb[T - 1:T, :]
        c = c_scr[...]
        lane = lax.broadcasted_iota(jnp.int32, c.shape, 1)
        ct = c.T
        for h in range(FOX_H):
            col_ref[h] = jnp.sum(jnp.where(lane == h, c, 0.0), axis=1, keepdims=True)
            for jb in range(nbr):
                row_ref[h, jb] = ct[h:h + 1, jb * RT:(jb + 1) * RT]

    return pl.pallas_call(
        body, name=name,
        out_shape=[jax.ShapeDtypeStruct((FOX_H, S, 1), F32), jax.ShapeDtypeStruct((FOX_H, nbr, 1, RT), F32)],
        scratch_shapes=[pltpu.VMEM((S, HEAD), F32)], compiler_params=_params(),
    )(f_logit, b_pad)


def _fgate_bwd(f_logit, b_pad, dc_col, dc_row, name):
    S = f_logit.shape[0]
    T = min(CUM_T, S)
    nb = S // T
    RT = min(ATT_T, S)

    def body(f_ref, b_ref, dcol_ref, drow_ref, df_ref, db_ref, dc_scr, rt_scr):
        lane = lax.broadcasted_iota(jnp.int32, (S, HEAD), 1)
        sub = lax.broadcasted_iota(jnp.int32, (HEAD, RT), 0)
        dc = jnp.zeros((S, HEAD), F32)
        for h in range(FOX_H):
            dc = jnp.where(lane == h, dcol_ref[h], dc)
        for jb in range(S // RT):
            rt = jnp.zeros((HEAD, RT), F32)
            for h in range(FOX_H):
                rt = jnp.where(sub == h, drow_ref[h, jb], rt)
            rt_scr[jb * RT:(jb + 1) * RT, :] = rt.T
        dc_scr[...] = dc - rt_scr[...]
        tri = (lax.broadcasted_iota(jnp.int32, (T, T), 0) <= lax.broadcasted_iota(jnp.int32, (T, T), 1)).astype(F32)
        carry = jnp.zeros((1, HEAD), F32)
        db = jnp.zeros((1, HEAD), F32)
        for blk in reversed(range(nb)):
            rows = slice(blk * T, (blk + 1) * T)
            dlf = jnp.dot(tri, dc_scr[rows, :], precision=lax.Precision.HIGHEST, preferred_element_type=F32) + carry
            carry = dlf[0:1, :]
            z = f_ref[rows, :] + b_ref[...]
            df = dlf * jax.nn.sigmoid(-z)
            df_ref[rows, :] = df.astype(BF16)
            db = db + jnp.sum(df, axis=0, keepdims=True)
        db_ref[...] = db

    return pl.pallas_call(
        body, name=name,
        out_shape=[jax.ShapeDtypeStruct((S, HEAD), BF16), jax.ShapeDtypeStruct((1, HEAD), F32)],
        scratch_shapes=[pltpu.VMEM((S, HEAD), F32), pltpu.VMEM((S, HEAD), F32)], compiler_params=_params(),
    )(f_logit, b_pad, dc_col, dc_row)


def _fox_fwd(qn, kn, vb, c_col, c_row, name, after=None):
    S = qn.shape[0]
    T = min(ATT_T, S)
    nb = S // T
    scale = 1.0 / math.sqrt(HEAD)
    more_specs, more = _behind(after)

    def body(q_ref, k_ref, v_ref, cc_ref, cr_ref, *rest):
        o_ref, lse_ref = rest[len(more):]
        i = pl.program_id(1)
        q = q_ref[...]
        cc = cc_ref[...]

        def step(j, carry, diagonal=False):
            m, l, acc = carry
            sl = pl.ds(pl.multiple_of(j * T, T), T)
            s = _dot(q, k_ref[sl, :], NT) * scale + cc - cr_ref[j]
            if diagonal:
                s = jnp.where(lax.broadcasted_iota(jnp.int32, (T, T), 0) >= lax.broadcasted_iota(jnp.int32, (T, T), 1), s, NEG)
            m2 = jnp.maximum(m, jnp.max(s, axis=1, keepdims=True))
            p = jnp.exp(s - m2)
            al = jnp.exp(m - m2)
            return m2, al * l + jnp.sum(p, axis=1, keepdims=True), al * acc + _dot(p.astype(BF16), v_ref[sl, :])

        init = (jnp.full((T, 1), NEG, F32), jnp.zeros((T, 1), F32), jnp.zeros((T, HEAD), F32))
        m, l, acc = step(i, lax.fori_loop(0, i, step, init), diagonal=True)
        o_ref[...] = acc / l
        lse_ref[...] = m + jnp.log(l)

    head = pl.BlockSpec((S, HEAD), lambda h, i: (0, h))
    return pl.pallas_call(
        body, name=name, grid=(FOX_H, nb),
        in_specs=[pl.BlockSpec((T, HEAD), lambda h, i: (i, h)), head, head,
                  pl.BlockSpec((None, T, 1), lambda h, i: (h, i, 0)),
                  pl.BlockSpec((None, nb, 1, T), lambda h, i: (h, 0, 0, 0))] + more_specs,
        out_specs=[pl.BlockSpec((T, HEAD), lambda h, i: (i, h)), pl.BlockSpec((None, T, 1), lambda h, i: (h, i, 0))],
        out_shape=[jax.ShapeDtypeStruct((S, FOX_H * HEAD), F32), jax.ShapeDtypeStruct((FOX_H, S, 1), F32)],
        compiler_params=_params("parallel", "parallel"),
    )(qn, kn, vb, c_col, c_row, *more)


def _fox_bwd(qn, kn, vb, c_col, c_row, lse, o, do, name):
    S = qn.shape[0]
    T = min(ATT_T, S)
    Th = T // 2
    nb = S // T
    scale = 1.0 / math.sqrt(HEAD)

    def body(q_ref, k_ref, v_ref, cc_ref, cr_ref, lse_ref, o_ref, do_ref, dq_ref, dk_ref, dv_ref, dcc_ref, dcr_ref):
        j = pl.program_id(1)

        @pl.when(j == 0)
        def _():
            dq_ref[...] = jnp.zeros_like(dq_ref)
            dcc_ref[...] = jnp.zeros_like(dcc_ref)

        k = k_ref[...]
        v = v_ref[...]
        cr = cr_ref[...]

        def tile(sl, kk, vv, crr, carry, masked):
            dk, dv, dcr = carry
            q = q_ref[sl, :]
            d_o = do_ref[sl, :]
            s = _dot(q, kk, NT) * scale + cc_ref[sl, :] - crr
            if masked:
                s = jnp.where(lax.broadcasted_iota(jnp.int32, s.shape, 0) >= lax.broadcasted_iota(jnp.int32, s.shape, 1), s, NEG)
            p = jnp.exp(s - lse_ref[sl, :])
            dob = d_o.astype(BF16)
            dp = _dot(dob, vv, NT)
            delta = jnp.sum(d_o * o_ref[sl, :], axis=1, keepdims=True)
            ds = p * (dp - delta)
            dsb = ds.astype(BF16)
            dq_ref[sl, :] += _dot(dsb, kk) * scale
            dcc_ref[sl, :] += jnp.sum(ds, axis=1, keepdims=True)
            return (dk + _dot(dsb, q, TN), dv + _dot(p.astype(BF16), dob, TN), dcr + jnp.sum(ds, axis=0, keepdims=True))

        def step(i, carry):
            return tile(pl.ds(pl.multiple_of(i * T, T), T), k, v, cr, carry, False)

        def rows(r0):
            return pl.ds(pl.multiple_of(j * T + r0, Th), Th)

        zero = (jnp.zeros((Th, HEAD), F32), jnp.zeros((Th, HEAD), F32), jnp.zeros((1, Th), F32))
        left = tile(rows(Th), k[:Th], v[:Th], cr[:, :Th], tile(rows(0), k[:Th], v[:Th], cr[:, :Th], zero, True), False)
        right = tile(rows(Th), k[Th:], v[Th:], cr[:, Th:], zero, True)
        diagonal = (jnp.concatenate([left[0], right[0]], axis=0), jnp.concatenate([left[1], right[1]], axis=0),
                    jnp.concatenate([left[2], right[2]], axis=1))
        dk, dv, dcr = lax.fori_loop(j + 1, nb, step, diagonal)
        dk_ref[...] = dk * scale
        dv_ref[...] = dv.astype(BF16)
        dcr_ref[...] = dcr

    head = pl.BlockSpec((S, HEAD), lambda h, j: (0, h))
    tile = pl.BlockSpec((T, HEAD), lambda h, j: (j, h))
    col = pl.BlockSpec((None, S, 1), lambda h, j: (h, 0, 0))
    row = pl.BlockSpec((None, None, 1, T), lambda h, j: (h, j, 0, 0))
    w = FOX_H * HEAD
    return pl.pallas_call(
        body, name=name, grid=(FOX_H, nb),
        in_specs=[head, tile, tile, col, row, col, head, head],
        out_specs=[head, tile, tile, col, row],
        out_shape=[jax.ShapeDtypeStruct((S, w), F32), jax.ShapeDtypeStruct((S, w), F32), jax.ShapeDtypeStruct((S, w), BF16),
                   jax.ShapeDtypeStruct((FOX_H, S, 1), F32), jax.ShapeDtypeStruct((FOX_H, nb, 1, T), F32)],
        compiler_params=_params("arbitrary", "arbitrary"),
    )(qn, kn, vb, c_col, c_row, lse, o, do)


LRU_STEP_BLOCKS = 2


def _block_dot(x, w, dn=NN):
    return jnp.concatenate([_dot(x[:, b * HEAD:(b + 1) * HEAD], w[b], dn) for b in range(w.shape[0])], axis=1)


def _lru_gates(uc, wra, bra, wri, bri, lam):
    ucb = uc.astype(BF16)
    r = jax.nn.sigmoid(_block_dot(ucb, wra.astype(BF16)) + bra)
    ig = jax.nn.sigmoid(_block_dot(ucb, wri.astype(BF16)) + bri)
    sp = jnp.maximum(-lam, 0.0) + jnp.log1p(jnp.exp(-jnp.abs(lam)))
    log_a = -LRU_C * r * sp
    a = jnp.exp(log_a)
    mult = jnp.sqrt(_one_minus_exp(2.0 * log_a))
    return r, ig, sp, a, mult


def _conv(pad_ref, cw, cb, S):
    uc = cb
    for j in range(CONV_K):
        uc = uc + cw[j:j + 1, :] * pad_ref[5 + j:5 + j + S, :]
    return uc


def _lru_specs(S, G):
    bw = G * HEAD
    u_col = 3 * FOX_H // G
    g_col = u_col + LRU_NB // G
    blk = pl.BlockSpec((S, bw), lambda n: (0, n))
    vec = pl.BlockSpec((1, bw), lambda n: (0, n))
    mat = pl.BlockSpec((G, HEAD, HEAD), lambda n: (n, 0, 0))
    return dict(
        u=pl.BlockSpec((S, bw), lambda n: (0, u_col + n)), gate=pl.BlockSpec((S, bw), lambda n: (0, g_col + n)),
        blk=blk, vec=vec, mat=mat, cw=pl.BlockSpec((CONV_K, bw), lambda n: (0, n)))


def _lru_fwd(proj, conv_w, conv_b, w_ra, b_ra, w_ri, b_ri, lam, name, after=None):
    S = proj.shape[0]
    G = LRU_STEP_BLOCKS
    bw = G * HEAD
    sp_ = _lru_specs(S, G)
    rows8 = S // 8
    more_specs, more = _behind(after)

    def body(u_ref, gt_ref, cw_ref, cb_ref, wra_ref, bra_ref, wri_ref, bri_ref, lam_ref, *rest):
        y_ref, h_ref, pad, a_scr, b_scr = rest[len(more):]
        pad[0:8, :] = jnp.zeros((8, bw), F32)
        pad[8:S + 8, :] = u_ref[...]
        uc = _conv(pad, cw_ref[...], cb_ref[...], S)
        r, ig, sp, a, mult = _lru_gates(uc, wra_ref[...], bra_ref[...], wri_ref[...], bri_ref[...], lam_ref[...])
        a_scr[...] = a
        b_scr[...] = mult * (ig * uc)
        sub = lax.broadcasted_iota(jnp.int32, (8, bw), 0)

        def step(t, carry):
            sl = pl.ds(pl.multiple_of(t * 8, 8), 8)
            A, B = a_scr[sl, :], b_scr[sl, :]
            for d in (1, 2, 4):
                m = sub >= d
                B = jnp.where(m, A * pltpu.roll(B, d, 0) + B, B)
                A = jnp.where(m, A * pltpu.roll(A, d, 0), A)
            h = A * carry + B
            h_ref[sl, :] = h
            return h[7:8, :]

        lax.fori_loop(0, rows8, step, jnp.zeros((1, bw), F32))
        y_ref[...] = h_ref[...] * _gelu(gt_ref[...])

    w = LRU_NB * HEAD
    return pl.pallas_call(
        body, name=name, grid=(LRU_NB // G,),
        in_specs=[sp_["u"], sp_["gate"], sp_["cw"], sp_["vec"], sp_["mat"], sp_["vec"], sp_["mat"], sp_["vec"], sp_["vec"]]
        + more_specs,
        out_specs=[sp_["blk"], sp_["blk"]],
        out_shape=[jax.ShapeDtypeStruct((S, w), F32)] * 2,
        scratch_shapes=[pltpu.VMEM((S + 8, bw), F32), pltpu.VMEM((S, bw), F32), pltpu.VMEM((S, bw), F32)],
        compiler_params=_params("parallel"),
    )(proj, proj, conv_w, conv_b, w_ra, b_ra, w_ri, b_ri, lam, *more)


def _lru_bwd(proj, h, dy, conv_w, conv_b, w_ra, b_ra, w_ri, b_ri, lam, name, after=None):
    S = proj.shape[0]
    G = LRU_STEP_BLOCKS
    bw = G * HEAD
    sp_ = _lru_specs(S, G)
    rows8 = S // 8
    more_specs, more = _behind(after)

    def body(u_ref, gt_ref, h_ref, dy_ref, cw_ref, cb_ref, wra_ref, bra_ref, wri_ref, bri_ref, lam_ref, *rest):
        (du_ref, dgt_ref, dcw_ref, dcb_ref, dwra_ref, dbra_ref, dwri_ref, dbri_ref, dlam_ref,
         pad, an_scr, d_scr, g_scr, hp_scr) = rest[len(more):]
        zero8 = jnp.zeros((8, bw), F32)
        pad[0:8, :] = zero8
        pad[8:S + 8, :] = u_ref[...]
        cw = cw_ref[...]
        uc = _conv(pad, cw, cb_ref[...], S)
        wra, wri, lam_v = wra_ref[...], wri_ref[...], lam_ref[...]
        r, ig, sp, a, mult = _lru_gates(uc, wra, bra_ref[...], wri, bri_ref[...], lam_v)
        gate = gt_ref[...]
        dy_v = dy_ref[...]
        hv = h_ref[...]
        dgt_ref[...] = (dy_v * hv * _gelu_grad(gate)).astype(BF16)
        d_scr[...] = dy_v * _gelu(gate)
        g_scr[0:S, :] = a
        g_scr[S:S + 8, :] = zero8
        an_scr[...] = g_scr[1:S + 1, :]
        sub = lax.broadcasted_iota(jnp.int32, (8, bw), 0)

        def step(t, carry):
            sl = pl.ds(pl.multiple_of((rows8 - 1 - t) * 8, 8), 8)
            A, D = an_scr[sl, :], d_scr[sl, :]
            for d in (1, 2, 4):
                m = sub + d <= 7
                D = jnp.where(m, A * pltpu.roll(D, 8 - d, 0) + D, D)
                A = jnp.where(m, A * pltpu.roll(A, 8 - d, 0), A)
            g = A * carry + D
            g_scr[sl, :] = g
            return g[0:1, :]

        lax.fori_loop(0, rows8, step, jnp.zeros((1, bw), F32))
        g = g_scr[0:S, :]
        hp_scr[0:8, :] = zero8
        hp_scr[8:S + 8, :] = hv
        da = g * hp_scr[7:S + 7, :]
        iu = ig * uc
        dmult = g * iu
        diu = g * mult
        dig = diu * uc
        duc = diu * ig
        dlog_a = da * a - dmult * (a * a) / mult
        dr = dlog_a * (-LRU_C * sp)
        dsp = jnp.sum(dlog_a * (-LRU_C * r), axis=0, keepdims=True)
        dlam_ref[...] = -dsp * jax.nn.sigmoid(-lam_v)
        dpr = dr * r * (1.0 - r)
        dpi = dig * ig * (1.0 - ig)
        dbra_ref[...] = jnp.sum(dpr, axis=0, keepdims=True)
        dbri_ref[...] = jnp.sum(dpi, axis=0, keepdims=True)
        ucb = uc.astype(BF16)
        dprb, dpib = dpr.astype(BF16), dpi.astype(BF16)
        for b in range(G):
            cols = slice(b * HEAD, (b + 1) * HEAD)
            dwra_ref[b] = _dot(ucb[:, cols], dprb[:, cols], TN).astype(BF16)
            dwri_ref[b] = _dot(ucb[:, cols], dpib[:, cols], TN).astype(BF16)
        duc = duc + _block_dot(dprb, wra.astype(BF16), NT) + _block_dot(dpib, wri.astype(BF16), NT)
        dcb_ref[...] = jnp.sum(duc, axis=0, keepdims=True)
        for j in range(CONV_K):
            dcw_ref[j:j + 1, :] = jnp.sum(duc * pad[5 + j:5 + j + S, :], axis=0, keepdims=True)
        g_scr[0:S, :] = duc
        g_scr[S:S + 8, :] = zero8
        du = jnp.zeros((S, bw), F32)
        for j in range(CONV_K):
            du = du + cw[j:j + 1, :] * g_scr[3 - j:3 - j + S, :]
        du_ref[...] = du.astype(BF16)

    w = LRU_NB * HEAD
    bf = jax.ShapeDtypeStruct((S, w), BF16)
    vec = jax.ShapeDtypeStruct((1, w), F32)
    mat = jax.ShapeDtypeStruct((LRU_NB, HEAD, HEAD), BF16)
    return pl.pallas_call(
        body, name=name, grid=(LRU_NB // G,),
        in_specs=[sp_["u"], sp_["gate"], sp_["blk"], sp_["blk"], sp_["cw"], sp_["vec"], sp_["mat"], sp_["vec"],
                  sp_["mat"], sp_["vec"], sp_["vec"]] + more_specs,
        out_specs=[sp_["blk"], sp_["blk"], sp_["cw"], sp_["vec"], sp_["mat"], sp_["vec"], sp_["mat"], sp_["vec"], sp_["vec"]],
        out_shape=[bf, bf, jax.ShapeDtypeStruct((CONV_K, w), F32), vec, mat, vec, mat, vec, vec],
        scratch_shapes=[pltpu.VMEM((S + 8, bw), F32), pltpu.VMEM((S, bw), F32), pltpu.VMEM((S, bw), F32),
                        pltpu.VMEM((S + 8, bw), F32), pltpu.VMEM((S + 8, bw), F32)],
        compiler_params=pltpu.CompilerParams(dimension_semantics=("parallel",), vmem_limit_bytes=V7X_VMEM_LIMIT_HIGH),
    )(proj, proj, h, dy, conv_w, conv_b, w_ra, b_ra, w_ri, b_ri, lam, *more)


def _xattn_fwd(cq, ckv, g_cq, g_ck, name):
    S, w = cq.shape
    M = ckv.shape[0]
    tr = _row_tile(S)
    scale = 1.0 / math.sqrt(HEAD)

    def body(cq_ref, ckv_ref, gq_ref, gk_ref, o_ref):
        for h in range(XATT_H):
            sl = slice(h * HEAD, (h + 1) * HEAD)
            qn = _rms(cq_ref[:, sl], gq_ref[...]).astype(BF16)
            kn = _rms(ckv_ref[:, sl], gk_ref[...]).astype(BF16)
            v = ckv_ref[:, w + h * HEAD:w + (h + 1) * HEAD].astype(BF16)
            s = _dot(qn, kn, NT) * scale
            p = jnp.exp(s - jnp.max(s, axis=1, keepdims=True))
            p = p / jnp.sum(p, axis=1, keepdims=True)
            o_ref[:, sl] = _dot(p.astype(BF16), v).astype(BF16)

    gain = pl.BlockSpec((1, HEAD), lambda i: (0, 0))
    return pl.pallas_call(
        body, name=name, grid=(S // tr,),
        in_specs=[pl.BlockSpec((tr, w), lambda i: (i, 0)), pl.BlockSpec((M, 2 * w), lambda i: (0, 0)), gain, gain],
        out_specs=pl.BlockSpec((tr, w), lambda i: (i, 0)), out_shape=jax.ShapeDtypeStruct((S, w), BF16),
        compiler_params=_params("parallel"),
    )(cq, ckv, g_cq, g_ck)


def _xattn_bwd(cq, ckv, g_cq, g_ck, do, name):
    S, w = cq.shape
    M = ckv.shape[0]
    tr = _row_tile(S)
    nsteps = S // tr
    scale = 1.0 / math.sqrt(HEAD)

    def body(cq_ref, ckv_ref, gq_ref, gk_ref, do_ref, dcq_ref, dckv_ref, dgq_ref, dgk_ref, dkn_scr, dv_scr):
        step = pl.program_id(0)

        @pl.when(step == 0)
        def _():
            dkn_scr[...] = jnp.zeros_like(dkn_scr)
            dv_scr[...] = jnp.zeros_like(dv_scr)
            dgq_ref[...] = jnp.zeros_like(dgq_ref)

        for h in range(XATT_H):
            sl = slice(h * HEAD, (h + 1) * HEAD)
            q_raw = cq_ref[:, sl]
            qn = _rms(q_raw, gq_ref[...]).astype(BF16)
            kn = _rms(ckv_ref[:, sl], gk_ref[...]).astype(BF16)
            v = ckv_ref[:, w + h * HEAD:w + (h + 1) * HEAD].astype(BF16)
            s = _dot(qn, kn, NT) * scale
            p = jnp.exp(s - jnp.max(s, axis=1, keepdims=True))
            p = p / jnp.sum(p, axis=1, keepdims=True)
            dob = do_ref[:, sl].astype(BF16)
            dp = _dot(dob, v, NT)
            ds = p * (dp - jnp.sum(p * dp, axis=1, keepdims=True)) * scale
            dsb = ds.astype(BF16)
            dv_scr[:, sl] += _dot(p.astype(BF16), dob, TN)
            dkn_scr[:, sl] += _dot(dsb, qn, TN)
            dq, dgq = _rms_grad(q_raw, gq_ref[...], _dot(dsb, kn))
            dcq_ref[:, sl] = dq.astype(BF16)
            dgq_ref[...] += dgq

        @pl.when(step == nsteps - 1)
        def _():
            dgk = jnp.zeros((1, HEAD), F32)
            for h in range(XATT_H):
                sl = slice(h * HEAD, (h + 1) * HEAD)
                dk, dgk_h = _rms_grad(ckv_ref[:, sl], gk_ref[...], dkn_scr[:, sl])
                dckv_ref[:, sl] = dk.astype(BF16)
                dgk = dgk + dgk_h
            dckv_ref[:, w:2 * w] = dv_scr[...].astype(BF16)
            dgk_ref[...] = dgk

    gain = pl.BlockSpec((1, HEAD), lambda i: (0, 0))
    row = pl.BlockSpec((tr, w), lambda i: (i, 0))
    mem = pl.BlockSpec((M, 2 * w), lambda i: (0, 0))
    return pl.pallas_call(
        body, name=name, grid=(nsteps,), in_specs=[row, mem, gain, gain, row], out_specs=[row, mem, gain, gain],
        out_shape=[jax.ShapeDtypeStruct((S, w), BF16), jax.ShapeDtypeStruct((M, 2 * w), BF16),
                   jax.ShapeDtypeStruct((1, HEAD), F32), jax.ShapeDtypeStruct((1, HEAD), F32)],
        scratch_shapes=[pltpu.VMEM((M, w), F32), pltpu.VMEM((M, w), F32)],
        compiler_params=_params("arbitrary"),
    )(cq, ckv, g_cq, g_ck, do)


def _ffn_up(h, w_gu, name, tm=512):
    S, D = h.shape
    nb, _, bw = w_gu.shape
    half = nb // 2
    tm = min(tm, S)

    def body(h_ref, wg_ref, wu_ref, g_ref, u_ref, act_ref):
        hv = h_ref[...]
        g = _dot(hv, wg_ref[...])
        u = _dot(hv, wu_ref[...])
        g_ref[...] = g
        u_ref[...] = u
        act_ref[...] = (g * jax.nn.sigmoid(g) * u).astype(BF16)

    blk = pl.BlockSpec((tm, bw), lambda j, i: (i, j))
    return pl.pallas_call(
        body, name=name, grid=(half, S // tm),
        in_specs=[pl.BlockSpec((tm, D), lambda j, i: (i, 0)), pl.BlockSpec((None, D, bw), lambda j, i: (j, 0, 0)),
                  pl.BlockSpec((None, D, bw), lambda j, i: (j + half, 0, 0))],
        out_specs=[blk, blk, blk],
        out_shape=[jax.ShapeDtypeStruct((S, half * bw), F32)] * 2 + [jax.ShapeDtypeStruct((S, half * bw), BF16)],
        compiler_params=_params("parallel", "parallel"),
    )(h, w_gu, w_gu)


def _ffn_down_dx(dy, w_down, g, u, name, tr=1024, tc=512):
    S, f = g.shape
    w2 = 2 * f
    tr, tc = min(tr, S), min(tc, f)
    assert S % tr == 0 and f % tc == 0
    n_ct = f // tc
    steps = (S // tr) * n_ct

    def body(dy_ref, w_ref, g_ref, u_ref, o_hbm, buf, sems):
        i, j = pl.program_id(0), pl.program_id(1)
        step = i * n_ct + j
        slot = step % 2

        def put(half):
            cols = pl.ds(pl.multiple_of(half * f + j * tc, 128), tc)
            return pltpu.make_async_copy(buf.at[slot, half], o_hbm.at[pl.ds(pl.multiple_of(i * tr, 8), tr), cols],
                                         sems.at[slot, half])

        @pl.when(step >= 2)
        def _():
            put(0).wait()
            put(1).wait()

        dy_v = dy_ref[...]
        for c0 in range(0, tc, 256):
            cs = slice(c0, c0 + 256)
            g = g_ref[:, cs]
            sg = jax.nn.sigmoid(g)
            da = _dot(dy_v, w_ref[cs, :], NT)
            buf[slot, 0, :, cs] = (da * u_ref[:, cs] * (sg * (1.0 + g * (1.0 - sg)))).astype(BF16)
            buf[slot, 1, :, cs] = (da * (g * sg)).astype(BF16)
        put(0).start()
        put(1).start()

        @pl.when(step == steps - 1)
        def _():
            put(0).wait()
            put(1).wait()
            if steps > 1:
                for half in (0, 1):
                    pltpu.make_async_copy(buf.at[1 - slot, half], o_hbm.at[pl.ds(0, tr), pl.ds(0, tc)],
                                          sems.at[1 - slot, half]).wait()

    D = dy.shape[1]
    return pl.pallas_call(
        body, name=name, grid=(S // tr, n_ct),
        in_specs=[pl.BlockSpec((tr, D), lambda i, j: (i, 0)), pl.BlockSpec((tc, D), lambda i, j: (j, 0)),
                  pl.BlockSpec((tr, tc), lambda i, j: (i, j)), pl.BlockSpec((tr, tc), lambda i, j: (i, j))],
        out_specs=ANY, out_shape=jax.ShapeDtypeStruct((S, w2), BF16),
        scratch_shapes=[pltpu.VMEM((2, 2, tr, tc), BF16), pltpu.SemaphoreType.DMA((2, 2))],
        compiler_params=_params("arbitrary", "arbitrary"),
    )(dy, w_down, g, u)


def _adamw_math(w, g, m, v):
    m = ADAM_B1 * m + (1.0 - ADAM_B1) * g
    v = ADAM_B2 * v + (1.0 - ADAM_B2) * (g * g)
    m_hat = m / (1.0 - ADAM_B1 ** ADAM_STEP)
    v_hat = v / (1.0 - ADAM_B2 ** ADAM_STEP)
    delta = -ADAM_LR * (m_hat / (jnp.sqrt(v_hat) + ADAM_EPS) + ADAM_WD * w)
    return delta, m, v


def _tile2(R, C, elems):
    if R % 16 == 0:
        return _row_tile(R, cap=max(16, elems // C // 16 * 16)), C
    tc = 128
    while C % (2 * tc) == 0 and R * 2 * tc <= elems:
        tc *= 2
    return R, tc


def _reduce_adamw(own, land, chip, w, m, v, name):
    _, R, C = own.shape
    tr, tc = _tile2(R, C, 1 << (19 if w.ndim == 3 else 18))

    def body(s_ref, o_ref, l1_ref, l2_ref, l3_ref, w_ref, m_ref, v_ref, g_ref, d_ref, nm_ref, nv_ref):
        g = o_ref[...].astype(F32) + l1_ref[...].astype(F32) + l2_ref[...].astype(F32) + l3_ref[...].astype(F32)
        g_ref[...] = g
        d_ref[...], nm_ref[...], nv_ref[...] = _adamw_math(w_ref[...], g, m_ref[...], v_ref[...])

    def part(flip):
        return pl.BlockSpec((None, tr, tc), lambda i, j, s: (s[0] ^ flip, i, j))

    if w.ndim == 3:
        blk = pl.BlockSpec((tr, None, tc), lambda i, j, s: (i, 0, j))
    else:
        blk = pl.BlockSpec((tr, tc), lambda i, j, s: (i, j))
    sds = jax.ShapeDtypeStruct(w.shape, F32)
    return pl.pallas_call(
        body, name=name,
        grid_spec=pltpu.PrefetchScalarGridSpec(
            num_scalar_prefetch=1, grid=(R // tr, C // tc),
            in_specs=[part(0), part(1), part(2), part(3), blk, blk, blk], out_specs=[blk] * 4),
        out_shape=[sds] * 4, compiler_params=_params("parallel", "parallel"),
    )(chip, own, land, land, land, w, m, v)


def _pair_sum(g8, recv, core, name):
    _, R, C = g8.shape
    tr, tc = _tile2(R, C, 1 << 21)

    def body(c_ref, a_ref, b_ref, o_ref):
        o_ref[...] = (a_ref[...].astype(F32) + b_ref[...].astype(F32)).astype(BF16)

    return pl.pallas_call(
        body, name=name,
        grid_spec=pltpu.PrefetchScalarGridSpec(
            num_scalar_prefetch=1, grid=(N_CHIP, R // tr, C // tc),
            in_specs=[pl.BlockSpec((None, tr, tc), lambda q, i, j, c: (2 * q + c[0], i, j)),
                      pl.BlockSpec((None, tr, tc), lambda q, i, j, c: (q, i, j))],
            out_specs=pl.BlockSpec((None, tr, tc), lambda q, i, j, c: (q, i, j))),
        out_shape=jax.ShapeDtypeStruct((N_CHIP, R, C), BF16),
        compiler_params=_params("parallel", "parallel", "parallel"),
    )(core, g8, recv)


def _place():
    return lax.axis_index("x"), lax.axis_index("y"), lax.axis_index("c")


def _land_with_own(shard, dev, name):
    R, C = shard.shape
    tr, tc = _tile2(R, C, 1 << 19)

    def body(d_ref, s_ref, o_ref):
        o_ref[...] = s_ref[...]

    return pl.pallas_call(
        body, name=name,
        grid_spec=pltpu.PrefetchScalarGridSpec(
            num_scalar_prefetch=1, grid=(R // tr, C // tc),
            in_specs=[pl.BlockSpec((tr, tc), lambda i, j, d: (i, j))],
            out_specs=pl.BlockSpec((None, tr, tc), lambda i, j, d: (d[0], i, j))),
        out_shape=jax.ShapeDtypeStruct((N_DEV, R, C), shard.dtype),
        compiler_params=_params("parallel", "parallel"),
    )(dev, shard)


def _all_gather(shards, lands, name):
    n = len(shards)

    def body(*refs):
        ins, outs = refs[:n], refs[2 * n:3 * n]
        send_sems, recv_sems = refs[3 * n:]
        x, y, c = _place()
        me, sibling = (x, y, c), (x, y, 1 - c)
        chips = [(1 - x, y), (x, 1 - y), (1 - x, 1 - y)]

        def copy(a, k, block, to, src=None):
            dst = outs[a].at[4 * block[0] + 2 * block[1] + block[2]]
            return pltpu.make_async_remote_copy(
                src_ref=dst if src is None else src, dst_ref=dst, send_sem=send_sems.at[a, k],
                recv_sem=recv_sems.at[a, k], device_id=to, device_id_type=MESH)

        sent = []
        for a in range(n):
            sent.append(copy(a, 0, me, sibling, src=ins[a]))
            sent += [copy(a, 1 + j, me, (*chip, c), src=ins[a]) for j, chip in enumerate(chips)]
        for cp in sent:
            cp.start()
        for j, chip in enumerate(chips):
            for a in range(n):
                copy(a, 1 + j, (*chip, c), me).wait_recv()
                fwd = copy(a, 4 + j, (*chip, c), sibling)
                fwd.start()
                sent.append(fwd)
        for a in range(n):
            copy(a, 0, sibling, me).wait_recv()
        for j, chip in enumerate(chips):
            for a in range(n):
                copy(a, 4 + j, (*chip, 1 - c), me).wait_recv()
        for cp in sent:
            cp.wait_send()

    return pl.pallas_call(
        body, name=name, in_specs=[ANY] * (2 * n), out_specs=[ANY] * n,
        out_shape=[jax.ShapeDtypeStruct(l.shape, l.dtype) for l in lands],
        input_output_aliases={n + a: a for a in range(n)},
        scratch_shapes=[pltpu.SemaphoreType.DMA((n, 7)), pltpu.SemaphoreType.DMA((n, 7))],
    )(*shards, *lands)


def _pair_plan(srcs, lands):
    x, y, c = _place()
    plan = []
    for a in range(len(srcs)):
        for q in range(N_CHIP):
            plan.append((srcs[a].at[2 * q + 1 - c], lands[a].at[q], lands[a].at[q], (x, y, 1 - c)))
    return plan


def _chip_plan(srcs, lands):
    x, y, c = _place()
    mine = 2 * x + y
    plan = []
    for a in range(len(srcs)):
        for px, py in [(1 - x, y), (x, 1 - y), (1 - x, 1 - y)]:
            peer = 2 * px + py
            plan.append((srcs[a].at[peer], lands[a].at[mine], lands[a].at[peer], (px, py, c)))
    return plan


def _gather_plan(srcs, lands):
    x, y, c = _place()
    mine = 4 * x + 2 * y + c
    plan = []
    for a in range(len(srcs)):
        for px, py, pc in [(x, y, 1 - c), (1 - x, y, c), (x, 1 - y, c), (1 - x, 1 - y, c)]:
            plan.append((srcs[a], lands[a].at[mine], lands[a].at[4 * px + 2 * py + pc], (px, py, pc)))
    return plan


def _forward_plan(srcs, lands):
    x, y, c = _place()
    plan = []
    for a in range(len(lands)):
        for px, py in [(1 - x, y), (x, 1 - y), (1 - x, 1 - y)]:
            mine, theirs = 4 * px + 2 * py + c, 4 * px + 2 * py + 1 - c
            plan.append((lands[a].at[mine], lands[a].at[mine], lands[a].at[theirs], (x, y, 1 - c)))
    return plan


def _remote(src, dst, send_sem, recv_sem, peer):
    return pltpu.make_async_remote_copy(src_ref=src, dst_ref=dst, send_sem=send_sem, recv_sem=recv_sem,
                                        device_id=peer, device_id_type=MESH)


HBM = pl.BlockSpec(memory_space=pltpu.HBM)
SEMS = pl.BlockSpec(memory_space=pltpu.SEMAPHORE)
DATAFLOW = pltpu.SideEffectType.DATAFLOW_SIDE_EFFECTING


def _split_start(srcs, lands, after, plan_fn, per_array, name):
    ns, nb = len(srcs), len(srcs) + len(lands)
    ncopy = per_array * len(lands)
    after = list(after) if isinstance(after, (list, tuple)) else [after]

    def body(*refs):
        send_sems, recv_sems = refs[nb + len(after)], refs[nb + len(after) + 1]
        token = refs[-1]
        for k, (src, dst, _, peer) in enumerate(plan_fn(refs[:ns], refs[ns:nb])):
            _remote(src, dst, send_sems.at[k], recv_sems.at[k], peer).start()
        token[...] = jnp.zeros_like(token)

    thru = [pltpu.HBM(a.shape, a.dtype) for a in (*srcs, *lands)]
    out = pl.pallas_call(
        body, name=name,
        out_shape=(pltpu.SemaphoreType.DMA((ncopy,)), pltpu.SemaphoreType.DMA((ncopy,)), *thru,
                   jax.ShapeDtypeStruct((8, 128), F32)),
        in_specs=[HBM] * nb + [ANY] * len(after),
        out_specs=(SEMS, SEMS, *[HBM] * nb, pl.BlockSpec(memory_space=pltpu.VMEM)),
        input_output_aliases={i: 2 + i for i in range(nb)},
        compiler_params=pltpu.CompilerParams(has_side_effects=DATAFLOW),
    )(*[pltpu.with_memory_space_constraint(a, pltpu.HBM) for a in (*srcs, *lands)], *after)
    return out[0], out[1], list(out[2:2 + ns]), list(out[2 + ns:2 + nb]), out[-1]


def _split_wait(send_sems, recv_sems, srcs, lands, after, plan_fn, name):
    ns, nb = len(srcs), len(srcs) + len(lands)
    after = list(after) if isinstance(after, (list, tuple)) else [after]

    def body(*refs):
        send_ref, recv_ref = refs[nb], refs[nb + 1]
        for k, (src, dst, mine, peer) in enumerate(plan_fn(refs[:ns], refs[ns:nb])):
            _remote(src, dst, send_ref.at[k], recv_ref.at[k], peer).wait_send()
            _remote(src, mine, send_ref.at[k], recv_ref.at[k], peer).wait_recv()

    thru = [pltpu.HBM(a.shape, a.dtype) for a in (*srcs, *lands)]
    out = pl.pallas_call(
        body, name=name, out_shape=tuple(thru),
        in_specs=[HBM] * nb + [SEMS, SEMS] + [ANY] * len(after), out_specs=tuple([HBM] * nb),
        input_output_aliases={i: i for i in range(nb)},
        compiler_params=pltpu.CompilerParams(has_side_effects=DATAFLOW),
    )(*srcs, *lands, send_sems, recv_sems, *after)
    return list(out[:ns]), list(out[ns:])


def _gather_forward(lands, name):
    n = len(lands)

    def body(*refs):
        landed, out = refs[:n], refs[n:2 * n]
        send_sems, recv_sems = refs[2 * n:]
        x, y, c = _place()
        chips = [(1 - x, y), (x, 1 - y), (1 - x, 1 - y)]
        sent = []
        for a in range(n):
            for j, (px, py) in enumerate(chips):
                blk = 4 * px + 2 * py + c
                sent.append(_remote(landed[a].at[blk], out[a].at[blk], send_sems.at[a, j], recv_sems.at[a, j], (x, y, 1 - c)))
        for cp in sent:
            cp.start()
        for a in range(n):
            for j, (px, py) in enumerate(chips):
                blk = 4 * px + 2 * py + 1 - c
                _remote(landed[a].at[blk], out[a].at[blk], send_sems.at[a, j], recv_sems.at[a, j], (x, y, 1 - c)).wait_recv()
        for cp in sent:
            cp.wait_send()

    return pl.pallas_call(
        body, name=name, in_specs=[ANY] * n, out_specs=[ANY] * n,
        out_shape=[jax.ShapeDtypeStruct(l.shape, l.dtype) for l in lands],
        input_output_aliases={a: a for a in range(n)},
        scratch_shapes=[pltpu.SemaphoreType.DMA((n, 3)), pltpu.SemaphoreType.DMA((n, 3))],
    )(*lands)


def _pack_rows(groups, name):
    n = len(groups[0])
    rows = -(-n // 8) * 8
    width = max(v.shape[1] for v in groups[0])

    def body(*refs):
        ins, outs = refs[:n * len(groups)], refs[n * len(groups):]
        for gi, o_ref in enumerate(outs):
            o_ref[...] = jnp.zeros_like(o_ref)
            for r in range(n):
                v_ref = ins[gi * n + r]
                o_ref[r:r + 1, 0:v_ref.shape[1]] = v_ref[...]

    return pl.pallas_call(
        body, name=name, out_shape=[jax.ShapeDtypeStruct((rows, width), F32)] * len(groups), compiler_params=_params(),
    )(*[v for g in groups for v in g])


def _unpack_rows(packs, like, name):
    n = len(like)

    def body(*refs):
        ins, outs = refs[:len(packs)], refs[len(packs):]
        for pi, p_ref in enumerate(ins):
            for r in range(n):
                o_ref = outs[pi * n + r]
                o_ref[...] = p_ref[r:r + 1, 0:o_ref.shape[1]]

    flat = pl.pallas_call(
        body, name=name, out_shape=[jax.ShapeDtypeStruct(a.shape, F32) for _ in packs for a in like],
        compiler_params=_params(),
    )(*packs)
    return [flat[pi * n:(pi + 1) * n] for pi in range(len(packs))]


def _sum_adamw(parts, w, m, v, name, col=None):
    P, R, _ = parts.shape
    C = w.shape[1]
    tr = _row_tile(R, cap=512, mult=8)

    def body(c_ref, p_ref, w_ref, m_ref, v_ref, g_ref, d_ref, nm_ref, nv_ref):
        g = p_ref[0].astype(F32)
        for q in range(1, P):
            g = g + p_ref[q].astype(F32)
        g_ref[...] = g
        d_ref[...], nm_ref[...], nv_ref[...] = _adamw_math(w_ref[...], g, m_ref[...], v_ref[...])

    blk = pl.BlockSpec((tr, C), lambda i, c: (i, 0))
    sds = jax.ShapeDtypeStruct((R, C), F32)
    at = jnp.zeros((1,), jnp.int32) if col is None else col
    return pl.pallas_call(
        body, name=name,
        grid_spec=pltpu.PrefetchScalarGridSpec(
            num_scalar_prefetch=1, grid=(R // tr,),
            in_specs=[pl.BlockSpec((P, tr, C), lambda i, c: (0, i, c[0])), blk, blk, blk], out_specs=[blk] * 4),
        out_shape=[sds] * 4, compiler_params=_params("parallel"),
    )(at, parts, w, m, v)


def _local_step(x, mem, target, sp, W, arrived, want, done, paired):
    h0 = _rms_fwd([x], [sp["g_mix"]], "rms_mix")
    proj = _mm(h0, W["main_t"], "nt", F32, "proj_main", tm=2048, tn=512)
    f_logit = _mm(h0, W["f_t"], "nt", F32, "proj_f", tm=2048)
    qn, kn, vb = _qk_fwd(proj, sp["g_q"], sp["g_k"], "qk_norm")
    b_pad = jnp.pad(sp["b_f"], ((0, 0), (0, HEAD - FOX_H)))
    c_col, c_row = _fgate_fwd(f_logit, b_pad, "forget_cumsum")
    o_fox, lse = _fox_fwd(qn, kn, vb, c_col, c_row, "fox_fwd")
    lru_p = (sp["conv_w"], sp["conv_b"], sp["w_ra"], sp["b_ra"], sp["w_ri"], sp["b_ri"], sp["lam"])
    y_lru, h_lru = _lru_fwd(proj, *lru_p, "lru_fwd", after=arrived("attn", o_fox))
    w_out, w_cq, w_ckv, w_co = want("attn", y_lru)
    mix = _rms_fwd([o_fox, y_lru], [sp["g_fox_out"], sp["g_lru_out"]], "rms_mix_out")
    x1 = _mm(mix, w_out, "nn", F32, "out_proj", add=x)
    hq = _rms_fwd([x1], [sp["g_xattn"]], "rms_xattn")
    mn = _rms_fwd([mem], [sp["g_mem"]], "rms_mem")
    cq = _mm(hq, w_cq, "nn", F32, "xattn_q", tm=2048)
    ckv = _mm(mn, w_ckv, "nn", F32, "xattn_kv")
    o_x = _xattn_fwd(cq, ckv, sp["g_cq"], sp["g_ck"], "xattn_fwd")
    co_w = w_co.shape[2]
    x2 = _mm(o_x, w_co, "nn", F32, "xattn_out", add=x1, tm=2048, tn=co_w, after=arrived("gate_up", o_x))
    hf = _rms_fwd([x2], [sp["g_ffn"]], "rms_ffn")
    (w_gu,) = want("gate_up", hf)
    gu_w = w_gu.shape[2]
    f_gate, f_up, act = _ffn_up(hf, w_gu, "ffn_up")
    (w_down,) = want("down", arrived("down", act))
    x3 = _mm(act, w_down, "nn", F32, "ffn_down", add=x2, tk=w_down.shape[0] // 2)
    dy, dyb, sq = _loss_head(x3, target, "loss_head")
    g_down = _mm(act, dyb, "tn", BF16, "ffn_down_dw", tm=w_down.shape[0] // 4, tn=2048)
    dgu = _ffn_down_dx(dyb, w_down, f_gate, f_up, "ffn_down_dx")
    g_gu = _mm(hf, dgu, "tn", BF16, "ffn_up_dw", out_blocks=N_DEV, tn=gu_w)
    tok = done("ffn", [g_gu, g_down])
    dhf = _mm(dgu, w_gu, "nt", F32, "ffn_up_dx", tn=2048, tk=gu_w, after=tok)
    tok = paired("ffn", dhf)
    dx2, dx2b, dg_ffn = _rms_bwd(x2, sp["g_ffn"], dhf, "rms_ffn_bwd", res=dy, after=tok)
    d_ox = _mm(dx2b, w_co, "nt", F32, "xattn_out_dx", tm=2048, tk=co_w)
    g_co = _mm(o_x, dx2b, "tn", BF16, "xattn_out_dw", out_blocks=N_DEV, tn=co_w)
    dcq, dckv, dg_cq, dg_ck = _xattn_bwd(cq, ckv, sp["g_cq"], sp["g_ck"], d_ox, "xattn_bwd")
    dhq = _mm(dcq, w_cq, "nt", F32, "xattn_q_dx", tm=2048)
    g_cq = _mm(hq, dcq, "tn", BF16, "xattn_q_dw")
    dmn = _mm(dckv, w_ckv, "nt", F32, "xattn_kv_dx")
    g_ckv = _mm(mn, dckv, "tn", BF16, "xattn_kv_dw")
    (dg_mem,) = _rms_bwd(mem, sp["g_mem"], dmn, "rms_mem_bwd", want_dx=False)
    dx1, dx1b, dg_xattn = _rms_bwd(x1, sp["g_xattn"], dhq, "rms_xattn_bwd", res=dx2)
    dmix = _mm(dx1b, w_out, "nt", F32, "out_proj_dx")
    g_out = _mm(mix, dx1b, "tn", BF16, "out_proj_dw", tn=2048)
    tok = done("attn", [g_out, g_cq, g_ckv, g_co])
    do_fox, dg_fox_out = _rms_bwd(o_fox, sp["g_fox_out"], dmix, "rms_fox_out_bwd", dy_col=0, want_bf16=False, after=tok)
    dy_lru, dg_lru_out = _rms_bwd(y_lru, sp["g_lru_out"], dmix, "rms_lru_out_bwd", dy_col=1, want_bf16=False)
    tok = paired("attn", dy_lru)
    du, dgate, dconv_w, dconv_b, dw_ra, db_ra, dw_ri, db_ri, dlam = _lru_bwd(proj, h_lru, dy_lru, *lru_p, "lru_bwd", after=tok)
    dqn, dkn, dv, dc_col, dc_row = _fox_bwd(qn, kn, vb, c_col, c_row, lse, o_fox, do_fox, "fox_bwd")
    dq, dk, dg_q, dg_k = _qk_bwd(proj, sp["g_q"], sp["g_k"], dqn, dkn, "qk_norm_bwd")
    df, db_f = _fgate_bwd(f_logit, b_pad, dc_col, dc_row, "forget_cumsum_bwd")
    dproj = jnp.concatenate([dq, dk, dv, du, dgate], axis=1)
    g_main_t = _mm(dproj, h0, "tn", BF16, "proj_main_dw", tn=2048)
    g_f_t = _mm(df, h0, "tn", BF16, "proj_f_dw", tn=2048)
    tok = done("w_in", [g_main_t, g_f_t])
    dh_f = _mm(df, W["f_t"], "nn", F32, "proj_f_dx", tm=2048)
    dh0 = _mm(dproj, W["main_t"], "nn", F32, "proj_main_dx", add=dh_f, tk=dproj.shape[1] // 2, after=tok)
    tok = paired("w_in", dh0)
    grad_x, dg_mix = _rms_bwd(x, sp["g_mix"], dh0, "rms_mix_bwd", res=dx1, want_bf16=False, after=tok)
    small = dict(g_mix=dg_mix, b_f=db_f[:, :FOX_H], g_q=dg_q, g_k=dg_k, conv_w=dconv_w, conv_b=dconv_b, w_ra=dw_ra,
                 b_ra=db_ra, w_ri=dw_ri, b_ri=db_ri, lam=dlam, g_fox_out=dg_fox_out, g_lru_out=dg_lru_out,
                 g_xattn=dg_xattn, g_mem=dg_mem, g_cq=dg_cq, g_ck=dg_ck, g_ffn=dg_ffn)
    return sq, grad_x, small


BIG = ("w_in", "w_out", "w_cq", "w_ckv", "w_co", "w_gate_up", "w_down")
SMALL = ("g_mix", "b_f", "g_q", "g_k", "conv_b", "w_ra", "b_ra", "w_ri", "b_ri", "lam", "g_fox_out", "g_lru_out",
         "g_xattn", "g_mem", "g_cq", "g_ck", "g_ffn")
ORDER = ("g_mix", "w_in", "b_f", "g_q", "g_k", "conv_w", "conv_b", "w_ra", "b_ra", "w_ri", "b_ri", "lam", "g_fox_out",
         "g_lru_out", "w_out", "g_xattn", "g_mem", "w_cq", "w_ckv", "g_cq", "g_ck", "w_co", "g_ffn", "w_gate_up", "w_down")


def kernel(x, mem, g_mix, w_in, b_f, g_q, g_k, conv_w, conv_b, w_ra, b_ra, w_ri, b_ri, lam, g_fox_out, g_lru_out, w_out, g_xattn, g_mem, w_cq, w_ckv, g_cq, g_ck, w_co, g_ffn, w_gate_up, w_down, loss_target, m_g_mix, m_w_in, m_b_f, m_g_q, m_g_k, m_conv_w, m_conv_b, m_w_ra, m_b_ra, m_w_ri, m_b_ri, m_lam, m_g_fox_out, m_g_lru_out, m_w_out, m_g_xattn, m_g_mem, m_w_cq, m_w_ckv, m_g_cq, m_g_ck, m_w_co, m_g_ffn, m_w_gate_up, m_w_down, v_g_mix, v_w_in, v_b_f, v_g_q, v_g_k, v_conv_w, v_conv_b, v_w_ra, v_b_ra, v_w_ri, v_b_ri, v_lam, v_g_fox_out, v_g_lru_out, v_w_out, v_g_xattn, v_g_mem, v_w_cq, v_w_ckv, v_g_cq, v_g_ck, v_w_co, v_g_ffn, v_w_gate_up, v_w_down):
    given = dict(locals())
    w = {n: given[n] for n in ORDER}
    m = {n: given["m_" + n] for n in ORDER}
    v = {n: given["v_" + n] for n in ORDER}
    D = x.shape[2]
    fw = FOX_H * HEAD
    dev_index = 4 * lax.axis_index("x") + 2 * lax.axis_index("y") + lax.axis_index("c")
    dev = jnp.reshape(dev_index, (1,)).astype(jnp.int32)
    core = jnp.reshape(lax.axis_index("c"), (1,)).astype(jnp.int32)
    chip = jnp.reshape(2 * lax.axis_index("x") + lax.axis_index("y"), (1,)).astype(jnp.int32)

    def shard(d, n):
        return jnp.transpose(d[n], (2, 0, 1)) if n == "w_in" else d[n][0]

    def unshard(a, n):
        return jnp.transpose(a, (1, 2, 0)) if n == "w_in" else a[None]

    gather_groups = dict(attn=("w_out", "w_cq", "w_ckv", "w_co"), gate_up=("w_gate_up",), down=("w_down",))
    reduce_groups = dict(ffn=("w_gate_up", "w_down"), attn=("w_out", "w_cq", "w_ckv", "w_co"), w_in=("w_in",))
    column_blocked = ("w_co", "w_gate_up")
    flying = {}

    def land(a, tag):
        return _land_with_own(a, dev, "own_" + tag)

    def launch(group, after):
        shards = [shard(w, n).astype(BF16) for n in gather_groups[group]]
        lands = [land(s, n) for s, n in zip(shards, gather_groups[group])]
        flying[group] = _split_start(shards, lands, after, _gather_plan, 4, "gather_" + group + "_start")
        return flying[group][4]

    def arrived(group, after):
        send, recv, shards, lands, _ = flying.pop(group)
        _, lands = _split_wait(send, recv, shards, lands, after, _gather_plan, "gather_" + group + "_wait")
        flying[group] = _split_start([], lands, after, _forward_plan, 3, "gather_" + group + "_forward_start")
        if group == "attn":
            return launch("down", flying[group][4])
        return flying[group][4]

    def want(group, after):
        send, recv, _, lands, _ = flying.pop(group)
        _, full = _split_wait(send, recv, [], lands, after, _forward_plan, "gather_" + group + "_forward_wait")
        return [g if n in column_blocked else g.reshape(-1, g.shape[2]) for n, g in zip(gather_groups[group], full)]

    first = [shard(w, "w_in").astype(BF16).reshape(-1, D), conv_w[0]]
    g_in, g_conv = _all_gather(first, [land(first[0], "w_in"), land(first[1], "conv_w")], "gather_w_in")
    tok = launch("gate_up", launch("attn", g_conv))
    per = g_in.shape[1]
    f_k, f_lo = divmod(3 * fw, per)
    assert f_lo + FOX_H <= per
    wt_in = g_in.reshape(-1, D)
    W = dict(main_t=jnp.concatenate([wt_in[:3 * fw], wt_in[3 * fw + FOX_H:]], axis=0),
             f_t=jnp.pad(wt_in[3 * fw:3 * fw + FOX_H], ((0, HEAD - FOX_H), (0, 0))))
    sp = {n: w[n] for n in SMALL if n not in ("w_ra", "w_ri")}
    sp["w_ra"], sp["w_ri"] = w_ra[0], w_ri[0]
    sp["conv_w"] = jnp.transpose(g_conv, (1, 0, 2)).reshape(CONV_K, -1)
    sp["g_mix"] = sp["g_mix"] + tok[0, 0]

    pairing, reducing = {}, {}

    def done(group, grads):
        if group == "w_in":
            g_main_t, g_f_t = grads
            shards = [g_main_t[k * per:(k + 1) * per] for k in range(f_k)]
            shards.append(jnp.concatenate([g_main_t[f_k * per:3 * fw], g_f_t[:FOX_H],
                                           g_main_t[3 * fw:(f_k + 1) * per - FOX_H]], axis=0))
            shards += [g_main_t[k * per - FOX_H:(k + 1) * per - FOX_H] for k in range(f_k + 1, N_DEV)]
            grads = [jnp.stack(shards)]
        g8 = [g if g.ndim == 3 else g.reshape(N_DEV, -1, g.shape[1]) for g in grads]
        lands = [lax.empty((N_CHIP,) + g.shape[1:], g.dtype) for g in g8]
        pairing[group] = _split_start(g8, lands, chip, _pair_plan, N_CHIP, "reduce_" + group + "_pair_start")
        return pairing[group][4]

    def paired(group, after):
        send, recv, g8, lands, _ = pairing.pop(group)
        g8, from_sibling = _split_wait(send, recv, g8, lands, after, _pair_plan, "reduce_" + group + "_pair_wait")
        p4 = [_pair_sum(g, r, core, "reduce_pair_sum_" + n) for g, r, n in zip(g8, from_sibling, reduce_groups[group])]
        lands = [lax.empty(p.shape, p.dtype) for p in p4]
        reducing[group] = _split_start(p4, lands, chip, _chip_plan, 3, "reduce_" + group + "_start")
        return reducing[group][4]

    def finish(group, after):
        send, recv, p4, lands, _ = reducing.pop(group)
        p4, lands = _split_wait(send, recv, p4, lands, after, _chip_plan, "reduce_" + group + "_wait")
        return {n: tuple(unshard(r, n) for r in _reduce_adamw(p, l, chip, shard(w, n), shard(m, n), shard(v, n), "adamw_" + n))
                for n, p, l in zip(reduce_groups[group], p4, lands)}

    sq, grad_x, gs = _local_step(x[0], mem[0], loss_target[0], sp, W, arrived, want, done, paired)

    vectors = tuple(n for n in SMALL if n not in ("w_ra", "w_ri"))
    mine = [_pack_rows([[gs[n] for n in vectors] + [sq[0:1]]], "pack_small_grads")[0], gs["w_ra"].reshape(-1, HEAD),
            gs["w_ri"].reshape(-1, HEAD), gs["conv_w"]]
    lands = [land(a, "small_grads_%d" % k) for k, a in enumerate(mine)]
    s_send, s_recv, s_src, s_land, tok = _split_start(mine, lands, grad_x, _gather_plan, 4, "gather_small_start")
    out = finish("ffn", tok)
    out.update(finish("attn", tok))
    updated = [r for n in reduce_groups["ffn"] + reduce_groups["attn"] for r in out[n]]
    _, s_land = _split_wait(s_send, s_recv, s_src, s_land, updated, _gather_plan, "gather_small_wait")
    all_vec, all_ra, all_ri, all_conv = _gather_forward(s_land, "gather_small_forward")

    nothing = jnp.zeros((1, HEAD), F32)
    state = _pack_rows([[d[n] for n in vectors] + [nothing] for d in (w, m, v)], "pack_small_state")
    upd = _sum_adamw(all_vec, *state, "adamw_small_vectors")
    per_vector = _unpack_rows(upd, [w[n] for n in vectors] + [nothing], "unpack_small")
    for r, n in enumerate(vectors):
        out[n] = tuple(per_vector[k][r] for k in range(4))
    loss = 0.5 * per_vector[0][len(vectors)][0, 0] / D
    for n, parts in (("w_ra", all_ra), ("w_ri", all_ri)):
        res = _sum_adamw(parts, *[d[n].reshape(-1, HEAD) for d in (w, m, v)], "adamw_" + n)
        out[n] = tuple(r.reshape(w[n].shape) for r in res)
    out["conv_w"] = tuple(r[None] for r in _sum_adamw(all_conv, conv_w[0], m_conv_w[0], v_conv_w[0], "adamw_conv_w", col=dev))
    out.update(finish("w_in", upd[1]))

    return (loss, grad_x[None], *[out[n][0] for n in ORDER], *[out[n][1] for n in ORDER],
            *[out[n][2] for n in ORDER], *[out[n][3] for n in ORDER])
```

```python
import functools
import math

import jax
import jax.numpy as jnp
from jax import lax
from jax.experimental import pallas as pl
from jax.experimental.pallas import tpu as pltpu

F32 = jnp.float32
BF16 = jnp.bfloat16
MESH = pl.DeviceIdType.MESH
N_DEV = 8
N_CHIP = 4

HEAD = 128
FOX_H = 8
XATT_H = 4
LRU_NB = 8
CONV_K = 4
LRU_C = 8.0
RMS_EPS = 1e-6
ATT_T = 1024
CUM_T = 256
ROW_T = 256
NEG = -1e30
V7X_VMEM_LIMIT = 48 * 1024 * 1024
V7X_VMEM_LIMIT_HIGH = 60 * 1024 * 1024

ADAM_LR = 0.001
ADAM_B1 = 0.9
ADAM_B2 = 0.999
ADAM_EPS = 1e-08
ADAM_WD = 0.01
ADAM_STEP = 10

NT = (((1,), (1,)), ((), ()))
TN = (((0,), (0,)), ((), ()))
NN = (((1,), (0,)), ((), ()))

ANY = pl.BlockSpec(memory_space=pl.ANY)


def _behind(after):
    return ([], []) if after is None else ([ANY], [after])


def _params(*sem):
    return pltpu.CompilerParams(dimension_semantics=sem or None, vmem_limit_bytes=V7X_VMEM_LIMIT)


def _dot(a, b, dn=NN):
    return lax.dot_general(a, b, dn, preferred_element_type=F32)


def _rms(x, g):
    r = lax.rsqrt(jnp.mean(x * x, axis=-1, keepdims=True) + RMS_EPS)
    return x * r * g


def _rms_grad(x, g, dy):
    r = lax.rsqrt(jnp.mean(x * x, axis=-1, keepdims=True) + RMS_EPS)
    xh = x * r
    dxh = dy * g
    dx = r * (dxh - xh * jnp.mean(dxh * xh, axis=-1, keepdims=True))
    return dx, jnp.sum(dy * xh, axis=0, keepdims=True)


def _gelu(x):
    k = math.sqrt(2.0 / math.pi)
    return 0.5 * x * (1.0 + jnp.tanh(k * (x + 0.044715 * (x * x * x))))


def _gelu_grad(x):
    k = math.sqrt(2.0 / math.pi)
    t = jnp.tanh(k * (x + 0.044715 * (x * x * x)))
    return 0.5 * (1.0 + t) + 0.5 * x * (1.0 - t * t) * k * (1.0 + 3.0 * 0.044715 * x * x)


def _one_minus_exp(z):
    series = -(z + 0.5 * z * z + (1.0 / 6.0) * z * z * z)
    return jnp.where(z > -1e-3, series, 1.0 - jnp.exp(z))


def _row_tile(rows, cap=ROW_T, mult=16):
    t = min(rows, cap)
    while rows % t or (t % mult and t != rows):
        t -= 1
    return t


def _mat_dims(z):
    return (z.shape[-2], z.shape[-1] * (z.shape[0] if z.ndim == 3 else 1))


def _mat_spec(arr, rblk, cblk, rsel, csel):
    if arr.ndim == 2:
        return pl.BlockSpec((rblk, cblk), lambda i, j, k: ((i, j, k)[rsel], (i, j, k)[csel]))
    nw = arr.shape[2]
    assert nw % cblk == 0, (arr.shape, cblk)
    per = nw // cblk
    return pl.BlockSpec((None, rblk, cblk),
                        lambda i, j, k: ((i, j, k)[csel] // per, (i, j, k)[rsel], (i, j, k)[csel] % per))


def _mm(a, b, mode, out_dtype, name, add=None, out_blocks=None, tm=1024, tn=1024, tk=None, after=None):
    ar, ac = _mat_dims(a)
    br, bc = _mat_dims(b)
    if mode == "nn":
        M, K, N = ar, ac, bc
        assert br == K
    elif mode == "nt":
        M, K, N = ar, ac, br
        assert bc == K
    else:
        M, K, N = ac, ar, bc
        assert br == K
    tm, tn = min(tm, M), min(tn, N)
    tk = K if tk is None or mode == "tn" else min(tk, K)
    assert M % tm == 0 and N % tn == 0 and K % tk == 0, (name, M, N, K, tm, tn, tk)
    nk = K // tk
    nj = N // tn
    if mode == "nn":
        specs = [_mat_spec(a, tm, tk, 0, 2), _mat_spec(b, tk, tn, 2, 1)]
        dn = NN
    elif mode == "nt":
        specs = [_mat_spec(a, tm, tk, 0, 2), _mat_spec(b, tn, tk, 1, 2)]
        dn = NT
    else:
        specs = [_mat_spec(a, tk, tm, 2, 0), _mat_spec(b, tk, tn, 2, 1)]
        dn = TN
    args = [a, b]
    if add is not None:
        specs.append(pl.BlockSpec((tm, tn), lambda i, j, k: (i, j)))
        args.append(add)
    specs += _behind(after)[0]
    args += _behind(after)[1]
    n_in = len(args)
    if out_blocks is None:
        out_shape = jax.ShapeDtypeStruct((M, N), out_dtype)
        out_spec = pl.BlockSpec((tm, tn), lambda i, j, k: (i, j))
    else:
        nb = out_blocks
        nw = N // nb
        assert nw % tn == 0
        per = nw // tn
        out_shape = jax.ShapeDtypeStruct((nb, M, nw), out_dtype)
        out_spec = pl.BlockSpec((None, tm, tn), lambda i, j, k: (j // per, i, j % per))
    keep_t = mode == "tn" and nj > 1
    scratch = []
    if nk > 1:
        scratch.append(pltpu.VMEM((tm, tn), F32))
    if keep_t:
        scratch.append(pltpu.VMEM((tm, tk), a.dtype))

    def body(*refs):
        a_ref, b_ref = refs[0], refs[1]
        add_ref = refs[2] if add is not None else None
        o_ref = refs[n_in]

        def finish(r):
            if add_ref is not None:
                r = r + add_ref[...]
            o_ref[...] = r.astype(out_dtype)

        if keep_t:
            at_ref = refs[-1]

            @pl.when(pl.program_id(1) == 0)
            def _():
                at_ref[...] = a_ref[...].T

            finish(_dot(at_ref[...], b_ref[...], NN))
        elif nk == 1:
            finish(_dot(a_ref[...], b_ref[...], dn))
        else:
            acc_ref = refs[n_in + 1]
            k = pl.program_id(2)

            @pl.when(k == 0)
            def _():
                acc_ref[...] = jnp.zeros_like(acc_ref)

            acc_ref[...] += _dot(a_ref[...], b_ref[...], dn)

            @pl.when(k == nk - 1)
            def _():
                finish(acc_ref[...])

    return pl.pallas_call(
        body, name=name, grid=(M // tm, nj, nk), in_specs=specs, out_specs=out_spec, out_shape=out_shape,
        scratch_shapes=scratch,
        compiler_params=_params("parallel", "arbitrary" if keep_t else "parallel", "arbitrary"),
    )(*args)


def _rms_fwd(xs, gs, name, after=None):
    n = len(xs)
    rows = xs[0].shape[0]
    widths = [x.shape[1] for x in xs]
    tr = _row_tile(rows)
    more_specs, more = _behind(after)

    def body(*refs):
        o_ref = refs[2 * n + len(more)]
        off = 0
        for t in range(n):
            o_ref[:, off:off + widths[t]] = _rms(refs[t][...], refs[n + t][...]).astype(BF16)
            off += widths[t]

    return pl.pallas_call(
        body, name=name, grid=(rows // tr,),
        in_specs=[pl.BlockSpec((tr, w), lambda i: (i, 0)) for w in widths]
        + [pl.BlockSpec((1, w), lambda i: (0, 0)) for w in widths] + more_specs,
        out_specs=pl.BlockSpec((tr, sum(widths)), lambda i: (i, 0)),
        out_shape=jax.ShapeDtypeStruct((rows, sum(widths)), BF16),
        compiler_params=_params("parallel"),
    )(*xs, *gs, *more)


def _rms_bwd(x, g, dy, name, dy_col=0, res=None, want_dx=True, want_bf16=True, after=None):
    rows, w = x.shape
    tr = _row_tile(rows)
    has_res = res is not None
    more_specs, more = _behind(after)

    def body(*refs):
        x_ref, g_ref, dy_ref = refs[:3]
        res_ref = refs[3] if has_res else None
        outs = refs[3 + has_res + len(more):]
        dg_ref = outs[-1]
        dx, dg = _rms_grad(x_ref[...], g_ref[...], dy_ref[...])

        @pl.when(pl.program_id(0) == 0)
        def _():
            dg_ref[...] = jnp.zeros_like(dg_ref)

        dg_ref[...] += dg
        if want_dx:
            if has_res:
                dx = dx + res_ref[...]
            outs[0][...] = dx
            if want_bf16:
                outs[1][...] = dx.astype(BF16)

    row_spec = pl.BlockSpec((tr, w), lambda i: (i, 0))
    in_specs = [row_spec, pl.BlockSpec((1, w), lambda i: (0, 0)), pl.BlockSpec((tr, w), lambda i: (i, dy_col))]
    args = [x, g, dy]
    if has_res:
        in_specs.append(row_spec)
        args.append(res)
    in_specs += more_specs
    args += more
    out_specs, out_shape = [], []
    if want_dx:
        out_specs += [row_spec] * (1 + want_bf16)
        out_shape += [jax.ShapeDtypeStruct((rows, w), F32)] + [jax.ShapeDtypeStruct((rows, w), BF16)] * want_bf16
    out_specs.append(pl.BlockSpec((1, w), lambda i: (0, 0)))
    out_shape.append(jax.ShapeDtypeStruct((1, w), F32))
    return pl.pallas_call(
        body, name=name, grid=(rows // tr,), in_specs=in_specs, out_specs=out_specs, out_shape=out_shape,
        compiler_params=_params("arbitrary"),
    )(*args)


def _loss_head(y, target, name):
    rows, w = y.shape
    tr = _row_tile(rows)

    def body(y_ref, t_ref, dy_ref, dyb_ref, acc_ref):
        e = y_ref[...] - t_ref[...]

        @pl.when(pl.program_id(0) == 0)
        def _():
            acc_ref[...] = jnp.zeros_like(acc_ref)

        acc_ref[...] += jnp.sum(e * e)
        dy = e * (1.0 / w)
        dy_ref[...] = dy
        dyb_ref[...] = dy.astype(BF16)

    row_spec = pl.BlockSpec((tr, w), lambda i: (i, 0))
    return pl.pallas_call(
        body, name=name, grid=(rows // tr,), in_specs=[row_spec, row_spec],
        out_specs=[row_spec, row_spec, pl.BlockSpec((8, 128), lambda i: (0, 0))],
        out_shape=[jax.ShapeDtypeStruct((rows, w), F32), jax.ShapeDtypeStruct((rows, w), BF16),
                   jax.ShapeDtypeStruct((8, 128), F32)],
        compiler_params=_params("arbitrary"),
    )(y, target)


def _qk_fwd(proj, g_q, g_k, name, after=None):
    rows = proj.shape[0]
    w = FOX_H * HEAD
    tr = _row_tile(rows)
    more_specs, more = _behind(after)

    def body(q_ref, k_ref, v_ref, gq_ref, gk_ref, *rest):
        qn_ref, kn_ref, vb_ref = rest[len(more):]
        for h in range(FOX_H):
            sl = slice(h * HEAD, (h + 1) * HEAD)
            qn_ref[:, sl] = _rms(q_ref[:, sl], gq_ref[...]).astype(BF16)
            kn_ref[:, sl] = _rms(k_ref[:, sl], gk_ref[...]).astype(BF16)
        vb_ref[...] = v_ref[...].astype(BF16)

    gain = pl.BlockSpec((1, HEAD), lambda i: (0, 0))
    out = pl.BlockSpec((tr, w), lambda i: (i, 0))
    return pl.pallas_call(
        body, name=name, grid=(rows // tr,),
        in_specs=[pl.BlockSpec((tr, w), lambda i: (i, 0)), pl.BlockSpec((tr, w), lambda i: (i, 1)),
                  pl.BlockSpec((tr, w), lambda i: (i, 2)), gain, gain] + more_specs,
        out_specs=[out, out, out], out_shape=[jax.ShapeDtypeStruct((rows, w), BF16)] * 3,
        compiler_params=_params("parallel"),
    )(proj, proj, proj, g_q, g_k, *more)


def _qk_bwd(proj, g_q, g_k, dqn, dkn, name):
    rows = proj.shape[0]
    w = FOX_H * HEAD
    tr = _row_tile(rows)

    def body(q_ref, k_ref, gq_ref, gk_ref, dqn_ref, dkn_ref, dq_ref, dk_ref, dgq_ref, dgk_ref):
        @pl.when(pl.program_id(0) == 0)
        def _():
            dgq_ref[...] = jnp.zeros_like(dgq_ref)
            dgk_ref[...] = jnp.zeros_like(dgk_ref)

        for h in range(FOX_H):
            sl = slice(h * HEAD, (h + 1) * HEAD)
            dq, dgq = _rms_grad(q_ref[:, sl], gq_ref[...], dqn_ref[:, sl])
            dk, dgk = _rms_grad(k_ref[:, sl], gk_ref[...], dkn_ref[:, sl])
            dq_ref[:, sl] = dq.astype(BF16)
            dk_ref[:, sl] = dk.astype(BF16)
            dgq_ref[...] += dgq
            dgk_ref[...] += dgk

    gain = pl.BlockSpec((1, HEAD), lambda i: (0, 0))
    row = pl.BlockSpec((tr, w), lambda i: (i, 0))
    return pl.pallas_call(
        body, name=name, grid=(rows // tr,),
        in_specs=[row, pl.BlockSpec((tr, w), lambda i: (i, 1)), gain, gain, row, row],
        out_specs=[row, row, gain, gain],
        out_shape=[jax.ShapeDtypeStruct((rows, w), BF16)] * 2 + [jax.ShapeDtypeStruct((1, HEAD), F32)] * 2,
        compiler_params=_params("arbitrary"),
    )(proj, proj, g_q, g_k, dqn, dkn)


def _fgate_fwd(f_logit, b_pad, name):
    S = f_logit.shape[0]
    T = min(CUM_T, S)
    nb = S // T
    RT = min(ATT_T, S)
    nbr = S // RT

    def body(f_ref, b_ref, col_ref, row_ref, c_scr):
        tri = (lax.broadcasted_iota(jnp.int32, (T, T), 0) >= lax.broadcasted_iota(jnp.int32, (T, T), 1)).astype(F32)
        carry = jnp.zeros((1, HEAD), F32)
        for blk in range(nb):
            z = f_ref[blk * T:(blk + 1) * T, :] + b_ref[...]
            lf = jnp.minimum(z, 0.0) - jnp.log1p(jnp.exp(-jnp.abs(z)))
            cb = jnp.dot(tri, lf, precision=lax.Precision.HIGHEST, preferred_element_type=F32) + carry
            c_scr[blk * T:(blk + 1) * T, :] = cb
            carry = cb[T - 1:T, :]
        c = c_scr[...]
        lane = lax.broadcasted_iota(jnp.int32, c.shape, 1)
        ct = c.T
        for h in range(FOX_H):
            col_ref[h] = jnp.sum(jnp.where(lane == h, c, 0.0), axis=1, keepdims=True)
            for jb in range(nbr):
                row_ref[h, jb] = ct[h:h + 1, jb * RT:(jb + 1) * RT]

    return pl.pallas_call(
        body, name=name,
        out_shape=[jax.ShapeDtypeStruct((FOX_H, S, 1), F32), jax.ShapeDtypeStruct((FOX_H, nbr, 1, RT), F32)],
        scratch_shapes=[pltpu.VMEM((S, HEAD), F32)], compiler_params=_params(),
    )(f_logit, b_pad)


def _fgate_bwd(f_logit, b_pad, dc_col, dc_row, name):
    S = f_logit.shape[0]
    T = min(CUM_T, S)
    nb = S // T
    RT = min(ATT_T, S)

    def body(f_ref, b_ref, dcol_ref, drow_ref, df_ref, db_ref, dc_scr, rt_scr):
        lane = lax.broadcasted_iota(jnp.int32, (S, HEAD), 1)
        sub = lax.broadcasted_iota(jnp.int32, (HEAD, RT), 0)
        dc = jnp.zeros((S, HEAD), F32)
        for h in range(FOX_H):
            dc = jnp.where(lane == h, dcol_ref[h], dc)
        for jb in range(S // RT):
            rt = jnp.zeros((HEAD, RT), F32)
            for h in range(FOX_H):
                rt = jnp.where(sub == h, drow_ref[h, jb], rt)
            rt_scr[jb * RT:(jb + 1) * RT, :] = rt.T
        dc_scr[...] = dc - rt_scr[...]
        tri = (lax.broadcasted_iota(jnp.int32, (T, T), 0) <= lax.broadcasted_iota(jnp.int32, (T, T), 1)).astype(F32)
        carry = jnp.zeros((1, HEAD), F32)
        db = jnp.zeros((1, HEAD), F32)
        for blk in reversed(range(nb)):
            rows = slice(blk * T, (blk + 1) * T)
            dlf = jnp.dot(tri, dc_scr[rows, :], precision=lax.Precision.HIGHEST, preferred_element_type=F32) + carry
            carry = dlf[0:1, :]
            z = f_ref[rows, :] + b_ref[...]
            df = dlf * jax.nn.sigmoid(-z)
            df_ref[rows, :] = df.astype(BF16)
            db = db + jnp.sum(df, axis=0, keepdims=True)
        db_ref[...] = db

    return pl.pallas_call(
        body, name=name,
        out_shape=[jax.ShapeDtypeStruct((S, HEAD), BF16), jax.ShapeDtypeStruct((1, HEAD), F32)],
        scratch_shapes=[pltpu.VMEM((S, HEAD), F32), pltpu.VMEM((S, HEAD), F32)], compiler_params=_params(),
    )(f_logit, b_pad, dc_col, dc_row)


def _fox_fwd(qn, kn, vb, c_col, c_row, name, after=None):
    S = qn.shape[0]
    T = min(ATT_T, S)
    nb = S // T
    scale = 1.0 / math.sqrt(HEAD)
    more_specs, more = _behind(after)

    def body(q_ref, k_ref, v_ref, cc_ref, cr_ref, *rest):
        o_ref, lse_ref = rest[len(more):]
        i = pl.program_id(1)
        q = q_ref[...]
        cc = cc_ref[...]

        def step(j, carry, diagonal=False):
            m, l, acc = carry
            sl = pl.ds(pl.multiple_of(j * T, T), T)
            s = _dot(q, k_ref[sl, :], NT) * scale + cc - cr_ref[j]
            if diagonal:
                s = jnp.where(lax.broadcasted_iota(jnp.int32, (T, T), 0) >= lax.broadcasted_iota(jnp.int32, (T, T), 1), s, NEG)
            m2 = jnp.maximum(m, jnp.max(s, axis=1, keepdims=True))
            p = jnp.exp(s - m2)
            al = jnp.exp(m - m2)
            return m2, al * l + jnp.sum(p, axis=1, keepdims=True), al * acc + _dot(p.astype(BF16), v_ref[sl, :])

        init = (jnp.full((T, 1), NEG, F32), jnp.zeros((T, 1), F32), jnp.zeros((T, HEAD), F32))
        m, l, acc = step(i, lax.fori_loop(0, i, step, init), diagonal=True)
        o_ref[...] = acc / l
        lse_ref[...] = m + jnp.log(l)

    head = pl.BlockSpec((S, HEAD), lambda h, i: (0, h))
    return pl.pallas_call(
        body, name=name, grid=(FOX_H, nb),
        in_specs=[pl.BlockSpec((T, HEAD), lambda h, i: (i, h)), head, head,
                  pl.BlockSpec((None, T, 1), lambda h, i: (h, i, 0)),
                  pl.BlockSpec((None, nb, 1, T), lambda h, i: (h, 0, 0, 0))] + more_specs,
        out_specs=[pl.BlockSpec((T, HEAD), lambda h, i: (i, h)), pl.BlockSpec((None, T, 1), lambda h, i: (h, i, 0))],
        out_shape=[jax.ShapeDtypeStruct((S, FOX_H * HEAD), F32), jax.ShapeDtypeStruct((FOX_H, S, 1), F32)],
        compiler_params=_params("parallel", "parallel"),
    )(qn, kn, vb, c_col, c_row, *more)


def _fox_bwd(qn, kn, vb, c_col, c_row, lse, o, do, name):
    S = qn.shape[0]
    T = min(ATT_T, S)
    Th = T // 2
    nb = S // T
    scale = 1.0 / math.sqrt(HEAD)

    def body(q_ref, k_ref, v_ref, cc_ref, cr_ref, lse_ref, o_ref, do_ref, dq_ref, dk_ref, dv_ref, dcc_ref, dcr_ref):
        j = pl.program_id(1)

        @pl.when(j == 0)
        def _():
            dq_ref[...] = jnp.zeros_like(dq_ref)
            dcc_ref[...] = jnp.zeros_like(dcc_ref)

        k = k_ref[...]
        v = v_ref[...]
        cr = cr_ref[...]

        def tile(sl, kk, vv, crr, carry, masked):
            dk, dv, dcr = carry
            q = q_ref[sl, :]
            d_o = do_ref[sl, :]
            s = _dot(q, kk, NT) * scale + cc_ref[sl, :] - crr
            if masked:
                s = jnp.where(lax.broadcasted_iota(jnp.int32, s.shape, 0) >= lax.broadcasted_iota(jnp.int32, s.shape, 1), s, NEG)
            p = jnp.exp(s - lse_ref[sl, :])
            dob = d_o.astype(BF16)
            dp = _dot(dob, vv, NT)
            delta = jnp.sum(d_o * o_ref[sl, :], axis=1, keepdims=True)
            ds = p * (dp - delta)
            dsb = ds.astype(BF16)
            dq_ref[sl, :] += _dot(dsb, kk) * scale
            dcc_ref[sl, :] += jnp.sum(ds, axis=1, keepdims=True)
            return (dk + _dot(dsb, q, TN), dv + _dot(p.astype(BF16), dob, TN), dcr + jnp.sum(ds, axis=0, keepdims=True))

        def step(i, carry):
            return tile(pl.ds(pl.multiple_of(i * T, T), T), k, v, cr, carry, False)

        def rows(r0):
            return pl.ds(pl.multiple_of(j * T + r0, Th), Th)

        zero = (jnp.zeros((Th, HEAD), F32), jnp.zeros((Th, HEAD), F32), jnp.zeros((1, Th), F32))
        left = tile(rows(Th), k[:Th], v[:Th], cr[:, :Th], tile(rows(0), k[:Th], v[:Th], cr[:, :Th], zero, True), False)
        right = tile(rows(Th), k[Th:], v[Th:], cr[:, Th:], zero, True)
        diagonal = (jnp.concatenate([left[0], right[0]], axis=0), jnp.concatenate([left[1], right[1]], axis=0),
                    jnp.concatenate([left[2], right[2]], axis=1))
        dk, dv, dcr = lax.fori_loop(j + 1, nb, step, diagonal)
        dk_ref[...] = dk * scale
        dv_ref[...] = dv.astype(BF16)
        dcr_ref[...] = dcr

    head = pl.BlockSpec((S, HEAD), lambda h, j: (0, h))
    tile = pl.BlockSpec((T, HEAD), lambda h, j: (j, h))
    col = pl.BlockSpec((None, S, 1), lambda h, j: (h, 0, 0))
    row = pl.BlockSpec((None, None, 1, T), lambda h, j: (h, j, 0, 0))
    w = FOX_H * HEAD
    return pl.pallas_call(
        body, name=name, grid=(FOX_H, nb),
        in_specs=[head, tile, tile, col, row, col, head, head],
        out_specs=[head, tile, tile, col, row],
        out_shape=[jax.ShapeDtypeStruct((S, w), F32), jax.ShapeDtypeStruct((S, w), F32), jax.ShapeDtypeStruct((S, w), BF16),
                   jax.ShapeDtypeStruct((FOX_H, S, 1), F32), jax.ShapeDtypeStruct((FOX_H, nb, 1, T), F32)],
        compiler_params=_params("arbitrary", "arbitrary"),
    )(qn, kn, vb, c_col, c_row, lse, o, do)


LRU_STEP_BLOCKS = 2


def _block_dot(x, w, dn=NN):
    return jnp.concatenate([_dot(x[:, b * HEAD:(b + 1) * HEAD], w[b], dn) for b in range(w.shape[0])], axis=1)


def _lru_gates(uc, wra, bra, wri, bri, lam):
    ucb = uc.astype(BF16)
    r = jax.nn.sigmoid(_block_dot(ucb, wra.astype(BF16)) + bra)
    ig = jax.nn.sigmoid(_block_dot(ucb, wri.astype(BF16)) + bri)
    sp = jnp.maximum(-lam, 0.0) + jnp.log1p(jnp.exp(-jnp.abs(lam)))
    log_a = -LRU_C * r * sp
    a = jnp.exp(log_a)
    mult = jnp.sqrt(_one_minus_exp(2.0 * log_a))
    return r, ig, sp, a, mult


def _conv(pad_ref, cw, cb, S):
    uc = cb
    for j in range(CONV_K):
        uc = uc + cw[j:j + 1, :] * pad_ref[5 + j:5 + j + S, :]
    return uc


def _lru_specs(S, G):
    bw = G * HEAD
    u_col = 3 * FOX_H // G
    g_col = u_col + LRU_NB // G
    blk = pl.BlockSpec((S, bw), lambda n: (0, n))
    vec = pl.BlockSpec((1, bw), lambda n: (0, n))
    mat = pl.BlockSpec((G, HEAD, HEAD), lambda n: (n, 0, 0))
    return dict(
        u=pl.BlockSpec((S, bw), lambda n: (0, u_col + n)), gate=pl.BlockSpec((S, bw), lambda n: (0, g_col + n)),
        blk=blk, vec=vec, mat=mat, cw=pl.BlockSpec((CONV_K, bw), lambda n: (0, n)))


def _lru_fwd(proj, conv_w, conv_b, w_ra, b_ra, w_ri, b_ri, lam, name, after=None):
    S = proj.shape[0]
    G = LRU_STEP_BLOCKS
    bw = G * HEAD
    sp_ = _lru_specs(S, G)
    rows8 = S // 8
    more_specs, more = _behind(after)

    def body(u_ref, gt_ref, cw_ref, cb_ref, wra_ref, bra_ref, wri_ref, bri_ref, lam_ref, *rest):
        y_ref, h_ref, pad, a_scr, b_scr = rest[len(more):]
        pad[0:8, :] = jnp.zeros((8, bw), F32)
        pad[8:S + 8, :] = u_ref[...]
        uc = _conv(pad, cw_ref[...], cb_ref[...], S)
        r, ig, sp, a, mult = _lru_gates(uc, wra_ref[...], bra_ref[...], wri_ref[...], bri_ref[...], lam_ref[...])
        a_scr[...] = a
        b_scr[...] = mult * (ig * uc)
        sub = lax.broadcasted_iota(jnp.int32, (8, bw), 0)

        def step(t, carry):
            sl = pl.ds(pl.multiple_of(t * 8, 8), 8)
            A, B = a_scr[sl, :], b_scr[sl, :]
            for d in (1, 2, 4):
                m = sub >= d
                B = jnp.where(m, A * pltpu.roll(B, d, 0) + B, B)
                A = jnp.where(m, A * pltpu.roll(A, d, 0), A)
            h = A * carry + B
            h_ref[sl, :] = h
            return h[7:8, :]

        lax.fori_loop(0, rows8, step, jnp.zeros((1, bw), F32))
        y_ref[...] = h_ref[...] * _gelu(gt_ref[...])

    w = LRU_NB * HEAD
    return pl.pallas_call(
        body, name=name, grid=(LRU_NB // G,),
        in_specs=[sp_["u"], sp_["gate"], sp_["cw"], sp_["vec"], sp_["mat"], sp_["vec"], sp_["mat"], sp_["vec"], sp_["vec"]]
        + more_specs,
        out_specs=[sp_["blk"], sp_["blk"]],
        out_shape=[jax.ShapeDtypeStruct((S, w), F32)] * 2,
        scratch_shapes=[pltpu.VMEM((S + 8, bw), F32), pltpu.VMEM((S, bw), F32), pltpu.VMEM((S, bw), F32)],
        compiler_params=_params("parallel"),
    )(proj, proj, conv_w, conv_b, w_ra, b_ra, w_ri, b_ri, lam, *more)


def _lru_bwd(proj, h, dy, conv_w, conv_b, w_ra, b_ra, w_ri, b_ri, lam, name, after=None):
    S = proj.shape[0]
    G = LRU_STEP_BLOCKS
    bw = G * HEAD
    sp_ = _lru_specs(S, G)
    rows8 = S // 8
    more_specs, more = _behind(after)

    def body(u_ref, gt_ref, h_ref, dy_ref, cw_ref, cb_ref, wra_ref, bra_ref, wri_ref, bri_ref, lam_ref, *rest):
        (du_ref, dgt_ref, dcw_ref, dcb_ref, dwra_ref, dbra_ref, dwri_ref, dbri_ref, dlam_ref,
         pad, an_scr, d_scr, g_scr, hp_scr) = rest[len(more):]
        zero8 = jnp.zeros((8, bw), F32)
        pad[0:8, :] = zero8
        pad[8:S + 8, :] = u_ref[...]
        cw = cw_ref[...]
        uc = _conv(pad, cw, cb_ref[...], S)
        wra, wri, lam_v = wra_ref[...], wri_ref[...], lam_ref[...]
        r, ig, sp, a, mult = _lru_gates(uc, wra, bra_ref[...], wri, bri_ref[...], lam_v)
        gate = gt_ref[...]
        dy_v = dy_ref[...]
        hv = h_ref[...]
        dgt_ref[...] = (dy_v * hv * _gelu_grad(gate)).astype(BF16)
        d_scr[...] = dy_v * _gelu(gate)
        g_scr[0:S, :] = a
        g_scr[S:S + 8, :] = zero8
        an_scr[...] = g_scr[1:S + 1, :]
        sub = lax.broadcasted_iota(jnp.int32, (8, bw), 0)

        def step(t, carry):
            sl = pl.ds(pl.multiple_of((rows8 - 1 - t) * 8, 8), 8)
            A, D = an_scr[sl, :], d_scr[sl, :]
            for d in (1, 2, 4):
                m = sub + d <= 7
                D = jnp.where(m, A * pltpu.roll(D, 8 - d, 0) + D, D)
                A = jnp.where(m, A * pltpu.roll(A, 8 - d, 0), A)
            g = A * carry + D
            g_scr[sl, :] = g
            return g[0:1, :]

        lax.fori_loop(0, rows8, step, jnp.zeros((1, bw), F32))
        g = g_scr[0:S, :]
        hp_scr[0:8, :] = zero8
        hp_scr[8:S + 8, :] = hv
        da = g * hp_scr[7:S + 7, :]
        iu = ig * uc
        dmult = g * iu
        diu = g * mult
        dig = diu * uc
        duc = diu * ig
        dlog_a = da * a - dmult * (a * a) / mult
        dr = dlog_a * (-LRU_C * sp)
        dsp = jnp.sum(dlog_a * (-LRU_C * r), axis=0, keepdims=True)
        dlam_ref[...] = -dsp * jax.nn.sigmoid(-lam_v)
        dpr = dr * r * (1.0 - r)
        dpi = dig * ig * (1.0 - ig)
        dbra_ref[...] = jnp.sum(dpr, axis=0, keepdims=True)
        dbri_ref[...] = jnp.sum(dpi, axis=0, keepdims=True)
        ucb = uc.astype(BF16)
        dprb, dpib = dpr.astype(BF16), dpi.astype(BF16)
        for b in range(G):
            cols = slice(b * HEAD, (b + 1) * HEAD)
            dwra_ref[b] = _dot(ucb[:, cols], dprb[:, cols], TN).astype(BF16)
            dwri_ref[b] = _dot(ucb[:, cols], dpib[:, cols], TN).astype(BF16)
        duc = duc + _block_dot(dprb, wra.astype(BF16), NT) + _block_dot(dpib, wri.astype(BF16), NT)
        dcb_ref[...] = jnp.sum(duc, axis=0, keepdims=True)
        for j in range(CONV_K):
            dcw_ref[j:j + 1, :] = jnp.sum(duc * pad[5 + j:5 + j + S, :], axis=0, keepdims=True)
        g_scr[0:S, :] = duc
        g_scr[S:S + 8, :] = zero8
        du = jnp.zeros((S, bw), F32)
        for j in range(CONV_K):
            du = du + cw[j:j + 1, :] * g_scr[3 - j:3 - j + S, :]
        du_ref[...] = du.astype(BF16)

    w = LRU_NB * HEAD
    bf = jax.ShapeDtypeStruct((S, w), BF16)
    vec = jax.ShapeDtypeStruct((1, w), F32)
    mat = jax.ShapeDtypeStruct((LRU_NB, HEAD, HEAD), BF16)
    return pl.pallas_call(
        body, name=name, grid=(LRU_NB // G,),
        in_specs=[sp_["u"], sp_["gate"], sp_["blk"], sp_["blk"], sp_["cw"], sp_["vec"], sp_["mat"], sp_["vec"],
                  sp_["mat"], sp_["vec"], sp_["vec"]] + more_specs,
        out_specs=[sp_["blk"], sp_["blk"], sp_["cw"], sp_["vec"], sp_["mat"], sp_["vec"], sp_["mat"], sp_["vec"], sp_["vec"]],
        out_shape=[bf, bf, jax.ShapeDtypeStruct((CONV_K, w), F32), vec, mat, vec, mat, vec, vec],
        scratch_shapes=[pltpu.VMEM((S + 8, bw), F32), pltpu.VMEM((S, bw), F32), pltpu.VMEM((S, bw), F32),
                        pltpu.VMEM((S + 8, bw), F32), pltpu.VMEM((S + 8, bw), F32)],
        compiler_params=pltpu.CompilerParams(dimension_semantics=("parallel",), vmem_limit_bytes=V7X_VMEM_LIMIT_HIGH),
    )(proj, proj, h, dy, conv_w, conv_b, w_ra, b_ra, w_ri, b_ri, lam, *more)


def _xattn_fwd(cq, ckv, g_cq, g_ck, name):
    S, w = cq.shape
    M = ckv.shape[0]
    tr = _row_tile(S)
    scale = 1.0 / math.sqrt(HEAD)

    def body(cq_ref, ckv_ref, gq_ref, gk_ref, o_ref):
        for h in range(XATT_H):
            sl = slice(h * HEAD, (h + 1) * HEAD)
            qn = _rms(cq_ref[:, sl], gq_ref[...]).astype(BF16)
            kn = _rms(ckv_ref[:, sl], gk_ref[...]).astype(BF16)
            v = ckv_ref[:, w + h * HEAD:w + (h + 1) * HEAD].astype(BF16)
            s = _dot(qn, kn, NT) * scale
            p = jnp.exp(s - jnp.max(s, axis=1, keepdims=True))
            p = p / jnp.sum(p, axis=1, keepdims=True)
            o_ref[:, sl] = _dot(p.astype(BF16), v).astype(BF16)

    gain = pl.BlockSpec((1, HEAD), lambda i: (0, 0))
    return pl.pallas_call(
        body, name=name, grid=(S // tr,),
        in_specs=[pl.BlockSpec((tr, w), lambda i: (i, 0)), pl.BlockSpec((M, 2 * w), lambda i: (0, 0)), gain, gain],
        out_specs=pl.BlockSpec((tr, w), lambda i: (i, 0)), out_shape=jax.ShapeDtypeStruct((S, w), BF16),
        compiler_params=_params("parallel"),
    )(cq, ckv, g_cq, g_ck)


def _xattn_bwd(cq, ckv, g_cq, g_ck, do, name):
    S, w = cq.shape
    M = ckv.shape[0]
    tr = _row_tile(S)
    nsteps = S // tr
    scale = 1.0 / math.sqrt(HEAD)

    def body(cq_ref, ckv_ref, gq_ref, gk_ref, do_ref, dcq_ref, dckv_ref, dgq_ref, dgk_ref, dkn_scr, dv_scr):
        step = pl.program_id(0)

        @pl.when(step == 0)
        def _():
            dkn_scr[...] = jnp.zeros_like(dkn_scr)
            dv_scr[...] = jnp.zeros_like(dv_scr)
            dgq_ref[...] = jnp.zeros_like(dgq_ref)

        for h in range(XATT_H):
            sl = slice(h * HEAD, (h + 1) * HEAD)
            q_raw = cq_ref[:, sl]
            qn = _rms(q_raw, gq_ref[...]).astype(BF16)
            kn = _rms(ckv_ref[:, sl], gk_ref[...]).astype(BF16)
            v = ckv_ref[:, w + h * HEAD:w + (h + 1) * HEAD].astype(BF16)
            s = _dot(qn, kn, NT) * scale
            p = jnp.exp(s - jnp.max(s, axis=1, keepdims=True))
            p = p / jnp.sum(p, axis=1, keepdims=True)
            dob = do_ref[:, sl].astype(BF16)
            dp = _dot(dob, v, NT)
            ds = p * (dp - jnp.sum(p * dp, axis=1, keepdims=True)) * scale
            dsb = ds.astype(BF16)
            dv_scr[:, sl] += _dot(p.astype(BF16), dob, TN)
            dkn_scr[:, sl] += _dot(dsb, qn, TN)
            dq, dgq = _rms_grad(q_raw, gq_ref[...], _dot(dsb, kn))
            dcq_ref[:, sl] = dq.astype(BF16)
            dgq_ref[...] += dgq

        @pl.when(step == nsteps - 1)
        def _():
            dgk = jnp.zeros((1, HEAD), F32)
            for h in range(XATT_H):
                sl = slice(h * HEAD, (h + 1) * HEAD)
                dk, dgk_h = _rms_grad(ckv_ref[:, sl], gk_ref[...], dkn_scr[:, sl])
                dckv_ref[:, sl] = dk.astype(BF16)
                dgk = dgk + dgk_h
            dckv_ref[:, w:2 * w] = dv_scr[...].astype(BF16)
            dgk_ref[...] = dgk

    gain = pl.BlockSpec((1, HEAD), lambda i: (0, 0))
    row = pl.BlockSpec((tr, w), lambda i: (i, 0))
    mem = pl.BlockSpec((M, 2 * w), lambda i: (0, 0))
    return pl.pallas_call(
        body, name=name, grid=(nsteps,), in_specs=[row, mem, gain, gain, row], out_specs=[row, mem, gain, gain],
        out_shape=[jax.ShapeDtypeStruct((S, w), BF16), jax.ShapeDtypeStruct((M, 2 * w), BF16),
                   jax.ShapeDtypeStruct((1, HEAD), F32), jax.ShapeDtypeStruct((1, HEAD), F32)],
        scratch_shapes=[pltpu.VMEM((M, w), F32), pltpu.VMEM((M, w), F32)],
        compiler_params=_params("arbitrary"),
    )(cq, ckv, g_cq, g_ck, do)


def _ffn_up(h, w_gu, name, tm=512):
    S, D = h.shape
    nb, _, bw = w_gu.shape
    half = nb // 2
    tm = min(tm, S)

    def body(h_ref, wg_ref, wu_ref, g_ref, u_ref, act_ref):
        hv = h_ref[...]
        g = _dot(hv, wg_ref[...])
        u = _dot(hv, wu_ref[...])
        g_ref[...] = g
        u_ref[...] = u
        act_ref[...] = (g * jax.nn.sigmoid(g) * u).astype(BF16)

    blk = pl.BlockSpec((tm, bw), lambda j, i: (i, j))
    return pl.pallas_call(
        body, name=name, grid=(half, S // tm),
        in_specs=[pl.BlockSpec((tm, D), lambda j, i: (i, 0)), pl.BlockSpec((None, D, bw), lambda j, i: (j, 0, 0)),
                  pl.BlockSpec((None, D, bw), lambda j, i: (j + half, 0, 0))],
        out_specs=[blk, blk, blk],
        out_shape=[jax.ShapeDtypeStruct((S, half * bw), F32)] * 2 + [jax.ShapeDtypeStruct((S, half * bw), BF16)],
        compiler_params=_params("parallel", "parallel"),
    )(h, w_gu, w_gu)


def _ffn_down_dx(dy, w_down, g, u, name, tr=1024, tc=512):
    S, f = g.shape
    w2 = 2 * f
    tr, tc = min(tr, S), min(tc, f)
    assert S % tr == 0 and f % tc == 0
    n_ct = f // tc
    steps = (S // tr) * n_ct

    def body(dy_ref, w_ref, g_ref, u_ref, o_hbm, buf, sems):
        i, j = pl.program_id(0), pl.program_id(1)
        step = i * n_ct + j
        slot = step % 2

        def put(half):
            cols = pl.ds(pl.multiple_of(half * f + j * tc, 128), tc)
            return pltpu.make_async_copy(buf.at[slot, half], o_hbm.at[pl.ds(pl.multiple_of(i * tr, 8), tr), cols],
                                         sems.at[slot, half])

        @pl.when(step >= 2)
        def _():
            put(0).wait()
            put(1).wait()

        g = g_ref[...]
        sg = jax.nn.sigmoid(g)
        da = _dot(dy_ref[...], w_ref[...], NT)
        buf[slot, 0] = (da * u_ref[...] * (sg * (1.0 + g * (1.0 - sg)))).astype(BF16)
        buf[slot, 1] = (da * (g * sg)).astype(BF16)
        put(0).start()
        put(1).start()

        @pl.when(step == steps - 1)
        def _():
            put(0).wait()
            put(1).wait()
            if steps > 1:
                for half in (0, 1):
                    pltpu.make_async_copy(buf.at[1 - slot, half], o_hbm.at[pl.ds(0, tr), pl.ds(0, tc)],
                                          sems.at[1 - slot, half]).wait()

    D = dy.shape[1]
    return pl.pallas_call(
        body, name=name, grid=(S // tr, n_ct),
        in_specs=[pl.BlockSpec((tr, D), lambda i, j: (i, 0)), pl.BlockSpec((tc, D), lambda i, j: (j, 0)),
                  pl.BlockSpec((tr, tc), lambda i, j: (i, j)), pl.BlockSpec((tr, tc), lambda i, j: (i, j))],
        out_specs=ANY, out_shape=jax.ShapeDtypeStruct((S, w2), BF16),
        scratch_shapes=[pltpu.VMEM((2, 2, tr, tc), BF16), pltpu.SemaphoreType.DMA((2, 2))],
        compiler_params=_params("arbitrary", "arbitrary"),
    )(dy, w_down, g, u)


def _adamw_math(w, g, m, v):
    m = ADAM_B1 * m + (1.0 - ADAM_B1) * g
    v = ADAM_B2 * v + (1.0 - ADAM_B2) * (g * g)
    m_hat = m / (1.0 - ADAM_B1 ** ADAM_STEP)
    v_hat = v / (1.0 - ADAM_B2 ** ADAM_STEP)
    delta = -ADAM_LR * (m_hat / (jnp.sqrt(v_hat) + ADAM_EPS) + ADAM_WD * w)
    return delta, m, v


def _tile2(R, C, elems):
    if R % 16 == 0:
        return _row_tile(R, cap=max(16, elems // C // 16 * 16)), C
    tc = 128
    while C % (2 * tc) == 0 and R * 2 * tc <= elems:
        tc *= 2
    return R, tc


def _reduce_adamw(own, land, chip, w, m, v, name):
    _, R, C = own.shape
    tr, tc = _tile2(R, C, 1 << (19 if w.ndim == 3 else 18))

    def body(s_ref, o_ref, l1_ref, l2_ref, l3_ref, w_ref, m_ref, v_ref, g_ref, d_ref, nm_ref, nv_ref):
        g = o_ref[...].astype(F32) + l1_ref[...].astype(F32) + l2_ref[...].astype(F32) + l3_ref[...].astype(F32)
        g_ref[...] = g
        d_ref[...], nm_ref[...], nv_ref[...] = _adamw_math(w_ref[...], g, m_ref[...], v_ref[...])

    def part(flip):
        return pl.BlockSpec((None, tr, tc), lambda i, j, s: (s[0] ^ flip, i, j))

    if w.ndim == 3:
        blk = pl.BlockSpec((tr, None, tc), lambda i, j, s: (i, 0, j))
    else:
        blk = pl.BlockSpec((tr, tc), lambda i, j, s: (i, j))
    sds = jax.ShapeDtypeStruct(w.shape, F32)
    return pl.pallas_call(
        body, name=name,
        grid_spec=pltpu.PrefetchScalarGridSpec(
            num_scalar_prefetch=1, grid=(R // tr, C // tc),
            in_specs=[part(0), part(1), part(2), part(3), blk, blk, blk], out_specs=[blk] * 4),
        out_shape=[sds] * 4, compiler_params=_params("parallel", "parallel"),
    )(chip, own, land, land, land, w, m, v)


def _pair_sum(g8, recv, core, name):
    _, R, C = g8.shape
    tr, tc = _tile2(R, C, 1 << 21)

    def body(c_ref, a_ref, b_ref, o_ref):
        o_ref[...] = (a_ref[...].astype(F32) + b_ref[...].astype(F32)).astype(BF16)

    return pl.pallas_call(
        body, name=name,
        grid_spec=pltpu.PrefetchScalarGridSpec(
            num_scalar_prefetch=1, grid=(N_CHIP, R // tr, C // tc),
            in_specs=[pl.BlockSpec((None, tr, tc), lambda q, i, j, c: (2 * q + c[0], i, j)),
                      pl.BlockSpec((None, tr, tc), lambda q, i, j, c: (q, i, j))],
            out_specs=pl.BlockSpec((None, tr, tc), lambda q, i, j, c: (q, i, j))),
        out_shape=jax.ShapeDtypeStruct((N_CHIP, R, C), BF16),
        compiler_params=_params("parallel", "parallel", "parallel"),
    )(core, g8, recv)


def _place():
    return lax.axis_index("x"), lax.axis_index("y"), lax.axis_index("c")


def _land_with_own(shard, dev, name):
    R, C = shard.shape
    tr, tc = _tile2(R, C, 1 << 19)

    def body(d_ref, s_ref, o_ref):
        o_ref[...] = s_ref[...]

    return pl.pallas_call(
        body, name=name,
        grid_spec=pltpu.PrefetchScalarGridSpec(
            num_scalar_prefetch=1, grid=(R // tr, C // tc),
            in_specs=[pl.BlockSpec((tr, tc), lambda i, j, d: (i, j))],
            out_specs=pl.BlockSpec((None, tr, tc), lambda i, j, d: (d[0], i, j))),
        out_shape=jax.ShapeDtypeStruct((N_DEV, R, C), shard.dtype),
        compiler_params=_params("parallel", "parallel"),
    )(dev, shard)


def _all_gather(shards, lands, name):
    n = len(shards)

    def body(*refs):
        ins, outs = refs[:n], refs[2 * n:3 * n]
        send_sems, recv_sems = refs[3 * n:]
        x, y, c = _place()
        me, sibling = (x, y, c), (x, y, 1 - c)
        chips = [(1 - x, y), (x, 1 - y), (1 - x, 1 - y)]

        def copy(a, k, block, to, src=None):
            dst = outs[a].at[4 * block[0] + 2 * block[1] + block[2]]
            return pltpu.make_async_remote_copy(
                src_ref=dst if src is None else src, dst_ref=dst, send_sem=send_sems.at[a, k],
                recv_sem=recv_sems.at[a, k], device_id=to, device_id_type=MESH)

        sent = []
        for a in range(n):
            sent.append(copy(a, 0, me, sibling, src=ins[a]))
            sent += [copy(a, 1 + j, me, (*chip, c), src=ins[a]) for j, chip in enumerate(chips)]
        for cp in sent:
            cp.start()
        for j, chip in enumerate(chips):
            for a in range(n):
                copy(a, 1 + j, (*chip, c), me).wait_recv()
                fwd = copy(a, 4 + j, (*chip, c), sibling)
                fwd.start()
                sent.append(fwd)
        for a in range(n):
            copy(a, 0, sibling, me).wait_recv()
        for j, chip in enumerate(chips):
            for a in range(n):
                copy(a, 4 + j, (*chip, 1 - c), me).wait_recv()
        for cp in sent:
            cp.wait_send()

    return pl.pallas_call(
        body, name=name, in_specs=[ANY] * (2 * n), out_specs=[ANY] * n,
        out_shape=[jax.ShapeDtypeStruct(l.shape, l.dtype) for l in lands],
        input_output_aliases={n + a: a for a in range(n)},
        scratch_shapes=[pltpu.SemaphoreType.DMA((n, 7)), pltpu.SemaphoreType.DMA((n, 7))],
    )(*shards, *lands)


def _pair_plan(srcs, lands):
    x, y, c = _place()
    plan = []
    for a in range(len(srcs)):
        for q in range(N_CHIP):
            plan.append((srcs[a].at[2 * q + 1 - c], lands[a].at[q], lands[a].at[q], (x, y, 1 - c)))
    return plan


def _chip_plan(srcs, lands):
    x, y, c = _place()
    mine = 2 * x + y
    plan = []
    for a in range(len(srcs)):
        for px, py in [(1 - x, y), (x, 1 - y), (1 - x, 1 - y)]:
            peer = 2 * px + py
            plan.append((srcs[a].at[peer], lands[a].at[mine], lands[a].at[peer], (px, py, c)))
    return plan


def _gather_plan(srcs, lands):
    x, y, c = _place()
    mine = 4 * x + 2 * y + c
    plan = []
    for a in range(len(srcs)):
        for px, py, pc in [(x, y, 1 - c), (1 - x, y, c), (x, 1 - y, c), (1 - x, 1 - y, c)]:
            plan.append((srcs[a], lands[a].at[mine], lands[a].at[4 * px + 2 * py + pc], (px, py, pc)))
    return plan


def _forward_plan(srcs, lands):
    x, y, c = _place()
    plan = []
    for a in range(len(lands)):
        for px, py in [(1 - x, y), (x, 1 - y), (1 - x, 1 - y)]:
            mine, theirs = 4 * px + 2 * py + c, 4 * px + 2 * py + 1 - c
            plan.append((lands[a].at[mine], lands[a].at[mine], lands[a].at[theirs], (x, y, 1 - c)))
    return plan


def _remote(src, dst, send_sem, recv_sem, peer):
    return pltpu.make_async_remote_copy(src_ref=src, dst_ref=dst, send_sem=send_sem, recv_sem=recv_sem,
                                        device_id=peer, device_id_type=MESH)


HBM = pl.BlockSpec(memory_space=pltpu.HBM)
SEMS = pl.BlockSpec(memory_space=pltpu.SEMAPHORE)
DATAFLOW = pltpu.SideEffectType.DATAFLOW_SIDE_EFFECTING


def _split_start(srcs, lands, after, plan_fn, per_array, name):
    ns, nb = len(srcs), len(srcs) + len(lands)
    ncopy = per_array * len(lands)
    after = list(after) if isinstance(after, (list, tuple)) else [after]

    def body(*refs):
        send_sems, recv_sems = refs[nb + len(after)], refs[nb + len(after) + 1]
        token = refs[-1]
        for k, (src, dst, _, peer) in enumerate(plan_fn(refs[:ns], refs[ns:nb])):
            _remote(src, dst, send_sems.at[k], recv_sems.at[k], peer).start()
        token[...] = jnp.zeros_like(token)

    thru = [pltpu.HBM(a.shape, a.dtype) for a in (*srcs, *lands)]
    out = pl.pallas_call(
        body, name=name,
        out_shape=(pltpu.SemaphoreType.DMA((ncopy,)), pltpu.SemaphoreType.DMA((ncopy,)), *thru,
                   jax.ShapeDtypeStruct((8, 128), F32)),
        in_specs=[HBM] * nb + [ANY] * len(after),
        out_specs=(SEMS, SEMS, *[HBM] * nb, pl.BlockSpec(memory_space=pltpu.VMEM)),
        input_output_aliases={i: 2 + i for i in range(nb)},
        compiler_params=pltpu.CompilerParams(has_side_effects=DATAFLOW),
    )(*[pltpu.with_memory_space_constraint(a, pltpu.HBM) for a in (*srcs, *lands)], *after)
    return out[0], out[1], list(out[2:2 + ns]), list(out[2 + ns:2 + nb]), out[-1]


def _split_wait(send_sems, recv_sems, srcs, lands, after, plan_fn, name):
    ns, nb = len(srcs), len(srcs) + len(lands)
    after = list(after) if isinstance(after, (list, tuple)) else [after]

    def body(*refs):
        send_ref, recv_ref = refs[nb], refs[nb + 1]
        for k, (src, dst, mine, peer) in enumerate(plan_fn(refs[:ns], refs[ns:nb])):
            _remote(src, dst, send_ref.at[k], recv_ref.at[k], peer).wait_send()
            _remote(src, mine, send_ref.at[k], recv_ref.at[k], peer).wait_recv()

    thru = [pltpu.HBM(a.shape, a.dtype) for a in (*srcs, *lands)]
    out = pl.pallas_call(
        body, name=name, out_shape=tuple(thru),
        in_specs=[HBM] * nb + [SEMS, SEMS] + [ANY] * len(after), out_specs=tuple([HBM] * nb),
        input_output_aliases={i: i for i in range(nb)},
        compiler_params=pltpu.CompilerParams(has_side_effects=DATAFLOW),
    )(*srcs, *lands, send_sems, recv_sems, *after)
    return list(out[:ns]), list(out[ns:])


def _gather_forward(lands, name):
    n = len(lands)

    def body(*refs):
        landed, out = refs[:n], refs[n:2 * n]
        send_sems, recv_sems = refs[2 * n:]
        x, y, c = _place()
        chips = [(1 - x, y), (x, 1 - y), (1 - x, 1 - y)]
        sent = []
        for a in range(n):
            for j, (px, py) in enumerate(chips):
                blk = 4 * px + 2 * py + c
                sent.append(_remote(landed[a].at[blk], out[a].at[blk], send_sems.at[a, j], recv_sems.at[a, j], (x, y, 1 - c)))
        for cp in sent:
            cp.start()
        for a in range(n):
            for j, (px, py) in enumerate(chips):
                blk = 4 * px + 2 * py + 1 - c
                _remote(landed[a].at[blk], out[a].at[blk], send_sems.at[a, j], recv_sems.at[a, j], (x, y, 1 - c)).wait_recv()
        for cp in sent:
            cp.wait_send()

    return pl.pallas_call(
        body, name=name, in_specs=[ANY] * n, out_specs=[ANY] * n,
        out_shape=[jax.ShapeDtypeStruct(l.shape, l.dtype) for l in lands],
        input_output_aliases={a: a for a in range(n)},
        scratch_shapes=[pltpu.SemaphoreType.DMA((n, 3)), pltpu.SemaphoreType.DMA((n, 3))],
    )(*lands)


def _pack_rows(groups, name):
    n = len(groups[0])
    rows = -(-n // 8) * 8
    width = max(v.shape[1] for v in groups[0])

    def body(*refs):
        ins, outs = refs[:n * len(groups)], refs[n * len(groups):]
        for gi, o_ref in enumerate(outs):
            o_ref[...] = jnp.zeros_like(o_ref)
            for r in range(n):
                v_ref = ins[gi * n + r]
                o_ref[r:r + 1, 0:v_ref.shape[1]] = v_ref[...]

    return pl.pallas_call(
        body, name=name, out_shape=[jax.ShapeDtypeStruct((rows, width), F32)] * len(groups), compiler_params=_params(),
    )(*[v for g in groups for v in g])


def _unpack_rows(packs, like, name):
    n = len(like)

    def body(*refs):
        ins, outs = refs[:len(packs)], refs[len(packs):]
        for pi, p_ref in enumerate(ins):
            for r in range(n):
                o_ref = outs[pi * n + r]
                o_ref[...] = p_ref[r:r + 1, 0:o_ref.shape[1]]

    flat = pl.pallas_call(
        body, name=name, out_shape=[jax.ShapeDtypeStruct(a.shape, F32) for _ in packs for a in like],
        compiler_params=_params(),
    )(*packs)
    return [flat[pi * n:(pi + 1) * n] for pi in range(len(packs))]


def _sum_adamw(parts, w, m, v, name, col=None):
    P, R, _ = parts.shape
    C = w.shape[1]
    tr = _row_tile(R, cap=512, mult=8)

    def body(c_ref, p_ref, w_ref, m_ref, v_ref, g_ref, d_ref, nm_ref, nv_ref):
        g = p_ref[0].astype(F32)
        for q in range(1, P):
            g = g + p_ref[q].astype(F32)
        g_ref[...] = g
        d_ref[...], nm_ref[...], nv_ref[...] = _adamw_math(w_ref[...], g, m_ref[...], v_ref[...])

    blk = pl.BlockSpec((tr, C), lambda i, c: (i, 0))
    sds = jax.ShapeDtypeStruct((R, C), F32)
    at = jnp.zeros((1,), jnp.int32) if col is None else col
    return pl.pallas_call(
        body, name=name,
        grid_spec=pltpu.PrefetchScalarGridSpec(
            num_scalar_prefetch=1, grid=(R // tr,),
            in_specs=[pl.BlockSpec((P, tr, C), lambda i, c: (0, i, c[0])), blk, blk, blk], out_specs=[blk] * 4),
        out_shape=[sds] * 4, compiler_params=_params("parallel"),
    )(at, parts, w, m, v)


def _local_step(x, mem, target, sp, W, arrived, want, done, paired):
    h0 = _rms_fwd([x], [sp["g_mix"]], "rms_mix")
    proj = _mm(h0, W["main_t"], "nt", F32, "proj_main", tm=2048, tn=512)
    f_logit = _mm(h0, W["f_t"], "nt", F32, "proj_f", tm=2048)
    qn, kn, vb = _qk_fwd(proj, sp["g_q"], sp["g_k"], "qk_norm")
    b_pad = jnp.pad(sp["b_f"], ((0, 0), (0, HEAD - FOX_H)))
    c_col, c_row = _fgate_fwd(f_logit, b_pad, "forget_cumsum")
    o_fox, lse = _fox_fwd(qn, kn, vb, c_col, c_row, "fox_fwd")
    lru_p = (sp["conv_w"], sp["conv_b"], sp["w_ra"], sp["b_ra"], sp["w_ri"], sp["b_ri"], sp["lam"])
    y_lru, h_lru = _lru_fwd(proj, *lru_p, "lru_fwd", after=arrived("attn", o_fox))
    w_out, w_cq, w_ckv, w_co = want("attn", y_lru)
    mix = _rms_fwd([o_fox, y_lru], [sp["g_fox_out"], sp["g_lru_out"]], "rms_mix_out")
    x1 = _mm(mix, w_out, "nn", F32, "out_proj", add=x)
    hq = _rms_fwd([x1], [sp["g_xattn"]], "rms_xattn")
    mn = _rms_fwd([mem], [sp["g_mem"]], "rms_mem")
    cq = _mm(hq, w_cq, "nn", F32, "xattn_q", tm=2048)
    ckv = _mm(mn, w_ckv, "nn", F32, "xattn_kv")
    o_x = _xattn_fwd(cq, ckv, sp["g_cq"], sp["g_ck"], "xattn_fwd")
    co_w = w_co.shape[2]
    x2 = _mm(o_x, w_co, "nn", F32, "xattn_out", add=x1, tm=2048, tn=co_w, after=arrived("gate_up", o_x))
    hf = _rms_fwd([x2], [sp["g_ffn"]], "rms_ffn")
    (w_gu,) = want("gate_up", hf)
    gu_w = w_gu.shape[2]
    f_gate, f_up, act = _ffn_up(hf, w_gu, "ffn_up")
    (w_down,) = want("down", arrived("down", act))
    x3 = _mm(act, w_down, "nn", F32, "ffn_down", add=x2, tk=w_down.shape[0] // 2)
    dy, dyb, sq = _loss_head(x3, target, "loss_head")
    g_down = _mm(act, dyb, "tn", BF16, "ffn_down_dw", tm=w_down.shape[0] // 4, tn=2048)
    dgu = _ffn_down_dx(dyb, w_down, f_gate, f_up, "ffn_down_dx")
    g_gu = _mm(hf, dgu, "tn", BF16, "ffn_up_dw", out_blocks=N_DEV, tn=gu_w)
    tok = done("ffn", [g_gu, g_down])
    dhf = _mm(dgu, w_gu, "nt", F32, "ffn_up_dx", tn=2048, tk=gu_w, after=tok)
    tok = paired("ffn", dhf)
    dx2, dx2b, dg_ffn = _rms_bwd(x2, sp["g_ffn"], dhf, "rms_ffn_bwd", res=dy, after=tok)
    d_ox = _mm(dx2b, w_co, "nt", F32, "xattn_out_dx", tm=2048, tk=co_w)
    g_co = _mm(o_x, dx2b, "tn", BF16, "xattn_out_dw", out_blocks=N_DEV, tn=co_w)
    dcq, dckv, dg_cq, dg_ck = _xattn_bwd(cq, ckv, sp["g_cq"], sp["g_ck"], d_ox, "xattn_bwd")
    dhq = _mm(dcq, w_cq, "nt", F32, "xattn_q_dx", tm=2048)
    g_cq = _mm(hq, dcq, "tn", BF16, "xattn_q_dw")
    dmn = _mm(dckv, w_ckv, "nt", F32, "xattn_kv_dx")
    g_ckv = _mm(mn, dckv, "tn", BF16, "xattn_kv_dw")
    (dg_mem,) = _rms_bwd(mem, sp["g_mem"], dmn, "rms_mem_bwd", want_dx=False)
    dx1, dx1b, dg_xattn = _rms_bwd(x1, sp["g_xattn"], dhq, "rms_xattn_bwd", res=dx2)
    dmix = _mm(dx1b, w_out, "nt", F32, "out_proj_dx")
    g_out = _mm(mix, dx1b, "tn", BF16, "out_proj_dw", tn=2048)
    tok = done("attn", [g_out, g_cq, g_ckv, g_co])
    do_fox, dg_fox_out = _rms_bwd(o_fox, sp["g_fox_out"], dmix, "rms_fox_out_bwd", dy_col=0, want_bf16=False, after=tok)
    dy_lru, dg_lru_out = _rms_bwd(y_lru, sp["g_lru_out"], dmix, "rms_lru_out_bwd", dy_col=1, want_bf16=False)
    tok = paired("attn", dy_lru)
    du, dgate, dconv_w, dconv_b, dw_ra, db_ra, dw_ri, db_ri, dlam = _lru_bwd(proj, h_lru, dy_lru, *lru_p, "lru_bwd", after=tok)
    dqn, dkn, dv, dc_col, dc_row = _fox_bwd(qn, kn, vb, c_col, c_row, lse, o_fox, do_fox, "fox_bwd")
    dq, dk, dg_q, dg_k = _qk_bwd(proj, sp["g_q"], sp["g_k"], dqn, dkn, "qk_norm_bwd")
    df, db_f = _fgate_bwd(f_logit, b_pad, dc_col, dc_row, "forget_cumsum_bwd")
    dproj = jnp.concatenate([dq, dk, dv, du, dgate], axis=1)
    g_main_t = _mm(dproj, h0, "tn", BF16, "proj_main_dw", tn=2048)
    g_f_t = _mm(df, h0, "tn", BF16, "proj_f_dw", tn=2048)
    tok = done("w_in", [g_main_t, g_f_t])
    dh_f = _mm(df, W["f_t"], "nn", F32, "proj_f_dx", tm=2048)
    dh0 = _mm(dproj, W["main_t"], "nn", F32, "proj_main_dx", add=dh_f, tk=dproj.shape[1] // 2, after=tok)
    tok = paired("w_in", dh0)
    grad_x, dg_mix = _rms_bwd(x, sp["g_mix"], dh0, "rms_mix_bwd", res=dx1, want_bf16=False, after=tok)
    small = dict(g_mix=dg_mix, b_f=db_f[:, :FOX_H], g_q=dg_q, g_k=dg_k, conv_w=dconv_w, conv_b=dconv_b, w_ra=dw_ra,
                 b_ra=db_ra, w_ri=dw_ri, b_ri=db_ri, lam=dlam, g_fox_out=dg_fox_out, g_lru_out=dg_lru_out,
                 g_xattn=dg_xattn, g_mem=dg_mem, g_cq=dg_cq, g_ck=dg_ck, g_ffn=dg_ffn)
    return sq, grad_x, small


BIG = ("w_in", "w_out", "w_cq", "w_ckv", "w_co", "w_gate_up", "w_down")
SMALL = ("g_mix", "b_f", "g_q", "g_k", "conv_b", "w_ra", "b_ra", "w_ri", "b_ri", "lam", "g_fox_out", "g_lru_out",
         "g_xattn", "g_mem", "g_cq", "g_ck", "g_ffn")
ORDER = ("g_mix", "w_in", "b_f", "g_q", "g_k", "conv_w", "conv_b", "w_ra", "b_ra", "w_ri", "b_ri", "lam", "g_fox_out",
         "g_lru_out", "w_out", "g_xattn", "g_mem", "w_cq", "w_ckv", "g_cq", "g_ck", "w_co", "g_ffn", "w_gate_up", "w_down")


def kernel(x, mem, g_mix, w_in, b_f, g_q, g_k, conv_w, conv_b, w_ra, b_ra, w_ri, b_ri, lam, g_fox_out, g_lru_out, w_out, g_xattn, g_mem, w_cq, w_ckv, g_cq, g_ck, w_co, g_ffn, w_gate_up, w_down, loss_target, m_g_mix, m_w_in, m_b_f, m_g_q, m_g_k, m_conv_w, m_conv_b, m_w_ra, m_b_ra, m_w_ri, m_b_ri, m_lam, m_g_fox_out, m_g_lru_out, m_w_out, m_g_xattn, m_g_mem, m_w_cq, m_w_ckv, m_g_cq, m_g_ck, m_w_co, m_g_ffn, m_w_gate_up, m_w_down, v_g_mix, v_w_in, v_b_f, v_g_q, v_g_k, v_conv_w, v_conv_b, v_w_ra, v_b_ra, v_w_ri, v_b_ri, v_lam, v_g_fox_out, v_g_lru_out, v_w_out, v_g_xattn, v_g_mem, v_w_cq, v_w_ckv, v_g_cq, v_g_ck, v_w_co, v_g_ffn, v_w_gate_up, v_w_down):
    given = dict(locals())
    w = {n: given[n] for n in ORDER}
    m = {n: given["m_" + n] for n in ORDER}
    v = {n: given["v_" + n] for n in ORDER}
    D = x.shape[2]
    fw = FOX_H * HEAD
    dev_index = 4 * lax.axis_index("x") + 2 * lax.axis_index("y") + lax.axis_index("c")
    dev = jnp.reshape(dev_index, (1,)).astype(jnp.int32)
    core = jnp.reshape(lax.axis_index("c"), (1,)).astype(jnp.int32)
    chip = jnp.reshape(2 * lax.axis_index("x") + lax.axis_index("y"), (1,)).astype(jnp.int32)

    def shard(d, n):
        return jnp.transpose(d[n], (2, 0, 1)) if n == "w_in" else d[n][0]

    def unshard(a, n):
        return jnp.transpose(a, (1, 2, 0)) if n == "w_in" else a[None]

    gather_groups = dict(attn=("w_out", "w_cq", "w_ckv", "w_co"), gate_up=("w_gate_up",), down=("w_down",))
    reduce_groups = dict(ffn=("w_gate_up", "w_down"), attn=("w_out", "w_cq", "w_ckv", "w_co"), w_in=("w_in",))
    column_blocked = ("w_co", "w_gate_up")
    flying = {}

    def land(a, tag):
        return _land_with_own(a, dev, "own_" + tag)

    def launch(group, after):
        shards = [shard(w, n).astype(BF16) for n in gather_groups[group]]
        lands = [land(s, n) for s, n in zip(shards, gather_groups[group])]
        flying[group] = _split_start(shards, lands, after, _gather_plan, 4, "gather_" + group + "_start")
        return flying[group][4]

    def arrived(group, after):
        send, recv, shards, lands, _ = flying.pop(group)
        _, lands = _split_wait(send, recv, shards, lands, after, _gather_plan, "gather_" + group + "_wait")
        flying[group] = _split_start([], lands, after, _forward_plan, 3, "gather_" + group + "_forward_start")
        if group == "attn":
            return launch("down", flying[group][4])
        return flying[group][4]

    def want(group, after):
        send, recv, _, lands, _ = flying.pop(group)
        _, full = _split_wait(send, recv, [], lands, after, _forward_plan, "gather_" + group + "_forward_wait")
        return [g if n in column_blocked else g.reshape(-1, g.shape[2]) for n, g in zip(gather_groups[group], full)]

    first = [shard(w, "w_in").astype(BF16).reshape(-1, D), conv_w[0]]
    g_in, g_conv = _all_gather(first, [land(first[0], "w_in"), land(first[1], "conv_w")], "gather_w_in")
    tok = launch("gate_up", launch("attn", g_conv))
    per = g_in.shape[1]
    f_k, f_lo = divmod(3 * fw, per)
    assert f_lo + FOX_H <= per
    wt_in = g_in.reshape(-1, D)
    W = dict(main_t=jnp.concatenate([wt_in[:3 * fw], wt_in[3 * fw + FOX_H:]], axis=0),
             f_t=jnp.pad(wt_in[3 * fw:3 * fw + FOX_H], ((0, HEAD - FOX_H), (0, 0))))
    sp = {n: w[n] for n in SMALL if n not in ("w_ra", "w_ri")}
    sp["w_ra"], sp["w_ri"] = w_ra[0], w_ri[0]
    sp["conv_w"] = jnp.transpose(g_conv, (1, 0, 2)).reshape(CONV_K, -1)
    sp["g_mix"] = sp["g_mix"] + tok[0, 0]

    pairing, reducing = {}, {}

    def done(group, grads):
        if group == "w_in":
            g_main_t, g_f_t = grads
            shards = [g_main_t[k * per:(k + 1) * per] for k in range(f_k)]
            shards.append(jnp.concatenate([g_main_t[f_k * per:3 * fw], g_f_t[:FOX_H],
                                           g_main_t[3 * fw:(f_k + 1) * per - FOX_H]], axis=0))
            shards += [g_main_t[k * per - FOX_H:(k + 1) * per - FOX_H] for k in range(f_k + 1, N_DEV)]
            grads = [jnp.stack(shards)]
        g8 = [g if g.ndim == 3 else g.reshape(N_DEV, -1, g.shape[1]) for g in grads]
        lands = [lax.empty((N_CHIP,) + g.shape[1:], g.dtype) for g in g8]
        pairing[group] = _split_start(g8, lands, chip, _pair_plan, N_CHIP, "reduce_" + group + "_pair_start")
        return pairing[group][4]

    def paired(group, after):
        send, recv, g8, lands, _ = pairing.pop(group)
        g8, from_sibling = _split_wait(send, recv, g8, lands, after, _pair_plan, "reduce_" + group + "_pair_wait")
        p4 = [_pair_sum(g, r, core, "reduce_pair_sum_" + n) for g, r, n in zip(g8, from_sibling, reduce_groups[group])]
        lands = [lax.empty(p.shape, p.dtype) for p in p4]
        reducing[group] = _split_start(p4, lands, chip, _chip_plan, 3, "reduce_" + group + "_start")
        return reducing[group][4]

    def finish(group, after):
        send, recv, p4, lands, _ = reducing.pop(group)
        p4, lands = _split_wait(send, recv, p4, lands, after, _chip_plan, "reduce_" + group + "_wait")
        return {n: tuple(unshard(r, n) for r in _reduce_adamw(p, l, chip, shard(w, n), shard(m, n), shard(v, n), "adamw_" + n))
                for n, p, l in zip(reduce_groups[group], p4, lands)}

    sq, grad_x, gs = _local_step(x[0], mem[0], loss_target[0], sp, W, arrived, want, done, paired)

    vectors = tuple(n for n in SMALL if n not in ("w_ra", "w_ri"))
    mine = [_pack_rows([[gs[n] for n in vectors] + [sq[0:1]]], "pack_small_grads")[0], gs["w_ra"].reshape(-1, HEAD),
            gs["w_ri"].reshape(-1, HEAD), gs["conv_w"]]
    lands = [land(a, "small_grads_%d" % k) for k, a in enumerate(mine)]
    s_send, s_recv, s_src, s_land, tok = _split_start(mine, lands, grad_x, _gather_plan, 4, "gather_small_start")
    out = finish("ffn", tok)
    out.update(finish("attn", tok))
    updated = [r for n in reduce_groups["ffn"] + reduce_groups["attn"] for r in out[n]]
    _, s_land = _split_wait(s_send, s_recv, s_src, s_land, updated, _gather_plan, "gather_small_wait")
    all_vec, all_ra, all_ri, all_conv = _gather_forward(s_land, "gather_small_forward")

    nothing = jnp.zeros((1, HEAD), F32)
    state = _pack_rows([[d[n] for n in vectors] + [nothing] for d in (w, m, v)], "pack_small_state")
    upd = _sum_adamw(all_vec, *state, "adamw_small_vectors")
    per_vector = _unpack_rows(upd, [w[n] for n in vectors] + [nothing], "unpack_small")
    for r, n in enumerate(vectors):
        out[n] = tuple(per_vector[k][r] for k in range(4))
    loss = 0.5 * per_vector[0][len(vectors)][0, 0] / D
    for n, parts in (("w_ra", all_ra), ("w_ri", all_ri)):
        res = _sum_adamw(parts, *[d[n].reshape(-1, HEAD) for d in (w, m, v)], "adamw_" + n)
        out[n] = tuple(r.reshape(w[n].shape) for r in res)
    out["conv_w"] = tuple(r[None] for r in _sum_adamw(all_conv, conv_w[0], m_conv_w[0], v_conv_w[0], "adamw_conv_w", col=dev))
    out.update(finish("w_in", upd[1]))

    return (loss, grad_x[None], *[out[n][0] for n in ORDER], *[out[n][1] for n in ORDER],
            *[out[n][2] for n in ORDER], *[out[n][3] for n in ORDER])
```

```python
import functools
import math

import jax
import jax.numpy as jnp
from jax import lax
from jax.experimental import pallas as pl
from jax.experimental.pallas import tpu as pltpu

F32 = jnp.float32
BF16 = jnp.bfloat16
MESH = pl.DeviceIdType.MESH
N_DEV = 8
N_CHIP = 4

HEAD = 128
FOX_H = 8
XATT_H = 4
LRU_NB = 8
CONV_K = 4
LRU_C = 8.0
RMS_EPS = 1e-6
ATT_T = 1024
CUM_T = 256
ROW_T = 256
NEG = -1e30
V7X_VMEM_LIMIT = 48 * 1024 * 1024
V7X_VMEM_LIMIT_HIGH = 60 * 1024 * 1024

ADAM_LR = 0.001
ADAM_B1 = 0.9
ADAM_B2 = 0.999
ADAM_EPS = 1e-08
ADAM_WD = 0.01
ADAM_STEP = 10

NT = (((1,), (1,)), ((), ()))
TN = (((0,), (0,)), ((), ()))
NN = (((1,), (0,)), ((), ()))

ANY = pl.BlockSpec(memory_space=pl.ANY)


def _behind(after):
    return ([], []) if after is None else ([ANY], [after])


def _params(*sem):
    return pltpu.CompilerParams(dimension_semantics=sem or None, vmem_limit_bytes=V7X_VMEM_LIMIT)


def _dot(a, b, dn=NN):
    return lax.dot_general(a, b, dn, preferred_element_type=F32)


def _rms(x, g):
    r = lax.rsqrt(jnp.mean(x * x, axis=-1, keepdims=True) + RMS_EPS)
    return x * r * g


def _rms_grad(x, g, dy):
    r = lax.rsqrt(jnp.mean(x * x, axis=-1, keepdims=True) + RMS_EPS)
    xh = x * r
    dxh = dy * g
    dx = r * (dxh - xh * jnp.mean(dxh * xh, axis=-1, keepdims=True))
    return dx, jnp.sum(dy * xh, axis=0, keepdims=True)


def _gelu(x):
    k = math.sqrt(2.0 / math.pi)
    return 0.5 * x * (1.0 + jnp.tanh(k * (x + 0.044715 * (x * x * x))))


def _gelu_grad(x):
    k = math.sqrt(2.0 / math.pi)
    t = jnp.tanh(k * (x + 0.044715 * (x * x * x)))
    return 0.5 * (1.0 + t) + 0.5 * x * (1.0 - t * t) * k * (1.0 + 3.0 * 0.044715 * x * x)


def _one_minus_exp(z):
    series = -(z + 0.5 * z * z + (1.0 / 6.0) * z * z * z)
    return jnp.where(z > -1e-3, series, 1.0 - jnp.exp(z))


def _row_tile(rows, cap=ROW_T, mult=16):
    t = min(rows, cap)
    while rows % t or (t % mult and t != rows):
        t -= 1
    return t


def _mat_dims(z):
    return (z.shape[-2], z.shape[-1] * (z.shape[0] if z.ndim == 3 else 1))


def _mat_spec(arr, rblk, cblk, rsel, csel):
    if arr.ndim == 2:
        return pl.BlockSpec((rblk, cblk), lambda i, j, k: ((i, j, k)[rsel], (i, j, k)[csel]))
    nw = arr.shape[2]
    assert nw % cblk == 0, (arr.shape, cblk)
    per = nw // cblk
    return pl.BlockSpec((None, rblk, cblk),
                        lambda i, j, k: ((i, j, k)[csel] // per, (i, j, k)[rsel], (i, j, k)[csel] % per))


def _mm(a, b, mode, out_dtype, name, add=None, out_blocks=None, tm=1024, tn=1024, tk=None, after=None):
    ar, ac = _mat_dims(a)
    br, bc = _mat_dims(b)
    if mode == "nn":
        M, K, N = ar, ac, bc
        assert br == K
    elif mode == "nt":
        M, K, N = ar, ac, br
        assert bc == K
    else:
        M, K, N = ac, ar, bc
        assert br == K
    tm, tn = min(tm, M), min(tn, N)
    tk = K if tk is None or mode == "tn" else min(tk, K)
    assert M % tm == 0 and N % tn == 0 and K % tk == 0, (name, M, N, K, tm, tn, tk)
    nk = K // tk
    nj = N // tn
    if mode == "nn":
        specs = [_mat_spec(a, tm, tk, 0, 2), _mat_spec(b, tk, tn, 2, 1)]
        dn = NN
    elif mode == "nt":
        specs = [_mat_spec(a, tm, tk, 0, 2), _mat_spec(b, tn, tk, 1, 2)]
        dn = NT
    else:
        specs = [_mat_spec(a, tk, tm, 2, 0), _mat_spec(b, tk, tn, 2, 1)]
        dn = TN
    args = [a, b]
    if add is not None:
        specs.append(pl.BlockSpec((tm, tn), lambda i, j, k: (i, j)))
        args.append(add)
    specs += _behind(after)[0]
    args += _behind(after)[1]
    n_in = len(args)
    if out_blocks is None:
        out_shape = jax.ShapeDtypeStruct((M, N), out_dtype)
        out_spec = pl.BlockSpec((tm, tn), lambda i, j, k: (i, j))
    else:
        nb = out_blocks
        nw = N // nb
        assert nw % tn == 0
        per = nw // tn
        out_shape = jax.ShapeDtypeStruct((nb, M, nw), out_dtype)
        out_spec = pl.BlockSpec((None, tm, tn), lambda i, j, k: (j // per, i, j % per))
    keep_t = mode == "tn" and nj > 1
    scratch = []
    if nk > 1:
        scratch.append(pltpu.VMEM((tm, tn), F32))
    if keep_t:
        scratch.append(pltpu.VMEM((tm, tk), a.dtype))

    def body(*refs):
        a_ref, b_ref = refs[0], refs[1]
        add_ref = refs[2] if add is not None else None
        o_ref = refs[n_in]

        def finish(r):
            if add_ref is not None:
                r = r + add_ref[...]
            o_ref[...] = r.astype(out_dtype)

        if keep_t:
            at_ref = refs[-1]

            @pl.when(pl.program_id(1) == 0)
            def _():
                at_ref[...] = a_ref[...].T

            finish(_dot(at_ref[...], b_ref[...], NN))
        elif nk == 1:
            finish(_dot(a_ref[...], b_ref[...], dn))
        else:
            acc_ref = refs[n_in + 1]
            k = pl.program_id(2)

            @pl.when(k == 0)
            def _():
                acc_ref[...] = jnp.zeros_like(acc_ref)

            acc_ref[...] += _dot(a_ref[...], b_ref[...], dn)

            @pl.when(k == nk - 1)
            def _():
                finish(acc_ref[...])

    return pl.pallas_call(
        body, name=name, grid=(M // tm, nj, nk), in_specs=specs, out_specs=out_spec, out_shape=out_shape,
        scratch_shapes=scratch,
        compiler_params=_params("parallel", "arbitrary" if keep_t else "parallel", "arbitrary"),
    )(*args)


def _rms_fwd(xs, gs, name, after=None):
    n = len(xs)
    rows = xs[0].shape[0]
    widths = [x.shape[1] for x in xs]
    tr = _row_tile(rows)
    more_specs, more = _behind(after)

    def body(*refs):
        o_ref = refs[2 * n + len(more)]
        off = 0
        for t in range(n):
            o_ref[:, off:off + widths[t]] = _rms(refs[t][...], refs[n + t][...]).astype(BF16)
            off += widths[t]

    return pl.pallas_call(
        body, name=name, grid=(rows // tr,),
        in_specs=[pl.BlockSpec((tr, w), lambda i: (i, 0)) for w in widths]
        + [pl.BlockSpec((1, w), lambda i: (0, 0)) for w in widths] + more_specs,
        out_specs=pl.BlockSpec((tr, sum(widths)), lambda i: (i, 0)),
        out_shape=jax.ShapeDtypeStruct((rows, sum(widths)), BF16),
        compiler_params=_params("parallel"),
    )(*xs, *gs, *more)


def _rms_bwd(x, g, dy, name, dy_col=0, res=None, want_dx=True, want_bf16=True, after=None):
    rows, w = x.shape
    tr = _row_tile(rows)
    has_res = res is not None
    more_specs, more = _behind(after)

    def body(*refs):
        x_ref, g_ref, dy_ref = refs[:3]
        res_ref = refs[3] if has_res else None
        outs = refs[3 + has_res + len(more):]
        dg_ref = outs[-1]
        dx, dg = _rms_grad(x_ref[...], g_ref[...], dy_ref[...])

        @pl.when(pl.program_id(0) == 0)
        def _():
            dg_ref[...] = jnp.zeros_like(dg_ref)

        dg_ref[...] += dg
        if want_dx:
            if has_res:
                dx = dx + res_ref[...]
            outs[0][...] = dx
            if want_bf16:
                outs[1][...] = dx.astype(BF16)

    row_spec = pl.BlockSpec((tr, w), lambda i: (i, 0))
    in_specs = [row_spec, pl.BlockSpec((1, w), lambda i: (0, 0)), pl.BlockSpec((tr, w), lambda i: (i, dy_col))]
    args = [x, g, dy]
    if has_res:
        in_specs.append(row_spec)
        args.append(res)
    in_specs += more_specs
    args += more
    out_specs, out_shape = [], []
    if want_dx:
        out_specs += [row_spec] * (1 + want_bf16)
        out_shape += [jax.ShapeDtypeStruct((rows, w), F32)] + [jax.ShapeDtypeStruct((rows, w), BF16)] * want_bf16
    out_specs.append(pl.BlockSpec((1, w), lambda i: (0, 0)))
    out_shape.append(jax.ShapeDtypeStruct((1, w), F32))
    return pl.pallas_call(
        body, name=name, grid=(rows // tr,), in_specs=in_specs, out_specs=out_specs, out_shape=out_shape,
        compiler_params=_params("arbitrary"),
    )(*args)


def _loss_head(y, target, name):
    rows, w = y.shape
    tr = _row_tile(rows)

    def body(y_ref, t_ref, dy_ref, dyb_ref, acc_ref):
        e = y_ref[...] - t_ref[...]

        @pl.when(pl.program_id(0) == 0)
        def _():
            acc_ref[...] = jnp.zeros_like(acc_ref)

        acc_ref[...] += jnp.sum(e * e)
        dy = e * (1.0 / w)
        dy_ref[...] = dy
        dyb_ref[...] = dy.astype(BF16)

    row_spec = pl.BlockSpec((tr, w), lambda i: (i, 0))
    return pl.pallas_call(
        body, name=name, grid=(rows // tr,), in_specs=[row_spec, row_spec],
        out_specs=[row_spec, row_spec, pl.BlockSpec((8, 128), lambda i: (0, 0))],
        out_shape=[jax.ShapeDtypeStruct((rows, w), F32), jax.ShapeDtypeStruct((rows, w), BF16),
                   jax.ShapeDtypeStruct((8, 128), F32)],
        compiler_params=_params("arbitrary"),
    )(y, target)


def _qk_fwd(proj, g_q, g_k, name, after=None):
    rows = proj.shape[0]
    w = FOX_H * HEAD
    tr = _row_tile(rows)
    more_specs, more = _behind(after)

    def body(q_ref, k_ref, v_ref, gq_ref, gk_ref, *rest):
        qn_ref, kn_ref, vb_ref = rest[len(more):]
        for h in range(FOX_H):
            sl = slice(h * HEAD, (h + 1) * HEAD)
            qn_ref[:, sl] = _rms(q_ref[:, sl], gq_ref[...]).astype(BF16)
            kn_ref[:, sl] = _rms(k_ref[:, sl], gk_ref[...]).astype(BF16)
        vb_ref[...] = v_ref[...].astype(BF16)

    gain = pl.BlockSpec((1, HEAD), lambda i: (0, 0))
    out = pl.BlockSpec((tr, w), lambda i: (i, 0))
    return pl.pallas_call(
        body, name=name, grid=(rows // tr,),
        in_specs=[pl.BlockSpec((tr, w), lambda i: (i, 0)), pl.BlockSpec((tr, w), lambda i: (i, 1)),
                  pl.BlockSpec((tr, w), lambda i: (i, 2)), gain, gain] + more_specs,
        out_specs=[out, out, out], out_shape=[jax.ShapeDtypeStruct((rows, w), BF16)] * 3,
        compiler_params=_params("parallel"),
    )(proj, proj, proj, g_q, g_k, *more)


def _qk_bwd(proj, g_q, g_k, dqn, dkn, name):
    rows = proj.shape[0]
    w = FOX_H * HEAD
    tr = _row_tile(rows)

    def body(q_ref, k_ref, gq_ref, gk_ref, dqn_ref, dkn_ref, dq_ref, dk_ref, dgq_ref, dgk_ref):
        @pl.when(pl.program_id(0) == 0)
        def _():
            dgq_ref[...] = jnp.zeros_like(dgq_ref)
            dgk_ref[...] = jnp.zeros_like(dgk_ref)

        for h in range(FOX_H):
            sl = slice(h * HEAD, (h + 1) * HEAD)
            dq, dgq = _rms_grad(q_ref[:, sl], gq_ref[...], dqn_ref[:, sl])
            dk, dgk = _rms_grad(k_ref[:, sl], gk_ref[...], dkn_ref[:, sl])
            dq_ref[:, sl] = dq.astype(BF16)
            dk_ref[:, sl] = dk.astype(BF16)
            dgq_ref[...] += dgq
            dgk_ref[...] += dgk

    gain = pl.BlockSpec((1, HEAD), lambda i: (0, 0))
    row = pl.BlockSpec((tr, w), lambda i: (i, 0))
    return pl.pallas_call(
        body, name=name, grid=(rows // tr,),
        in_specs=[row, pl.BlockSpec((tr, w), lambda i: (i, 1)), gain, gain, row, row],
        out_specs=[row, row, gain, gain],
        out_shape=[jax.ShapeDtypeStruct((rows, w), BF16)] * 2 + [jax.ShapeDtypeStruct((1, HEAD), F32)] * 2,
        compiler_params=_params("arbitrary"),
    )(proj, proj, g_q, g_k, dqn, dkn)


def _fgate_fwd(f_logit, b_pad, name):
    S = f_logit.shape[0]
    T = min(CUM_T, S)
    nb = S // T
    RT = min(ATT_T, S)
    nbr = S // RT

    def body(f_ref, b_ref, col_ref, row_ref, c_scr):
        tri = (lax.broadcasted_iota(jnp.int32, (T, T), 0) >= lax.broadcasted_iota(jnp.int32, (T, T), 1)).astype(F32)
        carry = jnp.zeros((1, HEAD), F32)
        for blk in range(nb):
            z = f_ref[blk * T:(blk + 1) * T, :] + b_ref[...]
            lf = jnp.minimum(z, 0.0) - jnp.log1p(jnp.exp(-jnp.abs(z)))
            cb = jnp.dot(tri, lf, precision=lax.Precision.HIGHEST, preferred_element_type=F32) + carry
            c_scr[blk * T:(blk + 1) * T, :] = cb
            carry = cb[T - 1:T, :]
        c = c_scr[...]
        lane = lax.broadcasted_iota(jnp.int32, c.shape, 1)
        ct = c.T
        for h in range(FOX_H):
            col_ref[h] = jnp.sum(jnp.where(lane == h, c, 0.0), axis=1, keepdims=True)
            for jb in range(nbr):
                row_ref[h, jb] = ct[h:h + 1, jb * RT:(jb + 1) * RT]

    return pl.pallas_call(
        body, name=name,
        out_shape=[jax.ShapeDtypeStruct((FOX_H, S, 1), F32), jax.ShapeDtypeStruct((FOX_H, nbr, 1, RT), F32)],
        scratch_shapes=[pltpu.VMEM((S, HEAD), F32)], compiler_params=_params(),
    )(f_logit, b_pad)


def _fgate_bwd(f_logit, b_pad, dc_col, dc_row, name):
    S = f_logit.shape[0]
    T = min(CUM_T, S)
    nb = S // T
    RT = min(ATT_T, S)

    def body(f_ref, b_ref, dcol_ref, drow_ref, df_ref, db_ref, dc_scr, rt_scr):
        lane = lax.broadcasted_iota(jnp.int32, (S, HEAD), 1)
        sub = lax.broadcasted_iota(jnp.int32, (HEAD, RT), 0)
        dc = jnp.zeros((S, HEAD), F32)
        for h in range(FOX_H):
            dc = jnp.where(lane == h, dcol_ref[h], dc)
        for jb in range(S // RT):
            rt = jnp.zeros((HEAD, RT), F32)
            for h in range(FOX_H):
                rt = jnp.where(sub == h, drow_ref[h, jb], rt)
            rt_scr[jb * RT:(jb + 1) * RT, :] = rt.T
        dc_scr[...] = dc - rt_scr[...]
        tri = (lax.broadcasted_iota(jnp.int32, (T, T), 0) <= lax.broadcasted_iota(jnp.int32, (T, T), 1)).astype(F32)
        carry = jnp.zeros((1, HEAD), F32)
        db = jnp.zeros((1, HEAD), F32)
        for blk in reversed(range(nb)):
            rows = slice(blk * T, (blk + 1) * T)
            dlf = jnp.dot(tri, dc_scr[rows, :], precision=lax.Precision.HIGHEST, preferred_element_type=F32) + carry
            carry = dlf[0:1, :]
            z = f_ref[rows, :] + b_ref[...]
            df = dlf * jax.nn.sigmoid(-z)
            df_ref[rows, :] = df.astype(BF16)
            db = db + jnp.sum(df, axis=0, keepdims=True)
        db_ref[...] = db

    return pl.pallas_call(
        body, name=name,
        out_shape=[jax.ShapeDtypeStruct((S, HEAD), BF16), jax.ShapeDtypeStruct((1, HEAD), F32)],
        scratch_shapes=[pltpu.VMEM((S, HEAD), F32), pltpu.VMEM((S, HEAD), F32)], compiler_params=_params(),
    )(f_logit, b_pad, dc_col, dc_row)


def _fox_fwd(qn, kn, vb, c_col, c_row, name, after=None):
    S = qn.shape[0]
    T = min(ATT_T, S)
    nb = S // T
    scale = 1.0 / math.sqrt(HEAD)
    more_specs, more = _behind(after)

    def body(q_ref, k_ref, v_ref, cc_ref, cr_ref, *rest):
        o_ref, lse_ref = rest[len(more):]
        i = pl.program_id(1)
        q = q_ref[...]
        cc = cc_ref[...]

        def step(j, carry, diagonal=False):
            m, l, acc = carry
            sl = pl.ds(pl.multiple_of(j * T, T), T)
            s = _dot(q, k_ref[sl, :], NT) * scale + cc - cr_ref[j]
            if diagonal:
                s = jnp.where(lax.broadcasted_iota(jnp.int32, (T, T), 0) >= lax.broadcasted_iota(jnp.int32, (T, T), 1), s, NEG)
            m2 = jnp.maximum(m, jnp.max(s, axis=1, keepdims=True))
            p = jnp.exp(s - m2)
            al = jnp.exp(m - m2)
            return m2, al * l + jnp.sum(p, axis=1, keepdims=True), al * acc + _dot(p.astype(BF16), v_ref[sl, :])

        init = (jnp.full((T, 1), NEG, F32), jnp.zeros((T, 1), F32), jnp.zeros((T, HEAD), F32))
        m, l, acc = step(i, lax.fori_loop(0, i, step, init), diagonal=True)
        o_ref[...] = acc / l
        lse_ref[...] = m + jnp.log(l)

    head = pl.BlockSpec((S, HEAD), lambda h, i: (0, h))
    return pl.pallas_call(
        body, name=name, grid=(FOX_H, nb),
        in_specs=[pl.BlockSpec((T, HEAD), lambda h, i: (i, h)), head, head,
                  pl.BlockSpec((None, T, 1), lambda h, i: (h, i, 0)),
                  pl.BlockSpec((None, nb, 1, T), lambda h, i: (h, 0, 0, 0))] + more_specs,
        out_specs=[pl.BlockSpec((T, HEAD), lambda h, i: (i, h)), pl.BlockSpec((None, T, 1), lambda h, i: (h, i, 0))],
        out_shape=[jax.ShapeDtypeStruct((S, FOX_H * HEAD), F32), jax.ShapeDtypeStruct((FOX_H, S, 1), F32)],
        compiler_params=_params("parallel", "parallel"),
    )(qn, kn, vb, c_col, c_row, *more)


def _fox_bwd(qn, kn, vb, c_col, c_row, lse, o, do, name):
    S = qn.shape[0]
    T = min(ATT_T, S)
    Th = T // 2
    nb = S // T
    scale = 1.0 / math.sqrt(HEAD)

    def body(q_ref, k_ref, v_ref, cc_ref, cr_ref, lse_ref, o_ref, do_ref, dq_ref, dk_ref, dv_ref, dcc_ref, dcr_ref):
        j = pl.program_id(1)

        @pl.when(j == 0)
        def _():
            dq_ref[...] = jnp.zeros_like(dq_ref)
            dcc_ref[...] = jnp.zeros_like(dcc_ref)

        k = k_ref[...]
        v = v_ref[...]
        cr = cr_ref[...]

        def tile(sl, kk, vv, crr, carry, masked):
            dk, dv, dcr = carry
            q = q_ref[sl, :]
            d_o = do_ref[sl, :]
            s = _dot(q, kk, NT) * scale + cc_ref[sl, :] - crr
            if masked:
                s = jnp.where(lax.broadcasted_iota(jnp.int32, s.shape, 0) >= lax.broadcasted_iota(jnp.int32, s.shape, 1), s, NEG)
            p = jnp.exp(s - lse_ref[sl, :])
            dob = d_o.astype(BF16)
            dp = _dot(dob, vv, NT)
            delta = jnp.sum(d_o * o_ref[sl, :], axis=1, keepdims=True)
            ds = p * (dp - delta)
            dsb = ds.astype(BF16)
            dq_ref[sl, :] += _dot(dsb, kk) * scale
            dcc_ref[sl, :] += jnp.sum(ds, axis=1, keepdims=True)
            return (dk + _dot(dsb, q, TN), dv + _dot(p.astype(BF16), dob, TN), dcr + jnp.sum(ds, axis=0, keepdims=True))

        def step(i, carry):
            return tile(pl.ds(pl.multiple_of(i * T, T), T), k, v, cr, carry, False)

        def rows(r0):
            return pl.ds(pl.multiple_of(j * T + r0, Th), Th)

        zero = (jnp.zeros((Th, HEAD), F32), jnp.zeros((Th, HEAD), F32), jnp.zeros((1, Th), F32))
        left = tile(rows(Th), k[:Th], v[:Th], cr[:, :Th], tile(rows(0), k[:Th], v[:Th], cr[:, :Th], zero, True), False)
        right = tile(rows(Th), k[Th:], v[Th:], cr[:, Th:], zero, True)
        diagonal = (jnp.concatenate([left[0], right[0]], axis=0), jnp.concatenate([left[1], right[1]], axis=0),
                    jnp.concatenate([left[2], right[2]], axis=1))
        dk, dv, dcr = lax.fori_loop(j + 1, nb, step, diagonal)
        dk_ref[...] = dk * scale
        dv_ref[...] = dv.astype(BF16)
        dcr_ref[...] = dcr

    head = pl.BlockSpec((S, HEAD), lambda h, j: (0, h))
    tile = pl.BlockSpec((T, HEAD), lambda h, j: (j, h))
    col = pl.BlockSpec((None, S, 1), lambda h, j: (h, 0, 0))
    row = pl.BlockSpec((None, None, 1, T), lambda h, j: (h, j, 0, 0))
    w = FOX_H * HEAD
    return pl.pallas_call(
        body, name=name, grid=(FOX_H, nb),
        in_specs=[head, tile, tile, col, row, col, head, head],
        out_specs=[head, tile, tile, col, row],
        out_shape=[jax.ShapeDtypeStruct((S, w), F32), jax.ShapeDtypeStruct((S, w), F32), jax.ShapeDtypeStruct((S, w), BF16),
                   jax.ShapeDtypeStruct((FOX_H, S, 1), F32), jax.ShapeDtypeStruct((FOX_H, nb, 1, T), F32)],
        compiler_params=_params("arbitrary", "arbitrary"),
    )(qn, kn, vb, c_col, c_row, lse, o, do)


LRU_STEP_BLOCKS = 2


def _block_dot(x, w, dn=NN):
    return jnp.concatenate([_dot(x[:, b * HEAD:(b + 1) * HEAD], w[b], dn) for b in range(w.shape[0])], axis=1)


def _lru_gates(uc, wra, bra, wri, bri, lam):
    ucb = uc.astype(BF16)
    r = jax.nn.sigmoid(_block_dot(ucb, wra.astype(BF16)) + bra)
    ig = jax.nn.sigmoid(_block_dot(ucb, wri.astype(BF16)) + bri)
    sp = jnp.maximum(-lam, 0.0) + jnp.log1p(jnp.exp(-jnp.abs(lam)))
    log_a = -LRU_C * r * sp
    a = jnp.exp(log_a)
    mult = jnp.sqrt(_one_minus_exp(2.0 * log_a))
    return r, ig, sp, a, mult


def _conv(pad_ref, cw, cb, S):
    uc = cb
    for j in range(CONV_K):
        uc = uc + cw[j:j + 1, :] * pad_ref[5 + j:5 + j + S, :]
    return uc


def _lru_specs(S, G):
    bw = G * HEAD
    u_col = 3 * FOX_H // G
    g_col = u_col + LRU_NB // G
    blk = pl.BlockSpec((S, bw), lambda n: (0, n))
    vec = pl.BlockSpec((1, bw), lambda n: (0, n))
    mat = pl.BlockSpec((G, HEAD, HEAD), lambda n: (n, 0, 0))
    return dict(
        u=pl.BlockSpec((S, bw), lambda n: (0, u_col + n)), gate=pl.BlockSpec((S, bw), lambda n: (0, g_col + n)),
        blk=blk, vec=vec, mat=mat, cw=pl.BlockSpec((CONV_K, bw), lambda n: (0, n)))


def _lru_fwd(proj, conv_w, conv_b, w_ra, b_ra, w_ri, b_ri, lam, name, after=None):
    S = proj.shape[0]
    G = LRU_STEP_BLOCKS
    bw = G * HEAD
    sp_ = _lru_specs(S, G)
    rows8 = S // 8
    more_specs, more = _behind(after)

    def body(u_ref, gt_ref, cw_ref, cb_ref, wra_ref, bra_ref, wri_ref, bri_ref, lam_ref, *rest):
        y_ref, h_ref, pad, a_scr, b_scr = rest[len(more):]
        pad[0:8, :] = jnp.zeros((8, bw), F32)
        pad[8:S + 8, :] = u_ref[...]
        uc = _conv(pad, cw_ref[...], cb_ref[...], S)
        r, ig, sp, a, mult = _lru_gates(uc, wra_ref[...], bra_ref[...], wri_ref[...], bri_ref[...], lam_ref[...])
        a_scr[...] = a
        b_scr[...] = mult * (ig * uc)
        sub = lax.broadcasted_iota(jnp.int32, (8, bw), 0)

        def step(t, carry):
            sl = pl.ds(pl.multiple_of(t * 8, 8), 8)
            A, B = a_scr[sl, :], b_scr[sl, :]
            for d in (1, 2, 4):
                m = sub >= d
                B = jnp.where(m, A * pltpu.roll(B, d, 0) + B, B)
                A = jnp.where(m, A * pltpu.roll(A, d, 0), A)
            h = A * carry + B
            h_ref[sl, :] = h
            return h[7:8, :]

        lax.fori_loop(0, rows8, step, jnp.zeros((1, bw), F32))
        y_ref[...] = h_ref[...] * _gelu(gt_ref[...])

    w = LRU_NB * HEAD
    return pl.pallas_call(
        body, name=name, grid=(LRU_NB // G,),
        in_specs=[sp_["u"], sp_["gate"], sp_["cw"], sp_["vec"], sp_["mat"], sp_["vec"], sp_["mat"], sp_["vec"], sp_["vec"]]
        + more_specs,
        out_specs=[sp_["blk"], sp_["blk"]],
        out_shape=[jax.ShapeDtypeStruct((S, w), F32)] * 2,
        scratch_shapes=[pltpu.VMEM((S + 8, bw), F32), pltpu.VMEM((S, bw), F32), pltpu.VMEM((S, bw), F32)],
        compiler_params=_params("parallel"),
    )(proj, proj, conv_w, conv_b, w_ra, b_ra, w_ri, b_ri, lam, *more)


def _lru_bwd(proj, h, dy, conv_w, conv_b, w_ra, b_ra, w_ri, b_ri, lam, name, after=None):
    S = proj.shape[0]
    G = LRU_STEP_BLOCKS
    bw = G * HEAD
    sp_ = _lru_specs(S, G)
    rows8 = S // 8
    more_specs, more = _behind(after)

    def body(u_ref, gt_ref, h_ref, dy_ref, cw_ref, cb_ref, wra_ref, bra_ref, wri_ref, bri_ref, lam_ref, *rest):
        (du_ref, dgt_ref, dcw_ref, dcb_ref, dwra_ref, dbra_ref, dwri_ref, dbri_ref, dlam_ref,
         pad, an_scr, d_scr, g_scr, hp_scr) = rest[len(more):]
        zero8 = jnp.zeros((8, bw), F32)
        pad[0:8, :] = zero8
        pad[8:S + 8, :] = u_ref[...]
        cw = cw_ref[...]
        uc = _conv(pad, cw, cb_ref[...], S)
        wra, wri, lam_v = wra_ref[...], wri_ref[...], lam_ref[...]
        r, ig, sp, a, mult = _lru_gates(uc, wra, bra_ref[...], wri, bri_ref[...], lam_v)
        gate = gt_ref[...]
        dy_v = dy_ref[...]
        hv = h_ref[...]
        dgt_ref[...] = (dy_v * hv * _gelu_grad(gate)).astype(BF16)
        d_scr[...] = dy_v * _gelu(gate)
        g_scr[0:S, :] = a
        g_scr[S:S + 8, :] = zero8
        an_scr[...] = g_scr[1:S + 1, :]
        sub = lax.broadcasted_iota(jnp.int32, (8, bw), 0)

        def step(t, carry):
            sl = pl.ds(pl.multiple_of((rows8 - 1 - t) * 8, 8), 8)
            A, D = an_scr[sl, :], d_scr[sl, :]
            for d in (1, 2, 4):
                m = sub + d <= 7
                D = jnp.where(m, A * pltpu.roll(D, 8 - d, 0) + D, D)
                A = jnp.where(m, A * pltpu.roll(A, 8 - d, 0), A)
            g = A * carry + D
            g_scr[sl, :] = g
            return g[0:1, :]

        lax.fori_loop(0, rows8, step, jnp.zeros((1, bw), F32))
        g = g_scr[0:S, :]
        hp_scr[0:8, :] = zero8
        hp_scr[8:S + 8, :] = hv
        da = g * hp_scr[7:S + 7, :]
        iu = ig * uc
        dmult = g * iu
        diu = g * mult
        dig = diu * uc
        duc = diu * ig
        dlog_a = da * a - dmult * (a * a) / mult
        dr = dlog_a * (-LRU_C * sp)
        dsp = jnp.sum(dlog_a * (-LRU_C * r), axis=0, keepdims=True)
        dlam_ref[...] = -dsp * jax.nn.sigmoid(-lam_v)
        dpr = dr * r * (1.0 - r)
        dpi = dig * ig * (1.0 - ig)
        dbra_ref[...] = jnp.sum(dpr, axis=0, keepdims=True)
        dbri_ref[...] = jnp.sum(dpi, axis=0, keepdims=True)
        ucb = uc.astype(BF16)
        dprb, dpib = dpr.astype(BF16), dpi.astype(BF16)
        for b in range(G):
            cols = slice(b * HEAD, (b + 1) * HEAD)
            dwra_ref[b] = _dot(ucb[:, cols], dprb[:, cols], TN).astype(BF16)
            dwri_ref[b] = _dot(ucb[:, cols], dpib[:, cols], TN).astype(BF16)
        duc = duc + _block_dot(dprb, wra.astype(BF16), NT) + _block_dot(dpib, wri.astype(BF16), NT)
        dcb_ref[...] = jnp.sum(duc, axis=0, keepdims=True)
        for j in range(CONV_K):
            dcw_ref[j:j + 1, :] = jnp.sum(duc * pad[5 + j:5 + j + S, :], axis=0, keepdims=True)
        g_scr[0:S, :] = duc
        g_scr[S:S + 8, :] = zero8
        du = jnp.zeros((S, bw), F32)
        for j in range(CONV_K):
            du = du + cw[j:j + 1, :] * g_scr[3 - j:3 - j + S, :]
        du_ref[...] = du.astype(BF16)

    w = LRU_NB * HEAD
    bf = jax.ShapeDtypeStruct((S, w), BF16)
    vec = jax.ShapeDtypeStruct((1, w), F32)
    mat = jax.ShapeDtypeStruct((LRU_NB, HEAD, HEAD), BF16)
    return pl.pallas_call(
        body, name=name, grid=(LRU_NB // G,),
        in_specs=[sp_["u"], sp_["gate"], sp_["blk"], sp_["blk"], sp_["cw"], sp_["vec"], sp_["mat"], sp_["vec"],
                  sp_["mat"], sp_["vec"], sp_["vec"]] + more_specs,
        out_specs=[sp_["blk"], sp_["blk"], sp_["cw"], sp_["vec"], sp_["mat"], sp_["vec"], sp_["mat"], sp_["vec"], sp_["vec"]],
        out_shape=[bf, bf, jax.ShapeDtypeStruct((CONV_K, w), F32), vec, mat, vec, mat, vec, vec],
        scratch_shapes=[pltpu.VMEM((S + 8, bw), F32), pltpu.VMEM((S, bw), F32), pltpu.VMEM((S, bw), F32),
                        pltpu.VMEM((S + 8, bw), F32), pltpu.VMEM((S + 8, bw), F32)],
        compiler_params=pltpu.CompilerParams(dimension_semantics=("parallel",), vmem_limit_bytes=V7X_VMEM_LIMIT_HIGH),
    )(proj, proj, h, dy, conv_w, conv_b, w_ra, b_ra, w_ri, b_ri, lam, *more)


def _xattn_fwd(cq, ckv, g_cq, g_ck, name):
    S, w = cq.shape
    M = ckv.shape[0]
    tr = _row_tile(S)
    scale = 1.0 / math.sqrt(HEAD)

    def body(cq_ref, ckv_ref, gq_ref, gk_ref, o_ref):
        for h in range(XATT_H):
            sl = slice(h * HEAD, (h + 1) * HEAD)
            qn = _rms(cq_ref[:, sl], gq_ref[...]).astype(BF16)
            kn = _rms(ckv_ref[:, sl], gk_ref[...]).astype(BF16)
            v = ckv_ref[:, w + h * HEAD:w + (h + 1) * HEAD].astype(BF16)
            s = _dot(qn, kn, NT) * scale
            p = jnp.exp(s - jnp.max(s, axis=1, keepdims=True))
            p = p / jnp.sum(p, axis=1, keepdims=True)
            o_ref[:, sl] = _dot(p.astype(BF16), v).astype(BF16)

    gain = pl.BlockSpec((1, HEAD), lambda i: (0, 0))
    return pl.pallas_call(
        body, name=name, grid=(S // tr,),
        in_specs=[pl.BlockSpec((tr, w), lambda i: (i, 0)), pl.BlockSpec((M, 2 * w), lambda i: (0, 0)), gain, gain],
        out_specs=pl.BlockSpec((tr, w), lambda i: (i, 0)), out_shape=jax.ShapeDtypeStruct((S, w), BF16),
        compiler_params=_params("parallel"),
    )(cq, ckv, g_cq, g_ck)


def _xattn_bwd(cq, ckv, g_cq, g_ck, do, name):
    S, w = cq.shape
    M = ckv.shape[0]
    tr = _row_tile(S)
    nsteps = S // tr
    scale = 1.0 / math.sqrt(HEAD)

    def body(cq_ref, ckv_ref, gq_ref, gk_ref, do_ref, dcq_ref, dckv_ref, dgq_ref, dgk_ref, dkn_scr, dv_scr):
        step = pl.program_id(0)

        @pl.when(step == 0)
        def _():
            dkn_scr[...] = jnp.zeros_like(dkn_scr)
            dv_scr[...] = jnp.zeros_like(dv_scr)
            dgq_ref[...] = jnp.zeros_like(dgq_ref)

        for h in range(XATT_H):
            sl = slice(h * HEAD, (h + 1) * HEAD)
            q_raw = cq_ref[:, sl]
            qn = _rms(q_raw, gq_ref[...]).astype(BF16)
            kn = _rms(ckv_ref[:, sl], gk_ref[...]).astype(BF16)
            v = ckv_ref[:, w + h * HEAD:w + (h + 1) * HEAD].astype(BF16)
            s = _dot(qn, kn, NT) * scale
            p = jnp.exp(s - jnp.max(s, axis=1, keepdims=True))
            p = p / jnp.sum(p, axis=1, keepdims=True)
            dob = do_ref[:, sl].astype(BF16)
            dp = _dot(dob, v, NT)
            ds = p * (dp - jnp.sum(p * dp, axis=1, keepdims=True)) * scale
            dsb = ds.astype(BF16)
            dv_scr[:, sl] += _dot(p.astype(BF16), dob, TN)
            dkn_scr[:, sl] += _dot(dsb, qn, TN)
            dq, dgq = _rms_grad(q_raw, gq_ref[...], _dot(dsb, kn))
            dcq_ref[:, sl] = dq.astype(BF16)
            dgq_ref[...] += dgq

        @pl.when(step == nsteps - 1)
        def _():
            dgk = jnp.zeros((1, HEAD), F32)
            for h in range(XATT_H):
                sl = slice(h * HEAD, (h + 1) * HEAD)
                dk, dgk_h = _rms_grad(ckv_ref[:, sl], gk_ref[...], dkn_scr[:, sl])
                dckv_ref[:, sl] = dk.astype(BF16)
                dgk = dgk + dgk_h
            dckv_ref[:, w:2 * w] = dv_scr[...].astype(BF16)
            dgk_ref[...] = dgk

    gain = pl.BlockSpec((1, HEAD), lambda i: (0, 0))
    row = pl.BlockSpec((tr, w), lambda i: (i, 0))
    mem = pl.BlockSpec((M, 2 * w), lambda i: (0, 0))
    return pl.pallas_call(
        body, name=name, grid=(nsteps,), in_specs=[row, mem, gain, gain, row], out_specs=[row, mem, gain, gain],
        out_shape=[jax.ShapeDtypeStruct((S, w), BF16), jax.ShapeDtypeStruct((M, 2 * w), BF16),
                   jax.ShapeDtypeStruct((1, HEAD), F32), jax.ShapeDtypeStruct((1, HEAD), F32)],
        scratch_shapes=[pltpu.VMEM((M, w), F32), pltpu.VMEM((M, w), F32)],
        compiler_params=_params("arbitrary"),
    )(cq, ckv, g_cq, g_ck, do)


def _ffn_up(h, w_gu, name, blocks, into=None, after=None, tm=512):
    S, D = h.shape
    nb, _, bw = w_gu.shape
    half = nb // 2
    j0, j1 = blocks
    tm = min(tm, S)
    kept = [] if into is None else list(into)
    more_specs, more = _behind(after)

    def body(h_ref, wg_ref, wu_ref, *rest):
        g_ref, u_ref, act_ref = rest[len(kept) + len(more):]
        hv = h_ref[...]
        g = _dot(hv, wg_ref[...])
        u = _dot(hv, wu_ref[...])
        g_ref[...] = g
        u_ref[...] = u
        act_ref[...] = (g * jax.nn.sigmoid(g) * u).astype(BF16)

    blk = pl.BlockSpec((tm, bw), lambda j, i: (i, j + j0))
    return pl.pallas_call(
        body, name=name, grid=(j1 - j0, S // tm),
        in_specs=[pl.BlockSpec((tm, D), lambda j, i: (i, 0)), pl.BlockSpec((None, D, bw), lambda j, i: (j + j0, 0, 0)),
                  pl.BlockSpec((None, D, bw), lambda j, i: (j + j0 + half, 0, 0))] + [ANY] * len(kept) + more_specs,
        out_specs=[blk, blk, blk],
        out_shape=[jax.ShapeDtypeStruct((S, half * bw), F32)] * 2 + [jax.ShapeDtypeStruct((S, half * bw), BF16)],
        input_output_aliases={3 + k: k for k in range(len(kept))},
        compiler_params=_params("parallel", "parallel"),
    )(h, w_gu, w_gu, *kept, *more)


def _ffn_down_dx(dy, w_down, g, u, name, tr=1024, tc=512):
    S, f = g.shape
    w2 = 2 * f
    tr, tc = min(tr, S), min(tc, f)
    assert S % tr == 0 and f % tc == 0
    n_ct = f // tc
    steps = (S // tr) * n_ct

    def body(dy_ref, w_ref, g_ref, u_ref, o_hbm, buf, sems):
        i, j = pl.program_id(0), pl.program_id(1)
        step = i * n_ct + j
        slot = step % 2

        def put(half):
            cols = pl.ds(pl.multiple_of(half * f + j * tc, 128), tc)
            return pltpu.make_async_copy(buf.at[slot, half], o_hbm.at[pl.ds(pl.multiple_of(i * tr, 8), tr), cols],
                                         sems.at[slot, half])

        @pl.when(step >= 2)
        def _():
            put(0).wait()
            put(1).wait()

        g = g_ref[...]
        sg = jax.nn.sigmoid(g)
        da = _dot(dy_ref[...], w_ref[...], NT)
        buf[slot, 0] = (da * u_ref[...] * (sg * (1.0 + g * (1.0 - sg)))).astype(BF16)
        buf[slot, 1] = (da * (g * sg)).astype(BF16)
        put(0).start()
        put(1).start()

        @pl.when(step == steps - 1)
        def _():
            put(0).wait()
            put(1).wait()
            if steps > 1:
                for half in (0, 1):
                    pltpu.make_async_copy(buf.at[1 - slot, half], o_hbm.at[pl.ds(0, tr), pl.ds(0, tc)],
                                          sems.at[1 - slot, half]).wait()

    D = dy.shape[1]
    return pl.pallas_call(
        body, name=name, grid=(S // tr, n_ct),
        in_specs=[pl.BlockSpec((tr, D), lambda i, j: (i, 0)), pl.BlockSpec((tc, D), lambda i, j: (j, 0)),
                  pl.BlockSpec((tr, tc), lambda i, j: (i, j)), pl.BlockSpec((tr, tc), lambda i, j: (i, j))],
        out_specs=ANY, out_shape=jax.ShapeDtypeStruct((S, w2), BF16),
        scratch_shapes=[pltpu.VMEM((2, 2, tr, tc), BF16), pltpu.SemaphoreType.DMA((2, 2))],
        compiler_params=_params("arbitrary", "arbitrary"),
    )(dy, w_down, g, u)


def _adamw_math(w, g, m, v):
    m = ADAM_B1 * m + (1.0 - ADAM_B1) * g
    v = ADAM_B2 * v + (1.0 - ADAM_B2) * (g * g)
    m_hat = m / (1.0 - ADAM_B1 ** ADAM_STEP)
    v_hat = v / (1.0 - ADAM_B2 ** ADAM_STEP)
    delta = -ADAM_LR * (m_hat / (jnp.sqrt(v_hat) + ADAM_EPS) + ADAM_WD * w)
    return delta, m, v


def _tile2(R, C, elems):
    if R % 16 == 0:
        return _row_tile(R, cap=max(16, elems // C // 16 * 16)), C
    tc = 128
    while C % (2 * tc) == 0 and R * 2 * tc <= elems:
        tc *= 2
    return R, tc


def _reduce_adamw(own, land, chip, w, m, v, name):
    _, R, C = own.shape
    tr, tc = _tile2(R, C, 1 << (19 if w.ndim == 3 else 18))

    def body(s_ref, o_ref, l1_ref, l2_ref, l3_ref, w_ref, m_ref, v_ref, g_ref, d_ref, nm_ref, nv_ref):
        g = o_ref[...].astype(F32) + l1_ref[...].astype(F32) + l2_ref[...].astype(F32) + l3_ref[...].astype(F32)
        g_ref[...] = g
        d_ref[...], nm_ref[...], nv_ref[...] = _adamw_math(w_ref[...], g, m_ref[...], v_ref[...])

    def part(flip):
        return pl.BlockSpec((None, tr, tc), lambda i, j, s: (s[0] ^ flip, i, j))

    if w.ndim == 3:
        blk = pl.BlockSpec((tr, None, tc), lambda i, j, s: (i, 0, j))
    else:
        blk = pl.BlockSpec((tr, tc), lambda i, j, s: (i, j))
    sds = jax.ShapeDtypeStruct(w.shape, F32)
    return pl.pallas_call(
        body, name=name,
        grid_spec=pltpu.PrefetchScalarGridSpec(
            num_scalar_prefetch=1, grid=(R // tr, C // tc),
            in_specs=[part(0), part(1), part(2), part(3), blk, blk, blk], out_specs=[blk] * 4),
        out_shape=[sds] * 4, compiler_params=_params("parallel", "parallel"),
    )(chip, own, land, land, land, w, m, v)


def _pair_sum(g8, recv, core, name):
    _, R, C = g8.shape
    tr, tc = _tile2(R, C, 1 << 21)

    def body(c_ref, a_ref, b_ref, o_ref):
        o_ref[...] = (a_ref[...].astype(F32) + b_ref[...].astype(F32)).astype(BF16)

    return pl.pallas_call(
        body, name=name,
        grid_spec=pltpu.PrefetchScalarGridSpec(
            num_scalar_prefetch=1, grid=(N_CHIP, R // tr, C // tc),
            in_specs=[pl.BlockSpec((None, tr, tc), lambda q, i, j, c: (2 * q + c[0], i, j)),
                      pl.BlockSpec((None, tr, tc), lambda q, i, j, c: (q, i, j))],
            out_specs=pl.BlockSpec((None, tr, tc), lambda q, i, j, c: (q, i, j))),
        out_shape=jax.ShapeDtypeStruct((N_CHIP, R, C), BF16),
        compiler_params=_params("parallel", "parallel", "parallel"),
    )(core, g8, recv)


def _place():
    return lax.axis_index("x"), lax.axis_index("y"), lax.axis_index("c")


def _land_with_own(shard, dev, name):
    R, C = shard.shape
    tr, tc = _tile2(R, C, 1 << 19)

    def body(d_ref, s_ref, o_ref):
        o_ref[...] = s_ref[...]

    return pl.pallas_call(
        body, name=name,
        grid_spec=pltpu.PrefetchScalarGridSpec(
            num_scalar_prefetch=1, grid=(R // tr, C // tc),
            in_specs=[pl.BlockSpec((tr, tc), lambda i, j, d: (i, j))],
            out_specs=pl.BlockSpec((None, tr, tc), lambda i, j, d: (d[0], i, j))),
        out_shape=jax.ShapeDtypeStruct((N_DEV, R, C), shard.dtype),
        compiler_params=_params("parallel", "parallel"),
    )(dev, shard)


def _all_gather(shards, lands, name):
    n = len(shards)

    def body(*refs):
        ins, outs = refs[:n], refs[2 * n:3 * n]
        send_sems, recv_sems = refs[3 * n:]
        x, y, c = _place()
        me, sibling = (x, y, c), (x, y, 1 - c)
        chips = [(1 - x, y), (x, 1 - y), (1 - x, 1 - y)]

        def copy(a, k, block, to, src=None):
            dst = outs[a].at[4 * block[0] + 2 * block[1] + block[2]]
            return pltpu.make_async_remote_copy(
                src_ref=dst if src is None else src, dst_ref=dst, send_sem=send_sems.at[a, k],
                recv_sem=recv_sems.at[a, k], device_id=to, device_id_type=MESH)

        sent = []
        for a in range(n):
            sent.append(copy(a, 0, me, sibling, src=ins[a]))
            sent += [copy(a, 1 + j, me, (*chip, c), src=ins[a]) for j, chip in enumerate(chips)]
        for cp in sent:
            cp.start()
        for j, chip in enumerate(chips):
            for a in range(n):
                copy(a, 1 + j, (*chip, c), me).wait_recv()
                fwd = copy(a, 4 + j, (*chip, c), sibling)
                fwd.start()
                sent.append(fwd)
        for a in range(n):
            copy(a, 0, sibling, me).wait_recv()
        for j, chip in enumerate(chips):
            for a in range(n):
                copy(a, 4 + j, (*chip, 1 - c), me).wait_recv()
        for cp in sent:
            cp.wait_send()

    return pl.pallas_call(
        body, name=name, in_specs=[ANY] * (2 * n), out_specs=[ANY] * n,
        out_shape=[jax.ShapeDtypeStruct(l.shape, l.dtype) for l in lands],
        input_output_aliases={n + a: a for a in range(n)},
        scratch_shapes=[pltpu.SemaphoreType.DMA((n, 7)), pltpu.SemaphoreType.DMA((n, 7))],
    )(*shards, *lands)


def _pair_plan(srcs, lands):
    x, y, c = _place()
    plan = []
    for a in range(len(srcs)):
        for q in range(N_CHIP):
            plan.append((srcs[a].at[2 * q + 1 - c], lands[a].at[q], lands[a].at[q], (x, y, 1 - c)))
    return plan


def _chip_plan(srcs, lands):
    x, y, c = _place()
    mine = 2 * x + y
    plan = []
    for a in range(len(srcs)):
        for px, py in [(1 - x, y), (x, 1 - y), (1 - x, 1 - y)]:
            peer = 2 * px + py
            plan.append((srcs[a].at[peer], lands[a].at[mine], lands[a].at[peer], (px, py, c)))
    return plan


def _gather_plan(srcs, lands):
    x, y, c = _place()
    mine = 4 * x + 2 * y + c
    plan = []
    for a in range(len(srcs)):
        for px, py, pc in [(x, y, 1 - c), (1 - x, y, c), (x, 1 - y, c), (1 - x, 1 - y, c)]:
            plan.append((srcs[a], lands[a].at[mine], lands[a].at[4 * px + 2 * py + pc], (px, py, pc)))
    return plan


def _forward_plan(srcs, lands):
    x, y, c = _place()
    plan = []
    for a in range(len(lands)):
        for px, py in [(1 - x, y), (x, 1 - y), (1 - x, 1 - y)]:
            mine, theirs = 4 * px + 2 * py + c, 4 * px + 2 * py + 1 - c
            plan.append((lands[a].at[mine], lands[a].at[mine], lands[a].at[theirs], (x, y, 1 - c)))
    return plan


def _remote(src, dst, send_sem, recv_sem, peer):
    return pltpu.make_async_remote_copy(src_ref=src, dst_ref=dst, send_sem=send_sem, recv_sem=recv_sem,
                                        device_id=peer, device_id_type=MESH)


HBM = pl.BlockSpec(memory_space=pltpu.HBM)
SEMS = pl.BlockSpec(memory_space=pltpu.SEMAPHORE)
DATAFLOW = pltpu.SideEffectType.DATAFLOW_SIDE_EFFECTING


def _split_start(srcs, lands, after, plan_fn, per_array, name):
    ns, nb = len(srcs), len(srcs) + len(lands)
    ncopy = per_array * len(lands)
    after = list(after) if isinstance(after, (list, tuple)) else [after]

    def body(*refs):
        send_sems, recv_sems = refs[nb + len(after)], refs[nb + len(after) + 1]
        token = refs[-1]
        for k, (src, dst, _, peer) in enumerate(plan_fn(refs[:ns], refs[ns:nb])):
            _remote(src, dst, send_sems.at[k], recv_sems.at[k], peer).start()
        token[...] = jnp.zeros_like(token)

    thru = [pltpu.HBM(a.shape, a.dtype) for a in (*srcs, *lands)]
    out = pl.pallas_call(
        body, name=name,
        out_shape=(pltpu.SemaphoreType.DMA((ncopy,)), pltpu.SemaphoreType.DMA((ncopy,)), *thru,
                   jax.ShapeDtypeStruct((8, 128), F32)),
        in_specs=[HBM] * nb + [ANY] * len(after),
        out_specs=(SEMS, SEMS, *[HBM] * nb, pl.BlockSpec(memory_space=pltpu.VMEM)),
        input_output_aliases={i: 2 + i for i in range(nb)},
        compiler_params=pltpu.CompilerParams(has_side_effects=DATAFLOW),
    )(*[pltpu.with_memory_space_constraint(a, pltpu.HBM) for a in (*srcs, *lands)], *after)
    return out[0], out[1], list(out[2:2 + ns]), list(out[2 + ns:2 + nb]), out[-1]


def _split_wait(send_sems, recv_sems, srcs, lands, after, plan_fn, name):
    ns, nb = len(srcs), len(srcs) + len(lands)
    after = list(after) if isinstance(after, (list, tuple)) else [after]

    def body(*refs):
        send_ref, recv_ref = refs[nb], refs[nb + 1]
        for k, (src, dst, mine, peer) in enumerate(plan_fn(refs[:ns], refs[ns:nb])):
            _remote(src, dst, send_ref.at[k], recv_ref.at[k], peer).wait_send()
            _remote(src, mine, send_ref.at[k], recv_ref.at[k], peer).wait_recv()

    thru = [pltpu.HBM(a.shape, a.dtype) for a in (*srcs, *lands)]
    out = pl.pallas_call(
        body, name=name, out_shape=tuple(thru),
        in_specs=[HBM] * nb + [SEMS, SEMS] + [ANY] * len(after), out_specs=tuple([HBM] * nb),
        input_output_aliases={i: i for i in range(nb)},
        compiler_params=pltpu.CompilerParams(has_side_effects=DATAFLOW),
    )(*srcs, *lands, send_sems, recv_sems, *after)
    return list(out[:ns]), list(out[ns:])


def _gather_forward(lands, name):
    n = len(lands)

    def body(*refs):
        landed, out = refs[:n], refs[n:2 * n]
        send_sems, recv_sems = refs[2 * n:]
        x, y, c = _place()
        chips = [(1 - x, y), (x, 1 - y), (1 - x, 1 - y)]
        sent = []
        for a in range(n):
            for j, (px, py) in enumerate(chips):
                blk = 4 * px + 2 * py + c
                sent.append(_remote(landed[a].at[blk], out[a].at[blk], send_sems.at[a, j], recv_sems.at[a, j], (x, y, 1 - c)))
        for cp in sent:
            cp.start()
        for a in range(n):
            for j, (px, py) in enumerate(chips):
                blk = 4 * px + 2 * py + 1 - c
                _remote(landed[a].at[blk], out[a].at[blk], send_sems.at[a, j], recv_sems.at[a, j], (x, y, 1 - c)).wait_recv()
        for cp in sent:
            cp.wait_send()

    return pl.pallas_call(
        body, name=name, in_specs=[ANY] * n, out_specs=[ANY] * n,
        out_shape=[jax.ShapeDtypeStruct(l.shape, l.dtype) for l in lands],
        input_output_aliases={a: a for a in range(n)},
        scratch_shapes=[pltpu.SemaphoreType.DMA((n, 3)), pltpu.SemaphoreType.DMA((n, 3))],
    )(*lands)


def _pack_rows(groups, name):
    n = len(groups[0])
    rows = -(-n // 8) * 8
    width = max(v.shape[1] for v in groups[0])

    def body(*refs):
        ins, outs = refs[:n * len(groups)], refs[n * len(groups):]
        for gi, o_ref in enumerate(outs):
            o_ref[...] = jnp.zeros_like(o_ref)
            for r in range(n):
                v_ref = ins[gi * n + r]
                o_ref[r:r + 1, 0:v_ref.shape[1]] = v_ref[...]

    return pl.pallas_call(
        body, name=name, out_shape=[jax.ShapeDtypeStruct((rows, width), F32)] * len(groups), compiler_params=_params(),
    )(*[v for g in groups for v in g])


def _unpack_rows(packs, like, name):
    n = len(like)

    def body(*refs):
        ins, outs = refs[:len(packs)], refs[len(packs):]
        for pi, p_ref in enumerate(ins):
            for r in range(n):
                o_ref = outs[pi * n + r]
                o_ref[...] = p_ref[r:r + 1, 0:o_ref.shape[1]]

    flat = pl.pallas_call(
        body, name=name, out_shape=[jax.ShapeDtypeStruct(a.shape, F32) for _ in packs for a in like],
        compiler_params=_params(),
    )(*packs)
    return [flat[pi * n:(pi + 1) * n] for pi in range(len(packs))]


def _sum_adamw(parts, w, m, v, name, col=None):
    P, R, _ = parts.shape
    C = w.shape[1]
    tr = _row_tile(R, cap=512, mult=8)

    def body(c_ref, p_ref, w_ref, m_ref, v_ref, g_ref, d_ref, nm_ref, nv_ref):
        g = p_ref[0].astype(F32)
        for q in range(1, P):
            g = g + p_ref[q].astype(F32)
        g_ref[...] = g
        d_ref[...], nm_ref[...], nv_ref[...] = _adamw_math(w_ref[...], g, m_ref[...], v_ref[...])

    blk = pl.BlockSpec((tr, C), lambda i, c: (i, 0))
    sds = jax.ShapeDtypeStruct((R, C), F32)
    at = jnp.zeros((1,), jnp.int32) if col is None else col
    return pl.pallas_call(
        body, name=name,
        grid_spec=pltpu.PrefetchScalarGridSpec(
            num_scalar_prefetch=1, grid=(R // tr,),
            in_specs=[pl.BlockSpec((P, tr, C), lambda i, c: (0, i, c[0])), blk, blk, blk], out_specs=[blk] * 4),
        out_shape=[sds] * 4, compiler_params=_params("parallel"),
    )(at, parts, w, m, v)


def _local_step(x, mem, target, sp, W, arrived, want, done, paired):
    h0 = _rms_fwd([x], [sp["g_mix"]], "rms_mix")
    proj = _mm(h0, W["main_t"], "nt", F32, "proj_main", tm=2048, tn=512)
    f_logit = _mm(h0, W["f_t"], "nt", F32, "proj_f", tm=2048)
    qn, kn, vb = _qk_fwd(proj, sp["g_q"], sp["g_k"], "qk_norm")
    b_pad = jnp.pad(sp["b_f"], ((0, 0), (0, HEAD - FOX_H)))
    c_col, c_row = _fgate_fwd(f_logit, b_pad, "forget_cumsum")
    o_fox, lse = _fox_fwd(qn, kn, vb, c_col, c_row, "fox_fwd")
    lru_p = (sp["conv_w"], sp["conv_b"], sp["w_ra"], sp["b_ra"], sp["w_ri"], sp["b_ri"], sp["lam"])
    y_lru, h_lru = _lru_fwd(proj, *lru_p, "lru_fwd", after=arrived("attn", o_fox))
    w_out, w_cq, w_ckv, w_co = want("attn", y_lru)
    mix = _rms_fwd([o_fox, y_lru], [sp["g_fox_out"], sp["g_lru_out"]], "rms_mix_out")
    x1 = _mm(mix, w_out, "nn", F32, "out_proj", add=x)
    hq = _rms_fwd([x1], [sp["g_xattn"]], "rms_xattn")
    mn = _rms_fwd([mem], [sp["g_mem"]], "rms_mem")
    cq = _mm(hq, w_cq, "nn", F32, "xattn_q", tm=2048)
    ckv = _mm(mn, w_ckv, "nn", F32, "xattn_kv")
    o_x = _xattn_fwd(cq, ckv, sp["g_cq"], sp["g_ck"], "xattn_fwd")
    co_w = w_co.shape[2]
    x2 = _mm(o_x, w_co, "nn", F32, "xattn_out", add=x1, tm=2048, tn=co_w, after=arrived("gate_up", o_x))
    hf = _rms_fwd([x2], [sp["g_ffn"]], "rms_ffn")
    (w_gu,) = want("gate_up", hf)
    gu_w = w_gu.shape[2]
    n_gate = w_gu.shape[0] // 2
    first = _ffn_up(hf, w_gu, "ffn_up_a", (0, n_gate // 2))
    f_gate, f_up, act = _ffn_up(hf, w_gu, "ffn_up_b", (n_gate // 2, n_gate), into=first, after=arrived("down", first[2]))
    (w_down,) = want("down", act)
    x3 = _mm(act, w_down, "nn", F32, "ffn_down", add=x2, tk=w_down.shape[0] // 2)
    dy, dyb, sq = _loss_head(x3, target, "loss_head")
    g_down = _mm(act, dyb, "tn", BF16, "ffn_down_dw", tm=w_down.shape[0] // 4, tn=2048)
    dgu = _ffn_down_dx(dyb, w_down, f_gate, f_up, "ffn_down_dx")
    g_gu = _mm(hf, dgu, "tn", BF16, "ffn_up_dw", out_blocks=N_DEV, tn=gu_w)
    tok = done("ffn", [g_gu, g_down])
    dhf = _mm(dgu, w_gu, "nt", F32, "ffn_up_dx", tn=2048, tk=gu_w, after=tok)
    tok = paired("ffn", dhf)
    dx2, dx2b, dg_ffn = _rms_bwd(x2, sp["g_ffn"], dhf, "rms_ffn_bwd", res=dy, after=tok)
    d_ox = _mm(dx2b, w_co, "nt", F32, "xattn_out_dx", tm=2048, tk=co_w)
    g_co = _mm(o_x, dx2b, "tn", BF16, "xattn_out_dw", out_blocks=N_DEV, tn=co_w)
    dcq, dckv, dg_cq, dg_ck = _xattn_bwd(cq, ckv, sp["g_cq"], sp["g_ck"], d_ox, "xattn_bwd")
    dhq = _mm(dcq, w_cq, "nt", F32, "xattn_q_dx", tm=2048)
    g_cq = _mm(hq, dcq, "tn", BF16, "xattn_q_dw")
    dmn = _mm(dckv, w_ckv, "nt", F32, "xattn_kv_dx")
    g_ckv = _mm(mn, dckv, "tn", BF16, "xattn_kv_dw")
    (dg_mem,) = _rms_bwd(mem, sp["g_mem"], dmn, "rms_mem_bwd", want_dx=False)
    dx1, dx1b, dg_xattn = _rms_bwd(x1, sp["g_xattn"], dhq, "rms_xattn_bwd", res=dx2)
    dmix = _mm(dx1b, w_out, "nt", F32, "out_proj_dx")
    g_out = _mm(mix, dx1b, "tn", BF16, "out_proj_dw", tn=2048)
    tok = done("attn", [g_out, g_cq, g_ckv, g_co])
    do_fox, dg_fox_out = _rms_bwd(o_fox, sp["g_fox_out"], dmix, "rms_fox_out_bwd", dy_col=0, want_bf16=False, after=tok)
    dy_lru, dg_lru_out = _rms_bwd(y_lru, sp["g_lru_out"], dmix, "rms_lru_out_bwd", dy_col=1, want_bf16=False)
    tok = paired("attn", dy_lru)
    du, dgate, dconv_w, dconv_b, dw_ra, db_ra, dw_ri, db_ri, dlam = _lru_bwd(proj, h_lru, dy_lru, *lru_p, "lru_bwd", after=tok)
    dqn, dkn, dv, dc_col, dc_row = _fox_bwd(qn, kn, vb, c_col, c_row, lse, o_fox, do_fox, "fox_bwd")
    dq, dk, dg_q, dg_k = _qk_bwd(proj, sp["g_q"], sp["g_k"], dqn, dkn, "qk_norm_bwd")
    df, db_f = _fgate_bwd(f_logit, b_pad, dc_col, dc_row, "forget_cumsum_bwd")
    dproj = jnp.concatenate([dq, dk, dv, du, dgate], axis=1)
    g_main_t = _mm(dproj, h0, "tn", BF16, "proj_main_dw", tn=2048)
    g_f_t = _mm(df, h0, "tn", BF16, "proj_f_dw", tn=2048)
    tok = done("w_in", [g_main_t, g_f_t])
    dh_f = _mm(df, W["f_t"], "nn", F32, "proj_f_dx", tm=2048)
    dh0 = _mm(dproj, W["main_t"], "nn", F32, "proj_main_dx", add=dh_f, tk=dproj.shape[1] // 2, after=tok)
    tok = paired("w_in", dh0)
    grad_x, dg_mix = _rms_bwd(x, sp["g_mix"], dh0, "rms_mix_bwd", res=dx1, want_bf16=False, after=tok)
    small = dict(g_mix=dg_mix, b_f=db_f[:, :FOX_H], g_q=dg_q, g_k=dg_k, conv_w=dconv_w, conv_b=dconv_b, w_ra=dw_ra,
                 b_ra=db_ra, w_ri=dw_ri, b_ri=db_ri, lam=dlam, g_fox_out=dg_fox_out, g_lru_out=dg_lru_out,
                 g_xattn=dg_xattn, g_mem=dg_mem, g_cq=dg_cq, g_ck=dg_ck, g_ffn=dg_ffn)
    return sq, grad_x, small


BIG = ("w_in", "w_out", "w_cq", "w_ckv", "w_co", "w_gate_up", "w_down")
SMALL = ("g_mix", "b_f", "g_q", "g_k", "conv_b", "w_ra", "b_ra", "w_ri", "b_ri", "lam", "g_fox_out", "g_lru_out",
         "g_xattn", "g_mem", "g_cq", "g_ck", "g_ffn")
ORDER = ("g_mix", "w_in", "b_f", "g_q", "g_k", "conv_w", "conv_b", "w_ra", "b_ra", "w_ri", "b_ri", "lam", "g_fox_out",
         "g_lru_out", "w_out", "g_xattn", "g_mem", "w_cq", "w_ckv", "g_cq", "g_ck", "w_co", "g_ffn", "w_gate_up", "w_down")


def kernel(x, mem, g_mix, w_in, b_f, g_q, g_k, conv_w, conv_b, w_ra, b_ra, w_ri, b_ri, lam, g_fox_out, g_lru_out, w_out, g_xattn, g_mem, w_cq, w_ckv, g_cq, g_ck, w_co, g_ffn, w_gate_up, w_down, loss_target, m_g_mix, m_w_in, m_b_f, m_g_q, m_g_k, m_conv_w, m_conv_b, m_w_ra, m_b_ra, m_w_ri, m_b_ri, m_lam, m_g_fox_out, m_g_lru_out, m_w_out, m_g_xattn, m_g_mem, m_w_cq, m_w_ckv, m_g_cq, m_g_ck, m_w_co, m_g_ffn, m_w_gate_up, m_w_down, v_g_mix, v_w_in, v_b_f, v_g_q, v_g_k, v_conv_w, v_conv_b, v_w_ra, v_b_ra, v_w_ri, v_b_ri, v_lam, v_g_fox_out, v_g_lru_out, v_w_out, v_g_xattn, v_g_mem, v_w_cq, v_w_ckv, v_g_cq, v_g_ck, v_w_co, v_g_ffn, v_w_gate_up, v_w_down):
    given = dict(locals())
    w = {n: given[n] for n in ORDER}
    m = {n: given["m_" + n] for n in ORDER}
    v = {n: given["v_" + n] for n in ORDER}
    D = x.shape[2]
    fw = FOX_H * HEAD
    dev_index = 4 * lax.axis_index("x") + 2 * lax.axis_index("y") + lax.axis_index("c")
    dev = jnp.reshape(dev_index, (1,)).astype(jnp.int32)
    core = jnp.reshape(lax.axis_index("c"), (1,)).astype(jnp.int32)
    chip = jnp.reshape(2 * lax.axis_index("x") + lax.axis_index("y"), (1,)).astype(jnp.int32)

    def shard(d, n):
        return jnp.transpose(d[n], (2, 0, 1)) if n == "w_in" else d[n][0]

    def unshard(a, n):
        return jnp.transpose(a, (1, 2, 0)) if n == "w_in" else a[None]

    gather_groups = dict(attn=("w_out", "w_cq", "w_ckv", "w_co"), gate_up=("w_gate_up",), down=("w_down",))
    reduce_groups = dict(ffn=("w_gate_up", "w_down"), attn=("w_out", "w_cq", "w_ckv", "w_co"), w_in=("w_in",))
    column_blocked = ("w_co", "w_gate_up")
    flying = {}

    def land(a, tag):
        return _land_with_own(a, dev, "own_" + tag)

    def launch(group, after):
        shards = [shard(w, n).astype(BF16) for n in gather_groups[group]]
        lands = [land(s, n) for s, n in zip(shards, gather_groups[group])]
        flying[group] = _split_start(shards, lands, after, _gather_plan, 4, "gather_" + group + "_start")
        return flying[group][4]

    def arrived(group, after):
        send, recv, shards, lands, _ = flying.pop(group)
        _, lands = _split_wait(send, recv, shards, lands, after, _gather_plan, "gather_" + group + "_wait")
        flying[group] = _split_start([], lands, after, _forward_plan, 3, "gather_" + group + "_forward_start")
        if group == "attn":
            return launch("down", flying[group][4])
        return flying[group][4]

    def want(group, after):
        send, recv, _, lands, _ = flying.pop(group)
        _, full = _split_wait(send, recv, [], lands, after, _forward_plan, "gather_" + group + "_forward_wait")
        return [g if n in column_blocked else g.reshape(-1, g.shape[2]) for n, g in zip(gather_groups[group], full)]

    first = [shard(w, "w_in").astype(BF16).reshape(-1, D), conv_w[0]]
    g_in, g_conv = _all_gather(first, [land(first[0], "w_in"), land(first[1], "conv_w")], "gather_w_in")
    tok = launch("gate_up", launch("attn", g_conv))
    per = g_in.shape[1]
    f_k, f_lo = divmod(3 * fw, per)
    assert f_lo + FOX_H <= per
    wt_in = g_in.reshape(-1, D)
    W = dict(main_t=jnp.concatenate([wt_in[:3 * fw], wt_in[3 * fw + FOX_H:]], axis=0),
             f_t=jnp.pad(wt_in[3 * fw:3 * fw + FOX_H], ((0, HEAD - FOX_H), (0, 0))))
    sp = {n: w[n] for n in SMALL if n not in ("w_ra", "w_ri")}
    sp["w_ra"], sp["w_ri"] = w_ra[0], w_ri[0]
    sp["conv_w"] = jnp.transpose(g_conv, (1, 0, 2)).reshape(CONV_K, -1)
    sp["g_mix"] = sp["g_mix"] + tok[0, 0]

    pairing, reducing = {}, {}

    def done(group, grads):
        if group == "w_in":
            g_main_t, g_f_t = grads
            shards = [g_main_t[k * per:(k + 1) * per] for k in range(f_k)]
            shards.append(jnp.concatenate([g_main_t[f_k * per:3 * fw], g_f_t[:FOX_H],
                                           g_main_t[3 * fw:(f_k + 1) * per - FOX_H]], axis=0))
            shards += [g_main_t[k * per - FOX_H:(k + 1) * per - FOX_H] for k in range(f_k + 1, N_DEV)]
            grads = [jnp.stack(shards)]
        g8 = [g if g.ndim == 3 else g.reshape(N_DEV, -1, g.shape[1]) for g in grads]
        lands = [lax.empty((N_CHIP,) + g.shape[1:], g.dtype) for g in g8]
        pairing[group] = _split_start(g8, lands, chip, _pair_plan, N_CHIP, "reduce_" + group + "_pair_start")
        return pairing[group][4]

    def paired(group, after):
        send, recv, g8, lands, _ = pairing.pop(group)
        g8, from_sibling = _split_wait(send, recv, g8, lands, after, _pair_plan, "reduce_" + group + "_pair_wait")
        p4 = [_pair_sum(g, r, core, "reduce_pair_sum_" + n) for g, r, n in zip(g8, from_sibling, reduce_groups[group])]
        lands = [lax.empty(p.shape, p.dtype) for p in p4]
        reducing[group] = _split_start(p4, lands, chip, _chip_plan, 3, "reduce_" + group + "_start")
        return reducing[group][4]

    def finish(group, after):
        send, recv, p4, lands, _ = reducing.pop(group)
        p4, lands = _split_wait(send, recv, p4, lands, after, _chip_plan, "reduce_" + group + "_wait")
        return {n: tuple(unshard(r, n) for r in _reduce_adamw(p, l, chip, shard(w, n), shard(m, n), shard(v, n), "adamw_" + n))
                for n, p, l in zip(reduce_groups[group], p4, lands)}

    sq, grad_x, gs = _local_step(x[0], mem[0], loss_target[0], sp, W, arrived, want, done, paired)

    vectors = tuple(n for n in SMALL if n not in ("w_ra", "w_ri"))
    mine = [_pack_rows([[gs[n] for n in vectors] + [sq[0:1]]], "pack_small_grads")[0], gs["w_ra"].reshape(-1, HEAD),
            gs["w_ri"].reshape(-1, HEAD), gs["conv_w"]]
    lands = [land(a, "small_grads_%d" % k) for k, a in enumerate(mine)]
    s_send, s_recv, s_src, s_land, tok = _split_start(mine, lands, grad_x, _gather_plan, 4, "gather_small_start")
    out = finish("ffn", tok)
    out.update(finish("attn", tok))
    updated = [r for n in reduce_groups["ffn"] + reduce_groups["attn"] for r in out[n]]
    _, s_land = _split_wait(s_send, s_recv, s_src, s_land, updated, _gather_plan, "gather_small_wait")
    all_vec, all_ra, all_ri, all_conv = _gather_forward(s_land, "gather_small_forward")

    nothing = jnp.zeros((1, HEAD), F32)
    state = _pack_rows([[d[n] for n in vectors] + [nothing] for d in (w, m, v)], "pack_small_state")
    upd = _sum_adamw(all_vec, *state, "adamw_small_vectors")
    per_vector = _unpack_rows(upd, [w[n] for n in vectors] + [nothing], "unpack_small")
    for r, n in enumerate(vectors):
        out[n] = tuple(per_vector[k][r] for k in range(4))
    loss = 0.5 * per_vector[0][len(vectors)][0, 0] / D
    for n, parts in (("w_ra", all_ra), ("w_ri", all_ri)):
        res = _sum_adamw(parts, *[d[n].reshape(-1, HEAD) for d in (w, m, v)], "adamw_" + n)
        out[n] = tuple(r.reshape(w[n].shape) for r in res)
    out["conv_w"] = tuple(r[None] for r in _sum_adamw(all_conv, conv_w[0], m_conv_w[0], v_conv_w[0], "adamw_conv_w", col=dev))
    out.update(finish("w_in", upd[1]))

    return (loss, grad_x[None], *[out[n][0] for n in ORDER], *[out[n][1] for n in ORDER],
            *[out[n][2] for n in ORDER], *[out[n][3] for n in ORDER])
```

```python
import functools
import math

import jax
import jax.numpy as jnp
from jax import lax
from jax.experimental import pallas as pl
from jax.experimental.pallas import tpu as pltpu

F32 = jnp.float32
BF16 = jnp.bfloat16
MESH = pl.DeviceIdType.MESH
N_DEV = 8
N_CHIP = 4

HEAD = 128
FOX_H = 8
XATT_H = 4
LRU_NB = 8
CONV_K = 4
LRU_C = 8.0
RMS_EPS = 1e-6
ATT_T = 1024
CUM_T = 256
ROW_T = 256
NEG = -1e30
V7X_VMEM_LIMIT = 48 * 1024 * 1024
V7X_VMEM_LIMIT_HIGH = 60 * 1024 * 1024

ADAM_LR = 0.001
ADAM_B1 = 0.9
ADAM_B2 = 0.999
ADAM_EPS = 1e-08
ADAM_WD = 0.01
ADAM_STEP = 10

NT = (((1,), (1,)), ((), ()))
TN = (((0,), (0,)), ((), ()))
NN = (((1,), (0,)), ((), ()))

ANY = pl.BlockSpec(memory_space=pl.ANY)


def _behind(after):
    return ([], []) if after is None else ([ANY], [after])


def _params(*sem):
    return pltpu.CompilerParams(dimension_semantics=sem or None, vmem_limit_bytes=V7X_VMEM_LIMIT)


def _dot(a, b, dn=NN):
    return lax.dot_general(a, b, dn, preferred_element_type=F32)


def _rms(x, g):
    r = lax.rsqrt(jnp.mean(x * x, axis=-1, keepdims=True) + RMS_EPS)
    return x * r * g


def _rms_grad(x, g, dy):
    r = lax.rsqrt(jnp.mean(x * x, axis=-1, keepdims=True) + RMS_EPS)
    xh = x * r
    dxh = dy * g
    dx = r * (dxh - xh * jnp.mean(dxh * xh, axis=-1, keepdims=True))
    return dx, jnp.sum(dy * xh, axis=0, keepdims=True)


def _gelu(x):
    k = math.sqrt(2.0 / math.pi)
    return 0.5 * x * (1.0 + jnp.tanh(k * (x + 0.044715 * (x * x * x))))


def _gelu_grad(x):
    k = math.sqrt(2.0 / math.pi)
    t = jnp.tanh(k * (x + 0.044715 * (x * x * x)))
    return 0.5 * (1.0 + t) + 0.5 * x * (1.0 - t * t) * k * (1.0 + 3.0 * 0.044715 * x * x)


def _one_minus_exp(z):
    series = -(z + 0.5 * z * z + (1.0 / 6.0) * z * z * z)
    return jnp.where(z > -1e-3, series, 1.0 - jnp.exp(z))


def _row_tile(rows, cap=ROW_T, mult=16):
    t = min(rows, cap)
    while rows % t or (t % mult and t != rows):
        t -= 1
    return t


def _mat_dims(z):
    return (z.shape[-2], z.shape[-1] * (z.shape[0] if z.ndim == 3 else 1))


def _mat_spec(arr, rblk, cblk, rsel, csel):
    if arr.ndim == 2:
        return pl.BlockSpec((rblk, cblk), lambda i, j, k: ((i, j, k)[rsel], (i, j, k)[csel]))
    nw = arr.shape[2]
    assert nw % cblk == 0, (arr.shape, cblk)
    per = nw // cblk
    return pl.BlockSpec((None, rblk, cblk),
                        lambda i, j, k: ((i, j, k)[csel] // per, (i, j, k)[rsel], (i, j, k)[csel] % per))


def _mm(a, b, mode, out_dtype, name, add=None, out_blocks=None, tm=1024, tn=1024, tk=None, after=None):
    ar, ac = _mat_dims(a)
    br, bc = _mat_dims(b)
    if mode == "nn":
        M, K, N = ar, ac, bc
        assert br == K
    elif mode == "nt":
        M, K, N = ar, ac, br
        assert bc == K
    else:
        M, K, N = ac, ar, bc
        assert br == K
    tm, tn = min(tm, M), min(tn, N)
    tk = K if tk is None or mode == "tn" else min(tk, K)
    assert M % tm == 0 and N % tn == 0 and K % tk == 0, (name, M, N, K, tm, tn, tk)
    nk = K // tk
    nj = N // tn
    if mode == "nn":
        specs = [_mat_spec(a, tm, tk, 0, 2), _mat_spec(b, tk, tn, 2, 1)]
        dn = NN
    elif mode == "nt":
        specs = [_mat_spec(a, tm, tk, 0, 2), _mat_spec(b, tn, tk, 1, 2)]
        dn = NT
    else:
        specs = [_mat_spec(a, tk, tm, 2, 0), _mat_spec(b, tk, tn, 2, 1)]
        dn = TN
    args = [a, b]
    if add is not None:
        specs.append(pl.BlockSpec((tm, tn), lambda i, j, k: (i, j)))
        args.append(add)
    specs += _behind(after)[0]
    args += _behind(after)[1]
    n_in = len(args)
    if out_blocks is None:
        out_shape = jax.ShapeDtypeStruct((M, N), out_dtype)
        out_spec = pl.BlockSpec((tm, tn), lambda i, j, k: (i, j))
    else:
        nb = out_blocks
        nw = N // nb
        assert nw % tn == 0
        per = nw // tn
        out_shape = jax.ShapeDtypeStruct((nb, M, nw), out_dtype)
        out_spec = pl.BlockSpec((None, tm, tn), lambda i, j, k: (j // per, i, j % per))
    keep_t = mode == "tn" and nj > 1
    scratch = []
    if nk > 1:
        scratch.append(pltpu.VMEM((tm, tn), F32))
    if keep_t:
        scratch.append(pltpu.VMEM((tm, tk), a.dtype))

    def body(*refs):
        a_ref, b_ref = refs[0], refs[1]
        add_ref = refs[2] if add is not None else None
        o_ref = refs[n_in]

        def finish(r):
            if add_ref is not None:
                r = r + add_ref[...]
            o_ref[...] = r.astype(out_dtype)

        if keep_t:
            at_ref = refs[-1]

            @pl.when(pl.program_id(1) == 0)
            def _():
                at_ref[...] = a_ref[...].T

            finish(_dot(at_ref[...], b_ref[...], NN))
        elif nk == 1:
            finish(_dot(a_ref[...], b_ref[...], dn))
        else:
            acc_ref = refs[n_in + 1]
            k = pl.program_id(2)

            @pl.when(k == 0)
            def _():
                acc_ref[...] = jnp.zeros_like(acc_ref)

            acc_ref[...] += _dot(a_ref[...], b_ref[...], dn)

            @pl.when(k == nk - 1)
            def _():
                finish(acc_ref[...])

    return pl.pallas_call(
        body, name=name, grid=(M // tm, nj, nk), in_specs=specs, out_specs=out_spec, out_shape=out_shape,
        scratch_shapes=scratch,
        compiler_params=_params("parallel", "arbitrary" if keep_t else "parallel", "arbitrary"),
    )(*args)


def _rms_fwd(xs, gs, name, after=None):
    n = len(xs)
    rows = xs[0].shape[0]
    widths = [x.shape[1] for x in xs]
    tr = _row_tile(rows)
    more_specs, more = _behind(after)

    def body(*refs):
        o_ref = refs[2 * n + len(more)]
        off = 0
        for t in range(n):
            o_ref[:, off:off + widths[t]] = _rms(refs[t][...], refs[n + t][...]).astype(BF16)
            off += widths[t]

    return pl.pallas_call(
        body, name=name, grid=(rows // tr,),
        in_specs=[pl.BlockSpec((tr, w), lambda i: (i, 0)) for w in widths]
        + [pl.BlockSpec((1, w), lambda i: (0, 0)) for w in widths] + more_specs,
        out_specs=pl.BlockSpec((tr, sum(widths)), lambda i: (i, 0)),
        out_shape=jax.ShapeDtypeStruct((rows, sum(widths)), BF16),
        compiler_params=_params("parallel"),
    )(*xs, *gs, *more)


def _rms_bwd(x, g, dy, name, dy_col=0, res=None, want_dx=True, want_bf16=True, after=None):
    rows, w = x.shape
    tr = _row_tile(rows)
    has_res = res is not None
    more_specs, more = _behind(after)

    def body(*refs):
        x_ref, g_ref, dy_ref = refs[:3]
        res_ref = refs[3] if has_res else None
        outs = refs[3 + has_res + len(more):]
        dg_ref = outs[-1]
        dx, dg = _rms_grad(x_ref[...], g_ref[...], dy_ref[...])

        @pl.when(pl.program_id(0) == 0)
        def _():
            dg_ref[...] = jnp.zeros_like(dg_ref)

        dg_ref[...] += dg
        if want_dx:
            if has_res:
                dx = dx + res_ref[...]
            outs[0][...] = dx
            if want_bf16:
                outs[1][...] = dx.astype(BF16)

    row_spec = pl.BlockSpec((tr, w), lambda i: (i, 0))
    in_specs = [row_spec, pl.BlockSpec((1, w), lambda i: (0, 0)), pl.BlockSpec((tr, w), lambda i: (i, dy_col))]
    args = [x, g, dy]
    if has_res:
        in_specs.append(row_spec)
        args.append(res)
    in_specs += more_specs
    args += more
    out_specs, out_shape = [], []
    if want_dx:
        out_specs += [row_spec] * (1 + want_bf16)
        out_shape += [jax.ShapeDtypeStruct((rows, w), F32)] + [jax.ShapeDtypeStruct((rows, w), BF16)] * want_bf16
    out_specs.append(pl.BlockSpec((1, w), lambda i: (0, 0)))
    out_shape.append(jax.ShapeDtypeStruct((1, w), F32))
    return pl.pallas_call(
        body, name=name, grid=(rows // tr,), in_specs=in_specs, out_specs=out_specs, out_shape=out_shape,
        compiler_params=_params("arbitrary"),
    )(*args)


def _loss_head(y, target, name):
    rows, w = y.shape
    tr = _row_tile(rows)

    def body(y_ref, t_ref, dy_ref, dyb_ref, acc_ref):
        e = y_ref[...] - t_ref[...]

        @pl.when(pl.program_id(0) == 0)
        def _():
            acc_ref[...] = jnp.zeros_like(acc_ref)

        acc_ref[...] += jnp.sum(e * e)
        dy = e * (1.0 / w)
        dy_ref[...] = dy
        dyb_ref[...] = dy.astype(BF16)

    row_spec = pl.BlockSpec((tr, w), lambda i: (i, 0))
    return pl.pallas_call(
        body, name=name, grid=(rows // tr,), in_specs=[row_spec, row_spec],
        out_specs=[row_spec, row_spec, pl.BlockSpec((8, 128), lambda i: (0, 0))],
        out_shape=[jax.ShapeDtypeStruct((rows, w), F32), jax.ShapeDtypeStruct((rows, w), BF16),
                   jax.ShapeDtypeStruct((8, 128), F32)],
        compiler_params=_params("arbitrary"),
    )(y, target)


def _qk_fwd(proj, g_q, g_k, name, after=None):
    rows = proj.shape[0]
    w = FOX_H * HEAD
    tr = _row_tile(rows)
    more_specs, more = _behind(after)

    def body(q_ref, k_ref, v_ref, gq_ref, gk_ref, *rest):
        qn_ref, kn_ref, vb_ref = rest[len(more):]
        for h in range(FOX_H):
            sl = slice(h * HEAD, (h + 1) * HEAD)
            qn_ref[:, sl] = _rms(q_ref[:, sl], gq_ref[...]).astype(BF16)
            kn_ref[:, sl] = _rms(k_ref[:, sl], gk_ref[...]).astype(BF16)
        vb_ref[...] = v_ref[...].astype(BF16)

    gain = pl.BlockSpec((1, HEAD), lambda i: (0, 0))
    out = pl.BlockSpec((tr, w), lambda i: (i, 0))
    return pl.pallas_call(
        body, name=name, grid=(rows // tr,),
        in_specs=[pl.BlockSpec((tr, w), lambda i: (i, 0)), pl.BlockSpec((tr, w), lambda i: (i, 1)),
                  pl.BlockSpec((tr, w), lambda i: (i, 2)), gain, gain] + more_specs,
        out_specs=[out, out, out], out_shape=[jax.ShapeDtypeStruct((rows, w), BF16)] * 3,
        compiler_params=_params("parallel"),
    )(proj, proj, proj, g_q, g_k, *more)


def _qk_bwd(proj, g_q, g_k, dqn, dkn, name):
    rows = proj.shape[0]
    w = FOX_H * HEAD
    tr = _row_tile(rows)

    def body(q_ref, k_ref, gq_ref, gk_ref, dqn_ref, dkn_ref, dq_ref, dk_ref, dgq_ref, dgk_ref):
        @pl.when(pl.program_id(0) == 0)
        def _():
            dgq_ref[...] = jnp.zeros_like(dgq_ref)
            dgk_ref[...] = jnp.zeros_like(dgk_ref)

        for h in range(FOX_H):
            sl = slice(h * HEAD, (h + 1) * HEAD)
            dq, dgq = _rms_grad(q_ref[:, sl], gq_ref[...], dqn_ref[:, sl])
            dk, dgk = _rms_grad(k_ref[:, sl], gk_ref[...], dkn_ref[:, sl])
            dq_ref[:, sl] = dq.astype(BF16)
            dk_ref[:, sl] = dk.astype(BF16)
            dgq_ref[...] += dgq
            dgk_ref[...] += dgk

    gain = pl.BlockSpec((1, HEAD), lambda i: (0, 0))
    row = pl.BlockSpec((tr, w), lambda i: (i, 0))
    return pl.pallas_call(
        body, name=name, grid=(rows // tr,),
        in_specs=[row, pl.BlockSpec((tr, w), lambda i: (i, 1)), gain, gain, row, row],
        out_specs=[row, row, gain, gain],
        out_shape=[jax.ShapeDtypeStruct((rows, w), BF16)] * 2 + [jax.ShapeDtypeStruct((1, HEAD), F32)] * 2,
        compiler_params=_params("arbitrary"),
    )(proj, proj, g_q, g_k, dqn, dkn)


def _fgate_fwd(f_logit, b_pad, name):
    S = f_logit.shape[0]
    T = min(CUM_T, S)
    nb = S // T
    RT = min(ATT_T, S)
    nbr = S // RT

    def body(f_ref, b_ref, col_ref, row_ref, c_scr):
        tri = (lax.broadcasted_iota(jnp.int32, (T, T), 0) >= lax.broadcasted_iota(jnp.int32, (T, T), 1)).astype(F32)
        carry = jnp.zeros((1, HEAD), F32)
        for blk in range(nb):
            z = f_ref[blk * T:(blk + 1) * T, :] + b_ref[...]
            lf = jnp.minimum(z, 0.0) - jnp.log1p(jnp.exp(-jnp.abs(z)))
            cb = jnp.dot(tri, lf, precision=lax.Precision.HIGHEST, preferred_element_type=F32) + carry
            c_scr[blk * T:(blk + 1) * T, :] = cb
            carry = cb[T - 1:T, :]
        c = c_scr[...]
        lane = lax.broadcasted_iota(jnp.int32, c.shape, 1)
        ct = c.T
        for h in range(FOX_H):
            col_ref[h] = jnp.sum(jnp.where(lane == h, c, 0.0), axis=1, keepdims=True)
            for jb in range(nbr):
                row_ref[h, jb] = ct[h:h + 1, jb * RT:(jb + 1) * RT]

    return pl.pallas_call(
        body, name=name,
        out_shape=[jax.ShapeDtypeStruct((FOX_H, S, 1), F32), jax.ShapeDtypeStruct((FOX_H, nbr, 1, RT), F32)],
        scratch_shapes=[pltpu.VMEM((S, HEAD), F32)], compiler_params=_params(),
    )(f_logit, b_pad)


def _fgate_bwd(f_logit, b_pad, dc_col, dc_row, name):
    S = f_logit.shape[0]
    T = min(CUM_T, S)
    nb = S // T
    RT = min(ATT_T, S)

    def body(f_ref, b_ref, dcol_ref, drow_ref, df_ref, db_ref, dc_scr, rt_scr):
        lane = lax.broadcasted_iota(jnp.int32, (S, HEAD), 1)
        sub = lax.broadcasted_iota(jnp.int32, (HEAD, RT), 0)
        dc = jnp.zeros((S, HEAD), F32)
        for h in range(FOX_H):
            dc = jnp.where(lane == h, dcol_ref[h], dc)
        for jb in range(S // RT):
            rt = jnp.zeros((HEAD, RT), F32)
            for h in range(FOX_H):
                rt = jnp.where(sub == h, drow_ref[h, jb], rt)
            rt_scr[jb * RT:(jb + 1) * RT, :] = rt.T
        dc_scr[...] = dc - rt_scr[...]
        tri = (lax.broadcasted_iota(jnp.int32, (T, T), 0) <= lax.broadcasted_iota(jnp.int32, (T, T), 1)).astype(F32)
        carry = jnp.zeros((1, HEAD), F32)
        db = jnp.zeros((1, HEAD), F32)
        for blk in reversed(range(nb)):
            rows = slice(blk * T, (blk + 1) * T)
            dlf = jnp.dot(tri, dc_scr[rows, :], precision=lax.Precision.HIGHEST, preferred_element_type=F32) + carry
            carry = dlf[0:1, :]
            z = f_ref[rows, :] + b_ref[...]
            df = dlf * jax.nn.sigmoid(-z)
            df_ref[rows, :] = df.astype(BF16)
            db = db + jnp.sum(df, axis=0, keepdims=True)
        db_ref[...] = db

    return pl.pallas_call(
        body, name=name,
        out_shape=[jax.ShapeDtypeStruct((S, HEAD), BF16), jax.ShapeDtypeStruct((1, HEAD), F32)],
        scratch_shapes=[pltpu.VMEM((S, HEAD), F32), pltpu.VMEM((S, HEAD), F32)], compiler_params=_params(),
    )(f_logit, b_pad, dc_col, dc_row)


def _fox_fwd(qn, kn, vb, c_col, c_row, name, after=None):
    S = qn.shape[0]
    T = min(ATT_T, S)
    nb = S // T
    scale = 1.0 / math.sqrt(HEAD)
    more_specs, more = _behind(after)

    def body(q_ref, k_ref, v_ref, cc_ref, cr_ref, *rest):
        o_ref, lse_ref = rest[len(more):]
        i = pl.program_id(1)
        q = q_ref[...]
        cc = cc_ref[...]

        def step(j, carry, diagonal=False):
            m, l, acc = carry
            sl = pl.ds(pl.multiple_of(j * T, T), T)
            s = _dot(q, k_ref[sl, :], NT) * scale + cc - cr_ref[j]
            if diagonal:
                s = jnp.where(lax.broadcasted_iota(jnp.int32, (T, T), 0) >= lax.broadcasted_iota(jnp.int32, (T, T), 1), s, NEG)
            m2 = jnp.maximum(m, jnp.max(s, axis=1, keepdims=True))
            p = jnp.exp(s - m2)
            al = jnp.exp(m - m2)
            return m2, al * l + jnp.sum(p, axis=1, keepdims=True), al * acc + _dot(p.astype(BF16), v_ref[sl, :])

        init = (jnp.full((T, 1), NEG, F32), jnp.zeros((T, 1), F32), jnp.zeros((T, HEAD), F32))
        m, l, acc = step(i, lax.fori_loop(0, i, step, init), diagonal=True)
        o_ref[...] = acc / l
        lse_ref[...] = m + jnp.log(l)

    head = pl.BlockSpec((S, HEAD), lambda h, i: (0, h))
    return pl.pallas_call(
        body, name=name, grid=(FOX_H, nb),
        in_specs=[pl.BlockSpec((T, HEAD), lambda h, i: (i, h)), head, head,
                  pl.BlockSpec((None, T, 1), lambda h, i: (h, i, 0)),
                  pl.BlockSpec((None, nb, 1, T), lambda h, i: (h, 0, 0, 0))] + more_specs,
        out_specs=[pl.BlockSpec((T, HEAD), lambda h, i: (i, h)), pl.BlockSpec((None, T, 1), lambda h, i: (h, i, 0))],
        out_shape=[jax.ShapeDtypeStruct((S, FOX_H * HEAD), F32), jax.ShapeDtypeStruct((FOX_H, S, 1), F32)],
        compiler_params=_params("parallel", "parallel"),
    )(qn, kn, vb, c_col, c_row, *more)


def _fox_bwd(qn, kn, vb, c_col, c_row, lse, o, do, name):
    S = qn.shape[0]
    T = min(ATT_T, S)
    Th = T // 2
    nb = S // T
    scale = 1.0 / math.sqrt(HEAD)

    def body(q_ref, k_ref, v_ref, cc_ref, cr_ref, lse_ref, o_ref, do_ref, dq_ref, dk_ref, dv_ref, dcc_ref, dcr_ref):
        j = pl.program_id(1)

        @pl.when(j == 0)
        def _():
            dq_ref[...] = jnp.zeros_like(dq_ref)
            dcc_ref[...] = jnp.zeros_like(dcc_ref)

        k = k_ref[...]
        v = v_ref[...]
        cr = cr_ref[...]

        def tile(sl, kk, vv, crr, carry, masked):
            dk, dv, dcr = carry
            q = q_ref[sl, :]
            d_o = do_ref[sl, :]
            s = _dot(q, kk, NT) * scale + cc_ref[sl, :] - crr
            if masked:
                s = jnp.where(lax.broadcasted_iota(jnp.int32, s.shape, 0) >= lax.broadcasted_iota(jnp.int32, s.shape, 1), s, NEG)
            p = jnp.exp(s - lse_ref[sl, :])
            dob = d_o.astype(BF16)
            dp = _dot(dob, vv, NT)
            delta = jnp.sum(d_o * o_ref[sl, :], axis=1, keepdims=True)
            ds = p * (dp - delta)
            dsb = ds.astype(BF16)
            dq_ref[sl, :] += _dot(dsb, kk) * scale
            dcc_ref[sl, :] += jnp.sum(ds, axis=1, keepdims=True)
            return (dk + _dot(dsb, q, TN), dv + _dot(p.astype(BF16), dob, TN), dcr + jnp.sum(ds, axis=0, keepdims=True))

        def step(i, carry):
            return tile(pl.ds(pl.multiple_of(i * T, T), T), k, v, cr, carry, False)

        def rows(r0):
            return pl.ds(pl.multiple_of(j * T + r0, Th), Th)

        zero = (jnp.zeros((Th, HEAD), F32), jnp.zeros((Th, HEAD), F32), jnp.zeros((1, Th), F32))
        left = tile(rows(Th), k[:Th], v[:Th], cr[:, :Th], tile(rows(0), k[:Th], v[:Th], cr[:, :Th], zero, True), False)
        right = tile(rows(Th), k[Th:], v[Th:], cr[:, Th:], zero, True)
        diagonal = (jnp.concatenate([left[0], right[0]], axis=0), jnp.concatenate([left[1], right[1]], axis=0),
                    jnp.concatenate([left[2], right[2]], axis=1))
        dk, dv, dcr = lax.fori_loop(j + 1, nb, step, diagonal)
        dk_ref[...] = dk * scale
        dv_ref[...] = dv.astype(BF16)
        dcr_ref[...] = dcr

    head = pl.BlockSpec((S, HEAD), lambda h, j: (0, h))
    tile = pl.BlockSpec((T, HEAD), lambda h, j: (j, h))
    col = pl.BlockSpec((None, S, 1), lambda h, j: (h, 0, 0))
    row = pl.BlockSpec((None, None, 1, T), lambda h, j: (h, j, 0, 0))
    w = FOX_H * HEAD
    return pl.pallas_call(
        body, name=name, grid=(FOX_H, nb),
        in_specs=[head, tile, tile, col, row, col, head, head],
        out_specs=[head, tile, tile, col, row],
        out_shape=[jax.ShapeDtypeStruct((S, w), F32), jax.ShapeDtypeStruct((S, w), F32), jax.ShapeDtypeStruct((S, w), BF16),
                   jax.ShapeDtypeStruct((FOX_H, S, 1), F32), jax.ShapeDtypeStruct((FOX_H, nb, 1, T), F32)],
        compiler_params=_params("arbitrary", "arbitrary"),
    )(qn, kn, vb, c_col, c_row, lse, o, do)


LRU_STEP_BLOCKS = 2


def _block_dot(x, w, dn=NN):
    return jnp.concatenate([_dot(x[:, b * HEAD:(b + 1) * HEAD], w[b], dn) for b in range(w.shape[0])], axis=1)


def _lru_gates(uc, wra, bra, wri, bri, lam):
    ucb = uc.astype(BF16)
    r = jax.nn.sigmoid(_block_dot(ucb, wra.astype(BF16)) + bra)
    ig = jax.nn.sigmoid(_block_dot(ucb, wri.astype(BF16)) + bri)
    sp = jnp.maximum(-lam, 0.0) + jnp.log1p(jnp.exp(-jnp.abs(lam)))
    log_a = -LRU_C * r * sp
    a = jnp.exp(log_a)
    mult = jnp.sqrt(_one_minus_exp(2.0 * log_a))
    return r, ig, sp, a, mult


def _conv(pad_ref, cw, cb, S):
    uc = cb
    for j in range(CONV_K):
        uc = uc + cw[j:j + 1, :] * pad_ref[5 + j:5 + j + S, :]
    return uc


def _lru_specs(S, G):
    bw = G * HEAD
    u_col = 3 * FOX_H // G
    g_col = u_col + LRU_NB // G
    blk = pl.BlockSpec((S, bw), lambda n: (0, n))
    vec = pl.BlockSpec((1, bw), lambda n: (0, n))
    mat = pl.BlockSpec((G, HEAD, HEAD), lambda n: (n, 0, 0))
    return dict(
        u=pl.BlockSpec((S, bw), lambda n: (0, u_col + n)), gate=pl.BlockSpec((S, bw), lambda n: (0, g_col + n)),
        blk=blk, vec=vec, mat=mat, cw=pl.BlockSpec((CONV_K, bw), lambda n: (0, n)))


def _lru_fwd(proj, conv_w, conv_b, w_ra, b_ra, w_ri, b_ri, lam, name, after=None):
    S = proj.shape[0]
    G = LRU_STEP_BLOCKS
    bw = G * HEAD
    sp_ = _lru_specs(S, G)
    rows8 = S // 8
    more_specs, more = _behind(after)

    def body(u_ref, gt_ref, cw_ref, cb_ref, wra_ref, bra_ref, wri_ref, bri_ref, lam_ref, *rest):
        y_ref, h_ref, pad, a_scr, b_scr = rest[len(more):]
        pad[0:8, :] = jnp.zeros((8, bw), F32)
        pad[8:S + 8, :] = u_ref[...]
        uc = _conv(pad, cw_ref[...], cb_ref[...], S)
        r, ig, sp, a, mult = _lru_gates(uc, wra_ref[...], bra_ref[...], wri_ref[...], bri_ref[...], lam_ref[...])
        a_scr[...] = a
        b_scr[...] = mult * (ig * uc)
        sub = lax.broadcasted_iota(jnp.int32, (8, bw), 0)

        def step(t, carry):
            sl = pl.ds(pl.multiple_of(t * 8, 8), 8)
            A, B = a_scr[sl, :], b_scr[sl, :]
            for d in (1, 2, 4):
                m = sub >= d
                B = jnp.where(m, A * pltpu.roll(B, d, 0) + B, B)
                A = jnp.where(m, A * pltpu.roll(A, d, 0), A)
            h = A * carry + B
            h_ref[sl, :] = h
            return h[7:8, :]

        lax.fori_loop(0, rows8, step, jnp.zeros((1, bw), F32))
        y_ref[...] = h_ref[...] * _gelu(gt_ref[...])

    w = LRU_NB * HEAD
    return pl.pallas_call(
        body, name=name, grid=(LRU_NB // G,),
        in_specs=[sp_["u"], sp_["gate"], sp_["cw"], sp_["vec"], sp_["mat"], sp_["vec"], sp_["mat"], sp_["vec"], sp_["vec"]]
        + more_specs,
        out_specs=[sp_["blk"], sp_["blk"]],
        out_shape=[jax.ShapeDtypeStruct((S, w), F32)] * 2,
        scratch_shapes=[pltpu.VMEM((S + 8, bw), F32), pltpu.VMEM((S, bw), F32), pltpu.VMEM((S, bw), F32)],
        compiler_params=_params("parallel"),
    )(proj, proj, conv_w, conv_b, w_ra, b_ra, w_ri, b_ri, lam, *more)


def _lru_bwd(proj, h, dy, conv_w, conv_b, w_ra, b_ra, w_ri, b_ri, lam, name, after=None):
    S = proj.shape[0]
    G = LRU_STEP_BLOCKS
    bw = G * HEAD
    sp_ = _lru_specs(S, G)
    rows8 = S // 8
    more_specs, more = _behind(after)

    def body(u_ref, gt_ref, h_ref, dy_ref, cw_ref, cb_ref, wra_ref, bra_ref, wri_ref, bri_ref, lam_ref, *rest):
        (du_ref, dgt_ref, dcw_ref, dcb_ref, dwra_ref, dbra_ref, dwri_ref, dbri_ref, dlam_ref,
         pad, an_scr, d_scr, g_scr, hp_scr) = rest[len(more):]
        zero8 = jnp.zeros((8, bw), F32)
        pad[0:8, :] = zero8
        pad[8:S + 8, :] = u_ref[...]
        cw = cw_ref[...]
        uc = _conv(pad, cw, cb_ref[...], S)
        wra, wri, lam_v = wra_ref[...], wri_ref[...], lam_ref[...]
        r, ig, sp, a, mult = _lru_gates(uc, wra, bra_ref[...], wri, bri_ref[...], lam_v)
        gate = gt_ref[...]
        dy_v = dy_ref[...]
        hv = h_ref[...]
        dgt_ref[...] = (dy_v * hv * _gelu_grad(gate)).astype(BF16)
        d_scr[...] = dy_v * _gelu(gate)
        g_scr[0:S, :] = a
        g_scr[S:S + 8, :] = zero8
        an_scr[...] = g_scr[1:S + 1, :]
        sub = lax.broadcasted_iota(jnp.int32, (8, bw), 0)

        def step(t, carry):
            sl = pl.ds(pl.multiple_of((rows8 - 1 - t) * 8, 8), 8)
            A, D = an_scr[sl, :], d_scr[sl, :]
            for d in (1, 2, 4):
                m = sub + d <= 7
                D = jnp.where(m, A * pltpu.roll(D, 8 - d, 0) + D, D)
                A = jnp.where(m, A * pltpu.roll(A, 8 - d, 0), A)
            g = A * carry + D
            g_scr[sl, :] = g
            return g[0:1, :]

        lax.fori_loop(0, rows8, step, jnp.zeros((1, bw), F32))
        g = g_scr[0:S, :]
        hp_scr[0:8, :] = zero8
        hp_scr[8:S + 8, :] = hv
        da = g * hp_scr[7:S + 7, :]
        iu = ig * uc
        dmult = g * iu
        diu = g * mult
        dig = diu * uc
        duc = diu * ig
        dlog_a = da * a - dmult * (a * a) / mult
        dr = dlog_a * (-LRU_C * sp)
        dsp = jnp.sum(dlog_a * (-LRU_C * r), axis=0, keepdims=True)
        dlam_ref[...] = -dsp * jax.nn.sigmoid(-lam_v)
        dpr = dr * r * (1.0 - r)
        dpi = dig * ig * (1.0 - ig)
        dbra_ref[...] = jnp.sum(dpr, axis=0, keepdims=True)
        dbri_ref[...] = jnp.sum(dpi, axis=0, keepdims=True)
        ucb = uc.astype(BF16)
        dprb, dpib = dpr.astype(BF16), dpi.astype(BF16)
        for b in range(G):
            cols = slice(b * HEAD, (b + 1) * HEAD)
            dwra_ref[b] = _dot(ucb[:, cols], dprb[:, cols], TN).astype(BF16)
            dwri_ref[b] = _dot(ucb[:, cols], dpib[:, cols], TN).astype(BF16)
        duc = duc + _block_dot(dprb, wra.astype(BF16), NT) + _block_dot(dpib, wri.astype(BF16), NT)
        dcb_ref[...] = jnp.sum(duc, axis=0, keepdims=True)
        for j in range(CONV_K):
            dcw_ref[j:j + 1, :] = jnp.sum(duc * pad[5 + j:5 + j + S, :], axis=0, keepdims=True)
        g_scr[0:S, :] = duc
        g_scr[S:S + 8, :] = zero8
        du = jnp.zeros((S, bw), F32)
        for j in range(CONV_K):
            du = du + cw[j:j + 1, :] * g_scr[3 - j:3 - j + S, :]
        du_ref[...] = du.astype(BF16)

    w = LRU_NB * HEAD
    bf = jax.ShapeDtypeStruct((S, w), BF16)
    vec = jax.ShapeDtypeStruct((1, w), F32)
    mat = jax.ShapeDtypeStruct((LRU_NB, HEAD, HEAD), BF16)
    return pl.pallas_call(
        body, name=name, grid=(LRU_NB // G,),
        in_specs=[sp_["u"], sp_["gate"], sp_["blk"], sp_["blk"], sp_["cw"], sp_["vec"], sp_["mat"], sp_["vec"],
                  sp_["mat"], sp_["vec"], sp_["vec"]] + more_specs,
        out_specs=[sp_["blk"], sp_["blk"], sp_["cw"], sp_["vec"], sp_["mat"], sp_["vec"], sp_["mat"], sp_["vec"], sp_["vec"]],
        out_shape=[bf, bf, jax.ShapeDtypeStruct((CONV_K, w), F32), vec, mat, vec, mat, vec, vec],
        scratch_shapes=[pltpu.VMEM((S + 8, bw), F32), pltpu.VMEM((S, bw), F32), pltpu.VMEM((S, bw), F32),
                        pltpu.VMEM((S + 8, bw), F32), pltpu.VMEM((S + 8, bw), F32)],
        compiler_params=pltpu.CompilerParams(dimension_semantics=("parallel",), vmem_limit_bytes=V7X_VMEM_LIMIT_HIGH),
    )(proj, proj, h, dy, conv_w, conv_b, w_ra, b_ra, w_ri, b_ri, lam, *more)


def _xattn_fwd(cq, ckv, g_cq, g_ck, name):
    S, w = cq.shape
    M = ckv.shape[0]
    tr = _row_tile(S)
    scale = 1.0 / math.sqrt(HEAD)

    def body(cq_ref, ckv_ref, gq_ref, gk_ref, o_ref):
        for h in range(XATT_H):
            sl = slice(h * HEAD, (h + 1) * HEAD)
            qn = _rms(cq_ref[:, sl], gq_ref[...]).astype(BF16)
            kn = _rms(ckv_ref[:, sl], gk_ref[...]).astype(BF16)
            v = ckv_ref[:, w + h * HEAD:w + (h + 1) * HEAD].astype(BF16)
            s = _dot(qn, kn, NT) * scale
            p = jnp.exp(s - jnp.max(s, axis=1, keepdims=True))
            p = p / jnp.sum(p, axis=1, keepdims=True)
            o_ref[:, sl] = _dot(p.astype(BF16), v).astype(BF16)

    gain = pl.BlockSpec((1, HEAD), lambda i: (0, 0))
    return pl.pallas_call(
        body, name=name, grid=(S // tr,),
        in_specs=[pl.BlockSpec((tr, w), lambda i: (i, 0)), pl.BlockSpec((M, 2 * w), lambda i: (0, 0)), gain, gain],
        out_specs=pl.BlockSpec((tr, w), lambda i: (i, 0)), out_shape=jax.ShapeDtypeStruct((S, w), BF16),
        compiler_params=_params("parallel"),
    )(cq, ckv, g_cq, g_ck)


def _xattn_bwd(cq, ckv, g_cq, g_ck, do, name):
    S, w = cq.shape
    M = ckv.shape[0]
    tr = _row_tile(S)
    nsteps = S // tr
    scale = 1.0 / math.sqrt(HEAD)

    def body(cq_ref, ckv_ref, gq_ref, gk_ref, do_ref, dcq_ref, dckv_ref, dgq_ref, dgk_ref, dkn_scr, dv_scr):
        step = pl.program_id(0)

        @pl.when(step == 0)
        def _():
            dkn_scr[...] = jnp.zeros_like(dkn_scr)
            dv_scr[...] = jnp.zeros_like(dv_scr)
            dgq_ref[...] = jnp.zeros_like(dgq_ref)

        for h in range(XATT_H):
            sl = slice(h * HEAD, (h + 1) * HEAD)
            q_raw = cq_ref[:, sl]
            qn = _rms(q_raw, gq_ref[...]).astype(BF16)
            kn = _rms(ckv_ref[:, sl], gk_ref[...]).astype(BF16)
            v = ckv_ref[:, w + h * HEAD:w + (h + 1) * HEAD].astype(BF16)
            s = _dot(qn, kn, NT) * scale
            p = jnp.exp(s - jnp.max(s, axis=1, keepdims=True))
            p = p / jnp.sum(p, axis=1, keepdims=True)
            dob = do_ref[:, sl].astype(BF16)
            dp = _dot(dob, v, NT)
            ds = p * (dp - jnp.sum(p * dp, axis=1, keepdims=True)) * scale
            dsb = ds.astype(BF16)
            dv_scr[:, sl] += _dot(p.astype(BF16), dob, TN)
            dkn_scr[:, sl] += _dot(dsb, qn, TN)
            dq, dgq = _rms_grad(q_raw, gq_ref[...], _dot(dsb, kn))
            dcq_ref[:, sl] = dq.astype(BF16)
            dgq_ref[...] += dgq

        @pl.when(step == nsteps - 1)
        def _():
            dgk = jnp.zeros((1, HEAD), F32)
            for h in range(XATT_H):
                sl = slice(h * HEAD, (h + 1) * HEAD)
                dk, dgk_h = _rms_grad(ckv_ref[:, sl], gk_ref[...], dkn_scr[:, sl])
                dckv_ref[:, sl] = dk.astype(BF16)
                dgk = dgk + dgk_h
            dckv_ref[:, w:2 * w] = dv_scr[...].astype(BF16)
            dgk_ref[...] = dgk

    gain = pl.BlockSpec((1, HEAD), lambda i: (0, 0))
    row = pl.BlockSpec((tr, w), lambda i: (i, 0))
    mem = pl.BlockSpec((M, 2 * w), lambda i: (0, 0))
    return pl.pallas_call(
        body, name=name, grid=(nsteps,), in_specs=[row, mem, gain, gain, row], out_specs=[row, mem, gain, gain],
        out_shape=[jax.ShapeDtypeStruct((S, w), BF16), jax.ShapeDtypeStruct((M, 2 * w), BF16),
                   jax.ShapeDtypeStruct((1, HEAD), F32), jax.ShapeDtypeStruct((1, HEAD), F32)],
        scratch_shapes=[pltpu.VMEM((M, w), F32), pltpu.VMEM((M, w), F32)],
        compiler_params=_params("arbitrary"),
    )(cq, ckv, g_cq, g_ck, do)


def _ffn_up(h, w_gu, name, blocks, into=None, after=None, tm=512):
    S, D = h.shape
    nb, _, bw = w_gu.shape
    half = nb // 2
    j0, j1 = blocks
    tm = min(tm, S)
    kept = [] if into is None else list(into)
    more_specs, more = _behind(after)

    def body(h_ref, wg_ref, wu_ref, *rest):
        g_ref, u_ref, act_ref = rest[len(kept) + len(more):]
        hv = h_ref[...]
        g = _dot(hv, wg_ref[...])
        u = _dot(hv, wu_ref[...])
        g_ref[...] = g
        u_ref[...] = u
        act_ref[...] = (g * jax.nn.sigmoid(g) * u).astype(BF16)

    blk = pl.BlockSpec((tm, bw), lambda j, i: (i, j + j0))
    return pl.pallas_call(
        body, name=name, grid=(j1 - j0, S // tm),
        in_specs=[pl.BlockSpec((tm, D), lambda j, i: (i, 0)), pl.BlockSpec((None, D, bw), lambda j, i: (j + j0, 0, 0)),
                  pl.BlockSpec((None, D, bw), lambda j, i: (j + j0 + half, 0, 0))] + [ANY] * len(kept) + more_specs,
        out_specs=[blk, blk, blk],
        out_shape=[jax.ShapeDtypeStruct((S, half * bw), F32)] * 2 + [jax.ShapeDtypeStruct((S, half * bw), BF16)],
        input_output_aliases={3 + k: k for k in range(len(kept))},
        compiler_params=_params("parallel", "parallel"),
    )(h, w_gu, w_gu, *kept, *more)


def _ffn_down_dx(dy, w_down, g, u, name, tr=1024, tc=512):
    S, f = g.shape
    w2 = 2 * f
    tr, tc = min(tr, S), min(tc, f)
    assert S % tr == 0 and f % tc == 0
    n_ct = f // tc
    steps = (S // tr) * n_ct

    def body(dy_ref, w_ref, g_ref, u_ref, o_hbm, buf, sems):
        i, j = pl.program_id(0), pl.program_id(1)
        step = i * n_ct + j
        slot = step % 2

        def put(half):
            cols = pl.ds(pl.multiple_of(half * f + j * tc, 128), tc)
            return pltpu.make_async_copy(buf.at[slot, half], o_hbm.at[pl.ds(pl.multiple_of(i * tr, 8), tr), cols],
                                         sems.at[slot, half])

        @pl.when(step >= 2)
        def _():
            put(0).wait()
            put(1).wait()

        g = g_ref[...]
        sg = jax.nn.sigmoid(g)
        da = _dot(dy_ref[...], w_ref[...], NT)
        buf[slot, 0] = (da * u_ref[...] * (sg * (1.0 + g * (1.0 - sg)))).astype(BF16)
        buf[slot, 1] = (da * (g * sg)).astype(BF16)
        put(0).start()
        put(1).start()

        @pl.when(step == steps - 1)
        def _():
            put(0).wait()
            put(1).wait()
            if steps > 1:
                for half in (0, 1):
                    pltpu.make_async_copy(buf.at[1 - slot, half], o_hbm.at[pl.ds(0, tr), pl.ds(0, tc)],
                                          sems.at[1 - slot, half]).wait()

    D = dy.shape[1]
    return pl.pallas_call(
        body, name=name, grid=(S // tr, n_ct),
        in_specs=[pl.BlockSpec((tr, D), lambda i, j: (i, 0)), pl.BlockSpec((tc, D), lambda i, j: (j, 0)),
                  pl.BlockSpec((tr, tc), lambda i, j: (i, j)), pl.BlockSpec((tr, tc), lambda i, j: (i, j))],
        out_specs=ANY, out_shape=jax.ShapeDtypeStruct((S, w2), BF16),
        scratch_shapes=[pltpu.VMEM((2, 2, tr, tc), BF16), pltpu.SemaphoreType.DMA((2, 2))],
        compiler_params=_params("arbitrary", "arbitrary"),
    )(dy, w_down, g, u)


def _adamw_math(w, g, m, v):
    m = ADAM_B1 * m + (1.0 - ADAM_B1) * g
    v = ADAM_B2 * v + (1.0 - ADAM_B2) * (g * g)
    m_hat = m / (1.0 - ADAM_B1 ** ADAM_STEP)
    v_hat = v / (1.0 - ADAM_B2 ** ADAM_STEP)
    delta = -ADAM_LR * (m_hat / (jnp.sqrt(v_hat) + ADAM_EPS) + ADAM_WD * w)
    return delta, m, v


def _tile2(R, C, elems):
    if R % 16 == 0:
        return _row_tile(R, cap=max(16, elems // C // 16 * 16)), C
    tc = 128
    while C % (2 * tc) == 0 and R * 2 * tc <= elems:
        tc *= 2
    return R, tc


def _reduce_adamw(own, land, chip, w, m, v, name):
    _, R, C = own.shape
    tr, tc = _tile2(R, C, 1 << (19 if w.ndim == 3 else 18))

    def body(s_ref, o_ref, l1_ref, l2_ref, l3_ref, w_ref, m_ref, v_ref, g_ref, d_ref, nm_ref, nv_ref):
        g = o_ref[...].astype(F32) + l1_ref[...].astype(F32) + l2_ref[...].astype(F32) + l3_ref[...].astype(F32)
        g_ref[...] = g
        d_ref[...], nm_ref[...], nv_ref[...] = _adamw_math(w_ref[...], g, m_ref[...], v_ref[...])

    def part(flip):
        return pl.BlockSpec((None, tr, tc), lambda i, j, s: (s[0] ^ flip, i, j))

    if w.ndim == 3:
        blk = pl.BlockSpec((tr, None, tc), lambda i, j, s: (i, 0, j))
    else:
        blk = pl.BlockSpec((tr, tc), lambda i, j, s: (i, j))
    sds = jax.ShapeDtypeStruct(w.shape, F32)
    return pl.pallas_call(
        body, name=name,
        grid_spec=pltpu.PrefetchScalarGridSpec(
            num_scalar_prefetch=1, grid=(R // tr, C // tc),
            in_specs=[part(0), part(1), part(2), part(3), blk, blk, blk], out_specs=[blk] * 4),
        out_shape=[sds] * 4, compiler_params=_params("parallel", "parallel"),
    )(chip, own, land, land, land, w, m, v)


def _pair_sum(g8, recv, core, name):
    _, R, C = g8.shape
    tr, tc = _tile2(R, C, 1 << 21)

    def body(c_ref, a_ref, b_ref, o_ref):
        o_ref[...] = (a_ref[...].astype(F32) + b_ref[...].astype(F32)).astype(BF16)

    return pl.pallas_call(
        body, name=name,
        grid_spec=pltpu.PrefetchScalarGridSpec(
            num_scalar_prefetch=1, grid=(N_CHIP, R // tr, C // tc),
            in_specs=[pl.BlockSpec((None, tr, tc), lambda q, i, j, c: (2 * q + c[0], i, j)),
                      pl.BlockSpec((None, tr, tc), lambda q, i, j, c: (q, i, j))],
            out_specs=pl.BlockSpec((None, tr, tc), lambda q, i, j, c: (q, i, j))),
        out_shape=jax.ShapeDtypeStruct((N_CHIP, R, C), BF16),
        compiler_params=_params("parallel", "parallel", "parallel"),
    )(core, g8, recv)


def _place():
    return lax.axis_index("x"), lax.axis_index("y"), lax.axis_index("c")


def _land_with_own(shard, dev, name):
    R, C = shard.shape
    tr, tc = _tile2(R, C, 1 << 19)

    def body(d_ref, s_ref, o_ref):
        o_ref[...] = s_ref[...]

    return pl.pallas_call(
        body, name=name,
        grid_spec=pltpu.PrefetchScalarGridSpec(
            num_scalar_prefetch=1, grid=(R // tr, C // tc),
            in_specs=[pl.BlockSpec((tr, tc), lambda i, j, d: (i, j))],
            out_specs=pl.BlockSpec((None, tr, tc), lambda i, j, d: (d[0], i, j))),
        out_shape=jax.ShapeDtypeStruct((N_DEV, R, C), shard.dtype),
        compiler_params=_params("parallel", "parallel"),
    )(dev, shard)


def _all_gather(shards, lands, name):
    n = len(shards)

    def body(*refs):
        ins, outs = refs[:n], refs[2 * n:3 * n]
        send_sems, recv_sems = refs[3 * n:]
        x, y, c = _place()
        me, sibling = (x, y, c), (x, y, 1 - c)
        chips = [(1 - x, y), (x, 1 - y), (1 - x, 1 - y)]

        def copy(a, k, block, to, src=None):
            dst = outs[a].at[4 * block[0] + 2 * block[1] + block[2]]
            return pltpu.make_async_remote_copy(
                src_ref=dst if src is None else src, dst_ref=dst, send_sem=send_sems.at[a, k],
                recv_sem=recv_sems.at[a, k], device_id=to, device_id_type=MESH)

        sent = []
        for a in range(n):
            sent.append(copy(a, 0, me, sibling, src=ins[a]))
            sent += [copy(a, 1 + j, me, (*chip, c), src=ins[a]) for j, chip in enumerate(chips)]
        for cp in sent:
            cp.start()
        for j, chip in enumerate(chips):
            for a in range(n):
                copy(a, 1 + j, (*chip, c), me).wait_recv()
                fwd = copy(a, 4 + j, (*chip, c), sibling)
                fwd.start()
                sent.append(fwd)
        for a in range(n):
            copy(a, 0, sibling, me).wait_recv()
        for j, chip in enumerate(chips):
            for a in range(n):
                copy(a, 4 + j, (*chip, 1 - c), me).wait_recv()
        for cp in sent:
            cp.wait_send()

    return pl.pallas_call(
        body, name=name, in_specs=[ANY] * (2 * n), out_specs=[ANY] * n,
        out_shape=[jax.ShapeDtypeStruct(l.shape, l.dtype) for l in lands],
        input_output_aliases={n + a: a for a in range(n)},
        scratch_shapes=[pltpu.SemaphoreType.DMA((n, 7)), pltpu.SemaphoreType.DMA((n, 7))],
    )(*shards, *lands)


def _pair_plan(srcs, lands):
    x, y, c = _place()
    plan = []
    for a in range(len(srcs)):
        for q in range(N_CHIP):
            plan.append((srcs[a].at[2 * q + 1 - c], lands[a].at[q], lands[a].at[q], (x, y, 1 - c)))
    return plan


def _chip_plan(srcs, lands):
    x, y, c = _place()
    mine = 2 * x + y
    plan = []
    for a in range(len(srcs)):
        for px, py in [(1 - x, y), (x, 1 - y), (1 - x, 1 - y)]:
            peer = 2 * px + py
            plan.append((srcs[a].at[peer], lands[a].at[mine], lands[a].at[peer], (px, py, c)))
    return plan


def _gather_plan(srcs, lands):
    x, y, c = _place()
    mine = 4 * x + 2 * y + c
    plan = []
    for a in range(len(srcs)):
        for px, py, pc in [(x, y, 1 - c), (1 - x, y, c), (x, 1 - y, c), (1 - x, 1 - y, c)]:
            plan.append((srcs[a], lands[a].at[mine], lands[a].at[4 * px + 2 * py + pc], (px, py, pc)))
    return plan


def _forward_plan(srcs, lands):
    x, y, c = _place()
    plan = []
    for a in range(len(lands)):
        for px, py in [(1 - x, y), (x, 1 - y), (1 - x, 1 - y)]:
            mine, theirs = 4 * px + 2 * py + c, 4 * px + 2 * py + 1 - c
            plan.append((lands[a].at[mine], lands[a].at[mine], lands[a].at[theirs], (x, y, 1 - c)))
    return plan


def _remote(src, dst, send_sem, recv_sem, peer):
    return pltpu.make_async_remote_copy(src_ref=src, dst_ref=dst, send_sem=send_sem, recv_sem=recv_sem,
                                        device_id=peer, device_id_type=MESH)


HBM = pl.BlockSpec(memory_space=pltpu.HBM)
SEMS = pl.BlockSpec(memory_space=pltpu.SEMAPHORE)
DATAFLOW = pltpu.SideEffectType.DATAFLOW_SIDE_EFFECTING


def _split_start(srcs, lands, after, plan_fn, per_array, name):
    ns, nb = len(srcs), len(srcs) + len(lands)
    ncopy = per_array * len(lands)
    after = list(after) if isinstance(after, (list, tuple)) else [after]

    def body(*refs):
        send_sems, recv_sems = refs[nb + len(after)], refs[nb + len(after) + 1]
        token = refs[-1]
        for k, (src, dst, _, peer) in enumerate(plan_fn(refs[:ns], refs[ns:nb])):
            _remote(src, dst, send_sems.at[k], recv_sems.at[k], peer).start()
        token[...] = jnp.zeros_like(token)

    thru = [pltpu.HBM(a.shape, a.dtype) for a in (*srcs, *lands)]
    out = pl.pallas_call(
        body, name=name,
        out_shape=(pltpu.SemaphoreType.DMA((ncopy,)), pltpu.SemaphoreType.DMA((ncopy,)), *thru,
                   jax.ShapeDtypeStruct((8, 128), F32)),
        in_specs=[HBM] * nb + [ANY] * len(after),
        out_specs=(SEMS, SEMS, *[HBM] * nb, pl.BlockSpec(memory_space=pltpu.VMEM)),
        input_output_aliases={i: 2 + i for i in range(nb)},
        compiler_params=pltpu.CompilerParams(has_side_effects=DATAFLOW),
    )(*[pltpu.with_memory_space_constraint(a, pltpu.HBM) for a in (*srcs, *lands)], *after)
    return out[0], out[1], list(out[2:2 + ns]), list(out[2 + ns:2 + nb]), out[-1]


def _split_wait(send_sems, recv_sems, srcs, lands, after, plan_fn, name):
    ns, nb = len(srcs), len(srcs) + len(lands)
    after = list(after) if isinstance(after, (list, tuple)) else [after]

    def body(*refs):
        send_ref, recv_ref = refs[nb], refs[nb + 1]
        for k, (src, dst, mine, peer) in enumerate(plan_fn(refs[:ns], refs[ns:nb])):
            _remote(src, dst, send_ref.at[k], recv_ref.at[k], peer).wait_send()
            _remote(src, mine, send_ref.at[k], recv_ref.at[k], peer).wait_recv()

    thru = [pltpu.HBM(a.shape, a.dtype) for a in (*srcs, *lands)]
    out = pl.pallas_call(
        body, name=name, out_shape=tuple(thru),
        in_specs=[HBM] * nb + [SEMS, SEMS] + [ANY] * len(after), out_specs=tuple([HBM] * nb),
        input_output_aliases={i: i for i in range(nb)},
        compiler_params=pltpu.CompilerParams(has_side_effects=DATAFLOW),
    )(*srcs, *lands, send_sems, recv_sems, *after)
    return list(out[:ns]), list(out[ns:])


def _gather_forward(lands, name):
    n = len(lands)

    def body(*refs):
        landed, out = refs[:n], refs[n:2 * n]
        send_sems, recv_sems = refs[2 * n:]
        x, y, c = _place()
        chips = [(1 - x, y), (x, 1 - y), (1 - x, 1 - y)]
        sent = []
        for a in range(n):
            for j, (px, py) in enumerate(chips):
                blk = 4 * px + 2 * py + c
                sent.append(_remote(landed[a].at[blk], out[a].at[blk], send_sems.at[a, j], recv_sems.at[a, j], (x, y, 1 - c)))
        for cp in sent:
            cp.start()
        for a in range(n):
            for j, (px, py) in enumerate(chips):
                blk = 4 * px + 2 * py + 1 - c
                _remote(landed[a].at[blk], out[a].at[blk], send_sems.at[a, j], recv_sems.at[a, j], (x, y, 1 - c)).wait_recv()
        for cp in sent:
            cp.wait_send()

    return pl.pallas_call(
        body, name=name, in_specs=[ANY] * n, out_specs=[ANY] * n,
        out_shape=[jax.ShapeDtypeStruct(l.shape, l.dtype) for l in lands],
        input_output_aliases={a: a for a in range(n)},
        scratch_shapes=[pltpu.SemaphoreType.DMA((n, 3)), pltpu.SemaphoreType.DMA((n, 3))],
    )(*lands)


def _pack_rows(groups, name):
    n = len(groups[0])
    rows = -(-n // 8) * 8
    width = max(v.shape[1] for v in groups[0])

    def body(*refs):
        ins, outs = refs[:n * len(groups)], refs[n * len(groups):]
        for gi, o_ref in enumerate(outs):
            o_ref[...] = jnp.zeros_like(o_ref)
            for r in range(n):
                v_ref = ins[gi * n + r]
                o_ref[r:r + 1, 0:v_ref.shape[1]] = v_ref[...]

    return pl.pallas_call(
        body, name=name, out_shape=[jax.ShapeDtypeStruct((rows, width), F32)] * len(groups), compiler_params=_params(),
    )(*[v for g in groups for v in g])


def _unpack_rows(packs, like, name):
    n = len(like)

    def body(*refs):
        ins, outs = refs[:len(packs)], refs[len(packs):]
        for pi, p_ref in enumerate(ins):
            for r in range(n):
                o_ref = outs[pi * n + r]
                o_ref[...] = p_ref[r:r + 1, 0:o_ref.shape[1]]

    flat = pl.pallas_call(
        body, name=name, out_shape=[jax.ShapeDtypeStruct(a.shape, F32) for _ in packs for a in like],
        compiler_params=_params(),
    )(*packs)
    return [flat[pi * n:(pi + 1) * n] for pi in range(len(packs))]


def _sum_adamw(parts, w, m, v, name, col=None):
    P, R, _ = parts.shape
    C = w.shape[1]
    tr = _row_tile(R, cap=512, mult=8)

    def body(c_ref, p_ref, w_ref, m_ref, v_ref, g_ref, d_ref, nm_ref, nv_ref):
        g = p_ref[0].astype(F32)
        for q in range(1, P):
            g = g + p_ref[q].astype(F32)
        g_ref[...] = g
        d_ref[...], nm_ref[...], nv_ref[...] = _adamw_math(w_ref[...], g, m_ref[...], v_ref[...])

    blk = pl.BlockSpec((tr, C), lambda i, c: (i, 0))
    sds = jax.ShapeDtypeStruct((R, C), F32)
    at = jnp.zeros((1,), jnp.int32) if col is None else col
    return pl.pallas_call(
        body, name=name,
        grid_spec=pltpu.PrefetchScalarGridSpec(
            num_scalar_prefetch=1, grid=(R // tr,),
            in_specs=[pl.BlockSpec((P, tr, C), lambda i, c: (0, i, c[0])), blk, blk, blk], out_specs=[blk] * 4),
        out_shape=[sds] * 4, compiler_params=_params("parallel"),
    )(at, parts, w, m, v)


def _local_step(x, mem, target, sp, W, arrived, want, done, paired):
    h0 = _rms_fwd([x], [sp["g_mix"]], "rms_mix")
    proj = _mm(h0, W["main_t"], "nt", F32, "proj_main", tm=2048, tn=512)
    f_logit = _mm(h0, W["f_t"], "nt", F32, "proj_f", tm=2048)
    qn, kn, vb = _qk_fwd(proj, sp["g_q"], sp["g_k"], "qk_norm")
    b_pad = jnp.pad(sp["b_f"], ((0, 0), (0, HEAD - FOX_H)))
    c_col, c_row = _fgate_fwd(f_logit, b_pad, "forget_cumsum")
    o_fox, lse = _fox_fwd(qn, kn, vb, c_col, c_row, "fox_fwd")
    lru_p = (sp["conv_w"], sp["conv_b"], sp["w_ra"], sp["b_ra"], sp["w_ri"], sp["b_ri"], sp["lam"])
    y_lru, h_lru = _lru_fwd(proj, *lru_p, "lru_fwd", after=arrived("attn", o_fox))
    w_out, w_cq, w_ckv, w_co = want("attn", y_lru)
    mix = _rms_fwd([o_fox, y_lru], [sp["g_fox_out"], sp["g_lru_out"]], "rms_mix_out")
    x1 = _mm(mix, w_out, "nn", F32, "out_proj", add=x)
    hq = _rms_fwd([x1], [sp["g_xattn"]], "rms_xattn")
    mn = _rms_fwd([mem], [sp["g_mem"]], "rms_mem")
    cq = _mm(hq, w_cq, "nn", F32, "xattn_q", tm=2048)
    ckv = _mm(mn, w_ckv, "nn", F32, "xattn_kv")
    o_x = _xattn_fwd(cq, ckv, sp["g_cq"], sp["g_ck"], "xattn_fwd")
    co_w = w_co.shape[2]
    x2 = _mm(o_x, w_co, "nn", F32, "xattn_out", add=x1, tm=2048, tn=co_w, after=arrived("gate_up", o_x))
    hf = _rms_fwd([x2], [sp["g_ffn"]], "rms_ffn")
    (w_gu,) = want("gate_up", hf)
    gu_w = w_gu.shape[2]
    n_gate = w_gu.shape[0] // 2
    first = _ffn_up(hf, w_gu, "ffn_up_a", (0, n_gate // 2))
    f_gate, f_up, act = _ffn_up(hf, w_gu, "ffn_up_b", (n_gate // 2, n_gate), into=first, after=arrived("down", first[2]))
    (w_down,) = want("down", act)
    x3 = _mm(act, w_down, "nn", F32, "ffn_down", add=x2, tk=w_down.shape[0] // 2)
    dy, dyb, sq = _loss_head(x3, target, "loss_head")
    g_down = _mm(act, dyb, "tn", BF16, "ffn_down_dw", tm=w_down.shape[0] // 4, tn=2048)
    dgu = _ffn_down_dx(dyb, w_down, f_gate, f_up, "ffn_down_dx")
    g_gu = _mm(hf, dgu, "tn", BF16, "ffn_up_dw", out_blocks=N_DEV, tn=gu_w)
    tok = done("ffn", [g_gu, g_down])
    dhf = _mm(dgu, w_gu, "nt", F32, "ffn_up_dx", tn=2048, tk=gu_w, after=tok)
    tok = paired("ffn", dhf)
    dx2, dx2b, dg_ffn = _rms_bwd(x2, sp["g_ffn"], dhf, "rms_ffn_bwd", res=dy, after=tok)
    d_ox = _mm(dx2b, w_co, "nt", F32, "xattn_out_dx", tm=2048, tk=co_w)
    g_co = _mm(o_x, dx2b, "tn", BF16, "xattn_out_dw", out_blocks=N_DEV, tn=co_w)
    dcq, dckv, dg_cq, dg_ck = _xattn_bwd(cq, ckv, sp["g_cq"], sp["g_ck"], d_ox, "xattn_bwd")
    dhq = _mm(dcq, w_cq, "nt", F32, "xattn_q_dx", tm=2048)
    g_cq = _mm(hq, dcq, "tn", BF16, "xattn_q_dw")
    dmn = _mm(dckv, w_ckv, "nt", F32, "xattn_kv_dx")
    g_ckv = _mm(mn, dckv, "tn", BF16, "xattn_kv_dw")
    (dg_mem,) = _rms_bwd(mem, sp["g_mem"], dmn, "rms_mem_bwd", want_dx=False)
    dx1, dx1b, dg_xattn = _rms_bwd(x1, sp["g_xattn"], dhq, "rms_xattn_bwd", res=dx2)
    dmix = _mm(dx1b, w_out, "nt", F32, "out_proj_dx")
    g_out = _mm(mix, dx1b, "tn", BF16, "out_proj_dw", tn=2048)
    tok = done("attn", [g_out, g_cq, g_ckv, g_co])
    do_fox, dg_fox_out = _rms_bwd(o_fox, sp["g_fox_out"], dmix, "rms_fox_out_bwd", dy_col=0, want_bf16=False, after=tok)
    dy_lru, dg_lru_out = _rms_bwd(y_lru, sp["g_lru_out"], dmix, "rms_lru_out_bwd", dy_col=1, want_bf16=False)
    tok = paired("attn", dy_lru)
    du, dgate, dconv_w, dconv_b, dw_ra, db_ra, dw_ri, db_ri, dlam = _lru_bwd(proj, h_lru, dy_lru, *lru_p, "lru_bwd", after=tok)
    dqn, dkn, dv, dc_col, dc_row = _fox_bwd(qn, kn, vb, c_col, c_row, lse, o_fox, do_fox, "fox_bwd")
    dq, dk, dg_q, dg_k = _qk_bwd(proj, sp["g_q"], sp["g_k"], dqn, dkn, "qk_norm_bwd")
    df, db_f = _fgate_bwd(f_logit, b_pad, dc_col, dc_row, "forget_cumsum_bwd")
    dproj = jnp.concatenate([dq, dk, dv, du, dgate], axis=1)
    g_main_t = _mm(dproj, h0, "tn", BF16, "proj_main_dw", tn=2048)
    g_f_t = _mm(df, h0, "tn", BF16, "proj_f_dw", tn=2048)
    tok = done("w_in", [g_main_t, g_f_t])
    dh_f = _mm(df, W["f_t"], "nn", F32, "proj_f_dx", tm=2048)
    dh0 = _mm(dproj, W["main_t"], "nn", F32, "proj_main_dx", add=dh_f, tk=dproj.shape[1] // 2, after=tok)
    tok = paired("w_in", dh0)
    grad_x, dg_mix = _rms_bwd(x, sp["g_mix"], dh0, "rms_mix_bwd", res=dx1, want_bf16=False, after=tok)
    small = dict(g_mix=dg_mix, b_f=db_f[:, :FOX_H], g_q=dg_q, g_k=dg_k, conv_w=dconv_w, conv_b=dconv_b, w_ra=dw_ra,
                 b_ra=db_ra, w_ri=dw_ri, b_ri=db_ri, lam=dlam, g_fox_out=dg_fox_out, g_lru_out=dg_lru_out,
                 g_xattn=dg_xattn, g_mem=dg_mem, g_cq=dg_cq, g_ck=dg_ck, g_ffn=dg_ffn)
    return sq, grad_x, small


BIG = ("w_in", "w_out", "w_cq", "w_ckv", "w_co", "w_gate_up", "w_down")
SMALL = ("g_mix", "b_f", "g_q", "g_k", "conv_b", "w_ra", "b_ra", "w_ri", "b_ri", "lam", "g_fox_out", "g_lru_out",
         "g_xattn", "g_mem", "g_cq", "g_ck", "g_ffn")
ORDER = ("g_mix", "w_in", "b_f", "g_q", "g_k", "conv_w", "conv_b", "w_ra", "b_ra", "w_ri", "b_ri", "lam", "g_fox_out",
         "g_lru_out", "w_out", "g_xattn", "g_mem", "w_cq", "w_ckv", "g_cq", "g_ck", "w_co", "g_ffn", "w_gate_up", "w_down")


def kernel(x, mem, g_mix, w_in, b_f, g_q, g_k, conv_w, conv_b, w_ra, b_ra, w_ri, b_ri, lam, g_fox_out, g_lru_out, w_out, g_xattn, g_mem, w_cq, w_ckv, g_cq, g_ck, w_co, g_ffn, w_gate_up, w_down, loss_target, m_g_mix, m_w_in, m_b_f, m_g_q, m_g_k, m_conv_w, m_conv_b, m_w_ra, m_b_ra, m_w_ri, m_b_ri, m_lam, m_g_fox_out, m_g_lru_out, m_w_out, m_g_xattn, m_g_mem, m_w_cq, m_w_ckv, m_g_cq, m_g_ck, m_w_co, m_g_ffn, m_w_gate_up, m_w_down, v_g_mix, v_w_in, v_b_f, v_g_q, v_g_k, v_conv_w, v_conv_b, v_w_ra, v_b_ra, v_w_ri, v_b_ri, v_lam, v_g_fox_out, v_g_lru_out, v_w_out, v_g_xattn, v_g_mem, v_w_cq, v_w_ckv, v_g_cq, v_g_ck, v_w_co, v_g_ffn, v_w_gate_up, v_w_down):
    given = dict(locals())
    w = {n: given[n] for n in ORDER}
    m = {n: given["m_" + n] for n in ORDER}
    v = {n: given["v_" + n] for n in ORDER}
    D = x.shape[2]
    fw = FOX_H * HEAD
    dev_index = 4 * lax.axis_index("x") + 2 * lax.axis_index("y") + lax.axis_index("c")
    dev = jnp.reshape(dev_index, (1,)).astype(jnp.int32)
    core = jnp.reshape(lax.axis_index("c"), (1,)).astype(jnp.int32)
    chip = jnp.reshape(2 * lax.axis_index("x") + lax.axis_index("y"), (1,)).astype(jnp.int32)

    def shard(d, n):
        return jnp.transpose(d[n], (2, 0, 1)) if n == "w_in" else d[n][0]

    def unshard(a, n):
        return jnp.transpose(a, (1, 2, 0)) if n == "w_in" else a[None]

    gather_groups = dict(attn=("w_out", "w_cq", "w_ckv", "w_co"), gate_up=("w_gate_up",), down=("w_down",))
    reduce_groups = dict(ffn=("w_gate_up", "w_down"), attn=("w_out", "w_cq", "w_ckv", "w_co"), w_in=("w_in",))
    column_blocked = ("w_co", "w_gate_up")
    flying = {}

    def land(a, tag):
        return _land_with_own(a, dev, "own_" + tag)

    def launch(group, after):
        shards = [shard(w, n).astype(BF16) for n in gather_groups[group]]
        lands = [land(s, n) for s, n in zip(shards, gather_groups[group])]
        flying[group] = _split_start(shards, lands, after, _gather_plan, 4, "gather_" + group + "_start")
        return flying[group][4]

    def arrived(group, after):
        send, recv, shards, lands, _ = flying.pop(group)
        _, lands = _split_wait(send, recv, shards, lands, after, _gather_plan, "gather_" + group + "_wait")
        flying[group] = _split_start([], lands, after, _forward_plan, 3, "gather_" + group + "_forward_start")
        if group == "attn":
            return launch("down", flying[group][4])
        return flying[group][4]

    def want(group, after):
        send, recv, _, lands, _ = flying.pop(group)
        _, full = _split_wait(send, recv, [], lands, after, _forward_plan, "gather_" + group + "_forward_wait")
        return [g if n in column_blocked else g.reshape(-1, g.shape[2]) for n, g in zip(gather_groups[group], full)]

    first = [shard(w, "w_in").astype(BF16).reshape(-1, D), conv_w[0]]
    g_in, g_conv = _all_gather(first, [land(first[0], "w_in"), land(first[1], "conv_w")], "gather_w_in")
    tok = launch("gate_up", launch("attn", g_conv))
    per = g_in.shape[1]
    f_k, f_lo = divmod(3 * fw, per)
    assert f_lo + FOX_H <= per
    wt_in = g_in.reshape(-1, D)
    W = dict(main_t=jnp.concatenate([wt_in[:3 * fw], wt_in[3 * fw + FOX_H:]], axis=0),
             f_t=jnp.pad(wt_in[3 * fw:3 * fw + FOX_H], ((0, HEAD - FOX_H), (0, 0))))
    sp = {n: w[n] for n in SMALL if n not in ("w_ra", "w_ri")}
    sp["w_ra"], sp["w_ri"] = w_ra[0], w_ri[0]
    sp["conv_w"] = jnp.transpose(g_conv, (1, 0, 2)).reshape(CONV_K, -1)
    sp["g_mix"] = sp["g_mix"] + tok[0, 0]

    pairing, reducing = {}, {}

    def done(group, grads):
        if group == "w_in":
            g_main_t, g_f_t = grads
            shards = [g_main_t[k * per:(k + 1) * per] for k in range(f_k)]
            shards.append(jnp.concatenate([g_main_t[f_k * per:3 * fw], g_f_t[:FOX_H],
                                           g_main_t[3 * fw:(f_k + 1) * per - FOX_H]], axis=0))
            shards += [g_main_t[k * per - FOX_H:(k + 1) * per - FOX_H] for k in range(f_k + 1, N_DEV)]
            grads = [jnp.stack(shards)]
        g8 = [g if g.ndim == 3 else g.reshape(N_DEV, -1, g.shape[1]) for g in grads]
        lands = [lax.empty((N_CHIP,) + g.shape[1:], g.dtype) for g in g8]
        pairing[group] = _split_start(g8, lands, chip, _pair_plan, N_CHIP, "reduce_" + group + "_pair_start")
        return pairing[group][4]

    def paired(group, after):
        send, recv, g8, lands, _ = pairing.pop(group)
        g8, from_sibling = _split_wait(send, recv, g8, lands, after, _pair_plan, "reduce_" + group + "_pair_wait")
        p4 = [_pair_sum(g, r, core, "reduce_pair_sum_" + n) for g, r, n in zip(g8, from_sibling, reduce_groups[group])]
        lands = [lax.empty(p.shape, p.dtype) for p in p4]
        reducing[group] = _split_start(p4, lands, chip, _chip_plan, 3, "reduce_" + group + "_start")
        return reducing[group][4]

    def finish(group, after):
        send, recv, p4, lands, _ = reducing.pop(group)
        p4, lands = _split_wait(send, recv, p4, lands, after, _chip_plan, "reduce_" + group + "_wait")
        return {n: tuple(unshard(r, n) for r in _reduce_adamw(p, l, chip, shard(w, n), shard(m, n), shard(v, n), "adamw_" + n))
                for n, p, l in zip(reduce_groups[group], p4, lands)}

    sq, grad_x, gs = _local_step(x[0], mem[0], loss_target[0], sp, W, arrived, want, done, paired)

    vectors = tuple(n for n in SMALL if n not in ("w_ra", "w_ri"))
    mine = [_pack_rows([[gs[n] for n in vectors] + [sq[0:1]]], "pack_small_grads")[0], gs["w_ra"].reshape(-1, HEAD),
            gs["w_ri"].reshape(-1, HEAD), gs["conv_w"]]
    lands = [land(a, "small_grads_%d" % k) for k, a in enumerate(mine)]
    s_send, s_recv, s_src, s_land, tok = _split_start(mine, lands, grad_x, _gather_plan, 4, "gather_small_start")
    out = finish("ffn", tok)
    out.update(finish("attn", tok))
    updated = [r for n in reduce_groups["ffn"] + reduce_groups["attn"] for r in out[n]]
    out.update(finish("w_in", updated))
    updated = list(out["w_in"])
    _, s_land = _split_wait(s_send, s_recv, s_src, s_land, updated, _gather_plan, "gather_small_wait")
    all_vec, all_ra, all_ri, all_conv = _gather_forward(s_land, "gather_small_forward")

    nothing = jnp.zeros((1, HEAD), F32)
    state = _pack_rows([[d[n] for n in vectors] + [nothing] for d in (w, m, v)], "pack_small_state")
    upd = _sum_adamw(all_vec, *state, "adamw_small_vectors")
    per_vector = _unpack_rows(upd, [w[n] for n in vectors] + [nothing], "unpack_small")
    for r, n in enumerate(vectors):
        out[n] = tuple(per_vector[k][r] for k in range(4))
    loss = 0.5 * per_vector[0][len(vectors)][0, 0] / D
    for n, parts in (("w_ra", all_ra), ("w_ri", all_ri)):
        res = _sum_adamw(parts, *[d[n].reshape(-1, HEAD) for d in (w, m, v)], "adamw_" + n)
        out[n] = tuple(r.reshape(w[n].shape) for r in res)
    out["conv_w"] = tuple(r[None] for r in _sum_adamw(all_conv, conv_w[0], m_conv_w[0], v_conv_w[0], "adamw_conv_w", col=dev))

    return (loss, grad_x[None], *[out[n][0] for n in ORDER], *[out[n][1] for n in ORDER],
            *[out[n][2] for n in ORDER], *[out[n][3] for n in ORDER])
```
